```python
import jax, jax.numpy as jnp
from jax import lax
import numpy as np

D_MODEL = 1024
BATCH = 4
SEQ = 4096
DEPTH = 4

CHUNK = 64
Q_BLOCK = 128
HEAD_DIM = 64
N_DIFF_HEADS = D_MODEL // 4 // HEAD_DIM
N_FOX_HEADS = D_MODEL // 2 // HEAD_DIM
DIFF_WIDTH = N_DIFF_HEADS * 2 * HEAD_DIM
FOX_WIDTH = N_FOX_HEADS * HEAD_DIM
MIX_WIDTH = DIFF_WIDTH + FOX_WIDTH
IN_COLS = 3 * DIFF_WIDTH + 3 * FOX_WIDTH + N_FOX_HEADS
ROT_DIM = HEAD_DIM // 4
ROPE_THETA = 500000.0
N_GROUPS = 4
EXPERTS_PER_GROUP = 8
N_EXPERTS = N_GROUPS * EXPERTS_PER_GROUP
TOP_K = 2
D_EXPERT = D_MODEL // 2
EXPERT_BLOCK = 128
EPS = 1e-6

kernel_name = "hybrid_diff_fox_hmoe_adaln"


def rms_norm(x, g):
    xf = x.astype(jnp.float32)
    y = xf * lax.rsqrt(jnp.mean(xf * xf, axis=-1, keepdims=True) + EPS)
    return (y * g.astype(jnp.float32)).astype(x.dtype)


def rope_tables(positions):
    inv = ROPE_THETA ** (-jnp.arange(0, ROT_DIM, 2, dtype=jnp.float32) / ROT_DIM)
    ang = positions.astype(jnp.float32)[..., None] * inv
    return jnp.cos(ang), jnp.sin(ang)


def partial_rope(x, cos, sin):
    half = ROT_DIM // 2
    cos = cos.astype(x.dtype)
    sin = sin.astype(x.dtype)
    x1 = x[..., :half]
    x2 = x[..., half:ROT_DIM]
    return jnp.concatenate([x1 * cos - x2 * sin, x2 * cos + x1 * sin, x[..., ROT_DIM:]], axis=-1)


def diff_attention(q, k, v, lam):
    B, S, H = q.shape[:3]
    nb = S // Q_BLOCK
    scale = HEAD_DIM ** -0.5
    kf = k.astype(jnp.float32)
    vf = v.astype(jnp.float32)
    qb = q.reshape(B, nb, Q_BLOCK, H, 2, HEAD_DIM).transpose(1, 0, 2, 3, 4, 5)
    key_chunk = jnp.arange(S) // CHUNK

    def block(args):
        qi, bi = args
        s = jnp.einsum('bqhcd,bkhcd->bhcqk', qi.astype(jnp.float32), kf) * scale
        q_chunk = (bi * Q_BLOCK + jnp.arange(Q_BLOCK)) // CHUNK
        mask = key_chunk[None, :] <= q_chunk[:, None]
        p = jax.nn.softmax(jnp.where(mask, s, -jnp.inf), axis=-1)
        a = p[:, :, 0] - lam * p[:, :, 1]
        return jnp.einsum('bhqk,bkhe->bqhe', a, vf)

    o = lax.map(block, (qb, jnp.arange(nb)))
    return o.transpose(1, 0, 2, 3, 4).reshape(B, S, H, 2 * HEAD_DIM).astype(v.dtype)


def forgetting_attention(q, k, v, log_f):
    B, S, H = q.shape[:3]
    nb = S // Q_BLOCK
    scale = HEAD_DIM ** -0.5
    cf = jnp.cumsum(log_f, axis=1)
    cf_k = cf.transpose(0, 2, 1)
    kf = k.astype(jnp.float32)
    vf = v.astype(jnp.float32)
    qb = q.reshape(B, nb, Q_BLOCK, H, HEAD_DIM).transpose(1, 0, 2, 3, 4)
    cqb = cf.reshape(B, nb, Q_BLOCK, H).transpose(1, 0, 3, 2)
    key_idx = jnp.arange(S)

    def block(args):
        qi, cq, bi = args
        s = jnp.einsum('bqhd,bkhd->bhqk', qi.astype(jnp.float32), kf) * scale
        s = s + (cq[..., :, None] - cf_k[:, :, None, :])
        t = bi * Q_BLOCK + jnp.arange(Q_BLOCK)
        mask = key_idx[None, :] <= t[:, None]
        p = jax.nn.softmax(jnp.where(mask, s, -jnp.inf), axis=-1)
        return jnp.einsum('bhqk,bkhd->bqhd', p, vf)

    o = lax.map(block, (qb, cqb, jnp.arange(nb)))
    return o.transpose(1, 0, 2, 3, 4).reshape(B, S, H, HEAD_DIM).astype(v.dtype)


def token_mixer(h, cos, sin, layer, w_in_l, b_forget_l, lq1, lk1, lq2, lk2, g_subln_l, g_fox_l, w_out_l):
    B, S, _ = h.shape
    proj = h @ w_in_l
    cuts = [DIFF_WIDTH, 2 * DIFF_WIDTH, 3 * DIFF_WIDTH,
            3 * DIFF_WIDTH + FOX_WIDTH, 3 * DIFF_WIDTH + 2 * FOX_WIDTH, 3 * DIFF_WIDTH + 3 * FOX_WIDTH]
    dq, dk, dv, fq, fk, fv, ff = jnp.split(proj, cuts, axis=-1)

    rc, rs = cos[:, :, None, None, :], sin[:, :, None, None, :]
    dq = partial_rope(dq.reshape(B, S, N_DIFF_HEADS, 2, HEAD_DIM), rc, rs)
    dk = partial_rope(dk.reshape(B, S, N_DIFF_HEADS, 2, HEAD_DIM), rc, rs)
    dv = dv.reshape(B, S, N_DIFF_HEADS, 2 * HEAD_DIM)
    lambda_init = 0.8 - 0.6 * float(np.exp(-0.3 * layer))
    lam = (jnp.exp(jnp.sum(lq1.astype(jnp.float32) * lk1.astype(jnp.float32)))
           - jnp.exp(jnp.sum(lq2.astype(jnp.float32) * lk2.astype(jnp.float32))) + lambda_init)
    o_diff = diff_attention(dq, dk, dv, lam)
    o_diff = (rms_norm(o_diff, g_subln_l) * (1.0 - lambda_init)).reshape(B, S, DIFF_WIDTH)

    fq = fq.reshape(B, S, N_FOX_HEADS, HEAD_DIM)
    fk = fk.reshape(B, S, N_FOX_HEADS, HEAD_DIM)
    fv = fv.reshape(B, S, N_FOX_HEADS, HEAD_DIM)
    log_f = jax.nn.log_sigmoid(ff.astype(jnp.float32) + b_forget_l.astype(jnp.float32))
    o_fox = forgetting_attention(fq, fk, fv, log_f)
    o_fox = rms_norm(o_fox, g_fox_l).reshape(B, S, FOX_WIDTH)

    return jnp.concatenate([o_diff, o_fox], axis=-1) @ w_out_l


def hierarchical_moe(h, wg, bg, we, be, w_gate, w_up, w_down):
    B, S, D = h.shape
    N = B * S
    hf = h.reshape(N, D)
    hr = hf.astype(jnp.float32)
    p_group = jax.nn.softmax(hr @ wg.astype(jnp.float32) + bg.astype(jnp.float32), axis=-1)
    g_sel = jnp.argmax(p_group, axis=-1)
    p_g = jnp.take_along_axis(p_group, g_sel[:, None], axis=-1)
    logits_e = (hr @ we.astype(jnp.float32) + be.astype(jnp.float32)).reshape(N, N_GROUPS, EXPERTS_PER_GROUP)
    logits_sel = jnp.take_along_axis(logits_e, g_sel[:, None, None], axis=1)[:, 0]
    top_l, top_i = lax.top_k(logits_sel, TOP_K)
    weights = p_g * jax.nn.softmax(top_l, axis=-1)
    expert_id = g_sel[:, None] * EXPERTS_PER_GROUP + top_i

    A = N * TOP_K
    e_flat = expert_id.reshape(A).astype(jnp.int32)
    tok_flat = jnp.repeat(jnp.arange(N, dtype=jnp.int32), TOP_K)
    w_flat = weights.reshape(A)
    order = jnp.argsort(e_flat)
    e_sorted = e_flat[order]
    counts = jax.ops.segment_sum(jnp.ones((A,), jnp.int32), e_flat, num_segments=N_EXPERTS)
    padded = ((counts + EXPERT_BLOCK - 1) // EXPERT_BLOCK) * EXPERT_BLOCK
    start = jnp.cumsum(counts) - counts
    pend = jnp.cumsum(padded)
    pstart = pend - padded
    dest = pstart[e_sorted] + (jnp.arange(A, dtype=jnp.int32) - start[e_sorted])
    P = A + N_EXPERTS * EXPERT_BLOCK
    n_blocks = P // EXPERT_BLOCK
    slot_tok = jnp.zeros((P,), jnp.int32).at[dest].set(tok_flat[order])
    slot_w = jnp.zeros((P,), jnp.float32).at[dest].set(w_flat[order])
    block_expert = jnp.minimum(
        jnp.searchsorted(pend, jnp.arange(n_blocks, dtype=jnp.int32) * EXPERT_BLOCK, side='right'),
        N_EXPERTS - 1)
    xs = hf[slot_tok].reshape(n_blocks, EXPERT_BLOCK, D)

    def run_block(args):
        xb, e = args
        return (jax.nn.silu(xb @ w_gate[e]) * (xb @ w_up[e])) @ w_down[e]

    ys = lax.map(run_block, (xs, block_expert)).reshape(P, D)
    out = jnp.zeros((N, D), ys.dtype).at[slot_tok].add(ys * slot_w[:, None].astype(ys.dtype))
    return out.reshape(B, S, D)


def setup_inputs(seed: int = 0) -> dict:
    key = jax.random.key(seed)
    ks = jax.random.split(key, 26)
    D = D_MODEL

    def nrm(k, shape, fan_in, mult=1.0):
        return jax.random.normal(k, shape, jnp.float32) * (mult * fan_in ** -0.5)

    def gain(k, shape):
        return 1.0 + 0.02 * jax.random.normal(k, shape, jnp.float32)

    offsets = jax.random.randint(ks[2], (BATCH, 1), 0, 4096)
    positions = (offsets + jnp.arange(SEQ)[None, :]).astype(jnp.int32)
    return {
        "x": jax.random.normal(ks[0], (BATCH, SEQ, D), jnp.float32),
        "c": jax.random.normal(ks[1], (BATCH, D), jnp.float32),
        "positions": positions,
        "w_ada": nrm(ks[3], (DEPTH, D, 6 * D), D, 0.5),
        "b_ada": 0.02 * jax.random.normal(ks[4], (DEPTH, 6 * D), jnp.float32),
        "g_mix": gain(ks[5], (DEPTH, D)),
        "w_in": nrm(ks[6], (DEPTH, D, IN_COLS), D),
        "b_forget": 1.0 + 4.0 * jax.random.uniform(ks[7], (DEPTH, N_FOX_HEADS), jnp.float32),
        "lambda_q1": 0.1 * jax.random.normal(ks[8], (DEPTH, HEAD_DIM), jnp.float32),
        "lambda_k1": 0.1 * jax.random.normal(ks[9], (DEPTH, HEAD_DIM), jnp.float32),
        "lambda_q2": 0.1 * jax.random.normal(ks[10], (DEPTH, HEAD_DIM), jnp.float32),
        "lambda_k2": 0.1 * jax.random.normal(ks[11], (DEPTH, HEAD_DIM), jnp.float32),
        "g_subln": gain(ks[12], (DEPTH, 2 * HEAD_DIM)),
        "g_fox_out": gain(ks[13], (DEPTH, HEAD_DIM)),
        "w_out": nrm(ks[14], (DEPTH, MIX_WIDTH, D), MIX_WIDTH),
        "g_ffn": gain(ks[15], (DEPTH, D)),
        "w_router_group": nrm(ks[16], (DEPTH, D, N_GROUPS), D),
        "b_router_group": 0.01 * jax.random.normal(ks[17], (DEPTH, N_GROUPS), jnp.float32),
        "w_router_expert": nrm(ks[18], (DEPTH, D, N_EXPERTS), D),
        "b_router_expert": 0.01 * jax.random.normal(ks[19], (DEPTH, N_EXPERTS), jnp.float32),
        "w_expert_gate": nrm(ks[20], (DEPTH, N_EXPERTS, D, D_EXPERT), D),
        "w_expert_up": nrm(ks[21], (DEPTH, N_EXPERTS, D, D_EXPERT), D),
        "w_expert_down": nrm(ks[22], (DEPTH, N_EXPERTS, D_EXPERT, D), D_EXPERT),
        "g_final": gain(ks[23], (D,)),
    }


def reference(x, c, positions, w_ada, b_ada, g_mix, w_in, b_forget, lambda_q1, lambda_k1,
              lambda_q2, lambda_k2, g_subln, g_fox_out, w_out, g_ffn, w_router_group,
              b_router_group, w_router_expert, b_router_expert, w_expert_gate, w_expert_up,
              w_expert_down, g_final):
    cond = jax.nn.silu(c)
    cos, sin = rope_tables(positions)
    for l in range(DEPTH):
        mod = (cond @ w_ada[l] + b_ada[l])[:, None, :]
        sh1, sc1, gt1, sh2, sc2, gt2 = jnp.split(mod, 6, axis=-1)
        h = rms_norm(x, g_mix[l]) * (1.0 + sc1) + sh1
        mix = token_mixer(h, cos, sin, l, w_in[l], b_forget[l], lambda_q1[l], lambda_k1[l],
                          lambda_q2[l], lambda_k2[l], g_subln[l], g_fox_out[l], w_out[l])
        x = x + gt1 * mix
        h = rms_norm(x, g_ffn[l]) * (1.0 + sc2) + sh2
        moe = hierarchical_moe(h, w_router_group[l], b_router_group[l], w_router_expert[l],
                               b_router_expert[l], w_expert_gate[l], w_expert_up[l], w_expert_down[l])
        x = x + gt2 * moe
    return rms_norm(x, g_final)
```

```python
import functools

import numpy as np
import jax
import jax.numpy as jnp
from jax import lax
from jax.experimental import pallas as pl
from jax.experimental.pallas import tpu as pltpu

D_MODEL = 1024
BATCH = 4
SEQ = 4096
DEPTH = 4
N_TOK = BATCH * SEQ

CHUNK = 64
HEAD_DIM = 64
N_DIFF_HEADS = 4
N_FOX_HEADS = 8
DIFF_WIDTH = 512
FOX_WIDTH = 512
IN_COLS = 3 * DIFF_WIDTH + 3 * FOX_WIDTH + N_FOX_HEADS
ROT_DIM = 16
ROPE_THETA = 500000.0
N_GROUPS = 4
EXPERTS_PER_GROUP = 8
N_EXPERTS = 32
TOP_K = 2
D_EXPERT = 512
EPS = 1e-6

LANES = 128
IN_COLS_PAD = 3200
FF_COL = 3 * DIFF_WIDTH + 3 * FOX_WIDTH
QK_WIDTH = 8 * LANES
TM = 512
TQ = 512
MOE_BLOCK = 256
MOE_ROWS = N_TOK * TOP_K + N_EXPERTS * MOE_BLOCK
MOE_NBLOCKS = MOE_ROWS // MOE_BLOCK
NEG = -1e30
VMEM_LIMIT = 56 * 1024 * 1024

F32 = jnp.float32
BF16 = jnp.bfloat16


def _bf16_round(x):
    return x.astype(BF16).astype(F32)


def _lane_iota(shape):
    return lax.broadcasted_iota(jnp.int32, shape, 1)


def _params(*sem):
    return pltpu.CompilerParams(dimension_semantics=sem, vmem_limit_bytes=VMEM_LIMIT)


def _mod_kernel(c_ref, w_ref, b_ref, o_ref):
    c = c_ref[...]
    cond = c / (1.0 + jnp.exp(-c))
    ch = cond.astype(BF16)
    cl = (cond - ch.astype(F32)).astype(BF16)
    w = w_ref[...]
    wh = w.astype(BF16)
    wl = (w - wh.astype(F32)).astype(BF16)
    acc = jnp.dot(ch, wh, preferred_element_type=F32)
    acc += jnp.dot(cl, wh, preferred_element_type=F32)
    acc += jnp.dot(ch, wl, preferred_element_type=F32)
    o_ref[...] = acc + b_ref[...]


def _modulation(c, w_ada, b_ada):
    rows = 16
    tn = 1536
    c_pad = jnp.zeros((rows, D_MODEL), F32).at[:BATCH].set(c)
    out = pl.pallas_call(
        _mod_kernel,
        grid=(DEPTH, 6 * D_MODEL // tn),
        in_specs=[
            pl.BlockSpec((rows, D_MODEL), lambda l, n: (0, 0)),
            pl.BlockSpec((None, D_MODEL, tn), lambda l, n: (l, 0, n)),
            pl.BlockSpec((None, 1, tn), lambda l, n: (l, 0, n)),
        ],
        out_specs=pl.BlockSpec((None, rows, tn), lambda l, n: (l, 0, n)),
        out_shape=jax.ShapeDtypeStruct((DEPTH, rows, 6 * D_MODEL), F32),
        compiler_params=_params("arbitrary", "arbitrary"),
        name="adaln_mod",
    )(c_pad, w_ada, b_ada.reshape(DEPTH, 1, 6 * D_MODEL))
    return out[:, :BATCH]


def _rope_kernel(pos_ref, inv_ref, c_ref, sa_ref, sb_ref):
    ang = pos_ref[...].astype(F32) * inv_ref[...]
    j = _lane_iota(ang.shape) % HEAD_DIM
    cosv = jnp.cos(ang)
    sinv = jnp.sin(ang)
    half = ROT_DIM // 2
    c_ref[...] = jnp.where(j < ROT_DIM, cosv, 1.0)
    sa_ref[...] = jnp.where(j < half, -sinv, 0.0)
    sb_ref[...] = jnp.where((j >= half) & (j < ROT_DIM), sinv, 0.0)


def _rope_tables(positions):
    half = ROT_DIM // 2
    inv = ROPE_THETA ** (-jnp.arange(0, ROT_DIM, 2, dtype=F32) / ROT_DIM)
    lane = np.arange(LANES)
    inv_lane = inv[(lane % HEAD_DIM) % half].reshape(1, LANES)
    spec = pl.BlockSpec((TM, LANES), lambda i: (i, 0))
    shape = jax.ShapeDtypeStruct((N_TOK, LANES), F32)
    return pl.pallas_call(
        _rope_kernel,
        grid=(N_TOK // TM,),
        in_specs=[pl.BlockSpec((TM, 1), lambda i: (i, 0)),
                  pl.BlockSpec((1, LANES), lambda i: (0, 0))],
        out_specs=[spec, spec, spec],
        out_shape=[shape, shape, shape],
        compiler_params=_params("arbitrary"),
        name="rope_tables",
    )(positions.reshape(N_TOK, 1), inv_lane)


def _rms_mod(x, g, sc, sh):
    ms = jnp.mean(x * x, axis=-1, keepdims=True)
    return (x * lax.rsqrt(ms + EPS) * g) * (1.0 + sc) + sh


def _inproj_kernel(fuse, *refs):
    if fuse:
        (x_ref, d_ref, gt_ref, sc_ref, sh_ref, g_ref, w_ref, bf_ref, c_ref, sa_ref, sb_ref,
         pq_ref, xo_ref, dq_ref, dk_ref, dv_ref, fq_ref, fk_ref, fv_ref, carry_ref) = refs
        x = x_ref[...] + gt_ref[...] * d_ref[...]
        xo_ref[...] = x
    else:
        (x_ref, sc_ref, sh_ref, g_ref, w_ref, bf_ref, c_ref, sa_ref, sb_ref,
         pq_ref, dq_ref, dk_ref, dv_ref, fq_ref, fk_ref, fv_ref, carry_ref) = refs
        x = x_ref[...]
    hb = _rms_mod(x, g_ref[...], sc_ref[...], sh_ref[...]).astype(BF16)

    @pl.when(pl.program_id(0) % (SEQ // TM) == 0)
    def _():
        carry_ref[...] = jnp.zeros_like(carry_ref)

    lane = _lane_iota((TM, LANES))
    nh = N_FOX_HEADS

    def pack3(a):
        hi = _bf16_round(a)
        r1 = a - hi
        mid = _bf16_round(r1)
        lo = _bf16_round(r1 - mid)
        return jnp.where(lane < nh, hi,
                         jnp.where(lane < 2 * nh, pltpu.roll(mid, nh, 1),
                                   jnp.where(lane < 3 * nh, pltpu.roll(lo, 2 * nh, 1), 0.0)))

    z = jnp.dot(hb, w_ref[:, FF_COL:FF_COL + LANES], preferred_element_type=F32) + bf_ref[...]
    logf = jnp.minimum(z, 0.0) - jnp.log(1.0 + jnp.exp(-jnp.abs(z)))
    logf = jnp.where(lane < nh, logf, 0.0)
    row = lax.broadcasted_iota(jnp.int32, (TM, TM), 0)
    col = lax.broadcasted_iota(jnp.int32, (TM, TM), 1)
    tri = (row >= col).astype(BF16)
    r = jnp.dot(tri, pack3(logf).astype(BF16), preferred_element_type=F32)
    cs = r + pltpu.roll(r, LANES - nh, 1) + pltpu.roll(r, LANES - 2 * nh, 1)
    cf = jnp.where(lane < nh, cs + carry_ref[0:1, :], 0.0)
    carry_ref[...] = jnp.broadcast_to(cf[TM - 1:TM, :], carry_ref.shape)

    t3 = jnp.where(lane == 3 * nh, 1.0, pack3(cf)).astype(BF16)
    aug = jnp.dot(t3, pq_ref[...], preferred_element_type=F32)

    low = lane < HEAD_DIM
    rc, rsa, rsb = c_ref[...], sa_ref[...], sb_ref[...]
    scale = HEAD_DIM ** -0.5

    def split_store(chunk, o_ref, m, extra_a=None, extra_b=None):
        a = jnp.where(low, chunk, 0.0)
        b = jnp.where(low, pltpu.roll(chunk, HEAD_DIM, 1), 0.0)
        if extra_a is not None:
            a = a + extra_a
            b = b + extra_b
        o_ref[:, (2 * m) * LANES:(2 * m + 1) * LANES] = a.astype(BF16)
        o_ref[:, (2 * m + 1) * LANES:(2 * m + 2) * LANES] = b.astype(BF16)

    def rope(xc):
        return xc * rc + pltpu.roll(xc, LANES - ROT_DIM // 2, 1) * rsa + pltpu.roll(xc, ROT_DIM // 2, 1) * rsb

    pdq = jnp.dot(hb, w_ref[:, 0:DIFF_WIDTH], preferred_element_type=F32)
    for m in range(N_DIFF_HEADS):
        split_store(rope(pdq[:, m * LANES:(m + 1) * LANES]) * scale, dq_ref, m)
    pdk = jnp.dot(hb, w_ref[:, DIFF_WIDTH:2 * DIFF_WIDTH], preferred_element_type=F32)
    for m in range(N_DIFF_HEADS):
        split_store(rope(pdk[:, m * LANES:(m + 1) * LANES]), dk_ref, m)
    dv_ref[...] = jnp.dot(hb, w_ref[:, 2 * DIFF_WIDTH:3 * DIFF_WIDTH],
                          preferred_element_type=F32).astype(BF16)
    o = 3 * DIFF_WIDTH
    pfq = jnp.dot(hb, w_ref[:, o:o + FOX_WIDTH], preferred_element_type=F32)
    for m in range(N_FOX_HEADS // 2):
        split_store(pfq[:, m * LANES:(m + 1) * LANES] * scale, fq_ref, m,
                    aug[:, (2 * m) * LANES:(2 * m + 1) * LANES],
                    aug[:, (2 * m + 1) * LANES:(2 * m + 2) * LANES])
    pfk = jnp.dot(hb, w_ref[:, o + FOX_WIDTH:o + 2 * FOX_WIDTH], preferred_element_type=F32)
    for m in range(N_FOX_HEADS // 2):
        split_store(pfk[:, m * LANES:(m + 1) * LANES], fk_ref, m,
                    aug[:, QK_WIDTH + (2 * m) * LANES:QK_WIDTH + (2 * m + 1) * LANES],
                    aug[:, QK_WIDTH + (2 * m + 1) * LANES:QK_WIDTH + (2 * m + 2) * LANES])
    fv_ref[...] = jnp.dot(hb, w_ref[:, o + 2 * FOX_WIDTH:o + 3 * FOX_WIDTH],
                          preferred_element_type=F32).astype(BF16)


def _forget_placement():
    nh = N_FOX_HEADS
    p = np.zeros((LANES, 2 * QK_WIDTH), np.float32)
    for h in range(nh):
        base_q = h * LANES + HEAD_DIM
        base_k = QK_WIDTH + h * LANES + HEAD_DIM
        for part in range(3):
            p[part * nh + h, base_q + part] = 1.0
            p[3 * nh, base_q + 3 + part] = 1.0
            p[3 * nh, base_k + part] = 1.0
            p[part * nh + h, base_k + 3 + part] = -1.0
    return jnp.asarray(p, BF16)


def _inproj(x, delta, gate, sc, sh, g, w_bf, b_forget, tables, pq):
    fuse = delta is not None
    tpb = SEQ // TM
    row = pl.BlockSpec((TM, D_MODEL), lambda i: (i, 0))
    per_batch = pl.BlockSpec((None, 1, D_MODEL), lambda i: (i // tpb, 0, 0))
    const = lambda shape: pl.BlockSpec(shape, lambda i: (0,) * len(shape))
    tab = pl.BlockSpec((TM, LANES), lambda i: (i, 0))
    in_specs = [row]
    args = [x]
    if fuse:
        in_specs += [row, per_batch]
        args += [delta, gate]
    in_specs += [per_batch, per_batch, const((1, D_MODEL)), const((D_MODEL, IN_COLS_PAD)),
                 const((1, LANES)), tab, tab, tab, const((LANES, 2 * QK_WIDTH))]
    args += [sc, sh, g.reshape(1, D_MODEL), w_bf, b_forget, *tables, pq]
    wide = pl.BlockSpec((TM, QK_WIDTH), lambda i: (i, 0))
    half = pl.BlockSpec((TM, DIFF_WIDTH), lambda i: (i, 0))
    wide_s = jax.ShapeDtypeStruct((N_TOK, QK_WIDTH), BF16)
    half_s = jax.ShapeDtypeStruct((N_TOK, DIFF_WIDTH), BF16)
    out_specs = [wide, wide, half, wide, wide, half]
    out_shape = [wide_s, wide_s, half_s, wide_s, wide_s, half_s]
    if fuse:
        out_specs = [row] + out_specs
        out_shape = [jax.ShapeDtypeStruct((N_TOK, D_MODEL), F32)] + out_shape
    outs = pl.pallas_call(
        functools.partial(_inproj_kernel, fuse),
        grid=(N_TOK // TM,),
        in_specs=in_specs,
        out_specs=out_specs,
        out_shape=out_shape,
        scratch_shapes=[pltpu.VMEM((8, LANES), F32)],
        compiler_params=_params("arbitrary"),
        name="norm_inproj",
    )(*args)
    if fuse:
        return outs[0], outs[1:]
    return x, outs


def _attn_kernel(diff, lambda_init, qa_ref, qb_ref, ka_ref, kb_ref, v_ref, g_ref, lam_ref, o_ref):
    qi = pl.program_id(2)
    qa = qa_ref[...]
    qb = qb_ref[...]
    nt = (((1,), (1,)), ((), ()))

    def step(q, kblk, vblk, carry, mask):
        m, l, acc = carry
        s = lax.dot_general(q, kblk, nt, preferred_element_type=F32)
        if mask is not None:
            s = jnp.where(mask, s, NEG)
        m_new = jnp.maximum(m, jnp.max(s, axis=1, keepdims=True))
        alpha = jnp.exp(m - m_new)
        p = jnp.exp(s - m_new)
        l = alpha * l + jnp.sum(p, axis=1, keepdims=True)
        acc = alpha * acc + jnp.dot(p.astype(BF16), vblk, preferred_element_type=F32)
        return m_new, l, acc

    def both(off, carry, mask):
        ca, cb = carry
        vblk = v_ref[pl.ds(off, TQ), :]
        ca = step(qa, ka_ref[pl.ds(off, TQ), :], vblk, ca, mask)
        cb = step(qb, kb_ref[pl.ds(off, TQ), :], vblk, cb, mask)
        return ca, cb

    def init():
        return (jnp.full((TQ, 1), NEG, F32), jnp.zeros((TQ, 1), F32), jnp.zeros((TQ, LANES), F32))

    carry = lax.fori_loop(0, qi, lambda j, c: both(pl.multiple_of(j * TQ, TQ), c, None),
                          (init(), init()))
    r = lax.broadcasted_iota(jnp.int32, (TQ, TQ), 0)
    c = lax.broadcasted_iota(jnp.int32, (TQ, TQ), 1)
    mask = (c // CHUNK <= r // CHUNK) if diff else (c <= r)
    (_, la, acca), (_, lb, accb) = both(pl.multiple_of(qi * TQ, TQ), carry, mask)
    oa = acca / la
    ob = accb / lb
    g = g_ref[...]
    if diff:
        lv = lam_ref[...]
        lam = (jnp.exp(jnp.sum(lv[0:1] * lv[1:2], axis=1, keepdims=True))
               - jnp.exp(jnp.sum(lv[2:3] * lv[3:4], axis=1, keepdims=True)) + lambda_init)
        o = oa - lam * ob
        y = o * lax.rsqrt(jnp.mean(o * o, axis=1, keepdims=True) + EPS) * g
        o_ref[...] = (y * (1.0 - lambda_init)).astype(o_ref.dtype)
    else:
        low = _lane_iota((TQ, LANES)) < HEAD_DIM
        o = jnp.where(low, oa, ob)
        sq = o * o
        msa = jnp.sum(jnp.where(low, sq, 0.0), axis=1, keepdims=True) / HEAD_DIM
        msb = jnp.sum(jnp.where(low, 0.0, sq), axis=1, keepdims=True) / HEAD_DIM
        inv = jnp.where(low, lax.rsqrt(msa + EPS), lax.rsqrt(msb + EPS))
        o_ref[...] = (o * inv * g).astype(o_ref.dtype)


def _attention(diff, lambda_init, q, k, v, g, lamv):
    nq = SEQ // TQ
    qspec = lambda par: pl.BlockSpec((TQ, LANES), lambda b, p, i: (b * nq + i, 2 * p + par))
    kspec = lambda par: pl.BlockSpec((SEQ, LANES), lambda b, p, i: (b, 2 * p + par))
    return pl.pallas_call(
        functools.partial(_attn_kernel, diff, lambda_init),
        grid=(BATCH, 4, nq),
        in_specs=[qspec(0), qspec(1), kspec(0), kspec(1),
                  pl.BlockSpec((SEQ, LANES), lambda b, p, i: (b, p)),
                  pl.BlockSpec((1, LANES), lambda b, p, i: (0, 0)),
                  pl.BlockSpec((8, LANES), lambda b, p, i: (0, 0))],
        out_specs=pl.BlockSpec((TQ, LANES), lambda b, p, i: (b * nq + i, p)),
        out_shape=jax.ShapeDtypeStruct((N_TOK, DIFF_WIDTH), BF16),
        compiler_params=_params("arbitrary", "arbitrary", "arbitrary"),
        name="diff_attention" if diff else "fox_attention",
    )(q, q, k, k, v, g, lamv)


def _outproj_kernel(x_ref, od_ref, of_ref, gt_ref, sc_ref, sh_ref, g_ref, wo_ref, wr_ref, br_ref,
                    x1_ref, h2_ref, rt_ref):
    mix = jnp.dot(od_ref[...], wo_ref[0:DIFF_WIDTH, :], preferred_element_type=F32)
    mix += jnp.dot(of_ref[...], wo_ref[DIFF_WIDTH:, :], preferred_element_type=F32)
    x1 = x_ref[...] + gt_ref[...] * mix
    x1_ref[...] = x1
    h = _rms_mod(x1, g_ref[...], sc_ref[...], sh_ref[...])
    hh = h.astype(BF16)
    h2_ref[...] = hh
    hl = (h - hh.astype(F32)).astype(BF16)
    r1 = jnp.dot(hh, wr_ref[...], preferred_element_type=F32)
    r2 = jnp.dot(hl, wr_ref[:, 0:LANES], preferred_element_type=F32)
    logits = r1[:, 0:LANES] + r1[:, LANES:] + r2 + br_ref[...]

    lane = _lane_iota((TM, LANES))
    lanef = lane.astype(F32)
    big = float(LANES)
    isg = lane < N_GROUPS
    lg = jnp.where(isg, logits, NEG)
    mg = jnp.max(lg, axis=1, keepdims=True)
    sg = jnp.sum(jnp.where(isg, jnp.exp(lg - mg), 0.0), axis=1, keepdims=True)
    p_g = 1.0 / sg
    gsel = jnp.min(jnp.where(isg & (lg == mg), lanef, big), axis=1, keepdims=True)
    lo = N_GROUPS + gsel * EXPERTS_PER_GROUP
    ise = (lanef >= lo) & (lanef < lo + EXPERTS_PER_GROUP)
    le = jnp.where(ise, logits, NEG)
    t1 = jnp.max(le, axis=1, keepdims=True)
    i1 = jnp.min(jnp.where(ise & (le == t1), lanef, big), axis=1, keepdims=True)
    ise2 = ise & (lanef != i1)
    le2 = jnp.where(ise2, logits, NEG)
    t2 = jnp.max(le2, axis=1, keepdims=True)
    i2 = jnp.min(jnp.where(ise2 & (le2 == t2), lanef, big), axis=1, keepdims=True)
    d = jnp.exp(t2 - t1)
    w1 = p_g / (1.0 + d)
    w2 = p_g * d / (1.0 + d)
    rt_ref[...] = jnp.where(lane == 0, i1 - N_GROUPS,
                            jnp.where(lane == 1, i2 - N_GROUPS,
                                      jnp.where(lane == 2, w1, jnp.where(lane == 3, w2, 0.0))))


def _outproj(x, od, of, gt, sc, sh, g, wo_bf, wr, br):
    tpb = SEQ // TM
    row = pl.BlockSpec((TM, D_MODEL), lambda i: (i, 0))
    half = pl.BlockSpec((TM, DIFF_WIDTH), lambda i: (i, 0))
    per_batch = pl.BlockSpec((None, 1, D_MODEL), lambda i: (i // tpb, 0, 0))
    const = lambda shape: pl.BlockSpec(shape, lambda i: (0,) * len(shape))
    return pl.pallas_call(
        _outproj_kernel,
        grid=(N_TOK // TM,),
        in_specs=[row, half, half, per_batch, per_batch, per_batch, const((1, D_MODEL)),
                  const((D_MODEL, D_MODEL)), const((D_MODEL, 2 * LANES)), const((1, LANES))],
        out_specs=[row, row, pl.BlockSpec((TM, LANES), lambda i: (i, 0))],
        out_shape=[jax.ShapeDtypeStruct((N_TOK, D_MODEL), F32),
                   jax.ShapeDtypeStruct((N_TOK, D_MODEL), BF16),
                   jax.ShapeDtypeStruct((N_TOK, LANES), F32)],
        compiler_params=_params("arbitrary"),
        name="outproj_router",
    )(x, od, of, gt, sc, sh, g.reshape(1, D_MODEL), wo_bf, wr, br)


def _expert_kernel(be_ref, nb_ref, xs_ref, wg_ref, wu_ref, wd_ref, ys_ref):
    i = pl.program_id(0)

    @pl.when(i < nb_ref[0])
    def _():
        xb = xs_ref[...]
        a = jnp.dot(xb, wg_ref[...], preferred_element_type=F32)
        u = jnp.dot(xb, wu_ref[...], preferred_element_type=F32)
        hid = (a / (1.0 + jnp.exp(-a)) * u).astype(BF16)
        ys_ref[...] = jnp.dot(hid, wd_ref[...], preferred_element_type=F32)

    @pl.when(i >= nb_ref[0])
    def _():
        ys_ref[...] = jnp.zeros_like(ys_ref)


def _experts(block_expert, n_used, xs, wg, wu, wd):
    grid_spec = pltpu.PrefetchScalarGridSpec(
        num_scalar_prefetch=2,
        grid=(MOE_NBLOCKS,),
        in_specs=[pl.BlockSpec((MOE_BLOCK, D_MODEL), lambda i, be, nb: (i, 0)),
                  pl.BlockSpec((None, D_MODEL, D_EXPERT), lambda i, be, nb: (be[i], 0, 0)),
                  pl.BlockSpec((None, D_MODEL, D_EXPERT), lambda i, be, nb: (be[i], 0, 0)),
                  pl.BlockSpec((None, D_EXPERT, D_MODEL), lambda i, be, nb: (be[i], 0, 0))],
        out_specs=pl.BlockSpec((MOE_BLOCK, D_MODEL), lambda i, be, nb: (i, 0)),
    )
    return pl.pallas_call(
        _expert_kernel,
        grid_spec=grid_spec,
        out_shape=jax.ShapeDtypeStruct((MOE_ROWS, D_MODEL), F32),
        compiler_params=_params("arbitrary"),
        name="expert_mlp",
    )(block_expert, n_used, xs, wg, wu, wd)


def _dispatch(route):
    a_tot = N_TOK * TOP_K
    e_flat = route[:, 0:TOP_K].astype(jnp.int32).reshape(a_tot)
    w_flat = route[:, TOP_K:2 * TOP_K].reshape(a_tot)
    order = jnp.argsort(e_flat)
    e_sorted = e_flat[order]
    counts = jnp.zeros((N_EXPERTS,), jnp.int32).at[e_flat].add(1)
    padded = ((counts + MOE_BLOCK - 1) // MOE_BLOCK) * MOE_BLOCK
    start = jnp.cumsum(counts) - counts
    pend = jnp.cumsum(padded)
    pstart = pend - padded
    dest = pstart[e_sorted] + (jnp.arange(a_tot, dtype=jnp.int32) - start[e_sorted])
    slot_tok = jnp.zeros((MOE_ROWS,), jnp.int32).at[dest].set((order // TOP_K).astype(jnp.int32))
    slot_of = jnp.zeros((a_tot,), jnp.int32).at[order].set(dest)
    block_expert = jnp.minimum(
        jnp.searchsorted(pend, jnp.arange(MOE_NBLOCKS, dtype=jnp.int32) * MOE_BLOCK, side='right'),
        N_EXPERTS - 1).astype(jnp.int32)
    n_used = (pend[-1] // MOE_BLOCK).astype(jnp.int32).reshape(1)
    return slot_tok, slot_of.reshape(N_TOK, TOP_K), w_flat.reshape(N_TOK, TOP_K), block_expert, n_used


def _final_kernel(x_ref, d_ref, gt_ref, g_ref, o_ref):
    x = x_ref[...] + gt_ref[...] * d_ref[...]
    ms = jnp.mean(x * x, axis=-1, keepdims=True)
    o_ref[...] = x * lax.rsqrt(ms + EPS) * g_ref[...]


def _final(x, delta, gate, g):
    tpb = SEQ // TM
    row = pl.BlockSpec((TM, D_MODEL), lambda i: (i, 0))
    return pl.pallas_call(
        _final_kernel,
        grid=(N_TOK // TM,),
        in_specs=[row, row, pl.BlockSpec((None, 1, D_MODEL), lambda i: (i // tpb, 0, 0)),
                  pl.BlockSpec((1, D_MODEL), lambda i: (0, 0))],
        out_specs=row,
        out_shape=jax.ShapeDtypeStruct((N_TOK, D_MODEL), F32),
        compiler_params=_params("arbitrary"),
        name="final_norm",
    )(x, delta, gate, g.reshape(1, D_MODEL))


def kernel(x, c, positions, w_ada, b_ada, g_mix, w_in, b_forget, lambda_q1, lambda_k1, lambda_q2,
           lambda_k2, g_subln, g_fox_out, w_out, g_ffn, w_router_group, b_router_group,
           w_router_expert, b_router_expert, w_expert_gate, w_expert_up, w_expert_down, g_final):
    mod = _modulation(c, w_ada, b_ada)
    mod = mod.reshape(DEPTH, BATCH, 6, 1, D_MODEL)
    tables = _rope_tables(positions)
    pq = _forget_placement()
    xf = x.reshape(N_TOK, D_MODEL)
    delta = None
    gate = None
    for l in range(DEPTH):
        sh1, sc1, gt1, sh2, sc2, gt2 = (mod[l, :, j] for j in range(6))
        w_bf = jnp.pad(w_in[l], ((0, 0), (0, IN_COLS_PAD - IN_COLS))).astype(BF16)
        bfp = jnp.pad(b_forget[l], (0, LANES - N_FOX_HEADS)).reshape(1, LANES)
        xf, (dq, dk, dv, fq, fk, fv) = _inproj(xf, delta, gate, sc1, sh1, g_mix[l], w_bf, bfp, tables, pq)

        lambda_init = 0.8 - 0.6 * float(np.exp(-0.3 * l))
        lamv = jnp.zeros((8, LANES), F32).at[0:4, 0:HEAD_DIM].set(
            jnp.stack([lambda_q1[l], lambda_k1[l], lambda_q2[l], lambda_k2[l]]))
        g_d = g_subln[l].reshape(1, LANES)
        g_f = jnp.concatenate([g_fox_out[l], g_fox_out[l]]).reshape(1, LANES)
        od = _attention(True, lambda_init, dq, dk, dv, g_d, lamv)
        of = _attention(False, lambda_init, fq, fk, fv, g_f, lamv)

        wr32 = jnp.pad(jnp.concatenate([w_router_group[l], w_router_expert[l]], axis=1),
                       ((0, 0), (0, LANES - N_GROUPS - N_EXPERTS)))
        wr_hi = wr32.astype(BF16)
        wr_lo = (wr32 - wr_hi.astype(F32)).astype(BF16)
        wr = jnp.concatenate([wr_hi, wr_lo], axis=1)
        br = jnp.pad(jnp.concatenate([b_router_group[l], b_router_expert[l]]),
                     (0, LANES - N_GROUPS - N_EXPERTS)).reshape(1, LANES)
        xf, h2, route = _outproj(xf, od, of, gt1, sc2, sh2, g_ffn[l], w_out[l].astype(BF16), wr, br)

        slot_tok, slot_of, w_tok, block_expert, n_used = _dispatch(route)
        xs = h2[slot_tok]
        ys = _experts(block_expert, n_used, xs, w_expert_gate[l].astype(BF16),
                      w_expert_up[l].astype(BF16), w_expert_down[l].astype(BF16))
        delta = ys[slot_of[:, 0]] * w_tok[:, 0:1] + ys[slot_of[:, 1]] * w_tok[:, 1:2]
        gate = gt2
    out = _final(xf, delta, gate, g_final)
    return out.reshape(BATCH, SEQ, D_MODEL)
```

```python
import functools

import numpy as np
import jax
import jax.numpy as jnp
from jax import lax
from jax.experimental import pallas as pl
from jax.experimental.pallas import tpu as pltpu
from jax.experimental.pallas import tpu_sc as plsc

D_MODEL = 1024
BATCH = 4
SEQ = 4096
DEPTH = 4
N_TOK = BATCH * SEQ

CHUNK = 64
HEAD_DIM = 64
N_DIFF_HEADS = 4
N_FOX_HEADS = 8
DIFF_WIDTH = 512
FOX_WIDTH = 512
IN_COLS = 3 * DIFF_WIDTH + 3 * FOX_WIDTH + N_FOX_HEADS
ROT_DIM = 16
ROPE_THETA = 500000.0
N_GROUPS = 4
EXPERTS_PER_GROUP = 8
N_EXPERTS = 32
TOP_K = 2
D_EXPERT = 512
EPS = 1e-6

LANES = 128
IN_COLS_PAD = 3200
FF_COL = 3 * DIFF_WIDTH + 3 * FOX_WIDTH
QK_WIDTH = 8 * LANES
TM = 512
TQ = 512
MOE_BLOCK = 256
MOE_ROWS = N_TOK * TOP_K + N_EXPERTS * MOE_BLOCK
MOE_NBLOCKS = MOE_ROWS // MOE_BLOCK
N_PLANES = D_MODEL // 2 // LANES
SC_WINDOW = 128
NEG = -1e30
VMEM_LIMIT = 56 * 1024 * 1024

F32 = jnp.float32
BF16 = jnp.bfloat16


def _bf16_round(x):
    return x.astype(BF16).astype(F32)


def _lane_iota(shape):
    return lax.broadcasted_iota(jnp.int32, shape, 1)


def _params(*sem):
    return pltpu.CompilerParams(dimension_semantics=sem, vmem_limit_bytes=VMEM_LIMIT)


def _pack_planes(y, o_ref):
    bits = lax.bitcast_convert_type(_bf16_round(y), jnp.uint32)
    half = D_MODEL // 2
    word = bits[:, half:] | lax.shift_right_logical(bits[:, :half], jnp.uint32(16))
    word = lax.bitcast_convert_type(word, jnp.int32)
    for p in range(N_PLANES):
        o_ref[p] = word[:, p * LANES:(p + 1) * LANES]


def _unpack_planes(planes):
    lo, hi = [], []
    for w in planes:
        u = lax.bitcast_convert_type(w, jnp.uint32)
        lo.append(lax.bitcast_convert_type(lax.shift_left(u, jnp.uint32(16)), F32))
        hi.append(lax.bitcast_convert_type(u & jnp.uint32(0xFFFF0000), F32))
    return jnp.concatenate(lo + hi, axis=1)


def _combine(route_ref, y_ref):
    rt = route_ref[...]
    y0 = _unpack_planes([y_ref[0, p] for p in range(N_PLANES)])
    y1 = _unpack_planes([y_ref[1, p] for p in range(N_PLANES)])
    return rt[:, 2:3] * y0 + rt[:, 3:4] * y1


def _mod_kernel(c_ref, w_ref, b_ref, o_ref):
    c = c_ref[...]
    cond = c / (1.0 + jnp.exp(-c))
    ch = cond.astype(BF16)
    cl = (cond - ch.astype(F32)).astype(BF16)
    w = w_ref[...]
    wh = w.astype(BF16)
    wl = (w - wh.astype(F32)).astype(BF16)
    acc = jnp.dot(ch, wh, preferred_element_type=F32)
    acc += jnp.dot(cl, wh, preferred_element_type=F32)
    acc += jnp.dot(ch, wl, preferred_element_type=F32)
    o_ref[...] = acc + b_ref[...]


def _modulation(c, w_ada, b_ada):
    rows = 16
    tn = 1536
    c_pad = jnp.zeros((rows, D_MODEL), F32).at[:BATCH].set(c)
    out = pl.pallas_call(
        _mod_kernel,
        grid=(DEPTH, 6 * D_MODEL // tn),
        in_specs=[
            pl.BlockSpec((rows, D_MODEL), lambda l, n: (0, 0)),
            pl.BlockSpec((None, D_MODEL, tn), lambda l, n: (l, 0, n)),
            pl.BlockSpec((None, 1, tn), lambda l, n: (l, 0, n)),
        ],
        out_specs=pl.BlockSpec((None, rows, tn), lambda l, n: (l, 0, n)),
        out_shape=jax.ShapeDtypeStruct((DEPTH, rows, 6 * D_MODEL), F32),
        compiler_params=_params("arbitrary", "arbitrary"),
        name="adaln_mod",
    )(c_pad, w_ada, b_ada.reshape(DEPTH, 1, 6 * D_MODEL))
    return out[:, :BATCH]


def _rope_kernel(pos_ref, inv_ref, c_ref, sa_ref, sb_ref):
    ang = pos_ref[...].astype(F32) * inv_ref[...]
    j = _lane_iota(ang.shape) % HEAD_DIM
    cosv = jnp.cos(ang)
    sinv = jnp.sin(ang)
    half = ROT_DIM // 2
    c_ref[...] = jnp.where(j < ROT_DIM, cosv, 1.0)
    sa_ref[...] = jnp.where(j < half, -sinv, 0.0)
    sb_ref[...] = jnp.where((j >= half) & (j < ROT_DIM), sinv, 0.0)


def _rope_tables(positions):
    half = ROT_DIM // 2
    inv = ROPE_THETA ** (-jnp.arange(0, ROT_DIM, 2, dtype=F32) / ROT_DIM)
    lane = np.arange(LANES)
    inv_lane = inv[(lane % HEAD_DIM) % half].reshape(1, LANES)
    spec = pl.BlockSpec((TM, LANES), lambda i: (i, 0))
    shape = jax.ShapeDtypeStruct((N_TOK, LANES), F32)
    return pl.pallas_call(
        _rope_kernel,
        grid=(N_TOK // TM,),
        in_specs=[pl.BlockSpec((TM, 1), lambda i: (i, 0)),
                  pl.BlockSpec((1, LANES), lambda i: (0, 0))],
        out_specs=[spec, spec, spec],
        out_shape=[shape, shape, shape],
        compiler_params=_params("arbitrary"),
        name="rope_tables",
    )(positions.reshape(N_TOK, 1), inv_lane)


def _rms_mod(x, g, sc, sh):
    ms = jnp.mean(x * x, axis=-1, keepdims=True)
    return (x * lax.rsqrt(ms + EPS) * g) * (1.0 + sc) + sh


def _inproj_kernel(fuse, *refs):
    if fuse:
        (x_ref, rt_ref, y_ref, gt_ref, sc_ref, sh_ref, g_ref, w_ref, bf_ref, c_ref, sa_ref, sb_ref,
         pq_ref, xo_ref, dq_ref, dk_ref, dv_ref, fq_ref, fk_ref, fv_ref, carry_ref) = refs
        x = x_ref[...] + gt_ref[...] * _combine(rt_ref, y_ref)
        xo_ref[...] = x
    else:
        (x_ref, sc_ref, sh_ref, g_ref, w_ref, bf_ref, c_ref, sa_ref, sb_ref,
         pq_ref, dq_ref, dk_ref, dv_ref, fq_ref, fk_ref, fv_ref, carry_ref) = refs
        x = x_ref[...]
    hb = _rms_mod(x, g_ref[...], sc_ref[...], sh_ref[...]).astype(BF16)

    @pl.when(pl.program_id(0) % (SEQ // TM) == 0)
    def _():
        carry_ref[...] = jnp.zeros_like(carry_ref)

    lane = _lane_iota((TM, LANES))
    nh = N_FOX_HEADS

    def pack3(a):
        hi = _bf16_round(a)
        r1 = a - hi
        mid = _bf16_round(r1)
        lo = _bf16_round(r1 - mid)
        return jnp.where(lane < nh, hi,
                         jnp.where(lane < 2 * nh, pltpu.roll(mid, nh, 1),
                                   jnp.where(lane < 3 * nh, pltpu.roll(lo, 2 * nh, 1), 0.0)))

    z = jnp.dot(hb, w_ref[:, FF_COL:FF_COL + LANES], preferred_element_type=F32) + bf_ref[...]
    logf = jnp.minimum(z, 0.0) - jnp.log(1.0 + jnp.exp(-jnp.abs(z)))
    logf = jnp.where(lane < nh, logf, 0.0)
    row = lax.broadcasted_iota(jnp.int32, (TM, TM), 0)
    col = lax.broadcasted_iota(jnp.int32, (TM, TM), 1)
    tri = (row >= col).astype(BF16)
    r = jnp.dot(tri, pack3(logf).astype(BF16), preferred_element_type=F32)
    cs = r + pltpu.roll(r, LANES - nh, 1) + pltpu.roll(r, LANES - 2 * nh, 1)
    cf = jnp.where(lane < nh, cs + carry_ref[0:1, :], 0.0)
    carry_ref[...] = jnp.broadcast_to(cf[TM - 1:TM, :], carry_ref.shape)

    t3 = jnp.where(lane == 3 * nh, 1.0, pack3(cf)).astype(BF16)
    aug = jnp.dot(t3, pq_ref[...], preferred_element_type=F32)

    low = lane < HEAD_DIM
    rc, rsa, rsb = c_ref[...], sa_ref[...], sb_ref[...]
    scale = HEAD_DIM ** -0.5

    def split_store(chunk, o_ref, m, extra_a=None, extra_b=None):
        a = jnp.where(low, chunk, 0.0)
        b = jnp.where(low, pltpu.roll(chunk, HEAD_DIM, 1), 0.0)
        if extra_a is not None:
            a = a + extra_a
            b = b + extra_b
        o_ref[:, (2 * m) * LANES:(2 * m + 1) * LANES] = a.astype(BF16)
        o_ref[:, (2 * m + 1) * LANES:(2 * m + 2) * LANES] = b.astype(BF16)

    def rope(xc):
        return xc * rc + pltpu.roll(xc, LANES - ROT_DIM // 2, 1) * rsa + pltpu.roll(xc, ROT_DIM // 2, 1) * rsb

    pdq = jnp.dot(hb, w_ref[:, 0:DIFF_WIDTH], preferred_element_type=F32)
    for m in range(N_DIFF_HEADS):
        split_store(rope(pdq[:, m * LANES:(m + 1) * LANES]) * scale, dq_ref, m)
    pdk = jnp.dot(hb, w_ref[:, DIFF_WIDTH:2 * DIFF_WIDTH], preferred_element_type=F32)
    for m in range(N_DIFF_HEADS):
        split_store(rope(pdk[:, m * LANES:(m + 1) * LANES]), dk_ref, m)
    dv_ref[...] = jnp.dot(hb, w_ref[:, 2 * DIFF_WIDTH:3 * DIFF_WIDTH],
                          preferred_element_type=F32).astype(BF16)
    o = 3 * DIFF_WIDTH
    pfq = jnp.dot(hb, w_ref[:, o:o + FOX_WIDTH], preferred_element_type=F32)
    for m in range(N_FOX_HEADS // 2):
        split_store(pfq[:, m * LANES:(m + 1) * LANES] * scale, fq_ref, m,
                    aug[:, (2 * m) * LANES:(2 * m + 1) * LANES],
                    aug[:, (2 * m + 1) * LANES:(2 * m + 2) * LANES])
    pfk = jnp.dot(hb, w_ref[:, o + FOX_WIDTH:o + 2 * FOX_WIDTH], preferred_element_type=F32)
    for m in range(N_FOX_HEADS // 2):
        split_store(pfk[:, m * LANES:(m + 1) * LANES], fk_ref, m,
                    aug[:, QK_WIDTH + (2 * m) * LANES:QK_WIDTH + (2 * m + 1) * LANES],
                    aug[:, QK_WIDTH + (2 * m + 1) * LANES:QK_WIDTH + (2 * m + 2) * LANES])
    fv_ref[...] = jnp.dot(hb, w_ref[:, o + 2 * FOX_WIDTH:o + 3 * FOX_WIDTH],
                          preferred_element_type=F32).astype(BF16)


def _forget_placement():
    nh = N_FOX_HEADS
    p = np.zeros((LANES, 2 * QK_WIDTH), np.float32)
    for h in range(nh):
        base_q = h * LANES + HEAD_DIM
        base_k = QK_WIDTH + h * LANES + HEAD_DIM
        for part in range(3):
            p[part * nh + h, base_q + part] = 1.0
            p[3 * nh, base_q + 3 + part] = 1.0
            p[3 * nh, base_k + part] = 1.0
            p[part * nh + h, base_k + 3 + part] = -1.0
    return jnp.asarray(p, BF16)


def _inproj(x, moe, gate, sc, sh, g, w_bf, b_forget, tables, pq):
    fuse = moe is not None
    tpb = SEQ // TM
    row = pl.BlockSpec((TM, D_MODEL), lambda i: (i, 0))
    per_batch = pl.BlockSpec((None, 1, D_MODEL), lambda i: (i // tpb, 0, 0))
    const = lambda shape: pl.BlockSpec(shape, lambda i: (0,) * len(shape))
    tab = pl.BlockSpec((TM, LANES), lambda i: (i, 0))
    in_specs = [row]
    args = [x]
    if fuse:
        in_specs += [tab, pl.BlockSpec((TOP_K, N_PLANES, TM, LANES), lambda i: (0, 0, i, 0)), per_batch]
        args += [moe[0], moe[1], gate]
    in_specs += [per_batch, per_batch, const((1, D_MODEL)), const((D_MODEL, IN_COLS_PAD)),
                 const((1, LANES)), tab, tab, tab, const((LANES, 2 * QK_WIDTH))]
    args += [sc, sh, g.reshape(1, D_MODEL), w_bf, b_forget, *tables, pq]
    wide = pl.BlockSpec((TM, QK_WIDTH), lambda i: (i, 0))
    half = pl.BlockSpec((TM, DIFF_WIDTH), lambda i: (i, 0))
    wide_s = jax.ShapeDtypeStruct((N_TOK, QK_WIDTH), BF16)
    half_s = jax.ShapeDtypeStruct((N_TOK, DIFF_WIDTH), BF16)
    out_specs = [wide, wide, half, wide, wide, half]
    out_shape = [wide_s, wide_s, half_s, wide_s, wide_s, half_s]
    if fuse:
        out_specs = [row] + out_specs
        out_shape = [jax.ShapeDtypeStruct((N_TOK, D_MODEL), F32)] + out_shape
    outs = pl.pallas_call(
        functools.partial(_inproj_kernel, fuse),
        grid=(N_TOK // TM,),
        in_specs=in_specs,
        out_specs=out_specs,
        out_shape=out_shape,
        scratch_shapes=[pltpu.VMEM((8, LANES), F32)],
        compiler_params=_params("arbitrary"),
        name="norm_inproj",
    )(*args)
    if fuse:
        return outs[0], outs[1:]
    return x, outs


def _attn_kernel(diff, lambda_init, qa_ref, qb_ref, ka_ref, kb_ref, v_ref, g_ref, lam_ref, o_ref):
    qi = pl.program_id(2)
    qa = qa_ref[...]
    qb = qb_ref[...]
    nt = (((1,), (1,)), ((), ()))

    def step(q, kblk, vblk, carry, mask):
        m, l, acc = carry
        s = lax.dot_general(q, kblk, nt, preferred_element_type=F32)
        if mask is not None:
            s = jnp.where(mask, s, NEG)
        m_new = jnp.maximum(m, jnp.max(s, axis=1, keepdims=True))
        alpha = jnp.exp(m - m_new)
        p = jnp.exp(s - m_new)
        l = alpha * l + jnp.sum(p, axis=1, keepdims=True)
        acc = alpha * acc + jnp.dot(p.astype(BF16), vblk, preferred_element_type=F32)
        return m_new, l, acc

    def both(off, carry, mask):
        ca, cb = carry
        vblk = v_ref[pl.ds(off, TQ), :]
        ca = step(qa, ka_ref[pl.ds(off, TQ), :], vblk, ca, mask)
        cb = step(qb, kb_ref[pl.ds(off, TQ), :], vblk, cb, mask)
        return ca, cb

    def init():
        return (jnp.full((TQ, 1), NEG, F32), jnp.zeros((TQ, 1), F32), jnp.zeros((TQ, LANES), F32))

    carry = lax.fori_loop(0, qi, lambda j, c: both(pl.multiple_of(j * TQ, TQ), c, None),
                          (init(), init()))
    r = lax.broadcasted_iota(jnp.int32, (TQ, TQ), 0)
    c = lax.broadcasted_iota(jnp.int32, (TQ, TQ), 1)
    mask = (c // CHUNK <= r // CHUNK) if diff else (c <= r)
    (_, la, acca), (_, lb, accb) = both(pl.multiple_of(qi * TQ, TQ), carry, mask)
    oa = acca / la
    ob = accb / lb
    g = g_ref[...]
    if diff:
        lv = lam_ref[...]
        lam = (jnp.exp(jnp.sum(lv[0:1] * lv[1:2], axis=1, keepdims=True))
               - jnp.exp(jnp.sum(lv[2:3] * lv[3:4], axis=1, keepdims=True)) + lambda_init)
        o = oa - lam * ob
        y = o * lax.rsqrt(jnp.mean(o * o, axis=1, keepdims=True) + EPS) * g
        o_ref[...] = (y * (1.0 - lambda_init)).astype(o_ref.dtype)
    else:
        low = _lane_iota((TQ, LANES)) < HEAD_DIM
        o = jnp.where(low, oa, ob)
        sq = o * o
        msa = jnp.sum(jnp.where(low, sq, 0.0), axis=1, keepdims=True) / HEAD_DIM
        msb = jnp.sum(jnp.where(low, 0.0, sq), axis=1, keepdims=True) / HEAD_DIM
        inv = jnp.where(low, lax.rsqrt(msa + EPS), lax.rsqrt(msb + EPS))
        o_ref[...] = (o * inv * g).astype(o_ref.dtype)


def _attention(diff, lambda_init, q, k, v, g, lamv):
    nq = SEQ // TQ
    qspec = lambda par: pl.BlockSpec((TQ, LANES), lambda b, p, i: (b * nq + i, 2 * p + par))
    kspec = lambda par: pl.BlockSpec((SEQ, LANES), lambda b, p, i: (b, 2 * p + par))
    return pl.pallas_call(
        functools.partial(_attn_kernel, diff, lambda_init),
        grid=(BATCH, 4, nq),
        in_specs=[qspec(0), qspec(1), kspec(0), kspec(1),
                  pl.BlockSpec((SEQ, LANES), lambda b, p, i: (b, p)),
                  pl.BlockSpec((1, LANES), lambda b, p, i: (0, 0)),
                  pl.BlockSpec((8, LANES), lambda b, p, i: (0, 0))],
        out_specs=pl.BlockSpec((TQ, LANES), lambda b, p, i: (b * nq + i, p)),
        out_shape=jax.ShapeDtypeStruct((N_TOK, DIFF_WIDTH), BF16),
        compiler_params=_params("arbitrary", "arbitrary", "arbitrary"),
        name="diff_attention" if diff else "fox_attention",
    )(q, q, k, k, v, g, lamv)


def _outproj_kernel(x_ref, od_ref, of_ref, gt_ref, sc_ref, sh_ref, g_ref, wo_ref, wr_ref, br_ref,
                    x1_ref, h2_ref, rt_ref, cnt_ref, carry_ref):
    @pl.when(pl.program_id(0) == 0)
    def _():
        carry_ref[...] = jnp.zeros_like(carry_ref)

    mix = jnp.dot(od_ref[...], wo_ref[0:DIFF_WIDTH, :], preferred_element_type=F32)
    mix += jnp.dot(of_ref[...], wo_ref[DIFF_WIDTH:, :], preferred_element_type=F32)
    x1 = x_ref[...] + gt_ref[...] * mix
    x1_ref[...] = x1
    h = _rms_mod(x1, g_ref[...], sc_ref[...], sh_ref[...])
    hh = h.astype(BF16)
    _pack_planes(h, h2_ref)
    hl = (h - hh.astype(F32)).astype(BF16)
    r1 = jnp.dot(hh, wr_ref[...], preferred_element_type=F32)
    r2 = jnp.dot(hl, wr_ref[:, 0:LANES], preferred_element_type=F32)
    logits = r1[:, 0:LANES] + r1[:, LANES:] + r2 + br_ref[...]

    lane = _lane_iota((TM, LANES))
    lanef = lane.astype(F32)
    big = float(LANES)
    isg = lane < N_GROUPS
    lg = jnp.where(isg, logits, NEG)
    mg = jnp.max(lg, axis=1, keepdims=True)
    sg = jnp.sum(jnp.where(isg, jnp.exp(lg - mg), 0.0), axis=1, keepdims=True)
    p_g = 1.0 / sg
    gsel = jnp.min(jnp.where(isg & (lg == mg), lanef, big), axis=1, keepdims=True)
    lo = N_GROUPS + gsel * EXPERTS_PER_GROUP
    ise = (lanef >= lo) & (lanef < lo + EXPERTS_PER_GROUP)
    le = jnp.where(ise, logits, NEG)
    t1 = jnp.max(le, axis=1, keepdims=True)
    i1 = jnp.min(jnp.where(ise & (le == t1), lanef, big), axis=1, keepdims=True)
    ise2 = ise & (lanef != i1)
    le2 = jnp.where(ise2, logits, NEG)
    t2 = jnp.max(le2, axis=1, keepdims=True)
    i2 = jnp.min(jnp.where(ise2 & (le2 == t2), lanef, big), axis=1, keepdims=True)
    d = jnp.exp(t2 - t1)
    w1 = p_g / (1.0 + d)
    w2 = p_g * d / (1.0 + d)
    e1 = i1 - N_GROUPS
    e2 = i2 - N_GROUPS
    oh1 = lanef == e1
    oh2 = lanef == e2
    both = jnp.where(oh1 | oh2, 1.0, 0.0)
    row = lax.broadcasted_iota(jnp.int32, (TM, TM), 0)
    col = lax.broadcasted_iota(jnp.int32, (TM, TM), 1)
    before = jnp.dot((row > col).astype(BF16), both.astype(BF16), preferred_element_type=F32)
    before = before + carry_ref[0:1, :]
    rank1 = jnp.sum(jnp.where(oh1, before, 0.0), axis=1, keepdims=True)
    rank2 = jnp.sum(jnp.where(oh2, before, 0.0), axis=1, keepdims=True)
    total = carry_ref[0:1, :] + jnp.sum(both, axis=0, keepdims=True)
    carry_ref[...] = jnp.broadcast_to(total, carry_ref.shape)
    cnt_ref[...] = jnp.broadcast_to(total, cnt_ref.shape)
    vals = (e1, e2, w1, w2, rank1, rank2)
    out = jnp.zeros((TM, LANES), F32)
    for j, v in enumerate(vals):
        out = jnp.where(lane == j, v, out)
    rt_ref[...] = out


def _outproj(x, od, of, gt, sc, sh, g, wo_bf, wr, br):
    tpb = SEQ // TM
    row = pl.BlockSpec((TM, D_MODEL), lambda i: (i, 0))
    half = pl.BlockSpec((TM, DIFF_WIDTH), lambda i: (i, 0))
    per_batch = pl.BlockSpec((None, 1, D_MODEL), lambda i: (i // tpb, 0, 0))
    const = lambda shape: pl.BlockSpec(shape, lambda i: (0,) * len(shape))
    return pl.pallas_call(
        _outproj_kernel,
        grid=(N_TOK // TM,),
        in_specs=[row, half, half, per_batch, per_batch, per_batch, const((1, D_MODEL)),
                  const((D_MODEL, D_MODEL)), const((D_MODEL, 2 * LANES)), const((1, LANES))],
        out_specs=[row, pl.BlockSpec((N_PLANES, TM, LANES), lambda i: (0, i, 0)),
                   pl.BlockSpec((TM, LANES), lambda i: (i, 0)), const((8, LANES))],
        out_shape=[jax.ShapeDtypeStruct((N_TOK, D_MODEL), F32),
                   jax.ShapeDtypeStruct((N_PLANES, N_TOK, LANES), jnp.int32),
                   jax.ShapeDtypeStruct((N_TOK, LANES), F32),
                   jax.ShapeDtypeStruct((8, LANES), F32)],
        scratch_shapes=[pltpu.VMEM((8, LANES), F32)],
        compiler_params=_params("arbitrary"),
        name="outproj_router",
    )(x, od, of, gt, sc, sh, g.reshape(1, D_MODEL), wo_bf, wr, br)


def _expert_kernel(be_ref, cnt_ref, xs_ref, wg_ref, wu_ref, wd_ref, ys_ref):
    i = pl.program_id(0)
    cnt = cnt_ref[i]

    @pl.when(cnt > 0)
    def _():
        live = lax.broadcasted_iota(jnp.int32, (MOE_BLOCK, LANES), 0) < cnt
        xb = _unpack_planes([jnp.where(live, xs_ref[p], 0) for p in range(N_PLANES)]).astype(BF16)
        a = jnp.dot(xb, wg_ref[...], preferred_element_type=F32)
        u = jnp.dot(xb, wu_ref[...], preferred_element_type=F32)
        hid = (a / (1.0 + jnp.exp(-a)) * u).astype(BF16)
        _pack_planes(jnp.dot(hid, wd_ref[...], preferred_element_type=F32), ys_ref)

    @pl.when(cnt == 0)
    def _():
        ys_ref[...] = jnp.zeros_like(ys_ref)


def _experts(block_expert, block_count, xs, wg, wu, wd):
    planes = pl.BlockSpec((N_PLANES, MOE_BLOCK, LANES), lambda i, be, bc: (0, i, 0))
    grid_spec = pltpu.PrefetchScalarGridSpec(
        num_scalar_prefetch=2,
        grid=(MOE_NBLOCKS,),
        in_specs=[planes,
                  pl.BlockSpec((None, D_MODEL, D_EXPERT), lambda i, be, bc: (be[i], 0, 0)),
                  pl.BlockSpec((None, D_MODEL, D_EXPERT), lambda i, be, bc: (be[i], 0, 0)),
                  pl.BlockSpec((None, D_EXPERT, D_MODEL), lambda i, be, bc: (be[i], 0, 0))],
        out_specs=planes,
    )
    return pl.pallas_call(
        _expert_kernel,
        grid_spec=grid_spec,
        out_shape=jax.ShapeDtypeStruct((N_PLANES, MOE_ROWS, LANES), jnp.int32),
        compiler_params=_params("arbitrary"),
        name="expert_mlp",
    )(block_expert, block_count, xs, wg, wu, wd)


def _slots(route, counts):
    counts = counts[0, :N_EXPERTS].astype(jnp.int32)
    padded = ((counts + MOE_BLOCK - 1) // MOE_BLOCK) * MOE_BLOCK
    pend = jnp.cumsum(padded)
    pstart = pend - padded
    e = route[:, 0:TOP_K].astype(jnp.int32)
    rank = route[:, 2 * TOP_K:3 * TOP_K].astype(jnp.int32)
    onehot = e[:, :, None] == jnp.arange(N_EXPERTS, dtype=jnp.int32)
    dest = jnp.sum(jnp.where(onehot, pstart, 0), axis=-1) + rank
    bstart = jnp.arange(MOE_NBLOCKS, dtype=jnp.int32) * MOE_BLOCK
    block_expert = jnp.minimum(jnp.sum(bstart[:, None] >= pend[None, :], axis=1), N_EXPERTS - 1)
    block_expert = block_expert.astype(jnp.int32)
    block_count = jnp.clip(counts[block_expert] - (bstart - pstart[block_expert]), 0, MOE_BLOCK)
    block_count = jnp.where(bstart < pend[-1], block_count, 0).astype(jnp.int32)
    plane_off = jnp.arange(N_PLANES, dtype=jnp.int32) * MOE_ROWS
    rows = dest.T[:, None, :] + plane_off[None, :, None]
    return rows.astype(jnp.int32), block_expert, block_count


def _sc_workers():
    info = plsc.get_sparse_core_info()
    return info.num_cores, info.num_cores * info.num_subcores


def _sc_scatter2(src, idx_a, idx_b, out_rows):
    n_src = src.shape[0]
    nc, nw = _sc_workers()
    per_w = n_src // nw
    steps = per_w // SC_WINDOW
    mesh = plsc.VectorSubcoreMesh(core_axis_name="c", subcore_axis_name="s")

    @functools.partial(
        pl.kernel, mesh=mesh,
        out_type=jax.ShapeDtypeStruct((out_rows, LANES), src.dtype),
        scratch_types=[pltpu.VMEM((SC_WINDOW,), jnp.int32), pltpu.VMEM((SC_WINDOW,), jnp.int32),
                       pltpu.VMEM((SC_WINDOW, LANES), src.dtype)],
        name="sc_dispatch_scatter",
    )
    def k(src_hbm, ia_hbm, ib_hbm, out_hbm, ia_v, ib_v, rows_v):
        base = (lax.axis_index("s") * nc + lax.axis_index("c")) * per_w

        @pl.loop(0, steps)
        def _(j):
            off = base + j * SC_WINDOW
            pltpu.sync_copy(ia_hbm.at[pl.ds(off, SC_WINDOW)], ia_v)
            pltpu.sync_copy(ib_hbm.at[pl.ds(off, SC_WINDOW)], ib_v)
            pltpu.sync_copy(src_hbm.at[pl.ds(off, SC_WINDOW)], rows_v)
            pltpu.sync_copy(rows_v, out_hbm.at[ia_v])
            pltpu.sync_copy(rows_v, out_hbm.at[ib_v])

    return k(src, idx_a, idx_b)


def _sc_gather(table, idx):
    n_out = idx.shape[0]
    nc, nw = _sc_workers()
    per_w = n_out // nw
    steps = per_w // SC_WINDOW
    mesh = plsc.VectorSubcoreMesh(core_axis_name="c", subcore_axis_name="s")

    @functools.partial(
        pl.kernel, mesh=mesh,
        out_type=jax.ShapeDtypeStruct((n_out, LANES), table.dtype),
        scratch_types=[pltpu.VMEM((SC_WINDOW,), jnp.int32), pltpu.VMEM((SC_WINDOW, LANES), table.dtype)],
        name="sc_combine_gather",
    )
    def k(table_hbm, idx_hbm, out_hbm, idx_v, rows_v):
        base = (lax.axis_index("s") * nc + lax.axis_index("c")) * per_w

        @pl.loop(0, steps)
        def _(j):
            off = base + j * SC_WINDOW
            pltpu.sync_copy(idx_hbm.at[pl.ds(off, SC_WINDOW)], idx_v)
            pltpu.sync_copy(table_hbm.at[idx_v], rows_v)
            pltpu.sync_copy(rows_v, out_hbm.at[pl.ds(off, SC_WINDOW)])

    return k(table, idx)


def _final_kernel(x_ref, rt_ref, y_ref, gt_ref, g_ref, o_ref):
    x = x_ref[...] + gt_ref[...] * _combine(rt_ref, y_ref)
    ms = jnp.mean(x * x, axis=-1, keepdims=True)
    o_ref[...] = x * lax.rsqrt(ms + EPS) * g_ref[...]


def _final(x, moe, gate, g):
    tpb = SEQ // TM
    row = pl.BlockSpec((TM, D_MODEL), lambda i: (i, 0))
    return pl.pallas_call(
        _final_kernel,
        grid=(N_TOK // TM,),
        in_specs=[row, pl.BlockSpec((TM, LANES), lambda i: (i, 0)),
                  pl.BlockSpec((TOP_K, N_PLANES, TM, LANES), lambda i: (0, 0, i, 0)),
                  pl.BlockSpec((None, 1, D_MODEL), lambda i: (i // tpb, 0, 0)),
                  pl.BlockSpec((1, D_MODEL), lambda i: (0, 0))],
        out_specs=row,
        out_shape=jax.ShapeDtypeStruct((N_TOK, D_MODEL), F32),
        compiler_params=_params("arbitrary"),
        name="final_norm",
    )(x, moe[0], moe[1], gate, g.reshape(1, D_MODEL))


def kernel(x, c, positions, w_ada, b_ada, g_mix, w_in, b_forget, lambda_q1, lambda_k1, lambda_q2,
           lambda_k2, g_subln, g_fox_out, w_out, g_ffn, w_router_group, b_router_group,
           w_router_expert, b_router_expert, w_expert_gate, w_expert_up, w_expert_down, g_final):
    mod = _modulation(c, w_ada, b_ada)
    mod = mod.reshape(DEPTH, BATCH, 6, 1, D_MODEL)
    tables = _rope_tables(positions)
    pq = _forget_placement()
    xf = x.reshape(N_TOK, D_MODEL)
    moe = None
    gate = None
    for l in range(DEPTH):
        sh1, sc1, gt1, sh2, sc2, gt2 = (mod[l, :, j] for j in range(6))
        w_bf = jnp.pad(w_in[l], ((0, 0), (0, IN_COLS_PAD - IN_COLS))).astype(BF16)
        bfp = jnp.pad(b_forget[l], (0, LANES - N_FOX_HEADS)).reshape(1, LANES)
        xf, (dq, dk, dv, fq, fk, fv) = _inproj(xf, moe, gate, sc1, sh1, g_mix[l], w_bf, bfp, tables, pq)

        lambda_init = 0.8 - 0.6 * float(np.exp(-0.3 * l))
        lamv = jnp.zeros((8, LANES), F32).at[0:4, 0:HEAD_DIM].set(
            jnp.stack([lambda_q1[l], lambda_k1[l], lambda_q2[l], lambda_k2[l]]))
        g_d = g_subln[l].reshape(1, LANES)
        g_f = jnp.concatenate([g_fox_out[l], g_fox_out[l]]).reshape(1, LANES)
        od = _attention(True, lambda_init, dq, dk, dv, g_d, lamv)
        of = _attention(False, lambda_init, fq, fk, fv, g_f, lamv)

        wr32 = jnp.pad(jnp.concatenate([w_router_group[l], w_router_expert[l]], axis=1),
                       ((0, 0), (0, LANES - N_GROUPS - N_EXPERTS)))
        wr_hi = wr32.astype(BF16)
        wr_lo = (wr32 - wr_hi.astype(F32)).astype(BF16)
        wr = jnp.concatenate([wr_hi, wr_lo], axis=1)
        br = jnp.pad(jnp.concatenate([b_router_group[l], b_router_expert[l]]),
                     (0, LANES - N_GROUPS - N_EXPERTS)).reshape(1, LANES)
        xf, h2, route, counts = _outproj(xf, od, of, gt1, sc2, sh2, g_ffn[l], w_out[l].astype(BF16),
                                         wr, br)

        rows, block_expert, block_count = _slots(route, counts)
        xs = _sc_scatter2(h2.reshape(N_PLANES * N_TOK, LANES), rows[0].reshape(-1),
                          rows[1].reshape(-1), N_PLANES * MOE_ROWS)
        ys = _experts(block_expert, block_count, xs.reshape(N_PLANES, MOE_ROWS, LANES),
                      w_expert_gate[l].astype(BF16), w_expert_up[l].astype(BF16),
                      w_expert_down[l].astype(BF16))
        y2 = _sc_gather(ys.reshape(N_PLANES * MOE_ROWS, LANES), rows.reshape(-1))
        moe = (route, y2.reshape(TOP_K, N_PLANES, N_TOK, LANES))
        gate = gt2
    out = _final(xf, moe, gate, g_final)
    return out.reshape(BATCH, SEQ, D_MODEL)
```

```python
import functools

import numpy as np
import jax
import jax.numpy as jnp
from jax import lax
from jax.experimental import pallas as pl
from jax.experimental.pallas import tpu as pltpu
from jax.experimental.pallas import tpu_sc as plsc

D_MODEL = 1024
BATCH = 4
SEQ = 4096
DEPTH = 4
N_TOK = BATCH * SEQ

CHUNK = 64
HEAD_DIM = 64
N_DIFF_HEADS = 4
N_FOX_HEADS = 8
DIFF_WIDTH = 512
FOX_WIDTH = 512
IN_COLS = 3 * DIFF_WIDTH + 3 * FOX_WIDTH + N_FOX_HEADS
ROT_DIM = 16
ROPE_THETA = 500000.0
N_GROUPS = 4
EXPERTS_PER_GROUP = 8
N_EXPERTS = 32
TOP_K = 2
D_EXPERT = 512
EPS = 1e-6

LANES = 128
IN_COLS_PAD = 3200
FF_COL = 3 * DIFF_WIDTH + 3 * FOX_WIDTH
QK_WIDTH = 8 * LANES
TM = 512
TQ = 512
ATTN_TQ = 256
ATTN_ROWS = 64
LOG2E = 1.4426950408889634
MOE_BLOCK = 256
MOE_ROWS = N_TOK * TOP_K + N_EXPERTS * MOE_BLOCK
MOE_NBLOCKS = MOE_ROWS // MOE_BLOCK
N_PLANES = D_MODEL // 2 // LANES
SC_WINDOW = 128
NEG = -1e30
VMEM_LIMIT = 56 * 1024 * 1024

F32 = jnp.float32
BF16 = jnp.bfloat16


def _bf16_round(x):
    return x.astype(BF16).astype(F32)


def _lane_iota(shape):
    return lax.broadcasted_iota(jnp.int32, shape, 1)


def _params(*sem):
    return pltpu.CompilerParams(dimension_semantics=sem, vmem_limit_bytes=VMEM_LIMIT)


def _pack_planes(y, o_ref):
    bits = lax.bitcast_convert_type(_bf16_round(y), jnp.uint32)
    half = D_MODEL // 2
    word = bits[:, half:] | lax.shift_right_logical(bits[:, :half], jnp.uint32(16))
    word = lax.bitcast_convert_type(word, jnp.int32)
    for p in range(N_PLANES):
        o_ref[p] = word[:, p * LANES:(p + 1) * LANES]


def _unpack_planes(planes):
    lo, hi = [], []
    for w in planes:
        u = lax.bitcast_convert_type(w, jnp.uint32)
        lo.append(lax.bitcast_convert_type(lax.shift_left(u, jnp.uint32(16)), F32))
        hi.append(lax.bitcast_convert_type(u & jnp.uint32(0xFFFF0000), F32))
    return jnp.concatenate(lo + hi, axis=1)


def _combine(route_ref, y_ref):
    rt = route_ref[...]
    y0 = _unpack_planes([y_ref[0, p] for p in range(N_PLANES)])
    y1 = _unpack_planes([y_ref[1, p] for p in range(N_PLANES)])
    return rt[:, 2:3] * y0 + rt[:, 3:4] * y1


def _mod_kernel(c_ref, w_ref, b_ref, o_ref):
    c = c_ref[...]
    cond = c / (1.0 + jnp.exp(-c))
    ch = cond.astype(BF16)
    cl = (cond - ch.astype(F32)).astype(BF16)
    w = w_ref[...]
    wh = w.astype(BF16)
    wl = (w - wh.astype(F32)).astype(BF16)
    acc = jnp.dot(ch, wh, preferred_element_type=F32)
    acc += jnp.dot(cl, wh, preferred_element_type=F32)
    acc += jnp.dot(ch, wl, preferred_element_type=F32)
    o_ref[...] = acc + b_ref[...]


def _modulation(c, w_ada, b_ada):
    rows = 16
    tn = 1536
    c_pad = jnp.zeros((rows, D_MODEL), F32).at[:BATCH].set(c)
    out = pl.pallas_call(
        _mod_kernel,
        grid=(DEPTH, 6 * D_MODEL // tn),
        in_specs=[
            pl.BlockSpec((rows, D_MODEL), lambda l, n: (0, 0)),
            pl.BlockSpec((None, D_MODEL, tn), lambda l, n: (l, 0, n)),
            pl.BlockSpec((None, 1, tn), lambda l, n: (l, 0, n)),
        ],
        out_specs=pl.BlockSpec((None, rows, tn), lambda l, n: (l, 0, n)),
        out_shape=jax.ShapeDtypeStruct((DEPTH, rows, 6 * D_MODEL), F32),
        compiler_params=_params("arbitrary", "arbitrary"),
        name="adaln_mod",
    )(c_pad, w_ada, b_ada.reshape(DEPTH, 1, 6 * D_MODEL))
    return out[:, :BATCH]


def _rope_kernel(pos_ref, inv_ref, c_ref, sa_ref, sb_ref):
    ang = pos_ref[...].astype(F32) * inv_ref[...]
    j = _lane_iota(ang.shape) % HEAD_DIM
    cosv = jnp.cos(ang)
    sinv = jnp.sin(ang)
    half = ROT_DIM // 2
    c_ref[...] = jnp.where(j < ROT_DIM, cosv, 1.0)
    sa_ref[...] = jnp.where(j < half, -sinv, 0.0)
    sb_ref[...] = jnp.where((j >= half) & (j < ROT_DIM), sinv, 0.0)


def _rope_tables(positions):
    half = ROT_DIM // 2
    inv = ROPE_THETA ** (-jnp.arange(0, ROT_DIM, 2, dtype=F32) / ROT_DIM)
    lane = np.arange(LANES)
    inv_lane = inv[(lane % HEAD_DIM) % half].reshape(1, LANES)
    spec = pl.BlockSpec((TM, LANES), lambda i: (i, 0))
    shape = jax.ShapeDtypeStruct((N_TOK, LANES), F32)
    return pl.pallas_call(
        _rope_kernel,
        grid=(N_TOK // TM,),
        in_specs=[pl.BlockSpec((TM, 1), lambda i: (i, 0)),
                  pl.BlockSpec((1, LANES), lambda i: (0, 0))],
        out_specs=[spec, spec, spec],
        out_shape=[shape, shape, shape],
        compiler_params=_params("arbitrary"),
        name="rope_tables",
    )(positions.reshape(N_TOK, 1), inv_lane)


def _rms_mod(x, g, sc, sh):
    ms = jnp.mean(x * x, axis=-1, keepdims=True)
    return (x * lax.rsqrt(ms + EPS) * g) * (1.0 + sc) + sh


def _inproj_kernel(fuse, *refs):
    if fuse:
        (x_ref, rt_ref, y_ref, gt_ref, sc_ref, sh_ref, g_ref, w_ref, bf_ref, c_ref, sa_ref, sb_ref,
         pq_ref, xo_ref, dq_ref, dk_ref, dv_ref, fq_ref, fk_ref, fv_ref, carry_ref) = refs
        x = x_ref[...] + gt_ref[...] * _combine(rt_ref, y_ref)
        xo_ref[...] = x
    else:
        (x_ref, sc_ref, sh_ref, g_ref, w_ref, bf_ref, c_ref, sa_ref, sb_ref,
         pq_ref, dq_ref, dk_ref, dv_ref, fq_ref, fk_ref, fv_ref, carry_ref) = refs
        x = x_ref[...]
    hb = _rms_mod(x, g_ref[...], sc_ref[...], sh_ref[...]).astype(BF16)

    @pl.when(pl.program_id(0) % (SEQ // TM) == 0)
    def _():
        carry_ref[...] = jnp.zeros_like(carry_ref)

    lane = _lane_iota((TM, LANES))
    nh = N_FOX_HEADS

    def pack3(a):
        hi = _bf16_round(a)
        r1 = a - hi
        mid = _bf16_round(r1)
        lo = _bf16_round(r1 - mid)
        return jnp.where(lane < nh, hi,
                         jnp.where(lane < 2 * nh, pltpu.roll(mid, nh, 1),
                                   jnp.where(lane < 3 * nh, pltpu.roll(lo, 2 * nh, 1), 0.0)))

    z = jnp.dot(hb, w_ref[:, FF_COL:FF_COL + LANES], preferred_element_type=F32) + bf_ref[...]
    logf = jnp.minimum(z, 0.0) - jnp.log(1.0 + jnp.exp(-jnp.abs(z)))
    logf = jnp.where(lane < nh, logf, 0.0)
    row = lax.broadcasted_iota(jnp.int32, (TM, TM), 0)
    col = lax.broadcasted_iota(jnp.int32, (TM, TM), 1)
    tri = (row >= col).astype(BF16)
    r = jnp.dot(tri, pack3(logf).astype(BF16), preferred_element_type=F32)
    cs = r + pltpu.roll(r, LANES - nh, 1) + pltpu.roll(r, LANES - 2 * nh, 1)
    cf = jnp.where(lane < nh, cs + carry_ref[0:1, :], 0.0)
    carry_ref[...] = jnp.broadcast_to(cf[TM - 1:TM, :], carry_ref.shape)

    t3 = jnp.where(lane == 3 * nh, 1.0, pack3(cf * LOG2E)).astype(BF16)
    aug = jnp.dot(t3, pq_ref[...], preferred_element_type=F32)

    low = lane < HEAD_DIM
    rc, rsa, rsb = c_ref[...], sa_ref[...], sb_ref[...]
    scale = HEAD_DIM ** -0.5 * LOG2E

    def split_store(chunk, o_ref, m, extra_a=None, extra_b=None):
        a = jnp.where(low, chunk, 0.0)
        b = jnp.where(low, pltpu.roll(chunk, HEAD_DIM, 1), 0.0)
        if extra_a is not None:
            a = a + extra_a
            b = b + extra_b
        o_ref[:, (2 * m) * LANES:(2 * m + 1) * LANES] = a.astype(BF16)
        o_ref[:, (2 * m + 1) * LANES:(2 * m + 2) * LANES] = b.astype(BF16)

    def rope(xc):
        return xc * rc + pltpu.roll(xc, LANES - ROT_DIM // 2, 1) * rsa + pltpu.roll(xc, ROT_DIM // 2, 1) * rsb

    pdq = jnp.dot(hb, w_ref[:, 0:DIFF_WIDTH], preferred_element_type=F32)
    for m in range(N_DIFF_HEADS):
        split_store(rope(pdq[:, m * LANES:(m + 1) * LANES]) * scale, dq_ref, m)
    pdk = jnp.dot(hb, w_ref[:, DIFF_WIDTH:2 * DIFF_WIDTH], preferred_element_type=F32)
    for m in range(N_DIFF_HEADS):
        split_store(rope(pdk[:, m * LANES:(m + 1) * LANES]), dk_ref, m)
    def store_values_t(pv, o_ref):
        for m in range(4):
            o_ref[m] = pv[:, m * LANES:(m + 1) * LANES].T.astype(BF16)

    store_values_t(jnp.dot(hb, w_ref[:, 2 * DIFF_WIDTH:3 * DIFF_WIDTH], preferred_element_type=F32),
                   dv_ref)
    o = 3 * DIFF_WIDTH
    pfq = jnp.dot(hb, w_ref[:, o:o + FOX_WIDTH], preferred_element_type=F32)
    for m in range(N_FOX_HEADS // 2):
        split_store(pfq[:, m * LANES:(m + 1) * LANES] * scale, fq_ref, m,
                    aug[:, (2 * m) * LANES:(2 * m + 1) * LANES],
                    aug[:, (2 * m + 1) * LANES:(2 * m + 2) * LANES])
    pfk = jnp.dot(hb, w_ref[:, o + FOX_WIDTH:o + 2 * FOX_WIDTH], preferred_element_type=F32)
    for m in range(N_FOX_HEADS // 2):
        split_store(pfk[:, m * LANES:(m + 1) * LANES], fk_ref, m,
                    aug[:, QK_WIDTH + (2 * m) * LANES:QK_WIDTH + (2 * m + 1) * LANES],
                    aug[:, QK_WIDTH + (2 * m + 1) * LANES:QK_WIDTH + (2 * m + 2) * LANES])
    store_values_t(jnp.dot(hb, w_ref[:, o + 2 * FOX_WIDTH:o + 3 * FOX_WIDTH],
                           preferred_element_type=F32), fv_ref)


def _forget_placement():
    nh = N_FOX_HEADS
    p = np.zeros((LANES, 2 * QK_WIDTH), np.float32)
    for h in range(nh):
        base_q = h * LANES + HEAD_DIM
        base_k = QK_WIDTH + h * LANES + HEAD_DIM
        for part in range(3):
            p[part * nh + h, base_q + part] = 1.0
            p[3 * nh, base_q + 3 + part] = 1.0
            p[3 * nh, base_k + part] = 1.0
            p[part * nh + h, base_k + 3 + part] = -1.0
    return jnp.asarray(p, BF16)


def _inproj(x, moe, gate, sc, sh, g, w_bf, b_forget, tables, pq):
    fuse = moe is not None
    tpb = SEQ // TM
    row = pl.BlockSpec((TM, D_MODEL), lambda i: (i, 0))
    per_batch = pl.BlockSpec((None, 1, D_MODEL), lambda i: (i // tpb, 0, 0))
    const = lambda shape: pl.BlockSpec(shape, lambda i: (0,) * len(shape))
    tab = pl.BlockSpec((TM, LANES), lambda i: (i, 0))
    in_specs = [row]
    args = [x]
    if fuse:
        in_specs += [tab, pl.BlockSpec((TOP_K, N_PLANES, TM, LANES), lambda i: (0, 0, i, 0)), per_batch]
        args += [moe[0], moe[1], gate]
    in_specs += [per_batch, per_batch, const((1, D_MODEL)), const((D_MODEL, IN_COLS_PAD)),
                 const((1, LANES)), tab, tab, tab, const((LANES, 2 * QK_WIDTH))]
    args += [sc, sh, g.reshape(1, D_MODEL), w_bf, b_forget, *tables, pq]
    wide = pl.BlockSpec((TM, QK_WIDTH), lambda i: (i, 0))
    half = pl.BlockSpec((None, 4, None, LANES, TM), lambda i: (i // tpb, 0, i % tpb, 0, 0))
    wide_s = jax.ShapeDtypeStruct((N_TOK, QK_WIDTH), BF16)
    half_s = jax.ShapeDtypeStruct((BATCH, 4, tpb, LANES, TM), BF16)
    out_specs = [wide, wide, half, wide, wide, half]
    out_shape = [wide_s, wide_s, half_s, wide_s, wide_s, half_s]
    if fuse:
        out_specs = [row] + out_specs
        out_shape = [jax.ShapeDtypeStruct((N_TOK, D_MODEL), F32)] + out_shape
    outs = pl.pallas_call(
        functools.partial(_inproj_kernel, fuse),
        grid=(N_TOK // TM,),
        in_specs=in_specs,
        out_specs=out_specs,
        out_shape=out_shape,
        scratch_shapes=[pltpu.VMEM((8, LANES), F32)],
        compiler_params=_params("arbitrary"),
        name="norm_inproj",
    )(*args)
    if fuse:
        return outs[0], outs[1:]
    return x, outs


def _attn_kernel(diff, lambda_init, qa_ref, qb_ref, ka_ref, kb_ref, v_ref, g_ref, lam_ref, o_ref,
                 *scratch):
    qi = pl.program_id(2)
    n_half = TQ // ATTN_TQ
    chains = []
    for mi, (q_ref, k_ref) in enumerate(((qa_ref, ka_ref), (qb_ref, kb_ref))):
        qt = q_ref[...].astype(F32).T.astype(BF16)
        for h in range(n_half):
            c = mi * n_half + h
            qt_sc, s_sc, p_sc, m_sc, l_sc, acc_sc = scratch[c::2 * n_half]
            qt_sc[...] = qt[:, h * ATTN_TQ:(h + 1) * ATTN_TQ]
            m_sc[...] = jnp.full(m_sc.shape, NEG, F32)
            l_sc[...] = jnp.zeros(l_sc.shape, F32)
            acc_sc[...] = jnp.zeros(acc_sc.shape, F32)
            chains.append((h, k_ref, qt_sc, s_sc, p_sc, m_sc, l_sc, acc_sc))

    def scores(chain, j, masked):
        h, k_ref, qt_sc, s_sc = chain[:4]
        off = pl.multiple_of(j * TQ, TQ)
        s = jnp.dot(k_ref[pl.ds(off, TQ), :], qt_sc[...], preferred_element_type=F32)
        if masked:
            kk = lax.broadcasted_iota(jnp.int32, (TQ, ATTN_TQ), 0)
            qq = h * ATTN_TQ + lax.broadcasted_iota(jnp.int32, (TQ, ATTN_TQ), 1)
            s = jnp.where((kk // CHUNK <= qq // CHUNK) if diff else (kk <= qq), s, NEG)
        s_sc[...] = s

    def softmax(chain):
        s_sc, p_sc, m_sc, l_sc = chain[3:7]
        m_all = m_sc[...]
        m_parts, sum_parts = [], []
        for c0 in range(0, ATTN_TQ, LANES):
            cols = slice(c0, c0 + LANES)
            pm = s_sc[0:ATTN_ROWS, cols]
            for r0 in range(ATTN_ROWS, TQ, ATTN_ROWS):
                pm = jnp.maximum(pm, s_sc[r0:r0 + ATTN_ROWS, cols])
            m_new = jnp.maximum(m_all[:, cols], jnp.max(pm, axis=0, keepdims=True))
            ps = jnp.zeros((ATTN_ROWS, LANES), F32)
            for r0 in range(0, TQ, ATTN_ROWS):
                p = jnp.exp2(s_sc[r0:r0 + ATTN_ROWS, cols] - m_new)
                p_sc[r0:r0 + ATTN_ROWS, cols] = p.astype(BF16)
                ps = ps + p
            m_parts.append(m_new)
            sum_parts.append(jnp.sum(ps, axis=0, keepdims=True))
        m_new = jnp.concatenate(m_parts, axis=1)
        alpha = jnp.exp2(m_all - m_new)
        m_sc[...] = m_new
        l_sc[...] = alpha * l_sc[...] + jnp.concatenate(sum_parts, axis=1)
        return alpha

    def values(chain, j, alpha):
        p_sc, acc_sc = chain[4], chain[7]
        pv = jnp.dot(v_ref[j], p_sc[...], preferred_element_type=F32)
        acc_sc[...] = alpha * acc_sc[...] + pv

    def all_scores(j, masked):
        for chain in chains:
            scores(chain, j, masked)

    def consume(j, nxt=None, nxt_masked=False):
        for chain in chains:
            alpha = softmax(chain)
            if nxt is not None:
                scores(chain, nxt, nxt_masked)
            values(chain, j, alpha)

    @pl.when(qi == 0)
    def _():
        all_scores(0, True)
        consume(0)

    @pl.when(qi > 0)
    def _():
        all_scores(0, False)

        @pl.loop(0, qi - 1)
        def _(j):
            consume(j, j + 1)

        consume(qi - 1, qi, True)
        consume(qi)

    ot = [jnp.concatenate([chains[mi * n_half + h][7][...] / chains[mi * n_half + h][6][...]
                           for h in range(n_half)], axis=1) for mi in range(2)]
    g = g_ref[...]
    if diff:
        lv = lam_ref[...]
        lam = (jnp.exp(jnp.sum(lv[0:1] * lv[1:2], axis=1, keepdims=True))
               - jnp.exp(jnp.sum(lv[2:3] * lv[3:4], axis=1, keepdims=True)) + lambda_init)
        o = (ot[0] - lam * ot[1]).T
        y = o * lax.rsqrt(jnp.mean(o * o, axis=1, keepdims=True) + EPS) * g
        o_ref[...] = (y * (1.0 - lambda_init)).astype(o_ref.dtype)
    else:
        o = jnp.concatenate([ot[0][:HEAD_DIM], ot[1][HEAD_DIM:]], axis=0).T
        low = _lane_iota((TQ, LANES)) < HEAD_DIM
        sq = o * o
        msa = jnp.sum(jnp.where(low, sq, 0.0), axis=1, keepdims=True) / HEAD_DIM
        msb = jnp.sum(jnp.where(low, 0.0, sq), axis=1, keepdims=True) / HEAD_DIM
        inv = jnp.where(low, lax.rsqrt(msa + EPS), lax.rsqrt(msb + EPS))
        o_ref[...] = (o * inv * g).astype(o_ref.dtype)


def _attention(diff, lambda_init, q, k, v, g, lamv):
    nq = SEQ // TQ
    qspec = lambda par: pl.BlockSpec((TQ, LANES), lambda b, p, i: (b * nq + i, 2 * p + par))
    kspec = lambda par: pl.BlockSpec((SEQ, LANES), lambda b, p, i: (b, 2 * p + par))
    return pl.pallas_call(
        functools.partial(_attn_kernel, diff, lambda_init),
        grid=(BATCH, 4, nq),
        in_specs=[qspec(0), qspec(1), kspec(0), kspec(1),
                  pl.BlockSpec((None, None, nq, LANES, TQ), lambda b, p, i: (b, p, 0, 0, 0)),
                  pl.BlockSpec((1, LANES), lambda b, p, i: (0, 0)),
                  pl.BlockSpec((8, LANES), lambda b, p, i: (0, 0))],
        out_specs=pl.BlockSpec((TQ, LANES), lambda b, p, i: (b * nq + i, p)),
        out_shape=jax.ShapeDtypeStruct((N_TOK, DIFF_WIDTH), BF16),
        scratch_shapes=[pltpu.VMEM(shape, dt)
                        for shape, dt in (((LANES, ATTN_TQ), BF16), ((TQ, ATTN_TQ), F32),
                                          ((TQ, ATTN_TQ), BF16), ((1, ATTN_TQ), F32),
                                          ((1, ATTN_TQ), F32), ((LANES, ATTN_TQ), F32))
                        for _ in range(2 * TQ // ATTN_TQ)],
        compiler_params=_params("arbitrary", "arbitrary", "arbitrary"),
        name="diff_attention" if diff else "fox_attention",
    )(q, q, k, k, v, g, lamv)


def _outproj_kernel(x_ref, od_ref, of_ref, gt_ref, sc_ref, sh_ref, g_ref, wo_ref, wr_ref, br_ref,
                    x1_ref, h2_ref, rt_ref, cnt_ref, carry_ref):
    @pl.when(pl.program_id(0) == 0)
    def _():
        carry_ref[...] = jnp.zeros_like(carry_ref)

    mix = jnp.dot(od_ref[...], wo_ref[0:DIFF_WIDTH, :], preferred_element_type=F32)
    mix += jnp.dot(of_ref[...], wo_ref[DIFF_WIDTH:, :], preferred_element_type=F32)
    x1 = x_ref[...] + gt_ref[...] * mix
    x1_ref[...] = x1
    h = _rms_mod(x1, g_ref[...], sc_ref[...], sh_ref[...])
    hh = h.astype(BF16)
    _pack_planes(h, h2_ref)
    hl = (h - hh.astype(F32)).astype(BF16)
    r1 = jnp.dot(hh, wr_ref[...], preferred_element_type=F32)
    r2 = jnp.dot(hl, wr_ref[:, 0:LANES], preferred_element_type=F32)
    logits = r1[:, 0:LANES] + r1[:, LANES:] + r2 + br_ref[...]

    lane = _lane_iota((TM, LANES))
    lanef = lane.astype(F32)
    big = float(LANES)
    isg = lane < N_GROUPS
    lg = jnp.where(isg, logits, NEG)
    mg = jnp.max(lg, axis=1, keepdims=True)
    sg = jnp.sum(jnp.where(isg, jnp.exp(lg - mg), 0.0), axis=1, keepdims=True)
    p_g = 1.0 / sg
    gsel = jnp.min(jnp.where(isg & (lg == mg), lanef, big), axis=1, keepdims=True)
    lo = N_GROUPS + gsel * EXPERTS_PER_GROUP
    ise = (lanef >= lo) & (lanef < lo + EXPERTS_PER_GROUP)
    le = jnp.where(ise, logits, NEG)
    t1 = jnp.max(le, axis=1, keepdims=True)
    i1 = jnp.min(jnp.where(ise & (le == t1), lanef, big), axis=1, keepdims=True)
    ise2 = ise & (lanef != i1)
    le2 = jnp.where(ise2, logits, NEG)
    t2 = jnp.max(le2, axis=1, keepdims=True)
    i2 = jnp.min(jnp.where(ise2 & (le2 == t2), lanef, big), axis=1, keepdims=True)
    d = jnp.exp(t2 - t1)
    w1 = p_g / (1.0 + d)
    w2 = p_g * d / (1.0 + d)
    e1 = i1 - N_GROUPS
    e2 = i2 - N_GROUPS
    oh1 = lanef == e1
    oh2 = lanef == e2
    both = jnp.where(oh1 | oh2, 1.0, 0.0)
    row = lax.broadcasted_iota(jnp.int32, (TM, TM), 0)
    col = lax.broadcasted_iota(jnp.int32, (TM, TM), 1)
    before = jnp.dot((row > col).astype(BF16), both.astype(BF16), preferred_element_type=F32)
    before = before + carry_ref[0:1, :]
    rank1 = jnp.sum(jnp.where(oh1, before, 0.0), axis=1, keepdims=True)
    rank2 = jnp.sum(jnp.where(oh2, before, 0.0), axis=1, keepdims=True)
    total = carry_ref[0:1, :] + jnp.sum(both, axis=0, keepdims=True)
    carry_ref[...] = jnp.broadcast_to(total, carry_ref.shape)
    cnt_ref[...] = jnp.broadcast_to(total, cnt_ref.shape)
    vals = (e1, e2, w1, w2, rank1, rank2)
    out = jnp.zeros((TM, LANES), F32)
    for j, v in enumerate(vals):
        out = jnp.where(lane == j, v, out)
    rt_ref[...] = out


def _outproj(x, od, of, gt, sc, sh, g, wo_bf, wr, br):
    tpb = SEQ // TM
    row = pl.BlockSpec((TM, D_MODEL), lambda i: (i, 0))
    half = pl.BlockSpec((TM, DIFF_WIDTH), lambda i: (i, 0))
    per_batch = pl.BlockSpec((None, 1, D_MODEL), lambda i: (i // tpb, 0, 0))
    const = lambda shape: pl.BlockSpec(shape, lambda i: (0,) * len(shape))
    return pl.pallas_call(
        _outproj_kernel,
        grid=(N_TOK // TM,),
        in_specs=[row, half, half, per_batch, per_batch, per_batch, const((1, D_MODEL)),
                  const((D_MODEL, D_MODEL)), const((D_MODEL, 2 * LANES)), const((1, LANES))],
        out_specs=[row, pl.BlockSpec((N_PLANES, TM, LANES), lambda i: (0, i, 0)),
                   pl.BlockSpec((TM, LANES), lambda i: (i, 0)), const((8, LANES))],
        out_shape=[jax.ShapeDtypeStruct((N_TOK, D_MODEL), F32),
                   jax.ShapeDtypeStruct((N_PLANES, N_TOK, LANES), jnp.int32),
                   jax.ShapeDtypeStruct((N_TOK, LANES), F32),
                   jax.ShapeDtypeStruct((8, LANES), F32)],
        scratch_shapes=[pltpu.VMEM((8, LANES), F32)],
        compiler_params=_params("arbitrary"),
        name="outproj_router",
    )(x, od, of, gt, sc, sh, g.reshape(1, D_MODEL), wo_bf, wr, br)


def _expert_kernel(be_ref, cnt_ref, xs_ref, wg_ref, wu_ref, wd_ref, ys_ref):
    i = pl.program_id(0)
    cnt = cnt_ref[i]

    @pl.when(cnt > 0)
    def _():
        live = lax.broadcasted_iota(jnp.int32, (MOE_BLOCK, LANES), 0) < cnt
        xb = _unpack_planes([jnp.where(live, xs_ref[p], 0) for p in range(N_PLANES)]).astype(BF16)
        a = jnp.dot(xb, wg_ref[...], preferred_element_type=F32)
        u = jnp.dot(xb, wu_ref[...], preferred_element_type=F32)
        hid = (a / (1.0 + jnp.exp(-a)) * u).astype(BF16)
        _pack_planes(jnp.dot(hid, wd_ref[...], preferred_element_type=F32), ys_ref)

    @pl.when(cnt == 0)
    def _():
        ys_ref[...] = jnp.zeros_like(ys_ref)


def _experts(block_expert, block_count, xs, wg, wu, wd):
    planes = pl.BlockSpec((N_PLANES, MOE_BLOCK, LANES), lambda i, be, bc: (0, i, 0))
    grid_spec = pltpu.PrefetchScalarGridSpec(
        num_scalar_prefetch=2,
        grid=(MOE_NBLOCKS,),
        in_specs=[planes,
                  pl.BlockSpec((None, D_MODEL, D_EXPERT), lambda i, be, bc: (be[i], 0, 0)),
                  pl.BlockSpec((None, D_MODEL, D_EXPERT), lambda i, be, bc: (be[i], 0, 0)),
                  pl.BlockSpec((None, D_EXPERT, D_MODEL), lambda i, be, bc: (be[i], 0, 0))],
        out_specs=planes,
    )
    return pl.pallas_call(
        _expert_kernel,
        grid_spec=grid_spec,
        out_shape=jax.ShapeDtypeStruct((N_PLANES, MOE_ROWS, LANES), jnp.int32),
        compiler_params=_params("arbitrary"),
        name="expert_mlp",
    )(block_expert, block_count, xs, wg, wu, wd)


def _slots(route, counts):
    counts = counts[0, :N_EXPERTS].astype(jnp.int32)
    padded = ((counts + MOE_BLOCK - 1) // MOE_BLOCK) * MOE_BLOCK
    pend = jnp.cumsum(padded)
    pstart = pend - padded
    e = route[:, 0:TOP_K].astype(jnp.int32)
    rank = route[:, 2 * TOP_K:3 * TOP_K].astype(jnp.int32)
    onehot = e[:, :, None] == jnp.arange(N_EXPERTS, dtype=jnp.int32)
    dest = jnp.sum(jnp.where(onehot, pstart, 0), axis=-1) + rank
    bstart = jnp.arange(MOE_NBLOCKS, dtype=jnp.int32) * MOE_BLOCK
    block_expert = jnp.minimum(jnp.sum(bstart[:, None] >= pend[None, :], axis=1), N_EXPERTS - 1)
    block_expert = block_expert.astype(jnp.int32)
    block_count = jnp.clip(counts[block_expert] - (bstart - pstart[block_expert]), 0, MOE_BLOCK)
    block_count = jnp.where(bstart < pend[-1], block_count, 0).astype(jnp.int32)
    plane_off = jnp.arange(N_PLANES, dtype=jnp.int32) * MOE_ROWS
    rows = dest.T[:, None, :] + plane_off[None, :, None]
    return rows.astype(jnp.int32), block_expert, block_count


def _sc_workers():
    info = plsc.get_sparse_core_info()
    return info.num_cores, info.num_cores * info.num_subcores


def _sc_scatter2(src, idx_a, idx_b, out_rows):
    n_src = src.shape[0]
    nc, nw = _sc_workers()
    per_w = n_src // nw
    steps = per_w // SC_WINDOW
    mesh = plsc.VectorSubcoreMesh(core_axis_name="c", subcore_axis_name="s")

    @functools.partial(
        pl.kernel, mesh=mesh,
        out_type=jax.ShapeDtypeStruct((out_rows, LANES), src.dtype),
        scratch_types=[pltpu.VMEM((SC_WINDOW,), jnp.int32), pltpu.VMEM((SC_WINDOW,), jnp.int32),
                       pltpu.VMEM((SC_WINDOW, LANES), src.dtype)],
        name="sc_dispatch_scatter",
    )
    def k(src_hbm, ia_hbm, ib_hbm, out_hbm, ia_v, ib_v, rows_v):
        base = (lax.axis_index("s") * nc + lax.axis_index("c")) * per_w

        @pl.loop(0, steps)
        def _(j):
            off = base + j * SC_WINDOW
            pltpu.sync_copy(ia_hbm.at[pl.ds(off, SC_WINDOW)], ia_v)
            pltpu.sync_copy(ib_hbm.at[pl.ds(off, SC_WINDOW)], ib_v)
            pltpu.sync_copy(src_hbm.at[pl.ds(off, SC_WINDOW)], rows_v)
            pltpu.sync_copy(rows_v, out_hbm.at[ia_v])
            pltpu.sync_copy(rows_v, out_hbm.at[ib_v])

    return k(src, idx_a, idx_b)


def _sc_gather(table, idx):
    n_out = idx.shape[0]
    nc, nw = _sc_workers()
    per_w = n_out // nw
    steps = per_w // SC_WINDOW
    mesh = plsc.VectorSubcoreMesh(core_axis_name="c", subcore_axis_name="s")

    @functools.partial(
        pl.kernel, mesh=mesh,
        out_type=jax.ShapeDtypeStruct((n_out, LANES), table.dtype),
        scratch_types=[pltpu.VMEM((SC_WINDOW,), jnp.int32), pltpu.VMEM((SC_WINDOW, LANES), table.dtype)],
        name="sc_combine_gather",
    )
    def k(table_hbm, idx_hbm, out_hbm, idx_v, rows_v):
        base = (lax.axis_index("s") * nc + lax.axis_index("c")) * per_w

        @pl.loop(0, steps)
        def _(j):
            off = base + j * SC_WINDOW
            pltpu.sync_copy(idx_hbm.at[pl.ds(off, SC_WINDOW)], idx_v)
            pltpu.sync_copy(table_hbm.at[idx_v], rows_v)
            pltpu.sync_copy(rows_v, out_hbm.at[pl.ds(off, SC_WINDOW)])

    return k(table, idx)


def _final_kernel(x_ref, rt_ref, y_ref, gt_ref, g_ref, o_ref):
    x = x_ref[...] + gt_ref[...] * _combine(rt_ref, y_ref)
    ms = jnp.mean(x * x, axis=-1, keepdims=True)
    o_ref[...] = x * lax.rsqrt(ms + EPS) * g_ref[...]


def _final(x, moe, gate, g):
    tpb = SEQ // TM
    row = pl.BlockSpec((TM, D_MODEL), lambda i: (i, 0))
    return pl.pallas_call(
        _final_kernel,
        grid=(N_TOK // TM,),
        in_specs=[row, pl.BlockSpec((TM, LANES), lambda i: (i, 0)),
                  pl.BlockSpec((TOP_K, N_PLANES, TM, LANES), lambda i: (0, 0, i, 0)),
                  pl.BlockSpec((None, 1, D_MODEL), lambda i: (i // tpb, 0, 0)),
                  pl.BlockSpec((1, D_MODEL), lambda i: (0, 0))],
        out_specs=row,
        out_shape=jax.ShapeDtypeStruct((N_TOK, D_MODEL), F32),
        compiler_params=_params("arbitrary"),
        name="final_norm",
    )(x, moe[0], moe[1], gate, g.reshape(1, D_MODEL))


def kernel(x, c, positions, w_ada, b_ada, g_mix, w_in, b_forget, lambda_q1, lambda_k1, lambda_q2,
           lambda_k2, g_subln, g_fox_out, w_out, g_ffn, w_router_group, b_router_group,
           w_router_expert, b_router_expert, w_expert_gate, w_expert_up, w_expert_down, g_final):
    mod = _modulation(c, w_ada, b_ada)
    mod = mod.reshape(DEPTH, BATCH, 6, 1, D_MODEL)
    tables = _rope_tables(positions)
    pq = _forget_placement()
    xf = x.reshape(N_TOK, D_MODEL)
    moe = None
    gate = None
    for l in range(DEPTH):
        sh1, sc1, gt1, sh2, sc2, gt2 = (mod[l, :, j] for j in range(6))
        w_bf = jnp.pad(w_in[l], ((0, 0), (0, IN_COLS_PAD - IN_COLS))).astype(BF16)
        bfp = jnp.pad(b_forget[l], (0, LANES - N_FOX_HEADS)).reshape(1, LANES)
        xf, (dq, dk, dv, fq, fk, fv) = _inproj(xf, moe, gate, sc1, sh1, g_mix[l], w_bf, bfp, tables, pq)

        lambda_init = 0.8 - 0.6 * float(np.exp(-0.3 * l))
        lamv = jnp.zeros((8, LANES), F32).at[0:4, 0:HEAD_DIM].set(
            jnp.stack([lambda_q1[l], lambda_k1[l], lambda_q2[l], lambda_k2[l]]))
        g_d = g_subln[l].reshape(1, LANES)
        g_f = jnp.concatenate([g_fox_out[l], g_fox_out[l]]).reshape(1, LANES)
        od = _attention(True, lambda_init, dq, dk, dv, g_d, lamv)
        of = _attention(False, lambda_init, fq, fk, fv, g_f, lamv)

        wr32 = jnp.pad(jnp.concatenate([w_router_group[l], w_router_expert[l]], axis=1),
                       ((0, 0), (0, LANES - N_GROUPS - N_EXPERTS)))
        wr_hi = wr32.astype(BF16)
        wr_lo = (wr32 - wr_hi.astype(F32)).astype(BF16)
        wr = jnp.concatenate([wr_hi, wr_lo], axis=1)
        br = jnp.pad(jnp.concatenate([b_router_group[l], b_router_expert[l]]),
                     (0, LANES - N_GROUPS - N_EXPERTS)).reshape(1, LANES)
        xf, h2, route, counts = _outproj(xf, od, of, gt1, sc2, sh2, g_ffn[l], w_out[l].astype(BF16),
                                         wr, br)

        rows, block_expert, block_count = _slots(route, counts)
        xs = _sc_scatter2(h2.reshape(N_PLANES * N_TOK, LANES), rows[0].reshape(-1),
                          rows[1].reshape(-1), N_PLANES * MOE_ROWS)
        ys = _experts(block_expert, block_count, xs.reshape(N_PLANES, MOE_ROWS, LANES),
                      w_expert_gate[l].astype(BF16), w_expert_up[l].astype(BF16),
                      w_expert_down[l].astype(BF16))
        y2 = _sc_gather(ys.reshape(N_PLANES * MOE_ROWS, LANES), rows.reshape(-1))
        moe = (route, y2.reshape(TOP_K, N_PLANES, N_TOK, LANES))
        gate = gt2
    out = _final(xf, moe, gate, g_final)
    return out.reshape(BATCH, SEQ, D_MODEL)
```

```python
import functools

import numpy as np
import jax
import jax.numpy as jnp
from jax import lax
from jax.experimental import pallas as pl
from jax.experimental.pallas import tpu as pltpu
from jax.experimental.pallas import tpu_sc as plsc

D_MODEL = 1024
BATCH = 4
SEQ = 4096
DEPTH = 4
N_TOK = BATCH * SEQ

CHUNK = 64
HEAD_DIM = 64
N_DIFF_HEADS = 4
N_FOX_HEADS = 8
DIFF_WIDTH = 512
FOX_WIDTH = 512
IN_COLS = 3 * DIFF_WIDTH + 3 * FOX_WIDTH + N_FOX_HEADS
ROT_DIM = 16
ROPE_THETA = 500000.0
N_GROUPS = 4
EXPERTS_PER_GROUP = 8
N_EXPERTS = 32
TOP_K = 2
D_EXPERT = 512
EPS = 1e-6

LANES = 128
IN_COLS_PAD = 3200
FF_COL = 3 * DIFF_WIDTH + 3 * FOX_WIDTH
QK_WIDTH = 8 * LANES
TM = 512
TQ = 512
ATTN_TQ = 256
ATTN_ROWS = 64
LOG2E = 1.4426950408889634
MOE_BLOCK = 256
MOE_ROWS = N_TOK * TOP_K + N_EXPERTS * MOE_BLOCK
MOE_NBLOCKS = MOE_ROWS // MOE_BLOCK
N_PLANES = D_MODEL // 2 // LANES
SC_WINDOW = 128
NEG = -1e30
VMEM_LIMIT = 56 * 1024 * 1024

F32 = jnp.float32
BF16 = jnp.bfloat16


def _bf16_round(x):
    return x.astype(BF16).astype(F32)


def _lane_iota(shape):
    return lax.broadcasted_iota(jnp.int32, shape, 1)


def _params(*sem):
    return pltpu.CompilerParams(dimension_semantics=sem, vmem_limit_bytes=VMEM_LIMIT)


def _pack_planes(y, o_ref):
    bits = lax.bitcast_convert_type(_bf16_round(y), jnp.uint32)
    half = D_MODEL // 2
    word = bits[:, half:] | lax.shift_right_logical(bits[:, :half], jnp.uint32(16))
    word = lax.bitcast_convert_type(word, jnp.int32)
    for p in range(N_PLANES):
        o_ref[p] = word[:, p * LANES:(p + 1) * LANES]


def _unpack_planes(planes):
    lo, hi = [], []
    for w in planes:
        u = lax.bitcast_convert_type(w, jnp.uint32)
        lo.append(lax.bitcast_convert_type(lax.shift_left(u, jnp.uint32(16)), F32))
        hi.append(lax.bitcast_convert_type(u & jnp.uint32(0xFFFF0000), F32))
    return jnp.concatenate(lo + hi, axis=1)


def _combine(route_ref, y_ref):
    rt = route_ref[...]
    y0 = _unpack_planes([y_ref[0, p] for p in range(N_PLANES)])
    y1 = _unpack_planes([y_ref[1, p] for p in range(N_PLANES)])
    return rt[:, 2:3] * y0 + rt[:, 3:4] * y1


def _mod_kernel(c_ref, w_ref, b_ref, o_ref):
    c = c_ref[...]
    cond = c / (1.0 + jnp.exp(-c))
    ch = cond.astype(BF16)
    cl = (cond - ch.astype(F32)).astype(BF16)
    w = w_ref[...]
    wh = w.astype(BF16)
    wl = (w - wh.astype(F32)).astype(BF16)
    acc = jnp.dot(ch, wh, preferred_element_type=F32)
    acc += jnp.dot(cl, wh, preferred_element_type=F32)
    acc += jnp.dot(ch, wl, preferred_element_type=F32)
    o_ref[...] = acc + b_ref[...]


def _modulation(c, w_ada, b_ada):
    rows = 16
    tn = 1536
    c_pad = jnp.zeros((rows, D_MODEL), F32).at[:BATCH].set(c)
    out = pl.pallas_call(
        _mod_kernel,
        grid=(DEPTH, 6 * D_MODEL // tn),
        in_specs=[
            pl.BlockSpec((rows, D_MODEL), lambda l, n: (0, 0)),
            pl.BlockSpec((None, D_MODEL, tn), lambda l, n: (l, 0, n)),
            pl.BlockSpec((None, 1, tn), lambda l, n: (l, 0, n)),
        ],
        out_specs=pl.BlockSpec((None, rows, tn), lambda l, n: (l, 0, n)),
        out_shape=jax.ShapeDtypeStruct((DEPTH, rows, 6 * D_MODEL), F32),
        compiler_params=_params("arbitrary", "arbitrary"),
        name="adaln_mod",
    )(c_pad, w_ada, b_ada.reshape(DEPTH, 1, 6 * D_MODEL))
    return out[:, :BATCH]


def _rope_kernel(pos_ref, inv_ref, c_ref, sa_ref, sb_ref):
    ang = pos_ref[...].astype(F32) * inv_ref[...]
    j = _lane_iota(ang.shape) % HEAD_DIM
    cosv = jnp.cos(ang)
    sinv = jnp.sin(ang)
    half = ROT_DIM // 2
    c_ref[...] = jnp.where(j < ROT_DIM, cosv, 1.0)
    sa_ref[...] = jnp.where(j < half, -sinv, 0.0)
    sb_ref[...] = jnp.where((j >= half) & (j < ROT_DIM), sinv, 0.0)


def _rope_tables(positions):
    half = ROT_DIM // 2
    inv = ROPE_THETA ** (-jnp.arange(0, ROT_DIM, 2, dtype=F32) / ROT_DIM)
    lane = np.arange(LANES)
    inv_lane = inv[(lane % HEAD_DIM) % half].reshape(1, LANES)
    spec = pl.BlockSpec((TM, LANES), lambda i: (i, 0))
    shape = jax.ShapeDtypeStruct((N_TOK, LANES), F32)
    return pl.pallas_call(
        _rope_kernel,
        grid=(N_TOK // TM,),
        in_specs=[pl.BlockSpec((TM, 1), lambda i: (i, 0)),
                  pl.BlockSpec((1, LANES), lambda i: (0, 0))],
        out_specs=[spec, spec, spec],
        out_shape=[shape, shape, shape],
        compiler_params=_params("arbitrary"),
        name="rope_tables",
    )(positions.reshape(N_TOK, 1), inv_lane)


def _rms_mod(x, g, sc, sh):
    ms = jnp.mean(x * x, axis=-1, keepdims=True)
    return (x * lax.rsqrt(ms + EPS) * g) * (1.0 + sc) + sh


def _inproj_kernel(fuse, *refs):
    if fuse:
        (x_ref, rt_ref, y_ref, gt_ref, sc_ref, sh_ref, g_ref, w_ref, bf_ref, c_ref, sa_ref, sb_ref,
         pq_ref, xo_ref, dq_ref, dk_ref, dv_ref, fq_ref, fk_ref, fv_ref, carry_ref) = refs
        x = x_ref[...] + gt_ref[...] * _combine(rt_ref, y_ref)
        xo_ref[...] = x
    else:
        (x_ref, sc_ref, sh_ref, g_ref, w_ref, bf_ref, c_ref, sa_ref, sb_ref,
         pq_ref, dq_ref, dk_ref, dv_ref, fq_ref, fk_ref, fv_ref, carry_ref) = refs
        x = x_ref[...]
    hb = _rms_mod(x, g_ref[...], sc_ref[...], sh_ref[...]).astype(BF16)

    @pl.when(pl.program_id(0) % (SEQ // TM) == 0)
    def _():
        carry_ref[...] = jnp.zeros_like(carry_ref)

    lane = _lane_iota((TM, LANES))
    nh = N_FOX_HEADS

    def pack3(a):
        hi = _bf16_round(a)
        r1 = a - hi
        mid = _bf16_round(r1)
        lo = _bf16_round(r1 - mid)
        return jnp.where(lane < nh, hi,
                         jnp.where(lane < 2 * nh, pltpu.roll(mid, nh, 1),
                                   jnp.where(lane < 3 * nh, pltpu.roll(lo, 2 * nh, 1), 0.0)))

    z = jnp.dot(hb, w_ref[:, FF_COL:FF_COL + LANES], preferred_element_type=F32) + bf_ref[...]
    logf = jnp.minimum(z, 0.0) - jnp.log(1.0 + jnp.exp(-jnp.abs(z)))
    logf = jnp.where(lane < nh, logf, 0.0)
    row = lax.broadcasted_iota(jnp.int32, (TM, TM), 0)
    col = lax.broadcasted_iota(jnp.int32, (TM, TM), 1)
    tri = (row >= col).astype(BF16)
    r = jnp.dot(tri, pack3(logf).astype(BF16), preferred_element_type=F32)
    cs = r + pltpu.roll(r, LANES - nh, 1) + pltpu.roll(r, LANES - 2 * nh, 1)
    cf = jnp.where(lane < nh, cs + carry_ref[0:1, :], 0.0)
    carry_ref[...] = jnp.broadcast_to(cf[TM - 1:TM, :], carry_ref.shape)

    t3 = jnp.where(lane == 3 * nh, 1.0, pack3(cf * LOG2E)).astype(BF16)
    aug = jnp.dot(t3, pq_ref[...], preferred_element_type=F32)

    low = lane < HEAD_DIM
    rc, rsa, rsb = c_ref[...], sa_ref[...], sb_ref[...]
    scale = HEAD_DIM ** -0.5 * LOG2E

    def split_store(chunk, o_ref, m, extra_a=None, extra_b=None):
        a = jnp.where(low, chunk, 0.0)
        b = jnp.where(low, pltpu.roll(chunk, HEAD_DIM, 1), 0.0)
        if extra_a is not None:
            a = a + extra_a
            b = b + extra_b
        o_ref[:, (2 * m) * LANES:(2 * m + 1) * LANES] = a.astype(BF16)
        o_ref[:, (2 * m + 1) * LANES:(2 * m + 2) * LANES] = b.astype(BF16)

    def rope(xc):
        return xc * rc + pltpu.roll(xc, LANES - ROT_DIM // 2, 1) * rsa + pltpu.roll(xc, ROT_DIM // 2, 1) * rsb

    pdq = jnp.dot(hb, w_ref[:, 0:DIFF_WIDTH], preferred_element_type=F32)
    for m in range(N_DIFF_HEADS):
        split_store(rope(pdq[:, m * LANES:(m + 1) * LANES]) * scale, dq_ref, m)
    pdk = jnp.dot(hb, w_ref[:, DIFF_WIDTH:2 * DIFF_WIDTH], preferred_element_type=F32)
    for m in range(N_DIFF_HEADS):
        split_store(rope(pdk[:, m * LANES:(m + 1) * LANES]), dk_ref, m)
    def store_values_t(pv, o_ref):
        for m in range(4):
            o_ref[m] = pv[:, m * LANES:(m + 1) * LANES].T.astype(BF16)

    store_values_t(jnp.dot(hb, w_ref[:, 2 * DIFF_WIDTH:3 * DIFF_WIDTH], preferred_element_type=F32),
                   dv_ref)
    o = 3 * DIFF_WIDTH
    pfq = jnp.dot(hb, w_ref[:, o:o + FOX_WIDTH], preferred_element_type=F32)
    for m in range(N_FOX_HEADS // 2):
        split_store(pfq[:, m * LANES:(m + 1) * LANES] * scale, fq_ref, m,
                    aug[:, (2 * m) * LANES:(2 * m + 1) * LANES],
                    aug[:, (2 * m + 1) * LANES:(2 * m + 2) * LANES])
    pfk = jnp.dot(hb, w_ref[:, o + FOX_WIDTH:o + 2 * FOX_WIDTH], preferred_element_type=F32)
    for m in range(N_FOX_HEADS // 2):
        split_store(pfk[:, m * LANES:(m + 1) * LANES], fk_ref, m,
                    aug[:, QK_WIDTH + (2 * m) * LANES:QK_WIDTH + (2 * m + 1) * LANES],
                    aug[:, QK_WIDTH + (2 * m + 1) * LANES:QK_WIDTH + (2 * m + 2) * LANES])
    store_values_t(jnp.dot(hb, w_ref[:, o + 2 * FOX_WIDTH:o + 3 * FOX_WIDTH],
                           preferred_element_type=F32), fv_ref)


def _forget_placement():
    nh = N_FOX_HEADS
    p = np.zeros((LANES, 2 * QK_WIDTH), np.float32)
    for h in range(nh):
        base_q = h * LANES + HEAD_DIM
        base_k = QK_WIDTH + h * LANES + HEAD_DIM
        for part in range(3):
            p[part * nh + h, base_q + part] = 1.0
            p[3 * nh, base_q + 3 + part] = 1.0
            p[3 * nh, base_k + part] = 1.0
            p[part * nh + h, base_k + 3 + part] = -1.0
    return jnp.asarray(p, BF16)


def _inproj(x, moe, gate, sc, sh, g, w_bf, b_forget, tables, pq):
    fuse = moe is not None
    tpb = SEQ // TM
    row = pl.BlockSpec((TM, D_MODEL), lambda i: (i, 0))
    per_batch = pl.BlockSpec((None, 1, D_MODEL), lambda i: (i // tpb, 0, 0))
    const = lambda shape: pl.BlockSpec(shape, lambda i: (0,) * len(shape))
    tab = pl.BlockSpec((TM, LANES), lambda i: (i, 0))
    in_specs = [row]
    args = [x]
    if fuse:
        in_specs += [tab, pl.BlockSpec((TOP_K, N_PLANES, TM, LANES), lambda i: (0, 0, i, 0)), per_batch]
        args += [moe[0], moe[1], gate]
    in_specs += [per_batch, per_batch, const((1, D_MODEL)), const((D_MODEL, IN_COLS_PAD)),
                 const((1, LANES)), tab, tab, tab, const((LANES, 2 * QK_WIDTH))]
    args += [sc, sh, g.reshape(1, D_MODEL), w_bf, b_forget, *tables, pq]
    wide = pl.BlockSpec((TM, QK_WIDTH), lambda i: (i, 0))
    half = pl.BlockSpec((None, 4, None, LANES, TM), lambda i: (i // tpb, 0, i % tpb, 0, 0))
    wide_s = jax.ShapeDtypeStruct((N_TOK, QK_WIDTH), BF16)
    half_s = jax.ShapeDtypeStruct((BATCH, 4, tpb, LANES, TM), BF16)
    out_specs = [wide, wide, half, wide, wide, half]
    out_shape = [wide_s, wide_s, half_s, wide_s, wide_s, half_s]
    if fuse:
        out_specs = [row] + out_specs
        out_shape = [jax.ShapeDtypeStruct((N_TOK, D_MODEL), F32)] + out_shape
    outs = pl.pallas_call(
        functools.partial(_inproj_kernel, fuse),
        grid=(N_TOK // TM,),
        in_specs=in_specs,
        out_specs=out_specs,
        out_shape=out_shape,
        scratch_shapes=[pltpu.VMEM((8, LANES), F32)],
        compiler_params=_params("arbitrary"),
        name="norm_inproj",
    )(*args)
    if fuse:
        return outs[0], outs[1:]
    return x, outs


def _attn_kernel(diff, lambda_init, qa_ref, qb_ref, ka_ref, kb_ref, v_ref, g_ref, lam_ref, o_ref,
                 *scratch):
    nq = SEQ // TQ
    n_half = TQ // ATTN_TQ
    chains = []
    for mi, (q_ref, k_ref) in enumerate(((qa_ref, ka_ref), (qb_ref, kb_ref))):
        for h in range(n_half):
            c = mi * n_half + h
            qt_sc, s_sc, p_sc, m_sc, l_sc, acc_sc = scratch[c::2 * n_half]
            chains.append((h, k_ref, qt_sc, s_sc, p_sc, m_sc, l_sc, acc_sc, q_ref))

    def load_queries(qi):
        for mi in range(2):
            q_ref = chains[mi * n_half][8]
            qt = q_ref[pl.ds(pl.multiple_of(qi * TQ, TQ), TQ), :].astype(F32).T.astype(BF16)
            for h in range(n_half):
                chains[mi * n_half + h][2][...] = qt[:, h * ATTN_TQ:(h + 1) * ATTN_TQ]

    def reset_state():
        for chain in chains:
            m_sc, l_sc, acc_sc = chain[5:8]
            m_sc[...] = jnp.full(m_sc.shape, NEG, F32)
            l_sc[...] = jnp.zeros(l_sc.shape, F32)
            acc_sc[...] = jnp.zeros(acc_sc.shape, F32)

    def scores(chain, j, masked):
        h, k_ref, qt_sc, s_sc = chain[:4]
        off = pl.multiple_of(j * TQ, TQ)
        s = jnp.dot(k_ref[pl.ds(off, TQ), :], qt_sc[...], preferred_element_type=F32)
        if masked:
            kk = lax.broadcasted_iota(jnp.int32, (TQ, ATTN_TQ), 0)
            qq = h * ATTN_TQ + lax.broadcasted_iota(jnp.int32, (TQ, ATTN_TQ), 1)
            s = jnp.where((kk // CHUNK <= qq // CHUNK) if diff else (kk <= qq), s, NEG)
        s_sc[...] = s

    def softmax(chain):
        s_sc, p_sc, m_sc, l_sc = chain[3:7]
        m_all = m_sc[...]
        m_parts, sum_parts = [], []
        for c0 in range(0, ATTN_TQ, LANES):
            cols = slice(c0, c0 + LANES)
            pm = s_sc[0:ATTN_ROWS, cols]
            for r0 in range(ATTN_ROWS, TQ, ATTN_ROWS):
                pm = jnp.maximum(pm, s_sc[r0:r0 + ATTN_ROWS, cols])
            m_new = jnp.maximum(m_all[:, cols], jnp.max(pm, axis=0, keepdims=True))
            ps = jnp.zeros((ATTN_ROWS, LANES), F32)
            for r0 in range(0, TQ, ATTN_ROWS):
                p = jnp.exp2(s_sc[r0:r0 + ATTN_ROWS, cols] - m_new)
                p_sc[r0:r0 + ATTN_ROWS, cols] = p.astype(BF16)
                ps = ps + p
            m_parts.append(m_new)
            sum_parts.append(jnp.sum(ps, axis=0, keepdims=True))
        m_new = jnp.concatenate(m_parts, axis=1)
        alpha = jnp.exp2(m_all - m_new)
        m_sc[...] = m_new
        l_sc[...] = alpha * l_sc[...] + jnp.concatenate(sum_parts, axis=1)
        return alpha

    def values(chain, j, alpha):
        p_sc, acc_sc = chain[4], chain[7]
        pv = jnp.dot(v_ref[j], p_sc[...], preferred_element_type=F32)
        acc_sc[...] = alpha * acc_sc[...] + pv

    def all_scores(j, masked):
        for chain in chains:
            scores(chain, j, masked)

    def consume(j, nxt=None, nxt_masked=False):
        for chain in chains:
            alpha = softmax(chain)
            if nxt is not None:
                scores(chain, nxt, nxt_masked)
            values(chain, j, alpha)

    def finalize(qi):
        ot = [jnp.concatenate([chains[mi * n_half + h][7][...] / chains[mi * n_half + h][6][...]
                               for h in range(n_half)], axis=1) for mi in range(2)]
        g = g_ref[...]
        rows = pl.ds(pl.multiple_of(qi * TQ, TQ), TQ)
        if diff:
            lv = lam_ref[...]
            lam = (jnp.exp(jnp.sum(lv[0:1] * lv[1:2], axis=1, keepdims=True))
                   - jnp.exp(jnp.sum(lv[2:3] * lv[3:4], axis=1, keepdims=True)) + lambda_init)
            o = (ot[0] - lam * ot[1]).T
            y = o * lax.rsqrt(jnp.mean(o * o, axis=1, keepdims=True) + EPS) * g
            o_ref[rows, :] = (y * (1.0 - lambda_init)).astype(o_ref.dtype)
        else:
            o = jnp.concatenate([ot[0][:HEAD_DIM], ot[1][HEAD_DIM:]], axis=0).T
            low = _lane_iota((TQ, LANES)) < HEAD_DIM
            sq = o * o
            msa = jnp.sum(jnp.where(low, sq, 0.0), axis=1, keepdims=True) / HEAD_DIM
            msb = jnp.sum(jnp.where(low, 0.0, sq), axis=1, keepdims=True) / HEAD_DIM
            inv = jnp.where(low, lax.rsqrt(msa + EPS), lax.rsqrt(msb + EPS))
            o_ref[rows, :] = (o * inv * g).astype(o_ref.dtype)

    load_queries(0)
    reset_state()
    all_scores(0, True)

    @pl.loop(0, nq)
    def _(qi):
        @pl.when(qi > 0)
        def _():
            @pl.loop(0, qi - 1)
            def _(j):
                consume(j, j + 1)

            consume(qi - 1, qi, True)

        @pl.when(qi < nq - 1)
        def _():
            load_queries(qi + 1)
            consume(qi, 0)

        @pl.when(qi == nq - 1)
        def _():
            consume(qi)

        finalize(qi)
        reset_state()


def _attention(diff, lambda_init, q, k, v, g, lamv):
    nq = SEQ // TQ
    kspec = lambda par: pl.BlockSpec((SEQ, LANES), lambda b, p: (b, 2 * p + par))
    return pl.pallas_call(
        functools.partial(_attn_kernel, diff, lambda_init),
        grid=(BATCH, 4),
        in_specs=[kspec(0), kspec(1), kspec(0), kspec(1),
                  pl.BlockSpec((None, None, nq, LANES, TQ), lambda b, p: (b, p, 0, 0, 0)),
                  pl.BlockSpec((1, LANES), lambda b, p: (0, 0)),
                  pl.BlockSpec((8, LANES), lambda b, p: (0, 0))],
        out_specs=pl.BlockSpec((SEQ, LANES), lambda b, p: (b, p)),
        out_shape=jax.ShapeDtypeStruct((N_TOK, DIFF_WIDTH), BF16),
        scratch_shapes=[pltpu.VMEM(shape, dt)
                        for shape, dt in (((LANES, ATTN_TQ), BF16), ((TQ, ATTN_TQ), F32),
                                          ((TQ, ATTN_TQ), BF16), ((1, ATTN_TQ), F32),
                                          ((1, ATTN_TQ), F32), ((LANES, ATTN_TQ), F32))
                        for _ in range(2 * TQ // ATTN_TQ)],
        compiler_params=_params("arbitrary", "arbitrary"),
        name="diff_attention" if diff else "fox_attention",
    )(q, q, k, k, v, g, lamv)


def _outproj_kernel(x_ref, od_ref, of_ref, gt_ref, sc_ref, sh_ref, g_ref, wo_ref, wr_ref, br_ref,
                    x1_ref, h2_ref, rt_ref, cnt_ref, carry_ref):
    @pl.when(pl.program_id(0) == 0)
    def _():
        carry_ref[...] = jnp.zeros_like(carry_ref)

    mix = jnp.dot(od_ref[...], wo_ref[0:DIFF_WIDTH, :], preferred_element_type=F32)
    mix += jnp.dot(of_ref[...], wo_ref[DIFF_WIDTH:, :], preferred_element_type=F32)
    x1 = x_ref[...] + gt_ref[...] * mix
    x1_ref[...] = x1
    h = _rms_mod(x1, g_ref[...], sc_ref[...], sh_ref[...])
    hh = h.astype(BF16)
    _pack_planes(h, h2_ref)
    hl = (h - hh.astype(F32)).astype(BF16)
    r1 = jnp.dot(hh, wr_ref[...], preferred_element_type=F32)
    r2 = jnp.dot(hl, wr_ref[:, 0:LANES], preferred_element_type=F32)
    logits = r1[:, 0:LANES] + r1[:, LANES:] + r2 + br_ref[...]

    lane = _lane_iota((TM, LANES))
    lanef = lane.astype(F32)
    big = float(LANES)
    isg = lane < N_GROUPS
    lg = jnp.where(isg, logits, NEG)
    mg = jnp.max(lg, axis=1, keepdims=True)
    sg = jnp.sum(jnp.where(isg, jnp.exp(lg - mg), 0.0), axis=1, keepdims=True)
    p_g = 1.0 / sg
    gsel = jnp.min(jnp.where(isg & (lg == mg), lanef, big), axis=1, keepdims=True)
    lo = N_GROUPS + gsel * EXPERTS_PER_GROUP
    ise = (lanef >= lo) & (lanef < lo + EXPERTS_PER_GROUP)
    le = jnp.where(ise, logits, NEG)
    t1 = jnp.max(le, axis=1, keepdims=True)
    i1 = jnp.min(jnp.where(ise & (le == t1), lanef, big), axis=1, keepdims=True)
    ise2 = ise & (lanef != i1)
    le2 = jnp.where(ise2, logits, NEG)
    t2 = jnp.max(le2, axis=1, keepdims=True)
    i2 = jnp.min(jnp.where(ise2 & (le2 == t2), lanef, big), axis=1, keepdims=True)
    d = jnp.exp(t2 - t1)
    w1 = p_g / (1.0 + d)
    w2 = p_g * d / (1.0 + d)
    e1 = i1 - N_GROUPS
    e2 = i2 - N_GROUPS
    oh1 = lanef == e1
    oh2 = lanef == e2
    both = jnp.where(oh1 | oh2, 1.0, 0.0)
    row = lax.broadcasted_iota(jnp.int32, (TM, TM), 0)
    col = lax.broadcasted_iota(jnp.int32, (TM, TM), 1)
    before = jnp.dot((row > col).astype(BF16), both.astype(BF16), preferred_element_type=F32)
    before = before + carry_ref[0:1, :]
    rank1 = jnp.sum(jnp.where(oh1, before, 0.0), axis=1, keepdims=True)
    rank2 = jnp.sum(jnp.where(oh2, before, 0.0), axis=1, keepdims=True)
    total = carry_ref[0:1, :] + jnp.sum(both, axis=0, keepdims=True)
    carry_ref[...] = jnp.broadcast_to(total, carry_ref.shape)
    cnt_ref[...] = jnp.broadcast_to(total, cnt_ref.shape)
    vals = (e1, e2, w1, w2, rank1, rank2)
    out = jnp.zeros((TM, LANES), F32)
    for j, v in enumerate(vals):
        out = jnp.where(lane == j, v, out)
    rt_ref[...] = out


def _outproj(x, od, of, gt, sc, sh, g, wo_bf, wr, br):
    tpb = SEQ // TM
    row = pl.BlockSpec((TM, D_MODEL), lambda i: (i, 0))
    half = pl.BlockSpec((TM, DIFF_WIDTH), lambda i: (i, 0))
    per_batch = pl.BlockSpec((None, 1, D_MODEL), lambda i: (i // tpb, 0, 0))
    const = lambda shape: pl.BlockSpec(shape, lambda i: (0,) * len(shape))
    return pl.pallas_call(
        _outproj_kernel,
        grid=(N_TOK // TM,),
        in_specs=[row, half, half, per_batch, per_batch, per_batch, const((1, D_MODEL)),
                  const((D_MODEL, D_MODEL)), const((D_MODEL, 2 * LANES)), const((1, LANES))],
        out_specs=[row, pl.BlockSpec((N_PLANES, TM, LANES), lambda i: (0, i, 0)),
                   pl.BlockSpec((TM, LANES), lambda i: (i, 0)), const((8, LANES))],
        out_shape=[jax.ShapeDtypeStruct((N_TOK, D_MODEL), F32),
                   jax.ShapeDtypeStruct((N_PLANES, N_TOK, LANES), jnp.int32),
                   jax.ShapeDtypeStruct((N_TOK, LANES), F32),
                   jax.ShapeDtypeStruct((8, LANES), F32)],
        scratch_shapes=[pltpu.VMEM((8, LANES), F32)],
        compiler_params=_params("arbitrary"),
        name="outproj_router",
    )(x, od, of, gt, sc, sh, g.reshape(1, D_MODEL), wo_bf, wr, br)


def _expert_kernel(be_ref, cnt_ref, xs_ref, wg_ref, wu_ref, wd_ref, ys_ref, wg_sc, wu_sc, wd_sc):
    i = pl.program_id(0)
    cnt = cnt_ref[i]

    @pl.when((i == 0) | (be_ref[i] != be_ref[jnp.maximum(i - 1, 0)]))
    def _():
        wg_sc[...] = wg_ref[...].astype(BF16)
        wu_sc[...] = wu_ref[...].astype(BF16)
        wd_sc[...] = wd_ref[...].astype(BF16)

    @pl.when(cnt > 0)
    def _():
        live = lax.broadcasted_iota(jnp.int32, (MOE_BLOCK, LANES), 0) < cnt
        xb = _unpack_planes([jnp.where(live, xs_ref[p], 0) for p in range(N_PLANES)]).astype(BF16)
        a = jnp.dot(xb, wg_sc[...], preferred_element_type=F32)
        u = jnp.dot(xb, wu_sc[...], preferred_element_type=F32)
        hid = (a / (1.0 + jnp.exp(-a)) * u).astype(BF16)
        _pack_planes(jnp.dot(hid, wd_sc[...], preferred_element_type=F32), ys_ref)

    @pl.when(cnt == 0)
    def _():
        ys_ref[...] = jnp.zeros_like(ys_ref)


def _experts(layer, block_expert, block_count, xs, wg, wu, wd):
    planes = pl.BlockSpec((N_PLANES, MOE_BLOCK, LANES), lambda i, be, bc: (0, i, 0))
    w_in = pl.BlockSpec((None, None, D_MODEL, D_EXPERT), lambda i, be, bc: (layer, be[i], 0, 0))
    w_out = pl.BlockSpec((None, None, D_EXPERT, D_MODEL), lambda i, be, bc: (layer, be[i], 0, 0))
    grid_spec = pltpu.PrefetchScalarGridSpec(
        num_scalar_prefetch=2,
        grid=(MOE_NBLOCKS,),
        in_specs=[planes, w_in, w_in, w_out],
        out_specs=planes,
        scratch_shapes=[pltpu.VMEM((D_MODEL, D_EXPERT), BF16), pltpu.VMEM((D_MODEL, D_EXPERT), BF16),
                        pltpu.VMEM((D_EXPERT, D_MODEL), BF16)],
    )
    return pl.pallas_call(
        _expert_kernel,
        grid_spec=grid_spec,
        out_shape=jax.ShapeDtypeStruct((N_PLANES, MOE_ROWS, LANES), jnp.int32),
        compiler_params=_params("arbitrary"),
        name="expert_mlp",
    )(block_expert, block_count, xs, wg, wu, wd)


def _slots(route, counts):
    counts = counts[0, :N_EXPERTS].astype(jnp.int32)
    padded = ((counts + MOE_BLOCK - 1) // MOE_BLOCK) * MOE_BLOCK
    pend = jnp.cumsum(padded)
    pstart = pend - padded
    e = route[:, 0:TOP_K].astype(jnp.int32)
    rank = route[:, 2 * TOP_K:3 * TOP_K].astype(jnp.int32)
    onehot = e[:, :, None] == jnp.arange(N_EXPERTS, dtype=jnp.int32)
    dest = jnp.sum(jnp.where(onehot, pstart, 0), axis=-1) + rank
    bstart = jnp.arange(MOE_NBLOCKS, dtype=jnp.int32) * MOE_BLOCK
    block_expert = jnp.minimum(jnp.sum(bstart[:, None] >= pend[None, :], axis=1), N_EXPERTS - 1)
    block_expert = block_expert.astype(jnp.int32)
    block_count = jnp.clip(counts[block_expert] - (bstart - pstart[block_expert]), 0, MOE_BLOCK)
    block_count = jnp.where(bstart < pend[-1], block_count, 0).astype(jnp.int32)
    plane_off = jnp.arange(N_PLANES, dtype=jnp.int32) * MOE_ROWS
    rows = dest.T[:, None, :] + plane_off[None, :, None]
    return rows.astype(jnp.int32), block_expert, block_count


def _sc_workers():
    info = plsc.get_sparse_core_info()
    return info.num_cores, info.num_cores * info.num_subcores


def _sc_scatter2(src, idx_a, idx_b, out_rows):
    n_src = src.shape[0]
    nc, nw = _sc_workers()
    per_w = n_src // nw
    steps = per_w // SC_WINDOW
    mesh = plsc.VectorSubcoreMesh(core_axis_name="c", subcore_axis_name="s")

    @functools.partial(
        pl.kernel, mesh=mesh,
        out_type=jax.ShapeDtypeStruct((out_rows, LANES), src.dtype),
        scratch_types=[pltpu.VMEM((SC_WINDOW,), jnp.int32), pltpu.VMEM((SC_WINDOW,), jnp.int32),
                       pltpu.VMEM((SC_WINDOW, LANES), src.dtype)],
        name="sc_dispatch_scatter",
    )
    def k(src_hbm, ia_hbm, ib_hbm, out_hbm, ia_v, ib_v, rows_v):
        base = (lax.axis_index("s") * nc + lax.axis_index("c")) * per_w

        @pl.loop(0, steps)
        def _(j):
            off = base + j * SC_WINDOW
            pltpu.sync_copy(ia_hbm.at[pl.ds(off, SC_WINDOW)], ia_v)
            pltpu.sync_copy(ib_hbm.at[pl.ds(off, SC_WINDOW)], ib_v)
            pltpu.sync_copy(src_hbm.at[pl.ds(off, SC_WINDOW)], rows_v)
            pltpu.sync_copy(rows_v, out_hbm.at[ia_v])
            pltpu.sync_copy(rows_v, out_hbm.at[ib_v])

    return k(src, idx_a, idx_b)


def _sc_gather(table, idx):
    n_out = idx.shape[0]
    nc, nw = _sc_workers()
    per_w = n_out // nw
    steps = per_w // SC_WINDOW
    mesh = plsc.VectorSubcoreMesh(core_axis_name="c", subcore_axis_name="s")

    @functools.partial(
        pl.kernel, mesh=mesh,
        out_type=jax.ShapeDtypeStruct((n_out, LANES), table.dtype),
        scratch_types=[pltpu.VMEM((SC_WINDOW,), jnp.int32), pltpu.VMEM((SC_WINDOW, LANES), table.dtype)],
        name="sc_combine_gather",
    )
    def k(table_hbm, idx_hbm, out_hbm, idx_v, rows_v):
        base = (lax.axis_index("s") * nc + lax.axis_index("c")) * per_w

        @pl.loop(0, steps)
        def _(j):
            off = base + j * SC_WINDOW
            pltpu.sync_copy(idx_hbm.at[pl.ds(off, SC_WINDOW)], idx_v)
            pltpu.sync_copy(table_hbm.at[idx_v], rows_v)
            pltpu.sync_copy(rows_v, out_hbm.at[pl.ds(off, SC_WINDOW)])

    return k(table, idx)


def _final_kernel(x_ref, rt_ref, y_ref, gt_ref, g_ref, o_ref):
    x = x_ref[...] + gt_ref[...] * _combine(rt_ref, y_ref)
    ms = jnp.mean(x * x, axis=-1, keepdims=True)
    o_ref[...] = x * lax.rsqrt(ms + EPS) * g_ref[...]


def _final(x, moe, gate, g):
    tpb = SEQ // TM
    row = pl.BlockSpec((TM, D_MODEL), lambda i: (i, 0))
    return pl.pallas_call(
        _final_kernel,
        grid=(N_TOK // TM,),
        in_specs=[row, pl.BlockSpec((TM, LANES), lambda i: (i, 0)),
                  pl.BlockSpec((TOP_K, N_PLANES, TM, LANES), lambda i: (0, 0, i, 0)),
                  pl.BlockSpec((None, 1, D_MODEL), lambda i: (i // tpb, 0, 0)),
                  pl.BlockSpec((1, D_MODEL), lambda i: (0, 0))],
        out_specs=row,
        out_shape=jax.ShapeDtypeStruct((N_TOK, D_MODEL), F32),
        compiler_params=_params("arbitrary"),
        name="final_norm",
    )(x, moe[0], moe[1], gate, g.reshape(1, D_MODEL))


def kernel(x, c, positions, w_ada, b_ada, g_mix, w_in, b_forget, lambda_q1, lambda_k1, lambda_q2,
           lambda_k2, g_subln, g_fox_out, w_out, g_ffn, w_router_group, b_router_group,
           w_router_expert, b_router_expert, w_expert_gate, w_expert_up, w_expert_down, g_final):
    mod = _modulation(c, w_ada, b_ada)
    mod = mod.reshape(DEPTH, BATCH, 6, 1, D_MODEL)
    tables = _rope_tables(positions)
    pq = _forget_placement()
    xf = x.reshape(N_TOK, D_MODEL)
    moe = None
    gate = None
    for l in range(DEPTH):
        sh1, sc1, gt1, sh2, sc2, gt2 = (mod[l, :, j] for j in range(6))
        w_bf = jnp.pad(w_in[l], ((0, 0), (0, IN_COLS_PAD - IN_COLS))).astype(BF16)
        bfp = jnp.pad(b_forget[l], (0, LANES - N_FOX_HEADS)).reshape(1, LANES)
        xf, (dq, dk, dv, fq, fk, fv) = _inproj(xf, moe, gate, sc1, sh1, g_mix[l], w_bf, bfp, tables, pq)

        lambda_init = 0.8 - 0.6 * float(np.exp(-0.3 * l))
        lamv = jnp.zeros((8, LANES), F32).at[0:4, 0:HEAD_DIM].set(
            jnp.stack([lambda_q1[l], lambda_k1[l], lambda_q2[l], lambda_k2[l]]))
        g_d = g_subln[l].reshape(1, LANES)
        g_f = jnp.concatenate([g_fox_out[l], g_fox_out[l]]).reshape(1, LANES)
        od = _attention(True, lambda_init, dq, dk, dv, g_d, lamv)
        of = _attention(False, lambda_init, fq, fk, fv, g_f, lamv)

        wr32 = jnp.pad(jnp.concatenate([w_router_group[l], w_router_expert[l]], axis=1),
                       ((0, 0), (0, LANES - N_GROUPS - N_EXPERTS)))
        wr_hi = wr32.astype(BF16)
        wr_lo = (wr32 - wr_hi.astype(F32)).astype(BF16)
        wr = jnp.concatenate([wr_hi, wr_lo], axis=1)
        br = jnp.pad(jnp.concatenate([b_router_group[l], b_router_expert[l]]),
                     (0, LANES - N_GROUPS - N_EXPERTS)).reshape(1, LANES)
        xf, h2, route, counts = _outproj(xf, od, of, gt1, sc2, sh2, g_ffn[l], w_out[l].astype(BF16),
                                         wr, br)

        rows, block_expert, block_count = _slots(route, counts)
        xs = _sc_scatter2(h2.reshape(N_PLANES * N_TOK, LANES), rows[0].reshape(-1),
                          rows[1].reshape(-1), N_PLANES * MOE_ROWS)
        ys = _experts(l, block_expert, block_count, xs.reshape(N_PLANES, MOE_ROWS, LANES),
                      w_expert_gate, w_expert_up, w_expert_down)
        y2 = _sc_gather(ys.reshape(N_PLANES * MOE_ROWS, LANES), rows.reshape(-1))
        moe = (route, y2.reshape(TOP_K, N_PLANES, N_TOK, LANES))
        gate = gt2
    out = _final(xf, moe, gate, g_final)
    return out.reshape(BATCH, SEQ, D_MODEL)
```

```python
import functools

import numpy as np
import jax
import jax.numpy as jnp
from jax import lax
from jax.experimental import pallas as pl
from jax.experimental.pallas import tpu as pltpu
from jax.experimental.pallas import tpu_sc as plsc

D_MODEL = 1024
BATCH = 4
SEQ = 4096
DEPTH = 4
N_TOK = BATCH * SEQ

CHUNK = 64
HEAD_DIM = 64
N_DIFF_HEADS = 4
N_FOX_HEADS = 8
DIFF_WIDTH = 512
FOX_WIDTH = 512
IN_COLS = 3 * DIFF_WIDTH + 3 * FOX_WIDTH + N_FOX_HEADS
ROT_DIM = 16
ROPE_THETA = 500000.0
N_GROUPS = 4
EXPERTS_PER_GROUP = 8
N_EXPERTS = 32
TOP_K = 2
D_EXPERT = 512
EPS = 1e-6

LANES = 128
IN_COLS_PAD = 3200
FF_COL = 3 * DIFF_WIDTH + 3 * FOX_WIDTH
QK_WIDTH = 8 * LANES
TM = 512
TQ = 512
ATTN_TQ = 256
N_LATE_CHAINS = 1
ONES_ROWS = 16
ATTN_ROWS = 64
LOG2E = 1.4426950408889634
MOE_BLOCK = 256
MOE_ROWS = N_TOK * TOP_K + N_EXPERTS * MOE_BLOCK
MOE_NBLOCKS = MOE_ROWS // MOE_BLOCK
N_PLANES = D_MODEL // 2 // LANES
SC_WINDOW = 128
NEG = -1e30
VMEM_LIMIT = 56 * 1024 * 1024

F32 = jnp.float32
BF16 = jnp.bfloat16


def _bf16_round(x):
    return x.astype(BF16).astype(F32)


def _lane_iota(shape):
    return lax.broadcasted_iota(jnp.int32, shape, 1)


def _params(*sem):
    return pltpu.CompilerParams(dimension_semantics=sem, vmem_limit_bytes=VMEM_LIMIT)


def _pack_planes(y, o_ref):
    bits = lax.bitcast_convert_type(_bf16_round(y), jnp.uint32)
    half = D_MODEL // 2
    word = bits[:, half:] | lax.shift_right_logical(bits[:, :half], jnp.uint32(16))
    word = lax.bitcast_convert_type(word, jnp.int32)
    for p in range(N_PLANES):
        o_ref[p] = word[:, p * LANES:(p + 1) * LANES]


def _unpack_planes(planes):
    lo, hi = [], []
    for w in planes:
        u = lax.bitcast_convert_type(w, jnp.uint32)
        lo.append(lax.bitcast_convert_type(lax.shift_left(u, jnp.uint32(16)), F32))
        hi.append(lax.bitcast_convert_type(u & jnp.uint32(0xFFFF0000), F32))
    return jnp.concatenate(lo + hi, axis=1)


def _combine(route_ref, y_ref):
    rt = route_ref[...]
    y0 = _unpack_planes([y_ref[0, p] for p in range(N_PLANES)])
    y1 = _unpack_planes([y_ref[1, p] for p in range(N_PLANES)])
    return rt[:, 2:3] * y0 + rt[:, 3:4] * y1


def _mod_kernel(c_ref, w_ref, b_ref, o_ref):
    c = c_ref[...]
    cond = c / (1.0 + jnp.exp(-c))
    ch = cond.astype(BF16)
    cl = (cond - ch.astype(F32)).astype(BF16)
    w = w_ref[...]
    wh = w.astype(BF16)
    wl = (w - wh.astype(F32)).astype(BF16)
    acc = jnp.dot(ch, wh, preferred_element_type=F32)
    acc += jnp.dot(cl, wh, preferred_element_type=F32)
    acc += jnp.dot(ch, wl, preferred_element_type=F32)
    o_ref[...] = acc + b_ref[...]


def _modulation(c, w_ada, b_ada):
    rows = 16
    tn = 1536
    c_pad = jnp.zeros((rows, D_MODEL), F32).at[:BATCH].set(c)
    out = pl.pallas_call(
        _mod_kernel,
        grid=(DEPTH, 6 * D_MODEL // tn),
        in_specs=[
            pl.BlockSpec((rows, D_MODEL), lambda l, n: (0, 0)),
            pl.BlockSpec((None, D_MODEL, tn), lambda l, n: (l, 0, n)),
            pl.BlockSpec((None, 1, tn), lambda l, n: (l, 0, n)),
        ],
        out_specs=pl.BlockSpec((None, rows, tn), lambda l, n: (l, 0, n)),
        out_shape=jax.ShapeDtypeStruct((DEPTH, rows, 6 * D_MODEL), F32),
        compiler_params=_params("arbitrary", "arbitrary"),
        name="adaln_mod",
    )(c_pad, w_ada, b_ada.reshape(DEPTH, 1, 6 * D_MODEL))
    return out[:, :BATCH]


def _rope_kernel(pos_ref, inv_ref, c_ref, sa_ref, sb_ref):
    ang = pos_ref[...].astype(F32) * inv_ref[...]
    j = _lane_iota(ang.shape) % HEAD_DIM
    cosv = jnp.cos(ang)
    sinv = jnp.sin(ang)
    half = ROT_DIM // 2
    c_ref[...] = jnp.where(j < ROT_DIM, cosv, 1.0)
    sa_ref[...] = jnp.where(j < half, -sinv, 0.0)
    sb_ref[...] = jnp.where((j >= half) & (j < ROT_DIM), sinv, 0.0)


def _rope_tables(positions):
    half = ROT_DIM // 2
    inv = ROPE_THETA ** (-jnp.arange(0, ROT_DIM, 2, dtype=F32) / ROT_DIM)
    lane = np.arange(LANES)
    inv_lane = inv[(lane % HEAD_DIM) % half].reshape(1, LANES)
    spec = pl.BlockSpec((TM, LANES), lambda i: (i, 0))
    shape = jax.ShapeDtypeStruct((N_TOK, LANES), F32)
    return pl.pallas_call(
        _rope_kernel,
        grid=(N_TOK // TM,),
        in_specs=[pl.BlockSpec((TM, 1), lambda i: (i, 0)),
                  pl.BlockSpec((1, LANES), lambda i: (0, 0))],
        out_specs=[spec, spec, spec],
        out_shape=[shape, shape, shape],
        compiler_params=_params("arbitrary"),
        name="rope_tables",
    )(positions.reshape(N_TOK, 1), inv_lane)


def _rms_mod(x, g, sc, sh):
    ms = jnp.mean(x * x, axis=-1, keepdims=True)
    return (x * lax.rsqrt(ms + EPS) * g) * (1.0 + sc) + sh


def _inproj_kernel(fuse, *refs):
    if fuse:
        (x_ref, rt_ref, y_ref, gt_ref, sc_ref, sh_ref, g_ref, w_ref, bf_ref, c_ref, sa_ref, sb_ref,
         pq_ref, xo_ref, dq_ref, dk_ref, dv_ref, fq_ref, fk_ref, fv_ref, carry_ref) = refs
        x = x_ref[...] + gt_ref[...] * _combine(rt_ref, y_ref)
        xo_ref[...] = x
    else:
        (x_ref, sc_ref, sh_ref, g_ref, w_ref, bf_ref, c_ref, sa_ref, sb_ref,
         pq_ref, dq_ref, dk_ref, dv_ref, fq_ref, fk_ref, fv_ref, carry_ref) = refs
        x = x_ref[...]
    hb = _rms_mod(x, g_ref[...], sc_ref[...], sh_ref[...]).astype(BF16)

    @pl.when(pl.program_id(0) % (SEQ // TM) == 0)
    def _():
        carry_ref[...] = jnp.zeros_like(carry_ref)

    lane = _lane_iota((TM, LANES))
    nh = N_FOX_HEADS

    def pack3(a):
        hi = _bf16_round(a)
        r1 = a - hi
        mid = _bf16_round(r1)
        lo = _bf16_round(r1 - mid)
        return jnp.where(lane < nh, hi,
                         jnp.where(lane < 2 * nh, pltpu.roll(mid, nh, 1),
                                   jnp.where(lane < 3 * nh, pltpu.roll(lo, 2 * nh, 1), 0.0)))

    z = jnp.dot(hb, w_ref[:, FF_COL:FF_COL + LANES], preferred_element_type=F32) + bf_ref[...]
    logf = jnp.minimum(z, 0.0) - jnp.log(1.0 + jnp.exp(-jnp.abs(z)))
    logf = jnp.where(lane < nh, logf, 0.0)
    row = lax.broadcasted_iota(jnp.int32, (TM, TM), 0)
    col = lax.broadcasted_iota(jnp.int32, (TM, TM), 1)
    tri = (row >= col).astype(BF16)
    r = jnp.dot(tri, pack3(logf).astype(BF16), preferred_element_type=F32)
    cs = r + pltpu.roll(r, LANES - nh, 1) + pltpu.roll(r, LANES - 2 * nh, 1)
    cf = jnp.where(lane < nh, cs + carry_ref[0:1, :], 0.0)
    carry_ref[...] = jnp.broadcast_to(cf[TM - 1:TM, :], carry_ref.shape)

    t3 = jnp.where(lane == 3 * nh, 1.0, pack3(cf * LOG2E)).astype(BF16)
    aug = jnp.dot(t3, pq_ref[...], preferred_element_type=F32)

    low = lane < HEAD_DIM
    rc, rsa, rsb = c_ref[...], sa_ref[...], sb_ref[...]
    scale = HEAD_DIM ** -0.5 * LOG2E

    def split_store(chunk, o_ref, m, extra_a=None, extra_b=None):
        a = jnp.where(low, chunk, 0.0)
        b = jnp.where(low, pltpu.roll(chunk, HEAD_DIM, 1), 0.0)
        if extra_a is not None:
            a = a + extra_a
            b = b + extra_b
        o_ref[:, (2 * m) * LANES:(2 * m + 1) * LANES] = a.astype(BF16)
        o_ref[:, (2 * m + 1) * LANES:(2 * m + 2) * LANES] = b.astype(BF16)

    def rope(xc):
        return xc * rc + pltpu.roll(xc, LANES - ROT_DIM // 2, 1) * rsa + pltpu.roll(xc, ROT_DIM // 2, 1) * rsb

    pdq = jnp.dot(hb, w_ref[:, 0:DIFF_WIDTH], preferred_element_type=F32)
    for m in range(N_DIFF_HEADS):
        split_store(rope(pdq[:, m * LANES:(m + 1) * LANES]) * scale, dq_ref, m)
    pdk = jnp.dot(hb, w_ref[:, DIFF_WIDTH:2 * DIFF_WIDTH], preferred_element_type=F32)
    for m in range(N_DIFF_HEADS):
        split_store(rope(pdk[:, m * LANES:(m + 1) * LANES]), dk_ref, m)
    def store_values_t(pv, o_ref, width):
        ones = jnp.ones((ONES_ROWS, TM), BF16)
        for m in range(4):
            vt = pv[:, m * LANES:(m + 1) * LANES].T.astype(BF16)
            for i in range(LANES // width):
                o_ref[m * (LANES // width) + i, 0:width, :] = vt[i * width:(i + 1) * width]
                o_ref[m * (LANES // width) + i, width:width + ONES_ROWS, :] = ones

    store_values_t(jnp.dot(hb, w_ref[:, 2 * DIFF_WIDTH:3 * DIFF_WIDTH], preferred_element_type=F32),
                   dv_ref, 2 * HEAD_DIM)
    o = 3 * DIFF_WIDTH
    pfq = jnp.dot(hb, w_ref[:, o:o + FOX_WIDTH], preferred_element_type=F32)
    for m in range(N_FOX_HEADS // 2):
        split_store(pfq[:, m * LANES:(m + 1) * LANES] * scale, fq_ref, m,
                    aug[:, (2 * m) * LANES:(2 * m + 1) * LANES],
                    aug[:, (2 * m + 1) * LANES:(2 * m + 2) * LANES])
    pfk = jnp.dot(hb, w_ref[:, o + FOX_WIDTH:o + 2 * FOX_WIDTH], preferred_element_type=F32)
    for m in range(N_FOX_HEADS // 2):
        split_store(pfk[:, m * LANES:(m + 1) * LANES], fk_ref, m,
                    aug[:, QK_WIDTH + (2 * m) * LANES:QK_WIDTH + (2 * m + 1) * LANES],
                    aug[:, QK_WIDTH + (2 * m + 1) * LANES:QK_WIDTH + (2 * m + 2) * LANES])
    store_values_t(jnp.dot(hb, w_ref[:, o + 2 * FOX_WIDTH:o + 3 * FOX_WIDTH],
                           preferred_element_type=F32), fv_ref, HEAD_DIM)


def _forget_placement():
    nh = N_FOX_HEADS
    p = np.zeros((LANES, 2 * QK_WIDTH), np.float32)
    for h in range(nh):
        base_q = h * LANES + HEAD_DIM
        base_k = QK_WIDTH + h * LANES + HEAD_DIM
        for part in range(3):
            p[part * nh + h, base_q + part] = 1.0
            p[3 * nh, base_q + 3 + part] = 1.0
            p[3 * nh, base_k + part] = 1.0
            p[part * nh + h, base_k + 3 + part] = -1.0
    return jnp.asarray(p, BF16)


def _inproj(x, moe, gate, sc, sh, g, w_bf, b_forget, tables, pq):
    fuse = moe is not None
    tpb = SEQ // TM
    row = pl.BlockSpec((TM, D_MODEL), lambda i: (i, 0))
    per_batch = pl.BlockSpec((None, 1, D_MODEL), lambda i: (i // tpb, 0, 0))
    const = lambda shape: pl.BlockSpec(shape, lambda i: (0,) * len(shape))
    tab = pl.BlockSpec((TM, LANES), lambda i: (i, 0))
    in_specs = [row]
    args = [x]
    if fuse:
        in_specs += [tab, pl.BlockSpec((TOP_K, N_PLANES, TM, LANES), lambda i: (0, 0, i, 0)), per_batch]
        args += [moe[0], moe[1], gate]
    in_specs += [per_batch, per_batch, const((1, D_MODEL)), const((D_MODEL, IN_COLS_PAD)),
                 const((1, LANES)), tab, tab, tab, const((LANES, 2 * QK_WIDTH))]
    args += [sc, sh, g.reshape(1, D_MODEL), w_bf, b_forget, *tables, pq]
    wide = pl.BlockSpec((TM, QK_WIDTH), lambda i: (i, 0))
    def vspec(heads, width):
        rows = width + ONES_ROWS
        return (pl.BlockSpec((None, heads, None, rows, TM), lambda i: (i // tpb, 0, i % tpb, 0, 0)),
                jax.ShapeDtypeStruct((BATCH, heads, tpb, rows, TM), BF16))

    wide_s = jax.ShapeDtypeStruct((N_TOK, QK_WIDTH), BF16)
    dv_spec, dv_s = vspec(N_DIFF_HEADS, 2 * HEAD_DIM)
    fv_spec, fv_s = vspec(N_FOX_HEADS, HEAD_DIM)
    out_specs = [wide, wide, dv_spec, wide, wide, fv_spec]
    out_shape = [wide_s, wide_s, dv_s, wide_s, wide_s, fv_s]
    if fuse:
        out_specs = [row] + out_specs
        out_shape = [jax.ShapeDtypeStruct((N_TOK, D_MODEL), F32)] + out_shape
    outs = pl.pallas_call(
        functools.partial(_inproj_kernel, fuse),
        grid=(N_TOK // TM,),
        in_specs=in_specs,
        out_specs=out_specs,
        out_shape=out_shape,
        scratch_shapes=[pltpu.VMEM((8, LANES), F32)],
        compiler_params=_params("arbitrary"),
        name="norm_inproj",
    )(*args)
    if fuse:
        return outs[0], outs[1:]
    return x, outs


def _attn_kernel(diff, lambda_init, qa_ref, qb_ref, ka_ref, kb_ref, v_ref, g_ref, lam_ref, o_ref,
                 *scratch):
    nq = SEQ // TQ
    n_half = TQ // ATTN_TQ
    feat = 2 * HEAD_DIM if diff else HEAD_DIM
    chains = []
    for mi, (q_ref, k_ref) in enumerate(((qa_ref, ka_ref), (qb_ref, kb_ref))):
        for h in range(n_half):
            c = mi * n_half + h
            qt_sc, s_sc, p_sc, m_sc, a_sc, acc_sc = scratch[c::2 * n_half]
            vh = 0 if diff else mi
            chains.append((h, k_ref, qt_sc, s_sc, p_sc, m_sc, a_sc, acc_sc, q_ref, vh))
    order = [chains[mi * n_half + h] for h in range(n_half) for mi in range(2)]
    early, late = order[:-N_LATE_CHAINS], order[-N_LATE_CHAINS:]

    def load_queries(qi):
        for mi in range(2):
            q_ref = chains[mi * n_half][8]
            qt = q_ref[pl.ds(pl.multiple_of(qi * TQ, TQ), TQ), :].astype(F32).T.astype(BF16)
            for h in range(n_half):
                chains[mi * n_half + h][2][...] = qt[:, h * ATTN_TQ:(h + 1) * ATTN_TQ]

    def reset_state():
        for chain in chains:
            m_sc, _, acc_sc = chain[5:8]
            m_sc[...] = jnp.full(m_sc.shape, NEG, F32)
            acc_sc[...] = jnp.zeros(acc_sc.shape, F32)

    def n_keys(chain, masked):
        return (chain[0] + 1) * ATTN_TQ if masked else TQ

    def scores(chain, j, masked):
        h, k_ref, qt_sc, s_sc = chain[:4]
        nk = n_keys(chain, masked)
        off = pl.multiple_of(j * TQ, TQ)
        s = jnp.dot(k_ref[pl.ds(off, nk), :], qt_sc[...], preferred_element_type=F32)
        if masked:
            kk = lax.broadcasted_iota(jnp.int32, (nk, ATTN_TQ), 0)
            qq = h * ATTN_TQ + lax.broadcasted_iota(jnp.int32, (nk, ATTN_TQ), 1)
            s = jnp.where((kk // CHUNK <= qq // CHUNK) if diff else (kk <= qq), s, NEG)
        s_sc[0:nk, :] = s

    def softmax(chain, masked):
        s_sc, p_sc, m_sc, a_sc = chain[3:7]
        nk = n_keys(chain, masked)
        m_all = m_sc[...]
        m_parts = []
        for c0 in range(0, ATTN_TQ, LANES):
            cols = slice(c0, c0 + LANES)
            pm = s_sc[0:ATTN_ROWS, cols]
            for r0 in range(ATTN_ROWS, nk, ATTN_ROWS):
                pm = jnp.maximum(pm, s_sc[r0:r0 + ATTN_ROWS, cols])
            m_new = jnp.maximum(m_all[:, cols], jnp.max(pm, axis=0, keepdims=True))
            for r0 in range(0, nk, ATTN_ROWS):
                p = jnp.exp2(s_sc[r0:r0 + ATTN_ROWS, cols] - m_new)
                p_sc[r0:r0 + ATTN_ROWS, cols] = p.astype(BF16)
            m_parts.append(m_new)
        m_new = jnp.concatenate(m_parts, axis=1)
        a_sc[...] = jnp.exp2(m_all - m_new)
        m_sc[...] = m_new

    def values(chain, j, masked=False):
        p_sc, a_sc, acc_sc, vh = chain[4], chain[6], chain[7], chain[9]
        nk = n_keys(chain, masked)
        pv = jnp.dot(v_ref[vh, j, :, 0:nk], p_sc[0:nk, :], preferred_element_type=F32)
        acc_sc[...] = a_sc[...] * acc_sc[...] + pv

    def idle_late():
        for chain in late:
            chain[4][...] = jnp.zeros(chain[4].shape, BF16)
            chain[6][...] = jnp.ones(chain[6].shape, F32)

    def consume(j, cur_masked=False, nxt=None, nxt_masked=False, final=False, before_next=None):
        for chain in late:
            scores(chain, j, cur_masked)
        for chain in late:
            values(chain, jnp.maximum(j - 1, 0))
        if before_next is not None:
            before_next()
        for chain in early:
            softmax(chain, cur_masked)
            if nxt is not None:
                scores(chain, nxt, nxt_masked)
            values(chain, j, cur_masked)
        for chain in late:
            softmax(chain, cur_masked)
        if final:
            for chain in late:
                values(chain, j, cur_masked)

    def finalize(qi):
        ot = [jnp.concatenate([chains[mi * n_half + h][7][0:feat] / chains[mi * n_half + h][7][feat:feat + 1]
                               for h in range(n_half)], axis=1) for mi in range(2)]
        g = g_ref[...]
        rows = pl.ds(pl.multiple_of(qi * TQ, TQ), TQ)
        if diff:
            lv = lam_ref[...]
            lam = (jnp.exp(jnp.sum(lv[0:1] * lv[1:2], axis=1, keepdims=True))
                   - jnp.exp(jnp.sum(lv[2:3] * lv[3:4], axis=1, keepdims=True)) + lambda_init)
            o = (ot[0] - lam * ot[1]).T
            y = o * lax.rsqrt(jnp.mean(o * o, axis=1, keepdims=True) + EPS) * g
            o_ref[rows, :] = (y * (1.0 - lambda_init)).astype(o_ref.dtype)
        else:
            o = jnp.concatenate(ot, axis=0).T
            low = _lane_iota((TQ, LANES)) < HEAD_DIM
            sq = o * o
            msa = jnp.sum(jnp.where(low, sq, 0.0), axis=1, keepdims=True) / HEAD_DIM
            msb = jnp.sum(jnp.where(low, 0.0, sq), axis=1, keepdims=True) / HEAD_DIM
            inv = jnp.where(low, lax.rsqrt(msa + EPS), lax.rsqrt(msb + EPS))
            o_ref[rows, :] = (o * inv * g).astype(o_ref.dtype)

    load_queries(0)
    reset_state()
    idle_late()
    for chain in early:
        scores(chain, 0, True)

    @pl.loop(0, nq)
    def _(qi):
        @pl.loop(0, qi - 1)
        def _(j):
            consume(j, nxt=j + 1)

        @pl.when(qi > 0)
        def _():
            consume(qi - 1, nxt=qi, nxt_masked=True)

        @pl.when(qi < nq - 1)
        def _():
            consume(qi, cur_masked=True, nxt=0, final=True, before_next=lambda: load_queries(qi + 1))

        @pl.when(qi == nq - 1)
        def _():
            consume(qi, cur_masked=True, final=True)

        finalize(qi)
        reset_state()
        idle_late()


def _attention(diff, lambda_init, q, k, v, g, lamv):
    nq = SEQ // TQ
    kspec = lambda par: pl.BlockSpec((SEQ, LANES), lambda b, p: (b, 2 * p + par))
    return pl.pallas_call(
        functools.partial(_attn_kernel, diff, lambda_init),
        grid=(BATCH, 4),
        in_specs=[kspec(0), kspec(1), kspec(0), kspec(1),
                  pl.BlockSpec((None, v.shape[1] // 4, nq, v.shape[3], TQ), lambda b, p: (b, p, 0, 0, 0)),
                  pl.BlockSpec((1, LANES), lambda b, p: (0, 0)),
                  pl.BlockSpec((8, LANES), lambda b, p: (0, 0))],
        out_specs=pl.BlockSpec((SEQ, LANES), lambda b, p: (b, p)),
        out_shape=jax.ShapeDtypeStruct((N_TOK, DIFF_WIDTH), BF16),
        scratch_shapes=[pltpu.VMEM(shape, dt)
                        for shape, dt in (((LANES, ATTN_TQ), BF16), ((TQ, ATTN_TQ), F32),
                                          ((TQ, ATTN_TQ), BF16), ((1, ATTN_TQ), F32),
                                          ((1, ATTN_TQ), F32), ((v.shape[3], ATTN_TQ), F32))
                        for _ in range(2 * TQ // ATTN_TQ)],
        compiler_params=_params("arbitrary", "arbitrary"),
        name="diff_attention" if diff else "fox_attention",
    )(q, q, k, k, v, g, lamv)


def _outproj_kernel(x_ref, od_ref, of_ref, gt_ref, sc_ref, sh_ref, g_ref, wo_ref, wr_ref, br_ref,
                    x1_ref, h2_ref, rt_ref, cnt_ref, carry_ref):
    @pl.when(pl.program_id(0) == 0)
    def _():
        carry_ref[...] = jnp.zeros_like(carry_ref)

    mix = jnp.dot(od_ref[...], wo_ref[0:DIFF_WIDTH, :], preferred_element_type=F32)
    mix += jnp.dot(of_ref[...], wo_ref[DIFF_WIDTH:, :], preferred_element_type=F32)
    x1 = x_ref[...] + gt_ref[...] * mix
    x1_ref[...] = x1
    h = _rms_mod(x1, g_ref[...], sc_ref[...], sh_ref[...])
    hh = h.astype(BF16)
    _pack_planes(h, h2_ref)
    hl = (h - hh.astype(F32)).astype(BF16)
    r1 = jnp.dot(hh, wr_ref[...], preferred_element_type=F32)
    r2 = jnp.dot(hl, wr_ref[:, 0:LANES], preferred_element_type=F32)
    logits = r1[:, 0:LANES] + r1[:, LANES:] + r2 + br_ref[...]

    lane = _lane_iota((TM, LANES))
    lanef = lane.astype(F32)
    big = float(LANES)
    isg = lane < N_GROUPS
    lg = jnp.where(isg, logits, NEG)
    mg = jnp.max(lg, axis=1, keepdims=True)
    sg = jnp.sum(jnp.where(isg, jnp.exp(lg - mg), 0.0), axis=1, keepdims=True)
    p_g = 1.0 / sg
    gsel = jnp.min(jnp.where(isg & (lg == mg), lanef, big), axis=1, keepdims=True)
    lo = N_GROUPS + gsel * EXPERTS_PER_GROUP
    ise = (lanef >= lo) & (lanef < lo + EXPERTS_PER_GROUP)
    le = jnp.where(ise, logits, NEG)
    t1 = jnp.max(le, axis=1, keepdims=True)
    i1 = jnp.min(jnp.where(ise & (le == t1), lanef, big), axis=1, keepdims=True)
    ise2 = ise & (lanef != i1)
    le2 = jnp.where(ise2, logits, NEG)
    t2 = jnp.max(le2, axis=1, keepdims=True)
    i2 = jnp.min(jnp.where(ise2 & (le2 == t2), lanef, big), axis=1, keepdims=True)
    d = jnp.exp(t2 - t1)
    w1 = p_g / (1.0 + d)
    w2 = p_g * d / (1.0 + d)
    e1 = i1 - N_GROUPS
    e2 = i2 - N_GROUPS
    oh1 = lanef == e1
    oh2 = lanef == e2
    both = jnp.where(oh1 | oh2, 1.0, 0.0)
    row = lax.broadcasted_iota(jnp.int32, (TM, TM), 0)
    col = lax.broadcasted_iota(jnp.int32, (TM, TM), 1)
    before = jnp.dot((row > col).astype(BF16), both.astype(BF16), preferred_element_type=F32)
    before = before + carry_ref[0:1, :]
    rank1 = jnp.sum(jnp.where(oh1, before, 0.0), axis=1, keepdims=True)
    rank2 = jnp.sum(jnp.where(oh2, before, 0.0), axis=1, keepdims=True)
    total = carry_ref[0:1, :] + jnp.sum(both, axis=0, keepdims=True)
    carry_ref[...] = jnp.broadcast_to(total, carry_ref.shape)
    cnt_ref[...] = jnp.broadcast_to(total, cnt_ref.shape)
    vals = (e1, e2, w1, w2, rank1, rank2)
    out = jnp.zeros((TM, LANES), F32)
    for j, v in enumerate(vals):
        out = jnp.where(lane == j, v, out)
    rt_ref[...] = out


def _outproj(x, od, of, gt, sc, sh, g, wo_bf, wr, br):
    tpb = SEQ // TM
    row = pl.BlockSpec((TM, D_MODEL), lambda i: (i, 0))
    half = pl.BlockSpec((TM, DIFF_WIDTH), lambda i: (i, 0))
    per_batch = pl.BlockSpec((None, 1, D_MODEL), lambda i: (i // tpb, 0, 0))
    const = lambda shape: pl.BlockSpec(shape, lambda i: (0,) * len(shape))
    return pl.pallas_call(
        _outproj_kernel,
        grid=(N_TOK // TM,),
        in_specs=[row, half, half, per_batch, per_batch, per_batch, const((1, D_MODEL)),
                  const((D_MODEL, D_MODEL)), const((D_MODEL, 2 * LANES)), const((1, LANES))],
        out_specs=[row, pl.BlockSpec((N_PLANES, TM, LANES), lambda i: (0, i, 0)),
                   pl.BlockSpec((TM, LANES), lambda i: (i, 0)), const((8, LANES))],
        out_shape=[jax.ShapeDtypeStruct((N_TOK, D_MODEL), F32),
                   jax.ShapeDtypeStruct((N_PLANES, N_TOK, LANES), jnp.int32),
                   jax.ShapeDtypeStruct((N_TOK, LANES), F32),
                   jax.ShapeDtypeStruct((8, LANES), F32)],
        scratch_shapes=[pltpu.VMEM((8, LANES), F32)],
        compiler_params=_params("arbitrary"),
        name="outproj_router",
    )(x, od, of, gt, sc, sh, g.reshape(1, D_MODEL), wo_bf, wr, br)


def _expert_kernel(be_ref, cnt_ref, xs_ref, wg_ref, wu_ref, wd_ref, ys_ref, wg_sc, wu_sc, wd_sc):
    i = pl.program_id(0)
    cnt = cnt_ref[i]

    @pl.when((i == 0) | (be_ref[i] != be_ref[jnp.maximum(i - 1, 0)]))
    def _():
        wg_sc[...] = wg_ref[...].astype(BF16)
        wu_sc[...] = wu_ref[...].astype(BF16)
        wd_sc[...] = wd_ref[...].astype(BF16)

    @pl.when(cnt > 0)
    def _():
        live = lax.broadcasted_iota(jnp.int32, (MOE_BLOCK, LANES), 0) < cnt
        xb = _unpack_planes([jnp.where(live, xs_ref[p], 0) for p in range(N_PLANES)]).astype(BF16)
        a = jnp.dot(xb, wg_sc[...], preferred_element_type=F32)
        u = jnp.dot(xb, wu_sc[...], preferred_element_type=F32)
        hid = (a / (1.0 + jnp.exp(-a)) * u).astype(BF16)
        _pack_planes(jnp.dot(hid, wd_sc[...], preferred_element_type=F32), ys_ref)

    @pl.when(cnt == 0)
    def _():
        ys_ref[...] = jnp.zeros_like(ys_ref)


def _experts(layer, block_expert, block_count, xs, wg, wu, wd):
    planes = pl.BlockSpec((N_PLANES, MOE_BLOCK, LANES), lambda i, be, bc: (0, i, 0))
    w_in = pl.BlockSpec((None, None, D_MODEL, D_EXPERT), lambda i, be, bc: (layer, be[i], 0, 0))
    w_out = pl.BlockSpec((None, None, D_EXPERT, D_MODEL), lambda i, be, bc: (layer, be[i], 0, 0))
    grid_spec = pltpu.PrefetchScalarGridSpec(
        num_scalar_prefetch=2,
        grid=(MOE_NBLOCKS,),
        in_specs=[planes, w_in, w_in, w_out],
        out_specs=planes,
        scratch_shapes=[pltpu.VMEM((D_MODEL, D_EXPERT), BF16), pltpu.VMEM((D_MODEL, D_EXPERT), BF16),
                        pltpu.VMEM((D_EXPERT, D_MODEL), BF16)],
    )
    return pl.pallas_call(
        _expert_kernel,
        grid_spec=grid_spec,
        out_shape=jax.ShapeDtypeStruct((N_PLANES, MOE_ROWS, LANES), jnp.int32),
        compiler_params=_params("arbitrary"),
        name="expert_mlp",
    )(block_expert, block_count, xs, wg, wu, wd)


def _slots(route, counts):
    counts = counts[0, :N_EXPERTS].astype(jnp.int32)
    padded = ((counts + MOE_BLOCK - 1) // MOE_BLOCK) * MOE_BLOCK
    pend = jnp.cumsum(padded)
    pstart = pend - padded
    e = route[:, 0:TOP_K].astype(jnp.int32)
    rank = route[:, 2 * TOP_K:3 * TOP_K].astype(jnp.int32)
    onehot = e[:, :, None] == jnp.arange(N_EXPERTS, dtype=jnp.int32)
    dest = jnp.sum(jnp.where(onehot, pstart, 0), axis=-1) + rank
    bstart = jnp.arange(MOE_NBLOCKS, dtype=jnp.int32) * MOE_BLOCK
    block_expert = jnp.minimum(jnp.sum(bstart[:, None] >= pend[None, :], axis=1), N_EXPERTS - 1)
    block_expert = block_expert.astype(jnp.int32)
    block_count = jnp.clip(counts[block_expert] - (bstart - pstart[block_expert]), 0, MOE_BLOCK)
    block_count = jnp.where(bstart < pend[-1], block_count, 0).astype(jnp.int32)
    plane_off = jnp.arange(N_PLANES, dtype=jnp.int32) * MOE_ROWS
    rows = dest.T[:, None, :] + plane_off[None, :, None]
    return rows.astype(jnp.int32), block_expert, block_count


def _sc_workers():
    info = plsc.get_sparse_core_info()
    return info.num_cores, info.num_cores * info.num_subcores


def _sc_scatter2(src, idx_a, idx_b, out_rows):
    n_src = src.shape[0]
    nc, nw = _sc_workers()
    per_w = n_src // nw
    steps = per_w // SC_WINDOW
    mesh = plsc.VectorSubcoreMesh(core_axis_name="c", subcore_axis_name="s")

    @functools.partial(
        pl.kernel, mesh=mesh,
        out_type=jax.ShapeDtypeStruct((out_rows, LANES), src.dtype),
        scratch_types=[pltpu.VMEM((SC_WINDOW,), jnp.int32), pltpu.VMEM((SC_WINDOW,), jnp.int32),
                       pltpu.VMEM((SC_WINDOW, LANES), src.dtype)],
        name="sc_dispatch_scatter",
    )
    def k(src_hbm, ia_hbm, ib_hbm, out_hbm, ia_v, ib_v, rows_v):
        base = (lax.axis_index("s") * nc + lax.axis_index("c")) * per_w

        @pl.loop(0, steps)
        def _(j):
            off = base + j * SC_WINDOW
            pltpu.sync_copy(ia_hbm.at[pl.ds(off, SC_WINDOW)], ia_v)
            pltpu.sync_copy(ib_hbm.at[pl.ds(off, SC_WINDOW)], ib_v)
            pltpu.sync_copy(src_hbm.at[pl.ds(off, SC_WINDOW)], rows_v)
            pltpu.sync_copy(rows_v, out_hbm.at[ia_v])
            pltpu.sync_copy(rows_v, out_hbm.at[ib_v])

    return k(src, idx_a, idx_b)


def _sc_gather(table, idx):
    n_out = idx.shape[0]
    nc, nw = _sc_workers()
    per_w = n_out // nw
    steps = per_w // SC_WINDOW
    mesh = plsc.VectorSubcoreMesh(core_axis_name="c", subcore_axis_name="s")

    @functools.partial(
        pl.kernel, mesh=mesh,
        out_type=jax.ShapeDtypeStruct((n_out, LANES), table.dtype),
        scratch_types=[pltpu.VMEM((SC_WINDOW,), jnp.int32), pltpu.VMEM((SC_WINDOW, LANES), table.dtype)],
        name="sc_combine_gather",
    )
    def k(table_hbm, idx_hbm, out_hbm, idx_v, rows_v):
        base = (lax.axis_index("s") * nc + lax.axis_index("c")) * per_w

        @pl.loop(0, steps)
        def _(j):
            off = base + j * SC_WINDOW
            pltpu.sync_copy(idx_hbm.at[pl.ds(off, SC_WINDOW)], idx_v)
            pltpu.sync_copy(table_hbm.at[idx_v], rows_v)
            pltpu.sync_copy(rows_v, out_hbm.at[pl.ds(off, SC_WINDOW)])

    return k(table, idx)


def _final_kernel(x_ref, rt_ref, y_ref, gt_ref, g_ref, o_ref):
    x = x_ref[...] + gt_ref[...] * _combine(rt_ref, y_ref)
    ms = jnp.mean(x * x, axis=-1, keepdims=True)
    o_ref[...] = x * lax.rsqrt(ms + EPS) * g_ref[...]


def _final(x, moe, gate, g):
    tpb = SEQ // TM
    row = pl.BlockSpec((TM, D_MODEL), lambda i: (i, 0))
    return pl.pallas_call(
        _final_kernel,
        grid=(N_TOK // TM,),
        in_specs=[row, pl.BlockSpec((TM, LANES), lambda i: (i, 0)),
                  pl.BlockSpec((TOP_K, N_PLANES, TM, LANES), lambda i: (0, 0, i, 0)),
                  pl.BlockSpec((None, 1, D_MODEL), lambda i: (i // tpb, 0, 0)),
                  pl.BlockSpec((1, D_MODEL), lambda i: (0, 0))],
        out_specs=row,
        out_shape=jax.ShapeDtypeStruct((N_TOK, D_MODEL), F32),
        compiler_params=_params("arbitrary"),
        name="final_norm",
    )(x, moe[0], moe[1], gate, g.reshape(1, D_MODEL))


def kernel(x, c, positions, w_ada, b_ada, g_mix, w_in, b_forget, lambda_q1, lambda_k1, lambda_q2,
           lambda_k2, g_subln, g_fox_out, w_out, g_ffn, w_router_group, b_router_group,
           w_router_expert, b_router_expert, w_expert_gate, w_expert_up, w_expert_down, g_final):
    mod = _modulation(c, w_ada, b_ada)
    mod = mod.reshape(DEPTH, BATCH, 6, 1, D_MODEL)
    tables = _rope_tables(positions)
    pq = _forget_placement()
    xf = x.reshape(N_TOK, D_MODEL)
    moe = None
    gate = None
    for l in range(DEPTH):
        sh1, sc1, gt1, sh2, sc2, gt2 = (mod[l, :, j] for j in range(6))
        w_bf = jnp.pad(w_in[l], ((0, 0), (0, IN_COLS_PAD - IN_COLS))).astype(BF16)
        bfp = jnp.pad(b_forget[l], (0, LANES - N_FOX_HEADS)).reshape(1, LANES)
        xf, (dq, dk, dv, fq, fk, fv) = _inproj(xf, moe, gate, sc1, sh1, g_mix[l], w_bf, bfp, tables, pq)

        lambda_init = 0.8 - 0.6 * float(np.exp(-0.3 * l))
        lamv = jnp.zeros((8, LANES), F32).at[0:4, 0:HEAD_DIM].set(
            jnp.stack([lambda_q1[l], lambda_k1[l], lambda_q2[l], lambda_k2[l]]))
        g_d = g_subln[l].reshape(1, LANES)
        g_f = jnp.concatenate([g_fox_out[l], g_fox_out[l]]).reshape(1, LANES)
        od = _attention(True, lambda_init, dq, dk, dv, g_d, lamv)
        of = _attention(False, lambda_init, fq, fk, fv, g_f, lamv)

        wr32 = jnp.pad(jnp.concatenate([w_router_group[l], w_router_expert[l]], axis=1),
                       ((0, 0), (0, LANES - N_GROUPS - N_EXPERTS)))
        wr_hi = wr32.astype(BF16)
        wr_lo = (wr32 - wr_hi.astype(F32)).astype(BF16)
        wr = jnp.concatenate([wr_hi, wr_lo], axis=1)
        br = jnp.pad(jnp.concatenate([b_router_group[l], b_router_expert[l]]),
                     (0, LANES - N_GROUPS - N_EXPERTS)).reshape(1, LANES)
        xf, h2, route, counts = _outproj(xf, od, of, gt1, sc2, sh2, g_ffn[l], w_out[l].astype(BF16),
                                         wr, br)

        rows, block_expert, block_count = _slots(route, counts)
        xs = _sc_scatter2(h2.reshape(N_PLANES * N_TOK, LANES), rows[0].reshape(-1),
                          rows[1].reshape(-1), N_PLANES * MOE_ROWS)
        ys = _experts(l, block_expert, block_count, xs.reshape(N_PLANES, MOE_ROWS, LANES),
                      w_expert_gate, w_expert_up, w_expert_down)
        y2 = _sc_gather(ys.reshape(N_PLANES * MOE_ROWS, LANES), rows.reshape(-1))
        moe = (route, y2.reshape(TOP_K, N_PLANES, N_TOK, LANES))
        gate = gt2
    out = _final(xf, moe, gate, g_final)
    return out.reshape(BATCH, SEQ, D_MODEL)
```

```python
import functools

import numpy as np
import jax
import jax.numpy as jnp
from jax import lax
from jax.experimental import pallas as pl
from jax.experimental.pallas import tpu as pltpu
from jax.experimental.pallas import tpu_sc as plsc

D_MODEL = 1024
BATCH = 4
SEQ = 4096
DEPTH = 4
N_TOK = BATCH * SEQ

CHUNK = 64
HEAD_DIM = 64
N_DIFF_HEADS = 4
N_FOX_HEADS = 8
DIFF_WIDTH = 512
FOX_WIDTH = 512
IN_COLS = 3 * DIFF_WIDTH + 3 * FOX_WIDTH + N_FOX_HEADS
ROT_DIM = 16
ROPE_THETA = 500000.0
N_GROUPS = 4
EXPERTS_PER_GROUP = 8
N_EXPERTS = 32
TOP_K = 2
D_EXPERT = 512
EPS = 1e-6

LANES = 128
IN_COLS_PAD = 3200
FF_COL = 3 * DIFF_WIDTH + 3 * FOX_WIDTH
QK_WIDTH = 8 * LANES
TM = 512
TQ = 512
ATTN_TQ = 256
N_LATE_CHAINS = 1
ONES_ROWS = 16
ATTN_ROWS = 64
LOG2E = 1.4426950408889634
MOE_BLOCK = 256
MOE_ROWS = N_TOK * TOP_K + N_EXPERTS * MOE_BLOCK
MOE_NBLOCKS = MOE_ROWS // MOE_BLOCK
N_PLANES = D_MODEL // 2 // LANES
SC_WINDOW = 128
SC_INFLIGHT = 4
NEG = -1e30
VMEM_LIMIT = 56 * 1024 * 1024

F32 = jnp.float32
BF16 = jnp.bfloat16


def _bf16_round(x):
    return x.astype(BF16).astype(F32)


def _lane_iota(shape):
    return lax.broadcasted_iota(jnp.int32, shape, 1)


def _params(*sem):
    return pltpu.CompilerParams(dimension_semantics=sem, vmem_limit_bytes=VMEM_LIMIT)


def _pack_planes(y, o_ref):
    bits = lax.bitcast_convert_type(_bf16_round(y), jnp.uint32)
    half = D_MODEL // 2
    word = bits[:, half:] | lax.shift_right_logical(bits[:, :half], jnp.uint32(16))
    word = lax.bitcast_convert_type(word, jnp.int32)
    for p in range(N_PLANES):
        o_ref[p] = word[:, p * LANES:(p + 1) * LANES]


def _unpack_planes(planes):
    lo, hi = [], []
    for w in planes:
        u = lax.bitcast_convert_type(w, jnp.uint32)
        lo.append(lax.bitcast_convert_type(lax.shift_left(u, jnp.uint32(16)), F32))
        hi.append(lax.bitcast_convert_type(u & jnp.uint32(0xFFFF0000), F32))
    return jnp.concatenate(lo + hi, axis=1)


def _combine(route_ref, y_ref):
    rt = route_ref[...]
    y0 = _unpack_planes([y_ref[0, p] for p in range(N_PLANES)])
    y1 = _unpack_planes([y_ref[1, p] for p in range(N_PLANES)])
    return rt[:, 2:3] * y0 + rt[:, 3:4] * y1


def _mod_kernel(c_ref, w_ref, b_ref, o_ref):
    c = c_ref[...]
    cond = c / (1.0 + jnp.exp(-c))
    ch = cond.astype(BF16)
    cl = (cond - ch.astype(F32)).astype(BF16)
    w = w_ref[...]
    wh = w.astype(BF16)
    wl = (w - wh.astype(F32)).astype(BF16)
    acc = jnp.dot(ch, wh, preferred_element_type=F32)
    acc += jnp.dot(cl, wh, preferred_element_type=F32)
    acc += jnp.dot(ch, wl, preferred_element_type=F32)
    o_ref[...] = acc + b_ref[...]


def _modulation(c, w_ada, b_ada):
    rows = 16
    tn = 1536
    c_pad = jnp.zeros((rows, D_MODEL), F32).at[:BATCH].set(c)
    out = pl.pallas_call(
        _mod_kernel,
        grid=(DEPTH, 6 * D_MODEL // tn),
        in_specs=[
            pl.BlockSpec((rows, D_MODEL), lambda l, n: (0, 0)),
            pl.BlockSpec((None, D_MODEL, tn), lambda l, n: (l, 0, n)),
            pl.BlockSpec((None, 1, tn), lambda l, n: (l, 0, n)),
        ],
        out_specs=pl.BlockSpec((None, rows, tn), lambda l, n: (l, 0, n)),
        out_shape=jax.ShapeDtypeStruct((DEPTH, rows, 6 * D_MODEL), F32),
        compiler_params=_params("arbitrary", "arbitrary"),
        name="adaln_mod",
    )(c_pad, w_ada, b_ada.reshape(DEPTH, 1, 6 * D_MODEL))
    return out[:, :BATCH]


def _rope_kernel(pos_ref, inv_ref, c_ref, sa_ref, sb_ref):
    ang = pos_ref[...].astype(F32) * inv_ref[...]
    j = _lane_iota(ang.shape) % HEAD_DIM
    cosv = jnp.cos(ang)
    sinv = jnp.sin(ang)
    half = ROT_DIM // 2
    c_ref[...] = jnp.where(j < ROT_DIM, cosv, 1.0)
    sa_ref[...] = jnp.where(j < half, -sinv, 0.0)
    sb_ref[...] = jnp.where((j >= half) & (j < ROT_DIM), sinv, 0.0)


def _rope_tables(positions):
    half = ROT_DIM // 2
    inv = ROPE_THETA ** (-jnp.arange(0, ROT_DIM, 2, dtype=F32) / ROT_DIM)
    lane = np.arange(LANES)
    inv_lane = inv[(lane % HEAD_DIM) % half].reshape(1, LANES)
    spec = pl.BlockSpec((TM, LANES), lambda i: (i, 0))
    shape = jax.ShapeDtypeStruct((N_TOK, LANES), F32)
    return pl.pallas_call(
        _rope_kernel,
        grid=(N_TOK // TM,),
        in_specs=[pl.BlockSpec((TM, 1), lambda i: (i, 0)),
                  pl.BlockSpec((1, LANES), lambda i: (0, 0))],
        out_specs=[spec, spec, spec],
        out_shape=[shape, shape, shape],
        compiler_params=_params("arbitrary"),
        name="rope_tables",
    )(positions.reshape(N_TOK, 1), inv_lane)


def _rms_mod(x, g, sc, sh):
    ms = jnp.mean(x * x, axis=-1, keepdims=True)
    return (x * lax.rsqrt(ms + EPS) * g) * (1.0 + sc) + sh


def _inproj_kernel(fuse, *refs):
    if fuse:
        (x_ref, rt_ref, y_ref, gt_ref, sc_ref, sh_ref, g_ref, w_ref, bf_ref, c_ref, sa_ref, sb_ref,
         pq_ref, xo_ref, dq_ref, dk_ref, dv_ref, fq_ref, fk_ref, fv_ref, carry_ref) = refs
        x = x_ref[...] + gt_ref[...] * _combine(rt_ref, y_ref)
        xo_ref[...] = x
    else:
        (x_ref, sc_ref, sh_ref, g_ref, w_ref, bf_ref, c_ref, sa_ref, sb_ref,
         pq_ref, dq_ref, dk_ref, dv_ref, fq_ref, fk_ref, fv_ref, carry_ref) = refs
        x = x_ref[...]
    hb = _rms_mod(x, g_ref[...], sc_ref[...], sh_ref[...]).astype(BF16)

    @pl.when(pl.program_id(0) % (SEQ // TM) == 0)
    def _():
        carry_ref[...] = jnp.zeros_like(carry_ref)

    lane = _lane_iota((TM, LANES))
    nh = N_FOX_HEADS

    def pack3(a):
        hi = _bf16_round(a)
        r1 = a - hi
        mid = _bf16_round(r1)
        lo = _bf16_round(r1 - mid)
        return jnp.where(lane < nh, hi,
                         jnp.where(lane < 2 * nh, pltpu.roll(mid, nh, 1),
                                   jnp.where(lane < 3 * nh, pltpu.roll(lo, 2 * nh, 1), 0.0)))

    z = jnp.dot(hb, w_ref[:, FF_COL:FF_COL + LANES], preferred_element_type=F32) + bf_ref[...]

    low = lane < HEAD_DIM
    rc, rsa, rsb = c_ref[...], sa_ref[...], sb_ref[...]
    scale = HEAD_DIM ** -0.5 * LOG2E

    def split_store(chunk, o_ref, m, extra_a=None, extra_b=None):
        a = jnp.where(low, chunk, 0.0)
        b = jnp.where(low, pltpu.roll(chunk, HEAD_DIM, 1), 0.0)
        if extra_a is not None:
            a = a + extra_a
            b = b + extra_b
        o_ref[:, (2 * m) * LANES:(2 * m + 1) * LANES] = a.astype(BF16)
        o_ref[:, (2 * m + 1) * LANES:(2 * m + 2) * LANES] = b.astype(BF16)

    def rope(xc):
        return xc * rc + pltpu.roll(xc, LANES - ROT_DIM // 2, 1) * rsa + pltpu.roll(xc, ROT_DIM // 2, 1) * rsb

    pdq = jnp.dot(hb, w_ref[:, 0:DIFF_WIDTH], preferred_element_type=F32)
    for m in range(N_DIFF_HEADS):
        split_store(rope(pdq[:, m * LANES:(m + 1) * LANES]) * scale, dq_ref, m)
    pdk = jnp.dot(hb, w_ref[:, DIFF_WIDTH:2 * DIFF_WIDTH], preferred_element_type=F32)
    for m in range(N_DIFF_HEADS):
        split_store(rope(pdk[:, m * LANES:(m + 1) * LANES]), dk_ref, m)
    def store_values_t(pv, o_ref, width):
        ones = jnp.ones((ONES_ROWS, TM), BF16)
        for m in range(4):
            vt = pv[:, m * LANES:(m + 1) * LANES].T.astype(BF16)
            for i in range(LANES // width):
                o_ref[m * (LANES // width) + i, 0:width, :] = vt[i * width:(i + 1) * width]
                o_ref[m * (LANES // width) + i, width:width + ONES_ROWS, :] = ones

    store_values_t(jnp.dot(hb, w_ref[:, 2 * DIFF_WIDTH:3 * DIFF_WIDTH], preferred_element_type=F32),
                   dv_ref, 2 * HEAD_DIM)
    o = 3 * DIFF_WIDTH
    store_values_t(jnp.dot(hb, w_ref[:, o + 2 * FOX_WIDTH:o + 3 * FOX_WIDTH],
                           preferred_element_type=F32), fv_ref, HEAD_DIM)

    logf =jnp.minimum(z, 0.0) - jnp.log(1.0 + jnp.exp(-jnp.abs(z)))
    logf = jnp.where(lane < nh, logf, 0.0)
    row = lax.broadcasted_iota(jnp.int32, (TM, TM), 0)
    col = lax.broadcasted_iota(jnp.int32, (TM, TM), 1)
    tri = (row >= col).astype(BF16)
    r = jnp.dot(tri, pack3(logf).astype(BF16), preferred_element_type=F32)
    cs = r + pltpu.roll(r, LANES - nh, 1) + pltpu.roll(r, LANES - 2 * nh, 1)
    cf = jnp.where(lane < nh, cs + carry_ref[0:1, :], 0.0)
    carry_ref[...] = jnp.broadcast_to(cf[TM - 1:TM, :], carry_ref.shape)

    t3 = jnp.where(lane == 3 * nh, 1.0, pack3(cf * LOG2E)).astype(BF16)
    aug = jnp.dot(t3, pq_ref[...], preferred_element_type=F32)

    pfq =jnp.dot(hb, w_ref[:, o:o + FOX_WIDTH], preferred_element_type=F32)
    for m in range(N_FOX_HEADS // 2):
        split_store(pfq[:, m * LANES:(m + 1) * LANES] * scale, fq_ref, m,
                    aug[:, (2 * m) * LANES:(2 * m + 1) * LANES],
                    aug[:, (2 * m + 1) * LANES:(2 * m + 2) * LANES])
    pfk = jnp.dot(hb, w_ref[:, o + FOX_WIDTH:o + 2 * FOX_WIDTH], preferred_element_type=F32)
    for m in range(N_FOX_HEADS // 2):
        split_store(pfk[:, m * LANES:(m + 1) * LANES], fk_ref, m,
                    aug[:, QK_WIDTH + (2 * m) * LANES:QK_WIDTH + (2 * m + 1) * LANES],
                    aug[:, QK_WIDTH + (2 * m + 1) * LANES:QK_WIDTH + (2 * m + 2) * LANES])


def _forget_placement():
    nh = N_FOX_HEADS
    p = np.zeros((LANES, 2 * QK_WIDTH), np.float32)
    for h in range(nh):
        base_q = h * LANES + HEAD_DIM
        base_k = QK_WIDTH + h * LANES + HEAD_DIM
        for part in range(3):
            p[part * nh + h, base_q + part] = 1.0
            p[3 * nh, base_q + 3 + part] = 1.0
            p[3 * nh, base_k + part] = 1.0
            p[part * nh + h, base_k + 3 + part] = -1.0
    return jnp.asarray(p, BF16)


def _inproj(x, moe, gate, sc, sh, g, w_bf, b_forget, tables, pq):
    fuse = moe is not None
    tpb = SEQ // TM
    row = pl.BlockSpec((TM, D_MODEL), lambda i: (i, 0))
    per_batch = pl.BlockSpec((None, 1, D_MODEL), lambda i: (i // tpb, 0, 0))
    const = lambda shape: pl.BlockSpec(shape, lambda i: (0,) * len(shape))
    tab = pl.BlockSpec((TM, LANES), lambda i: (i, 0))
    in_specs = [row]
    args = [x]
    if fuse:
        in_specs += [tab, pl.BlockSpec((TOP_K, N_PLANES, TM, LANES), lambda i: (0, 0, i, 0)), per_batch]
        args += [moe[0], moe[1], gate]
    in_specs += [per_batch, per_batch, const((1, D_MODEL)), const((D_MODEL, IN_COLS_PAD)),
                 const((1, LANES)), tab, tab, tab, const((LANES, 2 * QK_WIDTH))]
    args += [sc, sh, g.reshape(1, D_MODEL), w_bf, b_forget, *tables, pq]
    wide = pl.BlockSpec((TM, QK_WIDTH), lambda i: (i, 0))
    def vspec(heads, width):
        rows = width + ONES_ROWS
        return (pl.BlockSpec((None, heads, None, rows, TM), lambda i: (i // tpb, 0, i % tpb, 0, 0)),
                jax.ShapeDtypeStruct((BATCH, heads, tpb, rows, TM), BF16))

    wide_s = jax.ShapeDtypeStruct((N_TOK, QK_WIDTH), BF16)
    dv_spec, dv_s = vspec(N_DIFF_HEADS, 2 * HEAD_DIM)
    fv_spec, fv_s = vspec(N_FOX_HEADS, HEAD_DIM)
    out_specs = [wide, wide, dv_spec, wide, wide, fv_spec]
    out_shape = [wide_s, wide_s, dv_s, wide_s, wide_s, fv_s]
    if fuse:
        out_specs = [row] + out_specs
        out_shape = [jax.ShapeDtypeStruct((N_TOK, D_MODEL), F32)] + out_shape
    outs = pl.pallas_call(
        functools.partial(_inproj_kernel, fuse),
        grid=(N_TOK // TM,),
        in_specs=in_specs,
        out_specs=out_specs,
        out_shape=out_shape,
        scratch_shapes=[pltpu.VMEM((8, LANES), F32)],
        compiler_params=_params("arbitrary"),
        name="norm_inproj",
    )(*args)
    if fuse:
        return outs[0], outs[1:]
    return x, outs


def _attn_kernel(diff, lambda_init, qa_ref, qb_ref, ka_ref, kb_ref, v_ref, g_ref, lam_ref, o_ref,
                 *scratch):
    nq = SEQ // TQ
    n_half = TQ // ATTN_TQ
    feat = 2 * HEAD_DIM if diff else HEAD_DIM
    chains = []
    for mi, (q_ref, k_ref) in enumerate(((qa_ref, ka_ref), (qb_ref, kb_ref))):
        for h in range(n_half):
            c = mi * n_half + h
            qt_sc, s_sc, p_sc, m_sc, a_sc, acc_sc = scratch[c::2 * n_half]
            vh = 0 if diff else mi
            chains.append((h, k_ref, qt_sc, s_sc, p_sc, m_sc, a_sc, acc_sc, q_ref, vh))
    order = [chains[mi * n_half + h] for h in range(n_half) for mi in range(2)]
    early, late = order[:-N_LATE_CHAINS], order[-N_LATE_CHAINS:]

    def load_queries(qi):
        for mi in range(2):
            q_ref = chains[mi * n_half][8]
            qt = q_ref[pl.ds(pl.multiple_of(qi * TQ, TQ), TQ), :].astype(F32).T.astype(BF16)
            for h in range(n_half):
                chains[mi * n_half + h][2][...] = qt[:, h * ATTN_TQ:(h + 1) * ATTN_TQ]

    def reset_state():
        for chain in chains:
            m_sc, _, acc_sc = chain[5:8]
            m_sc[...] = jnp.full(m_sc.shape, NEG, F32)
            acc_sc[...] = jnp.zeros(acc_sc.shape, F32)

    def n_keys(chain, masked):
        return (chain[0] + 1) * ATTN_TQ if masked else TQ

    def scores(chain, j, masked):
        h, k_ref, qt_sc, s_sc = chain[:4]
        nk = n_keys(chain, masked)
        off = pl.multiple_of(j * TQ, TQ)
        s = jnp.dot(k_ref[pl.ds(off, nk), :], qt_sc[...], preferred_element_type=F32)
        if masked:
            kk = lax.broadcasted_iota(jnp.int32, (nk, ATTN_TQ), 0)
            qq = h * ATTN_TQ + lax.broadcasted_iota(jnp.int32, (nk, ATTN_TQ), 1)
            s = jnp.where((kk // CHUNK <= qq // CHUNK) if diff else (kk <= qq), s, NEG)
        s_sc[0:nk, :] = s

    def softmax(chain, masked):
        s_sc, p_sc, m_sc, a_sc = chain[3:7]
        nk = n_keys(chain, masked)
        m_all = m_sc[...]
        m_parts = []
        for c0 in range(0, ATTN_TQ, LANES):
            cols = slice(c0, c0 + LANES)
            pm = s_sc[0:ATTN_ROWS, cols]
            for r0 in range(ATTN_ROWS, nk, ATTN_ROWS):
                pm = jnp.maximum(pm, s_sc[r0:r0 + ATTN_ROWS, cols])
            m_new = jnp.maximum(m_all[:, cols], jnp.max(pm, axis=0, keepdims=True))
            for r0 in range(0, nk, ATTN_ROWS):
                p = jnp.exp2(s_sc[r0:r0 + ATTN_ROWS, cols] - m_new)
                p_sc[r0:r0 + ATTN_ROWS, cols] = p.astype(BF16)
            m_parts.append(m_new)
        m_new = jnp.concatenate(m_parts, axis=1)
        a_sc[...] = jnp.exp2(m_all - m_new)
        m_sc[...] = m_new

    def values(chain, j, masked=False):
        p_sc, a_sc, acc_sc, vh = chain[4], chain[6], chain[7], chain[9]
        nk = n_keys(chain, masked)
        pv = jnp.dot(v_ref[vh, j, :, 0:nk], p_sc[0:nk, :], preferred_element_type=F32)
        acc_sc[...] = a_sc[...] * acc_sc[...] + pv

    def idle_late():
        for chain in late:
            chain[4][...] = jnp.zeros(chain[4].shape, BF16)
            chain[6][...] = jnp.ones(chain[6].shape, F32)

    def consume(j, cur_masked=False, nxt=None, nxt_masked=False, final=False, before_next=None):
        def open_late(chain):
            scores(chain, j, cur_masked)
            values(chain, jnp.maximum(j - 1, 0))

        open_late(late[0])
        for i, chain in enumerate(early):
            softmax(chain, cur_masked)
            if i == 0:
                for other in late[1:]:
                    open_late(other)
                if before_next is not None:
                    before_next()
            if nxt is not None:
                scores(chain, nxt, nxt_masked)
            values(chain, j, cur_masked)
        for chain in late:
            softmax(chain, cur_masked)
        if final:
            for chain in late:
                values(chain, j, cur_masked)

    def finalize(qi):
        ot = [jnp.concatenate([chains[mi * n_half + h][7][0:feat] / chains[mi * n_half + h][7][feat:feat + 1]
                               for h in range(n_half)], axis=1) for mi in range(2)]
        g = g_ref[...]
        rows = pl.ds(pl.multiple_of(qi * TQ, TQ), TQ)
        if diff:
            lv = lam_ref[...]
            lam = (jnp.exp(jnp.sum(lv[0:1] * lv[1:2], axis=1, keepdims=True))
                   - jnp.exp(jnp.sum(lv[2:3] * lv[3:4], axis=1, keepdims=True)) + lambda_init)
            o = (ot[0] - lam * ot[1]).T
            y = o * lax.rsqrt(jnp.mean(o * o, axis=1, keepdims=True) + EPS) * g
            o_ref[rows, :] = (y * (1.0 - lambda_init)).astype(o_ref.dtype)
        else:
            o = jnp.concatenate(ot, axis=0).T
            low = _lane_iota((TQ, LANES)) < HEAD_DIM
            sq = o * o
            msa = jnp.sum(jnp.where(low, sq, 0.0), axis=1, keepdims=True) / HEAD_DIM
            msb = jnp.sum(jnp.where(low, 0.0, sq), axis=1, keepdims=True) / HEAD_DIM
            inv = jnp.where(low, lax.rsqrt(msa + EPS), lax.rsqrt(msb + EPS))
            o_ref[rows, :] = (o * inv * g).astype(o_ref.dtype)

    load_queries(0)
    reset_state()
    idle_late()
    for chain in early:
        scores(chain, 0, True)

    @pl.loop(0, nq)
    def _(qi):
        n_plain = jnp.maximum(qi - 1, 0)

        @pl.loop(0, n_plain // 2)
        def _(t):
            consume(2 * t, nxt=2 * t + 1)
            consume(2 * t + 1, nxt=2 * t + 2)

        @pl.when(n_plain % 2 == 1)
        def _():
            consume(qi - 2, nxt=qi - 1)

        def last_blocks(to_next_tile):
            if to_next_tile:
                consume(qi, cur_masked=True, nxt=0, final=True, before_next=lambda: load_queries(qi + 1))
            else:
                consume(qi, cur_masked=True, final=True)

        for to_next_tile in (True, False):
            more = (qi < nq - 1) if to_next_tile else (qi == nq - 1)

            @pl.when(more & (qi > 0))
            def _():
                consume(qi - 1, nxt=qi, nxt_masked=True)
                last_blocks(to_next_tile)

            if to_next_tile:
                @pl.when(qi == 0)
                def _():
                    last_blocks(to_next_tile)

        finalize(qi)
        reset_state()
        idle_late()


def _attention(diff, lambda_init, q, k, v, g, lamv):
    nq = SEQ // TQ
    kspec = lambda par: pl.BlockSpec((SEQ, LANES), lambda b, p: (b, 2 * p + par))
    return pl.pallas_call(
        functools.partial(_attn_kernel, diff, lambda_init),
        grid=(BATCH, 4),
        in_specs=[kspec(0), kspec(1), kspec(0), kspec(1),
                  pl.BlockSpec((None, v.shape[1] // 4, nq, v.shape[3], TQ), lambda b, p: (b, p, 0, 0, 0)),
                  pl.BlockSpec((1, LANES), lambda b, p: (0, 0)),
                  pl.BlockSpec((8, LANES), lambda b, p: (0, 0))],
        out_specs=pl.BlockSpec((SEQ, LANES), lambda b, p: (b, p)),
        out_shape=jax.ShapeDtypeStruct((N_TOK, DIFF_WIDTH), BF16),
        scratch_shapes=[pltpu.VMEM(shape, dt)
                        for shape, dt in (((LANES, ATTN_TQ), BF16), ((TQ, ATTN_TQ), F32),
                                          ((TQ, ATTN_TQ), BF16), ((1, ATTN_TQ), F32),
                                          ((1, ATTN_TQ), F32), ((v.shape[3], ATTN_TQ), F32))
                        for _ in range(2 * TQ // ATTN_TQ)],
        compiler_params=_params("arbitrary", "arbitrary"),
        name="diff_attention" if diff else "fox_attention",
    )(q, q, k, k, v, g, lamv)


def _outproj_kernel(x_ref, od_ref, of_ref, gt_ref, sc_ref, sh_ref, g_ref, wo_ref, wr_ref, br_ref,
                    x1_ref, h2_ref, rt_ref, cnt_ref, carry_ref):
    @pl.when(pl.program_id(0) == 0)
    def _():
        carry_ref[...] = jnp.zeros_like(carry_ref)

    mix = jnp.dot(od_ref[...], wo_ref[0:DIFF_WIDTH, :], preferred_element_type=F32)
    mix += jnp.dot(of_ref[...], wo_ref[DIFF_WIDTH:, :], preferred_element_type=F32)
    x1 = x_ref[...] + gt_ref[...] * mix
    x1_ref[...] = x1
    h = _rms_mod(x1, g_ref[...], sc_ref[...], sh_ref[...])
    hh = h.astype(BF16)
    _pack_planes(h, h2_ref)
    hl = (h - hh.astype(F32)).astype(BF16)
    r1 = jnp.dot(hh, wr_ref[...], preferred_element_type=F32)
    r2 = jnp.dot(hl, wr_ref[:, 0:LANES], preferred_element_type=F32)
    logits = r1[:, 0:LANES] + r1[:, LANES:] + r2 + br_ref[...]

    lane = _lane_iota((TM, LANES))
    lanef = lane.astype(F32)
    big = float(LANES)
    isg = lane < N_GROUPS
    lg = jnp.where(isg, logits, NEG)
    mg = jnp.max(lg, axis=1, keepdims=True)
    sg = jnp.sum(jnp.where(isg, jnp.exp(lg - mg), 0.0), axis=1, keepdims=True)
    p_g = 1.0 / sg
    gsel = jnp.min(jnp.where(isg & (lg == mg), lanef, big), axis=1, keepdims=True)
    lo = N_GROUPS + gsel * EXPERTS_PER_GROUP
    ise = (lanef >= lo) & (lanef < lo + EXPERTS_PER_GROUP)
    le = jnp.where(ise, logits, NEG)
    t1 = jnp.max(le, axis=1, keepdims=True)
    i1 = jnp.min(jnp.where(ise & (le == t1), lanef, big), axis=1, keepdims=True)
    ise2 = ise & (lanef != i1)
    le2 = jnp.where(ise2, logits, NEG)
    t2 = jnp.max(le2, axis=1, keepdims=True)
    i2 = jnp.min(jnp.where(ise2 & (le2 == t2), lanef, big), axis=1, keepdims=True)
    d = jnp.exp(t2 - t1)
    w1 = p_g / (1.0 + d)
    w2 = p_g * d / (1.0 + d)
    e1 = i1 - N_GROUPS
    e2 = i2 - N_GROUPS
    oh1 = lanef == e1
    oh2 = lanef == e2
    both = jnp.where(oh1 | oh2, 1.0, 0.0)
    row = lax.broadcasted_iota(jnp.int32, (TM, TM), 0)
    col = lax.broadcasted_iota(jnp.int32, (TM, TM), 1)
    before = jnp.dot((row > col).astype(BF16), both.astype(BF16), preferred_element_type=F32)
    before = before + carry_ref[0:1, :]
    rank1 = jnp.sum(jnp.where(oh1, before, 0.0), axis=1, keepdims=True)
    rank2 = jnp.sum(jnp.where(oh2, before, 0.0), axis=1, keepdims=True)
    total = carry_ref[0:1, :] + jnp.sum(both, axis=0, keepdims=True)
    carry_ref[...] = jnp.broadcast_to(total, carry_ref.shape)
    cnt_ref[...] = jnp.broadcast_to(total, cnt_ref.shape)
    vals = (e1, e2, w1, w2, rank1, rank2)
    out = jnp.zeros((TM, LANES), F32)
    for j, v in enumerate(vals):
        out = jnp.where(lane == j, v, out)
    rt_ref[...] = out


def _outproj(x, od, of, gt, sc, sh, g, wo_bf, wr, br):
    tpb = SEQ // TM
    row = pl.BlockSpec((TM, D_MODEL), lambda i: (i, 0))
    half = pl.BlockSpec((TM, DIFF_WIDTH), lambda i: (i, 0))
    per_batch = pl.BlockSpec((None, 1, D_MODEL), lambda i: (i // tpb, 0, 0))
    const = lambda shape: pl.BlockSpec(shape, lambda i: (0,) * len(shape))
    return pl.pallas_call(
        _outproj_kernel,
        grid=(N_TOK // TM,),
        in_specs=[row, half, half, per_batch, per_batch, per_batch, const((1, D_MODEL)),
                  const((D_MODEL, D_MODEL)), const((D_MODEL, 2 * LANES)), const((1, LANES))],
        out_specs=[row, pl.BlockSpec((N_PLANES, TM, LANES), lambda i: (0, i, 0)),
                   pl.BlockSpec((TM, LANES), lambda i: (i, 0)), const((8, LANES))],
        out_shape=[jax.ShapeDtypeStruct((N_TOK, D_MODEL), F32),
                   jax.ShapeDtypeStruct((N_PLANES, N_TOK, LANES), jnp.int32),
                   jax.ShapeDtypeStruct((N_TOK, LANES), F32),
                   jax.ShapeDtypeStruct((8, LANES), F32)],
        scratch_shapes=[pltpu.VMEM((8, LANES), F32)],
        compiler_params=_params("arbitrary"),
        name="outproj_router",
    )(x, od, of, gt, sc, sh, g.reshape(1, D_MODEL), wo_bf, wr, br)


def _expert_kernel(be_ref, cnt_ref, xs_ref, wg_ref, wu_ref, wd_ref, ys_ref, wg_sc, wu_sc, wd_sc):
    i = pl.program_id(0)
    cnt = cnt_ref[i]

    @pl.when((i == 0) | (be_ref[i] != be_ref[jnp.maximum(i - 1, 0)]))
    def _():
        wg_sc[...] = wg_ref[...].astype(BF16)
        wu_sc[...] = wu_ref[...].astype(BF16)
        wd_sc[...] = wd_ref[...].astype(BF16)

    @pl.when(cnt > 0)
    def _():
        live = lax.broadcasted_iota(jnp.int32, (MOE_BLOCK, LANES), 0) < cnt
        xb = _unpack_planes([jnp.where(live, xs_ref[p], 0) for p in range(N_PLANES)]).astype(BF16)
        a = jnp.dot(xb, wg_sc[...], preferred_element_type=F32)
        u = jnp.dot(xb, wu_sc[...], preferred_element_type=F32)
        hid = (a / (1.0 + jnp.exp(-a)) * u).astype(BF16)
        _pack_planes(jnp.dot(hid, wd_sc[...], preferred_element_type=F32), ys_ref)

    @pl.when(cnt == 0)
    def _():
        ys_ref[...] = jnp.zeros_like(ys_ref)


def _experts(layer, block_expert, block_count, xs, wg, wu, wd):
    planes = pl.BlockSpec((N_PLANES, MOE_BLOCK, LANES), lambda i, be, bc: (0, i, 0))
    w_in = pl.BlockSpec((None, None, D_MODEL, D_EXPERT), lambda i, be, bc: (layer, be[i], 0, 0))
    w_out = pl.BlockSpec((None, None, D_EXPERT, D_MODEL), lambda i, be, bc: (layer, be[i], 0, 0))
    grid_spec = pltpu.PrefetchScalarGridSpec(
        num_scalar_prefetch=2,
        grid=(MOE_NBLOCKS,),
        in_specs=[planes, w_in, w_in, w_out],
        out_specs=planes,
        scratch_shapes=[pltpu.VMEM((D_MODEL, D_EXPERT), BF16), pltpu.VMEM((D_MODEL, D_EXPERT), BF16),
                        pltpu.VMEM((D_EXPERT, D_MODEL), BF16)],
    )
    return pl.pallas_call(
        _expert_kernel,
        grid_spec=grid_spec,
        out_shape=jax.ShapeDtypeStruct((N_PLANES, MOE_ROWS, LANES), jnp.int32),
        compiler_params=_params("arbitrary"),
        name="expert_mlp",
    )(block_expert, block_count, xs, wg, wu, wd)


def _slots(route, counts):
    counts = counts[0, :N_EXPERTS].astype(jnp.int32)
    padded = ((counts + MOE_BLOCK - 1) // MOE_BLOCK) * MOE_BLOCK
    pend = jnp.cumsum(padded)
    pstart = pend - padded
    e = route[:, 0:TOP_K].astype(jnp.int32)
    rank = route[:, 2 * TOP_K:3 * TOP_K].astype(jnp.int32)
    dest = jnp.take(pstart, e, axis=0, mode="clip") + rank
    bstart = jnp.arange(MOE_NBLOCKS, dtype=jnp.int32) * MOE_BLOCK
    block_expert = jnp.minimum(jnp.sum(bstart[:, None] >= pend[None, :], axis=1), N_EXPERTS - 1)
    block_expert = block_expert.astype(jnp.int32)
    block_count = jnp.clip(counts[block_expert] - (bstart - pstart[block_expert]), 0, MOE_BLOCK)
    block_count = jnp.where(bstart < pend[-1], block_count, 0).astype(jnp.int32)
    plane_off = jnp.arange(N_PLANES, dtype=jnp.int32) * MOE_ROWS
    rows = dest.T[:, None, :] + plane_off[None, :, None]
    return rows.astype(jnp.int32), block_expert, block_count


def _sc_workers():
    info = plsc.get_sparse_core_info()
    return info.num_cores, info.num_cores * info.num_subcores


def _sc_scatter2(src, idx_a, idx_b, out_rows):
    n_src = src.shape[0]
    nc, nw = _sc_workers()
    per_w = n_src // nw
    steps = per_w // SC_WINDOW
    mesh = plsc.VectorSubcoreMesh(core_axis_name="c", subcore_axis_name="s")

    @functools.partial(
        pl.kernel, mesh=mesh,
        out_type=jax.ShapeDtypeStruct((out_rows, LANES), src.dtype),
        scratch_types=[pltpu.VMEM((SC_WINDOW,), jnp.int32), pltpu.VMEM((SC_WINDOW,), jnp.int32),
                       pltpu.VMEM((SC_WINDOW, LANES), src.dtype)],
        name="sc_dispatch_scatter",
    )
    def k(src_hbm, ia_hbm, ib_hbm, out_hbm, ia_v, ib_v, rows_v):
        base = (lax.axis_index("s") * nc + lax.axis_index("c")) * per_w

        @pl.loop(0, steps)
        def _(j):
            off = base + j * SC_WINDOW
            pltpu.sync_copy(ia_hbm.at[pl.ds(off, SC_WINDOW)], ia_v)
            pltpu.sync_copy(ib_hbm.at[pl.ds(off, SC_WINDOW)], ib_v)
            pltpu.sync_copy(src_hbm.at[pl.ds(off, SC_WINDOW)], rows_v)
            pltpu.sync_copy(rows_v, out_hbm.at[ia_v])
            pltpu.sync_copy(rows_v, out_hbm.at[ib_v])

    return k(src, idx_a, idx_b)


def _sc_gather(table, idx):
    n_out = idx.shape[0]
    nc, nw = _sc_workers()
    steps = n_out // nw // SC_WINDOW
    mesh = plsc.VectorSubcoreMesh(core_axis_name="c", subcore_axis_name="s")

    @functools.partial(
        pl.kernel, mesh=mesh,
        out_type=jax.ShapeDtypeStruct((n_out, LANES), table.dtype),
        scratch_types=[pltpu.VMEM((steps, SC_WINDOW), jnp.int32),
                       pltpu.VMEM((SC_INFLIGHT, SC_WINDOW, LANES), table.dtype),
                       pltpu.SemaphoreType.DMA((SC_INFLIGHT,)), pltpu.SemaphoreType.DMA((SC_INFLIGHT,))],
        name="sc_combine_gather",
    )
    def k(table_hbm, idx_hbm, out_hbm, idx_v, rows_v, gsem, wsem):
        first = (lax.axis_index("s") * nc + lax.axis_index("c")) * steps
        pltpu.sync_copy(idx_hbm.at[pl.ds(first, steps)], idx_v)

        @pl.loop(0, steps, step=SC_INFLIGHT)
        def _(j):
            gathers = [pltpu.async_copy(table_hbm.at[idx_v.at[j + b]], rows_v.at[b], gsem.at[b])
                       for b in range(SC_INFLIGHT)]
            writes = []
            for b in range(SC_INFLIGHT):
                gathers[b].wait()
                dst = out_hbm.at[pl.ds((first + j + b) * SC_WINDOW, SC_WINDOW)]
                writes.append(pltpu.async_copy(rows_v.at[b], dst, wsem.at[b]))
            for w in writes:
                w.wait()

    return k(table, idx.reshape(n_out // SC_WINDOW, SC_WINDOW))


def _final_kernel(x_ref, rt_ref, y_ref, gt_ref, g_ref, o_ref):
    x = x_ref[...] + gt_ref[...] * _combine(rt_ref, y_ref)
    ms = jnp.mean(x * x, axis=-1, keepdims=True)
    o_ref[...] = x * lax.rsqrt(ms + EPS) * g_ref[...]


def _final(x, moe, gate, g):
    tpb = SEQ // TM
    row = pl.BlockSpec((TM, D_MODEL), lambda i: (i, 0))
    return pl.pallas_call(
        _final_kernel,
        grid=(N_TOK // TM,),
        in_specs=[row, pl.BlockSpec((TM, LANES), lambda i: (i, 0)),
                  pl.BlockSpec((TOP_K, N_PLANES, TM, LANES), lambda i: (0, 0, i, 0)),
                  pl.BlockSpec((None, 1, D_MODEL), lambda i: (i // tpb, 0, 0)),
                  pl.BlockSpec((1, D_MODEL), lambda i: (0, 0))],
        out_specs=row,
        out_shape=jax.ShapeDtypeStruct((N_TOK, D_MODEL), F32),
        compiler_params=_params("arbitrary"),
        name="final_norm",
    )(x, moe[0], moe[1], gate, g.reshape(1, D_MODEL))


def kernel(x, c, positions, w_ada, b_ada, g_mix, w_in, b_forget, lambda_q1, lambda_k1, lambda_q2,
           lambda_k2, g_subln, g_fox_out, w_out, g_ffn, w_router_group, b_router_group,
           w_router_expert, b_router_expert, w_expert_gate, w_expert_up, w_expert_down, g_final):
    mod = _modulation(c, w_ada, b_ada)
    mod = mod.reshape(DEPTH, BATCH, 6, 1, D_MODEL)
    tables = _rope_tables(positions)
    pq = _forget_placement()
    xf = x.reshape(N_TOK, D_MODEL)
    moe = None
    gate = None
    for l in range(DEPTH):
        sh1, sc1, gt1, sh2, sc2, gt2 = (mod[l, :, j] for j in range(6))
        w_bf = jnp.pad(w_in[l], ((0, 0), (0, IN_COLS_PAD - IN_COLS))).astype(BF16)
        bfp = jnp.pad(b_forget[l], (0, LANES - N_FOX_HEADS)).reshape(1, LANES)
        xf, (dq, dk, dv, fq, fk, fv) = _inproj(xf, moe, gate, sc1, sh1, g_mix[l], w_bf, bfp, tables, pq)

        lambda_init = 0.8 - 0.6 * float(np.exp(-0.3 * l))
        lamv = jnp.zeros((8, LANES), F32).at[0:4, 0:HEAD_DIM].set(
            jnp.stack([lambda_q1[l], lambda_k1[l], lambda_q2[l], lambda_k2[l]]))
        g_d = g_subln[l].reshape(1, LANES)
        g_f = jnp.concatenate([g_fox_out[l], g_fox_out[l]]).reshape(1, LANES)
        od = _attention(True, lambda_init, dq, dk, dv, g_d, lamv)
        of = _attention(False, lambda_init, fq, fk, fv, g_f, lamv)

        wr32 = jnp.pad(jnp.concatenate([w_router_group[l], w_router_expert[l]], axis=1),
                       ((0, 0), (0, LANES - N_GROUPS - N_EXPERTS)))
        wr_hi = wr32.astype(BF16)
        wr_lo = (wr32 - wr_hi.astype(F32)).astype(BF16)
        wr = jnp.concatenate([wr_hi, wr_lo], axis=1)
        br = jnp.pad(jnp.concatenate([b_router_group[l], b_router_expert[l]]),
                     (0, LANES - N_GROUPS - N_EXPERTS)).reshape(1, LANES)
        xf, h2, route, counts = _outproj(xf, od, of, gt1, sc2, sh2, g_ffn[l], w_out[l].astype(BF16),
                                         wr, br)

        rows, block_expert, block_count = _slots(route, counts)
        xs = _sc_scatter2(h2.reshape(N_PLANES * N_TOK, LANES), rows[0].reshape(-1),
                          rows[1].reshape(-1), N_PLANES * MOE_ROWS)
        ys = _experts(l, block_expert, block_count, xs.reshape(N_PLANES, MOE_ROWS, LANES),
                      w_expert_gate, w_expert_up, w_expert_down)
        y2 = _sc_gather(ys.reshape(N_PLANES * MOE_ROWS, LANES), rows.reshape(-1))
        moe = (route, y2.reshape(TOP_K, N_PLANES, N_TOK, LANES))
        gate = gt2
    out = _final(xf, moe, gate, g_final)
    return out.reshape(BATCH, SEQ, D_MODEL)
```

```python
import functools

import numpy as np
import jax
import jax.numpy as jnp
from jax import lax
from jax.experimental import pallas as pl
from jax.experimental.pallas import tpu as pltpu
from jax.experimental.pallas import tpu_sc as plsc

D_MODEL = 1024
BATCH = 4
SEQ = 4096
DEPTH = 4
N_TOK = BATCH * SEQ

CHUNK = 64
HEAD_DIM = 64
N_DIFF_HEADS = 4
N_FOX_HEADS = 8
DIFF_WIDTH = 512
FOX_WIDTH = 512
IN_COLS = 3 * DIFF_WIDTH + 3 * FOX_WIDTH + N_FOX_HEADS
ROT_DIM = 16
ROPE_THETA = 500000.0
N_GROUPS = 4
EXPERTS_PER_GROUP = 8
N_EXPERTS = 32
TOP_K = 2
D_EXPERT = 512
EPS = 1e-6

LANES = 128
IN_COLS_PAD = 3200
FF_COL = 3 * DIFF_WIDTH + 3 * FOX_WIDTH
QK_WIDTH = 8 * LANES
TM = 512
TQ = 512
ATTN_TQ = 256
N_LATE_CHAINS = 1
ONES_ROWS = 16
ATTN_ROWS = 64
LOG2E = 1.4426950408889634
MOE_BLOCK = 256
MOE_ROWS = N_TOK * TOP_K + N_EXPERTS * MOE_BLOCK
MOE_NBLOCKS = MOE_ROWS // MOE_BLOCK
N_PLANES = D_MODEL // 2 // LANES
SC_WINDOW = 128
SC_INFLIGHT = 4
NEG = -1e30
VMEM_LIMIT = 56 * 1024 * 1024

F32 = jnp.float32
BF16 = jnp.bfloat16


def _bf16_round(x):
    return x.astype(BF16).astype(F32)


def _lane_iota(shape):
    return lax.broadcasted_iota(jnp.int32, shape, 1)


def _params(*sem):
    return pltpu.CompilerParams(dimension_semantics=sem, vmem_limit_bytes=VMEM_LIMIT)


def _pack_planes(y, o_ref):
    bits = lax.bitcast_convert_type(_bf16_round(y), jnp.uint32)
    half = D_MODEL // 2
    word = bits[:, half:] | lax.shift_right_logical(bits[:, :half], jnp.uint32(16))
    word = lax.bitcast_convert_type(word, jnp.int32)
    for p in range(N_PLANES):
        o_ref[p] = word[:, p * LANES:(p + 1) * LANES]


def _unpack_planes(planes):
    lo, hi = [], []
    for w in planes:
        u = lax.bitcast_convert_type(w, jnp.uint32)
        lo.append(lax.bitcast_convert_type(lax.shift_left(u, jnp.uint32(16)), F32))
        hi.append(lax.bitcast_convert_type(u & jnp.uint32(0xFFFF0000), F32))
    return jnp.concatenate(lo + hi, axis=1)


def _combine(route_ref, y_ref):
    rt = route_ref[...]
    y0 = _unpack_planes([y_ref[0, p] for p in range(N_PLANES)])
    y1 = _unpack_planes([y_ref[1, p] for p in range(N_PLANES)])
    return rt[:, 2:3] * y0 + rt[:, 3:4] * y1


def _mod_kernel(c_ref, w_ref, b_ref, o_ref):
    c = c_ref[...]
    cond = c / (1.0 + jnp.exp(-c))
    ch = cond.astype(BF16)
    cl = (cond - ch.astype(F32)).astype(BF16)
    w = w_ref[...]
    wh = w.astype(BF16)
    wl = (w - wh.astype(F32)).astype(BF16)
    acc = jnp.dot(ch, wh, preferred_element_type=F32)
    acc += jnp.dot(cl, wh, preferred_element_type=F32)
    acc += jnp.dot(ch, wl, preferred_element_type=F32)
    o_ref[...] = acc + b_ref[...]


def _modulation(c, w_ada, b_ada):
    rows = 16
    tn = 1536
    c_pad = jnp.zeros((rows, D_MODEL), F32).at[:BATCH].set(c)
    out = pl.pallas_call(
        _mod_kernel,
        grid=(DEPTH, 6 * D_MODEL // tn),
        in_specs=[
            pl.BlockSpec((rows, D_MODEL), lambda l, n: (0, 0)),
            pl.BlockSpec((None, D_MODEL, tn), lambda l, n: (l, 0, n)),
            pl.BlockSpec((None, 1, tn), lambda l, n: (l, 0, n)),
        ],
        out_specs=pl.BlockSpec((None, rows, tn), lambda l, n: (l, 0, n)),
        out_shape=jax.ShapeDtypeStruct((DEPTH, rows, 6 * D_MODEL), F32),
        compiler_params=_params("arbitrary", "arbitrary"),
        name="adaln_mod",
    )(c_pad, w_ada, b_ada.reshape(DEPTH, 1, 6 * D_MODEL))
    return out[:, :BATCH]


def _rope_kernel(pos_ref, inv_ref, c_ref, sa_ref, sb_ref):
    ang = pos_ref[...].astype(F32) * inv_ref[...]
    j = _lane_iota(ang.shape) % HEAD_DIM
    cosv = jnp.cos(ang)
    sinv = jnp.sin(ang)
    half = ROT_DIM // 2
    c_ref[...] = jnp.where(j < ROT_DIM, cosv, 1.0)
    sa_ref[...] = jnp.where(j < half, -sinv, 0.0)
    sb_ref[...] = jnp.where((j >= half) & (j < ROT_DIM), sinv, 0.0)


def _rope_tables(positions):
    half = ROT_DIM // 2
    inv = ROPE_THETA ** (-jnp.arange(0, ROT_DIM, 2, dtype=F32) / ROT_DIM)
    lane = np.arange(LANES)
    inv_lane = inv[(lane % HEAD_DIM) % half].reshape(1, LANES)
    spec = pl.BlockSpec((TM, LANES), lambda i: (i, 0))
    shape = jax.ShapeDtypeStruct((N_TOK, LANES), F32)
    return pl.pallas_call(
        _rope_kernel,
        grid=(N_TOK // TM,),
        in_specs=[pl.BlockSpec((TM, 1), lambda i: (i, 0)),
                  pl.BlockSpec((1, LANES), lambda i: (0, 0))],
        out_specs=[spec, spec, spec],
        out_shape=[shape, shape, shape],
        compiler_params=_params("arbitrary"),
        name="rope_tables",
    )(positions.reshape(N_TOK, 1), inv_lane)


def _rms_mod(x, g, sc, sh):
    ms = jnp.mean(x * x, axis=-1, keepdims=True)
    return (x * lax.rsqrt(ms + EPS) * g) * (1.0 + sc) + sh


def _inproj_kernel(fuse, *refs):
    if fuse:
        (x_ref, rt_ref, y_ref, gt_ref, sc_ref, sh_ref, g_ref, w_ref, bf_ref, c_ref, sa_ref, sb_ref,
         pq_ref, xo_ref, dq_ref, dk_ref, dv_ref, fq_ref, fk_ref, fv_ref, carry_ref) = refs
        x = x_ref[...] + gt_ref[...] * _combine(rt_ref, y_ref)
        xo_ref[...] = x
    else:
        (x_ref, sc_ref, sh_ref, g_ref, w_ref, bf_ref, c_ref, sa_ref, sb_ref,
         pq_ref, dq_ref, dk_ref, dv_ref, fq_ref, fk_ref, fv_ref, carry_ref) = refs
        x = x_ref[...]
    hb = _rms_mod(x, g_ref[...], sc_ref[...], sh_ref[...]).astype(BF16)

    @pl.when(pl.program_id(0) % (SEQ // TM) == 0)
    def _():
        carry_ref[...] = jnp.zeros_like(carry_ref)

    lane = _lane_iota((TM, LANES))
    nh = N_FOX_HEADS

    def pack3(a):
        hi = _bf16_round(a)
        r1 = a - hi
        mid = _bf16_round(r1)
        lo = _bf16_round(r1 - mid)
        return jnp.where(lane < nh, hi,
                         jnp.where(lane < 2 * nh, pltpu.roll(mid, nh, 1),
                                   jnp.where(lane < 3 * nh, pltpu.roll(lo, 2 * nh, 1), 0.0)))

    z = jnp.dot(hb, w_ref[:, FF_COL:FF_COL + LANES], preferred_element_type=F32) + bf_ref[...]

    low = lane < HEAD_DIM
    rc, rsa, rsb = c_ref[...], sa_ref[...], sb_ref[...]
    scale = HEAD_DIM ** -0.5 * LOG2E

    def split_store(chunk, o_ref, m, extra_a=None, extra_b=None):
        a = jnp.where(low, chunk, 0.0)
        b = jnp.where(low, pltpu.roll(chunk, HEAD_DIM, 1), 0.0)
        if extra_a is not None:
            a = a + extra_a
            b = b + extra_b
        o_ref[:, (2 * m) * LANES:(2 * m + 1) * LANES] = a.astype(BF16)
        o_ref[:, (2 * m + 1) * LANES:(2 * m + 2) * LANES] = b.astype(BF16)

    def rope(xc):
        return xc * rc + pltpu.roll(xc, LANES - ROT_DIM // 2, 1) * rsa + pltpu.roll(xc, ROT_DIM // 2, 1) * rsb

    pdq = jnp.dot(hb, w_ref[:, 0:DIFF_WIDTH], preferred_element_type=F32)
    for m in range(N_DIFF_HEADS):
        split_store(rope(pdq[:, m * LANES:(m + 1) * LANES]) * scale, dq_ref, m)
    pdk = jnp.dot(hb, w_ref[:, DIFF_WIDTH:2 * DIFF_WIDTH], preferred_element_type=F32)
    for m in range(N_DIFF_HEADS):
        split_store(rope(pdk[:, m * LANES:(m + 1) * LANES]), dk_ref, m)
    def store_values_t(pv, o_ref, width):
        ones = jnp.ones((ONES_ROWS, TM), BF16)
        for m in range(4):
            vt = pv[:, m * LANES:(m + 1) * LANES].T.astype(BF16)
            for i in range(LANES // width):
                o_ref[m * (LANES // width) + i, 0:width, :] = vt[i * width:(i + 1) * width]
                o_ref[m * (LANES // width) + i, width:width + ONES_ROWS, :] = ones

    store_values_t(jnp.dot(hb, w_ref[:, 2 * DIFF_WIDTH:3 * DIFF_WIDTH], preferred_element_type=F32),
                   dv_ref, 2 * HEAD_DIM)
    o = 3 * DIFF_WIDTH
    store_values_t(jnp.dot(hb, w_ref[:, o + 2 * FOX_WIDTH:o + 3 * FOX_WIDTH],
                           preferred_element_type=F32), fv_ref, HEAD_DIM)

    logf =jnp.minimum(z, 0.0) - jnp.log(1.0 + jnp.exp(-jnp.abs(z)))
    logf = jnp.where(lane < nh, logf, 0.0)
    row = lax.broadcasted_iota(jnp.int32, (TM, TM), 0)
    col = lax.broadcasted_iota(jnp.int32, (TM, TM), 1)
    tri = (row >= col).astype(BF16)
    r = jnp.dot(tri, pack3(logf).astype(BF16), preferred_element_type=F32)
    cs = r + pltpu.roll(r, LANES - nh, 1) + pltpu.roll(r, LANES - 2 * nh, 1)
    cf = jnp.where(lane < nh, cs + carry_ref[0:1, :], 0.0)
    carry_ref[...] = jnp.broadcast_to(cf[TM - 1:TM, :], carry_ref.shape)

    t3 = jnp.where(lane == 3 * nh, 1.0, pack3(cf * LOG2E)).astype(BF16)
    aug = jnp.dot(t3, pq_ref[...], preferred_element_type=F32)

    pfq =jnp.dot(hb, w_ref[:, o:o + FOX_WIDTH], preferred_element_type=F32)
    for m in range(N_FOX_HEADS // 2):
        split_store(pfq[:, m * LANES:(m + 1) * LANES] * scale, fq_ref, m,
                    aug[:, (2 * m) * LANES:(2 * m + 1) * LANES],
                    aug[:, (2 * m + 1) * LANES:(2 * m + 2) * LANES])
    pfk = jnp.dot(hb, w_ref[:, o + FOX_WIDTH:o + 2 * FOX_WIDTH], preferred_element_type=F32)
    for m in range(N_FOX_HEADS // 2):
        split_store(pfk[:, m * LANES:(m + 1) * LANES], fk_ref, m,
                    aug[:, QK_WIDTH + (2 * m) * LANES:QK_WIDTH + (2 * m + 1) * LANES],
                    aug[:, QK_WIDTH + (2 * m + 1) * LANES:QK_WIDTH + (2 * m + 2) * LANES])


def _forget_placement():
    nh = N_FOX_HEADS
    p = np.zeros((LANES, 2 * QK_WIDTH), np.float32)
    for h in range(nh):
        base_q = h * LANES + HEAD_DIM
        base_k = QK_WIDTH + h * LANES + HEAD_DIM
        for part in range(3):
            p[part * nh + h, base_q + part] = 1.0
            p[3 * nh, base_q + 3 + part] = 1.0
            p[3 * nh, base_k + part] = 1.0
            p[part * nh + h, base_k + 3 + part] = -1.0
    return jnp.asarray(p, BF16)


def _inproj(x, moe, gate, sc, sh, g, w_bf, b_forget, tables, pq):
    fuse = moe is not None
    tpb = SEQ // TM
    row = pl.BlockSpec((TM, D_MODEL), lambda i: (i, 0))
    per_batch = pl.BlockSpec((None, 1, D_MODEL), lambda i: (i // tpb, 0, 0))
    const = lambda shape: pl.BlockSpec(shape, lambda i: (0,) * len(shape))
    tab = pl.BlockSpec((TM, LANES), lambda i: (i, 0))
    in_specs = [row]
    args = [x]
    if fuse:
        in_specs += [tab, pl.BlockSpec((TOP_K, N_PLANES, TM, LANES), lambda i: (0, 0, i, 0)), per_batch]
        args += [moe[0], moe[1], gate]
    in_specs += [per_batch, per_batch, const((1, D_MODEL)), const((D_MODEL, IN_COLS_PAD)),
                 const((1, LANES)), tab, tab, tab, const((LANES, 2 * QK_WIDTH))]
    args += [sc, sh, g.reshape(1, D_MODEL), w_bf, b_forget, *tables, pq]
    wide = pl.BlockSpec((TM, QK_WIDTH), lambda i: (i, 0))
    def vspec(heads, width):
        rows = width + ONES_ROWS
        return (pl.BlockSpec((None, heads, None, rows, TM), lambda i: (i // tpb, 0, i % tpb, 0, 0)),
                jax.ShapeDtypeStruct((BATCH, heads, tpb, rows, TM), BF16))

    wide_s = jax.ShapeDtypeStruct((N_TOK, QK_WIDTH), BF16)
    dv_spec, dv_s = vspec(N_DIFF_HEADS, 2 * HEAD_DIM)
    fv_spec, fv_s = vspec(N_FOX_HEADS, HEAD_DIM)
    out_specs = [wide, wide, dv_spec, wide, wide, fv_spec]
    out_shape = [wide_s, wide_s, dv_s, wide_s, wide_s, fv_s]
    if fuse:
        out_specs = [row] + out_specs
        out_shape = [jax.ShapeDtypeStruct((N_TOK, D_MODEL), F32)] + out_shape
    outs = pl.pallas_call(
        functools.partial(_inproj_kernel, fuse),
        grid=(N_TOK // TM,),
        in_specs=in_specs,
        out_specs=out_specs,
        out_shape=out_shape,
        scratch_shapes=[pltpu.VMEM((8, LANES), F32)],
        compiler_params=_params("arbitrary"),
        name="norm_inproj",
    )(*args)
    if fuse:
        return outs[0], outs[1:]
    return x, outs


def _attn_kernel(diff, lambda_init, qa_ref, qb_ref, ka_ref, kb_ref, v_ref, g_ref, lam_ref, o_ref,
                 *scratch):
    nq = SEQ // TQ
    n_half = TQ // ATTN_TQ
    feat = 2 * HEAD_DIM if diff else HEAD_DIM
    chains = []
    for mi, (q_ref, k_ref) in enumerate(((qa_ref, ka_ref), (qb_ref, kb_ref))):
        for h in range(n_half):
            c = mi * n_half + h
            qt_sc, s_sc, p_sc, m_sc, a_sc, acc_sc = scratch[c::2 * n_half]
            vh = 0 if diff else mi
            chains.append((h, k_ref, qt_sc, s_sc, p_sc, m_sc, a_sc, acc_sc, q_ref, vh))
    order = [chains[mi * n_half + h] for h in range(n_half) for mi in range(2)]
    early, late = order[:-N_LATE_CHAINS], order[-N_LATE_CHAINS:]

    def load_queries(qi):
        for mi in range(2):
            q_ref = chains[mi * n_half][8]
            qt = q_ref[pl.ds(pl.multiple_of(qi * TQ, TQ), TQ), :].astype(F32).T.astype(BF16)
            for h in range(n_half):
                chains[mi * n_half + h][2][...] = qt[:, h * ATTN_TQ:(h + 1) * ATTN_TQ]

    def reset_state():
        for chain in chains:
            m_sc, _, acc_sc = chain[5:8]
            m_sc[...] = jnp.full(m_sc.shape, NEG, F32)
            acc_sc[...] = jnp.zeros(acc_sc.shape, F32)

    def n_keys(chain, masked):
        return (chain[0] + 1) * ATTN_TQ if masked else TQ

    def scores(chain, j, masked):
        h, k_ref, qt_sc, s_sc = chain[:4]
        nk = n_keys(chain, masked)
        off = pl.multiple_of(j * TQ, TQ)
        s = jnp.dot(k_ref[pl.ds(off, nk), :], qt_sc[...], preferred_element_type=F32)
        if masked:
            kk = lax.broadcasted_iota(jnp.int32, (nk, ATTN_TQ), 0)
            qq = h * ATTN_TQ + lax.broadcasted_iota(jnp.int32, (nk, ATTN_TQ), 1)
            s = jnp.where((kk // CHUNK <= qq // CHUNK) if diff else (kk <= qq), s, NEG)
        s_sc[0:nk, :] = s

    def softmax(chain, masked):
        s_sc, p_sc, m_sc, a_sc = chain[3:7]
        nk = n_keys(chain, masked)
        m_all = m_sc[...]
        m_parts = []
        for c0 in range(0, ATTN_TQ, LANES):
            cols = slice(c0, c0 + LANES)
            pm = s_sc[0:ATTN_ROWS, cols]
            for r0 in range(ATTN_ROWS, nk, ATTN_ROWS):
                pm = jnp.maximum(pm, s_sc[r0:r0 + ATTN_ROWS, cols])
            m_new = jnp.maximum(m_all[:, cols], jnp.max(pm, axis=0, keepdims=True))
            for r0 in range(0, nk, ATTN_ROWS):
                p = jnp.exp2(s_sc[r0:r0 + ATTN_ROWS, cols] - m_new)
                p_sc[r0:r0 + ATTN_ROWS, cols] = p.astype(BF16)
            m_parts.append(m_new)
        m_new = jnp.concatenate(m_parts, axis=1)
        a_sc[...] = jnp.exp2(m_all - m_new)
        m_sc[...] = m_new

    def values(chain, j, masked=False):
        p_sc, a_sc, acc_sc, vh = chain[4], chain[6], chain[7], chain[9]
        nk = n_keys(chain, masked)
        pv = jnp.dot(v_ref[vh, j, :, 0:nk], p_sc[0:nk, :], preferred_element_type=F32)
        acc_sc[...] = a_sc[...] * acc_sc[...] + pv

    def idle_late():
        for chain in late:
            chain[4][...] = jnp.zeros(chain[4].shape, BF16)
            chain[6][...] = jnp.ones(chain[6].shape, F32)

    def consume(j, cur_masked=False, nxt=None, nxt_masked=False, final=False, before_next=None):
        def open_late(chain):
            scores(chain, j, cur_masked)
            values(chain, jnp.maximum(j - 1, 0))

        open_late(late[0])
        for i, chain in enumerate(early):
            softmax(chain, cur_masked)
            if i == 0:
                for other in late[1:]:
                    open_late(other)
                if before_next is not None:
                    before_next()
            if nxt is not None:
                scores(chain, nxt, nxt_masked)
            values(chain, j, cur_masked)
        for chain in late:
            softmax(chain, cur_masked)
        if final:
            for chain in late:
                values(chain, j, cur_masked)

    def finalize(qi):
        ot = [jnp.concatenate([chains[mi * n_half + h][7][0:feat] / chains[mi * n_half + h][7][feat:feat + 1]
                               for h in range(n_half)], axis=1) for mi in range(2)]
        g = g_ref[...]
        rows = pl.ds(pl.multiple_of(qi * TQ, TQ), TQ)
        if diff:
            lv = lam_ref[...]
            lam = (jnp.exp(jnp.sum(lv[0:1] * lv[1:2], axis=1, keepdims=True))
                   - jnp.exp(jnp.sum(lv[2:3] * lv[3:4], axis=1, keepdims=True)) + lambda_init)
            o = (ot[0] - lam * ot[1]).T
            y = o * lax.rsqrt(jnp.mean(o * o, axis=1, keepdims=True) + EPS) * g
            o_ref[rows, :] = (y * (1.0 - lambda_init)).astype(o_ref.dtype)
        else:
            o = jnp.concatenate(ot, axis=0).T
            low = _lane_iota((TQ, LANES)) < HEAD_DIM
            sq = o * o
            msa = jnp.sum(jnp.where(low, sq, 0.0), axis=1, keepdims=True) / HEAD_DIM
            msb = jnp.sum(jnp.where(low, 0.0, sq), axis=1, keepdims=True) / HEAD_DIM
            inv = jnp.where(low, lax.rsqrt(msa + EPS), lax.rsqrt(msb + EPS))
            o_ref[rows, :] = (o * inv * g).astype(o_ref.dtype)

    load_queries(0)
    reset_state()
    idle_late()
    for chain in early:
        scores(chain, 0, True)

    @pl.loop(0, nq)
    def _(qi):
        n_plain = jnp.maximum(qi - 1, 0)

        @pl.loop(0, n_plain // 2)
        def _(t):
            consume(2 * t, nxt=2 * t + 1)
            consume(2 * t + 1, nxt=2 * t + 2)

        @pl.when(n_plain % 2 == 1)
        def _():
            consume(qi - 2, nxt=qi - 1)

        def last_blocks(to_next_tile):
            if to_next_tile:
                consume(qi, cur_masked=True, nxt=0, final=True, before_next=lambda: load_queries(qi + 1))
            else:
                consume(qi, cur_masked=True, final=True)

        for to_next_tile in (True, False):
            more = (qi < nq - 1) if to_next_tile else (qi == nq - 1)

            @pl.when(more & (qi > 0))
            def _():
                consume(qi - 1, nxt=qi, nxt_masked=True)
                last_blocks(to_next_tile)

            if to_next_tile:
                @pl.when(qi == 0)
                def _():
                    last_blocks(to_next_tile)

        finalize(qi)
        reset_state()
        idle_late()


def _attention(diff, lambda_init, q, k, v, g, lamv):
    nq = SEQ // TQ
    kspec = lambda par: pl.BlockSpec((SEQ, LANES), lambda b, p: (b, 2 * p + par))
    return pl.pallas_call(
        functools.partial(_attn_kernel, diff, lambda_init),
        grid=(BATCH, 4),
        in_specs=[kspec(0), kspec(1), kspec(0), kspec(1),
                  pl.BlockSpec((None, v.shape[1] // 4, nq, v.shape[3], TQ), lambda b, p: (b, p, 0, 0, 0)),
                  pl.BlockSpec((1, LANES), lambda b, p: (0, 0)),
                  pl.BlockSpec((8, LANES), lambda b, p: (0, 0))],
        out_specs=pl.BlockSpec((SEQ, LANES), lambda b, p: (b, p)),
        out_shape=jax.ShapeDtypeStruct((N_TOK, DIFF_WIDTH), BF16),
        scratch_shapes=[pltpu.VMEM(shape, dt)
                        for shape, dt in (((LANES, ATTN_TQ), BF16), ((TQ, ATTN_TQ), F32),
                                          ((TQ, ATTN_TQ), BF16), ((1, ATTN_TQ), F32),
                                          ((1, ATTN_TQ), F32), ((v.shape[3], ATTN_TQ), F32))
                        for _ in range(2 * TQ // ATTN_TQ)],
        compiler_params=_params("arbitrary", "arbitrary"),
        name="diff_attention" if diff else "fox_attention",
    )(q, q, k, k, v, g, lamv)


def _outproj_kernel(x_ref, od_ref, of_ref, gt_ref, sc_ref, sh_ref, g_ref, wo_ref, wr_ref, br_ref,
                    x1_ref, h2_ref, rt_ref, cnt_ref, carry_ref):
    @pl.when(pl.program_id(0) == 0)
    def _():
        carry_ref[...] = jnp.zeros_like(carry_ref)

    mix = jnp.dot(od_ref[...], wo_ref[0:DIFF_WIDTH, :], preferred_element_type=F32)
    mix += jnp.dot(of_ref[...], wo_ref[DIFF_WIDTH:, :], preferred_element_type=F32)
    x1 = x_ref[...] + gt_ref[...] * mix
    x1_ref[...] = x1
    h = _rms_mod(x1, g_ref[...], sc_ref[...], sh_ref[...])
    hh = h.astype(BF16)
    _pack_planes(h, h2_ref)
    hl = (h - hh.astype(F32)).astype(BF16)
    r1 = jnp.dot(hh, wr_ref[...], preferred_element_type=F32)
    r2 = jnp.dot(hl, wr_ref[:, 0:LANES], preferred_element_type=F32)
    logits = r1[:, 0:LANES] + r1[:, LANES:] + r2 + br_ref[...]

    lane = _lane_iota((TM, LANES))
    lanef = lane.astype(F32)
    big = float(LANES)
    isg = lane < N_GROUPS
    lg = jnp.where(isg, logits, NEG)
    mg = jnp.max(lg, axis=1, keepdims=True)
    sg = jnp.sum(jnp.where(isg, jnp.exp(lg - mg), 0.0), axis=1, keepdims=True)
    p_g = 1.0 / sg
    gsel = jnp.min(jnp.where(isg & (lg == mg), lanef, big), axis=1, keepdims=True)
    lo = N_GROUPS + gsel * EXPERTS_PER_GROUP
    ise = (lanef >= lo) & (lanef < lo + EXPERTS_PER_GROUP)
    le = jnp.where(ise, logits, NEG)
    t1 = jnp.max(le, axis=1, keepdims=True)
    i1 = jnp.min(jnp.where(ise & (le == t1), lanef, big), axis=1, keepdims=True)
    ise2 = ise & (lanef != i1)
    le2 = jnp.where(ise2, logits, NEG)
    t2 = jnp.max(le2, axis=1, keepdims=True)
    i2 = jnp.min(jnp.where(ise2 & (le2 == t2), lanef, big), axis=1, keepdims=True)
    d = jnp.exp(t2 - t1)
    w1 = p_g / (1.0 + d)
    w2 = p_g * d / (1.0 + d)
    e1 = i1 - N_GROUPS
    e2 = i2 - N_GROUPS
    oh1 = lanef == e1
    oh2 = lanef == e2
    both = jnp.where(oh1 | oh2, 1.0, 0.0)
    row = lax.broadcasted_iota(jnp.int32, (TM, TM), 0)
    col = lax.broadcasted_iota(jnp.int32, (TM, TM), 1)
    before = jnp.dot((row > col).astype(BF16), both.astype(BF16), preferred_element_type=F32)
    before = before + carry_ref[0:1, :]
    rank1 = jnp.sum(jnp.where(oh1, before, 0.0), axis=1, keepdims=True)
    rank2 = jnp.sum(jnp.where(oh2, before, 0.0), axis=1, keepdims=True)
    total = carry_ref[0:1, :] + jnp.sum(both, axis=0, keepdims=True)
    carry_ref[...] = jnp.broadcast_to(total, carry_ref.shape)
    cnt_ref[...] = jnp.broadcast_to(total, cnt_ref.shape)
    vals = (e1, e2, w1, w2, rank1, rank2)
    out = jnp.zeros((TM, LANES), F32)
    for j, v in enumerate(vals):
        out = jnp.where(lane == j, v, out)
    rt_ref[...] = out


def _outproj(x, od, of, gt, sc, sh, g, wo_bf, wr, br):
    tpb = SEQ // TM
    row = pl.BlockSpec((TM, D_MODEL), lambda i: (i, 0))
    half = pl.BlockSpec((TM, DIFF_WIDTH), lambda i: (i, 0))
    per_batch = pl.BlockSpec((None, 1, D_MODEL), lambda i: (i // tpb, 0, 0))
    const = lambda shape: pl.BlockSpec(shape, lambda i: (0,) * len(shape))
    return pl.pallas_call(
        _outproj_kernel,
        grid=(N_TOK // TM,),
        in_specs=[row, half, half, per_batch, per_batch, per_batch, const((1, D_MODEL)),
                  const((D_MODEL, D_MODEL)), const((D_MODEL, 2 * LANES)), const((1, LANES))],
        out_specs=[row, pl.BlockSpec((N_PLANES, TM, LANES), lambda i: (0, i, 0)),
                   pl.BlockSpec((TM, LANES), lambda i: (i, 0)), const((8, LANES))],
        out_shape=[jax.ShapeDtypeStruct((N_TOK, D_MODEL), F32),
                   jax.ShapeDtypeStruct((N_PLANES, N_TOK, LANES), jnp.int32),
                   jax.ShapeDtypeStruct((N_TOK, LANES), F32),
                   jax.ShapeDtypeStruct((8, LANES), F32)],
        scratch_shapes=[pltpu.VMEM((8, LANES), F32)],
        compiler_params=_params("arbitrary"),
        name="outproj_router",
    )(x, od, of, gt, sc, sh, g.reshape(1, D_MODEL), wo_bf, wr, br)


def _expert_kernel(be_ref, cnt_ref, xs_ref, wg_ref, wu_ref, wd_ref, ys_ref, wg_sc, wu_sc, wd_sc):
    i = pl.program_id(0)
    cnt = cnt_ref[i]

    @pl.when((i == 0) | (be_ref[i] != be_ref[jnp.maximum(i - 1, 0)]))
    def _():
        wg_sc[...] = wg_ref[...].astype(BF16)
        wu_sc[...] = wu_ref[...].astype(BF16)
        wd_sc[...] = wd_ref[...].astype(BF16)

    @pl.when(cnt > 0)
    def _():
        live = lax.broadcasted_iota(jnp.int32, (MOE_BLOCK, LANES), 0) < cnt
        xb = _unpack_planes([jnp.where(live, xs_ref[p], 0) for p in range(N_PLANES)]).astype(BF16)
        a = jnp.dot(xb, wg_sc[...], preferred_element_type=F32)
        u = jnp.dot(xb, wu_sc[...], preferred_element_type=F32)
        hid = (a / (1.0 + jnp.exp(-a)) * u).astype(BF16)
        _pack_planes(jnp.dot(hid, wd_sc[...], preferred_element_type=F32), ys_ref)

    @pl.when(cnt == 0)
    def _():
        ys_ref[...] = jnp.zeros_like(ys_ref)


def _experts(layer, block_expert, block_count, xs, wg, wu, wd):
    planes = pl.BlockSpec((N_PLANES, MOE_BLOCK, LANES), lambda i, be, bc: (0, i, 0))
    w_in = pl.BlockSpec((None, None, D_MODEL, D_EXPERT), lambda i, be, bc: (layer, be[i], 0, 0))
    w_out = pl.BlockSpec((None, None, D_EXPERT, D_MODEL), lambda i, be, bc: (layer, be[i], 0, 0))
    grid_spec = pltpu.PrefetchScalarGridSpec(
        num_scalar_prefetch=2,
        grid=(MOE_NBLOCKS,),
        in_specs=[planes, w_in, w_in, w_out],
        out_specs=planes,
        scratch_shapes=[pltpu.VMEM((D_MODEL, D_EXPERT), BF16), pltpu.VMEM((D_MODEL, D_EXPERT), BF16),
                        pltpu.VMEM((D_EXPERT, D_MODEL), BF16)],
    )
    return pl.pallas_call(
        _expert_kernel,
        grid_spec=grid_spec,
        out_shape=jax.ShapeDtypeStruct((N_PLANES, MOE_ROWS, LANES), jnp.int32),
        compiler_params=_params("arbitrary"),
        name="expert_mlp",
    )(block_expert, block_count, xs, wg, wu, wd)


def _slots(route, counts):
    counts = counts[0, :N_EXPERTS].astype(jnp.int32)
    padded = ((counts + MOE_BLOCK - 1) // MOE_BLOCK) * MOE_BLOCK
    pend = jnp.cumsum(padded)
    pstart = pend - padded
    bstart = jnp.arange(MOE_NBLOCKS, dtype=jnp.int32) * MOE_BLOCK
    block_expert = jnp.minimum(jnp.sum(bstart[:, None] >= pend[None, :], axis=1), N_EXPERTS - 1)
    block_expert = block_expert.astype(jnp.int32)
    block_count = jnp.clip(counts[block_expert] - (bstart - pstart[block_expert]), 0, MOE_BLOCK)
    block_count = jnp.where(bstart < pend[-1], block_count, 0).astype(jnp.int32)
    base = jnp.pad(pstart.astype(F32), (0, LANES - N_EXPERTS)).reshape(1, LANES)
    return _slot_rows(route, base), block_expert, block_count


def _slot_rows_kernel(rt_ref, base_ref, o_ref):
    rt = rt_ref[...]
    lanef = _lane_iota(rt.shape).astype(F32)
    base = base_ref[...]
    dest = jnp.zeros(rt.shape, F32)
    for k in range(TOP_K):
        b = jnp.sum(jnp.where(lanef == rt[:, k:k + 1], base, 0.0), axis=1, keepdims=True)
        dest = jnp.where(lanef == k, b + rt[:, 2 * TOP_K + k:2 * TOP_K + k + 1], dest)
    dest_t = dest.T.astype(jnp.int32)
    for k in range(TOP_K):
        for p in range(N_PLANES):
            for c in range(rt.shape[0] // LANES):
                o_ref[k * N_PLANES + p, c:c + 1, :] = dest_t[k:k + 1, c * LANES:(c + 1) * LANES] + p * MOE_ROWS


def _slot_rows(route, base):
    tm = 8 * LANES
    return pl.pallas_call(
        _slot_rows_kernel,
        grid=(N_TOK // tm,),
        in_specs=[pl.BlockSpec((tm, LANES), lambda i: (i, 0)), pl.BlockSpec((1, LANES), lambda i: (0, 0))],
        out_specs=pl.BlockSpec((TOP_K * N_PLANES, tm // LANES, LANES), lambda i: (0, i, 0)),
        out_shape=jax.ShapeDtypeStruct((TOP_K * N_PLANES, N_TOK // LANES, LANES), jnp.int32),
        compiler_params=_params("arbitrary"),
        name="slot_rows",
    )(route, base)


def _sc_workers():
    info = plsc.get_sparse_core_info()
    return info.num_cores, info.num_cores * info.num_subcores


def _sc_scatter2(src, idx, out_rows):
    n_win = src.shape[0] // SC_WINDOW
    nc, nw = _sc_workers()
    steps = n_win // nw
    mesh = plsc.VectorSubcoreMesh(core_axis_name="c", subcore_axis_name="s")

    @functools.partial(
        pl.kernel, mesh=mesh,
        out_type=jax.ShapeDtypeStruct((out_rows, LANES), src.dtype),
        scratch_types=[pltpu.VMEM((SC_WINDOW,), jnp.int32), pltpu.VMEM((SC_WINDOW,), jnp.int32),
                       pltpu.VMEM((SC_WINDOW, LANES), src.dtype)],
        name="sc_dispatch_scatter",
    )
    def k(src_hbm, idx_hbm, out_hbm, ia_v, ib_v, rows_v):
        first = (lax.axis_index("s") * nc + lax.axis_index("c")) * steps

        @pl.loop(0, steps)
        def _(j):
            w = first + j
            pltpu.sync_copy(idx_hbm.at[w], ia_v)
            pltpu.sync_copy(idx_hbm.at[n_win + w], ib_v)
            pltpu.sync_copy(src_hbm.at[pl.ds(w * SC_WINDOW, SC_WINDOW)], rows_v)
            pltpu.sync_copy(rows_v, out_hbm.at[ia_v])
            pltpu.sync_copy(rows_v, out_hbm.at[ib_v])

    return k(src, idx)


def _sc_gather(table, idx):
    n_out = idx.shape[0] * SC_WINDOW
    nc, nw = _sc_workers()
    steps = n_out // nw // SC_WINDOW
    mesh = plsc.VectorSubcoreMesh(core_axis_name="c", subcore_axis_name="s")

    @functools.partial(
        pl.kernel, mesh=mesh,
        out_type=jax.ShapeDtypeStruct((n_out, LANES), table.dtype),
        scratch_types=[pltpu.VMEM((steps, SC_WINDOW), jnp.int32),
                       pltpu.VMEM((SC_INFLIGHT, SC_WINDOW, LANES), table.dtype),
                       pltpu.SemaphoreType.DMA((SC_INFLIGHT,)), pltpu.SemaphoreType.DMA((SC_INFLIGHT,))],
        name="sc_combine_gather",
    )
    def k(table_hbm, idx_hbm, out_hbm, idx_v, rows_v, gsem, wsem):
        first = (lax.axis_index("s") * nc + lax.axis_index("c")) * steps
        pltpu.sync_copy(idx_hbm.at[pl.ds(first, steps)], idx_v)

        @pl.loop(0, steps, step=SC_INFLIGHT)
        def _(j):
            gathers = [pltpu.async_copy(table_hbm.at[idx_v.at[j + b]], rows_v.at[b], gsem.at[b])
                       for b in range(SC_INFLIGHT)]
            writes = []
            for b in range(SC_INFLIGHT):
                gathers[b].wait()
                dst = out_hbm.at[pl.ds((first + j + b) * SC_WINDOW, SC_WINDOW)]
                writes.append(pltpu.async_copy(rows_v.at[b], dst, wsem.at[b]))
            for w in writes:
                w.wait()

    return k(table, idx)


def _final_kernel(x_ref, rt_ref, y_ref, gt_ref, g_ref, o_ref):
    x = x_ref[...] + gt_ref[...] * _combine(rt_ref, y_ref)
    ms = jnp.mean(x * x, axis=-1, keepdims=True)
    o_ref[...] = x * lax.rsqrt(ms + EPS) * g_ref[...]


def _final(x, moe, gate, g):
    tpb = SEQ // TM
    row = pl.BlockSpec((TM, D_MODEL), lambda i: (i, 0))
    return pl.pallas_call(
        _final_kernel,
        grid=(N_TOK // TM,),
        in_specs=[row, pl.BlockSpec((TM, LANES), lambda i: (i, 0)),
                  pl.BlockSpec((TOP_K, N_PLANES, TM, LANES), lambda i: (0, 0, i, 0)),
                  pl.BlockSpec((None, 1, D_MODEL), lambda i: (i // tpb, 0, 0)),
                  pl.BlockSpec((1, D_MODEL), lambda i: (0, 0))],
        out_specs=row,
        out_shape=jax.ShapeDtypeStruct((N_TOK, D_MODEL), F32),
        compiler_params=_params("arbitrary"),
        name="final_norm",
    )(x, moe[0], moe[1], gate, g.reshape(1, D_MODEL))


def kernel(x, c, positions, w_ada, b_ada, g_mix, w_in, b_forget, lambda_q1, lambda_k1, lambda_q2,
           lambda_k2, g_subln, g_fox_out, w_out, g_ffn, w_router_group, b_router_group,
           w_router_expert, b_router_expert, w_expert_gate, w_expert_up, w_expert_down, g_final):
    mod = _modulation(c, w_ada, b_ada)
    mod = mod.reshape(DEPTH, BATCH, 6, 1, D_MODEL)
    tables = _rope_tables(positions)
    pq = _forget_placement()
    xf = x.reshape(N_TOK, D_MODEL)
    moe = None
    gate = None
    for l in range(DEPTH):
        sh1, sc1, gt1, sh2, sc2, gt2 = (mod[l, :, j] for j in range(6))
        w_bf = jnp.pad(w_in[l], ((0, 0), (0, IN_COLS_PAD - IN_COLS))).astype(BF16)
        bfp = jnp.pad(b_forget[l], (0, LANES - N_FOX_HEADS)).reshape(1, LANES)
        xf, (dq, dk, dv, fq, fk, fv) = _inproj(xf, moe, gate, sc1, sh1, g_mix[l], w_bf, bfp, tables, pq)

        lambda_init = 0.8 - 0.6 * float(np.exp(-0.3 * l))
        lamv = jnp.zeros((8, LANES), F32).at[0:4, 0:HEAD_DIM].set(
            jnp.stack([lambda_q1[l], lambda_k1[l], lambda_q2[l], lambda_k2[l]]))
        g_d = g_subln[l].reshape(1, LANES)
        g_f = jnp.concatenate([g_fox_out[l], g_fox_out[l]]).reshape(1, LANES)
        od = _attention(True, lambda_init, dq, dk, dv, g_d, lamv)
        of = _attention(False, lambda_init, fq, fk, fv, g_f, lamv)

        wr32 = jnp.pad(jnp.concatenate([w_router_group[l], w_router_expert[l]], axis=1),
                       ((0, 0), (0, LANES - N_GROUPS - N_EXPERTS)))
        wr_hi = wr32.astype(BF16)
        wr_lo = (wr32 - wr_hi.astype(F32)).astype(BF16)
        wr = jnp.concatenate([wr_hi, wr_lo], axis=1)
        br = jnp.pad(jnp.concatenate([b_router_group[l], b_router_expert[l]]),
                     (0, LANES - N_GROUPS - N_EXPERTS)).reshape(1, LANES)
        xf, h2, route, counts = _outproj(xf, od, of, gt1, sc2, sh2, g_ffn[l], w_out[l].astype(BF16),
                                         wr, br)

        rows, block_expert, block_count = _slots(route, counts)
        rows = rows.reshape(TOP_K * N_PLANES * N_TOK // SC_WINDOW, SC_WINDOW)
        xs = _sc_scatter2(h2.reshape(N_PLANES * N_TOK, LANES), rows, N_PLANES * MOE_ROWS)
        ys = _experts(l, block_expert, block_count, xs.reshape(N_PLANES, MOE_ROWS, LANES),
                      w_expert_gate, w_expert_up, w_expert_down)
        y2 = _sc_gather(ys.reshape(N_PLANES * MOE_ROWS, LANES), rows)
        moe = (route, y2.reshape(TOP_K, N_PLANES, N_TOK, LANES))
        gate = gt2
    out = _final(xf, moe, gate, g_final)
    return out.reshape(BATCH, SEQ, D_MODEL)
```

```python
import functools

import numpy as np
import jax
import jax.numpy as jnp
from jax import lax
from jax.experimental import pallas as pl
from jax.experimental.pallas import tpu as pltpu
from jax.experimental.pallas import tpu_sc as plsc

D_MODEL = 1024
BATCH = 4
SEQ = 4096
DEPTH = 4
N_TOK = BATCH * SEQ

CHUNK = 64
HEAD_DIM = 64
N_DIFF_HEADS = 4
N_FOX_HEADS = 8
DIFF_WIDTH = 512
FOX_WIDTH = 512
IN_COLS = 3 * DIFF_WIDTH + 3 * FOX_WIDTH + N_FOX_HEADS
ROT_DIM = 16
ROPE_THETA = 500000.0
N_GROUPS = 4
EXPERTS_PER_GROUP = 8
N_EXPERTS = 32
TOP_K = 2
D_EXPERT = 512
EPS = 1e-6

LANES = 128
IN_COLS_PAD = 3200
FF_COL = 3 * DIFF_WIDTH + 3 * FOX_WIDTH
QK_WIDTH = 8 * LANES
TM = 512
TQ = 512
ATTN_TQ = 256
N_LATE_CHAINS = 1
ONES_ROWS = 16
ATTN_ROWS = 64
LOG2E = 1.4426950408889634
MOE_BLOCK = 256
MOE_ROWS = N_TOK * TOP_K + N_EXPERTS * MOE_BLOCK
MOE_NBLOCKS = MOE_ROWS // MOE_BLOCK
N_PLANES = D_MODEL // 2 // LANES
SC_WINDOW = 128
SC_INFLIGHT = 4
NEG = -1e30
VMEM_LIMIT = 56 * 1024 * 1024

F32 = jnp.float32
BF16 = jnp.bfloat16


def _bf16_round(x):
    return x.astype(BF16).astype(F32)


def _lane_iota(shape):
    return lax.broadcasted_iota(jnp.int32, shape, 1)


def _params(*sem):
    return pltpu.CompilerParams(dimension_semantics=sem, vmem_limit_bytes=VMEM_LIMIT)


def _pack_planes(y, o_ref):
    bits = lax.bitcast_convert_type(_bf16_round(y), jnp.uint32)
    half = D_MODEL // 2
    word = bits[:, half:] | lax.shift_right_logical(bits[:, :half], jnp.uint32(16))
    word = lax.bitcast_convert_type(word, jnp.int32)
    for p in range(N_PLANES):
        o_ref[p] = word[:, p * LANES:(p + 1) * LANES]


def _unpack_planes(planes):
    lo, hi = [], []
    for w in planes:
        u = lax.bitcast_convert_type(w, jnp.uint32)
        lo.append(lax.bitcast_convert_type(lax.shift_left(u, jnp.uint32(16)), F32))
        hi.append(lax.bitcast_convert_type(u & jnp.uint32(0xFFFF0000), F32))
    return jnp.concatenate(lo + hi, axis=1)


def _combine(route_ref, y_ref):
    rt = route_ref[...]
    y0 = _unpack_planes([y_ref[0, p] for p in range(N_PLANES)])
    y1 = _unpack_planes([y_ref[1, p] for p in range(N_PLANES)])
    return rt[:, 2:3] * y0 + rt[:, 3:4] * y1


def _mod_kernel(c_ref, w_ref, b_ref, o_ref):
    c = c_ref[...]
    cond = c / (1.0 + jnp.exp(-c))
    ch = cond.astype(BF16)
    cl = (cond - ch.astype(F32)).astype(BF16)
    w = w_ref[...]
    wh = w.astype(BF16)
    wl = (w - wh.astype(F32)).astype(BF16)
    acc = jnp.dot(ch, wh, preferred_element_type=F32)
    acc += jnp.dot(cl, wh, preferred_element_type=F32)
    acc += jnp.dot(ch, wl, preferred_element_type=F32)
    o_ref[...] = acc + b_ref[...]


def _modulation(c, w_ada, b_ada):
    rows = 16
    tn = 1536
    c_pad = jnp.zeros((rows, D_MODEL), F32).at[:BATCH].set(c)
    out = pl.pallas_call(
        _mod_kernel,
        grid=(DEPTH, 6 * D_MODEL // tn),
        in_specs=[
            pl.BlockSpec((rows, D_MODEL), lambda l, n: (0, 0)),
            pl.BlockSpec((None, D_MODEL, tn), lambda l, n: (l, 0, n)),
            pl.BlockSpec((None, 1, tn), lambda l, n: (l, 0, n)),
        ],
        out_specs=pl.BlockSpec((None, rows, tn), lambda l, n: (l, 0, n)),
        out_shape=jax.ShapeDtypeStruct((DEPTH, rows, 6 * D_MODEL), F32),
        compiler_params=_params("arbitrary", "arbitrary"),
        name="adaln_mod",
    )(c_pad, w_ada, b_ada.reshape(DEPTH, 1, 6 * D_MODEL))
    return out[:, :BATCH]


def _rope_kernel(pos_ref, inv_ref, c_ref, sa_ref, sb_ref):
    ang = pos_ref[...].astype(F32) * inv_ref[...]
    j = _lane_iota(ang.shape) % HEAD_DIM
    cosv = jnp.cos(ang)
    sinv = jnp.sin(ang)
    half = ROT_DIM // 2
    c_ref[...] = jnp.where(j < ROT_DIM, cosv, 1.0)
    sa_ref[...] = jnp.where(j < half, -sinv, 0.0)
    sb_ref[...] = jnp.where((j >= half) & (j < ROT_DIM), sinv, 0.0)


def _rope_tables(positions):
    half = ROT_DIM // 2
    inv = ROPE_THETA ** (-jnp.arange(0, ROT_DIM, 2, dtype=F32) / ROT_DIM)
    lane = np.arange(LANES)
    inv_lane = inv[(lane % HEAD_DIM) % half].reshape(1, LANES)
    spec = pl.BlockSpec((TM, LANES), lambda i: (i, 0))
    shape = jax.ShapeDtypeStruct((N_TOK, LANES), F32)
    return pl.pallas_call(
        _rope_kernel,
        grid=(N_TOK // TM,),
        in_specs=[pl.BlockSpec((TM, 1), lambda i: (i, 0)),
                  pl.BlockSpec((1, LANES), lambda i: (0, 0))],
        out_specs=[spec, spec, spec],
        out_shape=[shape, shape, shape],
        compiler_params=_params("arbitrary"),
        name="rope_tables",
    )(positions.reshape(N_TOK, 1), inv_lane)


def _rms_mod(x, g, sc, sh):
    ms = jnp.mean(x * x, axis=-1, keepdims=True)
    return (x * lax.rsqrt(ms + EPS) * g) * (1.0 + sc) + sh


def _inproj_kernel(fuse, *refs):
    if fuse:
        (x_ref, rt_ref, y_ref, gt_ref, sc_ref, sh_ref, g_ref, w_ref, bf_ref, c_ref, sa_ref, sb_ref,
         pq_ref, xo_ref, dq_ref, dk_ref, dv_ref, fq_ref, fk_ref, fv_ref, carry_ref) = refs
        x = x_ref[...] + gt_ref[...] * _combine(rt_ref, y_ref)
        xo_ref[...] = x
    else:
        (x_ref, sc_ref, sh_ref, g_ref, w_ref, bf_ref, c_ref, sa_ref, sb_ref,
         pq_ref, dq_ref, dk_ref, dv_ref, fq_ref, fk_ref, fv_ref, carry_ref) = refs
        x = x_ref[...]
    hb = _rms_mod(x, g_ref[...], sc_ref[...], sh_ref[...]).astype(BF16)

    @pl.when(pl.program_id(0) % (SEQ // TM) == 0)
    def _():
        carry_ref[...] = jnp.zeros_like(carry_ref)

    lane = _lane_iota((TM, LANES))
    nh = N_FOX_HEADS

    def pack3(a):
        hi = _bf16_round(a)
        r1 = a - hi
        mid = _bf16_round(r1)
        lo = _bf16_round(r1 - mid)
        return jnp.where(lane < nh, hi,
                         jnp.where(lane < 2 * nh, pltpu.roll(mid, nh, 1),
                                   jnp.where(lane < 3 * nh, pltpu.roll(lo, 2 * nh, 1), 0.0)))

    z = jnp.dot(hb, w_ref[:, FF_COL:FF_COL + LANES], preferred_element_type=F32) + bf_ref[...]

    low = lane < HEAD_DIM
    rc, rsa, rsb = c_ref[...], sa_ref[...], sb_ref[...]
    scale = HEAD_DIM ** -0.5 * LOG2E

    def split_store(chunk, o_ref, m, extra_a=None, extra_b=None):
        a = jnp.where(low, chunk, 0.0)
        b = jnp.where(low, pltpu.roll(chunk, HEAD_DIM, 1), 0.0)
        if extra_a is not None:
            a = a + extra_a
            b = b + extra_b
        o_ref[:, (2 * m) * LANES:(2 * m + 1) * LANES] = a.astype(BF16)
        o_ref[:, (2 * m + 1) * LANES:(2 * m + 2) * LANES] = b.astype(BF16)

    def rope(xc):
        return xc * rc + pltpu.roll(xc, LANES - ROT_DIM // 2, 1) * rsa + pltpu.roll(xc, ROT_DIM // 2, 1) * rsb

    pdq = jnp.dot(hb, w_ref[:, 0:DIFF_WIDTH], preferred_element_type=F32)
    for m in range(N_DIFF_HEADS):
        split_store(rope(pdq[:, m * LANES:(m + 1) * LANES]) * scale, dq_ref, m)
    pdk = jnp.dot(hb, w_ref[:, DIFF_WIDTH:2 * DIFF_WIDTH], preferred_element_type=F32)
    for m in range(N_DIFF_HEADS):
        split_store(rope(pdk[:, m * LANES:(m + 1) * LANES]), dk_ref, m)
    def store_values_t(pv, o_ref, width):
        ones = jnp.ones((ONES_ROWS, TM), BF16)
        for m in range(4):
            vt = pv[:, m * LANES:(m + 1) * LANES].T.astype(BF16)
            for i in range(LANES // width):
                o_ref[m * (LANES // width) + i, 0:width, :] = vt[i * width:(i + 1) * width]
                o_ref[m * (LANES // width) + i, width:width + ONES_ROWS, :] = ones

    store_values_t(jnp.dot(hb, w_ref[:, 2 * DIFF_WIDTH:3 * DIFF_WIDTH], preferred_element_type=F32),
                   dv_ref, 2 * HEAD_DIM)
    o = 3 * DIFF_WIDTH
    store_values_t(jnp.dot(hb, w_ref[:, o + 2 * FOX_WIDTH:o + 3 * FOX_WIDTH],
                           preferred_element_type=F32), fv_ref, HEAD_DIM)

    logf =jnp.minimum(z, 0.0) - jnp.log(1.0 + jnp.exp(-jnp.abs(z)))
    logf = jnp.where(lane < nh, logf, 0.0)
    row = lax.broadcasted_iota(jnp.int32, (TM, TM), 0)
    col = lax.broadcasted_iota(jnp.int32, (TM, TM), 1)
    tri = (row >= col).astype(BF16)
    r = jnp.dot(tri, pack3(logf).astype(BF16), preferred_element_type=F32)
    cs = r + pltpu.roll(r, LANES - nh, 1) + pltpu.roll(r, LANES - 2 * nh, 1)
    cf = jnp.where(lane < nh, cs + carry_ref[0:1, :], 0.0)
    carry_ref[...] = jnp.broadcast_to(cf[TM - 1:TM, :], carry_ref.shape)

    t3 = jnp.where(lane == 3 * nh, 1.0, pack3(cf * LOG2E)).astype(BF16)
    aug = jnp.dot(t3, pq_ref[...], preferred_element_type=F32)

    pfq =jnp.dot(hb, w_ref[:, o:o + FOX_WIDTH], preferred_element_type=F32)
    for m in range(N_FOX_HEADS // 2):
        split_store(pfq[:, m * LANES:(m + 1) * LANES] * scale, fq_ref, m,
                    aug[:, (2 * m) * LANES:(2 * m + 1) * LANES],
                    aug[:, (2 * m + 1) * LANES:(2 * m + 2) * LANES])
    pfk = jnp.dot(hb, w_ref[:, o + FOX_WIDTH:o + 2 * FOX_WIDTH], preferred_element_type=F32)
    for m in range(N_FOX_HEADS // 2):
        split_store(pfk[:, m * LANES:(m + 1) * LANES], fk_ref, m,
                    aug[:, QK_WIDTH + (2 * m) * LANES:QK_WIDTH + (2 * m + 1) * LANES],
                    aug[:, QK_WIDTH + (2 * m + 1) * LANES:QK_WIDTH + (2 * m + 2) * LANES])


def _forget_placement():
    nh = N_FOX_HEADS
    p = np.zeros((LANES, 2 * QK_WIDTH), np.float32)
    for h in range(nh):
        base_q = h * LANES + HEAD_DIM
        base_k = QK_WIDTH + h * LANES + HEAD_DIM
        for part in range(3):
            p[part * nh + h, base_q + part] = 1.0
            p[3 * nh, base_q + 3 + part] = 1.0
            p[3 * nh, base_k + part] = 1.0
            p[part * nh + h, base_k + 3 + part] = -1.0
    return jnp.asarray(p, BF16)


def _inproj(x, moe, gate, sc, sh, g, w_bf, b_forget, tables, pq):
    fuse = moe is not None
    tpb = SEQ // TM
    row = pl.BlockSpec((TM, D_MODEL), lambda i: (i, 0))
    per_batch = pl.BlockSpec((None, 1, D_MODEL), lambda i: (i // tpb, 0, 0))
    const = lambda shape: pl.BlockSpec(shape, lambda i: (0,) * len(shape))
    tab = pl.BlockSpec((TM, LANES), lambda i: (i, 0))
    in_specs = [row]
    args = [x]
    if fuse:
        in_specs += [tab, pl.BlockSpec((TOP_K, N_PLANES, TM, LANES), lambda i: (0, 0, i, 0)), per_batch]
        args += [moe[0], moe[1], gate]
    in_specs += [per_batch, per_batch, const((1, D_MODEL)), const((D_MODEL, IN_COLS_PAD)),
                 const((1, LANES)), tab, tab, tab, const((LANES, 2 * QK_WIDTH))]
    args += [sc, sh, g.reshape(1, D_MODEL), w_bf, b_forget, *tables, pq]
    wide = pl.BlockSpec((TM, QK_WIDTH), lambda i: (i, 0))
    def vspec(heads, width):
        rows = width + ONES_ROWS
        return (pl.BlockSpec((None, heads, None, rows, TM), lambda i: (i // tpb, 0, i % tpb, 0, 0)),
                jax.ShapeDtypeStruct((BATCH, heads, tpb, rows, TM), BF16))

    wide_s = jax.ShapeDtypeStruct((N_TOK, QK_WIDTH), BF16)
    dv_spec, dv_s = vspec(N_DIFF_HEADS, 2 * HEAD_DIM)
    fv_spec, fv_s = vspec(N_FOX_HEADS, HEAD_DIM)
    out_specs = [wide, wide, dv_spec, wide, wide, fv_spec]
    out_shape = [wide_s, wide_s, dv_s, wide_s, wide_s, fv_s]
    if fuse:
        out_specs = [row] + out_specs
        out_shape = [jax.ShapeDtypeStruct((N_TOK, D_MODEL), F32)] + out_shape
    outs = pl.pallas_call(
        functools.partial(_inproj_kernel, fuse),
        grid=(N_TOK // TM,),
        in_specs=in_specs,
        out_specs=out_specs,
        out_shape=out_shape,
        scratch_shapes=[pltpu.VMEM((8, LANES), F32)],
        compiler_params=_params("arbitrary"),
        name="norm_inproj",
    )(*args)
    if fuse:
        return outs[0], outs[1:]
    return x, outs


def _attn_kernel(diff, lambda_init, qa_ref, qb_ref, ka_ref, kb_ref, v_ref, g_ref, lam_ref, o_ref,
                 *scratch):
    nq = SEQ // TQ
    n_half = TQ // ATTN_TQ
    feat = 2 * HEAD_DIM if diff else HEAD_DIM
    chains = []
    for mi, (q_ref, k_ref) in enumerate(((qa_ref, ka_ref), (qb_ref, kb_ref))):
        for h in range(n_half):
            c = mi * n_half + h
            qt_sc, s_sc, p_sc, m_sc, a_sc, acc_sc = scratch[c::2 * n_half]
            vh = 0 if diff else mi
            chains.append((h, k_ref, qt_sc, s_sc, p_sc, m_sc, a_sc, acc_sc, q_ref, vh))
    order = [chains[mi * n_half + h] for h in range(n_half) for mi in range(2)]
    early, late = order[:-N_LATE_CHAINS], order[-N_LATE_CHAINS:]

    def load_queries(qi):
        for mi in range(2):
            q_ref = chains[mi * n_half][8]
            qt = q_ref[pl.ds(pl.multiple_of(qi * TQ, TQ), TQ), :].astype(F32).T.astype(BF16)
            for h in range(n_half):
                chains[mi * n_half + h][2][...] = qt[:, h * ATTN_TQ:(h + 1) * ATTN_TQ]

    def reset_state():
        for chain in chains:
            m_sc, _, acc_sc = chain[5:8]
            m_sc[...] = jnp.full(m_sc.shape, NEG, F32)
            acc_sc[...] = jnp.zeros(acc_sc.shape, F32)

    def n_keys(chain, masked):
        return (chain[0] + 1) * ATTN_TQ if masked else TQ

    def scores(chain, j, masked):
        h, k_ref, qt_sc, s_sc = chain[:4]
        nk = n_keys(chain, masked)
        off = pl.multiple_of(j * TQ, TQ)
        s = jnp.dot(k_ref[pl.ds(off, nk), :], qt_sc[...], preferred_element_type=F32)
        if masked:
            kk = lax.broadcasted_iota(jnp.int32, (nk, ATTN_TQ), 0)
            qq = h * ATTN_TQ + lax.broadcasted_iota(jnp.int32, (nk, ATTN_TQ), 1)
            s = jnp.where((kk // CHUNK <= qq // CHUNK) if diff else (kk <= qq), s, NEG)
        s_sc[0:nk, :] = s

    def softmax(chain, masked):
        s_sc, p_sc, m_sc, a_sc = chain[3:7]
        nk = n_keys(chain, masked)
        m_all = m_sc[...]
        m_parts = []
        for c0 in range(0, ATTN_TQ, LANES):
            cols = slice(c0, c0 + LANES)
            pm = s_sc[0:ATTN_ROWS, cols]
            for r0 in range(ATTN_ROWS, nk, ATTN_ROWS):
                pm = jnp.maximum(pm, s_sc[r0:r0 + ATTN_ROWS, cols])
            m_new = jnp.maximum(m_all[:, cols], jnp.max(pm, axis=0, keepdims=True))
            for r0 in range(0, nk, ATTN_ROWS):
                p = jnp.exp2(s_sc[r0:r0 + ATTN_ROWS, cols] - m_new)
                p_sc[r0:r0 + ATTN_ROWS, cols] = p.astype(BF16)
            m_parts.append(m_new)
        m_new = jnp.concatenate(m_parts, axis=1)
        a_sc[...] = jnp.exp2(m_all - m_new)
        m_sc[...] = m_new

    def values(chain, j, masked=False):
        p_sc, a_sc, acc_sc, vh = chain[4], chain[6], chain[7], chain[9]
        nk = n_keys(chain, masked)
        pv = jnp.dot(v_ref[vh, j, :, 0:nk], p_sc[0:nk, :], preferred_element_type=F32)
        acc_sc[...] = a_sc[...] * acc_sc[...] + pv

    def idle_late():
        for chain in late:
            chain[4][...] = jnp.zeros(chain[4].shape, BF16)
            chain[6][...] = jnp.ones(chain[6].shape, F32)

    def consume(j, cur_masked=False, nxt=None, nxt_masked=False, final=False, before_next=None):
        def open_late(chain):
            scores(chain, j, cur_masked)
            values(chain, jnp.maximum(j - 1, 0))

        open_late(late[0])
        for i, chain in enumerate(early):
            softmax(chain, cur_masked)
            if i == 0:
                for other in late[1:]:
                    open_late(other)
                if before_next is not None:
                    before_next()
            if nxt is not None:
                scores(chain, nxt, nxt_masked)
            values(chain, j, cur_masked)
        for chain in late:
            softmax(chain, cur_masked)
        if final:
            for chain in late:
                values(chain, j, cur_masked)

    def finalize(qi):
        ot = [jnp.concatenate([chains[mi * n_half + h][7][0:feat] / chains[mi * n_half + h][7][feat:feat + 1]
                               for h in range(n_half)], axis=1) for mi in range(2)]
        g = g_ref[...]
        rows = pl.ds(pl.multiple_of(qi * TQ, TQ), TQ)
        if diff:
            lv = lam_ref[...]
            lam = (jnp.exp(jnp.sum(lv[0:1] * lv[1:2], axis=1, keepdims=True))
                   - jnp.exp(jnp.sum(lv[2:3] * lv[3:4], axis=1, keepdims=True)) + lambda_init)
            o = (ot[0] - lam * ot[1]).T
            y = o * lax.rsqrt(jnp.mean(o * o, axis=1, keepdims=True) + EPS) * g
            o_ref[rows, :] = (y * (1.0 - lambda_init)).astype(o_ref.dtype)
        else:
            o = jnp.concatenate(ot, axis=0).T
            low = _lane_iota((TQ, LANES)) < HEAD_DIM
            sq = o * o
            msa = jnp.sum(jnp.where(low, sq, 0.0), axis=1, keepdims=True) / HEAD_DIM
            msb = jnp.sum(jnp.where(low, 0.0, sq), axis=1, keepdims=True) / HEAD_DIM
            inv = jnp.where(low, lax.rsqrt(msa + EPS), lax.rsqrt(msb + EPS))
            o_ref[rows, :] = (o * inv * g).astype(o_ref.dtype)

    load_queries(0)
    reset_state()
    idle_late()
    for chain in early:
        scores(chain, 0, True)

    @pl.loop(0, nq)
    def _(qi):
        n_plain = jnp.maximum(qi - 1, 0)

        @pl.loop(0, n_plain // 2)
        def _(t):
            consume(2 * t, nxt=2 * t + 1)
            consume(2 * t + 1, nxt=2 * t + 2)

        @pl.when(n_plain % 2 == 1)
        def _():
            consume(qi - 2, nxt=qi - 1)

        def last_blocks(to_next_tile):
            if to_next_tile:
                consume(qi, cur_masked=True, nxt=0, final=True, before_next=lambda: load_queries(qi + 1))
            else:
                consume(qi, cur_masked=True, final=True)
            finalize(qi)

        for to_next_tile in (True, False):
            more = (qi < nq - 1) if to_next_tile else (qi == nq - 1)

            @pl.when(more & (qi > 0))
            def _():
                consume(qi - 1, nxt=qi, nxt_masked=True)
                last_blocks(to_next_tile)

            if to_next_tile:
                @pl.when(qi == 0)
                def _():
                    last_blocks(to_next_tile)

        reset_state()
        idle_late()


def _attention(diff, lambda_init, q, k, v, g, lamv):
    nq = SEQ // TQ
    kspec = lambda par: pl.BlockSpec((SEQ, LANES), lambda b, p: (b, 2 * p + par))
    return pl.pallas_call(
        functools.partial(_attn_kernel, diff, lambda_init),
        grid=(BATCH, 4),
        in_specs=[kspec(0), kspec(1), kspec(0), kspec(1),
                  pl.BlockSpec((None, v.shape[1] // 4, nq, v.shape[3], TQ), lambda b, p: (b, p, 0, 0, 0)),
                  pl.BlockSpec((1, LANES), lambda b, p: (0, 0)),
                  pl.BlockSpec((8, LANES), lambda b, p: (0, 0))],
        out_specs=pl.BlockSpec((SEQ, LANES), lambda b, p: (b, p)),
        out_shape=jax.ShapeDtypeStruct((N_TOK, DIFF_WIDTH), BF16),
        scratch_shapes=[pltpu.VMEM(shape, dt)
                        for shape, dt in (((LANES, ATTN_TQ), BF16), ((TQ, ATTN_TQ), F32),
                                          ((TQ, ATTN_TQ), BF16), ((1, ATTN_TQ), F32),
                                          ((1, ATTN_TQ), F32), ((v.shape[3], ATTN_TQ), F32))
                        for _ in range(2 * TQ // ATTN_TQ)],
        compiler_params=_params("arbitrary", "arbitrary"),
        name="diff_attention" if diff else "fox_attention",
    )(q, q, k, k, v, g, lamv)


def _outproj_kernel(x_ref, od_ref, of_ref, gt_ref, sc_ref, sh_ref, g_ref, wo_ref, wr_ref, br_ref,
                    x1_ref, h2_ref, rt_ref, cnt_ref, carry_ref):
    @pl.when(pl.program_id(0) == 0)
    def _():
        carry_ref[...] = jnp.zeros_like(carry_ref)

    mix = jnp.dot(od_ref[...], wo_ref[0:DIFF_WIDTH, :], preferred_element_type=F32)
    mix += jnp.dot(of_ref[...], wo_ref[DIFF_WIDTH:, :], preferred_element_type=F32)
    x1 = x_ref[...] + gt_ref[...] * mix
    x1_ref[...] = x1
    h = _rms_mod(x1, g_ref[...], sc_ref[...], sh_ref[...])
    hh = h.astype(BF16)
    _pack_planes(h, h2_ref)
    hl = (h - hh.astype(F32)).astype(BF16)
    r1 = jnp.dot(hh, wr_ref[...], preferred_element_type=F32)
    r2 = jnp.dot(hl, wr_ref[:, 0:LANES], preferred_element_type=F32)
    logits = r1[:, 0:LANES] + r1[:, LANES:] + r2 + br_ref[...]

    lane = _lane_iota((TM, LANES))
    lanef = lane.astype(F32)
    big = float(LANES)
    isg = lane < N_GROUPS
    lg = jnp.where(isg, logits, NEG)
    mg = jnp.max(lg, axis=1, keepdims=True)
    sg = jnp.sum(jnp.where(isg, jnp.exp(lg - mg), 0.0), axis=1, keepdims=True)
    p_g = 1.0 / sg
    gsel = jnp.min(jnp.where(isg & (lg == mg), lanef, big), axis=1, keepdims=True)
    lo = N_GROUPS + gsel * EXPERTS_PER_GROUP
    ise = (lanef >= lo) & (lanef < lo + EXPERTS_PER_GROUP)
    le = jnp.where(ise, logits, NEG)
    t1 = jnp.max(le, axis=1, keepdims=True)
    i1 = jnp.min(jnp.where(ise & (le == t1), lanef, big), axis=1, keepdims=True)
    ise2 = ise & (lanef != i1)
    le2 = jnp.where(ise2, logits, NEG)
    t2 = jnp.max(le2, axis=1, keepdims=True)
    i2 = jnp.min(jnp.where(ise2 & (le2 == t2), lanef, big), axis=1, keepdims=True)
    d = jnp.exp(t2 - t1)
    w1 = p_g / (1.0 + d)
    w2 = p_g * d / (1.0 + d)
    e1 = i1 - N_GROUPS
    e2 = i2 - N_GROUPS
    oh1 = lanef == e1
    oh2 = lanef == e2
    both = jnp.where(oh1 | oh2, 1.0, 0.0)
    row = lax.broadcasted_iota(jnp.int32, (TM, TM), 0)
    col = lax.broadcasted_iota(jnp.int32, (TM, TM), 1)
    before = jnp.dot((row > col).astype(BF16), both.astype(BF16), preferred_element_type=F32)
    before = before + carry_ref[0:1, :]
    rank1 = jnp.sum(jnp.where(oh1, before, 0.0), axis=1, keepdims=True)
    rank2 = jnp.sum(jnp.where(oh2, before, 0.0), axis=1, keepdims=True)
    total = carry_ref[0:1, :] + jnp.sum(both, axis=0, keepdims=True)
    carry_ref[...] = jnp.broadcast_to(total, carry_ref.shape)
    cnt_ref[...] = jnp.broadcast_to(total, cnt_ref.shape)
    vals = (e1, e2, w1, w2, rank1, rank2)
    out = jnp.zeros((TM, LANES), F32)
    for j, v in enumerate(vals):
        out = jnp.where(lane == j, v, out)
    rt_ref[...] = out


def _outproj(x, od, of, gt, sc, sh, g, wo_bf, wr, br):
    tpb = SEQ // TM
    row = pl.BlockSpec((TM, D_MODEL), lambda i: (i, 0))
    half = pl.BlockSpec((TM, DIFF_WIDTH), lambda i: (i, 0))
    per_batch = pl.BlockSpec((None, 1, D_MODEL), lambda i: (i // tpb, 0, 0))
    const = lambda shape: pl.BlockSpec(shape, lambda i: (0,) * len(shape))
    return pl.pallas_call(
        _outproj_kernel,
        grid=(N_TOK // TM,),
        in_specs=[row, half, half, per_batch, per_batch, per_batch, const((1, D_MODEL)),
                  const((D_MODEL, D_MODEL)), const((D_MODEL, 2 * LANES)), const((1, LANES))],
        out_specs=[row, pl.BlockSpec((N_PLANES, TM, LANES), lambda i: (0, i, 0)),
                   pl.BlockSpec((TM, LANES), lambda i: (i, 0)), const((8, LANES))],
        out_shape=[jax.ShapeDtypeStruct((N_TOK, D_MODEL), F32),
                   jax.ShapeDtypeStruct((N_PLANES, N_TOK, LANES), jnp.int32),
                   jax.ShapeDtypeStruct((N_TOK, LANES), F32),
                   jax.ShapeDtypeStruct((8, LANES), F32)],
        scratch_shapes=[pltpu.VMEM((8, LANES), F32)],
        compiler_params=_params("arbitrary"),
        name="outproj_router",
    )(x, od, of, gt, sc, sh, g.reshape(1, D_MODEL), wo_bf, wr, br)


def _expert_kernel(be_ref, cnt_ref, xs_ref, wg_ref, wu_ref, wd_ref, ys_ref, wg_sc, wu_sc, wd_sc):
    i = pl.program_id(0)
    cnt = cnt_ref[i]

    @pl.when((i == 0) | (be_ref[i] != be_ref[jnp.maximum(i - 1, 0)]))
    def _():
        wg_sc[...] = wg_ref[...].astype(BF16)
        wu_sc[...] = wu_ref[...].astype(BF16)
        wd_sc[...] = wd_ref[...].astype(BF16)

    @pl.when(cnt > 0)
    def _():
        live = lax.broadcasted_iota(jnp.int32, (MOE_BLOCK, LANES), 0) < cnt
        xb = _unpack_planes([jnp.where(live, xs_ref[p], 0) for p in range(N_PLANES)]).astype(BF16)
        a = jnp.dot(xb, wg_sc[...], preferred_element_type=F32)
        u = jnp.dot(xb, wu_sc[...], preferred_element_type=F32)
        hid = (a / (1.0 + jnp.exp(-a)) * u).astype(BF16)
        _pack_planes(jnp.dot(hid, wd_sc[...], preferred_element_type=F32), ys_ref)

    @pl.when(cnt == 0)
    def _():
        ys_ref[...] = jnp.zeros_like(ys_ref)


def _experts(layer, block_expert, block_count, xs, wg, wu, wd):
    planes = pl.BlockSpec((N_PLANES, MOE_BLOCK, LANES), lambda i, be, bc: (0, i, 0))
    w_in = pl.BlockSpec((None, None, D_MODEL, D_EXPERT), lambda i, be, bc: (layer, be[i], 0, 0))
    w_out = pl.BlockSpec((None, None, D_EXPERT, D_MODEL), lambda i, be, bc: (layer, be[i], 0, 0))
    grid_spec = pltpu.PrefetchScalarGridSpec(
        num_scalar_prefetch=2,
        grid=(MOE_NBLOCKS,),
        in_specs=[planes, w_in, w_in, w_out],
        out_specs=planes,
        scratch_shapes=[pltpu.VMEM((D_MODEL, D_EXPERT), BF16), pltpu.VMEM((D_MODEL, D_EXPERT), BF16),
                        pltpu.VMEM((D_EXPERT, D_MODEL), BF16)],
    )
    return pl.pallas_call(
        _expert_kernel,
        grid_spec=grid_spec,
        out_shape=jax.ShapeDtypeStruct((N_PLANES, MOE_ROWS, LANES), jnp.int32),
        compiler_params=_params("arbitrary"),
        name="expert_mlp",
    )(block_expert, block_count, xs, wg, wu, wd)


def _slots(route, counts):
    counts = counts[0, :N_EXPERTS].astype(jnp.int32)
    padded = ((counts + MOE_BLOCK - 1) // MOE_BLOCK) * MOE_BLOCK
    pend = jnp.cumsum(padded)
    pstart = pend - padded
    bstart = jnp.arange(MOE_NBLOCKS, dtype=jnp.int32) * MOE_BLOCK
    block_expert = jnp.minimum(jnp.sum(bstart[:, None] >= pend[None, :], axis=1), N_EXPERTS - 1)
    block_expert = block_expert.astype(jnp.int32)
    mine = block_expert[:, None] == jnp.arange(N_EXPERTS, dtype=jnp.int32)[None, :]
    left = jnp.sum(jnp.where(mine, counts + pstart, 0), axis=1) - bstart
    block_count = jnp.clip(left, 0, MOE_BLOCK).astype(jnp.int32)
    base = jnp.pad(pstart.astype(F32), (0, LANES - N_EXPERTS)).reshape(1, LANES)
    return _slot_rows(route, base), block_expert, block_count


def _slot_rows_kernel(rt_ref, base_ref, o_ref):
    rt = rt_ref[...]
    lanef = _lane_iota(rt.shape).astype(F32)
    base = base_ref[...]
    dest = jnp.zeros(rt.shape, F32)
    for k in range(TOP_K):
        b = jnp.sum(jnp.where(lanef == rt[:, k:k + 1], base, 0.0), axis=1, keepdims=True)
        dest = jnp.where(lanef == k, b + rt[:, 2 * TOP_K + k:2 * TOP_K + k + 1], dest)
    dest_t = dest.T.astype(jnp.int32)
    for k in range(TOP_K):
        for p in range(N_PLANES):
            for c in range(rt.shape[0] // LANES):
                o_ref[k * N_PLANES + p, c:c + 1, :] = dest_t[k:k + 1, c * LANES:(c + 1) * LANES] + p * MOE_ROWS


def _slot_rows(route, base):
    tm = 8 * LANES
    return pl.pallas_call(
        _slot_rows_kernel,
        grid=(N_TOK // tm,),
        in_specs=[pl.BlockSpec((tm, LANES), lambda i: (i, 0)), pl.BlockSpec((1, LANES), lambda i: (0, 0))],
        out_specs=pl.BlockSpec((TOP_K * N_PLANES, tm // LANES, LANES), lambda i: (0, i, 0)),
        out_shape=jax.ShapeDtypeStruct((TOP_K * N_PLANES, N_TOK // LANES, LANES), jnp.int32),
        compiler_params=_params("arbitrary"),
        name="slot_rows",
    )(route, base)


def _sc_workers():
    info = plsc.get_sparse_core_info()
    return info.num_cores, info.num_cores * info.num_subcores


def _sc_scatter2(src, idx, out_rows):
    n_win = src.shape[0] // SC_WINDOW
    nc, nw = _sc_workers()
    steps = n_win // nw
    mesh = plsc.VectorSubcoreMesh(core_axis_name="c", subcore_axis_name="s")

    @functools.partial(
        pl.kernel, mesh=mesh,
        out_type=jax.ShapeDtypeStruct((out_rows, LANES), src.dtype),
        scratch_types=[pltpu.VMEM((SC_WINDOW,), jnp.int32), pltpu.VMEM((SC_WINDOW,), jnp.int32),
                       pltpu.VMEM((SC_WINDOW, LANES), src.dtype)],
        name="sc_dispatch_scatter",
    )
    def k(src_hbm, idx_hbm, out_hbm, ia_v, ib_v, rows_v):
        first = (lax.axis_index("s") * nc + lax.axis_index("c")) * steps

        @pl.loop(0, steps)
        def _(j):
            w = first + j
            pltpu.sync_copy(idx_hbm.at[w], ia_v)
            pltpu.sync_copy(idx_hbm.at[n_win + w], ib_v)
            pltpu.sync_copy(src_hbm.at[pl.ds(w * SC_WINDOW, SC_WINDOW)], rows_v)
            pltpu.sync_copy(rows_v, out_hbm.at[ia_v])
            pltpu.sync_copy(rows_v, out_hbm.at[ib_v])

    return k(src, idx)


def _sc_gather(table, idx):
    n_out = idx.shape[0] * SC_WINDOW
    nc, nw = _sc_workers()
    steps = n_out // nw // SC_WINDOW
    mesh = plsc.VectorSubcoreMesh(core_axis_name="c", subcore_axis_name="s")

    @functools.partial(
        pl.kernel, mesh=mesh,
        out_type=jax.ShapeDtypeStruct((n_out, LANES), table.dtype),
        scratch_types=[pltpu.VMEM((steps, SC_WINDOW), jnp.int32),
                       pltpu.VMEM((SC_INFLIGHT, SC_WINDOW, LANES), table.dtype),
                       pltpu.SemaphoreType.DMA((SC_INFLIGHT,)), pltpu.SemaphoreType.DMA((SC_INFLIGHT,))],
        name="sc_combine_gather",
    )
    def k(table_hbm, idx_hbm, out_hbm, idx_v, rows_v, gsem, wsem):
        first = (lax.axis_index("s") * nc + lax.axis_index("c")) * steps
        pltpu.sync_copy(idx_hbm.at[pl.ds(first, steps)], idx_v)

        @pl.loop(0, steps, step=SC_INFLIGHT)
        def _(j):
            gathers = [pltpu.async_copy(table_hbm.at[idx_v.at[j + b]], rows_v.at[b], gsem.at[b])
                       for b in range(SC_INFLIGHT)]
            writes = []
            for b in range(SC_INFLIGHT):
                gathers[b].wait()
                dst = out_hbm.at[pl.ds((first + j + b) * SC_WINDOW, SC_WINDOW)]
                writes.append(pltpu.async_copy(rows_v.at[b], dst, wsem.at[b]))
            for w in writes:
                w.wait()

    return k(table, idx)


def _final_kernel(x_ref, rt_ref, y_ref, gt_ref, g_ref, o_ref):
    x = x_ref[...] + gt_ref[...] * _combine(rt_ref, y_ref)
    ms = jnp.mean(x * x, axis=-1, keepdims=True)
    o_ref[...] = x * lax.rsqrt(ms + EPS) * g_ref[...]


def _final(x, moe, gate, g):
    tpb = SEQ // TM
    row = pl.BlockSpec((TM, D_MODEL), lambda i: (i, 0))
    return pl.pallas_call(
        _final_kernel,
        grid=(N_TOK // TM,),
        in_specs=[row, pl.BlockSpec((TM, LANES), lambda i: (i, 0)),
                  pl.BlockSpec((TOP_K, N_PLANES, TM, LANES), lambda i: (0, 0, i, 0)),
                  pl.BlockSpec((None, 1, D_MODEL), lambda i: (i // tpb, 0, 0)),
                  pl.BlockSpec((1, D_MODEL), lambda i: (0, 0))],
        out_specs=row,
        out_shape=jax.ShapeDtypeStruct((N_TOK, D_MODEL), F32),
        compiler_params=_params("arbitrary"),
        name="final_norm",
    )(x, moe[0], moe[1], gate, g.reshape(1, D_MODEL))


def kernel(x, c, positions, w_ada, b_ada, g_mix, w_in, b_forget, lambda_q1, lambda_k1, lambda_q2,
           lambda_k2, g_subln, g_fox_out, w_out, g_ffn, w_router_group, b_router_group,
           w_router_expert, b_router_expert, w_expert_gate, w_expert_up, w_expert_down, g_final):
    mod = _modulation(c, w_ada, b_ada)
    mod = mod.reshape(DEPTH, BATCH, 6, 1, D_MODEL)
    tables = _rope_tables(positions)
    pq = _forget_placement()
    xf = x.reshape(N_TOK, D_MODEL)
    moe = None
    gate = None
    for l in range(DEPTH):
        sh1, sc1, gt1, sh2, sc2, gt2 = (mod[l, :, j] for j in range(6))
        w_bf = jnp.pad(w_in[l], ((0, 0), (0, IN_COLS_PAD - IN_COLS))).astype(BF16)
        bfp = jnp.pad(b_forget[l], (0, LANES - N_FOX_HEADS)).reshape(1, LANES)
        xf, (dq, dk, dv, fq, fk, fv) = _inproj(xf, moe, gate, sc1, sh1, g_mix[l], w_bf, bfp, tables, pq)

        lambda_init = 0.8 - 0.6 * float(np.exp(-0.3 * l))
        lamv = jnp.zeros((8, LANES), F32).at[0:4, 0:HEAD_DIM].set(
            jnp.stack([lambda_q1[l], lambda_k1[l], lambda_q2[l], lambda_k2[l]]))
        g_d = g_subln[l].reshape(1, LANES)
        g_f = jnp.concatenate([g_fox_out[l], g_fox_out[l]]).reshape(1, LANES)
        od = _attention(True, lambda_init, dq, dk, dv, g_d, lamv)
        of = _attention(False, lambda_init, fq, fk, fv, g_f, lamv)

        wr32 = jnp.pad(jnp.concatenate([w_router_group[l], w_router_expert[l]], axis=1),
                       ((0, 0), (0, LANES - N_GROUPS - N_EXPERTS)))
        wr_hi = wr32.astype(BF16)
        wr_lo = (wr32 - wr_hi.astype(F32)).astype(BF16)
        wr = jnp.concatenate([wr_hi, wr_lo], axis=1)
        br = jnp.pad(jnp.concatenate([b_router_group[l], b_router_expert[l]]),
                     (0, LANES - N_GROUPS - N_EXPERTS)).reshape(1, LANES)
        xf, h2, route, counts = _outproj(xf, od, of, gt1, sc2, sh2, g_ffn[l], w_out[l].astype(BF16),
                                         wr, br)

        rows, block_expert, block_count = _slots(route, counts)
        rows = rows.reshape(TOP_K * N_PLANES * N_TOK // SC_WINDOW, SC_WINDOW)
        xs = _sc_scatter2(h2.reshape(N_PLANES * N_TOK, LANES), rows, N_PLANES * MOE_ROWS)
        ys = _experts(l, block_expert, block_count, xs.reshape(N_PLANES, MOE_ROWS, LANES),
                      w_expert_gate, w_expert_up, w_expert_down)
        y2 = _sc_gather(ys.reshape(N_PLANES * MOE_ROWS, LANES), rows)
        moe = (route, y2.reshape(TOP_K, N_PLANES, N_TOK, LANES))
        gate = gt2
    out = _final(xf, moe, gate, g_final)
    return out.reshape(BATCH, SEQ, D_MODEL)
```

```python
import functools

import numpy as np
import jax
import jax.numpy as jnp
from jax import lax
from jax.experimental import pallas as pl
from jax.experimental.pallas import tpu as pltpu
from jax.experimental.pallas import tpu_sc as plsc

D_MODEL = 1024
BATCH = 4
SEQ = 4096
DEPTH = 4
N_TOK = BATCH * SEQ

CHUNK = 64
HEAD_DIM = 64
N_DIFF_HEADS = 4
N_FOX_HEADS = 8
DIFF_WIDTH = 512
FOX_WIDTH = 512
IN_COLS = 3 * DIFF_WIDTH + 3 * FOX_WIDTH + N_FOX_HEADS
ROT_DIM = 16
ROPE_THETA = 500000.0
N_GROUPS = 4
EXPERTS_PER_GROUP = 8
N_EXPERTS = 32
TOP_K = 2
D_EXPERT = 512
EPS = 1e-6

LANES = 128
IN_COLS_PAD = 3200
FF_COL = 3 * DIFF_WIDTH + 3 * FOX_WIDTH
QK_WIDTH = 8 * LANES
TM = 512
TQ = 512
ATTN_TQ = 256
N_LATE_CHAINS = 1
ONES_ROWS = 16
ATTN_ROWS = 64
LOG2E = 1.4426950408889634
MOE_BLOCK = 512
MOE_ROWS = N_TOK * TOP_K + N_EXPERTS * MOE_BLOCK
MOE_NBLOCKS = MOE_ROWS // MOE_BLOCK
N_PLANES = D_MODEL // 2 // LANES
SC_WINDOW = 128
SC_INFLIGHT = 4
NEG = -1e30
VMEM_LIMIT = 56 * 1024 * 1024

F32 = jnp.float32
BF16 = jnp.bfloat16


def _bf16_round(x):
    return x.astype(BF16).astype(F32)


def _lane_iota(shape):
    return lax.broadcasted_iota(jnp.int32, shape, 1)


def _params(*sem):
    return pltpu.CompilerParams(dimension_semantics=sem, vmem_limit_bytes=VMEM_LIMIT)


def _pack_planes(y, o_ref):
    bits = lax.bitcast_convert_type(_bf16_round(y), jnp.uint32)
    half = D_MODEL // 2
    word = bits[:, half:] | lax.shift_right_logical(bits[:, :half], jnp.uint32(16))
    word = lax.bitcast_convert_type(word, jnp.int32)
    for p in range(N_PLANES):
        o_ref[p] = word[:, p * LANES:(p + 1) * LANES]


def _unpack_planes(planes):
    lo, hi = [], []
    for w in planes:
        u = lax.bitcast_convert_type(w, jnp.uint32)
        lo.append(lax.bitcast_convert_type(lax.shift_left(u, jnp.uint32(16)), F32))
        hi.append(lax.bitcast_convert_type(u & jnp.uint32(0xFFFF0000), F32))
    return jnp.concatenate(lo + hi, axis=1)


def _combine(route_ref, y_ref):
    rt = route_ref[...]
    y0 = _unpack_planes([y_ref[0, p] for p in range(N_PLANES)])
    y1 = _unpack_planes([y_ref[1, p] for p in range(N_PLANES)])
    return rt[:, 2:3] * y0 + rt[:, 3:4] * y1


def _mod_kernel(c_ref, w_ref, b_ref, o_ref):
    c = c_ref[...]
    cond = c / (1.0 + jnp.exp(-c))
    ch = cond.astype(BF16)
    cl = (cond - ch.astype(F32)).astype(BF16)
    w = w_ref[...]
    wh = w.astype(BF16)
    wl = (w - wh.astype(F32)).astype(BF16)
    acc = jnp.dot(ch, wh, preferred_element_type=F32)
    acc += jnp.dot(cl, wh, preferred_element_type=F32)
    acc += jnp.dot(ch, wl, preferred_element_type=F32)
    o_ref[...] = acc + b_ref[...]


def _modulation(c, w_ada, b_ada):
    rows = 16
    tn = 1536
    c_pad = jnp.zeros((rows, D_MODEL), F32).at[:BATCH].set(c)
    out = pl.pallas_call(
        _mod_kernel,
        grid=(DEPTH, 6 * D_MODEL // tn),
        in_specs=[
            pl.BlockSpec((rows, D_MODEL), lambda l, n: (0, 0)),
            pl.BlockSpec((None, D_MODEL, tn), lambda l, n: (l, 0, n)),
            pl.BlockSpec((None, 1, tn), lambda l, n: (l, 0, n)),
        ],
        out_specs=pl.BlockSpec((None, rows, tn), lambda l, n: (l, 0, n)),
        out_shape=jax.ShapeDtypeStruct((DEPTH, rows, 6 * D_MODEL), F32),
        compiler_params=_params("arbitrary", "arbitrary"),
        name="adaln_mod",
    )(c_pad, w_ada, b_ada.reshape(DEPTH, 1, 6 * D_MODEL))
    return out[:, :BATCH]


def _rope_kernel(pos_ref, inv_ref, c_ref, sa_ref, sb_ref):
    ang = pos_ref[...].astype(F32) * inv_ref[...]
    j = _lane_iota(ang.shape) % HEAD_DIM
    cosv = jnp.cos(ang)
    sinv = jnp.sin(ang)
    half = ROT_DIM // 2
    c_ref[...] = jnp.where(j < ROT_DIM, cosv, 1.0)
    sa_ref[...] = jnp.where(j < half, -sinv, 0.0)
    sb_ref[...] = jnp.where((j >= half) & (j < ROT_DIM), sinv, 0.0)


def _rope_tables(positions):
    half = ROT_DIM // 2
    inv = ROPE_THETA ** (-jnp.arange(0, ROT_DIM, 2, dtype=F32) / ROT_DIM)
    lane = np.arange(LANES)
    inv_lane = inv[(lane % HEAD_DIM) % half].reshape(1, LANES)
    spec = pl.BlockSpec((TM, LANES), lambda i: (i, 0))
    shape = jax.ShapeDtypeStruct((N_TOK, LANES), F32)
    return pl.pallas_call(
        _rope_kernel,
        grid=(N_TOK // TM,),
        in_specs=[pl.BlockSpec((TM, 1), lambda i: (i, 0)),
                  pl.BlockSpec((1, LANES), lambda i: (0, 0))],
        out_specs=[spec, spec, spec],
        out_shape=[shape, shape, shape],
        compiler_params=_params("arbitrary"),
        name="rope_tables",
    )(positions.reshape(N_TOK, 1), inv_lane)


def _rms_mod(x, g, sc, sh):
    ms = jnp.mean(x * x, axis=-1, keepdims=True)
    return (x * lax.rsqrt(ms + EPS) * g) * (1.0 + sc) + sh


def _inproj_kernel(fuse, *refs):
    if fuse:
        (x_ref, rt_ref, y_ref, gt_ref, sc_ref, sh_ref, g_ref, w_ref, bf_ref, c_ref, sa_ref, sb_ref,
         pq_ref, xo_ref, dq_ref, dk_ref, dv_ref, fq_ref, fk_ref, fv_ref, carry_ref) = refs
        x = x_ref[...] + gt_ref[...] * _combine(rt_ref, y_ref)
        xo_ref[...] = x
    else:
        (x_ref, sc_ref, sh_ref, g_ref, w_ref, bf_ref, c_ref, sa_ref, sb_ref,
         pq_ref, dq_ref, dk_ref, dv_ref, fq_ref, fk_ref, fv_ref, carry_ref) = refs
        x = x_ref[...]
    hb = _rms_mod(x, g_ref[...], sc_ref[...], sh_ref[...]).astype(BF16)

    @pl.when(pl.program_id(0) % (SEQ // TM) == 0)
    def _():
        carry_ref[...] = jnp.zeros_like(carry_ref)

    lane = _lane_iota((TM, LANES))
    nh = N_FOX_HEADS

    def pack3(a):
        hi = _bf16_round(a)
        r1 = a - hi
        mid = _bf16_round(r1)
        lo = _bf16_round(r1 - mid)
        return jnp.where(lane < nh, hi,
                         jnp.where(lane < 2 * nh, pltpu.roll(mid, nh, 1),
                                   jnp.where(lane < 3 * nh, pltpu.roll(lo, 2 * nh, 1), 0.0)))

    z = jnp.dot(hb, w_ref[:, FF_COL:FF_COL + LANES], preferred_element_type=F32) + bf_ref[...]

    low = lane < HEAD_DIM
    rc, rsa, rsb = c_ref[...], sa_ref[...], sb_ref[...]
    scale = HEAD_DIM ** -0.5 * LOG2E

    def split_store(chunk, o_ref, m, extra_a=None, extra_b=None):
        a = jnp.where(low, chunk, 0.0)
        b = jnp.where(low, pltpu.roll(chunk, HEAD_DIM, 1), 0.0)
        if extra_a is not None:
            a = a + extra_a
            b = b + extra_b
        o_ref[:, (2 * m) * LANES:(2 * m + 1) * LANES] = a.astype(BF16)
        o_ref[:, (2 * m + 1) * LANES:(2 * m + 2) * LANES] = b.astype(BF16)

    def rope(xc):
        return xc * rc + pltpu.roll(xc, LANES - ROT_DIM // 2, 1) * rsa + pltpu.roll(xc, ROT_DIM // 2, 1) * rsb

    pdq = jnp.dot(hb, w_ref[:, 0:DIFF_WIDTH], preferred_element_type=F32)
    for m in range(N_DIFF_HEADS):
        split_store(rope(pdq[:, m * LANES:(m + 1) * LANES]) * scale, dq_ref, m)
    pdk = jnp.dot(hb, w_ref[:, DIFF_WIDTH:2 * DIFF_WIDTH], preferred_element_type=F32)
    for m in range(N_DIFF_HEADS):
        split_store(rope(pdk[:, m * LANES:(m + 1) * LANES]), dk_ref, m)
    def store_values_t(pv, o_ref, width):
        ones = jnp.ones((ONES_ROWS, TM), BF16)
        for m in range(4):
            vt = pv[:, m * LANES:(m + 1) * LANES].T.astype(BF16)
            for i in range(LANES // width):
                o_ref[m * (LANES // width) + i, 0:width, :] = vt[i * width:(i + 1) * width]
                o_ref[m * (LANES // width) + i, width:width + ONES_ROWS, :] = ones

    store_values_t(jnp.dot(hb, w_ref[:, 2 * DIFF_WIDTH:3 * DIFF_WIDTH], preferred_element_type=F32),
                   dv_ref, 2 * HEAD_DIM)
    o = 3 * DIFF_WIDTH
    store_values_t(jnp.dot(hb, w_ref[:, o + 2 * FOX_WIDTH:o + 3 * FOX_WIDTH],
                           preferred_element_type=F32), fv_ref, HEAD_DIM)

    logf =jnp.minimum(z, 0.0) - jnp.log(1.0 + jnp.exp(-jnp.abs(z)))
    logf = jnp.where(lane < nh, logf, 0.0)
    row = lax.broadcasted_iota(jnp.int32, (TM, TM), 0)
    col = lax.broadcasted_iota(jnp.int32, (TM, TM), 1)
    tri = (row >= col).astype(BF16)
    r = jnp.dot(tri, pack3(logf).astype(BF16), preferred_element_type=F32)
    cs = r + pltpu.roll(r, LANES - nh, 1) + pltpu.roll(r, LANES - 2 * nh, 1)
    cf = jnp.where(lane < nh, cs + carry_ref[0:1, :], 0.0)
    carry_ref[...] = jnp.broadcast_to(cf[TM - 1:TM, :], carry_ref.shape)

    t3 = jnp.where(lane == 3 * nh, 1.0, pack3(cf * LOG2E)).astype(BF16)
    aug = jnp.dot(t3, pq_ref[...], preferred_element_type=F32)

    pfq =jnp.dot(hb, w_ref[:, o:o + FOX_WIDTH], preferred_element_type=F32)
    for m in range(N_FOX_HEADS // 2):
        split_store(pfq[:, m * LANES:(m + 1) * LANES] * scale, fq_ref, m,
                    aug[:, (2 * m) * LANES:(2 * m + 1) * LANES],
                    aug[:, (2 * m + 1) * LANES:(2 * m + 2) * LANES])
    pfk = jnp.dot(hb, w_ref[:, o + FOX_WIDTH:o + 2 * FOX_WIDTH], preferred_element_type=F32)
    for m in range(N_FOX_HEADS // 2):
        split_store(pfk[:, m * LANES:(m + 1) * LANES], fk_ref, m,
                    aug[:, QK_WIDTH + (2 * m) * LANES:QK_WIDTH + (2 * m + 1) * LANES],
                    aug[:, QK_WIDTH + (2 * m + 1) * LANES:QK_WIDTH + (2 * m + 2) * LANES])


def _forget_placement():
    nh = N_FOX_HEADS
    p = np.zeros((LANES, 2 * QK_WIDTH), np.float32)
    for h in range(nh):
        base_q = h * LANES + HEAD_DIM
        base_k = QK_WIDTH + h * LANES + HEAD_DIM
        for part in range(3):
            p[part * nh + h, base_q + part] = 1.0
            p[3 * nh, base_q + 3 + part] = 1.0
            p[3 * nh, base_k + part] = 1.0
            p[part * nh + h, base_k + 3 + part] = -1.0
    return jnp.asarray(p, BF16)


def _inproj(x, moe, gate, sc, sh, g, w_bf, b_forget, tables, pq):
    fuse = moe is not None
    tpb = SEQ // TM
    row = pl.BlockSpec((TM, D_MODEL), lambda i: (i, 0))
    per_batch = pl.BlockSpec((None, 1, D_MODEL), lambda i: (i // tpb, 0, 0))
    const = lambda shape: pl.BlockSpec(shape, lambda i: (0,) * len(shape))
    tab = pl.BlockSpec((TM, LANES), lambda i: (i, 0))
    in_specs = [row]
    args = [x]
    if fuse:
        in_specs += [tab, pl.BlockSpec((TOP_K, N_PLANES, TM, LANES), lambda i: (0, 0, i, 0)), per_batch]
        args += [moe[0], moe[1], gate]
    in_specs += [per_batch, per_batch, const((1, D_MODEL)), const((D_MODEL, IN_COLS_PAD)),
                 const((1, LANES)), tab, tab, tab, const((LANES, 2 * QK_WIDTH))]
    args += [sc, sh, g.reshape(1, D_MODEL), w_bf, b_forget, *tables, pq]
    wide = pl.BlockSpec((TM, QK_WIDTH), lambda i: (i, 0))
    def vspec(heads, width):
        rows = width + ONES_ROWS
        return (pl.BlockSpec((None, heads, None, rows, TM), lambda i: (i // tpb, 0, i % tpb, 0, 0)),
                jax.ShapeDtypeStruct((BATCH, heads, tpb, rows, TM), BF16))

    wide_s = jax.ShapeDtypeStruct((N_TOK, QK_WIDTH), BF16)
    dv_spec, dv_s = vspec(N_DIFF_HEADS, 2 * HEAD_DIM)
    fv_spec, fv_s = vspec(N_FOX_HEADS, HEAD_DIM)
    out_specs = [wide, wide, dv_spec, wide, wide, fv_spec]
    out_shape = [wide_s, wide_s, dv_s, wide_s, wide_s, fv_s]
    if fuse:
        out_specs = [row] + out_specs
        out_shape = [jax.ShapeDtypeStruct((N_TOK, D_MODEL), F32)] + out_shape
    outs = pl.pallas_call(
        functools.partial(_inproj_kernel, fuse),
        grid=(N_TOK // TM,),
        in_specs=in_specs,
        out_specs=out_specs,
        out_shape=out_shape,
        scratch_shapes=[pltpu.VMEM((8, LANES), F32)],
        compiler_params=_params("arbitrary"),
        name="norm_inproj",
    )(*args)
    if fuse:
        return outs[0], outs[1:]
    return x, outs


def _attn_kernel(diff, lambda_init, qa_ref, qb_ref, ka_ref, kb_ref, v_ref, g_ref, lam_ref, o_ref,
                 *scratch):
    nq = SEQ // TQ
    n_half = TQ // ATTN_TQ
    feat = 2 * HEAD_DIM if diff else HEAD_DIM
    chains = []
    for mi, (q_ref, k_ref) in enumerate(((qa_ref, ka_ref), (qb_ref, kb_ref))):
        for h in range(n_half):
            c = mi * n_half + h
            qt_sc, s_sc, p_sc, m_sc, a_sc, acc_sc = scratch[c::2 * n_half]
            vh = 0 if diff else mi
            chains.append((h, k_ref, qt_sc, s_sc, p_sc, m_sc, a_sc, acc_sc, q_ref, vh))
    order = [chains[mi * n_half + h] for h in range(n_half) for mi in range(2)]
    early, late = order[:-N_LATE_CHAINS], order[-N_LATE_CHAINS:]

    def load_queries(qi):
        for mi in range(2):
            q_ref = chains[mi * n_half][8]
            qt = q_ref[pl.ds(pl.multiple_of(qi * TQ, TQ), TQ), :].astype(F32).T.astype(BF16)
            for h in range(n_half):
                chains[mi * n_half + h][2][...] = qt[:, h * ATTN_TQ:(h + 1) * ATTN_TQ]

    def reset_state():
        for chain in chains:
            m_sc, _, acc_sc = chain[5:8]
            m_sc[...] = jnp.full(m_sc.shape, NEG, F32)
            acc_sc[...] = jnp.zeros(acc_sc.shape, F32)

    def n_keys(chain, masked):
        return (chain[0] + 1) * ATTN_TQ if masked else TQ

    def scores(chain, j, masked):
        h, k_ref, qt_sc, s_sc = chain[:4]
        nk = n_keys(chain, masked)
        off = pl.multiple_of(j * TQ, TQ)
        s = jnp.dot(k_ref[pl.ds(off, nk), :], qt_sc[...], preferred_element_type=F32)
        if masked:
            kk = lax.broadcasted_iota(jnp.int32, (nk, ATTN_TQ), 0)
            qq = h * ATTN_TQ + lax.broadcasted_iota(jnp.int32, (nk, ATTN_TQ), 1)
            s = jnp.where((kk // CHUNK <= qq // CHUNK) if diff else (kk <= qq), s, NEG)
        s_sc[0:nk, :] = s

    def softmax(chain, masked):
        s_sc, p_sc, m_sc, a_sc = chain[3:7]
        nk = n_keys(chain, masked)
        m_all = m_sc[...]
        m_parts = []
        for c0 in range(0, ATTN_TQ, LANES):
            cols = slice(c0, c0 + LANES)
            pm = s_sc[0:ATTN_ROWS, cols]
            for r0 in range(ATTN_ROWS, nk, ATTN_ROWS):
                pm = jnp.maximum(pm, s_sc[r0:r0 + ATTN_ROWS, cols])
            m_new = jnp.maximum(m_all[:, cols], jnp.max(pm, axis=0, keepdims=True))
            for r0 in range(0, nk, ATTN_ROWS):
                p = jnp.exp2(s_sc[r0:r0 + ATTN_ROWS, cols] - m_new)
                p_sc[r0:r0 + ATTN_ROWS, cols] = p.astype(BF16)
            m_parts.append(m_new)
        m_new = jnp.concatenate(m_parts, axis=1)
        a_sc[...] = jnp.exp2(m_all - m_new)
        m_sc[...] = m_new

    def values(chain, j, masked=False):
        p_sc, a_sc, acc_sc, vh = chain[4], chain[6], chain[7], chain[9]
        nk = n_keys(chain, masked)
        pv = jnp.dot(v_ref[vh, j, :, 0:nk], p_sc[0:nk, :], preferred_element_type=F32)
        acc_sc[...] = a_sc[...] * acc_sc[...] + pv

    def idle_late():
        for chain in late:
            chain[4][...] = jnp.zeros(chain[4].shape, BF16)
            chain[6][...] = jnp.ones(chain[6].shape, F32)

    def consume(j, cur_masked=False, nxt=None, nxt_masked=False, final=False, before_next=None):
        def open_late(chain):
            scores(chain, j, cur_masked)
            values(chain, jnp.maximum(j - 1, 0))

        open_late(late[0])
        for i, chain in enumerate(early):
            softmax(chain, cur_masked)
            if i == 0:
                for other in late[1:]:
                    open_late(other)
                if before_next is not None:
                    before_next()
            if nxt is not None:
                scores(chain, nxt, nxt_masked)
            values(chain, j, cur_masked)
        for chain in late:
            softmax(chain, cur_masked)
        if final:
            for chain in late:
                values(chain, j, cur_masked)

    def finalize(qi):
        ot = [jnp.concatenate([chains[mi * n_half + h][7][0:feat] / chains[mi * n_half + h][7][feat:feat + 1]
                               for h in range(n_half)], axis=1) for mi in range(2)]
        g = g_ref[...]
        rows = pl.ds(pl.multiple_of(qi * TQ, TQ), TQ)
        if diff:
            lv = lam_ref[...]
            lam = (jnp.exp(jnp.sum(lv[0:1] * lv[1:2], axis=1, keepdims=True))
                   - jnp.exp(jnp.sum(lv[2:3] * lv[3:4], axis=1, keepdims=True)) + lambda_init)
            o = (ot[0] - lam * ot[1]).T
            y = o * lax.rsqrt(jnp.mean(o * o, axis=1, keepdims=True) + EPS) * g
            o_ref[rows, :] = (y * (1.0 - lambda_init)).astype(o_ref.dtype)
        else:
            o = jnp.concatenate(ot, axis=0).T
            low = _lane_iota((TQ, LANES)) < HEAD_DIM
            sq = o * o
            msa = jnp.sum(jnp.where(low, sq, 0.0), axis=1, keepdims=True) / HEAD_DIM
            msb = jnp.sum(jnp.where(low, 0.0, sq), axis=1, keepdims=True) / HEAD_DIM
            inv = jnp.where(low, lax.rsqrt(msa + EPS), lax.rsqrt(msb + EPS))
            o_ref[rows, :] = (o * inv * g).astype(o_ref.dtype)

    load_queries(0)
    reset_state()
    idle_late()
    for chain in early:
        scores(chain, 0, True)

    @pl.loop(0, nq)
    def _(qi):
        n_plain = jnp.maximum(qi - 1, 0)

        @pl.loop(0, n_plain // 2)
        def _(t):
            consume(2 * t, nxt=2 * t + 1)
            consume(2 * t + 1, nxt=2 * t + 2)

        @pl.when(n_plain % 2 == 1)
        def _():
            consume(qi - 2, nxt=qi - 1)

        def last_blocks(to_next_tile):
            if to_next_tile:
                consume(qi, cur_masked=True, nxt=0, final=True, before_next=lambda: load_queries(qi + 1))
            else:
                consume(qi, cur_masked=True, final=True)
            finalize(qi)

        for to_next_tile in (True, False):
            more = (qi < nq - 1) if to_next_tile else (qi == nq - 1)

            @pl.when(more & (qi > 0))
            def _():
                consume(qi - 1, nxt=qi, nxt_masked=True)
                last_blocks(to_next_tile)

            if to_next_tile:
                @pl.when(qi == 0)
                def _():
                    last_blocks(to_next_tile)

        reset_state()
        idle_late()


def _attention(diff, lambda_init, q, k, v, g, lamv):
    nq = SEQ // TQ
    kspec = lambda par: pl.BlockSpec((SEQ, LANES), lambda b, p: (b, 2 * p + par))
    return pl.pallas_call(
        functools.partial(_attn_kernel, diff, lambda_init),
        grid=(BATCH, 4),
        in_specs=[kspec(0), kspec(1), kspec(0), kspec(1),
                  pl.BlockSpec((None, v.shape[1] // 4, nq, v.shape[3], TQ), lambda b, p: (b, p, 0, 0, 0)),
                  pl.BlockSpec((1, LANES), lambda b, p: (0, 0)),
                  pl.BlockSpec((8, LANES), lambda b, p: (0, 0))],
        out_specs=pl.BlockSpec((SEQ, LANES), lambda b, p: (b, p)),
        out_shape=jax.ShapeDtypeStruct((N_TOK, DIFF_WIDTH), BF16),
        scratch_shapes=[pltpu.VMEM(shape, dt)
                        for shape, dt in (((LANES, ATTN_TQ), BF16), ((TQ, ATTN_TQ), F32),
                                          ((TQ, ATTN_TQ), BF16), ((1, ATTN_TQ), F32),
                                          ((1, ATTN_TQ), F32), ((v.shape[3], ATTN_TQ), F32))
                        for _ in range(2 * TQ // ATTN_TQ)],
        compiler_params=_params("arbitrary", "arbitrary"),
        name="diff_attention" if diff else "fox_attention",
    )(q, q, k, k, v, g, lamv)


def _outproj_kernel(x_ref, od_ref, of_ref, gt_ref, sc_ref, sh_ref, g_ref, wo_ref, wr_ref, br_ref,
                    x1_ref, h2_ref, rt_ref, cnt_ref, carry_ref):
    @pl.when(pl.program_id(0) == 0)
    def _():
        carry_ref[...] = jnp.zeros_like(carry_ref)

    mix = jnp.dot(od_ref[...], wo_ref[0:DIFF_WIDTH, :], preferred_element_type=F32)
    mix += jnp.dot(of_ref[...], wo_ref[DIFF_WIDTH:, :], preferred_element_type=F32)
    x1 = x_ref[...] + gt_ref[...] * mix
    x1_ref[...] = x1
    h = _rms_mod(x1, g_ref[...], sc_ref[...], sh_ref[...])
    hh = h.astype(BF16)
    _pack_planes(h, h2_ref)
    hl = (h - hh.astype(F32)).astype(BF16)
    r1 = jnp.dot(hh, wr_ref[...], preferred_element_type=F32)
    r2 = jnp.dot(hl, wr_ref[:, 0:LANES], preferred_element_type=F32)
    logits = r1[:, 0:LANES] + r1[:, LANES:] + r2 + br_ref[...]

    lane = _lane_iota((TM, LANES))
    lanef = lane.astype(F32)
    big = float(LANES)
    isg = lane < N_GROUPS
    lg = jnp.where(isg, logits, NEG)
    mg = jnp.max(lg, axis=1, keepdims=True)
    sg = jnp.sum(jnp.where(isg, jnp.exp(lg - mg), 0.0), axis=1, keepdims=True)
    p_g = 1.0 / sg
    gsel = jnp.min(jnp.where(isg & (lg == mg), lanef, big), axis=1, keepdims=True)
    lo = N_GROUPS + gsel * EXPERTS_PER_GROUP
    ise = (lanef >= lo) & (lanef < lo + EXPERTS_PER_GROUP)
    le = jnp.where(ise, logits, NEG)
    t1 = jnp.max(le, axis=1, keepdims=True)
    i1 = jnp.min(jnp.where(ise & (le == t1), lanef, big), axis=1, keepdims=True)
    ise2 = ise & (lanef != i1)
    le2 = jnp.where(ise2, logits, NEG)
    t2 = jnp.max(le2, axis=1, keepdims=True)
    i2 = jnp.min(jnp.where(ise2 & (le2 == t2), lanef, big), axis=1, keepdims=True)
    d = jnp.exp(t2 - t1)
    w1 = p_g / (1.0 + d)
    w2 = p_g * d / (1.0 + d)
    e1 = i1 - N_GROUPS
    e2 = i2 - N_GROUPS
    oh1 = lanef == e1
    oh2 = lanef == e2
    both = jnp.where(oh1 | oh2, 1.0, 0.0)
    row = lax.broadcasted_iota(jnp.int32, (TM, TM), 0)
    col = lax.broadcasted_iota(jnp.int32, (TM, TM), 1)
    before = jnp.dot((row > col).astype(BF16), both.astype(BF16), preferred_element_type=F32)
    before = before + carry_ref[0:1, :]
    rank1 = jnp.sum(jnp.where(oh1, before, 0.0), axis=1, keepdims=True)
    rank2 = jnp.sum(jnp.where(oh2, before, 0.0), axis=1, keepdims=True)
    total = carry_ref[0:1, :] + jnp.sum(both, axis=0, keepdims=True)
    carry_ref[...] = jnp.broadcast_to(total, carry_ref.shape)
    cnt_ref[...] = jnp.broadcast_to(total, cnt_ref.shape)
    vals = (e1, e2, w1, w2, rank1, rank2)
    out = jnp.zeros((TM, LANES), F32)
    for j, v in enumerate(vals):
        out = jnp.where(lane == j, v, out)
    rt_ref[...] = out


def _outproj(x, od, of, gt, sc, sh, g, wo_bf, wr, br):
    tpb = SEQ // TM
    row = pl.BlockSpec((TM, D_MODEL), lambda i: (i, 0))
    half = pl.BlockSpec((TM, DIFF_WIDTH), lambda i: (i, 0))
    per_batch = pl.BlockSpec((None, 1, D_MODEL), lambda i: (i // tpb, 0, 0))
    const = lambda shape: pl.BlockSpec(shape, lambda i: (0,) * len(shape))
    return pl.pallas_call(
        _outproj_kernel,
        grid=(N_TOK // TM,),
        in_specs=[row, half, half, per_batch, per_batch, per_batch, const((1, D_MODEL)),
                  const((D_MODEL, D_MODEL)), const((D_MODEL, 2 * LANES)), const((1, LANES))],
        out_specs=[row, pl.BlockSpec((N_PLANES, TM, LANES), lambda i: (0, i, 0)),
                   pl.BlockSpec((TM, LANES), lambda i: (i, 0)), const((8, LANES))],
        out_shape=[jax.ShapeDtypeStruct((N_TOK, D_MODEL), F32),
                   jax.ShapeDtypeStruct((N_PLANES, N_TOK, LANES), jnp.int32),
                   jax.ShapeDtypeStruct((N_TOK, LANES), F32),
                   jax.ShapeDtypeStruct((8, LANES), F32)],
        scratch_shapes=[pltpu.VMEM((8, LANES), F32)],
        compiler_params=_params("arbitrary"),
        name="outproj_router",
    )(x, od, of, gt, sc, sh, g.reshape(1, D_MODEL), wo_bf, wr, br)


def _expert_kernel(be_ref, cnt_ref, xs_ref, wg_ref, wu_ref, wd_ref, ys_ref, wg_sc, wu_sc, wd_sc):
    i = pl.program_id(0)
    cnt = cnt_ref[i]

    @pl.when((i == 0) | (be_ref[i] != be_ref[jnp.maximum(i - 1, 0)]))
    def _():
        wg_sc[...] = wg_ref[...].astype(BF16)
        wu_sc[...] = wu_ref[...].astype(BF16)
        wd_sc[...] = wd_ref[...].astype(BF16)

    @pl.when(cnt > 0)
    def _():
        live = lax.broadcasted_iota(jnp.int32, (MOE_BLOCK, LANES), 0) < cnt
        xb = _unpack_planes([jnp.where(live, xs_ref[p], 0) for p in range(N_PLANES)]).astype(BF16)
        a = jnp.dot(xb, wg_sc[...], preferred_element_type=F32)
        u = jnp.dot(xb, wu_sc[...], preferred_element_type=F32)
        hid = (a / (1.0 + jnp.exp(-a)) * u).astype(BF16)
        _pack_planes(jnp.dot(hid, wd_sc[...], preferred_element_type=F32), ys_ref)

    @pl.when(cnt == 0)
    def _():
        ys_ref[...] = jnp.zeros_like(ys_ref)


def _experts(layer, block_expert, block_count, xs, wg, wu, wd):
    planes = pl.BlockSpec((N_PLANES, MOE_BLOCK, LANES), lambda i, be, bc: (0, i, 0))
    w_in = pl.BlockSpec((None, None, D_MODEL, D_EXPERT), lambda i, be, bc: (layer, be[i], 0, 0))
    w_out = pl.BlockSpec((None, None, D_EXPERT, D_MODEL), lambda i, be, bc: (layer, be[i], 0, 0))
    grid_spec = pltpu.PrefetchScalarGridSpec(
        num_scalar_prefetch=2,
        grid=(MOE_NBLOCKS,),
        in_specs=[planes, w_in, w_in, w_out],
        out_specs=planes,
        scratch_shapes=[pltpu.VMEM((D_MODEL, D_EXPERT), BF16), pltpu.VMEM((D_MODEL, D_EXPERT), BF16),
                        pltpu.VMEM((D_EXPERT, D_MODEL), BF16)],
    )
    return pl.pallas_call(
        _expert_kernel,
        grid_spec=grid_spec,
        out_shape=jax.ShapeDtypeStruct((N_PLANES, MOE_ROWS, LANES), jnp.int32),
        compiler_params=_params("arbitrary"),
        name="expert_mlp",
    )(block_expert, block_count, xs, wg, wu, wd)


def _slots(route, counts):
    counts = counts[0, :N_EXPERTS].astype(jnp.int32)
    padded = ((counts + MOE_BLOCK - 1) // MOE_BLOCK) * MOE_BLOCK
    pend = jnp.cumsum(padded)
    pstart = pend - padded
    bstart = jnp.arange(MOE_NBLOCKS, dtype=jnp.int32) * MOE_BLOCK
    block_expert = jnp.minimum(jnp.sum(bstart[:, None] >= pend[None, :], axis=1), N_EXPERTS - 1)
    block_expert = block_expert.astype(jnp.int32)
    mine = block_expert[:, None] == jnp.arange(N_EXPERTS, dtype=jnp.int32)[None, :]
    left = jnp.sum(jnp.where(mine, counts + pstart, 0), axis=1) - bstart
    block_count = jnp.clip(left, 0, MOE_BLOCK).astype(jnp.int32)
    base = jnp.pad(pstart.astype(F32), (0, LANES - N_EXPERTS)).reshape(1, LANES)
    return _slot_rows(route, base), block_expert, block_count


def _slot_rows_kernel(rt_ref, base_ref, o_ref):
    rt = rt_ref[...]
    lanef = _lane_iota(rt.shape).astype(F32)
    base = base_ref[...]
    dest = jnp.zeros(rt.shape, F32)
    for k in range(TOP_K):
        b = jnp.sum(jnp.where(lanef == rt[:, k:k + 1], base, 0.0), axis=1, keepdims=True)
        dest = jnp.where(lanef == k, b + rt[:, 2 * TOP_K + k:2 * TOP_K + k + 1], dest)
    dest_t = dest.T.astype(jnp.int32)
    for k in range(TOP_K):
        for p in range(N_PLANES):
            for c in range(rt.shape[0] // LANES):
                o_ref[k * N_PLANES + p, c:c + 1, :] = dest_t[k:k + 1, c * LANES:(c + 1) * LANES] + p * MOE_ROWS


def _slot_rows(route, base):
    tm = 8 * LANES
    return pl.pallas_call(
        _slot_rows_kernel,
        grid=(N_TOK // tm,),
        in_specs=[pl.BlockSpec((tm, LANES), lambda i: (i, 0)), pl.BlockSpec((1, LANES), lambda i: (0, 0))],
        out_specs=pl.BlockSpec((TOP_K * N_PLANES, tm // LANES, LANES), lambda i: (0, i, 0)),
        out_shape=jax.ShapeDtypeStruct((TOP_K * N_PLANES, N_TOK // LANES, LANES), jnp.int32),
        compiler_params=_params("arbitrary"),
        name="slot_rows",
    )(route, base)


def _sc_workers():
    info = plsc.get_sparse_core_info()
    return info.num_cores, info.num_cores * info.num_subcores


def _sc_scatter2(src, idx, out_rows):
    n_win = src.shape[0] // SC_WINDOW
    nc, nw = _sc_workers()
    steps = n_win // nw
    mesh = plsc.VectorSubcoreMesh(core_axis_name="c", subcore_axis_name="s")

    @functools.partial(
        pl.kernel, mesh=mesh,
        out_type=jax.ShapeDtypeStruct((out_rows, LANES), src.dtype),
        scratch_types=[pltpu.VMEM((SC_WINDOW,), jnp.int32), pltpu.VMEM((SC_WINDOW,), jnp.int32),
                       pltpu.VMEM((SC_WINDOW, LANES), src.dtype)],
        name="sc_dispatch_scatter",
    )
    def k(src_hbm, idx_hbm, out_hbm, ia_v, ib_v, rows_v):
        first = (lax.axis_index("s") * nc + lax.axis_index("c")) * steps

        @pl.loop(0, steps)
        def _(j):
            w = first + j
            pltpu.sync_copy(idx_hbm.at[w], ia_v)
            pltpu.sync_copy(idx_hbm.at[n_win + w], ib_v)
            pltpu.sync_copy(src_hbm.at[pl.ds(w * SC_WINDOW, SC_WINDOW)], rows_v)
            pltpu.sync_copy(rows_v, out_hbm.at[ia_v])
            pltpu.sync_copy(rows_v, out_hbm.at[ib_v])

    return k(src, idx)


def _sc_gather(table, idx):
    n_out = idx.shape[0] * SC_WINDOW
    nc, nw = _sc_workers()
    steps = n_out // nw // SC_WINDOW
    mesh = plsc.VectorSubcoreMesh(core_axis_name="c", subcore_axis_name="s")

    @functools.partial(
        pl.kernel, mesh=mesh,
        out_type=jax.ShapeDtypeStruct((n_out, LANES), table.dtype),
        scratch_types=[pltpu.VMEM((steps, SC_WINDOW), jnp.int32),
                       pltpu.VMEM((SC_INFLIGHT, SC_WINDOW, LANES), table.dtype),
                       pltpu.SemaphoreType.DMA((SC_INFLIGHT,)), pltpu.SemaphoreType.DMA((SC_INFLIGHT,))],
        name="sc_combine_gather",
    )
    def k(table_hbm, idx_hbm, out_hbm, idx_v, rows_v, gsem, wsem):
        first = (lax.axis_index("s") * nc + lax.axis_index("c")) * steps
        pltpu.sync_copy(idx_hbm.at[pl.ds(first, steps)], idx_v)

        @pl.loop(0, steps, step=SC_INFLIGHT)
        def _(j):
            gathers = [pltpu.async_copy(table_hbm.at[idx_v.at[j + b]], rows_v.at[b], gsem.at[b])
                       for b in range(SC_INFLIGHT)]
            writes = []
            for b in range(SC_INFLIGHT):
                gathers[b].wait()
                dst = out_hbm.at[pl.ds((first + j + b) * SC_WINDOW, SC_WINDOW)]
                writes.append(pltpu.async_copy(rows_v.at[b], dst, wsem.at[b]))
            for w in writes:
                w.wait()

    return k(table, idx)


def _final_kernel(x_ref, rt_ref, y_ref, gt_ref, g_ref, o_ref):
    x = x_ref[...] + gt_ref[...] * _combine(rt_ref, y_ref)
    ms = jnp.mean(x * x, axis=-1, keepdims=True)
    o_ref[...] = x * lax.rsqrt(ms + EPS) * g_ref[...]


def _final(x, moe, gate, g):
    tpb = SEQ // TM
    row = pl.BlockSpec((TM, D_MODEL), lambda i: (i, 0))
    return pl.pallas_call(
        _final_kernel,
        grid=(N_TOK // TM,),
        in_specs=[row, pl.BlockSpec((TM, LANES), lambda i: (i, 0)),
                  pl.BlockSpec((TOP_K, N_PLANES, TM, LANES), lambda i: (0, 0, i, 0)),
                  pl.BlockSpec((None, 1, D_MODEL), lambda i: (i // tpb, 0, 0)),
                  pl.BlockSpec((1, D_MODEL), lambda i: (0, 0))],
        out_specs=row,
        out_shape=jax.ShapeDtypeStruct((N_TOK, D_MODEL), F32),
        compiler_params=_params("arbitrary"),
        name="final_norm",
    )(x, moe[0], moe[1], gate, g.reshape(1, D_MODEL))


def kernel(x, c, positions, w_ada, b_ada, g_mix, w_in, b_forget, lambda_q1, lambda_k1, lambda_q2,
           lambda_k2, g_subln, g_fox_out, w_out, g_ffn, w_router_group, b_router_group,
           w_router_expert, b_router_expert, w_expert_gate, w_expert_up, w_expert_down, g_final):
    mod = _modulation(c, w_ada, b_ada)
    mod = mod.reshape(DEPTH, BATCH, 6, 1, D_MODEL)
    tables = _rope_tables(positions)
    pq = _forget_placement()
    xf = x.reshape(N_TOK, D_MODEL)
    moe = None
    gate = None
    for l in range(DEPTH):
        sh1, sc1, gt1, sh2, sc2, gt2 = (mod[l, :, j] for j in range(6))
        w_bf = jnp.pad(w_in[l], ((0, 0), (0, IN_COLS_PAD - IN_COLS))).astype(BF16)
        bfp = jnp.pad(b_forget[l], (0, LANES - N_FOX_HEADS)).reshape(1, LANES)
        xf, (dq, dk, dv, fq, fk, fv) = _inproj(xf, moe, gate, sc1, sh1, g_mix[l], w_bf, bfp, tables, pq)

        lambda_init = 0.8 - 0.6 * float(np.exp(-0.3 * l))
        lamv = jnp.zeros((8, LANES), F32).at[0:4, 0:HEAD_DIM].set(
            jnp.stack([lambda_q1[l], lambda_k1[l], lambda_q2[l], lambda_k2[l]]))
        g_d = g_subln[l].reshape(1, LANES)
        g_f = jnp.concatenate([g_fox_out[l], g_fox_out[l]]).reshape(1, LANES)
        od = _attention(True, lambda_init, dq, dk, dv, g_d, lamv)
        of = _attention(False, lambda_init, fq, fk, fv, g_f, lamv)

        wr32 = jnp.pad(jnp.concatenate([w_router_group[l], w_router_expert[l]], axis=1),
                       ((0, 0), (0, LANES - N_GROUPS - N_EXPERTS)))
        wr_hi = wr32.astype(BF16)
        wr_lo = (wr32 - wr_hi.astype(F32)).astype(BF16)
        wr = jnp.concatenate([wr_hi, wr_lo], axis=1)
        br = jnp.pad(jnp.concatenate([b_router_group[l], b_router_expert[l]]),
                     (0, LANES - N_GROUPS - N_EXPERTS)).reshape(1, LANES)
        xf, h2, route, counts = _outproj(xf, od, of, gt1, sc2, sh2, g_ffn[l], w_out[l].astype(BF16),
                                         wr, br)

        rows, block_expert, block_count = _slots(route, counts)
        rows = rows.reshape(TOP_K * N_PLANES * N_TOK // SC_WINDOW, SC_WINDOW)
        xs = _sc_scatter2(h2.reshape(N_PLANES * N_TOK, LANES), rows, N_PLANES * MOE_ROWS)
        ys = _experts(l, block_expert, block_count, xs.reshape(N_PLANES, MOE_ROWS, LANES),
                      w_expert_gate, w_expert_up, w_expert_down)
        y2 = _sc_gather(ys.reshape(N_PLANES * MOE_ROWS, LANES), rows)
        moe = (route, y2.reshape(TOP_K, N_PLANES, N_TOK, LANES))
        gate = gt2
    out = _final(xf, moe, gate, g_final)
    return out.reshape(BATCH, SEQ, D_MODEL)
```

```python
import functools

import numpy as np
import jax
import jax.numpy as jnp
from jax import lax
from jax.experimental import pallas as pl
from jax.experimental.pallas import tpu as pltpu
from jax.experimental.pallas import tpu_sc as plsc

D_MODEL = 1024
BATCH = 4
SEQ = 4096
DEPTH = 4
N_TOK = BATCH * SEQ

CHUNK = 64
HEAD_DIM = 64
N_DIFF_HEADS = 4
N_FOX_HEADS = 8
DIFF_WIDTH = 512
FOX_WIDTH = 512
IN_COLS = 3 * DIFF_WIDTH + 3 * FOX_WIDTH + N_FOX_HEADS
ROT_DIM = 16
ROPE_THETA = 500000.0
N_GROUPS = 4
EXPERTS_PER_GROUP = 8
N_EXPERTS = 32
TOP_K = 2
D_EXPERT = 512
EPS = 1e-6

LANES = 128
IN_COLS_PAD = 3200
FF_COL = 3 * DIFF_WIDTH + 3 * FOX_WIDTH
QK_WIDTH = 8 * LANES
TM = 512
TQ = 512
ATTN_TQ = 256
N_LATE_CHAINS = 1
ONES_ROWS = 16
ATTN_ROWS = 64
LOG2E = 1.4426950408889634
MOE_BLOCK = 512
MOE_ROWS = N_TOK * TOP_K + N_EXPERTS * MOE_BLOCK
MOE_NBLOCKS = MOE_ROWS // MOE_BLOCK
N_PLANES = D_MODEL // 2 // LANES
SC_WINDOW = 128
SC_INFLIGHT = 4
NEG = -1e30
VMEM_LIMIT = 56 * 1024 * 1024

F32 = jnp.float32
BF16 = jnp.bfloat16


def _bf16_round(x):
    return x.astype(BF16).astype(F32)


def _lane_iota(shape):
    return lax.broadcasted_iota(jnp.int32, shape, 1)


def _params(*sem):
    return pltpu.CompilerParams(dimension_semantics=sem, vmem_limit_bytes=VMEM_LIMIT)


def _pack_planes(y, o_ref):
    bits = lax.bitcast_convert_type(_bf16_round(y), jnp.uint32)
    half = D_MODEL // 2
    word = bits[:, half:] | lax.shift_right_logical(bits[:, :half], jnp.uint32(16))
    word = lax.bitcast_convert_type(word, jnp.int32)
    for p in range(N_PLANES):
        o_ref[p] = word[:, p * LANES:(p + 1) * LANES]


def _unpack_planes(planes):
    lo, hi = [], []
    for w in planes:
        u = lax.bitcast_convert_type(w, jnp.uint32)
        lo.append(lax.bitcast_convert_type(lax.shift_left(u, jnp.uint32(16)), F32))
        hi.append(lax.bitcast_convert_type(u & jnp.uint32(0xFFFF0000), F32))
    return jnp.concatenate(lo + hi, axis=1)


def _combine(route_ref, y_ref):
    rt = route_ref[...]
    y0 = _unpack_planes([y_ref[0, p] for p in range(N_PLANES)])
    y1 = _unpack_planes([y_ref[1, p] for p in range(N_PLANES)])
    return rt[:, 2:3] * y0 + rt[:, 3:4] * y1


def _mod_kernel(c_ref, w_ref, b_ref, o_ref):
    c = c_ref[...]
    cond = c / (1.0 + jnp.exp(-c))
    ch = cond.astype(BF16)
    cl = (cond - ch.astype(F32)).astype(BF16)
    w = w_ref[...]
    wh = w.astype(BF16)
    wl = (w - wh.astype(F32)).astype(BF16)
    acc = jnp.dot(ch, wh, preferred_element_type=F32)
    acc += jnp.dot(cl, wh, preferred_element_type=F32)
    acc += jnp.dot(ch, wl, preferred_element_type=F32)
    o_ref[...] = acc + b_ref[...]


def _modulation(c, w_ada, b_ada):
    rows = 16
    tn = 1536
    c_pad = jnp.zeros((rows, D_MODEL), F32).at[:BATCH].set(c)
    out = pl.pallas_call(
        _mod_kernel,
        grid=(DEPTH, 6 * D_MODEL // tn),
        in_specs=[
            pl.BlockSpec((rows, D_MODEL), lambda l, n: (0, 0)),
            pl.BlockSpec((None, D_MODEL, tn), lambda l, n: (l, 0, n)),
            pl.BlockSpec((None, 1, tn), lambda l, n: (l, 0, n)),
        ],
        out_specs=pl.BlockSpec((None, rows, tn), lambda l, n: (l, 0, n)),
        out_shape=jax.ShapeDtypeStruct((DEPTH, rows, 6 * D_MODEL), F32),
        compiler_params=_params("arbitrary", "arbitrary"),
        name="adaln_mod",
    )(c_pad, w_ada, b_ada.reshape(DEPTH, 1, 6 * D_MODEL))
    return out[:, :BATCH]


def _rope_kernel(pos_ref, inv_ref, c_ref, sa_ref, sb_ref):
    ang = pos_ref[...].astype(F32) * inv_ref[...]
    j = _lane_iota(ang.shape) % HEAD_DIM
    cosv = jnp.cos(ang)
    sinv = jnp.sin(ang)
    half = ROT_DIM // 2
    c_ref[...] = jnp.where(j < ROT_DIM, cosv, 1.0)
    sa_ref[...] = jnp.where(j < half, -sinv, 0.0)
    sb_ref[...] = jnp.where((j >= half) & (j < ROT_DIM), sinv, 0.0)


def _rope_tables(positions):
    half = ROT_DIM // 2
    inv = ROPE_THETA ** (-jnp.arange(0, ROT_DIM, 2, dtype=F32) / ROT_DIM)
    lane = np.arange(LANES)
    inv_lane = inv[(lane % HEAD_DIM) % half].reshape(1, LANES)
    spec = pl.BlockSpec((TM, LANES), lambda i: (i, 0))
    shape = jax.ShapeDtypeStruct((N_TOK, LANES), F32)
    return pl.pallas_call(
        _rope_kernel,
        grid=(N_TOK // TM,),
        in_specs=[pl.BlockSpec((TM, 1), lambda i: (i, 0)),
                  pl.BlockSpec((1, LANES), lambda i: (0, 0))],
        out_specs=[spec, spec, spec],
        out_shape=[shape, shape, shape],
        compiler_params=_params("arbitrary"),
        name="rope_tables",
    )(positions.reshape(N_TOK, 1), inv_lane)


def _rms_mod(x, g, sc, sh):
    ms = jnp.mean(x * x, axis=-1, keepdims=True)
    return (x * lax.rsqrt(ms + EPS) * g) * (1.0 + sc) + sh


def _inproj_kernel(fuse, *refs):
    if fuse:
        (x_ref, rt_ref, y_ref, gt_ref, sc_ref, sh_ref, g_ref, w_ref, bf_ref, c_ref, sa_ref, sb_ref,
         pq_ref, xo_ref, dq_ref, dk_ref, dv_ref, fq_ref, fk_ref, fv_ref, carry_ref) = refs
        x = x_ref[...] + gt_ref[...] * _combine(rt_ref, y_ref)
        xo_ref[...] = x
    else:
        (x_ref, sc_ref, sh_ref, g_ref, w_ref, bf_ref, c_ref, sa_ref, sb_ref,
         pq_ref, dq_ref, dk_ref, dv_ref, fq_ref, fk_ref, fv_ref, carry_ref) = refs
        x = x_ref[...]
    hb = _rms_mod(x, g_ref[...], sc_ref[...], sh_ref[...]).astype(BF16)

    @pl.when(pl.program_id(0) % (SEQ // TM) == 0)
    def _():
        carry_ref[...] = jnp.zeros_like(carry_ref)

    lane = _lane_iota((TM, LANES))
    nh = N_FOX_HEADS

    def pack3(a):
        hi = _bf16_round(a)
        r1 = a - hi
        mid = _bf16_round(r1)
        lo = _bf16_round(r1 - mid)
        return jnp.where(lane < nh, hi,
                         jnp.where(lane < 2 * nh, pltpu.roll(mid, nh, 1),
                                   jnp.where(lane < 3 * nh, pltpu.roll(lo, 2 * nh, 1), 0.0)))

    z = jnp.dot(hb, w_ref[:, FF_COL:FF_COL + LANES], preferred_element_type=F32) + bf_ref[...]

    low = lane < HEAD_DIM
    rc, rsa, rsb = c_ref[...], sa_ref[...], sb_ref[...]
    scale = HEAD_DIM ** -0.5 * LOG2E

    def split_store(chunk, o_ref, m, extra_a=None, extra_b=None):
        a = jnp.where(low, chunk, 0.0)
        b = jnp.where(low, pltpu.roll(chunk, HEAD_DIM, 1), 0.0)
        if extra_a is not None:
            a = a + extra_a
            b = b + extra_b
        o_ref[:, (2 * m) * LANES:(2 * m + 1) * LANES] = a.astype(BF16)
        o_ref[:, (2 * m + 1) * LANES:(2 * m + 2) * LANES] = b.astype(BF16)

    def rope(xc):
        return xc * rc + pltpu.roll(xc, LANES - ROT_DIM // 2, 1) * rsa + pltpu.roll(xc, ROT_DIM // 2, 1) * rsb

    pdq = jnp.dot(hb, w_ref[:, 0:DIFF_WIDTH], preferred_element_type=F32)
    for m in range(N_DIFF_HEADS):
        split_store(rope(pdq[:, m * LANES:(m + 1) * LANES]) * scale, dq_ref, m)
    pdk = jnp.dot(hb, w_ref[:, DIFF_WIDTH:2 * DIFF_WIDTH], preferred_element_type=F32)
    for m in range(N_DIFF_HEADS):
        split_store(rope(pdk[:, m * LANES:(m + 1) * LANES]), dk_ref, m)
    def store_values_t(pv, o_ref, width):
        ones = jnp.ones((ONES_ROWS, TM), BF16)
        for m in range(4):
            vt = pv[:, m * LANES:(m + 1) * LANES].T.astype(BF16)
            for i in range(LANES // width):
                o_ref[m * (LANES // width) + i, 0:width, :] = vt[i * width:(i + 1) * width]
                o_ref[m * (LANES // width) + i, width:width + ONES_ROWS, :] = ones

    store_values_t(jnp.dot(hb, w_ref[:, 2 * DIFF_WIDTH:3 * DIFF_WIDTH], preferred_element_type=F32),
                   dv_ref, 2 * HEAD_DIM)
    o = 3 * DIFF_WIDTH
    store_values_t(jnp.dot(hb, w_ref[:, o + 2 * FOX_WIDTH:o + 3 * FOX_WIDTH],
                           preferred_element_type=F32), fv_ref, HEAD_DIM)

    logf =jnp.minimum(z, 0.0) - jnp.log(1.0 + jnp.exp(-jnp.abs(z)))
    logf = jnp.where(lane < nh, logf, 0.0)
    row = lax.broadcasted_iota(jnp.int32, (TM, TM), 0)
    col = lax.broadcasted_iota(jnp.int32, (TM, TM), 1)
    tri = (row >= col).astype(BF16)
    r = jnp.dot(tri, pack3(logf).astype(BF16), preferred_element_type=F32)
    cs = r + pltpu.roll(r, LANES - nh, 1) + pltpu.roll(r, LANES - 2 * nh, 1)
    cf = jnp.where(lane < nh, cs + carry_ref[0:1, :], 0.0)
    carry_ref[...] = jnp.broadcast_to(cf[TM - 1:TM, :], carry_ref.shape)

    t3 = jnp.where(lane == 3 * nh, 1.0, pack3(cf * LOG2E)).astype(BF16)
    aug = jnp.dot(t3, pq_ref[...], preferred_element_type=F32)

    pfq =jnp.dot(hb, w_ref[:, o:o + FOX_WIDTH], preferred_element_type=F32)
    for m in range(N_FOX_HEADS // 2):
        split_store(pfq[:, m * LANES:(m + 1) * LANES] * scale, fq_ref, m,
                    aug[:, (2 * m) * LANES:(2 * m + 1) * LANES],
                    aug[:, (2 * m + 1) * LANES:(2 * m + 2) * LANES])
    pfk = jnp.dot(hb, w_ref[:, o + FOX_WIDTH:o + 2 * FOX_WIDTH], preferred_element_type=F32)
    for m in range(N_FOX_HEADS // 2):
        split_store(pfk[:, m * LANES:(m + 1) * LANES], fk_ref, m,
                    aug[:, QK_WIDTH + (2 * m) * LANES:QK_WIDTH + (2 * m + 1) * LANES],
                    aug[:, QK_WIDTH + (2 * m + 1) * LANES:QK_WIDTH + (2 * m + 2) * LANES])


def _forget_placement():
    nh = N_FOX_HEADS
    p = np.zeros((LANES, 2 * QK_WIDTH), np.float32)
    for h in range(nh):
        base_q = h * LANES + HEAD_DIM
        base_k = QK_WIDTH + h * LANES + HEAD_DIM
        for part in range(3):
            p[part * nh + h, base_q + part] = 1.0
            p[3 * nh, base_q + 3 + part] = 1.0
            p[3 * nh, base_k + part] = 1.0
            p[part * nh + h, base_k + 3 + part] = -1.0
    return jnp.asarray(p, BF16)


def _inproj(x, moe, gate, sc, sh, g, w_bf, b_forget, tables, pq):
    fuse = moe is not None
    tpb = SEQ // TM
    row = pl.BlockSpec((TM, D_MODEL), lambda i: (i, 0))
    per_batch = pl.BlockSpec((None, 1, D_MODEL), lambda i: (i // tpb, 0, 0))
    const = lambda shape: pl.BlockSpec(shape, lambda i: (0,) * len(shape))
    tab = pl.BlockSpec((TM, LANES), lambda i: (i, 0))
    in_specs = [row]
    args = [x]
    if fuse:
        in_specs += [tab, pl.BlockSpec((TOP_K, N_PLANES, TM, LANES), lambda i: (0, 0, i, 0)), per_batch]
        args += [moe[0], moe[1], gate]
    in_specs += [per_batch, per_batch, const((1, D_MODEL)), const((D_MODEL, IN_COLS_PAD)),
                 const((1, LANES)), tab, tab, tab, const((LANES, 2 * QK_WIDTH))]
    args += [sc, sh, g.reshape(1, D_MODEL), w_bf, b_forget, *tables, pq]
    wide = pl.BlockSpec((TM, QK_WIDTH), lambda i: (i, 0))
    def vspec(heads, width):
        rows = width + ONES_ROWS
        return (pl.BlockSpec((None, heads, None, rows, TM), lambda i: (i // tpb, 0, i % tpb, 0, 0)),
                jax.ShapeDtypeStruct((BATCH, heads, tpb, rows, TM), BF16))

    wide_s = jax.ShapeDtypeStruct((N_TOK, QK_WIDTH), BF16)
    dv_spec, dv_s = vspec(N_DIFF_HEADS, 2 * HEAD_DIM)
    fv_spec, fv_s = vspec(N_FOX_HEADS, HEAD_DIM)
    out_specs = [wide, wide, dv_spec, wide, wide, fv_spec]
    out_shape = [wide_s, wide_s, dv_s, wide_s, wide_s, fv_s]
    if fuse:
        out_specs = [row] + out_specs
        out_shape = [jax.ShapeDtypeStruct((N_TOK, D_MODEL), F32)] + out_shape
    outs = pl.pallas_call(
        functools.partial(_inproj_kernel, fuse),
        grid=(N_TOK // TM,),
        in_specs=in_specs,
        out_specs=out_specs,
        out_shape=out_shape,
        scratch_shapes=[pltpu.VMEM((8, LANES), F32)],
        compiler_params=_params("arbitrary"),
        name="norm_inproj",
    )(*args)
    if fuse:
        return outs[0], outs[1:]
    return x, outs


def _attn_kernel(diff, lambda_init, qa_ref, qb_ref, ka_ref, kb_ref, v_ref, g_ref, lam_ref, o_ref,
                 *scratch):
    nq = SEQ // TQ
    n_half = TQ // ATTN_TQ
    feat = 2 * HEAD_DIM if diff else HEAD_DIM
    chains = []
    for mi, (q_ref, k_ref) in enumerate(((qa_ref, ka_ref), (qb_ref, kb_ref))):
        for h in range(n_half):
            c = mi * n_half + h
            qt_sc, s_sc, p_sc, m_sc, a_sc, acc_sc = scratch[c::2 * n_half]
            vh = 0 if diff else mi
            chains.append((h, k_ref, qt_sc, s_sc, p_sc, m_sc, a_sc, acc_sc, q_ref, vh))
    order = [chains[mi * n_half + h] for h in range(n_half) for mi in range(2)]
    early, late = order[:-N_LATE_CHAINS], order[-N_LATE_CHAINS:]

    def load_queries(qi):
        for mi in range(2):
            q_ref = chains[mi * n_half][8]
            qt = q_ref[pl.ds(pl.multiple_of(qi * TQ, TQ), TQ), :].astype(F32).T.astype(BF16)
            for h in range(n_half):
                chains[mi * n_half + h][2][...] = qt[:, h * ATTN_TQ:(h + 1) * ATTN_TQ]

    def reset_state():
        for chain in chains:
            m_sc, _, acc_sc = chain[5:8]
            m_sc[...] = jnp.full(m_sc.shape, NEG, F32)
            acc_sc[...] = jnp.zeros(acc_sc.shape, F32)

    def n_keys(chain, masked):
        return (chain[0] + 1) * ATTN_TQ if masked else TQ

    def scores(chain, j, masked):
        h, k_ref, qt_sc, s_sc = chain[:4]
        nk = n_keys(chain, masked)
        off = pl.multiple_of(j * TQ, TQ)
        s = jnp.dot(k_ref[pl.ds(off, nk), :], qt_sc[...], preferred_element_type=F32)
        if masked:
            kk = lax.broadcasted_iota(jnp.int32, (nk, ATTN_TQ), 0)
            qq = h * ATTN_TQ + lax.broadcasted_iota(jnp.int32, (nk, ATTN_TQ), 1)
            s = jnp.where((kk // CHUNK <= qq // CHUNK) if diff else (kk <= qq), s, NEG)
        s_sc[0:nk, :] = s

    def softmax(chain, masked):
        s_sc, p_sc, m_sc, a_sc = chain[3:7]
        nk = n_keys(chain, masked)
        m_all = m_sc[...]
        m_parts = []
        for c0 in range(0, ATTN_TQ, LANES):
            cols = slice(c0, c0 + LANES)
            pm = s_sc[0:ATTN_ROWS, cols]
            for r0 in range(ATTN_ROWS, nk, ATTN_ROWS):
                pm = jnp.maximum(pm, s_sc[r0:r0 + ATTN_ROWS, cols])
            m_new = jnp.maximum(m_all[:, cols], jnp.max(pm, axis=0, keepdims=True))
            for r0 in range(0, nk, ATTN_ROWS):
                p = jnp.exp2(s_sc[r0:r0 + ATTN_ROWS, cols] - m_new)
                p_sc[r0:r0 + ATTN_ROWS, cols] = p.astype(BF16)
            m_parts.append(m_new)
        m_new = jnp.concatenate(m_parts, axis=1)
        a_sc[...] = jnp.exp2(m_all - m_new)
        m_sc[...] = m_new

    def values(chain, j, masked=False):
        p_sc, a_sc, acc_sc, vh = chain[4], chain[6], chain[7], chain[9]
        nk = n_keys(chain, masked)
        pv = jnp.dot(v_ref[vh, j, :, 0:nk], p_sc[0:nk, :], preferred_element_type=F32)
        acc_sc[...] = a_sc[...] * acc_sc[...] + pv

    def idle_late():
        for chain in late:
            chain[4][...] = jnp.zeros(chain[4].shape, BF16)
            chain[6][...] = jnp.ones(chain[6].shape, F32)

    def consume(j, cur_masked=False, nxt=None, nxt_masked=False, final=False, before_next=None):
        def open_late(chain):
            scores(chain, j, cur_masked)
            values(chain, jnp.maximum(j - 1, 0))

        open_late(late[0])
        for i, chain in enumerate(early):
            softmax(chain, cur_masked)
            if i == 0:
                for other in late[1:]:
                    open_late(other)
                if before_next is not None:
                    before_next()
            if nxt is not None:
                scores(chain, nxt, nxt_masked)
            values(chain, j, cur_masked)
        for chain in late:
            softmax(chain, cur_masked)
        if final:
            for chain in late:
                values(chain, j, cur_masked)

    def finalize(qi):
        ot = [jnp.concatenate([chains[mi * n_half + h][7][0:feat] / chains[mi * n_half + h][7][feat:feat + 1]
                               for h in range(n_half)], axis=1) for mi in range(2)]
        g = g_ref[...]
        rows = pl.ds(pl.multiple_of(qi * TQ, TQ), TQ)
        if diff:
            lv = lam_ref[...]
            lam = (jnp.exp(jnp.sum(lv[0:1] * lv[1:2], axis=1, keepdims=True))
                   - jnp.exp(jnp.sum(lv[2:3] * lv[3:4], axis=1, keepdims=True)) + lambda_init)
            o = (ot[0] - lam * ot[1]).T
            y = o * lax.rsqrt(jnp.mean(o * o, axis=1, keepdims=True) + EPS) * g
            o_ref[rows, :] = (y * (1.0 - lambda_init)).astype(o_ref.dtype)
        else:
            o = jnp.concatenate(ot, axis=0).T
            low = _lane_iota((TQ, LANES)) < HEAD_DIM
            sq = o * o
            msa = jnp.sum(jnp.where(low, sq, 0.0), axis=1, keepdims=True) / HEAD_DIM
            msb = jnp.sum(jnp.where(low, 0.0, sq), axis=1, keepdims=True) / HEAD_DIM
            inv = jnp.where(low, lax.rsqrt(msa + EPS), lax.rsqrt(msb + EPS))
            o_ref[rows, :] = (o * inv * g).astype(o_ref.dtype)

    load_queries(0)
    reset_state()
    idle_late()
    for chain in early:
        scores(chain, 0, True)

    @pl.loop(0, nq)
    def _(qi):
        n_plain = jnp.maximum(qi - 1, 0)

        @pl.loop(0, n_plain // 2)
        def _(t):
            consume(2 * t, nxt=2 * t + 1)
            consume(2 * t + 1, nxt=2 * t + 2)

        @pl.when(n_plain % 2 == 1)
        def _():
            consume(qi - 2, nxt=qi - 1)

        def last_blocks(to_next_tile):
            if to_next_tile:
                consume(qi, cur_masked=True, nxt=0, final=True, before_next=lambda: load_queries(qi + 1))
            else:
                consume(qi, cur_masked=True, final=True)
            finalize(qi)

        for to_next_tile in (True, False):
            more = (qi < nq - 1) if to_next_tile else (qi == nq - 1)

            @pl.when(more & (qi > 0))
            def _():
                consume(qi - 1, nxt=qi, nxt_masked=True)
                last_blocks(to_next_tile)

            if to_next_tile:
                @pl.when(qi == 0)
                def _():
                    last_blocks(to_next_tile)

        reset_state()
        idle_late()


def _attention(diff, lambda_init, q, k, v, g, lamv):
    nq = SEQ // TQ
    kspec = lambda par: pl.BlockSpec((SEQ, LANES), lambda b, p: (b, 2 * p + par))
    return pl.pallas_call(
        functools.partial(_attn_kernel, diff, lambda_init),
        grid=(BATCH, 4),
        in_specs=[kspec(0), kspec(1), kspec(0), kspec(1),
                  pl.BlockSpec((None, v.shape[1] // 4, nq, v.shape[3], TQ), lambda b, p: (b, p, 0, 0, 0)),
                  pl.BlockSpec((1, LANES), lambda b, p: (0, 0)),
                  pl.BlockSpec((8, LANES), lambda b, p: (0, 0))],
        out_specs=pl.BlockSpec((SEQ, LANES), lambda b, p: (b, p)),
        out_shape=jax.ShapeDtypeStruct((N_TOK, DIFF_WIDTH), BF16),
        scratch_shapes=[pltpu.VMEM(shape, dt)
                        for shape, dt in (((LANES, ATTN_TQ), BF16), ((TQ, ATTN_TQ), F32),
                                          ((TQ, ATTN_TQ), BF16), ((1, ATTN_TQ), F32),
                                          ((1, ATTN_TQ), F32), ((v.shape[3], ATTN_TQ), F32))
                        for _ in range(2 * TQ // ATTN_TQ)],
        compiler_params=_params("arbitrary", "arbitrary"),
        name="diff_attention" if diff else "fox_attention",
    )(q, q, k, k, v, g, lamv)


def _outproj_kernel(x_ref, od_ref, of_ref, gt_ref, sc_ref, sh_ref, g_ref, wo_ref, wr_ref, br_ref,
                    x1_ref, h2_ref, rt_ref, cnt_ref, carry_ref):
    @pl.when(pl.program_id(0) == 0)
    def _():
        carry_ref[...] = jnp.zeros_like(carry_ref)

    mix = jnp.dot(od_ref[...], wo_ref[0:DIFF_WIDTH, :], preferred_element_type=F32)
    mix += jnp.dot(of_ref[...], wo_ref[DIFF_WIDTH:, :], preferred_element_type=F32)
    x1 = x_ref[...] + gt_ref[...] * mix
    x1_ref[...] = x1
    h = _rms_mod(x1, g_ref[...], sc_ref[...], sh_ref[...])
    hh = h.astype(BF16)
    _pack_planes(h, h2_ref)
    hl = (h - hh.astype(F32)).astype(BF16)
    r1 = jnp.dot(hh, wr_ref[...], preferred_element_type=F32)
    r2 = jnp.dot(hl, wr_ref[:, 0:LANES], preferred_element_type=F32)
    logits = r1[:, 0:LANES] + r1[:, LANES:] + r2 + br_ref[...]

    lane = _lane_iota((TM, LANES))
    lanef = lane.astype(F32)
    big = float(LANES)
    isg = lane < N_GROUPS
    lg = jnp.where(isg, logits, NEG)
    mg = jnp.max(lg, axis=1, keepdims=True)
    sg = jnp.sum(jnp.where(isg, jnp.exp(lg - mg), 0.0), axis=1, keepdims=True)
    p_g = 1.0 / sg
    gsel = jnp.min(jnp.where(isg & (lg == mg), lanef, big), axis=1, keepdims=True)
    lo = N_GROUPS + gsel * EXPERTS_PER_GROUP
    ise = (lanef >= lo) & (lanef < lo + EXPERTS_PER_GROUP)
    le = jnp.where(ise, logits, NEG)
    t1 = jnp.max(le, axis=1, keepdims=True)
    i1 = jnp.min(jnp.where(ise & (le == t1), lanef, big), axis=1, keepdims=True)
    ise2 = ise & (lanef != i1)
    le2 = jnp.where(ise2, logits, NEG)
    t2 = jnp.max(le2, axis=1, keepdims=True)
    i2 = jnp.min(jnp.where(ise2 & (le2 == t2), lanef, big), axis=1, keepdims=True)
    d = jnp.exp(t2 - t1)
    w1 = p_g / (1.0 + d)
    w2 = p_g * d / (1.0 + d)
    e1 = i1 - N_GROUPS
    e2 = i2 - N_GROUPS
    oh1 = lanef == e1
    oh2 = lanef == e2
    both = jnp.where(oh1 | oh2, 1.0, 0.0)
    row = lax.broadcasted_iota(jnp.int32, (TM, TM), 0)
    col = lax.broadcasted_iota(jnp.int32, (TM, TM), 1)
    before = jnp.dot((row > col).astype(BF16), both.astype(BF16), preferred_element_type=F32)
    before = before + carry_ref[0:1, :]
    rank1 = jnp.sum(jnp.where(oh1, before, 0.0), axis=1, keepdims=True)
    rank2 = jnp.sum(jnp.where(oh2, before, 0.0), axis=1, keepdims=True)
    total = carry_ref[0:1, :] + jnp.sum(both, axis=0, keepdims=True)
    carry_ref[...] = jnp.broadcast_to(total, carry_ref.shape)
    cnt_ref[...] = jnp.broadcast_to(total, cnt_ref.shape)
    vals = (e1, e2, w1, w2, rank1, rank2)
    out = jnp.zeros((TM, LANES), F32)
    for j, v in enumerate(vals):
        out = jnp.where(lane == j, v, out)
    rt_ref[...] = out


def _outproj(x, od, of, gt, sc, sh, g, wo_bf, wr, br):
    tpb = SEQ // TM
    row = pl.BlockSpec((TM, D_MODEL), lambda i: (i, 0))
    half = pl.BlockSpec((TM, DIFF_WIDTH), lambda i: (i, 0))
    per_batch = pl.BlockSpec((None, 1, D_MODEL), lambda i: (i // tpb, 0, 0))
    const = lambda shape: pl.BlockSpec(shape, lambda i: (0,) * len(shape))
    return pl.pallas_call(
        _outproj_kernel,
        grid=(N_TOK // TM,),
        in_specs=[row, half, half, per_batch, per_batch, per_batch, const((1, D_MODEL)),
                  const((D_MODEL, D_MODEL)), const((D_MODEL, 2 * LANES)), const((1, LANES))],
        out_specs=[row, pl.BlockSpec((N_PLANES, TM, LANES), lambda i: (0, i, 0)),
                   pl.BlockSpec((TM, LANES), lambda i: (i, 0)), const((8, LANES))],
        out_shape=[jax.ShapeDtypeStruct((N_TOK, D_MODEL), F32),
                   jax.ShapeDtypeStruct((N_PLANES, N_TOK, LANES), jnp.int32),
                   jax.ShapeDtypeStruct((N_TOK, LANES), F32),
                   jax.ShapeDtypeStruct((8, LANES), F32)],
        scratch_shapes=[pltpu.VMEM((8, LANES), F32)],
        compiler_params=_params("arbitrary"),
        name="outproj_router",
    )(x, od, of, gt, sc, sh, g.reshape(1, D_MODEL), wo_bf, wr, br)


def _expert_kernel(be_ref, cnt_ref, xs_ref, wg_ref, wu_ref, wd_ref, ys_ref, wg_sc, wu_sc, wd_sc):
    i = pl.program_id(0)
    cnt = cnt_ref[i]

    @pl.when((i == 0) | (be_ref[i] != be_ref[jnp.maximum(i - 1, 0)]))
    def _():
        wg_sc[...] = wg_ref[...].astype(BF16)
        wu_sc[...] = wu_ref[...].astype(BF16)
        wd_sc[...] = wd_ref[...].astype(BF16)

    @pl.when(cnt > 0)
    def _():
        live = lax.broadcasted_iota(jnp.int32, (MOE_BLOCK, LANES), 0) < cnt
        xb = _unpack_planes([jnp.where(live, xs_ref[p], 0) for p in range(N_PLANES)]).astype(BF16)
        a = jnp.dot(xb, wg_sc[...], preferred_element_type=F32)
        u = jnp.dot(xb, wu_sc[...], preferred_element_type=F32)
        hid = (a / (1.0 + jnp.exp(-a)) * u).astype(BF16)
        _pack_planes(jnp.dot(hid, wd_sc[...], preferred_element_type=F32), ys_ref)

    @pl.when(cnt == 0)
    def _():
        ys_ref[...] = jnp.zeros_like(ys_ref)


def _experts(layer, block_expert, block_count, xs, wg, wu, wd):
    planes = pl.BlockSpec((N_PLANES, MOE_BLOCK, LANES), lambda i, be, bc: (0, i, 0))
    w_in = pl.BlockSpec((None, None, D_MODEL, D_EXPERT), lambda i, be, bc: (layer, be[i], 0, 0))
    w_out = pl.BlockSpec((None, None, D_EXPERT, D_MODEL), lambda i, be, bc: (layer, be[i], 0, 0))
    grid_spec = pltpu.PrefetchScalarGridSpec(
        num_scalar_prefetch=2,
        grid=(MOE_NBLOCKS,),
        in_specs=[planes, w_in, w_in, w_out],
        out_specs=planes,
        scratch_shapes=[pltpu.VMEM((D_MODEL, D_EXPERT), BF16), pltpu.VMEM((D_MODEL, D_EXPERT), BF16),
                        pltpu.VMEM((D_EXPERT, D_MODEL), BF16)],
    )
    return pl.pallas_call(
        _expert_kernel,
        grid_spec=grid_spec,
        out_shape=jax.ShapeDtypeStruct((N_PLANES, MOE_ROWS, LANES), jnp.int32),
        compiler_params=_params("arbitrary"),
        name="expert_mlp",
    )(block_expert, block_count, xs, wg, wu, wd)


def _slots(route, counts):
    counts = counts[0, :N_EXPERTS].astype(jnp.int32)
    padded = ((counts + MOE_BLOCK - 1) // MOE_BLOCK) * MOE_BLOCK
    pend = jnp.cumsum(padded)
    pstart = pend - padded
    bstart = jnp.arange(MOE_NBLOCKS, dtype=jnp.int32) * MOE_BLOCK
    block_expert = jnp.minimum(jnp.sum(bstart[:, None] >= pend[None, :], axis=1), N_EXPERTS - 1)
    block_expert = block_expert.astype(jnp.int32)
    mine = block_expert[:, None] == jnp.arange(N_EXPERTS, dtype=jnp.int32)[None, :]
    left = jnp.sum(jnp.where(mine, counts + pstart, 0), axis=1) - bstart
    block_count = jnp.clip(left, 0, MOE_BLOCK).astype(jnp.int32)
    base = jnp.pad(pstart.astype(F32), (0, LANES - N_EXPERTS)).reshape(1, LANES)
    return _slot_rows(route, base), block_expert, block_count


def _slot_rows_kernel(rt_ref, base_ref, o_ref):
    rt = rt_ref[...]
    lanef = _lane_iota(rt.shape).astype(F32)
    base = base_ref[...]
    dest = jnp.zeros(rt.shape, F32)
    for k in range(TOP_K):
        b = jnp.sum(jnp.where(lanef == rt[:, k:k + 1], base, 0.0), axis=1, keepdims=True)
        dest = jnp.where(lanef == k, b + rt[:, 2 * TOP_K + k:2 * TOP_K + k + 1], dest)
    dest_t = dest.T.astype(jnp.int32)
    for k in range(TOP_K):
        for p in range(N_PLANES):
            for c in range(rt.shape[0] // LANES):
                o_ref[k * N_PLANES + p, c:c + 1, :] = dest_t[k:k + 1, c * LANES:(c + 1) * LANES] + p * MOE_ROWS


def _slot_rows(route, base):
    tm = 8 * LANES
    return pl.pallas_call(
        _slot_rows_kernel,
        grid=(N_TOK // tm,),
        in_specs=[pl.BlockSpec((tm, LANES), lambda i: (i, 0)), pl.BlockSpec((1, LANES), lambda i: (0, 0))],
        out_specs=pl.BlockSpec((TOP_K * N_PLANES, tm // LANES, LANES), lambda i: (0, i, 0)),
        out_shape=jax.ShapeDtypeStruct((TOP_K * N_PLANES, N_TOK // LANES, LANES), jnp.int32),
        compiler_params=_params("arbitrary"),
        name="slot_rows",
    )(route, base)


def _sc_workers():
    info = plsc.get_sparse_core_info()
    return info.num_cores, info.num_cores * info.num_subcores


def _sc_scatter2(src, idx, out_rows):
    n_win = src.shape[0] // SC_WINDOW
    nc, nw = _sc_workers()
    steps = n_win // nw
    mesh = plsc.VectorSubcoreMesh(core_axis_name="c", subcore_axis_name="s")

    @functools.partial(
        pl.kernel, mesh=mesh,
        out_type=jax.ShapeDtypeStruct((out_rows, LANES), src.dtype),
        scratch_types=[pltpu.VMEM((2 * steps, SC_WINDOW), jnp.int32),
                       pltpu.VMEM((SC_INFLIGHT, SC_WINDOW, LANES), src.dtype),
                       pltpu.SemaphoreType.DMA((SC_INFLIGHT,)), pltpu.SemaphoreType.DMA((SC_INFLIGHT,))],
        name="sc_dispatch_scatter",
    )
    def k(src_hbm, idx_hbm, out_hbm, idx_v, rows_v, lsem, wsem):
        first = (lax.axis_index("s") * nc + lax.axis_index("c")) * steps
        pltpu.sync_copy(idx_hbm.at[pl.ds(first, steps)], idx_v.at[pl.ds(0, steps)])
        pltpu.sync_copy(idx_hbm.at[pl.ds(n_win + first, steps)], idx_v.at[pl.ds(steps, steps)])

        @pl.loop(0, steps, step=SC_INFLIGHT)
        def _(j):
            loads = [pltpu.async_copy(src_hbm.at[pl.ds((first + j + b) * SC_WINDOW, SC_WINDOW)],
                                      rows_v.at[b], lsem.at[b]) for b in range(SC_INFLIGHT)]
            writes = []
            for b in range(SC_INFLIGHT):
                loads[b].wait()
                for half in range(TOP_K):
                    dst = out_hbm.at[idx_v.at[half * steps + j + b]]
                    writes.append(pltpu.async_copy(rows_v.at[b], dst, wsem.at[b]))
            for w in writes:
                w.wait()

    return k(src, idx)


def _sc_gather(table, idx):
    n_out = idx.shape[0] * SC_WINDOW
    nc, nw = _sc_workers()
    steps = n_out // nw // SC_WINDOW
    mesh = plsc.VectorSubcoreMesh(core_axis_name="c", subcore_axis_name="s")

    @functools.partial(
        pl.kernel, mesh=mesh,
        out_type=jax.ShapeDtypeStruct((n_out, LANES), table.dtype),
        scratch_types=[pltpu.VMEM((steps, SC_WINDOW), jnp.int32),
                       pltpu.VMEM((SC_INFLIGHT, SC_WINDOW, LANES), table.dtype),
                       pltpu.SemaphoreType.DMA((SC_INFLIGHT,)), pltpu.SemaphoreType.DMA((SC_INFLIGHT,))],
        name="sc_combine_gather",
    )
    def k(table_hbm, idx_hbm, out_hbm, idx_v, rows_v, gsem, wsem):
        first = (lax.axis_index("s") * nc + lax.axis_index("c")) * steps
        pltpu.sync_copy(idx_hbm.at[pl.ds(first, steps)], idx_v)

        @pl.loop(0, steps, step=SC_INFLIGHT)
        def _(j):
            gathers = [pltpu.async_copy(table_hbm.at[idx_v.at[j + b]], rows_v.at[b], gsem.at[b])
                       for b in range(SC_INFLIGHT)]
            writes = []
            for b in range(SC_INFLIGHT):
                gathers[b].wait()
                dst = out_hbm.at[pl.ds((first + j + b) * SC_WINDOW, SC_WINDOW)]
                writes.append(pltpu.async_copy(rows_v.at[b], dst, wsem.at[b]))
            for w in writes:
                w.wait()

    return k(table, idx)


def _final_kernel(x_ref, rt_ref, y_ref, gt_ref, g_ref, o_ref):
    x = x_ref[...] + gt_ref[...] * _combine(rt_ref, y_ref)
    ms = jnp.mean(x * x, axis=-1, keepdims=True)
    o_ref[...] = x * lax.rsqrt(ms + EPS) * g_ref[...]


def _final(x, moe, gate, g):
    tpb = SEQ // TM
    row = pl.BlockSpec((TM, D_MODEL), lambda i: (i, 0))
    return pl.pallas_call(
        _final_kernel,
        grid=(N_TOK // TM,),
        in_specs=[row, pl.BlockSpec((TM, LANES), lambda i: (i, 0)),
                  pl.BlockSpec((TOP_K, N_PLANES, TM, LANES), lambda i: (0, 0, i, 0)),
                  pl.BlockSpec((None, 1, D_MODEL), lambda i: (i // tpb, 0, 0)),
                  pl.BlockSpec((1, D_MODEL), lambda i: (0, 0))],
        out_specs=row,
        out_shape=jax.ShapeDtypeStruct((N_TOK, D_MODEL), F32),
        compiler_params=_params("arbitrary"),
        name="final_norm",
    )(x, moe[0], moe[1], gate, g.reshape(1, D_MODEL))


def kernel(x, c, positions, w_ada, b_ada, g_mix, w_in, b_forget, lambda_q1, lambda_k1, lambda_q2,
           lambda_k2, g_subln, g_fox_out, w_out, g_ffn, w_router_group, b_router_group,
           w_router_expert, b_router_expert, w_expert_gate, w_expert_up, w_expert_down, g_final):
    mod = _modulation(c, w_ada, b_ada)
    mod = mod.reshape(DEPTH, BATCH, 6, 1, D_MODEL)
    tables = _rope_tables(positions)
    pq = _forget_placement()
    xf = x.reshape(N_TOK, D_MODEL)
    moe = None
    gate = None
    for l in range(DEPTH):
        sh1, sc1, gt1, sh2, sc2, gt2 = (mod[l, :, j] for j in range(6))
        w_bf = jnp.pad(w_in[l], ((0, 0), (0, IN_COLS_PAD - IN_COLS))).astype(BF16)
        bfp = jnp.pad(b_forget[l], (0, LANES - N_FOX_HEADS)).reshape(1, LANES)
        xf, (dq, dk, dv, fq, fk, fv) = _inproj(xf, moe, gate, sc1, sh1, g_mix[l], w_bf, bfp, tables, pq)

        lambda_init = 0.8 - 0.6 * float(np.exp(-0.3 * l))
        lamv = jnp.zeros((8, LANES), F32).at[0:4, 0:HEAD_DIM].set(
            jnp.stack([lambda_q1[l], lambda_k1[l], lambda_q2[l], lambda_k2[l]]))
        g_d = g_subln[l].reshape(1, LANES)
        g_f = jnp.concatenate([g_fox_out[l], g_fox_out[l]]).reshape(1, LANES)
        od = _attention(True, lambda_init, dq, dk, dv, g_d, lamv)
        of = _attention(False, lambda_init, fq, fk, fv, g_f, lamv)

        wr32 = jnp.pad(jnp.concatenate([w_router_group[l], w_router_expert[l]], axis=1),
                       ((0, 0), (0, LANES - N_GROUPS - N_EXPERTS)))
        wr_hi = wr32.astype(BF16)
        wr_lo = (wr32 - wr_hi.astype(F32)).astype(BF16)
        wr = jnp.concatenate([wr_hi, wr_lo], axis=1)
        br = jnp.pad(jnp.concatenate([b_router_group[l], b_router_expert[l]]),
                     (0, LANES - N_GROUPS - N_EXPERTS)).reshape(1, LANES)
        xf, h2, route, counts = _outproj(xf, od, of, gt1, sc2, sh2, g_ffn[l], w_out[l].astype(BF16),
                                         wr, br)

        rows, block_expert, block_count = _slots(route, counts)
        rows = rows.reshape(TOP_K * N_PLANES * N_TOK // SC_WINDOW, SC_WINDOW)
        xs = _sc_scatter2(h2.reshape(N_PLANES * N_TOK, LANES), rows, N_PLANES * MOE_ROWS)
        ys = _experts(l, block_expert, block_count, xs.reshape(N_PLANES, MOE_ROWS, LANES),
                      w_expert_gate, w_expert_up, w_expert_down)
        y2 = _sc_gather(ys.reshape(N_PLANES * MOE_ROWS, LANES), rows)
        moe = (route, y2.reshape(TOP_K, N_PLANES, N_TOK, LANES))
        gate = gt2
    out = _final(xf, moe, gate, g_final)
    return out.reshape(BATCH, SEQ, D_MODEL)
```

```python
import functools

import numpy as np
import jax
import jax.numpy as jnp
from jax import lax
from jax.experimental import pallas as pl
from jax.experimental.pallas import tpu as pltpu
from jax.experimental.pallas import tpu_sc as plsc

D_MODEL = 1024
BATCH = 4
SEQ = 4096
DEPTH = 4
N_TOK = BATCH * SEQ

CHUNK = 64
HEAD_DIM = 64
N_DIFF_HEADS = 4
N_FOX_HEADS = 8
DIFF_WIDTH = 512
FOX_WIDTH = 512
IN_COLS = 3 * DIFF_WIDTH + 3 * FOX_WIDTH + N_FOX_HEADS
ROT_DIM = 16
ROPE_THETA = 500000.0
N_GROUPS = 4
EXPERTS_PER_GROUP = 8
N_EXPERTS = 32
TOP_K = 2
D_EXPERT = 512
EPS = 1e-6

LANES = 128
IN_COLS_PAD = 3200
FF_COL = 3 * DIFF_WIDTH + 3 * FOX_WIDTH
QK_WIDTH = 8 * LANES
TM = 512
TQ = 512
ATTN_TQ = 256
N_LATE_CHAINS = 1
ONES_ROWS = 16
ATTN_ROWS = 64
LOG2E = 1.4426950408889634
MOE_BLOCK = 512
MOE_ROWS = N_TOK * TOP_K + N_EXPERTS * MOE_BLOCK
MOE_NBLOCKS = MOE_ROWS // MOE_BLOCK
PLANE_ROWS = MOE_ROWS + 8 * 17
N_PLANES = D_MODEL // 2 // LANES
SC_WINDOW = 128
SC_INFLIGHT = 4
NEG = -1e30
VMEM_LIMIT = 56 * 1024 * 1024

F32 = jnp.float32
BF16 = jnp.bfloat16


def _bf16_round(x):
    return x.astype(BF16).astype(F32)


def _lane_iota(shape):
    return lax.broadcasted_iota(jnp.int32, shape, 1)


def _params(*sem):
    return pltpu.CompilerParams(dimension_semantics=sem, vmem_limit_bytes=VMEM_LIMIT)


def _pack_planes(y, o_ref):
    bits = lax.bitcast_convert_type(_bf16_round(y), jnp.uint32)
    half = D_MODEL // 2
    word = bits[:, half:] | lax.shift_right_logical(bits[:, :half], jnp.uint32(16))
    word = lax.bitcast_convert_type(word, jnp.int32)
    for p in range(N_PLANES):
        o_ref[p] = word[:, p * LANES:(p + 1) * LANES]


def _unpack_planes(planes):
    lo, hi = [], []
    for w in planes:
        u = lax.bitcast_convert_type(w, jnp.uint32)
        lo.append(lax.bitcast_convert_type(lax.shift_left(u, jnp.uint32(16)), F32))
        hi.append(lax.bitcast_convert_type(u & jnp.uint32(0xFFFF0000), F32))
    return jnp.concatenate(lo + hi, axis=1)


def _combine(route_ref, y_ref):
    rt = route_ref[...]
    y0 = _unpack_planes([y_ref[0, p] for p in range(N_PLANES)])
    y1 = _unpack_planes([y_ref[1, p] for p in range(N_PLANES)])
    return rt[:, 2:3] * y0 + rt[:, 3:4] * y1


def _mod_kernel(c_ref, w_ref, b_ref, o_ref):
    c = c_ref[...]
    cond = c / (1.0 + jnp.exp(-c))
    ch = cond.astype(BF16)
    cl = (cond - ch.astype(F32)).astype(BF16)
    w = w_ref[...]
    wh = w.astype(BF16)
    wl = (w - wh.astype(F32)).astype(BF16)
    acc = jnp.dot(ch, wh, preferred_element_type=F32)
    acc += jnp.dot(cl, wh, preferred_element_type=F32)
    acc += jnp.dot(ch, wl, preferred_element_type=F32)
    o_ref[...] = acc + b_ref[...]


def _modulation(c, w_ada, b_ada):
    rows = 16
    tn = 1536
    c_pad = jnp.zeros((rows, D_MODEL), F32).at[:BATCH].set(c)
    out = pl.pallas_call(
        _mod_kernel,
        grid=(DEPTH, 6 * D_MODEL // tn),
        in_specs=[
            pl.BlockSpec((rows, D_MODEL), lambda l, n: (0, 0)),
            pl.BlockSpec((None, D_MODEL, tn), lambda l, n: (l, 0, n)),
            pl.BlockSpec((None, 1, tn), lambda l, n: (l, 0, n)),
        ],
        out_specs=pl.BlockSpec((None, rows, tn), lambda l, n: (l, 0, n)),
        out_shape=jax.ShapeDtypeStruct((DEPTH, rows, 6 * D_MODEL), F32),
        compiler_params=_params("arbitrary", "arbitrary"),
        name="adaln_mod",
    )(c_pad, w_ada, b_ada.reshape(DEPTH, 1, 6 * D_MODEL))
    return out[:, :BATCH]


def _rope_kernel(pos_ref, inv_ref, c_ref, sa_ref, sb_ref):
    ang = pos_ref[...].astype(F32) * inv_ref[...]
    j = _lane_iota(ang.shape) % HEAD_DIM
    cosv = jnp.cos(ang)
    sinv = jnp.sin(ang)
    half = ROT_DIM // 2
    c_ref[...] = jnp.where(j < ROT_DIM, cosv, 1.0)
    sa_ref[...] = jnp.where(j < half, -sinv, 0.0)
    sb_ref[...] = jnp.where((j >= half) & (j < ROT_DIM), sinv, 0.0)


def _rope_tables(positions):
    half = ROT_DIM // 2
    inv = ROPE_THETA ** (-jnp.arange(0, ROT_DIM, 2, dtype=F32) / ROT_DIM)
    lane = np.arange(LANES)
    inv_lane = inv[(lane % HEAD_DIM) % half].reshape(1, LANES)
    spec = pl.BlockSpec((TM, LANES), lambda i: (i, 0))
    shape = jax.ShapeDtypeStruct((N_TOK, LANES), F32)
    return pl.pallas_call(
        _rope_kernel,
        grid=(N_TOK // TM,),
        in_specs=[pl.BlockSpec((TM, 1), lambda i: (i, 0)),
                  pl.BlockSpec((1, LANES), lambda i: (0, 0))],
        out_specs=[spec, spec, spec],
        out_shape=[shape, shape, shape],
        compiler_params=_params("arbitrary"),
        name="rope_tables",
    )(positions.reshape(N_TOK, 1), inv_lane)


def _rms_mod(x, g, sc, sh):
    ms = jnp.mean(x * x, axis=-1, keepdims=True)
    return (x * lax.rsqrt(ms + EPS) * g) * (1.0 + sc) + sh


def _inproj_kernel(fuse, *refs):
    if fuse:
        (x_ref, rt_ref, y_ref, gt_ref, sc_ref, sh_ref, g_ref, w_ref, bf_ref, c_ref, sa_ref, sb_ref,
         pq_ref, xo_ref, dq_ref, dk_ref, dv_ref, fq_ref, fk_ref, fv_ref, carry_ref) = refs
        x = x_ref[...] + gt_ref[...] * _combine(rt_ref, y_ref)
        xo_ref[...] = x
    else:
        (x_ref, sc_ref, sh_ref, g_ref, w_ref, bf_ref, c_ref, sa_ref, sb_ref,
         pq_ref, dq_ref, dk_ref, dv_ref, fq_ref, fk_ref, fv_ref, carry_ref) = refs
        x = x_ref[...]
    hb = _rms_mod(x, g_ref[...], sc_ref[...], sh_ref[...]).astype(BF16)

    @pl.when(pl.program_id(0) % (SEQ // TM) == 0)
    def _():
        carry_ref[...] = jnp.zeros_like(carry_ref)

    lane = _lane_iota((TM, LANES))
    nh = N_FOX_HEADS

    def pack3(a):
        hi = _bf16_round(a)
        r1 = a - hi
        mid = _bf16_round(r1)
        lo = _bf16_round(r1 - mid)
        return jnp.where(lane < nh, hi,
                         jnp.where(lane < 2 * nh, pltpu.roll(mid, nh, 1),
                                   jnp.where(lane < 3 * nh, pltpu.roll(lo, 2 * nh, 1), 0.0)))

    z = jnp.dot(hb, w_ref[:, FF_COL:FF_COL + LANES], preferred_element_type=F32) + bf_ref[...]

    low = lane < HEAD_DIM
    rc, rsa, rsb = c_ref[...], sa_ref[...], sb_ref[...]
    scale = HEAD_DIM ** -0.5 * LOG2E

    def split_store(chunk, o_ref, m, extra_a=None, extra_b=None):
        a = jnp.where(low, chunk, 0.0)
        b = jnp.where(low, pltpu.roll(chunk, HEAD_DIM, 1), 0.0)
        if extra_a is not None:
            a = a + extra_a
            b = b + extra_b
        o_ref[:, (2 * m) * LANES:(2 * m + 1) * LANES] = a.astype(BF16)
        o_ref[:, (2 * m + 1) * LANES:(2 * m + 2) * LANES] = b.astype(BF16)

    def rope(xc):
        return xc * rc + pltpu.roll(xc, LANES - ROT_DIM // 2, 1) * rsa + pltpu.roll(xc, ROT_DIM // 2, 1) * rsb

    pdq = jnp.dot(hb, w_ref[:, 0:DIFF_WIDTH], preferred_element_type=F32)
    for m in range(N_DIFF_HEADS):
        split_store(rope(pdq[:, m * LANES:(m + 1) * LANES]) * scale, dq_ref, m)
    pdk = jnp.dot(hb, w_ref[:, DIFF_WIDTH:2 * DIFF_WIDTH], preferred_element_type=F32)
    for m in range(N_DIFF_HEADS):
        split_store(rope(pdk[:, m * LANES:(m + 1) * LANES]), dk_ref, m)
    def store_values_t(pv, o_ref, width):
        ones = jnp.ones((ONES_ROWS, TM), BF16)
        for m in range(4):
            vt = pv[:, m * LANES:(m + 1) * LANES].T.astype(BF16)
            for i in range(LANES // width):
                o_ref[m * (LANES // width) + i, 0:width, :] = vt[i * width:(i + 1) * width]
                o_ref[m * (LANES // width) + i, width:width + ONES_ROWS, :] = ones

    store_values_t(jnp.dot(hb, w_ref[:, 2 * DIFF_WIDTH:3 * DIFF_WIDTH], preferred_element_type=F32),
                   dv_ref, 2 * HEAD_DIM)
    o = 3 * DIFF_WIDTH
    store_values_t(jnp.dot(hb, w_ref[:, o + 2 * FOX_WIDTH:o + 3 * FOX_WIDTH],
                           preferred_element_type=F32), fv_ref, HEAD_DIM)

    logf =jnp.minimum(z, 0.0) - jnp.log(1.0 + jnp.exp(-jnp.abs(z)))
    logf = jnp.where(lane < nh, logf, 0.0)
    row = lax.broadcasted_iota(jnp.int32, (TM, TM), 0)
    col = lax.broadcasted_iota(jnp.int32, (TM, TM), 1)
    tri = (row >= col).astype(BF16)
    r = jnp.dot(tri, pack3(logf).astype(BF16), preferred_element_type=F32)
    cs = r + pltpu.roll(r, LANES - nh, 1) + pltpu.roll(r, LANES - 2 * nh, 1)
    cf = jnp.where(lane < nh, cs + carry_ref[0:1, :], 0.0)
    carry_ref[...] = jnp.broadcast_to(cf[TM - 1:TM, :], carry_ref.shape)

    t3 = jnp.where(lane == 3 * nh, 1.0, pack3(cf * LOG2E)).astype(BF16)
    aug = jnp.dot(t3, pq_ref[...], preferred_element_type=F32)

    pfq =jnp.dot(hb, w_ref[:, o:o + FOX_WIDTH], preferred_element_type=F32)
    for m in range(N_FOX_HEADS // 2):
        split_store(pfq[:, m * LANES:(m + 1) * LANES] * scale, fq_ref, m,
                    aug[:, (2 * m) * LANES:(2 * m + 1) * LANES],
                    aug[:, (2 * m + 1) * LANES:(2 * m + 2) * LANES])
    pfk = jnp.dot(hb, w_ref[:, o + FOX_WIDTH:o + 2 * FOX_WIDTH], preferred_element_type=F32)
    for m in range(N_FOX_HEADS // 2):
        split_store(pfk[:, m * LANES:(m + 1) * LANES], fk_ref, m,
                    aug[:, QK_WIDTH + (2 * m) * LANES:QK_WIDTH + (2 * m + 1) * LANES],
                    aug[:, QK_WIDTH + (2 * m + 1) * LANES:QK_WIDTH + (2 * m + 2) * LANES])


def _forget_placement():
    nh = N_FOX_HEADS
    p = np.zeros((LANES, 2 * QK_WIDTH), np.float32)
    for h in range(nh):
        base_q = h * LANES + HEAD_DIM
        base_k = QK_WIDTH + h * LANES + HEAD_DIM
        for part in range(3):
            p[part * nh + h, base_q + part] = 1.0
            p[3 * nh, base_q + 3 + part] = 1.0
            p[3 * nh, base_k + part] = 1.0
            p[part * nh + h, base_k + 3 + part] = -1.0
    return jnp.asarray(p, BF16)


def _inproj(x, moe, gate, sc, sh, g, w_bf, b_forget, tables, pq):
    fuse = moe is not None
    tpb = SEQ // TM
    row = pl.BlockSpec((TM, D_MODEL), lambda i: (i, 0))
    per_batch = pl.BlockSpec((None, 1, D_MODEL), lambda i: (i // tpb, 0, 0))
    const = lambda shape: pl.BlockSpec(shape, lambda i: (0,) * len(shape))
    tab = pl.BlockSpec((TM, LANES), lambda i: (i, 0))
    in_specs = [row]
    args = [x]
    if fuse:
        in_specs += [tab, pl.BlockSpec((TOP_K, N_PLANES, TM, LANES), lambda i: (0, 0, i, 0)), per_batch]
        args += [moe[0], moe[1], gate]
    in_specs += [per_batch, per_batch, const((1, D_MODEL)), const((D_MODEL, IN_COLS_PAD)),
                 const((1, LANES)), tab, tab, tab, const((LANES, 2 * QK_WIDTH))]
    args += [sc, sh, g.reshape(1, D_MODEL), w_bf, b_forget, *tables, pq]
    wide = pl.BlockSpec((TM, QK_WIDTH), lambda i: (i, 0))
    def vspec(heads, width):
        rows = width + ONES_ROWS
        return (pl.BlockSpec((None, heads, None, rows, TM), lambda i: (i // tpb, 0, i % tpb, 0, 0)),
                jax.ShapeDtypeStruct((BATCH, heads, tpb, rows, TM), BF16))

    wide_s = jax.ShapeDtypeStruct((N_TOK, QK_WIDTH), BF16)
    dv_spec, dv_s = vspec(N_DIFF_HEADS, 2 * HEAD_DIM)
    fv_spec, fv_s = vspec(N_FOX_HEADS, HEAD_DIM)
    out_specs = [wide, wide, dv_spec, wide, wide, fv_spec]
    out_shape = [wide_s, wide_s, dv_s, wide_s, wide_s, fv_s]
    if fuse:
        out_specs = [row] + out_specs
        out_shape = [jax.ShapeDtypeStruct((N_TOK, D_MODEL), F32)] + out_shape
    outs = pl.pallas_call(
        functools.partial(_inproj_kernel, fuse),
        grid=(N_TOK // TM,),
        in_specs=in_specs,
        out_specs=out_specs,
        out_shape=out_shape,
        scratch_shapes=[pltpu.VMEM((8, LANES), F32)],
        compiler_params=_params("arbitrary"),
        name="norm_inproj",
    )(*args)
    if fuse:
        return outs[0], outs[1:]
    return x, outs


def _attn_kernel(diff, lambda_init, qa_ref, qb_ref, ka_ref, kb_ref, v_ref, g_ref, lam_ref, o_ref,
                 *scratch):
    nq = SEQ // TQ
    n_half = TQ // ATTN_TQ
    feat = 2 * HEAD_DIM if diff else HEAD_DIM
    chains = []
    for mi, (q_ref, k_ref) in enumerate(((qa_ref, ka_ref), (qb_ref, kb_ref))):
        for h in range(n_half):
            c = mi * n_half + h
            qt_sc, s_sc, p_sc, m_sc, a_sc, acc_sc = scratch[c::2 * n_half]
            vh = 0 if diff else mi
            chains.append((h, k_ref, qt_sc, s_sc, p_sc, m_sc, a_sc, acc_sc, q_ref, vh))
    order = [chains[mi * n_half + h] for h in range(n_half) for mi in range(2)]
    early, late = order[:-N_LATE_CHAINS], order[-N_LATE_CHAINS:]

    def load_queries(qi):
        for mi in range(2):
            q_ref = chains[mi * n_half][8]
            qt = q_ref[pl.ds(pl.multiple_of(qi * TQ, TQ), TQ), :].astype(F32).T.astype(BF16)
            for h in range(n_half):
                chains[mi * n_half + h][2][...] = qt[:, h * ATTN_TQ:(h + 1) * ATTN_TQ]

    def reset_state():
        for chain in chains:
            m_sc, _, acc_sc = chain[5:8]
            m_sc[...] = jnp.full(m_sc.shape, NEG, F32)
            acc_sc[...] = jnp.zeros(acc_sc.shape, F32)

    def n_keys(chain, masked):
        return (chain[0] + 1) * ATTN_TQ if masked else TQ

    def scores(chain, j, masked):
        h, k_ref, qt_sc, s_sc = chain[:4]
        nk = n_keys(chain, masked)
        off = pl.multiple_of(j * TQ, TQ)
        s = jnp.dot(k_ref[pl.ds(off, nk), :], qt_sc[...], preferred_element_type=F32)
        if masked:
            kk = lax.broadcasted_iota(jnp.int32, (nk, ATTN_TQ), 0)
            qq = h * ATTN_TQ + lax.broadcasted_iota(jnp.int32, (nk, ATTN_TQ), 1)
            s = jnp.where((kk // CHUNK <= qq // CHUNK) if diff else (kk <= qq), s, NEG)
        s_sc[0:nk, :] = s

    def softmax(chain, masked):
        s_sc, p_sc, m_sc, a_sc = chain[3:7]
        nk = n_keys(chain, masked)
        m_all = m_sc[...]
        m_parts = []
        for c0 in range(0, ATTN_TQ, LANES):
            cols = slice(c0, c0 + LANES)
            pm = s_sc[0:ATTN_ROWS, cols]
            for r0 in range(ATTN_ROWS, nk, ATTN_ROWS):
                pm = jnp.maximum(pm, s_sc[r0:r0 + ATTN_ROWS, cols])
            m_new = jnp.maximum(m_all[:, cols], jnp.max(pm, axis=0, keepdims=True))
            for r0 in range(0, nk, ATTN_ROWS):
                p = jnp.exp2(s_sc[r0:r0 + ATTN_ROWS, cols] - m_new)
                p_sc[r0:r0 + ATTN_ROWS, cols] = p.astype(BF16)
            m_parts.append(m_new)
        m_new = jnp.concatenate(m_parts, axis=1)
        a_sc[...] = jnp.exp2(m_all - m_new)
        m_sc[...] = m_new

    def values(chain, j, masked=False):
        p_sc, a_sc, acc_sc, vh = chain[4], chain[6], chain[7], chain[9]
        nk = n_keys(chain, masked)
        pv = jnp.dot(v_ref[vh, j, :, 0:nk], p_sc[0:nk, :], preferred_element_type=F32)
        acc_sc[...] = a_sc[...] * acc_sc[...] + pv

    def idle_late():
        for chain in late:
            chain[4][...] = jnp.zeros(chain[4].shape, BF16)
            chain[6][...] = jnp.ones(chain[6].shape, F32)

    def consume(j, cur_masked=False, nxt=None, nxt_masked=False, final=False, before_next=None):
        def open_late(chain):
            scores(chain, j, cur_masked)
            values(chain, jnp.maximum(j - 1, 0))

        open_late(late[0])
        for i, chain in enumerate(early):
            softmax(chain, cur_masked)
            if i == 0:
                for other in late[1:]:
                    open_late(other)
                if before_next is not None:
                    before_next()
            if nxt is not None:
                scores(chain, nxt, nxt_masked)
            values(chain, j, cur_masked)
        for chain in late:
            softmax(chain, cur_masked)
        if final:
            for chain in late:
                values(chain, j, cur_masked)

    def finalize(qi):
        ot = [jnp.concatenate([chains[mi * n_half + h][7][0:feat] / chains[mi * n_half + h][7][feat:feat + 1]
                               for h in range(n_half)], axis=1) for mi in range(2)]
        g = g_ref[...]
        rows = pl.ds(pl.multiple_of(qi * TQ, TQ), TQ)
        if diff:
            lv = lam_ref[...]
            lam = (jnp.exp(jnp.sum(lv[0:1] * lv[1:2], axis=1, keepdims=True))
                   - jnp.exp(jnp.sum(lv[2:3] * lv[3:4], axis=1, keepdims=True)) + lambda_init)
            o = (ot[0] - lam * ot[1]).T
            y = o * lax.rsqrt(jnp.mean(o * o, axis=1, keepdims=True) + EPS) * g
            o_ref[rows, :] = (y * (1.0 - lambda_init)).astype(o_ref.dtype)
        else:
            o = jnp.concatenate(ot, axis=0).T
            low = _lane_iota((TQ, LANES)) < HEAD_DIM
            sq = o * o
            msa = jnp.sum(jnp.where(low, sq, 0.0), axis=1, keepdims=True) / HEAD_DIM
            msb = jnp.sum(jnp.where(low, 0.0, sq), axis=1, keepdims=True) / HEAD_DIM
            inv = jnp.where(low, lax.rsqrt(msa + EPS), lax.rsqrt(msb + EPS))
            o_ref[rows, :] = (o * inv * g).astype(o_ref.dtype)

    load_queries(0)
    reset_state()
    idle_late()
    for chain in early:
        scores(chain, 0, True)

    @pl.loop(0, nq)
    def _(qi):
        n_plain = jnp.maximum(qi - 1, 0)

        @pl.loop(0, n_plain // 2)
        def _(t):
            consume(2 * t, nxt=2 * t + 1)
            consume(2 * t + 1, nxt=2 * t + 2)

        @pl.when(n_plain % 2 == 1)
        def _():
            consume(qi - 2, nxt=qi - 1)

        def last_blocks(to_next_tile):
            if to_next_tile:
                consume(qi, cur_masked=True, nxt=0, final=True, before_next=lambda: load_queries(qi + 1))
            else:
                consume(qi, cur_masked=True, final=True)
            finalize(qi)

        for to_next_tile in (True, False):
            more = (qi < nq - 1) if to_next_tile else (qi == nq - 1)

            @pl.when(more & (qi > 0))
            def _():
                consume(qi - 1, nxt=qi, nxt_masked=True)
                last_blocks(to_next_tile)

            if to_next_tile:
                @pl.when(qi == 0)
                def _():
                    last_blocks(to_next_tile)

        reset_state()
        idle_late()


def _attention(diff, lambda_init, q, k, v, g, lamv):
    nq = SEQ // TQ
    kspec = lambda par: pl.BlockSpec((SEQ, LANES), lambda b, p: (b, 2 * p + par))
    return pl.pallas_call(
        functools.partial(_attn_kernel, diff, lambda_init),
        grid=(BATCH, 4),
        in_specs=[kspec(0), kspec(1), kspec(0), kspec(1),
                  pl.BlockSpec((None, v.shape[1] // 4, nq, v.shape[3], TQ), lambda b, p: (b, p, 0, 0, 0)),
                  pl.BlockSpec((1, LANES), lambda b, p: (0, 0)),
                  pl.BlockSpec((8, LANES), lambda b, p: (0, 0))],
        out_specs=pl.BlockSpec((SEQ, LANES), lambda b, p: (b, p)),
        out_shape=jax.ShapeDtypeStruct((N_TOK, DIFF_WIDTH), BF16),
        scratch_shapes=[pltpu.VMEM(shape, dt)
                        for shape, dt in (((LANES, ATTN_TQ), BF16), ((TQ, ATTN_TQ), F32),
                                          ((TQ, ATTN_TQ), BF16), ((1, ATTN_TQ), F32),
                                          ((1, ATTN_TQ), F32), ((v.shape[3], ATTN_TQ), F32))
                        for _ in range(2 * TQ // ATTN_TQ)],
        compiler_params=_params("arbitrary", "arbitrary"),
        name="diff_attention" if diff else "fox_attention",
    )(q, q, k, k, v, g, lamv)


def _outproj_kernel(x_ref, od_ref, of_ref, gt_ref, sc_ref, sh_ref, g_ref, wo_ref, wr_ref, br_ref,
                    x1_ref, h2_ref, rt_ref, cnt_ref, carry_ref):
    @pl.when(pl.program_id(0) == 0)
    def _():
        carry_ref[...] = jnp.zeros_like(carry_ref)

    mix = jnp.dot(od_ref[...], wo_ref[0:DIFF_WIDTH, :], preferred_element_type=F32)
    mix += jnp.dot(of_ref[...], wo_ref[DIFF_WIDTH:, :], preferred_element_type=F32)
    x1 = x_ref[...] + gt_ref[...] * mix
    x1_ref[...] = x1
    h = _rms_mod(x1, g_ref[...], sc_ref[...], sh_ref[...])
    hh = h.astype(BF16)
    _pack_planes(h, h2_ref)
    hl = (h - hh.astype(F32)).astype(BF16)
    r1 = jnp.dot(hh, wr_ref[...], preferred_element_type=F32)
    r2 = jnp.dot(hl, wr_ref[:, 0:LANES], preferred_element_type=F32)
    logits = r1[:, 0:LANES] + r1[:, LANES:] + r2 + br_ref[...]

    lane = _lane_iota((TM, LANES))
    lanef = lane.astype(F32)
    big = float(LANES)
    isg = lane < N_GROUPS
    lg = jnp.where(isg, logits, NEG)
    mg = jnp.max(lg, axis=1, keepdims=True)
    sg = jnp.sum(jnp.where(isg, jnp.exp(lg - mg), 0.0), axis=1, keepdims=True)
    p_g = 1.0 / sg
    gsel = jnp.min(jnp.where(isg & (lg == mg), lanef, big), axis=1, keepdims=True)
    lo = N_GROUPS + gsel * EXPERTS_PER_GROUP
    ise = (lanef >= lo) & (lanef < lo + EXPERTS_PER_GROUP)
    le = jnp.where(ise, logits, NEG)
    t1 = jnp.max(le, axis=1, keepdims=True)
    i1 = jnp.min(jnp.where(ise & (le == t1), lanef, big), axis=1, keepdims=True)
    ise2 = ise & (lanef != i1)
    le2 = jnp.where(ise2, logits, NEG)
    t2 = jnp.max(le2, axis=1, keepdims=True)
    i2 = jnp.min(jnp.where(ise2 & (le2 == t2), lanef, big), axis=1, keepdims=True)
    d = jnp.exp(t2 - t1)
    w1 = p_g / (1.0 + d)
    w2 = p_g * d / (1.0 + d)
    e1 = i1 - N_GROUPS
    e2 = i2 - N_GROUPS
    oh1 = lanef == e1
    oh2 = lanef == e2
    both = jnp.where(oh1 | oh2, 1.0, 0.0)
    row = lax.broadcasted_iota(jnp.int32, (TM, TM), 0)
    col = lax.broadcasted_iota(jnp.int32, (TM, TM), 1)
    before = jnp.dot((row > col).astype(BF16), both.astype(BF16), preferred_element_type=F32)
    before = before + carry_ref[0:1, :]
    rank1 = jnp.sum(jnp.where(oh1, before, 0.0), axis=1, keepdims=True)
    rank2 = jnp.sum(jnp.where(oh2, before, 0.0), axis=1, keepdims=True)
    total = carry_ref[0:1, :] + jnp.sum(both, axis=0, keepdims=True)
    carry_ref[...] = jnp.broadcast_to(total, carry_ref.shape)
    cnt_ref[...] = jnp.broadcast_to(total, cnt_ref.shape)
    vals = (e1, e2, w1, w2, rank1, rank2)
    out = jnp.zeros((TM, LANES), F32)
    for j, v in enumerate(vals):
        out = jnp.where(lane == j, v, out)
    rt_ref[...] = out


def _outproj(x, od, of, gt, sc, sh, g, wo_bf, wr, br):
    tpb = SEQ // TM
    row = pl.BlockSpec((TM, D_MODEL), lambda i: (i, 0))
    half = pl.BlockSpec((TM, DIFF_WIDTH), lambda i: (i, 0))
    per_batch = pl.BlockSpec((None, 1, D_MODEL), lambda i: (i // tpb, 0, 0))
    const = lambda shape: pl.BlockSpec(shape, lambda i: (0,) * len(shape))
    return pl.pallas_call(
        _outproj_kernel,
        grid=(N_TOK // TM,),
        in_specs=[row, half, half, per_batch, per_batch, per_batch, const((1, D_MODEL)),
                  const((D_MODEL, D_MODEL)), const((D_MODEL, 2 * LANES)), const((1, LANES))],
        out_specs=[row, pl.BlockSpec((N_PLANES, TM, LANES), lambda i: (0, i, 0)),
                   pl.BlockSpec((TM, LANES), lambda i: (i, 0)), const((8, LANES))],
        out_shape=[jax.ShapeDtypeStruct((N_TOK, D_MODEL), F32),
                   jax.ShapeDtypeStruct((N_PLANES, N_TOK, LANES), jnp.int32),
                   jax.ShapeDtypeStruct((N_TOK, LANES), F32),
                   jax.ShapeDtypeStruct((8, LANES), F32)],
        scratch_shapes=[pltpu.VMEM((8, LANES), F32)],
        compiler_params=_params("arbitrary"),
        name="outproj_router",
    )(x, od, of, gt, sc, sh, g.reshape(1, D_MODEL), wo_bf, wr, br)


def _expert_kernel(be_ref, cnt_ref, xs_ref, wg_ref, wu_ref, wd_ref, ys_ref, wg_sc, wu_sc, wd_sc):
    i = pl.program_id(0)
    cnt = cnt_ref[i]

    @pl.when((i == 0) | (be_ref[i] != be_ref[jnp.maximum(i - 1, 0)]))
    def _():
        wg_sc[...] = wg_ref[...].astype(BF16)
        wu_sc[...] = wu_ref[...].astype(BF16)
        wd_sc[...] = wd_ref[...].astype(BF16)

    @pl.when(cnt > 0)
    def _():
        live = lax.broadcasted_iota(jnp.int32, (MOE_BLOCK, LANES), 0) < cnt
        xb = _unpack_planes([jnp.where(live, xs_ref[p], 0) for p in range(N_PLANES)]).astype(BF16)
        a = jnp.dot(xb, wg_sc[...], preferred_element_type=F32)
        u = jnp.dot(xb, wu_sc[...], preferred_element_type=F32)
        hid = (a / (1.0 + jnp.exp(-a)) * u).astype(BF16)
        _pack_planes(jnp.dot(hid, wd_sc[...], preferred_element_type=F32), ys_ref)

    @pl.when(cnt == 0)
    def _():
        ys_ref[...] = jnp.zeros_like(ys_ref)


def _experts(layer, block_expert, block_count, xs, wg, wu, wd):
    planes = pl.BlockSpec((N_PLANES, MOE_BLOCK, LANES), lambda i, be, bc: (0, i, 0))
    w_in = pl.BlockSpec((None, None, D_MODEL, D_EXPERT), lambda i, be, bc: (layer, be[i], 0, 0))
    w_out = pl.BlockSpec((None, None, D_EXPERT, D_MODEL), lambda i, be, bc: (layer, be[i], 0, 0))
    grid_spec = pltpu.PrefetchScalarGridSpec(
        num_scalar_prefetch=2,
        grid=(MOE_NBLOCKS,),
        in_specs=[planes, w_in, w_in, w_out],
        out_specs=planes,
        scratch_shapes=[pltpu.VMEM((D_MODEL, D_EXPERT), BF16), pltpu.VMEM((D_MODEL, D_EXPERT), BF16),
                        pltpu.VMEM((D_EXPERT, D_MODEL), BF16)],
    )
    return pl.pallas_call(
        _expert_kernel,
        grid_spec=grid_spec,
        out_shape=jax.ShapeDtypeStruct((N_PLANES, PLANE_ROWS, LANES), jnp.int32),
        compiler_params=_params("arbitrary"),
        name="expert_mlp",
    )(block_expert, block_count, xs, wg, wu, wd)


def _slots(route, counts):
    counts = counts[0, :N_EXPERTS].astype(jnp.int32)
    padded = ((counts + MOE_BLOCK - 1) // MOE_BLOCK) * MOE_BLOCK
    pend = jnp.cumsum(padded)
    pstart = pend - padded
    bstart = jnp.arange(MOE_NBLOCKS, dtype=jnp.int32) * MOE_BLOCK
    block_expert = jnp.minimum(jnp.sum(bstart[:, None] >= pend[None, :], axis=1), N_EXPERTS - 1)
    block_expert = block_expert.astype(jnp.int32)
    mine = block_expert[:, None] == jnp.arange(N_EXPERTS, dtype=jnp.int32)[None, :]
    left = jnp.sum(jnp.where(mine, counts + pstart, 0), axis=1) - bstart
    block_count = jnp.clip(left, 0, MOE_BLOCK).astype(jnp.int32)
    base = jnp.pad(pstart.astype(F32), (0, LANES - N_EXPERTS)).reshape(1, LANES)
    return _slot_rows(route, base), block_expert, block_count


def _slot_rows_kernel(rt_ref, base_ref, o_ref):
    rt = rt_ref[...]
    lanef = _lane_iota(rt.shape).astype(F32)
    base = base_ref[...]
    dest = jnp.zeros(rt.shape, F32)
    for k in range(TOP_K):
        b = jnp.sum(jnp.where(lanef == rt[:, k:k + 1], base, 0.0), axis=1, keepdims=True)
        dest = jnp.where(lanef == k, b + rt[:, 2 * TOP_K + k:2 * TOP_K + k + 1], dest)
    dest_t = dest.T.astype(jnp.int32)
    for k in range(TOP_K):
        for p in range(N_PLANES):
            for c in range(rt.shape[0] // LANES):
                o_ref[k * N_PLANES + p, c:c + 1, :] = dest_t[k:k + 1, c * LANES:(c + 1) * LANES] + p * PLANE_ROWS


def _slot_rows(route, base):
    tm = 8 * LANES
    return pl.pallas_call(
        _slot_rows_kernel,
        grid=(N_TOK // tm,),
        in_specs=[pl.BlockSpec((tm, LANES), lambda i: (i, 0)), pl.BlockSpec((1, LANES), lambda i: (0, 0))],
        out_specs=pl.BlockSpec((TOP_K * N_PLANES, tm // LANES, LANES), lambda i: (0, i, 0)),
        out_shape=jax.ShapeDtypeStruct((TOP_K * N_PLANES, N_TOK // LANES, LANES), jnp.int32),
        compiler_params=_params("arbitrary"),
        name="slot_rows",
    )(route, base)


def _sc_workers():
    info = plsc.get_sparse_core_info()
    return info.num_cores, info.num_cores * info.num_subcores


def _sc_scatter2(src, idx, out_rows):
    n_win = src.shape[0] // SC_WINDOW
    nc, nw = _sc_workers()
    steps = n_win // nw
    mesh = plsc.VectorSubcoreMesh(core_axis_name="c", subcore_axis_name="s")

    @functools.partial(
        pl.kernel, mesh=mesh,
        out_type=jax.ShapeDtypeStruct((out_rows, LANES), src.dtype),
        scratch_types=[pltpu.VMEM((2 * steps, SC_WINDOW), jnp.int32),
                       pltpu.VMEM((SC_INFLIGHT, SC_WINDOW, LANES), src.dtype),
                       pltpu.SemaphoreType.DMA((SC_INFLIGHT,)), pltpu.SemaphoreType.DMA((SC_INFLIGHT,))],
        name="sc_dispatch_scatter",
    )
    def k(src_hbm, idx_hbm, out_hbm, idx_v, rows_v, lsem, wsem):
        first = (lax.axis_index("s") * nc + lax.axis_index("c")) * steps
        pltpu.sync_copy(idx_hbm.at[pl.ds(first, steps)], idx_v.at[pl.ds(0, steps)])
        pltpu.sync_copy(idx_hbm.at[pl.ds(n_win + first, steps)], idx_v.at[pl.ds(steps, steps)])

        @pl.loop(0, steps, step=SC_INFLIGHT)
        def _(j):
            loads = [pltpu.async_copy(src_hbm.at[pl.ds((first + j + b) * SC_WINDOW, SC_WINDOW)],
                                      rows_v.at[b], lsem.at[b]) for b in range(SC_INFLIGHT)]
            writes = []
            for b in range(SC_INFLIGHT):
                loads[b].wait()
                for half in range(TOP_K):
                    dst = out_hbm.at[idx_v.at[half * steps + j + b]]
                    writes.append(pltpu.async_copy(rows_v.at[b], dst, wsem.at[b]))
            for w in writes:
                w.wait()

    return k(src, idx)


def _sc_gather(table, idx):
    n_out = idx.shape[0] * SC_WINDOW
    nc, nw = _sc_workers()
    steps = n_out // nw // SC_WINDOW
    mesh = plsc.VectorSubcoreMesh(core_axis_name="c", subcore_axis_name="s")

    @functools.partial(
        pl.kernel, mesh=mesh,
        out_type=jax.ShapeDtypeStruct((n_out, LANES), table.dtype),
        scratch_types=[pltpu.VMEM((steps, SC_WINDOW), jnp.int32),
                       pltpu.VMEM((SC_INFLIGHT, SC_WINDOW, LANES), table.dtype),
                       pltpu.SemaphoreType.DMA((SC_INFLIGHT,)), pltpu.SemaphoreType.DMA((SC_INFLIGHT,))],
        name="sc_combine_gather",
    )
    def k(table_hbm, idx_hbm, out_hbm, idx_v, rows_v, gsem, wsem):
        first = (lax.axis_index("s") * nc + lax.axis_index("c")) * steps
        pltpu.sync_copy(idx_hbm.at[pl.ds(first, steps)], idx_v)

        @pl.loop(0, steps, step=SC_INFLIGHT)
        def _(j):
            gathers = [pltpu.async_copy(table_hbm.at[idx_v.at[j + b]], rows_v.at[b], gsem.at[b])
                       for b in range(SC_INFLIGHT)]
            writes = []
            for b in range(SC_INFLIGHT):
                gathers[b].wait()
                dst = out_hbm.at[pl.ds((first + j + b) * SC_WINDOW, SC_WINDOW)]
                writes.append(pltpu.async_copy(rows_v.at[b], dst, wsem.at[b]))
            for w in writes:
                w.wait()

    return k(table, idx)


def _final_kernel(x_ref, rt_ref, y_ref, gt_ref, g_ref, o_ref):
    x = x_ref[...] + gt_ref[...] * _combine(rt_ref, y_ref)
    ms = jnp.mean(x * x, axis=-1, keepdims=True)
    o_ref[...] = x * lax.rsqrt(ms + EPS) * g_ref[...]


def _final(x, moe, gate, g):
    tpb = SEQ // TM
    row = pl.BlockSpec((TM, D_MODEL), lambda i: (i, 0))
    return pl.pallas_call(
        _final_kernel,
        grid=(N_TOK // TM,),
        in_specs=[row, pl.BlockSpec((TM, LANES), lambda i: (i, 0)),
                  pl.BlockSpec((TOP_K, N_PLANES, TM, LANES), lambda i: (0, 0, i, 0)),
                  pl.BlockSpec((None, 1, D_MODEL), lambda i: (i // tpb, 0, 0)),
                  pl.BlockSpec((1, D_MODEL), lambda i: (0, 0))],
        out_specs=row,
        out_shape=jax.ShapeDtypeStruct((N_TOK, D_MODEL), F32),
        compiler_params=_params("arbitrary"),
        name="final_norm",
    )(x, moe[0], moe[1], gate, g.reshape(1, D_MODEL))


def kernel(x, c, positions, w_ada, b_ada, g_mix, w_in, b_forget, lambda_q1, lambda_k1, lambda_q2,
           lambda_k2, g_subln, g_fox_out, w_out, g_ffn, w_router_group, b_router_group,
           w_router_expert, b_router_expert, w_expert_gate, w_expert_up, w_expert_down, g_final):
    mod = _modulation(c, w_ada, b_ada)
    mod = mod.reshape(DEPTH, BATCH, 6, 1, D_MODEL)
    tables = _rope_tables(positions)
    pq = _forget_placement()
    xf = x.reshape(N_TOK, D_MODEL)
    moe = None
    gate = None
    for l in range(DEPTH):
        sh1, sc1, gt1, sh2, sc2, gt2 = (mod[l, :, j] for j in range(6))
        w_bf = jnp.pad(w_in[l], ((0, 0), (0, IN_COLS_PAD - IN_COLS))).astype(BF16)
        bfp = jnp.pad(b_forget[l], (0, LANES - N_FOX_HEADS)).reshape(1, LANES)
        xf, (dq, dk, dv, fq, fk, fv) = _inproj(xf, moe, gate, sc1, sh1, g_mix[l], w_bf, bfp, tables, pq)

        lambda_init = 0.8 - 0.6 * float(np.exp(-0.3 * l))
        lamv = jnp.zeros((8, LANES), F32).at[0:4, 0:HEAD_DIM].set(
            jnp.stack([lambda_q1[l], lambda_k1[l], lambda_q2[l], lambda_k2[l]]))
        g_d = g_subln[l].reshape(1, LANES)
        g_f = jnp.concatenate([g_fox_out[l], g_fox_out[l]]).reshape(1, LANES)
        od = _attention(True, lambda_init, dq, dk, dv, g_d, lamv)
        of = _attention(False, lambda_init, fq, fk, fv, g_f, lamv)

        wr32 = jnp.pad(jnp.concatenate([w_router_group[l], w_router_expert[l]], axis=1),
                       ((0, 0), (0, LANES - N_GROUPS - N_EXPERTS)))
        wr_hi = wr32.astype(BF16)
        wr_lo = (wr32 - wr_hi.astype(F32)).astype(BF16)
        wr = jnp.concatenate([wr_hi, wr_lo], axis=1)
        br = jnp.pad(jnp.concatenate([b_router_group[l], b_router_expert[l]]),
                     (0, LANES - N_GROUPS - N_EXPERTS)).reshape(1, LANES)
        xf, h2, route, counts = _outproj(xf, od, of, gt1, sc2, sh2, g_ffn[l], w_out[l].astype(BF16),
                                         wr, br)

        rows, block_expert, block_count = _slots(route, counts)
        rows = rows.reshape(TOP_K * N_PLANES * N_TOK // SC_WINDOW, SC_WINDOW)
        xs = _sc_scatter2(h2.reshape(N_PLANES * N_TOK, LANES), rows, N_PLANES * PLANE_ROWS)
        ys = _experts(l, block_expert, block_count, xs.reshape(N_PLANES, PLANE_ROWS, LANES),
                      w_expert_gate, w_expert_up, w_expert_down)
        y2 = _sc_gather(ys.reshape(N_PLANES * PLANE_ROWS, LANES), rows)
        moe = (route, y2.reshape(TOP_K, N_PLANES, N_TOK, LANES))
        gate = gt2
    out = _final(xf, moe, gate, g_final)
    return out.reshape(BATCH, SEQ, D_MODEL)
```

```python
import functools

import numpy as np
import jax
import jax.numpy as jnp
from jax import lax
from jax.experimental import pallas as pl
from jax.experimental.pallas import tpu as pltpu
from jax.experimental.pallas import tpu_sc as plsc

D_MODEL = 1024
BATCH = 4
SEQ = 4096
DEPTH = 4
N_TOK = BATCH * SEQ

CHUNK = 64
HEAD_DIM = 64
N_DIFF_HEADS = 4
N_FOX_HEADS = 8
DIFF_WIDTH = 512
FOX_WIDTH = 512
IN_COLS = 3 * DIFF_WIDTH + 3 * FOX_WIDTH + N_FOX_HEADS
ROT_DIM = 16
ROPE_THETA = 500000.0
N_GROUPS = 4
EXPERTS_PER_GROUP = 8
N_EXPERTS = 32
TOP_K = 2
D_EXPERT = 512
EPS = 1e-6

LANES = 128
IN_COLS_PAD = 3200
FF_COL = 3 * DIFF_WIDTH + 3 * FOX_WIDTH
QK_WIDTH = 8 * LANES
TM = 512
OUTPROJ_TILES = 2
TQ = 512
ATTN_TQ = 256
N_LATE_CHAINS = 1
ONES_ROWS = 16
ATTN_ROWS = 64
LOG2E = 1.4426950408889634
MOE_BLOCK = 512
MOE_ROWS = N_TOK * TOP_K + N_EXPERTS * MOE_BLOCK
MOE_NBLOCKS = MOE_ROWS // MOE_BLOCK
PLANE_ROWS = MOE_ROWS
N_PLANES = D_MODEL // 2 // LANES
SC_WINDOW = 128
SC_INFLIGHT = 4
NEG = -1e30
VMEM_LIMIT = 56 * 1024 * 1024

F32 = jnp.float32
BF16 = jnp.bfloat16


def _bf16_round(x):
    return x.astype(BF16).astype(F32)


def _lane_iota(shape):
    return lax.broadcasted_iota(jnp.int32, shape, 1)


def _params(*sem):
    return pltpu.CompilerParams(dimension_semantics=sem, vmem_limit_bytes=VMEM_LIMIT)


def _pack_planes(y, o_ref, rows=slice(None)):
    bits = lax.bitcast_convert_type(_bf16_round(y), jnp.uint32)
    half = D_MODEL // 2
    word = bits[:, half:] | lax.shift_right_logical(bits[:, :half], jnp.uint32(16))
    word = lax.bitcast_convert_type(word, jnp.int32)
    for p in range(N_PLANES):
        o_ref[p, rows, :] = word[:, p * LANES:(p + 1) * LANES]


def _unpack_planes(planes):
    lo, hi = [], []
    for w in planes:
        u = lax.bitcast_convert_type(w, jnp.uint32)
        lo.append(lax.bitcast_convert_type(lax.shift_left(u, jnp.uint32(16)), F32))
        hi.append(lax.bitcast_convert_type(u & jnp.uint32(0xFFFF0000), F32))
    return jnp.concatenate(lo + hi, axis=1)


def _combine(route_ref, y_ref):
    rt = route_ref[...]
    y0 = _unpack_planes([y_ref[0, p] for p in range(N_PLANES)])
    y1 = _unpack_planes([y_ref[1, p] for p in range(N_PLANES)])
    return rt[:, 2:3] * y0 + rt[:, 3:4] * y1


def _mod_kernel(c_ref, w_ref, b_ref, o_ref):
    c = c_ref[...]
    cond = c / (1.0 + jnp.exp(-c))
    ch = cond.astype(BF16)
    cl = (cond - ch.astype(F32)).astype(BF16)
    w = w_ref[...]
    wh = w.astype(BF16)
    wl = (w - wh.astype(F32)).astype(BF16)
    acc = jnp.dot(ch, wh, preferred_element_type=F32)
    acc += jnp.dot(cl, wh, preferred_element_type=F32)
    acc += jnp.dot(ch, wl, preferred_element_type=F32)
    o_ref[...] = acc + b_ref[...]


def _modulation(c, w_ada, b_ada):
    rows = 16
    tn = 1536
    c_pad = jnp.zeros((rows, D_MODEL), F32).at[:BATCH].set(c)
    out = pl.pallas_call(
        _mod_kernel,
        grid=(DEPTH, 6 * D_MODEL // tn),
        in_specs=[
            pl.BlockSpec((rows, D_MODEL), lambda l, n: (0, 0)),
            pl.BlockSpec((None, D_MODEL, tn), lambda l, n: (l, 0, n)),
            pl.BlockSpec((None, 1, tn), lambda l, n: (l, 0, n)),
        ],
        out_specs=pl.BlockSpec((None, rows, tn), lambda l, n: (l, 0, n)),
        out_shape=jax.ShapeDtypeStruct((DEPTH, rows, 6 * D_MODEL), F32),
        compiler_params=_params("arbitrary", "arbitrary"),
        name="adaln_mod",
    )(c_pad, w_ada, b_ada.reshape(DEPTH, 1, 6 * D_MODEL))
    return out[:, :BATCH]


def _rope_kernel(pos_ref, inv_ref, c_ref, sa_ref, sb_ref):
    ang = pos_ref[...].astype(F32) * inv_ref[...]
    j = _lane_iota(ang.shape) % HEAD_DIM
    cosv = jnp.cos(ang)
    sinv = jnp.sin(ang)
    half = ROT_DIM // 2
    c_ref[...] = jnp.where(j < ROT_DIM, cosv, 1.0)
    sa_ref[...] = jnp.where(j < half, -sinv, 0.0)
    sb_ref[...] = jnp.where((j >= half) & (j < ROT_DIM), sinv, 0.0)


def _rope_tables(positions):
    half = ROT_DIM // 2
    inv = ROPE_THETA ** (-jnp.arange(0, ROT_DIM, 2, dtype=F32) / ROT_DIM)
    lane = np.arange(LANES)
    inv_lane = inv[(lane % HEAD_DIM) % half].reshape(1, LANES)
    spec = pl.BlockSpec((TM, LANES), lambda i: (i, 0))
    shape = jax.ShapeDtypeStruct((N_TOK, LANES), F32)
    return pl.pallas_call(
        _rope_kernel,
        grid=(N_TOK // TM,),
        in_specs=[pl.BlockSpec((TM, 1), lambda i: (i, 0)),
                  pl.BlockSpec((1, LANES), lambda i: (0, 0))],
        out_specs=[spec, spec, spec],
        out_shape=[shape, shape, shape],
        compiler_params=_params("arbitrary"),
        name="rope_tables",
    )(positions.reshape(N_TOK, 1), inv_lane)


def _rms_mod(x, g, sc, sh):
    ms = jnp.mean(x * x, axis=-1, keepdims=True)
    return (x * lax.rsqrt(ms + EPS) * g) * (1.0 + sc) + sh


def _inproj_kernel(fuse, *refs):
    if fuse:
        (x_ref, rt_ref, y_ref, gt_ref, sc_ref, sh_ref, g_ref, w_ref, bf_ref, c_ref, sa_ref, sb_ref,
         pq_ref, xo_ref, dq_ref, dk_ref, dv_ref, fq_ref, fk_ref, fv_ref, carry_ref) = refs
        x = x_ref[...] + gt_ref[...] * _combine(rt_ref, y_ref)
        xo_ref[...] = x
    else:
        (x_ref, sc_ref, sh_ref, g_ref, w_ref, bf_ref, c_ref, sa_ref, sb_ref,
         pq_ref, dq_ref, dk_ref, dv_ref, fq_ref, fk_ref, fv_ref, carry_ref) = refs
        x = x_ref[...]
    hb = _rms_mod(x, g_ref[...], sc_ref[...], sh_ref[...]).astype(BF16)

    @pl.when(pl.program_id(0) % (SEQ // TM) == 0)
    def _():
        carry_ref[...] = jnp.zeros_like(carry_ref)

    lane = _lane_iota((TM, LANES))
    nh = N_FOX_HEADS

    def pack3(a):
        hi = _bf16_round(a)
        r1 = a - hi
        mid = _bf16_round(r1)
        lo = _bf16_round(r1 - mid)
        return jnp.where(lane < nh, hi,
                         jnp.where(lane < 2 * nh, pltpu.roll(mid, nh, 1),
                                   jnp.where(lane < 3 * nh, pltpu.roll(lo, 2 * nh, 1), 0.0)))

    z = jnp.dot(hb, w_ref[:, FF_COL:FF_COL + LANES], preferred_element_type=F32) + bf_ref[...]

    low = lane < HEAD_DIM
    rc, rsa, rsb = c_ref[...], sa_ref[...], sb_ref[...]
    scale = HEAD_DIM ** -0.5 * LOG2E

    def split_store(chunk, o_ref, m, extra_a=None, extra_b=None):
        a = jnp.where(low, chunk, 0.0)
        b = jnp.where(low, pltpu.roll(chunk, HEAD_DIM, 1), 0.0)
        if extra_a is not None:
            a = a + extra_a
            b = b + extra_b
        o_ref[:, (2 * m) * LANES:(2 * m + 1) * LANES] = a.astype(BF16)
        o_ref[:, (2 * m + 1) * LANES:(2 * m + 2) * LANES] = b.astype(BF16)

    def rope(xc):
        return xc * rc + pltpu.roll(xc, LANES - ROT_DIM // 2, 1) * rsa + pltpu.roll(xc, ROT_DIM // 2, 1) * rsb

    pdq = jnp.dot(hb, w_ref[:, 0:DIFF_WIDTH], preferred_element_type=F32)
    for m in range(N_DIFF_HEADS):
        split_store(rope(pdq[:, m * LANES:(m + 1) * LANES]) * scale, dq_ref, m)
    pdk = jnp.dot(hb, w_ref[:, DIFF_WIDTH:2 * DIFF_WIDTH], preferred_element_type=F32)
    for m in range(N_DIFF_HEADS):
        split_store(rope(pdk[:, m * LANES:(m + 1) * LANES]), dk_ref, m)
    def store_values_t(pv, o_ref, width):
        ones = jnp.ones((ONES_ROWS, TM), BF16)
        for m in range(4):
            vt = pv[:, m * LANES:(m + 1) * LANES].T.astype(BF16)
            for i in range(LANES // width):
                o_ref[m * (LANES // width) + i, 0:width, :] = vt[i * width:(i + 1) * width]
                o_ref[m * (LANES // width) + i, width:width + ONES_ROWS, :] = ones

    store_values_t(jnp.dot(hb, w_ref[:, 2 * DIFF_WIDTH:3 * DIFF_WIDTH], preferred_element_type=F32),
                   dv_ref, 2 * HEAD_DIM)
    o = 3 * DIFF_WIDTH
    store_values_t(jnp.dot(hb, w_ref[:, o + 2 * FOX_WIDTH:o + 3 * FOX_WIDTH],
                           preferred_element_type=F32), fv_ref, HEAD_DIM)

    logf =jnp.minimum(z, 0.0) - jnp.log(1.0 + jnp.exp(-jnp.abs(z)))
    logf = jnp.where(lane < nh, logf, 0.0)
    row = lax.broadcasted_iota(jnp.int32, (TM, TM), 0)
    col = lax.broadcasted_iota(jnp.int32, (TM, TM), 1)
    tri = (row >= col).astype(BF16)
    r = jnp.dot(tri, pack3(logf).astype(BF16), preferred_element_type=F32)
    cs = r + pltpu.roll(r, LANES - nh, 1) + pltpu.roll(r, LANES - 2 * nh, 1)
    cf = jnp.where(lane < nh, cs + carry_ref[0:1, :], 0.0)
    carry_ref[...] = jnp.broadcast_to(cf[TM - 1:TM, :], carry_ref.shape)

    t3 = jnp.where(lane == 3 * nh, 1.0, pack3(cf * LOG2E)).astype(BF16)
    aug = jnp.dot(t3, pq_ref[...], preferred_element_type=F32)

    pfq =jnp.dot(hb, w_ref[:, o:o + FOX_WIDTH], preferred_element_type=F32)
    for m in range(N_FOX_HEADS // 2):
        split_store(pfq[:, m * LANES:(m + 1) * LANES] * scale, fq_ref, m,
                    aug[:, (2 * m) * LANES:(2 * m + 1) * LANES],
                    aug[:, (2 * m + 1) * LANES:(2 * m + 2) * LANES])
    pfk = jnp.dot(hb, w_ref[:, o + FOX_WIDTH:o + 2 * FOX_WIDTH], preferred_element_type=F32)
    for m in range(N_FOX_HEADS // 2):
        split_store(pfk[:, m * LANES:(m + 1) * LANES], fk_ref, m,
                    aug[:, QK_WIDTH + (2 * m) * LANES:QK_WIDTH + (2 * m + 1) * LANES],
                    aug[:, QK_WIDTH + (2 * m + 1) * LANES:QK_WIDTH + (2 * m + 2) * LANES])


def _forget_placement():
    nh = N_FOX_HEADS
    p = np.zeros((LANES, 2 * QK_WIDTH), np.float32)
    for h in range(nh):
        base_q = h * LANES + HEAD_DIM
        base_k = QK_WIDTH + h * LANES + HEAD_DIM
        for part in range(3):
            p[part * nh + h, base_q + part] = 1.0
            p[3 * nh, base_q + 3 + part] = 1.0
            p[3 * nh, base_k + part] = 1.0
            p[part * nh + h, base_k + 3 + part] = -1.0
    return jnp.asarray(p, BF16)


def _inproj(x, moe, gate, sc, sh, g, w_bf, b_forget, tables, pq):
    fuse = moe is not None
    tpb = SEQ // TM
    row = pl.BlockSpec((TM, D_MODEL), lambda i: (i, 0))
    per_batch = pl.BlockSpec((None, 1, D_MODEL), lambda i: (i // tpb, 0, 0))
    const = lambda shape: pl.BlockSpec(shape, lambda i: (0,) * len(shape))
    tab = pl.BlockSpec((TM, LANES), lambda i: (i, 0))
    in_specs = [row]
    args = [x]
    if fuse:
        in_specs += [tab, pl.BlockSpec((TOP_K, N_PLANES, TM, LANES), lambda i: (0, 0, i, 0)), per_batch]
        args += [moe[0], moe[1], gate]
    in_specs += [per_batch, per_batch, const((1, D_MODEL)), const((D_MODEL, IN_COLS_PAD)),
                 const((1, LANES)), tab, tab, tab, const((LANES, 2 * QK_WIDTH))]
    args += [sc, sh, g.reshape(1, D_MODEL), w_bf, b_forget, *tables, pq]
    wide = pl.BlockSpec((TM, QK_WIDTH), lambda i: (i, 0))
    def vspec(heads, width):
        rows = width + ONES_ROWS
        return (pl.BlockSpec((None, heads, None, rows, TM), lambda i: (i // tpb, 0, i % tpb, 0, 0)),
                jax.ShapeDtypeStruct((BATCH, heads, tpb, rows, TM), BF16))

    wide_s = jax.ShapeDtypeStruct((N_TOK, QK_WIDTH), BF16)
    dv_spec, dv_s = vspec(N_DIFF_HEADS, 2 * HEAD_DIM)
    fv_spec, fv_s = vspec(N_FOX_HEADS, HEAD_DIM)
    out_specs = [wide, wide, dv_spec, wide, wide, fv_spec]
    out_shape = [wide_s, wide_s, dv_s, wide_s, wide_s, fv_s]
    if fuse:
        out_specs = [row] + out_specs
        out_shape = [jax.ShapeDtypeStruct((N_TOK, D_MODEL), F32)] + out_shape
    outs = pl.pallas_call(
        functools.partial(_inproj_kernel, fuse),
        grid=(N_TOK // TM,),
        in_specs=in_specs,
        out_specs=out_specs,
        out_shape=out_shape,
        scratch_shapes=[pltpu.VMEM((8, LANES), F32)],
        compiler_params=_params("arbitrary"),
        name="norm_inproj",
    )(*args)
    if fuse:
        return outs[0], outs[1:]
    return x, outs


def _attn_kernel(diff, lambda_init, qa_ref, qb_ref, ka_ref, kb_ref, v_ref, g_ref, lam_ref, o_ref,
                 *scratch):
    nq = SEQ // TQ
    n_half = TQ // ATTN_TQ
    feat = 2 * HEAD_DIM if diff else HEAD_DIM
    chains = []
    for mi, (q_ref, k_ref) in enumerate(((qa_ref, ka_ref), (qb_ref, kb_ref))):
        for h in range(n_half):
            c = mi * n_half + h
            qt_sc, s_sc, p_sc, m_sc, a_sc, acc_sc = scratch[c::2 * n_half]
            vh = 0 if diff else mi
            chains.append((h, k_ref, qt_sc, s_sc, p_sc, m_sc, a_sc, acc_sc, q_ref, vh))
    order = [chains[mi * n_half + h] for h in range(n_half) for mi in range(2)]
    early, late = order[:-N_LATE_CHAINS], order[-N_LATE_CHAINS:]

    def load_queries(qi):
        for mi in range(2):
            q_ref = chains[mi * n_half][8]
            qt = q_ref[pl.ds(pl.multiple_of(qi * TQ, TQ), TQ), :].astype(F32).T.astype(BF16)
            for h in range(n_half):
                chains[mi * n_half + h][2][...] = qt[:, h * ATTN_TQ:(h + 1) * ATTN_TQ]

    def reset_state():
        for chain in chains:
            m_sc, _, acc_sc = chain[5:8]
            m_sc[...] = jnp.full(m_sc.shape, NEG, F32)
            acc_sc[...] = jnp.zeros(acc_sc.shape, F32)

    def n_keys(chain, masked):
        return (chain[0] + 1) * ATTN_TQ if masked else TQ

    def scores(chain, j, masked):
        h, k_ref, qt_sc, s_sc = chain[:4]
        nk = n_keys(chain, masked)
        off = pl.multiple_of(j * TQ, TQ)
        s = jnp.dot(k_ref[pl.ds(off, nk), :], qt_sc[...], preferred_element_type=F32)
        if masked:
            kk = lax.broadcasted_iota(jnp.int32, (nk, ATTN_TQ), 0)
            qq = h * ATTN_TQ + lax.broadcasted_iota(jnp.int32, (nk, ATTN_TQ), 1)
            s = jnp.where((kk // CHUNK <= qq // CHUNK) if diff else (kk <= qq), s, NEG)
        s_sc[0:nk, :] = s

    def softmax(chain, masked):
        s_sc, p_sc, m_sc, a_sc = chain[3:7]
        nk = n_keys(chain, masked)
        m_all = m_sc[...]
        m_parts = []
        for c0 in range(0, ATTN_TQ, LANES):
            cols = slice(c0, c0 + LANES)
            pm = s_sc[0:ATTN_ROWS, cols]
            for r0 in range(ATTN_ROWS, nk, ATTN_ROWS):
                pm = jnp.maximum(pm, s_sc[r0:r0 + ATTN_ROWS, cols])
            m_new = jnp.maximum(m_all[:, cols], jnp.max(pm, axis=0, keepdims=True))
            for r0 in range(0, nk, ATTN_ROWS):
                p = jnp.exp2(s_sc[r0:r0 + ATTN_ROWS, cols] - m_new)
                p_sc[r0:r0 + ATTN_ROWS, cols] = p.astype(BF16)
            m_parts.append(m_new)
        m_new = jnp.concatenate(m_parts, axis=1)
        a_sc[...] = jnp.exp2(m_all - m_new)
        m_sc[...] = m_new

    def values(chain, j, masked=False):
        p_sc, a_sc, acc_sc, vh = chain[4], chain[6], chain[7], chain[9]
        nk = n_keys(chain, masked)
        pv = jnp.dot(v_ref[vh, j, :, 0:nk], p_sc[0:nk, :], preferred_element_type=F32)
        acc_sc[...] = a_sc[...] * acc_sc[...] + pv

    def idle_late():
        for chain in late:
            chain[4][...] = jnp.zeros(chain[4].shape, BF16)
            chain[6][...] = jnp.ones(chain[6].shape, F32)

    def consume(j, cur_masked=False, nxt=None, nxt_masked=False, final=False, before_next=None):
        def open_late(chain):
            scores(chain, j, cur_masked)
            values(chain, jnp.maximum(j - 1, 0))

        open_late(late[0])
        for i, chain in enumerate(early):
            softmax(chain, cur_masked)
            if i == 0:
                for other in late[1:]:
                    open_late(other)
                if before_next is not None:
                    before_next()
            if nxt is not None:
                scores(chain, nxt, nxt_masked)
            values(chain, j, cur_masked)
        for chain in late:
            softmax(chain, cur_masked)
        if final:
            for chain in late:
                values(chain, j, cur_masked)

    def finalize(qi):
        ot = [jnp.concatenate([chains[mi * n_half + h][7][0:feat] / chains[mi * n_half + h][7][feat:feat + 1]
                               for h in range(n_half)], axis=1) for mi in range(2)]
        g = g_ref[...]
        rows = pl.ds(pl.multiple_of(qi * TQ, TQ), TQ)
        if diff:
            lv = lam_ref[...]
            lam = (jnp.exp(jnp.sum(lv[0:1] * lv[1:2], axis=1, keepdims=True))
                   - jnp.exp(jnp.sum(lv[2:3] * lv[3:4], axis=1, keepdims=True)) + lambda_init)
            o = (ot[0] - lam * ot[1]).T
            y = o * lax.rsqrt(jnp.mean(o * o, axis=1, keepdims=True) + EPS) * g
            o_ref[rows, :] = (y * (1.0 - lambda_init)).astype(o_ref.dtype)
        else:
            o = jnp.concatenate(ot, axis=0).T
            low = _lane_iota((TQ, LANES)) < HEAD_DIM
            sq = o * o
            msa = jnp.sum(jnp.where(low, sq, 0.0), axis=1, keepdims=True) / HEAD_DIM
            msb = jnp.sum(jnp.where(low, 0.0, sq), axis=1, keepdims=True) / HEAD_DIM
            inv = jnp.where(low, lax.rsqrt(msa + EPS), lax.rsqrt(msb + EPS))
            o_ref[rows, :] = (o * inv * g).astype(o_ref.dtype)

    load_queries(0)
    reset_state()
    idle_late()
    for chain in early:
        scores(chain, 0, True)

    @pl.loop(0, nq)
    def _(qi):
        n_plain = jnp.maximum(qi - 1, 0)

        @pl.loop(0, n_plain // 2)
        def _(t):
            consume(2 * t, nxt=2 * t + 1)
            consume(2 * t + 1, nxt=2 * t + 2)

        @pl.when(n_plain % 2 == 1)
        def _():
            consume(qi - 2, nxt=qi - 1)

        def last_blocks(to_next_tile):
            if to_next_tile:
                consume(qi, cur_masked=True, nxt=0, final=True, before_next=lambda: load_queries(qi + 1))
            else:
                consume(qi, cur_masked=True, final=True)
            finalize(qi)

        for to_next_tile in (True, False):
            more = (qi < nq - 1) if to_next_tile else (qi == nq - 1)

            @pl.when(more & (qi > 0))
            def _():
                consume(qi - 1, nxt=qi, nxt_masked=True)
                last_blocks(to_next_tile)

            if to_next_tile:
                @pl.when(qi == 0)
                def _():
                    last_blocks(to_next_tile)

        reset_state()
        idle_late()


def _attention(diff, lambda_init, q, k, v, g, lamv):
    nq = SEQ // TQ
    kspec = lambda par: pl.BlockSpec((SEQ, LANES), lambda b, p: (b, 2 * p + par))
    return pl.pallas_call(
        functools.partial(_attn_kernel, diff, lambda_init),
        grid=(BATCH, 4),
        in_specs=[kspec(0), kspec(1), kspec(0), kspec(1),
                  pl.BlockSpec((None, v.shape[1] // 4, nq, v.shape[3], TQ), lambda b, p: (b, p, 0, 0, 0)),
                  pl.BlockSpec((1, LANES), lambda b, p: (0, 0)),
                  pl.BlockSpec((8, LANES), lambda b, p: (0, 0))],
        out_specs=pl.BlockSpec((SEQ, LANES), lambda b, p: (b, p)),
        out_shape=jax.ShapeDtypeStruct((N_TOK, DIFF_WIDTH), BF16),
        scratch_shapes=[pltpu.VMEM(shape, dt)
                        for shape, dt in (((LANES, ATTN_TQ), BF16), ((TQ, ATTN_TQ), F32),
                                          ((TQ, ATTN_TQ), BF16), ((1, ATTN_TQ), F32),
                                          ((1, ATTN_TQ), F32), ((v.shape[3], ATTN_TQ), F32))
                        for _ in range(2 * TQ // ATTN_TQ)],
        compiler_params=_params("arbitrary", "arbitrary"),
        name="diff_attention" if diff else "fox_attention",
    )(q, q, k, k, v, g, lamv)


def _outproj_kernel(x_ref, od_ref, of_ref, gt_ref, sc_ref, sh_ref, g_ref, wo_ref, wr_ref, br_ref,
                    x1_ref, h2_ref, rt_ref, cnt_ref, carry_ref):
    @pl.when(pl.program_id(0) == 0)
    def _():
        carry_ref[...] = jnp.zeros_like(carry_ref)

    tiles = [slice(t * TM, (t + 1) * TM) for t in range(OUTPROJ_TILES)]
    lane = _lane_iota((TM, LANES))
    lanef = lane.astype(F32)
    big = float(LANES)

    def project(rows):
        mix = jnp.dot(od_ref[rows, :], wo_ref[0:DIFF_WIDTH, :], preferred_element_type=F32)
        return mix + jnp.dot(of_ref[rows, :], wo_ref[DIFF_WIDTH:, :], preferred_element_type=F32)

    def normalise(rows, mix):
        x1 = x_ref[rows, :] + gt_ref[...] * mix
        x1_ref[rows, :] = x1
        h = _rms_mod(x1, g_ref[...], sc_ref[...], sh_ref[...])
        hh = h.astype(BF16)
        _pack_planes(h, h2_ref, rows)
        return hh, (h - hh.astype(F32)).astype(BF16)

    def router_logits(hh, hl):
        r1 = jnp.dot(hh, wr_ref[...], preferred_element_type=F32)
        r2 = jnp.dot(hl, wr_ref[:, 0:LANES], preferred_element_type=F32)
        return r1[:, 0:LANES] + r1[:, LANES:] + r2 + br_ref[...]

    def top_k(logits):
        isg = lane < N_GROUPS
        lg = jnp.where(isg, logits, NEG)
        mg = jnp.max(lg, axis=1, keepdims=True)
        sg = jnp.sum(jnp.where(isg, jnp.exp(lg - mg), 0.0), axis=1, keepdims=True)
        p_g = 1.0 / sg
        gsel = jnp.min(jnp.where(isg & (lg == mg), lanef, big), axis=1, keepdims=True)
        lo = N_GROUPS + gsel * EXPERTS_PER_GROUP
        ise = (lanef >= lo) & (lanef < lo + EXPERTS_PER_GROUP)
        le = jnp.where(ise, logits, NEG)
        t1 = jnp.max(le, axis=1, keepdims=True)
        i1 = jnp.min(jnp.where(ise & (le == t1), lanef, big), axis=1, keepdims=True)
        ise2 = ise & (lanef != i1)
        le2 = jnp.where(ise2, logits, NEG)
        t2 = jnp.max(le2, axis=1, keepdims=True)
        i2 = jnp.min(jnp.where(ise2 & (le2 == t2), lanef, big), axis=1, keepdims=True)
        d = jnp.exp(t2 - t1)
        return i1 - N_GROUPS, i2 - N_GROUPS, p_g / (1.0 + d), p_g * d / (1.0 + d)

    def earlier_in_tile(e1, e2):
        both = jnp.where((lanef == e1) | (lanef == e2), 1.0, 0.0)
        row = lax.broadcasted_iota(jnp.int32, (TM, TM), 0)
        col = lax.broadcasted_iota(jnp.int32, (TM, TM), 1)
        before = jnp.dot((row > col).astype(BF16), both.astype(BF16), preferred_element_type=F32)
        return before, jnp.sum(both, axis=0, keepdims=True)

    mixes = [project(rows) for rows in tiles]
    splits = [normalise(rows, mix) for rows, mix in zip(tiles, mixes)]
    logits = [router_logits(hh, hl) for hh, hl in splits]
    picks = [top_k(lg) for lg in logits]
    befores = [earlier_in_tile(e1, e2) for e1, e2, _, _ in picks]
    counts = carry_ref[0:1, :]
    for rows, (e1, e2, w1, w2), (before, added) in zip(tiles, picks, befores):
        before = before + counts
        rank1 = jnp.sum(jnp.where(lanef == e1, before, 0.0), axis=1, keepdims=True)
        rank2 = jnp.sum(jnp.where(lanef == e2, before, 0.0), axis=1, keepdims=True)
        out = jnp.zeros((TM, LANES), F32)
        for j, v in enumerate((e1, e2, w1, w2, rank1, rank2)):
            out = jnp.where(lane == j, v, out)
        rt_ref[rows, :] = out
        counts = counts + added
    carry_ref[...] = jnp.broadcast_to(counts, carry_ref.shape)
    cnt_ref[...] = jnp.broadcast_to(counts, cnt_ref.shape)


def _outproj(x, od, of, gt, sc, sh, g, wo_bf, wr, br):
    tm = OUTPROJ_TILES * TM
    tpb = SEQ // tm
    row = pl.BlockSpec((tm, D_MODEL), lambda i: (i, 0))
    half = pl.BlockSpec((tm, DIFF_WIDTH), lambda i: (i, 0))
    per_batch = pl.BlockSpec((None, 1, D_MODEL), lambda i: (i // tpb, 0, 0))
    const = lambda shape: pl.BlockSpec(shape, lambda i: (0,) * len(shape))
    return pl.pallas_call(
        _outproj_kernel,
        grid=(N_TOK // tm,),
        in_specs=[row, half, half, per_batch, per_batch, per_batch, const((1, D_MODEL)),
                  const((D_MODEL, D_MODEL)), const((D_MODEL, 2 * LANES)), const((1, LANES))],
        out_specs=[row, pl.BlockSpec((N_PLANES, tm, LANES), lambda i: (0, i, 0)),
                   pl.BlockSpec((tm, LANES), lambda i: (i, 0)), const((8, LANES))],
        out_shape=[jax.ShapeDtypeStruct((N_TOK, D_MODEL), F32),
                   jax.ShapeDtypeStruct((N_PLANES, N_TOK, LANES), jnp.int32),
                   jax.ShapeDtypeStruct((N_TOK, LANES), F32),
                   jax.ShapeDtypeStruct((8, LANES), F32)],
        scratch_shapes=[pltpu.VMEM((8, LANES), F32)],
        compiler_params=_params("arbitrary"),
        name="outproj_router",
    )(x, od, of, gt, sc, sh, g.reshape(1, D_MODEL), wo_bf, wr, br)


def _expert_kernel(be_ref, cnt_ref, xs_ref, wg_ref, wu_ref, wd_ref, ys_ref, wg_sc, wu_sc, wd_sc):
    i = pl.program_id(0)
    cnt = cnt_ref[i]

    @pl.when((i == 0) | (be_ref[i] != be_ref[jnp.maximum(i - 1, 0)]))
    def _():
        wg_sc[...] = wg_ref[...].astype(BF16)
        wu_sc[...] = wu_ref[...].astype(BF16)
        wd_sc[...] = wd_ref[...].astype(BF16)

    @pl.when(cnt > 0)
    def _():
        live = lax.broadcasted_iota(jnp.int32, (MOE_BLOCK, LANES), 0) < cnt
        xb = _unpack_planes([jnp.where(live, xs_ref[p], 0) for p in range(N_PLANES)]).astype(BF16)
        a = jnp.dot(xb, wg_sc[...], preferred_element_type=F32)
        u = jnp.dot(xb, wu_sc[...], preferred_element_type=F32)
        hid = (a / (1.0 + jnp.exp(-a)) * u).astype(BF16)
        _pack_planes(jnp.dot(hid, wd_sc[...], preferred_element_type=F32), ys_ref)

    @pl.when(cnt == 0)
    def _():
        ys_ref[...] = jnp.zeros_like(ys_ref)


def _experts(layer, block_expert, block_count, xs, wg, wu, wd):
    planes = pl.BlockSpec((N_PLANES, MOE_BLOCK, LANES), lambda i, be, bc: (0, i, 0))
    w_in = pl.BlockSpec((None, None, D_MODEL, D_EXPERT), lambda i, be, bc: (layer, be[i], 0, 0))
    w_out = pl.BlockSpec((None, None, D_EXPERT, D_MODEL), lambda i, be, bc: (layer, be[i], 0, 0))
    grid_spec = pltpu.PrefetchScalarGridSpec(
        num_scalar_prefetch=2,
        grid=(MOE_NBLOCKS,),
        in_specs=[planes, w_in, w_in, w_out],
        out_specs=planes,
        scratch_shapes=[pltpu.VMEM((D_MODEL, D_EXPERT), BF16), pltpu.VMEM((D_MODEL, D_EXPERT), BF16),
                        pltpu.VMEM((D_EXPERT, D_MODEL), BF16)],
    )
    return pl.pallas_call(
        _expert_kernel,
        grid_spec=grid_spec,
        out_shape=jax.ShapeDtypeStruct((N_PLANES, PLANE_ROWS, LANES), jnp.int32),
        compiler_params=_params("arbitrary"),
        name="expert_mlp",
    )(block_expert, block_count, xs, wg, wu, wd)


def _slots(route, counts):
    counts = counts[0, :N_EXPERTS].astype(jnp.int32)
    padded = ((counts + MOE_BLOCK - 1) // MOE_BLOCK) * MOE_BLOCK
    pend = jnp.cumsum(padded)
    pstart = pend - padded
    bstart = jnp.arange(MOE_NBLOCKS, dtype=jnp.int32) * MOE_BLOCK
    block_expert = jnp.minimum(jnp.sum(bstart[:, None] >= pend[None, :], axis=1), N_EXPERTS - 1)
    block_expert = block_expert.astype(jnp.int32)
    mine = block_expert[:, None] == jnp.arange(N_EXPERTS, dtype=jnp.int32)[None, :]
    left = jnp.sum(jnp.where(mine, counts + pstart, 0), axis=1) - bstart
    block_count = jnp.clip(left, 0, MOE_BLOCK).astype(jnp.int32)
    base = jnp.pad(pstart.astype(F32), (0, LANES - N_EXPERTS)).reshape(1, LANES)
    return _slot_rows(route, base), block_expert, block_count


def _slot_rows_kernel(rt_ref, base_ref, o_ref):
    rt = rt_ref[...]
    lanef = _lane_iota(rt.shape).astype(F32)
    base = base_ref[...]
    dest = jnp.zeros(rt.shape, F32)
    for k in range(TOP_K):
        b = jnp.sum(jnp.where(lanef == rt[:, k:k + 1], base, 0.0), axis=1, keepdims=True)
        dest = jnp.where(lanef == k, b + rt[:, 2 * TOP_K + k:2 * TOP_K + k + 1], dest)
    dest_t = dest.T.astype(jnp.int32)
    for k in range(TOP_K):
        for p in range(N_PLANES):
            for c in range(rt.shape[0] // LANES):
                o_ref[k * N_PLANES + p, c:c + 1, :] = dest_t[k:k + 1, c * LANES:(c + 1) * LANES] + p * PLANE_ROWS


def _slot_rows(route, base):
    tm = 8 * LANES
    return pl.pallas_call(
        _slot_rows_kernel,
        grid=(N_TOK // tm,),
        in_specs=[pl.BlockSpec((tm, LANES), lambda i: (i, 0)), pl.BlockSpec((1, LANES), lambda i: (0, 0))],
        out_specs=pl.BlockSpec((TOP_K * N_PLANES, tm // LANES, LANES), lambda i: (0, i, 0)),
        out_shape=jax.ShapeDtypeStruct((TOP_K * N_PLANES, N_TOK // LANES, LANES), jnp.int32),
        compiler_params=_params("arbitrary"),
        name="slot_rows",
    )(route, base)


def _sc_workers():
    info = plsc.get_sparse_core_info()
    return info.num_cores, info.num_cores * info.num_subcores


def _sc_scatter2(src, idx, out_rows):
    n_win = src.shape[0] // SC_WINDOW
    nc, nw = _sc_workers()
    steps = n_win // nw
    mesh = plsc.VectorSubcoreMesh(core_axis_name="c", subcore_axis_name="s")

    @functools.partial(
        pl.kernel, mesh=mesh,
        out_type=jax.ShapeDtypeStruct((out_rows, LANES), src.dtype),
        scratch_types=[pltpu.VMEM((2 * steps, SC_WINDOW), jnp.int32),
                       pltpu.VMEM((SC_INFLIGHT, SC_WINDOW, LANES), src.dtype),
                       pltpu.SemaphoreType.DMA((SC_INFLIGHT,)), pltpu.SemaphoreType.DMA((SC_INFLIGHT,))],
        name="sc_dispatch_scatter",
    )
    def k(src_hbm, idx_hbm, out_hbm, idx_v, rows_v, lsem, wsem):
        first = (lax.axis_index("s") * nc + lax.axis_index("c")) * steps
        pltpu.sync_copy(idx_hbm.at[pl.ds(first, steps)], idx_v.at[pl.ds(0, steps)])
        pltpu.sync_copy(idx_hbm.at[pl.ds(n_win + first, steps)], idx_v.at[pl.ds(steps, steps)])

        @pl.loop(0, steps, step=SC_INFLIGHT)
        def _(j):
            loads = [pltpu.async_copy(src_hbm.at[pl.ds((first + j + b) * SC_WINDOW, SC_WINDOW)],
                                      rows_v.at[b], lsem.at[b]) for b in range(SC_INFLIGHT)]
            writes = []
            for b in range(SC_INFLIGHT):
                loads[b].wait()
                for half in range(TOP_K):
                    dst = out_hbm.at[idx_v.at[half * steps + j + b]]
                    writes.append(pltpu.async_copy(rows_v.at[b], dst, wsem.at[b]))
            for w in writes:
                w.wait()

    return k(src, idx)


def _sc_gather(table, idx):
    n_out = idx.shape[0] * SC_WINDOW
    nc, nw = _sc_workers()
    steps = n_out // nw // SC_WINDOW
    mesh = plsc.VectorSubcoreMesh(core_axis_name="c", subcore_axis_name="s")

    @functools.partial(
        pl.kernel, mesh=mesh,
        out_type=jax.ShapeDtypeStruct((n_out, LANES), table.dtype),
        scratch_types=[pltpu.VMEM((steps, SC_WINDOW), jnp.int32),
                       pltpu.VMEM((SC_INFLIGHT, SC_WINDOW, LANES), table.dtype),
                       pltpu.SemaphoreType.DMA((SC_INFLIGHT,)), pltpu.SemaphoreType.DMA((SC_INFLIGHT,))],
        name="sc_combine_gather",
    )
    def k(table_hbm, idx_hbm, out_hbm, idx_v, rows_v, gsem, wsem):
        first = (lax.axis_index("s") * nc + lax.axis_index("c")) * steps
        pltpu.sync_copy(idx_hbm.at[pl.ds(first, steps)], idx_v)

        @pl.loop(0, steps, step=SC_INFLIGHT)
        def _(j):
            gathers = [pltpu.async_copy(table_hbm.at[idx_v.at[j + b]], rows_v.at[b], gsem.at[b])
                       for b in range(SC_INFLIGHT)]
            writes = []
            for b in range(SC_INFLIGHT):
                gathers[b].wait()
                dst = out_hbm.at[pl.ds((first + j + b) * SC_WINDOW, SC_WINDOW)]
                writes.append(pltpu.async_copy(rows_v.at[b], dst, wsem.at[b]))
            for w in writes:
                w.wait()

    return k(table, idx)


def _final_kernel(x_ref, rt_ref, y_ref, gt_ref, g_ref, o_ref):
    x = x_ref[...] + gt_ref[...] * _combine(rt_ref, y_ref)
    ms = jnp.mean(x * x, axis=-1, keepdims=True)
    o_ref[...] = x * lax.rsqrt(ms + EPS) * g_ref[...]


def _final(x, moe, gate, g):
    tpb = SEQ // TM
    row = pl.BlockSpec((TM, D_MODEL), lambda i: (i, 0))
    return pl.pallas_call(
        _final_kernel,
        grid=(N_TOK // TM,),
        in_specs=[row, pl.BlockSpec((TM, LANES), lambda i: (i, 0)),
                  pl.BlockSpec((TOP_K, N_PLANES, TM, LANES), lambda i: (0, 0, i, 0)),
                  pl.BlockSpec((None, 1, D_MODEL), lambda i: (i // tpb, 0, 0)),
                  pl.BlockSpec((1, D_MODEL), lambda i: (0, 0))],
        out_specs=row,
        out_shape=jax.ShapeDtypeStruct((N_TOK, D_MODEL), F32),
        compiler_params=_params("arbitrary"),
        name="final_norm",
    )(x, moe[0], moe[1], gate, g.reshape(1, D_MODEL))


def kernel(x, c, positions, w_ada, b_ada, g_mix, w_in, b_forget, lambda_q1, lambda_k1, lambda_q2,
           lambda_k2, g_subln, g_fox_out, w_out, g_ffn, w_router_group, b_router_group,
           w_router_expert, b_router_expert, w_expert_gate, w_expert_up, w_expert_down, g_final):
    mod = _modulation(c, w_ada, b_ada)
    mod = mod.reshape(DEPTH, BATCH, 6, 1, D_MODEL)
    tables = _rope_tables(positions)
    pq = _forget_placement()
    xf = x.reshape(N_TOK, D_MODEL)
    moe = None
    gate = None
    for l in range(DEPTH):
        sh1, sc1, gt1, sh2, sc2, gt2 = (mod[l, :, j] for j in range(6))
        w_bf = jnp.pad(w_in[l], ((0, 0), (0, IN_COLS_PAD - IN_COLS))).astype(BF16)
        bfp = jnp.pad(b_forget[l], (0, LANES - N_FOX_HEADS)).reshape(1, LANES)
        xf, (dq, dk, dv, fq, fk, fv) = _inproj(xf, moe, gate, sc1, sh1, g_mix[l], w_bf, bfp, tables, pq)

        lambda_init = 0.8 - 0.6 * float(np.exp(-0.3 * l))
        lamv = jnp.zeros((8, LANES), F32).at[0:4, 0:HEAD_DIM].set(
            jnp.stack([lambda_q1[l], lambda_k1[l], lambda_q2[l], lambda_k2[l]]))
        g_d = g_subln[l].reshape(1, LANES)
        g_f = jnp.concatenate([g_fox_out[l], g_fox_out[l]]).reshape(1, LANES)
        od = _attention(True, lambda_init, dq, dk, dv, g_d, lamv)
        of = _attention(False, lambda_init, fq, fk, fv, g_f, lamv)

        wr32 = jnp.pad(jnp.concatenate([w_router_group[l], w_router_expert[l]], axis=1),
                       ((0, 0), (0, LANES - N_GROUPS - N_EXPERTS)))
        wr_hi = wr32.astype(BF16)
        wr_lo = (wr32 - wr_hi.astype(F32)).astype(BF16)
        wr = jnp.concatenate([wr_hi, wr_lo], axis=1)
        br = jnp.pad(jnp.concatenate([b_router_group[l], b_router_expert[l]]),
                     (0, LANES - N_GROUPS - N_EXPERTS)).reshape(1, LANES)
        xf, h2, route, counts = _outproj(xf, od, of, gt1, sc2, sh2, g_ffn[l], w_out[l].astype(BF16),
                                         wr, br)

        rows, block_expert, block_count = _slots(route, counts)
        rows = rows.reshape(TOP_K * N_PLANES * N_TOK // SC_WINDOW, SC_WINDOW)
        xs = _sc_scatter2(h2.reshape(N_PLANES * N_TOK, LANES), rows, N_PLANES * PLANE_ROWS)
        ys = _experts(l, block_expert, block_count, xs.reshape(N_PLANES, PLANE_ROWS, LANES),
                      w_expert_gate, w_expert_up, w_expert_down)
        y2 = _sc_gather(ys.reshape(N_PLANES * PLANE_ROWS, LANES), rows)
        moe = (route, y2.reshape(TOP_K, N_PLANES, N_TOK, LANES))
        gate = gt2
    out = _final(xf, moe, gate, g_final)
    return out.reshape(BATCH, SEQ, D_MODEL)
```

```python
import functools

import numpy as np
import jax
import jax.numpy as jnp
from jax import lax
from jax.experimental import pallas as pl
from jax.experimental.pallas import tpu as pltpu
from jax.experimental.pallas import tpu_sc as plsc

D_MODEL = 1024
BATCH = 4
SEQ = 4096
DEPTH = 4
N_TOK = BATCH * SEQ

CHUNK = 64
HEAD_DIM = 64
N_DIFF_HEADS = 4
N_FOX_HEADS = 8
DIFF_WIDTH = 512
FOX_WIDTH = 512
IN_COLS = 3 * DIFF_WIDTH + 3 * FOX_WIDTH + N_FOX_HEADS
ROT_DIM = 16
ROPE_THETA = 500000.0
N_GROUPS = 4
EXPERTS_PER_GROUP = 8
N_EXPERTS = 32
TOP_K = 2
D_EXPERT = 512
EPS = 1e-6

LANES = 128
IN_COLS_PAD = 3200
FF_COL = 3 * DIFF_WIDTH + 3 * FOX_WIDTH
QK_WIDTH = 8 * LANES
TM = 512
OUTPROJ_TILES = 2
TQ = 512
ATTN_TQ = 256
N_LATE_CHAINS = 1
ONES_ROWS = 16
ATTN_ROWS = 64
LOG2E = 1.4426950408889634
MOE_BLOCK = 512
MOE_ROWS = N_TOK * TOP_K + N_EXPERTS * MOE_BLOCK
MOE_NBLOCKS = MOE_ROWS // MOE_BLOCK
PLANE_ROWS = MOE_ROWS
N_PLANES = D_MODEL // 2 // LANES
SC_WINDOW = 128
SC_INFLIGHT = 4
NEG = -1e30
VMEM_LIMIT = 56 * 1024 * 1024

F32 = jnp.float32
BF16 = jnp.bfloat16


def _bf16_round(x):
    return x.astype(BF16).astype(F32)


def _lane_iota(shape):
    return lax.broadcasted_iota(jnp.int32, shape, 1)


def _params(*sem):
    return pltpu.CompilerParams(dimension_semantics=sem, vmem_limit_bytes=VMEM_LIMIT)


def _pack_planes(y, o_ref, rows=slice(None)):
    bits = lax.bitcast_convert_type(_bf16_round(y), jnp.uint32)
    half = D_MODEL // 2
    word = bits[:, half:] | lax.shift_right_logical(bits[:, :half], jnp.uint32(16))
    word = lax.bitcast_convert_type(word, jnp.int32)
    for p in range(N_PLANES):
        o_ref[p, rows, :] = word[:, p * LANES:(p + 1) * LANES]


def _unpack_planes(planes):
    lo, hi = [], []
    for w in planes:
        u = lax.bitcast_convert_type(w, jnp.uint32)
        lo.append(lax.bitcast_convert_type(lax.shift_left(u, jnp.uint32(16)), F32))
        hi.append(lax.bitcast_convert_type(u & jnp.uint32(0xFFFF0000), F32))
    return jnp.concatenate(lo + hi, axis=1)


def _combine(route_ref, y_ref):
    rt = route_ref[...]
    y0 = _unpack_planes([y_ref[0, p] for p in range(N_PLANES)])
    y1 = _unpack_planes([y_ref[1, p] for p in range(N_PLANES)])
    return rt[:, 2:3] * y0 + rt[:, 3:4] * y1


def _mod_kernel(c_ref, w_ref, b_ref, o_ref):
    c = c_ref[...]
    cond = c / (1.0 + jnp.exp(-c))
    ch = cond.astype(BF16)
    cl = (cond - ch.astype(F32)).astype(BF16)
    w = w_ref[...]
    wh = w.astype(BF16)
    wl = (w - wh.astype(F32)).astype(BF16)
    acc = jnp.dot(ch, wh, preferred_element_type=F32)
    acc += jnp.dot(cl, wh, preferred_element_type=F32)
    acc += jnp.dot(ch, wl, preferred_element_type=F32)
    o_ref[...] = acc + b_ref[...]


def _modulation(c, w_ada, b_ada):
    rows = 16
    tn = 1536
    c_pad = jnp.zeros((rows, D_MODEL), F32).at[:BATCH].set(c)
    out = pl.pallas_call(
        _mod_kernel,
        grid=(DEPTH, 6 * D_MODEL // tn),
        in_specs=[
            pl.BlockSpec((rows, D_MODEL), lambda l, n: (0, 0)),
            pl.BlockSpec((None, D_MODEL, tn), lambda l, n: (l, 0, n)),
            pl.BlockSpec((None, 1, tn), lambda l, n: (l, 0, n)),
        ],
        out_specs=pl.BlockSpec((None, rows, tn), lambda l, n: (l, 0, n)),
        out_shape=jax.ShapeDtypeStruct((DEPTH, rows, 6 * D_MODEL), F32),
        compiler_params=_params("arbitrary", "arbitrary"),
        name="adaln_mod",
    )(c_pad, w_ada, b_ada.reshape(DEPTH, 1, 6 * D_MODEL))
    return out[:, :BATCH]


def _rope_kernel(pos_ref, inv_ref, c_ref, sa_ref, sb_ref):
    ang = pos_ref[...].astype(F32) * inv_ref[...]
    j = _lane_iota(ang.shape) % HEAD_DIM
    cosv = jnp.cos(ang)
    sinv = jnp.sin(ang)
    half = ROT_DIM // 2
    c_ref[...] = jnp.where(j < ROT_DIM, cosv, 1.0)
    sa_ref[...] = jnp.where(j < half, -sinv, 0.0)
    sb_ref[...] = jnp.where((j >= half) & (j < ROT_DIM), sinv, 0.0)


def _rope_tables(positions):
    half = ROT_DIM // 2
    inv = ROPE_THETA ** (-jnp.arange(0, ROT_DIM, 2, dtype=F32) / ROT_DIM)
    lane = np.arange(LANES)
    inv_lane = inv[(lane % HEAD_DIM) % half].reshape(1, LANES)
    spec = pl.BlockSpec((TM, LANES), lambda i: (i, 0))
    shape = jax.ShapeDtypeStruct((N_TOK, LANES), F32)
    return pl.pallas_call(
        _rope_kernel,
        grid=(N_TOK // TM,),
        in_specs=[pl.BlockSpec((TM, 1), lambda i: (i, 0)),
                  pl.BlockSpec((1, LANES), lambda i: (0, 0))],
        out_specs=[spec, spec, spec],
        out_shape=[shape, shape, shape],
        compiler_params=_params("arbitrary"),
        name="rope_tables",
    )(positions.reshape(N_TOK, 1), inv_lane)


def _rms_mod(x, g, sc, sh):
    ms = jnp.mean(x * x, axis=-1, keepdims=True)
    return (x * lax.rsqrt(ms + EPS) * g) * (1.0 + sc) + sh


def _inproj_kernel(fuse, *refs):
    if fuse:
        (x_ref, rt_ref, y_ref, gt_ref, sc_ref, sh_ref, g_ref, w_ref, bf_ref, c_ref, sa_ref, sb_ref,
         pq_ref, xo_ref, dq_ref, dk_ref, dv_ref, fq_ref, fk_ref, fv_ref, carry_ref) = refs
        x = x_ref[...] + gt_ref[...] * _combine(rt_ref, y_ref)
        xo_ref[...] = x
    else:
        (x_ref, sc_ref, sh_ref, g_ref, w_ref, bf_ref, c_ref, sa_ref, sb_ref,
         pq_ref, dq_ref, dk_ref, dv_ref, fq_ref, fk_ref, fv_ref, carry_ref) = refs
        x = x_ref[...]
    hb = _rms_mod(x, g_ref[...], sc_ref[...], sh_ref[...]).astype(BF16)

    @pl.when(pl.program_id(0) % (SEQ // TM) == 0)
    def _():
        carry_ref[...] = jnp.zeros_like(carry_ref)

    lane = _lane_iota((TM, LANES))
    nh = N_FOX_HEADS

    def pack3(a):
        hi = _bf16_round(a)
        r1 = a - hi
        mid = _bf16_round(r1)
        lo = _bf16_round(r1 - mid)
        return jnp.where(lane < nh, hi,
                         jnp.where(lane < 2 * nh, pltpu.roll(mid, nh, 1),
                                   jnp.where(lane < 3 * nh, pltpu.roll(lo, 2 * nh, 1), 0.0)))

    z = jnp.dot(hb, w_ref[:, FF_COL:FF_COL + LANES], preferred_element_type=F32) + bf_ref[...]

    low = lane < HEAD_DIM
    rc, rsa, rsb = c_ref[...], sa_ref[...], sb_ref[...]
    scale = HEAD_DIM ** -0.5 * LOG2E

    def split_store(chunk, o_ref, m, extra_a=None, extra_b=None):
        a = jnp.where(low, chunk, 0.0)
        b = jnp.where(low, pltpu.roll(chunk, HEAD_DIM, 1), 0.0)
        if extra_a is not None:
            a = a + extra_a
            b = b + extra_b
        o_ref[:, (2 * m) * LANES:(2 * m + 1) * LANES] = a.astype(BF16)
        o_ref[:, (2 * m + 1) * LANES:(2 * m + 2) * LANES] = b.astype(BF16)

    def rope(xc):
        return xc * rc + pltpu.roll(xc, LANES - ROT_DIM // 2, 1) * rsa + pltpu.roll(xc, ROT_DIM // 2, 1) * rsb

    pdq = jnp.dot(hb, w_ref[:, 0:DIFF_WIDTH], preferred_element_type=F32)
    for m in range(N_DIFF_HEADS):
        split_store(rope(pdq[:, m * LANES:(m + 1) * LANES]) * scale, dq_ref, m)
    pdk = jnp.dot(hb, w_ref[:, DIFF_WIDTH:2 * DIFF_WIDTH], preferred_element_type=F32)
    for m in range(N_DIFF_HEADS):
        split_store(rope(pdk[:, m * LANES:(m + 1) * LANES]), dk_ref, m)
    def store_values_t(pv, o_ref, width):
        ones = jnp.ones((ONES_ROWS, TM), BF16)
        for m in range(4):
            vt = pv[:, m * LANES:(m + 1) * LANES].T.astype(BF16)
            for i in range(LANES // width):
                o_ref[m * (LANES // width) + i, 0:width, :] = vt[i * width:(i + 1) * width]
                o_ref[m * (LANES // width) + i, width:width + ONES_ROWS, :] = ones

    store_values_t(jnp.dot(hb, w_ref[:, 2 * DIFF_WIDTH:3 * DIFF_WIDTH], preferred_element_type=F32),
                   dv_ref, 2 * HEAD_DIM)
    o = 3 * DIFF_WIDTH
    store_values_t(jnp.dot(hb, w_ref[:, o + 2 * FOX_WIDTH:o + 3 * FOX_WIDTH],
                           preferred_element_type=F32), fv_ref, HEAD_DIM)

    logf =jnp.minimum(z, 0.0) - jnp.log(1.0 + jnp.exp(-jnp.abs(z)))
    logf = jnp.where(lane < nh, logf, 0.0)
    row = lax.broadcasted_iota(jnp.int32, (TM, TM), 0)
    col = lax.broadcasted_iota(jnp.int32, (TM, TM), 1)
    tri = (row >= col).astype(BF16)
    r = jnp.dot(tri, pack3(logf).astype(BF16), preferred_element_type=F32)
    cs = r + pltpu.roll(r, LANES - nh, 1) + pltpu.roll(r, LANES - 2 * nh, 1)
    cf = jnp.where(lane < nh, cs + carry_ref[0:1, :], 0.0)
    carry_ref[...] = jnp.broadcast_to(cf[TM - 1:TM, :], carry_ref.shape)

    t3 = jnp.where(lane == 3 * nh, 1.0, pack3(cf * LOG2E)).astype(BF16)
    aug = jnp.dot(t3, pq_ref[...], preferred_element_type=F32)

    pfq =jnp.dot(hb, w_ref[:, o:o + FOX_WIDTH], preferred_element_type=F32)
    for m in range(N_FOX_HEADS // 2):
        split_store(pfq[:, m * LANES:(m + 1) * LANES] * scale, fq_ref, m,
                    aug[:, (2 * m) * LANES:(2 * m + 1) * LANES],
                    aug[:, (2 * m + 1) * LANES:(2 * m + 2) * LANES])
    pfk = jnp.dot(hb, w_ref[:, o + FOX_WIDTH:o + 2 * FOX_WIDTH], preferred_element_type=F32)
    for m in range(N_FOX_HEADS // 2):
        split_store(pfk[:, m * LANES:(m + 1) * LANES], fk_ref, m,
                    aug[:, QK_WIDTH + (2 * m) * LANES:QK_WIDTH + (2 * m + 1) * LANES],
                    aug[:, QK_WIDTH + (2 * m + 1) * LANES:QK_WIDTH + (2 * m + 2) * LANES])


def _forget_placement():
    nh = N_FOX_HEADS
    p = np.zeros((LANES, 2 * QK_WIDTH), np.float32)
    for h in range(nh):
        base_q = h * LANES + HEAD_DIM
        base_k = QK_WIDTH + h * LANES + HEAD_DIM
        for part in range(3):
            p[part * nh + h, base_q + part] = 1.0
            p[3 * nh, base_q + 3 + part] = 1.0
            p[3 * nh, base_k + part] = 1.0
            p[part * nh + h, base_k + 3 + part] = -1.0
    return jnp.asarray(p, BF16)


def _inproj(x, moe, gate, sc, sh, g, w_bf, b_forget, tables, pq):
    fuse = moe is not None
    tpb = SEQ // TM
    row = pl.BlockSpec((TM, D_MODEL), lambda i: (i, 0))
    per_batch = pl.BlockSpec((None, 1, D_MODEL), lambda i: (i // tpb, 0, 0))
    const = lambda shape: pl.BlockSpec(shape, lambda i: (0,) * len(shape))
    tab = pl.BlockSpec((TM, LANES), lambda i: (i, 0))
    in_specs = [row]
    args = [x]
    if fuse:
        in_specs += [tab, pl.BlockSpec((TOP_K, N_PLANES, TM, LANES), lambda i: (0, 0, i, 0)), per_batch]
        args += [moe[0], moe[1], gate]
    in_specs += [per_batch, per_batch, const((1, D_MODEL)), const((D_MODEL, IN_COLS_PAD)),
                 const((1, LANES)), tab, tab, tab, const((LANES, 2 * QK_WIDTH))]
    args += [sc, sh, g.reshape(1, D_MODEL), w_bf, b_forget, *tables, pq]
    wide = pl.BlockSpec((TM, QK_WIDTH), lambda i: (i, 0))
    def vspec(heads, width):
        rows = width + ONES_ROWS
        return (pl.BlockSpec((None, heads, None, rows, TM), lambda i: (i // tpb, 0, i % tpb, 0, 0)),
                jax.ShapeDtypeStruct((BATCH, heads, tpb, rows, TM), BF16))

    wide_s = jax.ShapeDtypeStruct((N_TOK, QK_WIDTH), BF16)
    dv_spec, dv_s = vspec(N_DIFF_HEADS, 2 * HEAD_DIM)
    fv_spec, fv_s = vspec(N_FOX_HEADS, HEAD_DIM)
    out_specs = [wide, wide, dv_spec, wide, wide, fv_spec]
    out_shape = [wide_s, wide_s, dv_s, wide_s, wide_s, fv_s]
    if fuse:
        out_specs = [row] + out_specs
        out_shape = [jax.ShapeDtypeStruct((N_TOK, D_MODEL), F32)] + out_shape
    outs = pl.pallas_call(
        functools.partial(_inproj_kernel, fuse),
        grid=(N_TOK // TM,),
        in_specs=in_specs,
        out_specs=out_specs,
        out_shape=out_shape,
        scratch_shapes=[pltpu.VMEM((8, LANES), F32)],
        compiler_params=_params("arbitrary"),
        name="norm_inproj",
    )(*args)
    if fuse:
        return outs[0], outs[1:]
    return x, outs


def _attn_kernel(diff, lambda_init, qa_ref, qb_ref, ka_ref, kb_ref, v_ref, g_ref, lam_ref, o_ref,
                 *scratch):
    nq = SEQ // TQ
    n_half = TQ // ATTN_TQ
    feat = 2 * HEAD_DIM if diff else HEAD_DIM
    chains = []
    for mi, (q_ref, k_ref) in enumerate(((qa_ref, ka_ref), (qb_ref, kb_ref))):
        for h in range(n_half):
            c = mi * n_half + h
            qt_sc, s_sc, p_sc, m_sc, a_sc, acc_sc = scratch[c::2 * n_half]
            vh = 0 if diff else mi
            chains.append((h, k_ref, qt_sc, s_sc, p_sc, m_sc, a_sc, acc_sc, q_ref, vh))
    order = [chains[mi * n_half + h] for h in range(n_half) for mi in range(2)]
    early, late = order[:-N_LATE_CHAINS], order[-N_LATE_CHAINS:]

    def load_queries(qi):
        for mi in range(2):
            q_ref = chains[mi * n_half][8]
            qt = q_ref[pl.ds(pl.multiple_of(qi * TQ, TQ), TQ), :].astype(F32).T.astype(BF16)
            for h in range(n_half):
                chains[mi * n_half + h][2][...] = qt[:, h * ATTN_TQ:(h + 1) * ATTN_TQ]

    def reset_state():
        for chain in chains:
            m_sc, _, acc_sc = chain[5:8]
            m_sc[...] = jnp.full(m_sc.shape, NEG, F32)
            acc_sc[...] = jnp.zeros(acc_sc.shape, F32)

    def n_keys(chain, masked):
        return (chain[0] + 1) * ATTN_TQ if masked else TQ

    def scores(chain, j, masked):
        h, k_ref, qt_sc, s_sc = chain[:4]
        nk = n_keys(chain, masked)
        off = pl.multiple_of(j * TQ, TQ)
        s = jnp.dot(k_ref[pl.ds(off, nk), :], qt_sc[...], preferred_element_type=F32)
        if masked:
            kk = lax.broadcasted_iota(jnp.int32, (nk, ATTN_TQ), 0)
            qq = h * ATTN_TQ + lax.broadcasted_iota(jnp.int32, (nk, ATTN_TQ), 1)
            s = jnp.where((kk // CHUNK <= qq // CHUNK) if diff else (kk <= qq), s, NEG)
        s_sc[0:nk, :] = s

    def softmax(chain, masked):
        s_sc, p_sc, m_sc, a_sc = chain[3:7]
        nk = n_keys(chain, masked)
        m_all = m_sc[...]
        m_parts = []
        for c0 in range(0, ATTN_TQ, LANES):
            cols = slice(c0, c0 + LANES)
            pm = s_sc[0:ATTN_ROWS, cols]
            for r0 in range(ATTN_ROWS, nk, ATTN_ROWS):
                pm = jnp.maximum(pm, s_sc[r0:r0 + ATTN_ROWS, cols])
            m_new = jnp.maximum(m_all[:, cols], jnp.max(pm, axis=0, keepdims=True))
            for r0 in range(0, nk, ATTN_ROWS):
                p = jnp.exp2(s_sc[r0:r0 + ATTN_ROWS, cols] - m_new)
                p_sc[r0:r0 + ATTN_ROWS, cols] = p.astype(BF16)
            m_parts.append(m_new)
        m_new = jnp.concatenate(m_parts, axis=1)
        a_sc[...] = jnp.exp2(m_all - m_new)
        m_sc[...] = m_new

    def values(chain, j, masked=False):
        p_sc, a_sc, acc_sc, vh = chain[4], chain[6], chain[7], chain[9]
        nk = n_keys(chain, masked)
        pv = jnp.dot(v_ref[vh, j, :, 0:nk], p_sc[0:nk, :], preferred_element_type=F32)
        acc_sc[...] = a_sc[...] * acc_sc[...] + pv

    def idle_late():
        for chain in late:
            chain[4][...] = jnp.zeros(chain[4].shape, BF16)
            chain[6][...] = jnp.ones(chain[6].shape, F32)

    def consume(j, cur_masked=False, nxt=None, nxt_masked=False, final=False, before_next=None):
        def open_late(chain):
            scores(chain, j, cur_masked)
            values(chain, jnp.maximum(j - 1, 0))

        open_late(late[0])
        for i, chain in enumerate(early):
            softmax(chain, cur_masked)
            if i == 0:
                for other in late[1:]:
                    open_late(other)
                if before_next is not None:
                    before_next()
            if nxt is not None:
                scores(chain, nxt, nxt_masked)
            values(chain, j, cur_masked)
        for chain in late:
            softmax(chain, cur_masked)
        if final:
            for chain in late:
                values(chain, j, cur_masked)

    def finalize(qi):
        ot = [jnp.concatenate([chains[mi * n_half + h][7][0:feat] / chains[mi * n_half + h][7][feat:feat + 1]
                               for h in range(n_half)], axis=1) for mi in range(2)]
        g = g_ref[...]
        rows = pl.ds(pl.multiple_of(qi * TQ, TQ), TQ)
        if diff:
            lv = lam_ref[...]
            lam = (jnp.exp(jnp.sum(lv[0:1] * lv[1:2], axis=1, keepdims=True))
                   - jnp.exp(jnp.sum(lv[2:3] * lv[3:4], axis=1, keepdims=True)) + lambda_init)
            o = (ot[0] - lam * ot[1]).T
            y = o * lax.rsqrt(jnp.mean(o * o, axis=1, keepdims=True) + EPS) * g
            o_ref[rows, :] = (y * (1.0 - lambda_init)).astype(o_ref.dtype)
        else:
            o = jnp.concatenate(ot, axis=0).T
            low = _lane_iota((TQ, LANES)) < HEAD_DIM
            sq = o * o
            msa = jnp.sum(jnp.where(low, sq, 0.0), axis=1, keepdims=True) / HEAD_DIM
            msb = jnp.sum(jnp.where(low, 0.0, sq), axis=1, keepdims=True) / HEAD_DIM
            inv = jnp.where(low, lax.rsqrt(msa + EPS), lax.rsqrt(msb + EPS))
            o_ref[rows, :] = (o * inv * g).astype(o_ref.dtype)

    load_queries(0)
    reset_state()
    idle_late()
    for chain in early:
        scores(chain, 0, True)

    @pl.loop(0, nq)
    def _(qi):
        n_plain = jnp.maximum(qi - 1, 0)

        def run(first, count):
            for i in range(count):
                consume(first + i, nxt=first + i + 1)

        @pl.loop(0, n_plain // 4)
        def _(t):
            run(4 * t, 4)

        done = (n_plain // 4) * 4

        @pl.when(n_plain - done >= 2)
        def _():
            run(done, 2)

        @pl.when(n_plain % 2 == 1)
        def _():
            run(qi - 2, 1)

        def last_blocks(to_next_tile):
            if to_next_tile:
                consume(qi, cur_masked=True, nxt=0, final=True, before_next=lambda: load_queries(qi + 1))
            else:
                consume(qi, cur_masked=True, final=True)
            finalize(qi)

        for to_next_tile in (True, False):
            more = (qi < nq - 1) if to_next_tile else (qi == nq - 1)

            @pl.when(more & (qi > 0))
            def _():
                consume(qi - 1, nxt=qi, nxt_masked=True)
                last_blocks(to_next_tile)

            if to_next_tile:
                @pl.when(qi == 0)
                def _():
                    last_blocks(to_next_tile)

        reset_state()
        idle_late()


def _attention(diff, lambda_init, q, k, v, g, lamv):
    nq = SEQ // TQ
    kspec = lambda par: pl.BlockSpec((SEQ, LANES), lambda b, p: (b, 2 * p + par))
    return pl.pallas_call(
        functools.partial(_attn_kernel, diff, lambda_init),
        grid=(BATCH, 4),
        in_specs=[kspec(0), kspec(1), kspec(0), kspec(1),
                  pl.BlockSpec((None, v.shape[1] // 4, nq, v.shape[3], TQ), lambda b, p: (b, p, 0, 0, 0)),
                  pl.BlockSpec((1, LANES), lambda b, p: (0, 0)),
                  pl.BlockSpec((8, LANES), lambda b, p: (0, 0))],
        out_specs=pl.BlockSpec((SEQ, LANES), lambda b, p: (b, p)),
        out_shape=jax.ShapeDtypeStruct((N_TOK, DIFF_WIDTH), BF16),
        scratch_shapes=[pltpu.VMEM(shape, dt)
                        for shape, dt in (((LANES, ATTN_TQ), BF16), ((TQ, ATTN_TQ), F32),
                                          ((TQ, ATTN_TQ), BF16), ((1, ATTN_TQ), F32),
                                          ((1, ATTN_TQ), F32), ((v.shape[3], ATTN_TQ), F32))
                        for _ in range(2 * TQ // ATTN_TQ)],
        compiler_params=_params("arbitrary", "arbitrary"),
        name="diff_attention" if diff else "fox_attention",
    )(q, q, k, k, v, g, lamv)


def _outproj_kernel(x_ref, od_ref, of_ref, gt_ref, sc_ref, sh_ref, g_ref, wo_ref, wr_ref, br_ref,
                    x1_ref, h2_ref, rt_ref, cnt_ref, carry_ref):
    @pl.when(pl.program_id(0) == 0)
    def _():
        carry_ref[...] = jnp.zeros_like(carry_ref)

    tiles = [slice(t * TM, (t + 1) * TM) for t in range(OUTPROJ_TILES)]
    lane = _lane_iota((TM, LANES))
    lanef = lane.astype(F32)
    big = float(LANES)

    def project(rows):
        mix = jnp.dot(od_ref[rows, :], wo_ref[0:DIFF_WIDTH, :], preferred_element_type=F32)
        return mix + jnp.dot(of_ref[rows, :], wo_ref[DIFF_WIDTH:, :], preferred_element_type=F32)

    def normalise(rows, mix):
        x1 = x_ref[rows, :] + gt_ref[...] * mix
        x1_ref[rows, :] = x1
        h = _rms_mod(x1, g_ref[...], sc_ref[...], sh_ref[...])
        hh = h.astype(BF16)
        _pack_planes(h, h2_ref, rows)
        return hh, (h - hh.astype(F32)).astype(BF16)

    def router_logits(hh, hl):
        r1 = jnp.dot(hh, wr_ref[...], preferred_element_type=F32)
        r2 = jnp.dot(hl, wr_ref[:, 0:LANES], preferred_element_type=F32)
        return r1[:, 0:LANES] + r1[:, LANES:] + r2 + br_ref[...]

    def top_k(logits):
        isg = lane < N_GROUPS
        lg = jnp.where(isg, logits, NEG)
        mg = jnp.max(lg, axis=1, keepdims=True)
        sg = jnp.sum(jnp.where(isg, jnp.exp(lg - mg), 0.0), axis=1, keepdims=True)
        p_g = 1.0 / sg
        gsel = jnp.min(jnp.where(isg & (lg == mg), lanef, big), axis=1, keepdims=True)
        lo = N_GROUPS + gsel * EXPERTS_PER_GROUP
        ise = (lanef >= lo) & (lanef < lo + EXPERTS_PER_GROUP)
        le = jnp.where(ise, logits, NEG)
        t1 = jnp.max(le, axis=1, keepdims=True)
        i1 = jnp.min(jnp.where(ise & (le == t1), lanef, big), axis=1, keepdims=True)
        ise2 = ise & (lanef != i1)
        le2 = jnp.where(ise2, logits, NEG)
        t2 = jnp.max(le2, axis=1, keepdims=True)
        i2 = jnp.min(jnp.where(ise2 & (le2 == t2), lanef, big), axis=1, keepdims=True)
        d = jnp.exp(t2 - t1)
        return i1 - N_GROUPS, i2 - N_GROUPS, p_g / (1.0 + d), p_g * d / (1.0 + d)

    def earlier_in_tile(e1, e2):
        both = jnp.where((lanef == e1) | (lanef == e2), 1.0, 0.0)
        row = lax.broadcasted_iota(jnp.int32, (TM, TM), 0)
        col = lax.broadcasted_iota(jnp.int32, (TM, TM), 1)
        before = jnp.dot((row > col).astype(BF16), both.astype(BF16), preferred_element_type=F32)
        return before, jnp.sum(both, axis=0, keepdims=True)

    mixes = [project(rows) for rows in tiles]
    splits = [normalise(rows, mix) for rows, mix in zip(tiles, mixes)]
    logits = [router_logits(hh, hl) for hh, hl in splits]
    picks = [top_k(lg) for lg in logits]
    befores = [earlier_in_tile(e1, e2) for e1, e2, _, _ in picks]
    counts = carry_ref[0:1, :]
    for rows, (e1, e2, w1, w2), (before, added) in zip(tiles, picks, befores):
        before = before + counts
        rank1 = jnp.sum(jnp.where(lanef == e1, before, 0.0), axis=1, keepdims=True)
        rank2 = jnp.sum(jnp.where(lanef == e2, before, 0.0), axis=1, keepdims=True)
        out = jnp.zeros((TM, LANES), F32)
        for j, v in enumerate((e1, e2, w1, w2, rank1, rank2)):
            out = jnp.where(lane == j, v, out)
        rt_ref[rows, :] = out
        counts = counts + added
    carry_ref[...] = jnp.broadcast_to(counts, carry_ref.shape)
    cnt_ref[...] = jnp.broadcast_to(counts, cnt_ref.shape)


def _outproj(x, od, of, gt, sc, sh, g, wo_bf, wr, br):
    tm = OUTPROJ_TILES * TM
    tpb = SEQ // tm
    row = pl.BlockSpec((tm, D_MODEL), lambda i: (i, 0))
    half = pl.BlockSpec((tm, DIFF_WIDTH), lambda i: (i, 0))
    per_batch = pl.BlockSpec((None, 1, D_MODEL), lambda i: (i // tpb, 0, 0))
    const = lambda shape: pl.BlockSpec(shape, lambda i: (0,) * len(shape))
    return pl.pallas_call(
        _outproj_kernel,
        grid=(N_TOK // tm,),
        in_specs=[row, half, half, per_batch, per_batch, per_batch, const((1, D_MODEL)),
                  const((D_MODEL, D_MODEL)), const((D_MODEL, 2 * LANES)), const((1, LANES))],
        out_specs=[row, pl.BlockSpec((N_PLANES, tm, LANES), lambda i: (0, i, 0)),
                   pl.BlockSpec((tm, LANES), lambda i: (i, 0)), const((8, LANES))],
        out_shape=[jax.ShapeDtypeStruct((N_TOK, D_MODEL), F32),
                   jax.ShapeDtypeStruct((N_PLANES, N_TOK, LANES), jnp.int32),
                   jax.ShapeDtypeStruct((N_TOK, LANES), F32),
                   jax.ShapeDtypeStruct((8, LANES), F32)],
        scratch_shapes=[pltpu.VMEM((8, LANES), F32)],
        compiler_params=_params("arbitrary"),
        name="outproj_router",
    )(x, od, of, gt, sc, sh, g.reshape(1, D_MODEL), wo_bf, wr, br)


def _expert_kernel(be_ref, cnt_ref, xs_ref, wg_ref, wu_ref, wd_ref, ys_ref, wg_sc, wu_sc, wd_sc):
    i = pl.program_id(0)
    cnt = cnt_ref[i]

    @pl.when((i == 0) | (be_ref[i] != be_ref[jnp.maximum(i - 1, 0)]))
    def _():
        wg_sc[...] = wg_ref[...].astype(BF16)
        wu_sc[...] = wu_ref[...].astype(BF16)
        wd_sc[...] = wd_ref[...].astype(BF16)

    @pl.when(cnt > 0)
    def _():
        live = lax.broadcasted_iota(jnp.int32, (MOE_BLOCK, LANES), 0) < cnt
        xb = _unpack_planes([jnp.where(live, xs_ref[p], 0) for p in range(N_PLANES)]).astype(BF16)
        a = jnp.dot(xb, wg_sc[...], preferred_element_type=F32)
        u = jnp.dot(xb, wu_sc[...], preferred_element_type=F32)
        hid = (a / (1.0 + jnp.exp(-a)) * u).astype(BF16)
        _pack_planes(jnp.dot(hid, wd_sc[...], preferred_element_type=F32), ys_ref)

    @pl.when(cnt == 0)
    def _():
        ys_ref[...] = jnp.zeros_like(ys_ref)


def _experts(layer, block_expert, block_count, xs, wg, wu, wd):
    planes = pl.BlockSpec((N_PLANES, MOE_BLOCK, LANES), lambda i, be, bc: (0, i, 0))
    w_in = pl.BlockSpec((None, None, D_MODEL, D_EXPERT), lambda i, be, bc: (layer, be[i], 0, 0))
    w_out = pl.BlockSpec((None, None, D_EXPERT, D_MODEL), lambda i, be, bc: (layer, be[i], 0, 0))
    grid_spec = pltpu.PrefetchScalarGridSpec(
        num_scalar_prefetch=2,
        grid=(MOE_NBLOCKS,),
        in_specs=[planes, w_in, w_in, w_out],
        out_specs=planes,
        scratch_shapes=[pltpu.VMEM((D_MODEL, D_EXPERT), BF16), pltpu.VMEM((D_MODEL, D_EXPERT), BF16),
                        pltpu.VMEM((D_EXPERT, D_MODEL), BF16)],
    )
    return pl.pallas_call(
        _expert_kernel,
        grid_spec=grid_spec,
        out_shape=jax.ShapeDtypeStruct((N_PLANES, PLANE_ROWS, LANES), jnp.int32),
        compiler_params=_params("arbitrary"),
        name="expert_mlp",
    )(block_expert, block_count, xs, wg, wu, wd)


def _slots(route, counts):
    counts = counts[0, :N_EXPERTS].astype(jnp.int32)
    padded = ((counts + MOE_BLOCK - 1) // MOE_BLOCK) * MOE_BLOCK
    pend = jnp.cumsum(padded)
    pstart = pend - padded
    bstart = jnp.arange(MOE_NBLOCKS, dtype=jnp.int32) * MOE_BLOCK
    block_expert = jnp.minimum(jnp.sum(bstart[:, None] >= pend[None, :], axis=1), N_EXPERTS - 1)
    block_expert = block_expert.astype(jnp.int32)
    mine = block_expert[:, None] == jnp.arange(N_EXPERTS, dtype=jnp.int32)[None, :]
    left = jnp.sum(jnp.where(mine, counts + pstart, 0), axis=1) - bstart
    block_count = jnp.clip(left, 0, MOE_BLOCK).astype(jnp.int32)
    base = jnp.pad(pstart.astype(F32), (0, LANES - N_EXPERTS)).reshape(1, LANES)
    return _slot_rows(route, base), block_expert, block_count


def _slot_rows_kernel(rt_ref, base_ref, o_ref):
    rt = rt_ref[...]
    lanef = _lane_iota(rt.shape).astype(F32)
    base = base_ref[...]
    dest = jnp.zeros(rt.shape, F32)
    for k in range(TOP_K):
        b = jnp.sum(jnp.where(lanef == rt[:, k:k + 1], base, 0.0), axis=1, keepdims=True)
        dest = jnp.where(lanef == k, b + rt[:, 2 * TOP_K + k:2 * TOP_K + k + 1], dest)
    dest_t = dest.T.astype(jnp.int32)
    for k in range(TOP_K):
        for p in range(N_PLANES):
            for c in range(rt.shape[0] // LANES):
                o_ref[k * N_PLANES + p, c:c + 1, :] = dest_t[k:k + 1, c * LANES:(c + 1) * LANES] + p * PLANE_ROWS


def _slot_rows(route, base):
    tm = 8 * LANES
    return pl.pallas_call(
        _slot_rows_kernel,
        grid=(N_TOK // tm,),
        in_specs=[pl.BlockSpec((tm, LANES), lambda i: (i, 0)), pl.BlockSpec((1, LANES), lambda i: (0, 0))],
        out_specs=pl.BlockSpec((TOP_K * N_PLANES, tm // LANES, LANES), lambda i: (0, i, 0)),
        out_shape=jax.ShapeDtypeStruct((TOP_K * N_PLANES, N_TOK // LANES, LANES), jnp.int32),
        compiler_params=_params("arbitrary"),
        name="slot_rows",
    )(route, base)


def _sc_workers():
    info = plsc.get_sparse_core_info()
    return info.num_cores, info.num_cores * info.num_subcores


def _sc_scatter2(src, idx, out_rows):
    n_win = src.shape[0] // SC_WINDOW
    nc, nw = _sc_workers()
    steps = n_win // nw
    mesh = plsc.VectorSubcoreMesh(core_axis_name="c", subcore_axis_name="s")

    @functools.partial(
        pl.kernel, mesh=mesh,
        out_type=jax.ShapeDtypeStruct((out_rows, LANES), src.dtype),
        scratch_types=[pltpu.VMEM((2 * steps, SC_WINDOW), jnp.int32),
                       pltpu.VMEM((SC_INFLIGHT, SC_WINDOW, LANES), src.dtype),
                       pltpu.SemaphoreType.DMA((SC_INFLIGHT,)), pltpu.SemaphoreType.DMA((SC_INFLIGHT,))],
        name="sc_dispatch_scatter",
    )
    def k(src_hbm, idx_hbm, out_hbm, idx_v, rows_v, lsem, wsem):
        first = (lax.axis_index("s") * nc + lax.axis_index("c")) * steps
        pltpu.sync_copy(idx_hbm.at[pl.ds(first, steps)], idx_v.at[pl.ds(0, steps)])
        pltpu.sync_copy(idx_hbm.at[pl.ds(n_win + first, steps)], idx_v.at[pl.ds(steps, steps)])

        @pl.loop(0, steps, step=SC_INFLIGHT)
        def _(j):
            loads = [pltpu.async_copy(src_hbm.at[pl.ds((first + j + b) * SC_WINDOW, SC_WINDOW)],
                                      rows_v.at[b], lsem.at[b]) for b in range(SC_INFLIGHT)]
            writes = []
            for b in range(SC_INFLIGHT):
                loads[b].wait()
                for half in range(TOP_K):
                    dst = out_hbm.at[idx_v.at[half * steps + j + b]]
                    writes.append(pltpu.async_copy(rows_v.at[b], dst, wsem.at[b]))
            for w in writes:
                w.wait()

    return k(src, idx)


def _sc_gather(table, idx):
    n_out = idx.shape[0] * SC_WINDOW
    nc, nw = _sc_workers()
    steps = n_out // nw // SC_WINDOW
    mesh = plsc.VectorSubcoreMesh(core_axis_name="c", subcore_axis_name="s")

    @functools.partial(
        pl.kernel, mesh=mesh,
        out_type=jax.ShapeDtypeStruct((n_out, LANES), table.dtype),
        scratch_types=[pltpu.VMEM((steps, SC_WINDOW), jnp.int32),
                       pltpu.VMEM((SC_INFLIGHT, SC_WINDOW, LANES), table.dtype),
                       pltpu.SemaphoreType.DMA((SC_INFLIGHT,)), pltpu.SemaphoreType.DMA((SC_INFLIGHT,))],
        name="sc_combine_gather",
    )
    def k(table_hbm, idx_hbm, out_hbm, idx_v, rows_v, gsem, wsem):
        first = (lax.axis_index("s") * nc + lax.axis_index("c")) * steps
        pltpu.sync_copy(idx_hbm.at[pl.ds(first, steps)], idx_v)

        @pl.loop(0, steps, step=SC_INFLIGHT)
        def _(j):
            gathers = [pltpu.async_copy(table_hbm.at[idx_v.at[j + b]], rows_v.at[b], gsem.at[b])
                       for b in range(SC_INFLIGHT)]
            writes = []
            for b in range(SC_INFLIGHT):
                gathers[b].wait()
                dst = out_hbm.at[pl.ds((first + j + b) * SC_WINDOW, SC_WINDOW)]
                writes.append(pltpu.async_copy(rows_v.at[b], dst, wsem.at[b]))
            for w in writes:
                w.wait()

    return k(table, idx)


def _final_kernel(x_ref, rt_ref, y_ref, gt_ref, g_ref, o_ref):
    x = x_ref[...] + gt_ref[...] * _combine(rt_ref, y_ref)
    ms = jnp.mean(x * x, axis=-1, keepdims=True)
    o_ref[...] = x * lax.rsqrt(ms + EPS) * g_ref[...]


def _final(x, moe, gate, g):
    tpb = SEQ // TM
    row = pl.BlockSpec((TM, D_MODEL), lambda i: (i, 0))
    return pl.pallas_call(
        _final_kernel,
        grid=(N_TOK // TM,),
        in_specs=[row, pl.BlockSpec((TM, LANES), lambda i: (i, 0)),
                  pl.BlockSpec((TOP_K, N_PLANES, TM, LANES), lambda i: (0, 0, i, 0)),
                  pl.BlockSpec((None, 1, D_MODEL), lambda i: (i // tpb, 0, 0)),
                  pl.BlockSpec((1, D_MODEL), lambda i: (0, 0))],
        out_specs=row,
        out_shape=jax.ShapeDtypeStruct((N_TOK, D_MODEL), F32),
        compiler_params=_params("arbitrary"),
        name="final_norm",
    )(x, moe[0], moe[1], gate, g.reshape(1, D_MODEL))


def kernel(x, c, positions, w_ada, b_ada, g_mix, w_in, b_forget, lambda_q1, lambda_k1, lambda_q2,
           lambda_k2, g_subln, g_fox_out, w_out, g_ffn, w_router_group, b_router_group,
           w_router_expert, b_router_expert, w_expert_gate, w_expert_up, w_expert_down, g_final):
    mod = _modulation(c, w_ada, b_ada)
    mod = mod.reshape(DEPTH, BATCH, 6, 1, D_MODEL)
    tables = _rope_tables(positions)
    pq = _forget_placement()
    xf = x.reshape(N_TOK, D_MODEL)
    moe = None
    gate = None
    for l in range(DEPTH):
        sh1, sc1, gt1, sh2, sc2, gt2 = (mod[l, :, j] for j in range(6))
        w_bf = jnp.pad(w_in[l], ((0, 0), (0, IN_COLS_PAD - IN_COLS))).astype(BF16)
        bfp = jnp.pad(b_forget[l], (0, LANES - N_FOX_HEADS)).reshape(1, LANES)
        xf, (dq, dk, dv, fq, fk, fv) = _inproj(xf, moe, gate, sc1, sh1, g_mix[l], w_bf, bfp, tables, pq)

        lambda_init = 0.8 - 0.6 * float(np.exp(-0.3 * l))
        lamv = jnp.zeros((8, LANES), F32).at[0:4, 0:HEAD_DIM].set(
            jnp.stack([lambda_q1[l], lambda_k1[l], lambda_q2[l], lambda_k2[l]]))
        g_d = g_subln[l].reshape(1, LANES)
        g_f = jnp.concatenate([g_fox_out[l], g_fox_out[l]]).reshape(1, LANES)
        od = _attention(True, lambda_init, dq, dk, dv, g_d, lamv)
        of = _attention(False, lambda_init, fq, fk, fv, g_f, lamv)

        wr32 = jnp.pad(jnp.concatenate([w_router_group[l], w_router_expert[l]], axis=1),
                       ((0, 0), (0, LANES - N_GROUPS - N_EXPERTS)))
        wr_hi = wr32.astype(BF16)
        wr_lo = (wr32 - wr_hi.astype(F32)).astype(BF16)
        wr = jnp.concatenate([wr_hi, wr_lo], axis=1)
        br = jnp.pad(jnp.concatenate([b_router_group[l], b_router_expert[l]]),
                     (0, LANES - N_GROUPS - N_EXPERTS)).reshape(1, LANES)
        xf, h2, route, counts = _outproj(xf, od, of, gt1, sc2, sh2, g_ffn[l], w_out[l].astype(BF16),
                                         wr, br)

        rows, block_expert, block_count = _slots(route, counts)
        rows = rows.reshape(TOP_K * N_PLANES * N_TOK // SC_WINDOW, SC_WINDOW)
        xs = _sc_scatter2(h2.reshape(N_PLANES * N_TOK, LANES), rows, N_PLANES * PLANE_ROWS)
        ys = _experts(l, block_expert, block_count, xs.reshape(N_PLANES, PLANE_ROWS, LANES),
                      w_expert_gate, w_expert_up, w_expert_down)
        y2 = _sc_gather(ys.reshape(N_PLANES * PLANE_ROWS, LANES), rows)
        moe = (route, y2.reshape(TOP_K, N_PLANES, N_TOK, LANES))
        gate = gt2
    out = _final(xf, moe, gate, g_final)
    return out.reshape(BATCH, SEQ, D_MODEL)
```

```python
import functools

import numpy as np
import jax
import jax.numpy as jnp
from jax import lax
from jax.experimental import pallas as pl
from jax.experimental.pallas import tpu as pltpu
from jax.experimental.pallas import tpu_sc as plsc

D_MODEL = 1024
BATCH = 4
SEQ = 4096
DEPTH = 4
N_TOK = BATCH * SEQ

CHUNK = 64
HEAD_DIM = 64
N_DIFF_HEADS = 4
N_FOX_HEADS = 8
DIFF_WIDTH = 512
FOX_WIDTH = 512
IN_COLS = 3 * DIFF_WIDTH + 3 * FOX_WIDTH + N_FOX_HEADS
ROT_DIM = 16
ROPE_THETA = 500000.0
N_GROUPS = 4
EXPERTS_PER_GROUP = 8
N_EXPERTS = 32
TOP_K = 2
D_EXPERT = 512
EPS = 1e-6

LANES = 128
IN_COLS_PAD = 3200
FF_COL = 3 * DIFF_WIDTH + 3 * FOX_WIDTH
QK_WIDTH = 8 * LANES
TM = 512
OUTPROJ_TILES = 2
TQ = 512
ATTN_TQ = 256
N_LATE_CHAINS = 1
ONES_ROWS = 16
ATTN_ROWS = 64
LOG2E = 1.4426950408889634
MOE_BLOCK = 512
MOE_ROWS = N_TOK * TOP_K + N_EXPERTS * MOE_BLOCK
MOE_NBLOCKS = MOE_ROWS // MOE_BLOCK
PLANE_ROWS = MOE_ROWS
N_PLANES = D_MODEL // 2 // LANES
SC_WINDOW = 128
SC_INFLIGHT = 4
NEG = -1e30
VMEM_LIMIT = 56 * 1024 * 1024

F32 = jnp.float32
BF16 = jnp.bfloat16


def _bf16_round(x):
    return x.astype(BF16).astype(F32)


def _lane_iota(shape):
    return lax.broadcasted_iota(jnp.int32, shape, 1)


def _params(*sem):
    return pltpu.CompilerParams(dimension_semantics=sem, vmem_limit_bytes=VMEM_LIMIT)


def _pack_planes(y, o_ref, rows=slice(None)):
    bits = lax.bitcast_convert_type(_bf16_round(y), jnp.uint32)
    half = D_MODEL // 2
    word = bits[:, half:] | lax.shift_right_logical(bits[:, :half], jnp.uint32(16))
    word = lax.bitcast_convert_type(word, jnp.int32)
    for p in range(N_PLANES):
        o_ref[p, rows, :] = word[:, p * LANES:(p + 1) * LANES]


def _unpack_planes(planes):
    lo, hi = [], []
    for w in planes:
        u = lax.bitcast_convert_type(w, jnp.uint32)
        lo.append(lax.bitcast_convert_type(lax.shift_left(u, jnp.uint32(16)), F32))
        hi.append(lax.bitcast_convert_type(u & jnp.uint32(0xFFFF0000), F32))
    return jnp.concatenate(lo + hi, axis=1)


def _combine(route_ref, y_ref):
    rt = route_ref[...]
    y0 = _unpack_planes([y_ref[0, p] for p in range(N_PLANES)])
    y1 = _unpack_planes([y_ref[1, p] for p in range(N_PLANES)])
    return rt[:, 2:3] * y0 + rt[:, 3:4] * y1


def _mod_kernel(c_ref, w_ref, b_ref, o_ref):
    c = c_ref[...]
    cond = c / (1.0 + jnp.exp(-c))
    ch = cond.astype(BF16)
    cl = (cond - ch.astype(F32)).astype(BF16)
    w = w_ref[...]
    wh = w.astype(BF16)
    wl = (w - wh.astype(F32)).astype(BF16)
    acc = jnp.dot(ch, wh, preferred_element_type=F32)
    acc += jnp.dot(cl, wh, preferred_element_type=F32)
    acc += jnp.dot(ch, wl, preferred_element_type=F32)
    o_ref[...] = acc + b_ref[...]


def _modulation(c, w_ada, b_ada):
    rows = 16
    tn = 1536
    c_pad = jnp.zeros((rows, D_MODEL), F32).at[:BATCH].set(c)
    out = pl.pallas_call(
        _mod_kernel,
        grid=(DEPTH, 6 * D_MODEL // tn),
        in_specs=[
            pl.BlockSpec((rows, D_MODEL), lambda l, n: (0, 0)),
            pl.BlockSpec((None, D_MODEL, tn), lambda l, n: (l, 0, n)),
            pl.BlockSpec((None, 1, tn), lambda l, n: (l, 0, n)),
        ],
        out_specs=pl.BlockSpec((None, rows, tn), lambda l, n: (l, 0, n)),
        out_shape=jax.ShapeDtypeStruct((DEPTH, rows, 6 * D_MODEL), F32),
        compiler_params=_params("arbitrary", "arbitrary"),
        name="adaln_mod",
    )(c_pad, w_ada, b_ada.reshape(DEPTH, 1, 6 * D_MODEL))
    return out[:, :BATCH]


def _rope_kernel(pos_ref, inv_ref, c_ref, sa_ref, sb_ref):
    ang = pos_ref[...].astype(F32) * inv_ref[...]
    j = _lane_iota(ang.shape) % HEAD_DIM
    cosv = jnp.cos(ang)
    sinv = jnp.sin(ang)
    half = ROT_DIM // 2
    c_ref[...] = jnp.where(j < ROT_DIM, cosv, 1.0)
    sa_ref[...] = jnp.where(j < half, -sinv, 0.0)
    sb_ref[...] = jnp.where((j >= half) & (j < ROT_DIM), sinv, 0.0)


def _rope_tables(positions):
    half = ROT_DIM // 2
    inv = ROPE_THETA ** (-jnp.arange(0, ROT_DIM, 2, dtype=F32) / ROT_DIM)
    lane = np.arange(LANES)
    inv_lane = inv[(lane % HEAD_DIM) % half].reshape(1, LANES)
    spec = pl.BlockSpec((TM, LANES), lambda i: (i, 0))
    shape = jax.ShapeDtypeStruct((N_TOK, LANES), F32)
    return pl.pallas_call(
        _rope_kernel,
        grid=(N_TOK // TM,),
        in_specs=[pl.BlockSpec((TM, 1), lambda i: (i, 0)),
                  pl.BlockSpec((1, LANES), lambda i: (0, 0))],
        out_specs=[spec, spec, spec],
        out_shape=[shape, shape, shape],
        compiler_params=_params("arbitrary"),
        name="rope_tables",
    )(positions.reshape(N_TOK, 1), inv_lane)


def _rms_mod(x, g, sc, sh):
    ms = jnp.mean(x * x, axis=-1, keepdims=True)
    return (x * lax.rsqrt(ms + EPS) * g) * (1.0 + sc) + sh


def _inproj_kernel(fuse, *refs):
    if fuse:
        (x_ref, rt_ref, y_ref, gt_ref, sc_ref, sh_ref, g_ref, w_ref, bf_ref, c_ref, sa_ref, sb_ref,
         pq_ref, xo_ref, dq_ref, dk_ref, dv_ref, fq_ref, fk_ref, fv_ref, carry_ref) = refs
        x = x_ref[...] + gt_ref[...] * _combine(rt_ref, y_ref)
        xo_ref[...] = x
    else:
        (x_ref, sc_ref, sh_ref, g_ref, w_ref, bf_ref, c_ref, sa_ref, sb_ref,
         pq_ref, dq_ref, dk_ref, dv_ref, fq_ref, fk_ref, fv_ref, carry_ref) = refs
        x = x_ref[...]
    hb = _rms_mod(x, g_ref[...], sc_ref[...], sh_ref[...]).astype(BF16)

    @pl.when(pl.program_id(0) % (SEQ // TM) == 0)
    def _():
        carry_ref[...] = jnp.zeros_like(carry_ref)

    lane = _lane_iota((TM, LANES))
    nh = N_FOX_HEADS

    def pack3(a):
        hi = _bf16_round(a)
        r1 = a - hi
        mid = _bf16_round(r1)
        lo = _bf16_round(r1 - mid)
        return jnp.where(lane < nh, hi,
                         jnp.where(lane < 2 * nh, pltpu.roll(mid, nh, 1),
                                   jnp.where(lane < 3 * nh, pltpu.roll(lo, 2 * nh, 1), 0.0)))

    z = jnp.dot(hb, w_ref[:, FF_COL:FF_COL + LANES], preferred_element_type=F32) + bf_ref[...]

    low = lane < HEAD_DIM
    rc, rsa, rsb = c_ref[...], sa_ref[...], sb_ref[...]
    scale = HEAD_DIM ** -0.5 * LOG2E

    def split_store(chunk, o_ref, m, extra_a=None, extra_b=None):
        a = jnp.where(low, chunk, 0.0)
        b = jnp.where(low, pltpu.roll(chunk, HEAD_DIM, 1), 0.0)
        if extra_a is not None:
            a = a + extra_a
            b = b + extra_b
        o_ref[:, (2 * m) * LANES:(2 * m + 1) * LANES] = a.astype(BF16)
        o_ref[:, (2 * m + 1) * LANES:(2 * m + 2) * LANES] = b.astype(BF16)

    def rope(xc):
        return xc * rc + pltpu.roll(xc, LANES - ROT_DIM // 2, 1) * rsa + pltpu.roll(xc, ROT_DIM // 2, 1) * rsb

    pdq = jnp.dot(hb, w_ref[:, 0:DIFF_WIDTH], preferred_element_type=F32)
    for m in range(N_DIFF_HEADS):
        split_store(rope(pdq[:, m * LANES:(m + 1) * LANES]) * scale, dq_ref, m)
    pdk = jnp.dot(hb, w_ref[:, DIFF_WIDTH:2 * DIFF_WIDTH], preferred_element_type=F32)
    for m in range(N_DIFF_HEADS):
        split_store(rope(pdk[:, m * LANES:(m + 1) * LANES]), dk_ref, m)
    def store_values_t(pv, o_ref, width):
        ones = jnp.ones((ONES_ROWS, TM), BF16)
        for m in range(4):
            vt = pv[:, m * LANES:(m + 1) * LANES].T.astype(BF16)
            for i in range(LANES // width):
                o_ref[m * (LANES // width) + i, 0:width, :] = vt[i * width:(i + 1) * width]
                o_ref[m * (LANES // width) + i, width:width + ONES_ROWS, :] = ones

    store_values_t(jnp.dot(hb, w_ref[:, 2 * DIFF_WIDTH:3 * DIFF_WIDTH], preferred_element_type=F32),
                   dv_ref, 2 * HEAD_DIM)
    o = 3 * DIFF_WIDTH
    store_values_t(jnp.dot(hb, w_ref[:, o + 2 * FOX_WIDTH:o + 3 * FOX_WIDTH],
                           preferred_element_type=F32), fv_ref, HEAD_DIM)

    logf =jnp.minimum(z, 0.0) - jnp.log(1.0 + jnp.exp(-jnp.abs(z)))
    logf = jnp.where(lane < nh, logf, 0.0)
    row = lax.broadcasted_iota(jnp.int32, (TM, TM), 0)
    col = lax.broadcasted_iota(jnp.int32, (TM, TM), 1)
    tri = (row >= col).astype(BF16)
    r = jnp.dot(tri, pack3(logf).astype(BF16), preferred_element_type=F32)
    cs = r + pltpu.roll(r, LANES - nh, 1) + pltpu.roll(r, LANES - 2 * nh, 1)
    cf = jnp.where(lane < nh, cs + carry_ref[0:1, :], 0.0)
    carry_ref[...] = jnp.broadcast_to(cf[TM - 1:TM, :], carry_ref.shape)

    t3 = jnp.where(lane == 3 * nh, 1.0, pack3(cf * LOG2E)).astype(BF16)
    aug = jnp.dot(t3, pq_ref[...], preferred_element_type=F32)

    pfq =jnp.dot(hb, w_ref[:, o:o + FOX_WIDTH], preferred_element_type=F32)
    for m in range(N_FOX_HEADS // 2):
        split_store(pfq[:, m * LANES:(m + 1) * LANES] * scale, fq_ref, m,
                    aug[:, (2 * m) * LANES:(2 * m + 1) * LANES],
                    aug[:, (2 * m + 1) * LANES:(2 * m + 2) * LANES])
    pfk = jnp.dot(hb, w_ref[:, o + FOX_WIDTH:o + 2 * FOX_WIDTH], preferred_element_type=F32)
    for m in range(N_FOX_HEADS // 2):
        split_store(pfk[:, m * LANES:(m + 1) * LANES], fk_ref, m,
                    aug[:, QK_WIDTH + (2 * m) * LANES:QK_WIDTH + (2 * m + 1) * LANES],
                    aug[:, QK_WIDTH + (2 * m + 1) * LANES:QK_WIDTH + (2 * m + 2) * LANES])


def _forget_placement():
    nh = N_FOX_HEADS
    p = np.zeros((LANES, 2 * QK_WIDTH), np.float32)
    for h in range(nh):
        base_q = h * LANES + HEAD_DIM
        base_k = QK_WIDTH + h * LANES + HEAD_DIM
        for part in range(3):
            p[part * nh + h, base_q + part] = 1.0
            p[3 * nh, base_q + 3 + part] = 1.0
            p[3 * nh, base_k + part] = 1.0
            p[part * nh + h, base_k + 3 + part] = -1.0
    return jnp.asarray(p, BF16)


def _inproj(x, moe, gate, sc, sh, g, w_bf, b_forget, tables, pq):
    fuse = moe is not None
    tpb = SEQ // TM
    row = pl.BlockSpec((TM, D_MODEL), lambda i: (i, 0))
    per_batch = pl.BlockSpec((None, 1, D_MODEL), lambda i: (i // tpb, 0, 0))
    const = lambda shape: pl.BlockSpec(shape, lambda i: (0,) * len(shape))
    tab = pl.BlockSpec((TM, LANES), lambda i: (i, 0))
    in_specs = [row]
    args = [x]
    if fuse:
        in_specs += [tab, pl.BlockSpec((TOP_K, N_PLANES, TM, LANES), lambda i: (0, 0, i, 0)), per_batch]
        args += [moe[0], moe[1], gate]
    in_specs += [per_batch, per_batch, const((1, D_MODEL)), const((D_MODEL, IN_COLS_PAD)),
                 const((1, LANES)), tab, tab, tab, const((LANES, 2 * QK_WIDTH))]
    args += [sc, sh, g.reshape(1, D_MODEL), w_bf, b_forget, *tables, pq]
    wide = pl.BlockSpec((TM, QK_WIDTH), lambda i: (i, 0))
    def vspec(heads, width):
        rows = width + ONES_ROWS
        return (pl.BlockSpec((None, heads, None, rows, TM), lambda i: (i // tpb, 0, i % tpb, 0, 0)),
                jax.ShapeDtypeStruct((BATCH, heads, tpb, rows, TM), BF16))

    wide_s = jax.ShapeDtypeStruct((N_TOK, QK_WIDTH), BF16)
    dv_spec, dv_s = vspec(N_DIFF_HEADS, 2 * HEAD_DIM)
    fv_spec, fv_s = vspec(N_FOX_HEADS, HEAD_DIM)
    out_specs = [wide, wide, dv_spec, wide, wide, fv_spec]
    out_shape = [wide_s, wide_s, dv_s, wide_s, wide_s, fv_s]
    if fuse:
        out_specs = [row] + out_specs
        out_shape = [jax.ShapeDtypeStruct((N_TOK, D_MODEL), F32)] + out_shape
    outs = pl.pallas_call(
        functools.partial(_inproj_kernel, fuse),
        grid=(N_TOK // TM,),
        in_specs=in_specs,
        out_specs=out_specs,
        out_shape=out_shape,
        scratch_shapes=[pltpu.VMEM((8, LANES), F32)],
        compiler_params=_params("arbitrary"),
        name="norm_inproj",
    )(*args)
    if fuse:
        return outs[0], outs[1:]
    return x, outs


def _attn_kernel(diff, lambda_init, qa_ref, qb_ref, ka_ref, kb_ref, v_ref, g_ref, lam_ref, o_ref,
                 *scratch):
    nq = SEQ // TQ
    n_half = TQ // ATTN_TQ
    feat = 2 * HEAD_DIM if diff else HEAD_DIM
    chains = []
    for mi, (q_ref, k_ref) in enumerate(((qa_ref, ka_ref), (qb_ref, kb_ref))):
        for h in range(n_half):
            c = mi * n_half + h
            qt_sc, s_sc, p_sc, m_sc, a_sc, acc_sc = scratch[c::2 * n_half]
            vh = 0 if diff else mi
            chains.append((h, k_ref, qt_sc, s_sc, p_sc, m_sc, a_sc, acc_sc, q_ref, vh))
    order = [chains[mi * n_half + h] for h in range(n_half) for mi in range(2)]
    early, late = order[:-N_LATE_CHAINS], order[-N_LATE_CHAINS:]

    def load_queries(qi):
        for mi in range(2):
            q_ref = chains[mi * n_half][8]
            qt = q_ref[pl.ds(pl.multiple_of(qi * TQ, TQ), TQ), :].astype(F32).T.astype(BF16)
            for h in range(n_half):
                chains[mi * n_half + h][2][...] = qt[:, h * ATTN_TQ:(h + 1) * ATTN_TQ]

    def reset_state():
        for chain in chains:
            m_sc, _, acc_sc = chain[5:8]
            m_sc[...] = jnp.full(m_sc.shape, NEG, F32)
            acc_sc[...] = jnp.zeros(acc_sc.shape, F32)

    def n_keys(chain, masked):
        return (chain[0] + 1) * ATTN_TQ if masked else TQ

    def scores(chain, j, masked):
        h, k_ref, qt_sc, s_sc = chain[:4]
        nk = n_keys(chain, masked)
        off = pl.multiple_of(j * TQ, TQ)
        s = jnp.dot(k_ref[pl.ds(off, nk), :], qt_sc[...], preferred_element_type=F32)
        if masked:
            kk = lax.broadcasted_iota(jnp.int32, (nk, ATTN_TQ), 0)
            qq = h * ATTN_TQ + lax.broadcasted_iota(jnp.int32, (nk, ATTN_TQ), 1)
            s = jnp.where((kk // CHUNK <= qq // CHUNK) if diff else (kk <= qq), s, NEG)
        s_sc[0:nk, :] = s

    def softmax(chain, masked):
        s_sc, p_sc, m_sc, a_sc = chain[3:7]
        nk = n_keys(chain, masked)
        m_all = m_sc[...]
        m_parts = []
        for c0 in range(0, ATTN_TQ, LANES):
            cols = slice(c0, c0 + LANES)
            pm = s_sc[0:ATTN_ROWS, cols]
            for r0 in range(ATTN_ROWS, nk, ATTN_ROWS):
                pm = jnp.maximum(pm, s_sc[r0:r0 + ATTN_ROWS, cols])
            m_new = jnp.maximum(m_all[:, cols], jnp.max(pm, axis=0, keepdims=True))
            for r0 in range(0, nk, ATTN_ROWS):
                p = jnp.exp2(s_sc[r0:r0 + ATTN_ROWS, cols] - m_new)
                p_sc[r0:r0 + ATTN_ROWS, cols] = p.astype(BF16)
            m_parts.append(m_new)
        m_new = jnp.concatenate(m_parts, axis=1)
        a_sc[...] = jnp.exp2(m_all - m_new)
        m_sc[...] = m_new

    def values(chain, j, masked=False):
        p_sc, a_sc, acc_sc, vh = chain[4], chain[6], chain[7], chain[9]
        nk = n_keys(chain, masked)
        pv = jnp.dot(v_ref[vh, j, :, 0:nk], p_sc[0:nk, :], preferred_element_type=F32)
        acc_sc[...] = a_sc[...] * acc_sc[...] + pv

    def idle_late():
        for chain in late:
            chain[4][...] = jnp.zeros(chain[4].shape, BF16)
            chain[6][...] = jnp.ones(chain[6].shape, F32)

    def consume(j, cur_masked=False, nxt=None, nxt_masked=False, final=False, before_next=None):
        def open_late(chain):
            scores(chain, j, cur_masked)
            values(chain, jnp.maximum(j - 1, 0))

        open_late(late[0])
        for i, chain in enumerate(early):
            softmax(chain, cur_masked)
            if i == 0:
                for other in late[1:]:
                    open_late(other)
                if before_next is not None:
                    before_next()
            if nxt is not None:
                scores(chain, nxt, nxt_masked)
            values(chain, j, cur_masked)
        for chain in late:
            softmax(chain, cur_masked)
        if final:
            for chain in late:
                values(chain, j, cur_masked)

    def finalize(qi):
        ot = [jnp.concatenate([chains[mi * n_half + h][7][0:feat] / chains[mi * n_half + h][7][feat:feat + 1]
                               for h in range(n_half)], axis=1) for mi in range(2)]
        g = g_ref[...]
        rows = pl.ds(pl.multiple_of(qi * TQ, TQ), TQ)
        if diff:
            lv = lam_ref[...]
            lam = (jnp.exp(jnp.sum(lv[0:1] * lv[1:2], axis=1, keepdims=True))
                   - jnp.exp(jnp.sum(lv[2:3] * lv[3:4], axis=1, keepdims=True)) + lambda_init)
            o = (ot[0] - lam * ot[1]).T
            y = o * lax.rsqrt(jnp.mean(o * o, axis=1, keepdims=True) + EPS) * g
            o_ref[rows, :] = (y * (1.0 - lambda_init)).astype(o_ref.dtype)
        else:
            o = jnp.concatenate(ot, axis=0).T
            low = _lane_iota((TQ, LANES)) < HEAD_DIM
            sq = o * o
            msa = jnp.sum(jnp.where(low, sq, 0.0), axis=1, keepdims=True) / HEAD_DIM
            msb = jnp.sum(jnp.where(low, 0.0, sq), axis=1, keepdims=True) / HEAD_DIM
            inv = jnp.where(low, lax.rsqrt(msa + EPS), lax.rsqrt(msb + EPS))
            o_ref[rows, :] = (o * inv * g).astype(o_ref.dtype)

    load_queries(0)
    reset_state()
    idle_late()
    for chain in early:
        scores(chain, 0, True)

    @pl.loop(0, nq)
    def _(qi):
        n_plain = jnp.maximum(qi - 1, 0)

        def run(first, count):
            for i in range(count):
                consume(first + i, nxt=first + i + 1)

        @pl.loop(0, n_plain // 4)
        def _(t):
            run(4 * t, 4)

        done = (n_plain // 4) * 4

        @pl.when(n_plain - done >= 2)
        def _():
            run(done, 2)

        def last_blocks(to_next_tile):
            if to_next_tile:
                consume(qi, cur_masked=True, nxt=0, final=True, before_next=lambda: load_queries(qi + 1))
            else:
                consume(qi, cur_masked=True, final=True)
            finalize(qi)

        for to_next_tile in (True, False):
            more = (qi < nq - 1) if to_next_tile else (qi == nq - 1)

            last_tile_odd = (nq - 2) % 2 == 1
            for odd in ((True, False) if to_next_tile else (last_tile_odd,)):
                @pl.when(more & (qi > 0) & ((n_plain % 2 == 1) == odd))
                def _():
                    if odd:
                        run(qi - 2, 1)
                    consume(qi - 1, nxt=qi, nxt_masked=True)
                    last_blocks(to_next_tile)

            if to_next_tile:
                @pl.when(qi == 0)
                def _():
                    last_blocks(to_next_tile)

        reset_state()
        idle_late()


def _attention(diff, lambda_init, q, k, v, g, lamv):
    nq = SEQ // TQ
    kspec = lambda par: pl.BlockSpec((SEQ, LANES), lambda b, p: (b, 2 * p + par))
    return pl.pallas_call(
        functools.partial(_attn_kernel, diff, lambda_init),
        grid=(BATCH, 4),
        in_specs=[kspec(0), kspec(1), kspec(0), kspec(1),
                  pl.BlockSpec((None, v.shape[1] // 4, nq, v.shape[3], TQ), lambda b, p: (b, p, 0, 0, 0)),
                  pl.BlockSpec((1, LANES), lambda b, p: (0, 0)),
                  pl.BlockSpec((8, LANES), lambda b, p: (0, 0))],
        out_specs=pl.BlockSpec((SEQ, LANES), lambda b, p: (b, p)),
        out_shape=jax.ShapeDtypeStruct((N_TOK, DIFF_WIDTH), BF16),
        scratch_shapes=[pltpu.VMEM(shape, dt)
                        for shape, dt in (((LANES, ATTN_TQ), BF16), ((TQ, ATTN_TQ), F32),
                                          ((TQ, ATTN_TQ), BF16), ((1, ATTN_TQ), F32),
                                          ((1, ATTN_TQ), F32), ((v.shape[3], ATTN_TQ), F32))
                        for _ in range(2 * TQ // ATTN_TQ)],
        compiler_params=_params("arbitrary", "arbitrary"),
        name="diff_attention" if diff else "fox_attention",
    )(q, q, k, k, v, g, lamv)


def _outproj_kernel(x_ref, od_ref, of_ref, gt_ref, sc_ref, sh_ref, g_ref, wo_ref, wr_ref, br_ref,
                    x1_ref, h2_ref, rt_ref, cnt_ref, carry_ref):
    @pl.when(pl.program_id(0) == 0)
    def _():
        carry_ref[...] = jnp.zeros_like(carry_ref)

    tiles = [slice(t * TM, (t + 1) * TM) for t in range(OUTPROJ_TILES)]
    lane = _lane_iota((TM, LANES))
    lanef = lane.astype(F32)
    big = float(LANES)

    def project(rows):
        mix = jnp.dot(od_ref[rows, :], wo_ref[0:DIFF_WIDTH, :], preferred_element_type=F32)
        return mix + jnp.dot(of_ref[rows, :], wo_ref[DIFF_WIDTH:, :], preferred_element_type=F32)

    def normalise(rows, mix):
        x1 = x_ref[rows, :] + gt_ref[...] * mix
        x1_ref[rows, :] = x1
        h = _rms_mod(x1, g_ref[...], sc_ref[...], sh_ref[...])
        hh = h.astype(BF16)
        _pack_planes(h, h2_ref, rows)
        return hh, (h - hh.astype(F32)).astype(BF16)

    def router_logits(hh, hl):
        r1 = jnp.dot(hh, wr_ref[...], preferred_element_type=F32)
        r2 = jnp.dot(hl, wr_ref[:, 0:LANES], preferred_element_type=F32)
        return r1[:, 0:LANES] + r1[:, LANES:] + r2 + br_ref[...]

    def top_k(logits):
        isg = lane < N_GROUPS
        lg = jnp.where(isg, logits, NEG)
        mg = jnp.max(lg, axis=1, keepdims=True)
        sg = jnp.sum(jnp.where(isg, jnp.exp(lg - mg), 0.0), axis=1, keepdims=True)
        p_g = 1.0 / sg
        gsel = jnp.min(jnp.where(isg & (lg == mg), lanef, big), axis=1, keepdims=True)
        lo = N_GROUPS + gsel * EXPERTS_PER_GROUP
        ise = (lanef >= lo) & (lanef < lo + EXPERTS_PER_GROUP)
        le = jnp.where(ise, logits, NEG)
        t1 = jnp.max(le, axis=1, keepdims=True)
        i1 = jnp.min(jnp.where(ise & (le == t1), lanef, big), axis=1, keepdims=True)
        ise2 = ise & (lanef != i1)
        le2 = jnp.where(ise2, logits, NEG)
        t2 = jnp.max(le2, axis=1, keepdims=True)
        i2 = jnp.min(jnp.where(ise2 & (le2 == t2), lanef, big), axis=1, keepdims=True)
        d = jnp.exp(t2 - t1)
        return i1 - N_GROUPS, i2 - N_GROUPS, p_g / (1.0 + d), p_g * d / (1.0 + d)

    def earlier_in_tile(e1, e2):
        both = jnp.where((lanef == e1) | (lanef == e2), 1.0, 0.0)
        row = lax.broadcasted_iota(jnp.int32, (TM, TM), 0)
        col = lax.broadcasted_iota(jnp.int32, (TM, TM), 1)
        before = jnp.dot((row > col).astype(BF16), both.astype(BF16), preferred_element_type=F32)
        return before, jnp.sum(both, axis=0, keepdims=True)

    mixes = [project(rows) for rows in tiles]
    splits = [normalise(rows, mix) for rows, mix in zip(tiles, mixes)]
    logits = [router_logits(hh, hl) for hh, hl in splits]
    picks = [top_k(lg) for lg in logits]
    befores = [earlier_in_tile(e1, e2) for e1, e2, _, _ in picks]
    counts = carry_ref[0:1, :]
    for rows, (e1, e2, w1, w2), (before, added) in zip(tiles, picks, befores):
        before = before + counts
        rank1 = jnp.sum(jnp.where(lanef == e1, before, 0.0), axis=1, keepdims=True)
        rank2 = jnp.sum(jnp.where(lanef == e2, before, 0.0), axis=1, keepdims=True)
        out = jnp.zeros((TM, LANES), F32)
        for j, v in enumerate((e1, e2, w1, w2, rank1, rank2)):
            out = jnp.where(lane == j, v, out)
        rt_ref[rows, :] = out
        counts = counts + added
    carry_ref[...] = jnp.broadcast_to(counts, carry_ref.shape)
    cnt_ref[...] = jnp.broadcast_to(counts, cnt_ref.shape)


def _outproj(x, od, of, gt, sc, sh, g, wo_bf, wr, br):
    tm = OUTPROJ_TILES * TM
    tpb = SEQ // tm
    row = pl.BlockSpec((tm, D_MODEL), lambda i: (i, 0))
    half = pl.BlockSpec((tm, DIFF_WIDTH), lambda i: (i, 0))
    per_batch = pl.BlockSpec((None, 1, D_MODEL), lambda i: (i // tpb, 0, 0))
    const = lambda shape: pl.BlockSpec(shape, lambda i: (0,) * len(shape))
    return pl.pallas_call(
        _outproj_kernel,
        grid=(N_TOK // tm,),
        in_specs=[row, half, half, per_batch, per_batch, per_batch, const((1, D_MODEL)),
                  const((D_MODEL, D_MODEL)), const((D_MODEL, 2 * LANES)), const((1, LANES))],
        out_specs=[row, pl.BlockSpec((N_PLANES, tm, LANES), lambda i: (0, i, 0)),
                   pl.BlockSpec((tm, LANES), lambda i: (i, 0)), const((8, LANES))],
        out_shape=[jax.ShapeDtypeStruct((N_TOK, D_MODEL), F32),
                   jax.ShapeDtypeStruct((N_PLANES, N_TOK, LANES), jnp.int32),
                   jax.ShapeDtypeStruct((N_TOK, LANES), F32),
                   jax.ShapeDtypeStruct((8, LANES), F32)],
        scratch_shapes=[pltpu.VMEM((8, LANES), F32)],
        compiler_params=_params("arbitrary"),
        name="outproj_router",
    )(x, od, of, gt, sc, sh, g.reshape(1, D_MODEL), wo_bf, wr, br)


def _expert_kernel(be_ref, cnt_ref, xs_ref, wg_ref, wu_ref, wd_ref, ys_ref, wg_sc, wu_sc, wd_sc):
    i = pl.program_id(0)
    cnt = cnt_ref[i]

    @pl.when((i == 0) | (be_ref[i] != be_ref[jnp.maximum(i - 1, 0)]))
    def _():
        wg_sc[...] = wg_ref[...].astype(BF16)
        wu_sc[...] = wu_ref[...].astype(BF16)
        wd_sc[...] = wd_ref[...].astype(BF16)

    @pl.when(cnt > 0)
    def _():
        live = lax.broadcasted_iota(jnp.int32, (MOE_BLOCK, LANES), 0) < cnt
        xb = _unpack_planes([jnp.where(live, xs_ref[p], 0) for p in range(N_PLANES)]).astype(BF16)
        a = jnp.dot(xb, wg_sc[...], preferred_element_type=F32)
        u = jnp.dot(xb, wu_sc[...], preferred_element_type=F32)
        hid = (a / (1.0 + jnp.exp(-a)) * u).astype(BF16)
        _pack_planes(jnp.dot(hid, wd_sc[...], preferred_element_type=F32), ys_ref)

    @pl.when(cnt == 0)
    def _():
        ys_ref[...] = jnp.zeros_like(ys_ref)


def _experts(layer, block_expert, block_count, xs, wg, wu, wd):
    planes = pl.BlockSpec((N_PLANES, MOE_BLOCK, LANES), lambda i, be, bc: (0, i, 0))
    w_in = pl.BlockSpec((None, None, D_MODEL, D_EXPERT), lambda i, be, bc: (layer, be[i], 0, 0))
    w_out = pl.BlockSpec((None, None, D_EXPERT, D_MODEL), lambda i, be, bc: (layer, be[i], 0, 0))
    grid_spec = pltpu.PrefetchScalarGridSpec(
        num_scalar_prefetch=2,
        grid=(MOE_NBLOCKS,),
        in_specs=[planes, w_in, w_in, w_out],
        out_specs=planes,
        scratch_shapes=[pltpu.VMEM((D_MODEL, D_EXPERT), BF16), pltpu.VMEM((D_MODEL, D_EXPERT), BF16),
                        pltpu.VMEM((D_EXPERT, D_MODEL), BF16)],
    )
    return pl.pallas_call(
        _expert_kernel,
        grid_spec=grid_spec,
        out_shape=jax.ShapeDtypeStruct((N_PLANES, PLANE_ROWS, LANES), jnp.int32),
        compiler_params=_params("arbitrary"),
        name="expert_mlp",
    )(block_expert, block_count, xs, wg, wu, wd)


def _slots(route, counts):
    counts = counts[0, :N_EXPERTS].astype(jnp.int32)
    padded = ((counts + MOE_BLOCK - 1) // MOE_BLOCK) * MOE_BLOCK
    pend = jnp.cumsum(padded)
    pstart = pend - padded
    bstart = jnp.arange(MOE_NBLOCKS, dtype=jnp.int32) * MOE_BLOCK
    block_expert = jnp.minimum(jnp.sum(bstart[:, None] >= pend[None, :], axis=1), N_EXPERTS - 1)
    block_expert = block_expert.astype(jnp.int32)
    mine = block_expert[:, None] == jnp.arange(N_EXPERTS, dtype=jnp.int32)[None, :]
    left = jnp.sum(jnp.where(mine, counts + pstart, 0), axis=1) - bstart
    block_count = jnp.clip(left, 0, MOE_BLOCK).astype(jnp.int32)
    base = jnp.pad(pstart.astype(F32), (0, LANES - N_EXPERTS)).reshape(1, LANES)
    return _slot_rows(route, base), block_expert, block_count


def _slot_rows_kernel(rt_ref, base_ref, o_ref):
    rt = rt_ref[...]
    lanef = _lane_iota(rt.shape).astype(F32)
    base = base_ref[...]
    dest = jnp.zeros(rt.shape, F32)
    for k in range(TOP_K):
        b = jnp.sum(jnp.where(lanef == rt[:, k:k + 1], base, 0.0), axis=1, keepdims=True)
        dest = jnp.where(lanef == k, b + rt[:, 2 * TOP_K + k:2 * TOP_K + k + 1], dest)
    dest_t = dest.T.astype(jnp.int32)
    for k in range(TOP_K):
        for p in range(N_PLANES):
            for c in range(rt.shape[0] // LANES):
                o_ref[k * N_PLANES + p, c:c + 1, :] = dest_t[k:k + 1, c * LANES:(c + 1) * LANES] + p * PLANE_ROWS


def _slot_rows(route, base):
    tm = 8 * LANES
    return pl.pallas_call(
        _slot_rows_kernel,
        grid=(N_TOK // tm,),
        in_specs=[pl.BlockSpec((tm, LANES), lambda i: (i, 0)), pl.BlockSpec((1, LANES), lambda i: (0, 0))],
        out_specs=pl.BlockSpec((TOP_K * N_PLANES, tm // LANES, LANES), lambda i: (0, i, 0)),
        out_shape=jax.ShapeDtypeStruct((TOP_K * N_PLANES, N_TOK // LANES, LANES), jnp.int32),
        compiler_params=_params("arbitrary"),
        name="slot_rows",
    )(route, base)


def _sc_workers():
    info = plsc.get_sparse_core_info()
    return info.num_cores, info.num_cores * info.num_subcores


def _sc_scatter2(src, idx, out_rows):
    n_win = src.shape[0] // SC_WINDOW
    nc, nw = _sc_workers()
    steps = n_win // nw
    mesh = plsc.VectorSubcoreMesh(core_axis_name="c", subcore_axis_name="s")

    @functools.partial(
        pl.kernel, mesh=mesh,
        out_type=jax.ShapeDtypeStruct((out_rows, LANES), src.dtype),
        scratch_types=[pltpu.VMEM((2 * steps, SC_WINDOW), jnp.int32),
                       pltpu.VMEM((SC_INFLIGHT, SC_WINDOW, LANES), src.dtype),
                       pltpu.SemaphoreType.DMA((SC_INFLIGHT,)), pltpu.SemaphoreType.DMA((SC_INFLIGHT,))],
        name="sc_dispatch_scatter",
    )
    def k(src_hbm, idx_hbm, out_hbm, idx_v, rows_v, lsem, wsem):
        first = (lax.axis_index("s") * nc + lax.axis_index("c")) * steps
        pltpu.sync_copy(idx_hbm.at[pl.ds(first, steps)], idx_v.at[pl.ds(0, steps)])
        pltpu.sync_copy(idx_hbm.at[pl.ds(n_win + first, steps)], idx_v.at[pl.ds(steps, steps)])

        @pl.loop(0, steps, step=SC_INFLIGHT)
        def _(j):
            loads = [pltpu.async_copy(src_hbm.at[pl.ds((first + j + b) * SC_WINDOW, SC_WINDOW)],
                                      rows_v.at[b], lsem.at[b]) for b in range(SC_INFLIGHT)]
            writes = []
            for b in range(SC_INFLIGHT):
                loads[b].wait()
                for half in range(TOP_K):
                    dst = out_hbm.at[idx_v.at[half * steps + j + b]]
                    writes.append(pltpu.async_copy(rows_v.at[b], dst, wsem.at[b]))
            for w in writes:
                w.wait()

    return k(src, idx)


def _sc_gather(table, idx):
    n_out = idx.shape[0] * SC_WINDOW
    nc, nw = _sc_workers()
    steps = n_out // nw // SC_WINDOW
    mesh = plsc.VectorSubcoreMesh(core_axis_name="c", subcore_axis_name="s")

    @functools.partial(
        pl.kernel, mesh=mesh,
        out_type=jax.ShapeDtypeStruct((n_out, LANES), table.dtype),
        scratch_types=[pltpu.VMEM((steps, SC_WINDOW), jnp.int32),
                       pltpu.VMEM((SC_INFLIGHT, SC_WINDOW, LANES), table.dtype),
                       pltpu.SemaphoreType.DMA((SC_INFLIGHT,)), pltpu.SemaphoreType.DMA((SC_INFLIGHT,))],
        name="sc_combine_gather",
    )
    def k(table_hbm, idx_hbm, out_hbm, idx_v, rows_v, gsem, wsem):
        first = (lax.axis_index("s") * nc + lax.axis_index("c")) * steps
        pltpu.sync_copy(idx_hbm.at[pl.ds(first, steps)], idx_v)

        @pl.loop(0, steps, step=SC_INFLIGHT)
        def _(j):
            gathers = [pltpu.async_copy(table_hbm.at[idx_v.at[j + b]], rows_v.at[b], gsem.at[b])
                       for b in range(SC_INFLIGHT)]
            writes = []
            for b in range(SC_INFLIGHT):
                gathers[b].wait()
                dst = out_hbm.at[pl.ds((first + j + b) * SC_WINDOW, SC_WINDOW)]
                writes.append(pltpu.async_copy(rows_v.at[b], dst, wsem.at[b]))
            for w in writes:
                w.wait()

    return k(table, idx)


def _final_kernel(x_ref, rt_ref, y_ref, gt_ref, g_ref, o_ref):
    x = x_ref[...] + gt_ref[...] * _combine(rt_ref, y_ref)
    ms = jnp.mean(x * x, axis=-1, keepdims=True)
    o_ref[...] = x * lax.rsqrt(ms + EPS) * g_ref[...]


def _final(x, moe, gate, g):
    tpb = SEQ // TM
    row = pl.BlockSpec((TM, D_MODEL), lambda i: (i, 0))
    return pl.pallas_call(
        _final_kernel,
        grid=(N_TOK // TM,),
        in_specs=[row, pl.BlockSpec((TM, LANES), lambda i: (i, 0)),
                  pl.BlockSpec((TOP_K, N_PLANES, TM, LANES), lambda i: (0, 0, i, 0)),
                  pl.BlockSpec((None, 1, D_MODEL), lambda i: (i // tpb, 0, 0)),
                  pl.BlockSpec((1, D_MODEL), lambda i: (0, 0))],
        out_specs=row,
        out_shape=jax.ShapeDtypeStruct((N_TOK, D_MODEL), F32),
        compiler_params=_params("arbitrary"),
        name="final_norm",
    )(x, moe[0], moe[1], gate, g.reshape(1, D_MODEL))


def kernel(x, c, positions, w_ada, b_ada, g_mix, w_in, b_forget, lambda_q1, lambda_k1, lambda_q2,
           lambda_k2, g_subln, g_fox_out, w_out, g_ffn, w_router_group, b_router_group,
           w_router_expert, b_router_expert, w_expert_gate, w_expert_up, w_expert_down, g_final):
    mod = _modulation(c, w_ada, b_ada)
    mod = mod.reshape(DEPTH, BATCH, 6, 1, D_MODEL)
    tables = _rope_tables(positions)
    pq = _forget_placement()
    xf = x.reshape(N_TOK, D_MODEL)
    moe = None
    gate = None
    for l in range(DEPTH):
        sh1, sc1, gt1, sh2, sc2, gt2 = (mod[l, :, j] for j in range(6))
        w_bf = jnp.pad(w_in[l], ((0, 0), (0, IN_COLS_PAD - IN_COLS))).astype(BF16)
        bfp = jnp.pad(b_forget[l], (0, LANES - N_FOX_HEADS)).reshape(1, LANES)
        xf, (dq, dk, dv, fq, fk, fv) = _inproj(xf, moe, gate, sc1, sh1, g_mix[l], w_bf, bfp, tables, pq)

        lambda_init = 0.8 - 0.6 * float(np.exp(-0.3 * l))
        lamv = jnp.zeros((8, LANES), F32).at[0:4, 0:HEAD_DIM].set(
            jnp.stack([lambda_q1[l], lambda_k1[l], lambda_q2[l], lambda_k2[l]]))
        g_d = g_subln[l].reshape(1, LANES)
        g_f = jnp.concatenate([g_fox_out[l], g_fox_out[l]]).reshape(1, LANES)
        od = _attention(True, lambda_init, dq, dk, dv, g_d, lamv)
        of = _attention(False, lambda_init, fq, fk, fv, g_f, lamv)

        wr32 = jnp.pad(jnp.concatenate([w_router_group[l], w_router_expert[l]], axis=1),
                       ((0, 0), (0, LANES - N_GROUPS - N_EXPERTS)))
        wr_hi = wr32.astype(BF16)
        wr_lo = (wr32 - wr_hi.astype(F32)).astype(BF16)
        wr = jnp.concatenate([wr_hi, wr_lo], axis=1)
        br = jnp.pad(jnp.concatenate([b_router_group[l], b_router_expert[l]]),
                     (0, LANES - N_GROUPS - N_EXPERTS)).reshape(1, LANES)
        xf, h2, route, counts = _outproj(xf, od, of, gt1, sc2, sh2, g_ffn[l], w_out[l].astype(BF16),
                                         wr, br)

        rows, block_expert, block_count = _slots(route, counts)
        rows = rows.reshape(TOP_K * N_PLANES * N_TOK // SC_WINDOW, SC_WINDOW)
        xs = _sc_scatter2(h2.reshape(N_PLANES * N_TOK, LANES), rows, N_PLANES * PLANE_ROWS)
        ys = _experts(l, block_expert, block_count, xs.reshape(N_PLANES, PLANE_ROWS, LANES),
                      w_expert_gate, w_expert_up, w_expert_down)
        y2 = _sc_gather(ys.reshape(N_PLANES * PLANE_ROWS, LANES), rows)
        moe = (route, y2.reshape(TOP_K, N_PLANES, N_TOK, LANES))
        gate = gt2
    out = _final(xf, moe, gate, g_final)
    return out.reshape(BATCH, SEQ, D_MODEL)
```

```python
import functools

import numpy as np
import jax
import jax.numpy as jnp
from jax import lax
from jax.experimental import pallas as pl
from jax.experimental.pallas import tpu as pltpu
from jax.experimental.pallas import tpu_sc as plsc

D_MODEL = 1024
BATCH = 4
SEQ = 4096
DEPTH = 4
N_TOK = BATCH * SEQ

CHUNK = 64
HEAD_DIM = 64
N_DIFF_HEADS = 4
N_FOX_HEADS = 8
DIFF_WIDTH = 512
FOX_WIDTH = 512
IN_COLS = 3 * DIFF_WIDTH + 3 * FOX_WIDTH + N_FOX_HEADS
ROT_DIM = 16
ROPE_THETA = 500000.0
N_GROUPS = 4
EXPERTS_PER_GROUP = 8
N_EXPERTS = 32
TOP_K = 2
D_EXPERT = 512
EPS = 1e-6

LANES = 128
IN_COLS_PAD = 3200
FF_COL = 3 * DIFF_WIDTH + 3 * FOX_WIDTH
QK_WIDTH = 8 * LANES
TM = 512
OUTPROJ_TILES = 2
TQ = 512
ATTN_TQ = 256
N_LATE_CHAINS = 1
ONES_ROWS = 16
ATTN_ROWS = 64
LOG2E = 1.4426950408889634
MOE_BLOCK = 512
MOE_ROWS = N_TOK * TOP_K + N_EXPERTS * MOE_BLOCK
MOE_NBLOCKS = MOE_ROWS // MOE_BLOCK
PLANE_ROWS = MOE_ROWS
N_PLANES = D_MODEL // 2 // LANES
SC_WINDOW = 128
SC_INFLIGHT = 4
NEG = -1e30
VMEM_LIMIT = 56 * 1024 * 1024

F32 = jnp.float32
BF16 = jnp.bfloat16


def _bf16_round(x):
    return x.astype(BF16).astype(F32)


def _lane_iota(shape):
    return lax.broadcasted_iota(jnp.int32, shape, 1)


def _params(*sem):
    return pltpu.CompilerParams(dimension_semantics=sem, vmem_limit_bytes=VMEM_LIMIT)


def _pack_planes(y, o_ref, rows=slice(None)):
    bits = lax.bitcast_convert_type(_bf16_round(y), jnp.uint32)
    half = D_MODEL // 2
    word = bits[:, half:] | lax.shift_right_logical(bits[:, :half], jnp.uint32(16))
    word = lax.bitcast_convert_type(word, jnp.int32)
    for p in range(N_PLANES):
        o_ref[p, rows, :] = word[:, p * LANES:(p + 1) * LANES]


def _unpack_planes(planes):
    lo, hi = [], []
    for w in planes:
        u = lax.bitcast_convert_type(w, jnp.uint32)
        lo.append(lax.bitcast_convert_type(lax.shift_left(u, jnp.uint32(16)), F32))
        hi.append(lax.bitcast_convert_type(u & jnp.uint32(0xFFFF0000), F32))
    return jnp.concatenate(lo + hi, axis=1)


def _combine(route_ref, y_ref):
    rt = route_ref[...]
    y0 = _unpack_planes([y_ref[0, p] for p in range(N_PLANES)])
    y1 = _unpack_planes([y_ref[1, p] for p in range(N_PLANES)])
    return rt[:, 2:3] * y0 + rt[:, 3:4] * y1


def _mod_kernel(c_ref, w_ref, b_ref, o_ref):
    c = c_ref[...]
    cond = c / (1.0 + jnp.exp(-c))
    ch = cond.astype(BF16)
    cl = (cond - ch.astype(F32)).astype(BF16)
    w = w_ref[...]
    wh = w.astype(BF16)
    wl = (w - wh.astype(F32)).astype(BF16)
    acc = jnp.dot(ch, wh, preferred_element_type=F32)
    acc += jnp.dot(cl, wh, preferred_element_type=F32)
    acc += jnp.dot(ch, wl, preferred_element_type=F32)
    o_ref[...] = acc + b_ref[...]


def _modulation(c, w_ada, b_ada):
    rows = 16
    tn = 1536
    c_pad = jnp.zeros((rows, D_MODEL), F32).at[:BATCH].set(c)
    out = pl.pallas_call(
        _mod_kernel,
        grid=(DEPTH, 6 * D_MODEL // tn),
        in_specs=[
            pl.BlockSpec((rows, D_MODEL), lambda l, n: (0, 0)),
            pl.BlockSpec((None, D_MODEL, tn), lambda l, n: (l, 0, n)),
            pl.BlockSpec((None, 1, tn), lambda l, n: (l, 0, n)),
        ],
        out_specs=pl.BlockSpec((None, rows, tn), lambda l, n: (l, 0, n)),
        out_shape=jax.ShapeDtypeStruct((DEPTH, rows, 6 * D_MODEL), F32),
        compiler_params=_params("arbitrary", "arbitrary"),
        name="adaln_mod",
    )(c_pad, w_ada, b_ada.reshape(DEPTH, 1, 6 * D_MODEL))
    return out[:, :BATCH]


def _rope_kernel(pos_ref, inv_ref, c_ref, sa_ref, sb_ref):
    ang = pos_ref[...].astype(F32) * inv_ref[...]
    j = _lane_iota(ang.shape) % HEAD_DIM
    cosv = jnp.cos(ang)
    sinv = jnp.sin(ang)
    half = ROT_DIM // 2
    c_ref[...] = jnp.where(j < ROT_DIM, cosv, 1.0)
    sa_ref[...] = jnp.where(j < half, -sinv, 0.0)
    sb_ref[...] = jnp.where((j >= half) & (j < ROT_DIM), sinv, 0.0)


def _rope_tables(positions):
    half = ROT_DIM // 2
    inv = ROPE_THETA ** (-jnp.arange(0, ROT_DIM, 2, dtype=F32) / ROT_DIM)
    lane = np.arange(LANES)
    inv_lane = inv[(lane % HEAD_DIM) % half].reshape(1, LANES)
    spec = pl.BlockSpec((TM, LANES), lambda i: (i, 0))
    shape = jax.ShapeDtypeStruct((N_TOK, LANES), F32)
    return pl.pallas_call(
        _rope_kernel,
        grid=(N_TOK // TM,),
        in_specs=[pl.BlockSpec((TM, 1), lambda i: (i, 0)),
                  pl.BlockSpec((1, LANES), lambda i: (0, 0))],
        out_specs=[spec, spec, spec],
        out_shape=[shape, shape, shape],
        compiler_params=_params("arbitrary"),
        name="rope_tables",
    )(positions.reshape(N_TOK, 1), inv_lane)


def _rms_mod(x, g, sc, sh):
    ms = jnp.mean(x * x, axis=-1, keepdims=True)
    return (x * lax.rsqrt(ms + EPS) * g) * (1.0 + sc) + sh


def _inproj_kernel(fuse, *refs):
    if fuse:
        (x_ref, rt_ref, y_ref, gt_ref, sc_ref, sh_ref, g_ref, w_ref, bf_ref, c_ref, sa_ref, sb_ref,
         pq_ref, xo_ref, dq_ref, dk_ref, dv_ref, fq_ref, fk_ref, fv_ref, carry_ref) = refs
        x = x_ref[...] + gt_ref[...] * _combine(rt_ref, y_ref)
        xo_ref[...] = x
    else:
        (x_ref, sc_ref, sh_ref, g_ref, w_ref, bf_ref, c_ref, sa_ref, sb_ref,
         pq_ref, dq_ref, dk_ref, dv_ref, fq_ref, fk_ref, fv_ref, carry_ref) = refs
        x = x_ref[...]
    hb = _rms_mod(x, g_ref[...], sc_ref[...], sh_ref[...]).astype(BF16)

    @pl.when(pl.program_id(0) % (SEQ // TM) == 0)
    def _():
        carry_ref[...] = jnp.zeros_like(carry_ref)

    lane = _lane_iota((TM, LANES))
    nh = N_FOX_HEADS

    def pack3(a):
        hi = _bf16_round(a)
        r1 = a - hi
        mid = _bf16_round(r1)
        lo = _bf16_round(r1 - mid)
        return jnp.where(lane < nh, hi,
                         jnp.where(lane < 2 * nh, pltpu.roll(mid, nh, 1),
                                   jnp.where(lane < 3 * nh, pltpu.roll(lo, 2 * nh, 1), 0.0)))

    z = jnp.dot(hb, w_ref[:, FF_COL:FF_COL + LANES], preferred_element_type=F32) + bf_ref[...]

    low = lane < HEAD_DIM
    rc, rsa, rsb = c_ref[...], sa_ref[...], sb_ref[...]
    scale = HEAD_DIM ** -0.5 * LOG2E

    def split_store(chunk, o_ref, m, extra_a=None, extra_b=None):
        a = jnp.where(low, chunk, 0.0)
        b = jnp.where(low, pltpu.roll(chunk, HEAD_DIM, 1), 0.0)
        if extra_a is not None:
            a = a + extra_a
            b = b + extra_b
        o_ref[:, (2 * m) * LANES:(2 * m + 1) * LANES] = a.astype(BF16)
        o_ref[:, (2 * m + 1) * LANES:(2 * m + 2) * LANES] = b.astype(BF16)

    def rope(xc):
        return xc * rc + pltpu.roll(xc, LANES - ROT_DIM // 2, 1) * rsa + pltpu.roll(xc, ROT_DIM // 2, 1) * rsb

    pdq = jnp.dot(hb, w_ref[:, 0:DIFF_WIDTH], preferred_element_type=F32)
    for m in range(N_DIFF_HEADS):
        split_store(rope(pdq[:, m * LANES:(m + 1) * LANES]) * scale, dq_ref, m)
    pdk = jnp.dot(hb, w_ref[:, DIFF_WIDTH:2 * DIFF_WIDTH], preferred_element_type=F32)
    for m in range(N_DIFF_HEADS):
        split_store(rope(pdk[:, m * LANES:(m + 1) * LANES]), dk_ref, m)
    def store_values_t(pv, o_ref, width):
        ones = jnp.ones((ONES_ROWS, TM), BF16)
        for m in range(4):
            vt = pv[:, m * LANES:(m + 1) * LANES].T.astype(BF16)
            for i in range(LANES // width):
                o_ref[m * (LANES // width) + i, 0:width, :] = vt[i * width:(i + 1) * width]
                o_ref[m * (LANES // width) + i, width:width + ONES_ROWS, :] = ones

    store_values_t(jnp.dot(hb, w_ref[:, 2 * DIFF_WIDTH:3 * DIFF_WIDTH], preferred_element_type=F32),
                   dv_ref, 2 * HEAD_DIM)
    o = 3 * DIFF_WIDTH
    store_values_t(jnp.dot(hb, w_ref[:, o + 2 * FOX_WIDTH:o + 3 * FOX_WIDTH],
                           preferred_element_type=F32), fv_ref, HEAD_DIM)

    logf =jnp.minimum(z, 0.0) - jnp.log(1.0 + jnp.exp(-jnp.abs(z)))
    logf = jnp.where(lane < nh, logf, 0.0)
    row = lax.broadcasted_iota(jnp.int32, (TM, TM), 0)
    col = lax.broadcasted_iota(jnp.int32, (TM, TM), 1)
    tri = (row >= col).astype(BF16)
    r = jnp.dot(tri, pack3(logf).astype(BF16), preferred_element_type=F32)
    cs = r + pltpu.roll(r, LANES - nh, 1) + pltpu.roll(r, LANES - 2 * nh, 1)
    cf = jnp.where(lane < nh, cs + carry_ref[0:1, :], 0.0)
    carry_ref[...] = jnp.broadcast_to(cf[TM - 1:TM, :], carry_ref.shape)

    t3 = jnp.where(lane == 3 * nh, 1.0, pack3(cf * LOG2E)).astype(BF16)
    aug = jnp.dot(t3, pq_ref[...], preferred_element_type=F32)

    pfq =jnp.dot(hb, w_ref[:, o:o + FOX_WIDTH], preferred_element_type=F32)
    for m in range(N_FOX_HEADS // 2):
        split_store(pfq[:, m * LANES:(m + 1) * LANES] * scale, fq_ref, m,
                    aug[:, (2 * m) * LANES:(2 * m + 1) * LANES],
                    aug[:, (2 * m + 1) * LANES:(2 * m + 2) * LANES])
    pfk = jnp.dot(hb, w_ref[:, o + FOX_WIDTH:o + 2 * FOX_WIDTH], preferred_element_type=F32)
    for m in range(N_FOX_HEADS // 2):
        split_store(pfk[:, m * LANES:(m + 1) * LANES], fk_ref, m,
                    aug[:, QK_WIDTH + (2 * m) * LANES:QK_WIDTH + (2 * m + 1) * LANES],
                    aug[:, QK_WIDTH + (2 * m + 1) * LANES:QK_WIDTH + (2 * m + 2) * LANES])


def _forget_placement():
    nh = N_FOX_HEADS
    p = np.zeros((LANES, 2 * QK_WIDTH), np.float32)
    for h in range(nh):
        base_q = h * LANES + HEAD_DIM
        base_k = QK_WIDTH + h * LANES + HEAD_DIM
        for part in range(3):
            p[part * nh + h, base_q + part] = 1.0
            p[3 * nh, base_q + 3 + part] = 1.0
            p[3 * nh, base_k + part] = 1.0
            p[part * nh + h, base_k + 3 + part] = -1.0
    return jnp.asarray(p, BF16)


def _inproj(x, moe, gate, sc, sh, g, w_bf, b_forget, tables, pq):
    fuse = moe is not None
    tpb = SEQ // TM
    row = pl.BlockSpec((TM, D_MODEL), lambda i: (i, 0))
    per_batch = pl.BlockSpec((None, 1, D_MODEL), lambda i: (i // tpb, 0, 0))
    const = lambda shape: pl.BlockSpec(shape, lambda i: (0,) * len(shape))
    tab = pl.BlockSpec((TM, LANES), lambda i: (i, 0))
    in_specs = [row]
    args = [x]
    if fuse:
        in_specs += [tab, pl.BlockSpec((TOP_K, N_PLANES, TM, LANES), lambda i: (0, 0, i, 0)), per_batch]
        args += [moe[0], moe[1], gate]
    layer, w_all = w_bf
    in_specs += [per_batch, per_batch, const((1, D_MODEL)),
                 pl.BlockSpec((None, D_MODEL, IN_COLS_PAD), lambda i: (layer, 0, 0)),
                 const((1, LANES)), tab, tab, tab, const((LANES, 2 * QK_WIDTH))]
    args += [sc, sh, g.reshape(1, D_MODEL), w_all, b_forget, *tables, pq]
    wide = pl.BlockSpec((TM, QK_WIDTH), lambda i: (i, 0))
    def vspec(heads, width):
        rows = width + ONES_ROWS
        return (pl.BlockSpec((None, heads, None, rows, TM), lambda i: (i // tpb, 0, i % tpb, 0, 0)),
                jax.ShapeDtypeStruct((BATCH, heads, tpb, rows, TM), BF16))

    wide_s = jax.ShapeDtypeStruct((N_TOK, QK_WIDTH), BF16)
    dv_spec, dv_s = vspec(N_DIFF_HEADS, 2 * HEAD_DIM)
    fv_spec, fv_s = vspec(N_FOX_HEADS, HEAD_DIM)
    out_specs = [wide, wide, dv_spec, wide, wide, fv_spec]
    out_shape = [wide_s, wide_s, dv_s, wide_s, wide_s, fv_s]
    if fuse:
        out_specs = [row] + out_specs
        out_shape = [jax.ShapeDtypeStruct((N_TOK, D_MODEL), F32)] + out_shape
    outs = pl.pallas_call(
        functools.partial(_inproj_kernel, fuse),
        grid=(N_TOK // TM,),
        in_specs=in_specs,
        out_specs=out_specs,
        out_shape=out_shape,
        scratch_shapes=[pltpu.VMEM((8, LANES), F32)],
        compiler_params=_params("arbitrary"),
        name="norm_inproj",
    )(*args)
    if fuse:
        return outs[0], outs[1:]
    return x, outs


def _attn_kernel(diff, lambda_init, qa_ref, qb_ref, ka_ref, kb_ref, v_ref, g_ref, lam_ref, o_ref,
                 *scratch):
    nq = SEQ // TQ
    n_half = TQ // ATTN_TQ
    feat = 2 * HEAD_DIM if diff else HEAD_DIM
    chains = []
    for mi, (q_ref, k_ref) in enumerate(((qa_ref, ka_ref), (qb_ref, kb_ref))):
        for h in range(n_half):
            c = mi * n_half + h
            qt_sc, s_sc, p_sc, m_sc, a_sc, acc_sc = scratch[c::2 * n_half]
            vh = 0 if diff else mi
            chains.append((h, k_ref, qt_sc, s_sc, p_sc, m_sc, a_sc, acc_sc, q_ref, vh))
    order = [chains[mi * n_half + h] for h in range(n_half) for mi in range(2)]
    early, late = order[:-N_LATE_CHAINS], order[-N_LATE_CHAINS:]

    def load_queries(qi):
        for mi in range(2):
            q_ref = chains[mi * n_half][8]
            qt = q_ref[pl.ds(pl.multiple_of(qi * TQ, TQ), TQ), :].astype(F32).T.astype(BF16)
            for h in range(n_half):
                chains[mi * n_half + h][2][...] = qt[:, h * ATTN_TQ:(h + 1) * ATTN_TQ]

    def reset_state():
        for chain in chains:
            m_sc, _, acc_sc = chain[5:8]
            m_sc[...] = jnp.full(m_sc.shape, NEG, F32)
            acc_sc[...] = jnp.zeros(acc_sc.shape, F32)

    def n_keys(chain, masked):
        return (chain[0] + 1) * ATTN_TQ if masked else TQ

    def scores(chain, j, masked):
        h, k_ref, qt_sc, s_sc = chain[:4]
        nk = n_keys(chain, masked)
        off = pl.multiple_of(j * TQ, TQ)
        s = jnp.dot(k_ref[pl.ds(off, nk), :], qt_sc[...], preferred_element_type=F32)
        if masked:
            kk = lax.broadcasted_iota(jnp.int32, (nk, ATTN_TQ), 0)
            qq = h * ATTN_TQ + lax.broadcasted_iota(jnp.int32, (nk, ATTN_TQ), 1)
            s = jnp.where((kk // CHUNK <= qq // CHUNK) if diff else (kk <= qq), s, NEG)
        s_sc[0:nk, :] = s

    def softmax(chain, masked):
        s_sc, p_sc, m_sc, a_sc = chain[3:7]
        nk = n_keys(chain, masked)
        m_all = m_sc[...]
        m_parts = []
        for c0 in range(0, ATTN_TQ, LANES):
            cols = slice(c0, c0 + LANES)
            pm = s_sc[0:ATTN_ROWS, cols]
            for r0 in range(ATTN_ROWS, nk, ATTN_ROWS):
                pm = jnp.maximum(pm, s_sc[r0:r0 + ATTN_ROWS, cols])
            m_new = jnp.maximum(m_all[:, cols], jnp.max(pm, axis=0, keepdims=True))
            for r0 in range(0, nk, ATTN_ROWS):
                p = jnp.exp2(s_sc[r0:r0 + ATTN_ROWS, cols] - m_new)
                p_sc[r0:r0 + ATTN_ROWS, cols] = p.astype(BF16)
            m_parts.append(m_new)
        m_new = jnp.concatenate(m_parts, axis=1)
        a_sc[...] = jnp.exp2(m_all - m_new)
        m_sc[...] = m_new

    def values(chain, j, masked=False):
        p_sc, a_sc, acc_sc, vh = chain[4], chain[6], chain[7], chain[9]
        nk = n_keys(chain, masked)
        pv = jnp.dot(v_ref[vh, j, :, 0:nk], p_sc[0:nk, :], preferred_element_type=F32)
        acc_sc[...] = a_sc[...] * acc_sc[...] + pv

    def idle_late():
        for chain in late:
            chain[4][...] = jnp.zeros(chain[4].shape, BF16)
            chain[6][...] = jnp.ones(chain[6].shape, F32)

    def consume(j, cur_masked=False, nxt=None, nxt_masked=False, final=False, before_next=None):
        def open_late(chain):
            scores(chain, j, cur_masked)
            values(chain, jnp.maximum(j - 1, 0))

        open_late(late[0])
        for i, chain in enumerate(early):
            softmax(chain, cur_masked)
            if i == 0:
                for other in late[1:]:
                    open_late(other)
                if before_next is not None:
                    before_next()
            if nxt is not None:
                scores(chain, nxt, nxt_masked)
            values(chain, j, cur_masked)
        for chain in late:
            softmax(chain, cur_masked)
        if final:
            for chain in late:
                values(chain, j, cur_masked)

    def finalize(qi):
        ot = [jnp.concatenate([chains[mi * n_half + h][7][0:feat] / chains[mi * n_half + h][7][feat:feat + 1]
                               for h in range(n_half)], axis=1) for mi in range(2)]
        g = g_ref[...]
        rows = pl.ds(pl.multiple_of(qi * TQ, TQ), TQ)
        if diff:
            lv = lam_ref[...]
            lam = (jnp.exp(jnp.sum(lv[0:1] * lv[1:2], axis=1, keepdims=True))
                   - jnp.exp(jnp.sum(lv[2:3] * lv[3:4], axis=1, keepdims=True)) + lambda_init)
            o = (ot[0] - lam * ot[1]).T
            y = o * lax.rsqrt(jnp.mean(o * o, axis=1, keepdims=True) + EPS) * g
            o_ref[rows, :] = (y * (1.0 - lambda_init)).astype(o_ref.dtype)
        else:
            o = jnp.concatenate(ot, axis=0).T
            low = _lane_iota((TQ, LANES)) < HEAD_DIM
            sq = o * o
            msa = jnp.sum(jnp.where(low, sq, 0.0), axis=1, keepdims=True) / HEAD_DIM
            msb = jnp.sum(jnp.where(low, 0.0, sq), axis=1, keepdims=True) / HEAD_DIM
            inv = jnp.where(low, lax.rsqrt(msa + EPS), lax.rsqrt(msb + EPS))
            o_ref[rows, :] = (o * inv * g).astype(o_ref.dtype)

    load_queries(0)
    reset_state()
    idle_late()
    for chain in early:
        scores(chain, 0, True)

    @pl.loop(0, nq)
    def _(qi):
        n_plain = jnp.maximum(qi - 1, 0)

        def run(first, count):
            for i in range(count):
                consume(first + i, nxt=first + i + 1)

        @pl.loop(0, n_plain // 4)
        def _(t):
            run(4 * t, 4)

        done = (n_plain // 4) * 4

        @pl.when(n_plain - done >= 2)
        def _():
            run(done, 2)

        def last_blocks(to_next_tile):
            if to_next_tile:
                consume(qi, cur_masked=True, nxt=0, final=True, before_next=lambda: load_queries(qi + 1))
            else:
                consume(qi, cur_masked=True, final=True)
            finalize(qi)

        for to_next_tile in (True, False):
            more = (qi < nq - 1) if to_next_tile else (qi == nq - 1)

            last_tile_odd = (nq - 2) % 2 == 1
            for odd in ((True, False) if to_next_tile else (last_tile_odd,)):
                @pl.when(more & (qi > 0) & ((n_plain % 2 == 1) == odd))
                def _():
                    if odd:
                        run(qi - 2, 1)
                    consume(qi - 1, nxt=qi, nxt_masked=True)
                    last_blocks(to_next_tile)

            if to_next_tile:
                @pl.when(qi == 0)
                def _():
                    last_blocks(to_next_tile)

        reset_state()
        idle_late()


def _attention(diff, lambda_init, q, k, v, g, lamv):
    nq = SEQ // TQ
    kspec = lambda par: pl.BlockSpec((SEQ, LANES), lambda b, p: (b, 2 * p + par))
    return pl.pallas_call(
        functools.partial(_attn_kernel, diff, lambda_init),
        grid=(BATCH, 4),
        in_specs=[kspec(0), kspec(1), kspec(0), kspec(1),
                  pl.BlockSpec((None, v.shape[1] // 4, nq, v.shape[3], TQ), lambda b, p: (b, p, 0, 0, 0)),
                  pl.BlockSpec((1, LANES), lambda b, p: (0, 0)),
                  pl.BlockSpec((8, LANES), lambda b, p: (0, 0))],
        out_specs=pl.BlockSpec((SEQ, LANES), lambda b, p: (b, p)),
        out_shape=jax.ShapeDtypeStruct((N_TOK, DIFF_WIDTH), BF16),
        scratch_shapes=[pltpu.VMEM(shape, dt)
                        for shape, dt in (((LANES, ATTN_TQ), BF16), ((TQ, ATTN_TQ), F32),
                                          ((TQ, ATTN_TQ), BF16), ((1, ATTN_TQ), F32),
                                          ((1, ATTN_TQ), F32), ((v.shape[3], ATTN_TQ), F32))
                        for _ in range(2 * TQ // ATTN_TQ)],
        compiler_params=_params("arbitrary", "arbitrary"),
        name="diff_attention" if diff else "fox_attention",
    )(q, q, k, k, v, g, lamv)


def _outproj_kernel(x_ref, od_ref, of_ref, gt_ref, sc_ref, sh_ref, g_ref, wo_ref, wr_ref, br_ref,
                    x1_ref, h2_ref, rt_ref, cnt_ref, carry_ref):
    @pl.when(pl.program_id(0) == 0)
    def _():
        carry_ref[...] = jnp.zeros_like(carry_ref)

    tiles = [slice(t * TM, (t + 1) * TM) for t in range(OUTPROJ_TILES)]
    lane = _lane_iota((TM, LANES))
    lanef = lane.astype(F32)
    big = float(LANES)

    def project(rows):
        mix = jnp.dot(od_ref[rows, :], wo_ref[0:DIFF_WIDTH, :], preferred_element_type=F32)
        return mix + jnp.dot(of_ref[rows, :], wo_ref[DIFF_WIDTH:, :], preferred_element_type=F32)

    def normalise(rows, mix):
        x1 = x_ref[rows, :] + gt_ref[...] * mix
        x1_ref[rows, :] = x1
        h = _rms_mod(x1, g_ref[...], sc_ref[...], sh_ref[...])
        hh = h.astype(BF16)
        _pack_planes(h, h2_ref, rows)
        return hh, (h - hh.astype(F32)).astype(BF16)

    def router_logits(hh, hl):
        r1 = jnp.dot(hh, wr_ref[...], preferred_element_type=F32)
        r2 = jnp.dot(hl, wr_ref[:, 0:LANES], preferred_element_type=F32)
        return r1[:, 0:LANES] + r1[:, LANES:] + r2 + br_ref[...]

    def top_k(logits):
        isg = lane < N_GROUPS
        lg = jnp.where(isg, logits, NEG)
        mg = jnp.max(lg, axis=1, keepdims=True)
        sg = jnp.sum(jnp.where(isg, jnp.exp(lg - mg), 0.0), axis=1, keepdims=True)
        p_g = 1.0 / sg
        gsel = jnp.min(jnp.where(isg & (lg == mg), lanef, big), axis=1, keepdims=True)
        lo = N_GROUPS + gsel * EXPERTS_PER_GROUP
        ise = (lanef >= lo) & (lanef < lo + EXPERTS_PER_GROUP)
        le = jnp.where(ise, logits, NEG)
        t1 = jnp.max(le, axis=1, keepdims=True)
        i1 = jnp.min(jnp.where(ise & (le == t1), lanef, big), axis=1, keepdims=True)
        ise2 = ise & (lanef != i1)
        le2 = jnp.where(ise2, logits, NEG)
        t2 = jnp.max(le2, axis=1, keepdims=True)
        i2 = jnp.min(jnp.where(ise2 & (le2 == t2), lanef, big), axis=1, keepdims=True)
        d = jnp.exp(t2 - t1)
        return i1 - N_GROUPS, i2 - N_GROUPS, p_g / (1.0 + d), p_g * d / (1.0 + d)

    def earlier_in_tile(e1, e2):
        both = jnp.where((lanef == e1) | (lanef == e2), 1.0, 0.0)
        row = lax.broadcasted_iota(jnp.int32, (TM, TM), 0)
        col = lax.broadcasted_iota(jnp.int32, (TM, TM), 1)
        before = jnp.dot((row > col).astype(BF16), both.astype(BF16), preferred_element_type=F32)
        return before, jnp.sum(both, axis=0, keepdims=True)

    mixes = [project(rows) for rows in tiles]
    splits = [normalise(rows, mix) for rows, mix in zip(tiles, mixes)]
    logits = [router_logits(hh, hl) for hh, hl in splits]
    picks = [top_k(lg) for lg in logits]
    befores = [earlier_in_tile(e1, e2) for e1, e2, _, _ in picks]
    counts = carry_ref[0:1, :]
    for rows, (e1, e2, w1, w2), (before, added) in zip(tiles, picks, befores):
        before = before + counts
        rank1 = jnp.sum(jnp.where(lanef == e1, before, 0.0), axis=1, keepdims=True)
        rank2 = jnp.sum(jnp.where(lanef == e2, before, 0.0), axis=1, keepdims=True)
        out = jnp.zeros((TM, LANES), F32)
        for j, v in enumerate((e1, e2, w1, w2, rank1, rank2)):
            out = jnp.where(lane == j, v, out)
        rt_ref[rows, :] = out
        counts = counts + added
    carry_ref[...] = jnp.broadcast_to(counts, carry_ref.shape)
    cnt_ref[...] = jnp.broadcast_to(counts, cnt_ref.shape)


def _outproj(x, od, of, gt, sc, sh, g, wo_bf, wr, br):
    tm = OUTPROJ_TILES * TM
    tpb = SEQ // tm
    row = pl.BlockSpec((tm, D_MODEL), lambda i: (i, 0))
    half = pl.BlockSpec((tm, DIFF_WIDTH), lambda i: (i, 0))
    per_batch = pl.BlockSpec((None, 1, D_MODEL), lambda i: (i // tpb, 0, 0))
    const = lambda shape: pl.BlockSpec(shape, lambda i: (0,) * len(shape))
    return pl.pallas_call(
        _outproj_kernel,
        grid=(N_TOK // tm,),
        in_specs=[row, half, half, per_batch, per_batch, per_batch, const((1, D_MODEL)),
                  const((D_MODEL, D_MODEL)), const((D_MODEL, 2 * LANES)), const((1, LANES))],
        out_specs=[row, pl.BlockSpec((N_PLANES, tm, LANES), lambda i: (0, i, 0)),
                   pl.BlockSpec((tm, LANES), lambda i: (i, 0)), const((8, LANES))],
        out_shape=[jax.ShapeDtypeStruct((N_TOK, D_MODEL), F32),
                   jax.ShapeDtypeStruct((N_PLANES, N_TOK, LANES), jnp.int32),
                   jax.ShapeDtypeStruct((N_TOK, LANES), F32),
                   jax.ShapeDtypeStruct((8, LANES), F32)],
        scratch_shapes=[pltpu.VMEM((8, LANES), F32)],
        compiler_params=_params("arbitrary"),
        name="outproj_router",
    )(x, od, of, gt, sc, sh, g.reshape(1, D_MODEL), wo_bf, wr, br)


def _expert_kernel(be_ref, cnt_ref, xs_ref, wg_ref, wu_ref, wd_ref, ys_ref, wg_sc, wu_sc, wd_sc):
    i = pl.program_id(0)
    cnt = cnt_ref[i]

    @pl.when((i == 0) | (be_ref[i] != be_ref[jnp.maximum(i - 1, 0)]))
    def _():
        wg_sc[...] = wg_ref[...].astype(BF16)
        wu_sc[...] = wu_ref[...].astype(BF16)
        wd_sc[...] = wd_ref[...].astype(BF16)

    def mlp(n_rows):
        rows = slice(0, n_rows)
        live = lax.broadcasted_iota(jnp.int32, (n_rows, LANES), 0) < cnt
        xb = _unpack_planes([jnp.where(live, xs_ref[p, rows, :], 0) for p in range(N_PLANES)]).astype(BF16)
        a = jnp.dot(xb, wg_sc[...], preferred_element_type=F32)
        u = jnp.dot(xb, wu_sc[...], preferred_element_type=F32)
        hid = (a / (1.0 + jnp.exp(-a)) * u).astype(BF16)
        _pack_planes(jnp.dot(hid, wd_sc[...], preferred_element_type=F32), ys_ref, rows)
        if n_rows < MOE_BLOCK:
            ys_ref[:, n_rows:, :] = jnp.zeros((N_PLANES, MOE_BLOCK - n_rows, LANES), ys_ref.dtype)

    half = MOE_BLOCK // 2

    @pl.when(cnt > half)
    def _():
        mlp(MOE_BLOCK)

    @pl.when((cnt > 0) & (cnt <= half))
    def _():
        mlp(half)

    @pl.when(cnt == 0)
    def _():
        ys_ref[...] = jnp.zeros_like(ys_ref)


def _experts(layer, block_expert, block_count, xs, wg, wu, wd):
    planes = pl.BlockSpec((N_PLANES, MOE_BLOCK, LANES), lambda i, be, bc: (0, i, 0))
    w_in = pl.BlockSpec((None, None, D_MODEL, D_EXPERT), lambda i, be, bc: (layer, be[i], 0, 0))
    w_out = pl.BlockSpec((None, None, D_EXPERT, D_MODEL), lambda i, be, bc: (layer, be[i], 0, 0))
    grid_spec = pltpu.PrefetchScalarGridSpec(
        num_scalar_prefetch=2,
        grid=(MOE_NBLOCKS,),
        in_specs=[planes, w_in, w_in, w_out],
        out_specs=planes,
        scratch_shapes=[pltpu.VMEM((D_MODEL, D_EXPERT), BF16), pltpu.VMEM((D_MODEL, D_EXPERT), BF16),
                        pltpu.VMEM((D_EXPERT, D_MODEL), BF16)],
    )
    return pl.pallas_call(
        _expert_kernel,
        grid_spec=grid_spec,
        out_shape=jax.ShapeDtypeStruct((N_PLANES, PLANE_ROWS, LANES), jnp.int32),
        compiler_params=_params("arbitrary"),
        name="expert_mlp",
    )(block_expert, block_count, xs, wg, wu, wd)


def _slots(route, counts):
    counts = counts[0, :N_EXPERTS].astype(jnp.int32)
    padded = ((counts + MOE_BLOCK - 1) // MOE_BLOCK) * MOE_BLOCK
    pend = jnp.cumsum(padded)
    pstart = pend - padded
    bstart = jnp.arange(MOE_NBLOCKS, dtype=jnp.int32) * MOE_BLOCK
    block_expert = jnp.minimum(jnp.sum(bstart[:, None] >= pend[None, :], axis=1), N_EXPERTS - 1)
    block_expert = block_expert.astype(jnp.int32)
    mine = block_expert[:, None] == jnp.arange(N_EXPERTS, dtype=jnp.int32)[None, :]
    left = jnp.sum(jnp.where(mine, counts + pstart, 0), axis=1) - bstart
    block_count = jnp.clip(left, 0, MOE_BLOCK).astype(jnp.int32)
    base = jnp.pad(pstart.astype(F32), (0, LANES - N_EXPERTS)).reshape(1, LANES)
    return _slot_rows(route, base), block_expert, block_count


def _slot_rows_kernel(rt_ref, base_ref, o_ref):
    rt = rt_ref[...]
    lanef = _lane_iota(rt.shape).astype(F32)
    base = base_ref[...]
    dest = jnp.zeros(rt.shape, F32)
    for k in range(TOP_K):
        b = jnp.sum(jnp.where(lanef == rt[:, k:k + 1], base, 0.0), axis=1, keepdims=True)
        dest = jnp.where(lanef == k, b + rt[:, 2 * TOP_K + k:2 * TOP_K + k + 1], dest)
    dest_t = dest.T.astype(jnp.int32)
    for k in range(TOP_K):
        for p in range(N_PLANES):
            for c in range(rt.shape[0] // LANES):
                o_ref[k * N_PLANES + p, c:c + 1, :] = dest_t[k:k + 1, c * LANES:(c + 1) * LANES] + p * PLANE_ROWS


def _slot_rows(route, base):
    tm = 8 * LANES
    return pl.pallas_call(
        _slot_rows_kernel,
        grid=(N_TOK // tm,),
        in_specs=[pl.BlockSpec((tm, LANES), lambda i: (i, 0)), pl.BlockSpec((1, LANES), lambda i: (0, 0))],
        out_specs=pl.BlockSpec((TOP_K * N_PLANES, tm // LANES, LANES), lambda i: (0, i, 0)),
        out_shape=jax.ShapeDtypeStruct((TOP_K * N_PLANES, N_TOK // LANES, LANES), jnp.int32),
        compiler_params=_params("arbitrary"),
        name="slot_rows",
    )(route, base)


def _sc_workers():
    info = plsc.get_sparse_core_info()
    return info.num_cores, info.num_cores * info.num_subcores


def _sc_scatter2(src, idx, out_rows):
    n_win = src.shape[0] // SC_WINDOW
    nc, nw = _sc_workers()
    steps = n_win // nw
    mesh = plsc.VectorSubcoreMesh(core_axis_name="c", subcore_axis_name="s")

    @functools.partial(
        pl.kernel, mesh=mesh,
        out_type=jax.ShapeDtypeStruct((out_rows, LANES), src.dtype),
        scratch_types=[pltpu.VMEM((2 * steps, SC_WINDOW), jnp.int32),
                       pltpu.VMEM((SC_INFLIGHT, SC_WINDOW, LANES), src.dtype),
                       pltpu.SemaphoreType.DMA((SC_INFLIGHT,)), pltpu.SemaphoreType.DMA((SC_INFLIGHT,))],
        name="sc_dispatch_scatter",
    )
    def k(src_hbm, idx_hbm, out_hbm, idx_v, rows_v, lsem, wsem):
        first = (lax.axis_index("s") * nc + lax.axis_index("c")) * steps
        pltpu.sync_copy(idx_hbm.at[pl.ds(first, steps)], idx_v.at[pl.ds(0, steps)])
        pltpu.sync_copy(idx_hbm.at[pl.ds(n_win + first, steps)], idx_v.at[pl.ds(steps, steps)])

        @pl.loop(0, steps, step=SC_INFLIGHT)
        def _(j):
            loads = [pltpu.async_copy(src_hbm.at[pl.ds((first + j + b) * SC_WINDOW, SC_WINDOW)],
                                      rows_v.at[b], lsem.at[b]) for b in range(SC_INFLIGHT)]
            writes = []
            for b in range(SC_INFLIGHT):
                loads[b].wait()
                for half in range(TOP_K):
                    dst = out_hbm.at[idx_v.at[half * steps + j + b]]
                    writes.append(pltpu.async_copy(rows_v.at[b], dst, wsem.at[b]))
            for w in writes:
                w.wait()

    return k(src, idx)


def _sc_gather(table, idx):
    n_out = idx.shape[0] * SC_WINDOW
    nc, nw = _sc_workers()
    steps = n_out // nw // SC_WINDOW
    mesh = plsc.VectorSubcoreMesh(core_axis_name="c", subcore_axis_name="s")

    @functools.partial(
        pl.kernel, mesh=mesh,
        out_type=jax.ShapeDtypeStruct((n_out, LANES), table.dtype),
        scratch_types=[pltpu.VMEM((steps, SC_WINDOW), jnp.int32),
                       pltpu.VMEM((SC_INFLIGHT, SC_WINDOW, LANES), table.dtype),
                       pltpu.SemaphoreType.DMA((SC_INFLIGHT,)), pltpu.SemaphoreType.DMA((SC_INFLIGHT,))],
        name="sc_combine_gather",
    )
    def k(table_hbm, idx_hbm, out_hbm, idx_v, rows_v, gsem, wsem):
        first = (lax.axis_index("s") * nc + lax.axis_index("c")) * steps
        pltpu.sync_copy(idx_hbm.at[pl.ds(first, steps)], idx_v)

        @pl.loop(0, steps, step=SC_INFLIGHT)
        def _(j):
            gathers = [pltpu.async_copy(table_hbm.at[idx_v.at[j + b]], rows_v.at[b], gsem.at[b])
                       for b in range(SC_INFLIGHT)]
            writes = []
            for b in range(SC_INFLIGHT):
                gathers[b].wait()
                dst = out_hbm.at[pl.ds((first + j + b) * SC_WINDOW, SC_WINDOW)]
                writes.append(pltpu.async_copy(rows_v.at[b], dst, wsem.at[b]))
            for w in writes:
                w.wait()

    return k(table, idx)


def _final_kernel(x_ref, rt_ref, y_ref, gt_ref, g_ref, o_ref):
    x = x_ref[...] + gt_ref[...] * _combine(rt_ref, y_ref)
    ms = jnp.mean(x * x, axis=-1, keepdims=True)
    o_ref[...] = x * lax.rsqrt(ms + EPS) * g_ref[...]


def _final(x, moe, gate, g):
    tpb = SEQ // TM
    row = pl.BlockSpec((TM, D_MODEL), lambda i: (i, 0))
    return pl.pallas_call(
        _final_kernel,
        grid=(N_TOK // TM,),
        in_specs=[row, pl.BlockSpec((TM, LANES), lambda i: (i, 0)),
                  pl.BlockSpec((TOP_K, N_PLANES, TM, LANES), lambda i: (0, 0, i, 0)),
                  pl.BlockSpec((None, 1, D_MODEL), lambda i: (i // tpb, 0, 0)),
                  pl.BlockSpec((1, D_MODEL), lambda i: (0, 0))],
        out_specs=row,
        out_shape=jax.ShapeDtypeStruct((N_TOK, D_MODEL), F32),
        compiler_params=_params("arbitrary"),
        name="final_norm",
    )(x, moe[0], moe[1], gate, g.reshape(1, D_MODEL))


def kernel(x, c, positions, w_ada, b_ada, g_mix, w_in, b_forget, lambda_q1, lambda_k1, lambda_q2,
           lambda_k2, g_subln, g_fox_out, w_out, g_ffn, w_router_group, b_router_group,
           w_router_expert, b_router_expert, w_expert_gate, w_expert_up, w_expert_down, g_final):
    mod = _modulation(c, w_ada, b_ada)
    mod = mod.reshape(DEPTH, BATCH, 6, 1, D_MODEL)
    tables = _rope_tables(positions)
    pq = _forget_placement()
    w_in_bf = jnp.pad(w_in.astype(BF16), ((0, 0), (0, 0), (0, IN_COLS_PAD - IN_COLS)))
    xf = x.reshape(N_TOK, D_MODEL)
    moe = None
    gate = None
    for l in range(DEPTH):
        sh1, sc1, gt1, sh2, sc2, gt2 = (mod[l, :, j] for j in range(6))
        w_bf = (l, w_in_bf)
        bfp =jnp.pad(b_forget[l], (0, LANES - N_FOX_HEADS)).reshape(1, LANES)
        xf, (dq, dk, dv, fq, fk, fv) = _inproj(xf, moe, gate, sc1, sh1, g_mix[l], w_bf, bfp, tables, pq)

        lambda_init = 0.8 - 0.6 * float(np.exp(-0.3 * l))
        lamv = jnp.zeros((8, LANES), F32).at[0:4, 0:HEAD_DIM].set(
            jnp.stack([lambda_q1[l], lambda_k1[l], lambda_q2[l], lambda_k2[l]]))
        g_d = g_subln[l].reshape(1, LANES)
        g_f = jnp.concatenate([g_fox_out[l], g_fox_out[l]]).reshape(1, LANES)
        od = _attention(True, lambda_init, dq, dk, dv, g_d, lamv)
        of = _attention(False, lambda_init, fq, fk, fv, g_f, lamv)

        wr32 = jnp.pad(jnp.concatenate([w_router_group[l], w_router_expert[l]], axis=1),
                       ((0, 0), (0, LANES - N_GROUPS - N_EXPERTS)))
        wr_hi = wr32.astype(BF16)
        wr_lo = (wr32 - wr_hi.astype(F32)).astype(BF16)
        wr = jnp.concatenate([wr_hi, wr_lo], axis=1)
        br = jnp.pad(jnp.concatenate([b_router_group[l], b_router_expert[l]]),
                     (0, LANES - N_GROUPS - N_EXPERTS)).reshape(1, LANES)
        xf, h2, route, counts = _outproj(xf, od, of, gt1, sc2, sh2, g_ffn[l], w_out[l].astype(BF16),
                                         wr, br)

        rows, block_expert, block_count = _slots(route, counts)
        rows = rows.reshape(TOP_K * N_PLANES * N_TOK // SC_WINDOW, SC_WINDOW)
        xs = _sc_scatter2(h2.reshape(N_PLANES * N_TOK, LANES), rows, N_PLANES * PLANE_ROWS)
        ys = _experts(l, block_expert, block_count, xs.reshape(N_PLANES, PLANE_ROWS, LANES),
                      w_expert_gate, w_expert_up, w_expert_down)
        y2 = _sc_gather(ys.reshape(N_PLANES * PLANE_ROWS, LANES), rows)
        moe = (route, y2.reshape(TOP_K, N_PLANES, N_TOK, LANES))
        gate = gt2
    out = _final(xf, moe, gate, g_final)
    return out.reshape(BATCH, SEQ, D_MODEL)
```

```python
import functools

import numpy as np
import jax
import jax.numpy as jnp
from jax import lax
from jax.experimental import pallas as pl
from jax.experimental.pallas import tpu as pltpu
from jax.experimental.pallas import tpu_sc as plsc

D_MODEL = 1024
BATCH = 4
SEQ = 4096
DEPTH = 4
N_TOK = BATCH * SEQ

CHUNK = 64
HEAD_DIM = 64
N_DIFF_HEADS = 4
N_FOX_HEADS = 8
DIFF_WIDTH = 512
FOX_WIDTH = 512
IN_COLS = 3 * DIFF_WIDTH + 3 * FOX_WIDTH + N_FOX_HEADS
ROT_DIM = 16
ROPE_THETA = 500000.0
N_GROUPS = 4
EXPERTS_PER_GROUP = 8
N_EXPERTS = 32
TOP_K = 2
D_EXPERT = 512
EPS = 1e-6

LANES = 128
IN_COLS_PAD = 3200
FF_COL = 3 * DIFF_WIDTH + 3 * FOX_WIDTH
QK_WIDTH = 8 * LANES
TM = 512
OUTPROJ_TILES = 2
TQ = 512
ATTN_TQ = 256
N_LATE_CHAINS = 1
ONES_ROWS = 16
ATTN_ROWS = 64
LOG2E = 1.4426950408889634
MOE_BLOCK = 512
MOE_ROWS = N_TOK * TOP_K + N_EXPERTS * MOE_BLOCK
MOE_NBLOCKS = MOE_ROWS // MOE_BLOCK
PLANE_ROWS = MOE_ROWS
N_PLANES = D_MODEL // 2 // LANES
SC_WINDOW = 128
SC_INFLIGHT = 4
NEG = -1e30
VMEM_LIMIT = 56 * 1024 * 1024

F32 = jnp.float32
BF16 = jnp.bfloat16


def _bf16_round(x):
    return x.astype(BF16).astype(F32)


def _lane_iota(shape):
    return lax.broadcasted_iota(jnp.int32, shape, 1)


def _params(*sem):
    return pltpu.CompilerParams(dimension_semantics=sem, vmem_limit_bytes=VMEM_LIMIT)


def _pack_planes(y, o_ref, rows=slice(None)):
    bits = lax.bitcast_convert_type(_bf16_round(y), jnp.uint32)
    half = D_MODEL // 2
    word = bits[:, half:] | lax.shift_right_logical(bits[:, :half], jnp.uint32(16))
    word = lax.bitcast_convert_type(word, jnp.int32)
    for p in range(N_PLANES):
        o_ref[p, rows, :] = word[:, p * LANES:(p + 1) * LANES]


def _unpack_planes(planes):
    lo, hi = [], []
    for w in planes:
        u = lax.bitcast_convert_type(w, jnp.uint32)
        lo.append(lax.bitcast_convert_type(lax.shift_left(u, jnp.uint32(16)), F32))
        hi.append(lax.bitcast_convert_type(u & jnp.uint32(0xFFFF0000), F32))
    return jnp.concatenate(lo + hi, axis=1)


def _combine(route_ref, y_ref):
    rt = route_ref[...]
    y0 = _unpack_planes([y_ref[0, p] for p in range(N_PLANES)])
    y1 = _unpack_planes([y_ref[1, p] for p in range(N_PLANES)])
    return rt[:, 2:3] * y0 + rt[:, 3:4] * y1


def _mod_kernel(c_ref, w_ref, b_ref, o_ref):
    c = c_ref[...]
    cond = c / (1.0 + jnp.exp(-c))
    ch = cond.astype(BF16)
    cl = (cond - ch.astype(F32)).astype(BF16)
    w = w_ref[...]
    wh = w.astype(BF16)
    wl = (w - wh.astype(F32)).astype(BF16)
    acc = jnp.dot(ch, wh, preferred_element_type=F32)
    acc += jnp.dot(cl, wh, preferred_element_type=F32)
    acc += jnp.dot(ch, wl, preferred_element_type=F32)
    o_ref[...] = acc + b_ref[...]


def _modulation(c, w_ada, b_ada):
    rows = 16
    tn = 1536
    c_pad = jnp.zeros((rows, D_MODEL), F32).at[:BATCH].set(c)
    out = pl.pallas_call(
        _mod_kernel,
        grid=(DEPTH, 6 * D_MODEL // tn),
        in_specs=[
            pl.BlockSpec((rows, D_MODEL), lambda l, n: (0, 0)),
            pl.BlockSpec((None, D_MODEL, tn), lambda l, n: (l, 0, n)),
            pl.BlockSpec((None, 1, tn), lambda l, n: (l, 0, n)),
        ],
        out_specs=pl.BlockSpec((None, rows, tn), lambda l, n: (l, 0, n)),
        out_shape=jax.ShapeDtypeStruct((DEPTH, rows, 6 * D_MODEL), F32),
        compiler_params=_params("arbitrary", "arbitrary"),
        name="adaln_mod",
    )(c_pad, w_ada, b_ada.reshape(DEPTH, 1, 6 * D_MODEL))
    return out[:, :BATCH]


def _rope_kernel(pos_ref, inv_ref, c_ref, sa_ref, sb_ref):
    ang = pos_ref[...].astype(F32) * inv_ref[...]
    j = _lane_iota(ang.shape) % HEAD_DIM
    cosv = jnp.cos(ang)
    sinv = jnp.sin(ang)
    half = ROT_DIM // 2
    c_ref[...] = jnp.where(j < ROT_DIM, cosv, 1.0)
    sa_ref[...] = jnp.where(j < half, -sinv, 0.0)
    sb_ref[...] = jnp.where((j >= half) & (j < ROT_DIM), sinv, 0.0)


def _rope_tables(positions):
    half = ROT_DIM // 2
    inv = ROPE_THETA ** (-jnp.arange(0, ROT_DIM, 2, dtype=F32) / ROT_DIM)
    lane = np.arange(LANES)
    inv_lane = inv[(lane % HEAD_DIM) % half].reshape(1, LANES)
    spec = pl.BlockSpec((TM, LANES), lambda i: (i, 0))
    shape = jax.ShapeDtypeStruct((N_TOK, LANES), F32)
    return pl.pallas_call(
        _rope_kernel,
        grid=(N_TOK // TM,),
        in_specs=[pl.BlockSpec((TM, 1), lambda i: (i, 0)),
                  pl.BlockSpec((1, LANES), lambda i: (0, 0))],
        out_specs=[spec, spec, spec],
        out_shape=[shape, shape, shape],
        compiler_params=_params("arbitrary"),
        name="rope_tables",
    )(positions.reshape(N_TOK, 1), inv_lane)


def _rms_mod(x, g, sc, sh):
    ms = jnp.mean(x * x, axis=-1, keepdims=True)
    return (x * lax.rsqrt(ms + EPS) * g) * (1.0 + sc) + sh


def _inproj_kernel(fuse, *refs):
    if fuse:
        (x_ref, rt_ref, y_ref, gt_ref, sc_ref, sh_ref, g_ref, w_ref, bf_ref, c_ref, sa_ref, sb_ref,
         pq_ref, xo_ref, dq_ref, dk_ref, dv_ref, fq_ref, fk_ref, fv_ref, carry_ref) = refs
        x = x_ref[...] + gt_ref[...] * _combine(rt_ref, y_ref)
        xo_ref[...] = x
    else:
        (x_ref, sc_ref, sh_ref, g_ref, w_ref, bf_ref, c_ref, sa_ref, sb_ref,
         pq_ref, dq_ref, dk_ref, dv_ref, fq_ref, fk_ref, fv_ref, carry_ref) = refs
        x = x_ref[...]
    hb = _rms_mod(x, g_ref[...], sc_ref[...], sh_ref[...]).astype(BF16)

    @pl.when(pl.program_id(0) % (SEQ // TM) == 0)
    def _():
        carry_ref[...] = jnp.zeros_like(carry_ref)

    lane = _lane_iota((TM, LANES))
    nh = N_FOX_HEADS

    def pack3(a):
        hi = _bf16_round(a)
        r1 = a - hi
        mid = _bf16_round(r1)
        lo = _bf16_round(r1 - mid)
        return jnp.where(lane < nh, hi,
                         jnp.where(lane < 2 * nh, pltpu.roll(mid, nh, 1),
                                   jnp.where(lane < 3 * nh, pltpu.roll(lo, 2 * nh, 1), 0.0)))

    z = jnp.dot(hb, w_ref[:, FF_COL:FF_COL + LANES], preferred_element_type=F32) + bf_ref[...]

    low = lane < HEAD_DIM
    rc, rsa, rsb = c_ref[...], sa_ref[...], sb_ref[...]
    scale = HEAD_DIM ** -0.5 * LOG2E

    def split_store(chunk, o_ref, m, extra_a=None, extra_b=None):
        a = jnp.where(low, chunk, 0.0)
        b = jnp.where(low, pltpu.roll(chunk, HEAD_DIM, 1), 0.0)
        if extra_a is not None:
            a = a + extra_a
            b = b + extra_b
        o_ref[:, (2 * m) * LANES:(2 * m + 1) * LANES] = a.astype(BF16)
        o_ref[:, (2 * m + 1) * LANES:(2 * m + 2) * LANES] = b.astype(BF16)

    def rope(xc):
        return xc * rc + pltpu.roll(xc, LANES - ROT_DIM // 2, 1) * rsa + pltpu.roll(xc, ROT_DIM // 2, 1) * rsb

    pdq = jnp.dot(hb, w_ref[:, 0:DIFF_WIDTH], preferred_element_type=F32)
    for m in range(N_DIFF_HEADS):
        split_store(rope(pdq[:, m * LANES:(m + 1) * LANES]) * scale, dq_ref, m)
    pdk = jnp.dot(hb, w_ref[:, DIFF_WIDTH:2 * DIFF_WIDTH], preferred_element_type=F32)
    for m in range(N_DIFF_HEADS):
        split_store(rope(pdk[:, m * LANES:(m + 1) * LANES]), dk_ref, m)
    def store_values_t(pv, o_ref, width):
        ones = jnp.ones((ONES_ROWS, TM), BF16)
        for m in range(4):
            vt = pv[:, m * LANES:(m + 1) * LANES].T.astype(BF16)
            for i in range(LANES // width):
                o_ref[m * (LANES // width) + i, 0:width, :] = vt[i * width:(i + 1) * width]
                o_ref[m * (LANES // width) + i, width:width + ONES_ROWS, :] = ones

    store_values_t(jnp.dot(hb, w_ref[:, 2 * DIFF_WIDTH:3 * DIFF_WIDTH], preferred_element_type=F32),
                   dv_ref, 2 * HEAD_DIM)
    o = 3 * DIFF_WIDTH
    store_values_t(jnp.dot(hb, w_ref[:, o + 2 * FOX_WIDTH:o + 3 * FOX_WIDTH],
                           preferred_element_type=F32), fv_ref, HEAD_DIM)

    logf =jnp.minimum(z, 0.0) - jnp.log(1.0 + jnp.exp(-jnp.abs(z)))
    logf = jnp.where(lane < nh, logf, 0.0)
    row = lax.broadcasted_iota(jnp.int32, (TM, TM), 0)
    col = lax.broadcasted_iota(jnp.int32, (TM, TM), 1)
    tri = (row >= col).astype(BF16)
    r = jnp.dot(tri, pack3(logf).astype(BF16), preferred_element_type=F32)
    cs = r + pltpu.roll(r, LANES - nh, 1) + pltpu.roll(r, LANES - 2 * nh, 1)
    cf = jnp.where(lane < nh, cs + carry_ref[0:1, :], 0.0)
    carry_ref[...] = jnp.broadcast_to(cf[TM - 1:TM, :], carry_ref.shape)

    t3 = jnp.where(lane == 3 * nh, 1.0, pack3(cf * LOG2E)).astype(BF16)
    aug = jnp.dot(t3, pq_ref[...], preferred_element_type=F32)

    pfq =jnp.dot(hb, w_ref[:, o:o + FOX_WIDTH], preferred_element_type=F32)
    for m in range(N_FOX_HEADS // 2):
        split_store(pfq[:, m * LANES:(m + 1) * LANES] * scale, fq_ref, m,
                    aug[:, (2 * m) * LANES:(2 * m + 1) * LANES],
                    aug[:, (2 * m + 1) * LANES:(2 * m + 2) * LANES])
    pfk = jnp.dot(hb, w_ref[:, o + FOX_WIDTH:o + 2 * FOX_WIDTH], preferred_element_type=F32)
    for m in range(N_FOX_HEADS // 2):
        split_store(pfk[:, m * LANES:(m + 1) * LANES], fk_ref, m,
                    aug[:, QK_WIDTH + (2 * m) * LANES:QK_WIDTH + (2 * m + 1) * LANES],
                    aug[:, QK_WIDTH + (2 * m + 1) * LANES:QK_WIDTH + (2 * m + 2) * LANES])


def _forget_placement():
    nh = N_FOX_HEADS
    p = np.zeros((LANES, 2 * QK_WIDTH), np.float32)
    for h in range(nh):
        base_q = h * LANES + HEAD_DIM
        base_k = QK_WIDTH + h * LANES + HEAD_DIM
        for part in range(3):
            p[part * nh + h, base_q + part] = 1.0
            p[3 * nh, base_q + 3 + part] = 1.0
            p[3 * nh, base_k + part] = 1.0
            p[part * nh + h, base_k + 3 + part] = -1.0
    return jnp.asarray(p, BF16)


def _inproj(x, moe, gate, sc, sh, g, w_bf, b_forget, tables, pq):
    fuse = moe is not None
    tpb = SEQ // TM
    row = pl.BlockSpec((TM, D_MODEL), lambda i: (i, 0))
    per_batch = pl.BlockSpec((None, 1, D_MODEL), lambda i: (i // tpb, 0, 0))
    const = lambda shape: pl.BlockSpec(shape, lambda i: (0,) * len(shape))
    tab = pl.BlockSpec((TM, LANES), lambda i: (i, 0))
    in_specs = [row]
    args = [x]
    if fuse:
        in_specs += [tab, pl.BlockSpec((TOP_K, N_PLANES, TM, LANES), lambda i: (0, 0, i, 0)), per_batch]
        args += [moe[0], moe[1], gate]
    layer, w_all = w_bf
    in_specs += [per_batch, per_batch, const((1, D_MODEL)),
                 pl.BlockSpec((None, D_MODEL, IN_COLS_PAD), lambda i: (layer, 0, 0)),
                 const((1, LANES)), tab, tab, tab, const((LANES, 2 * QK_WIDTH))]
    args += [sc, sh, g.reshape(1, D_MODEL), w_all, b_forget, *tables, pq]
    wide = pl.BlockSpec((TM, QK_WIDTH), lambda i: (i, 0))
    def vspec(heads, width):
        rows = width + ONES_ROWS
        return (pl.BlockSpec((None, heads, None, rows, TM), lambda i: (i // tpb, 0, i % tpb, 0, 0)),
                jax.ShapeDtypeStruct((BATCH, heads, tpb, rows, TM), BF16))

    wide_s = jax.ShapeDtypeStruct((N_TOK, QK_WIDTH), BF16)
    dv_spec, dv_s = vspec(N_DIFF_HEADS, 2 * HEAD_DIM)
    fv_spec, fv_s = vspec(N_FOX_HEADS, HEAD_DIM)
    out_specs = [wide, wide, dv_spec, wide, wide, fv_spec]
    out_shape = [wide_s, wide_s, dv_s, wide_s, wide_s, fv_s]
    if fuse:
        out_specs = [row] + out_specs
        out_shape = [jax.ShapeDtypeStruct((N_TOK, D_MODEL), F32)] + out_shape
    outs = pl.pallas_call(
        functools.partial(_inproj_kernel, fuse),
        grid=(N_TOK // TM,),
        in_specs=in_specs,
        out_specs=out_specs,
        out_shape=out_shape,
        scratch_shapes=[pltpu.VMEM((8, LANES), F32)],
        compiler_params=_params("arbitrary"),
        name="norm_inproj",
    )(*args)
    if fuse:
        return outs[0], outs[1:]
    return x, outs


def _attn_kernel(diff, lambda_init, qa_ref, qb_ref, ka_ref, kb_ref, v_ref, g_ref, lam_ref, o_ref,
                 *scratch):
    nq = SEQ // TQ
    n_half = TQ // ATTN_TQ
    feat = 2 * HEAD_DIM if diff else HEAD_DIM
    chains = []
    for mi, (q_ref, k_ref) in enumerate(((qa_ref, ka_ref), (qb_ref, kb_ref))):
        for h in range(n_half):
            c = mi * n_half + h
            qt_sc, s_sc, p_sc, m_sc, a_sc, acc_sc = scratch[c::2 * n_half]
            vh = 0 if diff else mi
            chains.append((h, k_ref, qt_sc, s_sc, p_sc, m_sc, a_sc, acc_sc, q_ref, vh))
    order = [chains[mi * n_half + h] for h in range(n_half) for mi in range(2)]
    early, late = order[:-N_LATE_CHAINS], order[-N_LATE_CHAINS:]

    def load_queries(qi):
        for mi in range(2):
            q_ref = chains[mi * n_half][8]
            qt = q_ref[pl.ds(pl.multiple_of(qi * TQ, TQ), TQ), :].astype(F32).T.astype(BF16)
            for h in range(n_half):
                chains[mi * n_half + h][2][...] = qt[:, h * ATTN_TQ:(h + 1) * ATTN_TQ]

    def reset_state():
        for chain in chains:
            m_sc, _, acc_sc = chain[5:8]
            m_sc[...] = jnp.full(m_sc.shape, NEG, F32)
            acc_sc[...] = jnp.zeros(acc_sc.shape, F32)

    def n_keys(chain, masked):
        return (chain[0] + 1) * ATTN_TQ if masked else TQ

    def scores(chain, j, masked):
        h, k_ref, qt_sc, s_sc = chain[:4]
        nk = n_keys(chain, masked)
        off = pl.multiple_of(j * TQ, TQ)
        s = jnp.dot(k_ref[pl.ds(off, nk), :], qt_sc[...], preferred_element_type=F32)
        if masked:
            kk = lax.broadcasted_iota(jnp.int32, (nk, ATTN_TQ), 0)
            qq = h * ATTN_TQ + lax.broadcasted_iota(jnp.int32, (nk, ATTN_TQ), 1)
            s = jnp.where((kk // CHUNK <= qq // CHUNK) if diff else (kk <= qq), s, NEG)
        s_sc[0:nk, :] = s

    def softmax(chain, masked):
        s_sc, p_sc, m_sc, a_sc = chain[3:7]
        nk = n_keys(chain, masked)
        m_all = m_sc[...]
        m_parts = []
        for c0 in range(0, ATTN_TQ, LANES):
            cols = slice(c0, c0 + LANES)
            pm = s_sc[0:ATTN_ROWS, cols]
            for r0 in range(ATTN_ROWS, nk, ATTN_ROWS):
                pm = jnp.maximum(pm, s_sc[r0:r0 + ATTN_ROWS, cols])
            m_new = jnp.maximum(m_all[:, cols], jnp.max(pm, axis=0, keepdims=True))
            for r0 in range(0, nk, ATTN_ROWS):
                p = jnp.exp2(s_sc[r0:r0 + ATTN_ROWS, cols] - m_new)
                p_sc[r0:r0 + ATTN_ROWS, cols] = p.astype(BF16)
            m_parts.append(m_new)
        m_new = jnp.concatenate(m_parts, axis=1)
        a_sc[...] = jnp.exp2(m_all - m_new)
        m_sc[...] = m_new

    def values(chain, j, masked=False):
        p_sc, a_sc, acc_sc, vh = chain[4], chain[6], chain[7], chain[9]
        nk = n_keys(chain, masked)
        pv = jnp.dot(v_ref[vh, j, :, 0:nk], p_sc[0:nk, :], preferred_element_type=F32)
        acc_sc[...] = a_sc[...] * acc_sc[...] + pv

    def idle_late():
        for chain in late:
            chain[4][...] = jnp.zeros(chain[4].shape, BF16)
            chain[6][...] = jnp.ones(chain[6].shape, F32)

    def consume(j, cur_masked=False, nxt=None, nxt_masked=False, final=False, before_next=None):
        def open_late(chain):
            scores(chain, j, cur_masked)
            values(chain, jnp.maximum(j - 1, 0))

        open_late(late[0])
        for i, chain in enumerate(early):
            softmax(chain, cur_masked)
            if i == 0:
                for other in late[1:]:
                    open_late(other)
                if before_next is not None:
                    before_next()
            if nxt is not None:
                scores(chain, nxt, nxt_masked)
            values(chain, j, cur_masked)
        for chain in late:
            softmax(chain, cur_masked)
        if final:
            for chain in late:
                values(chain, j, cur_masked)

    def finalize(qi):
        ot = [jnp.concatenate([chains[mi * n_half + h][7][0:feat] / chains[mi * n_half + h][7][feat:feat + 1]
                               for h in range(n_half)], axis=1) for mi in range(2)]
        g = g_ref[...]
        rows = pl.ds(pl.multiple_of(qi * TQ, TQ), TQ)
        if diff:
            lv = lam_ref[...]
            lam = (jnp.exp(jnp.sum(lv[0:1] * lv[1:2], axis=1, keepdims=True))
                   - jnp.exp(jnp.sum(lv[2:3] * lv[3:4], axis=1, keepdims=True)) + lambda_init)
            o = (ot[0] - lam * ot[1]).T
            y = o * lax.rsqrt(jnp.mean(o * o, axis=1, keepdims=True) + EPS) * g
            o_ref[rows, :] = (y * (1.0 - lambda_init)).astype(o_ref.dtype)
        else:
            o = jnp.concatenate(ot, axis=0).T
            low = _lane_iota((TQ, LANES)) < HEAD_DIM
            sq = o * o
            msa = jnp.sum(jnp.where(low, sq, 0.0), axis=1, keepdims=True) / HEAD_DIM
            msb = jnp.sum(jnp.where(low, 0.0, sq), axis=1, keepdims=True) / HEAD_DIM
            inv = jnp.where(low, lax.rsqrt(msa + EPS), lax.rsqrt(msb + EPS))
            o_ref[rows, :] = (o * inv * g).astype(o_ref.dtype)

    load_queries(0)
    reset_state()
    idle_late()
    for chain in early:
        scores(chain, 0, True)

    @pl.loop(0, nq)
    def _(qi):
        n_plain = jnp.maximum(qi - 1, 0)

        def run(first, count):
            for i in range(count):
                consume(first + i, nxt=first + i + 1)

        @pl.loop(0, n_plain // 4)
        def _(t):
            run(4 * t, 4)

        done = (n_plain // 4) * 4

        @pl.when(n_plain - done >= 2)
        def _():
            run(done, 2)

        def last_blocks(to_next_tile):
            if to_next_tile:
                consume(qi, cur_masked=True, nxt=0, final=True, before_next=lambda: load_queries(qi + 1))
            else:
                consume(qi, cur_masked=True, final=True)
            finalize(qi)

        for to_next_tile in (True, False):
            more = (qi < nq - 1) if to_next_tile else (qi == nq - 1)

            last_tile_odd = (nq - 2) % 2 == 1
            for odd in ((True, False) if to_next_tile else (last_tile_odd,)):
                @pl.when(more & (qi > 0) & ((n_plain % 2 == 1) == odd))
                def _():
                    if odd:
                        run(qi - 2, 1)
                    consume(qi - 1, nxt=qi, nxt_masked=True)
                    last_blocks(to_next_tile)

            if to_next_tile:
                @pl.when(qi == 0)
                def _():
                    last_blocks(to_next_tile)

        reset_state()
        idle_late()


def _attention(diff, lambda_init, q, k, v, g, lamv):
    nq = SEQ // TQ
    kspec = lambda par: pl.BlockSpec((SEQ, LANES), lambda b, p: (b, 2 * p + par))
    return pl.pallas_call(
        functools.partial(_attn_kernel, diff, lambda_init),
        grid=(BATCH, 4),
        in_specs=[kspec(0), kspec(1), kspec(0), kspec(1),
                  pl.BlockSpec((None, v.shape[1] // 4, nq, v.shape[3], TQ), lambda b, p: (b, p, 0, 0, 0)),
                  pl.BlockSpec((1, LANES), lambda b, p: (0, 0)),
                  pl.BlockSpec((8, LANES), lambda b, p: (0, 0))],
        out_specs=pl.BlockSpec((SEQ, LANES), lambda b, p: (b, p)),
        out_shape=jax.ShapeDtypeStruct((N_TOK, DIFF_WIDTH), BF16),
        scratch_shapes=[pltpu.VMEM(shape, dt)
                        for shape, dt in (((LANES, ATTN_TQ), BF16), ((TQ, ATTN_TQ), F32),
                                          ((TQ, ATTN_TQ), BF16), ((1, ATTN_TQ), F32),
                                          ((1, ATTN_TQ), F32), ((v.shape[3], ATTN_TQ), F32))
                        for _ in range(2 * TQ // ATTN_TQ)],
        compiler_params=_params("arbitrary", "arbitrary"),
        name="diff_attention" if diff else "fox_attention",
    )(q, q, k, k, v, g, lamv)


def _outproj_kernel(x_ref, od_ref, of_ref, gt_ref, sc_ref, sh_ref, g_ref, wo_ref, wr_ref, br_ref,
                    x1_ref, h2_ref, rt_ref, cnt_ref, carry_ref):
    @pl.when(pl.program_id(0) == 0)
    def _():
        carry_ref[...] = jnp.zeros_like(carry_ref)

    tiles = [slice(t * TM, (t + 1) * TM) for t in range(OUTPROJ_TILES)]
    lane = _lane_iota((TM, LANES))
    lanef = lane.astype(F32)
    big = float(LANES)

    def project(rows):
        mix = jnp.dot(od_ref[rows, :], wo_ref[0:DIFF_WIDTH, :], preferred_element_type=F32)
        return mix + jnp.dot(of_ref[rows, :], wo_ref[DIFF_WIDTH:, :], preferred_element_type=F32)

    def normalise(rows, mix):
        x1 = x_ref[rows, :] + gt_ref[...] * mix
        x1_ref[rows, :] = x1
        h = _rms_mod(x1, g_ref[...], sc_ref[...], sh_ref[...])
        hh = h.astype(BF16)
        _pack_planes(h, h2_ref, rows)
        return hh, (h - hh.astype(F32)).astype(BF16)

    def router_logits(hh, hl):
        r1 = jnp.dot(hh, wr_ref[...], preferred_element_type=F32)
        r2 = jnp.dot(hl, wr_ref[:, 0:LANES], preferred_element_type=F32)
        return r1[:, 0:LANES] + r1[:, LANES:] + r2 + br_ref[...]

    def top_k(logits):
        isg = lane < N_GROUPS
        lg = jnp.where(isg, logits, NEG)
        mg = jnp.max(lg, axis=1, keepdims=True)
        sg = jnp.sum(jnp.where(isg, jnp.exp(lg - mg), 0.0), axis=1, keepdims=True)
        p_g = 1.0 / sg
        gsel = jnp.min(jnp.where(isg & (lg == mg), lanef, big), axis=1, keepdims=True)
        lo = N_GROUPS + gsel * EXPERTS_PER_GROUP
        ise = (lanef >= lo) & (lanef < lo + EXPERTS_PER_GROUP)
        le = jnp.where(ise, logits, NEG)
        t1 = jnp.max(le, axis=1, keepdims=True)
        i1 = jnp.min(jnp.where(ise & (le == t1), lanef, big), axis=1, keepdims=True)
        ise2 = ise & (lanef != i1)
        le2 = jnp.where(ise2, logits, NEG)
        t2 = jnp.max(le2, axis=1, keepdims=True)
        i2 = jnp.min(jnp.where(ise2 & (le2 == t2), lanef, big), axis=1, keepdims=True)
        d = jnp.exp(t2 - t1)
        return i1 - N_GROUPS, i2 - N_GROUPS, p_g / (1.0 + d), p_g * d / (1.0 + d)

    def earlier_in_tile(e1, e2):
        both = jnp.where((lanef == e1) | (lanef == e2), 1.0, 0.0)
        row = lax.broadcasted_iota(jnp.int32, (TM, TM), 0)
        col = lax.broadcasted_iota(jnp.int32, (TM, TM), 1)
        before = jnp.dot((row > col).astype(BF16), both.astype(BF16), preferred_element_type=F32)
        return before, jnp.sum(both, axis=0, keepdims=True)

    mixes = [project(rows) for rows in tiles]
    splits = [normalise(rows, mix) for rows, mix in zip(tiles, mixes)]
    logits = [router_logits(hh, hl) for hh, hl in splits]
    picks = [top_k(lg) for lg in logits]
    befores = [earlier_in_tile(e1, e2) for e1, e2, _, _ in picks]
    counts = carry_ref[0:1, :]
    for rows, (e1, e2, w1, w2), (before, added) in zip(tiles, picks, befores):
        before = before + counts
        rank1 = jnp.sum(jnp.where(lanef == e1, before, 0.0), axis=1, keepdims=True)
        rank2 = jnp.sum(jnp.where(lanef == e2, before, 0.0), axis=1, keepdims=True)
        out = jnp.zeros((TM, LANES), F32)
        for j, v in enumerate((e1, e2, w1, w2, rank1, rank2)):
            out = jnp.where(lane == j, v, out)
        rt_ref[rows, :] = out
        counts = counts + added
    carry_ref[...] = jnp.broadcast_to(counts, carry_ref.shape)
    cnt_ref[...] = jnp.broadcast_to(counts, cnt_ref.shape)


def _outproj(x, od, of, gt, sc, sh, g, wo_bf, wr, br):
    tm = OUTPROJ_TILES * TM
    tpb = SEQ // tm
    row = pl.BlockSpec((tm, D_MODEL), lambda i: (i, 0))
    half = pl.BlockSpec((tm, DIFF_WIDTH), lambda i: (i, 0))
    per_batch = pl.BlockSpec((None, 1, D_MODEL), lambda i: (i // tpb, 0, 0))
    const = lambda shape: pl.BlockSpec(shape, lambda i: (0,) * len(shape))
    return pl.pallas_call(
        _outproj_kernel,
        grid=(N_TOK // tm,),
        in_specs=[row, half, half, per_batch, per_batch, per_batch, const((1, D_MODEL)),
                  const((D_MODEL, D_MODEL)), const((D_MODEL, 2 * LANES)), const((1, LANES))],
        out_specs=[row, pl.BlockSpec((N_PLANES, tm, LANES), lambda i: (0, i, 0)),
                   pl.BlockSpec((tm, LANES), lambda i: (i, 0)), const((8, LANES))],
        out_shape=[jax.ShapeDtypeStruct((N_TOK, D_MODEL), F32),
                   jax.ShapeDtypeStruct((N_PLANES, N_TOK, LANES), jnp.int32),
                   jax.ShapeDtypeStruct((N_TOK, LANES), F32),
                   jax.ShapeDtypeStruct((8, LANES), F32)],
        scratch_shapes=[pltpu.VMEM((8, LANES), F32)],
        compiler_params=_params("arbitrary"),
        name="outproj_router",
    )(x, od, of, gt, sc, sh, g.reshape(1, D_MODEL), wo_bf, wr, br)


def _expert_kernel(layer, be_ref, cnt_ref, first_ref, slot_ref, next_ref, xs_ref, wg_hbm, wu_hbm, wd_hbm,
                   ys_ref, wg_sc, wu_sc, wd_sc, wg_f32, wu_f32, wd_f32, sem):
    i = pl.program_id(0)
    cnt = cnt_ref[i]

    def weight_copies(expert, slot):
        return [pltpu.make_async_copy(hbm.at[layer, expert], buf.at[slot], sem.at[slot, n])
                for n, (hbm, buf) in enumerate(((wg_hbm, wg_f32), (wu_hbm, wu_f32), (wd_hbm, wd_f32)))]

    @pl.when(first_ref[i] == 1)
    def _():
        slot = slot_ref[i]

        @pl.when(i == 0)
        def _():
            for copy in weight_copies(be_ref[0], 0):
                copy.start()

        for copy in weight_copies(be_ref[i], slot):
            copy.wait()
        wg_sc[...] = wg_f32[slot].astype(BF16)
        wu_sc[...] = wu_f32[slot].astype(BF16)
        wd_sc[...] = wd_f32[slot].astype(BF16)

        @pl.when(next_ref[i] >= 0)
        def _():
            for copy in weight_copies(next_ref[i], 1 - slot):
                copy.start()

    def mlp(n_rows):
        rows = slice(0, n_rows)
        live = lax.broadcasted_iota(jnp.int32, (n_rows, LANES), 0) < cnt
        xb = _unpack_planes([jnp.where(live, xs_ref[p, rows, :], 0) for p in range(N_PLANES)]).astype(BF16)
        a = jnp.dot(xb, wg_sc[...], preferred_element_type=F32)
        u = jnp.dot(xb, wu_sc[...], preferred_element_type=F32)
        hid = (a / (1.0 + jnp.exp(-a)) * u).astype(BF16)
        _pack_planes(jnp.dot(hid, wd_sc[...], preferred_element_type=F32), ys_ref, rows)
        if n_rows < MOE_BLOCK:
            ys_ref[:, n_rows:, :] = jnp.zeros((N_PLANES, MOE_BLOCK - n_rows, LANES), ys_ref.dtype)

    half = MOE_BLOCK // 2

    @pl.when(cnt > half)
    def _():
        mlp(MOE_BLOCK)

    @pl.when((cnt > 0) & (cnt <= half))
    def _():
        mlp(half)

    @pl.when(cnt == 0)
    def _():
        ys_ref[...] = jnp.zeros_like(ys_ref)


def _experts(layer, block_expert, block_count, xs, wg, wu, wd):
    idx = jnp.arange(MOE_NBLOCKS, dtype=jnp.int32)
    first = jnp.concatenate([jnp.ones((1,), jnp.bool_), block_expert[1:] != block_expert[:-1]])
    slot = (jnp.cumsum(first.astype(jnp.int32)) - 1) % 2
    later_first = (idx[None, :] > idx[:, None]) & first[None, :]
    nxt = jnp.min(jnp.where(later_first, block_expert[None, :], N_EXPERTS), axis=1)
    nxt = jnp.where(nxt == N_EXPERTS, -1, nxt).astype(jnp.int32)

    planes = pl.BlockSpec((N_PLANES, MOE_BLOCK, LANES), lambda i, *_: (0, i, 0))
    hbm = pl.BlockSpec(memory_space=pl.ANY)
    grid_spec = pltpu.PrefetchScalarGridSpec(
        num_scalar_prefetch=5,
        grid=(MOE_NBLOCKS,),
        in_specs=[planes, hbm, hbm, hbm],
        out_specs=planes,
        scratch_shapes=[pltpu.VMEM((D_MODEL, D_EXPERT), BF16), pltpu.VMEM((D_MODEL, D_EXPERT), BF16),
                        pltpu.VMEM((D_EXPERT, D_MODEL), BF16),
                        pltpu.VMEM((2, D_MODEL, D_EXPERT), F32), pltpu.VMEM((2, D_MODEL, D_EXPERT), F32),
                        pltpu.VMEM((2, D_EXPERT, D_MODEL), F32), pltpu.SemaphoreType.DMA((2, 3))],
    )
    return pl.pallas_call(
        functools.partial(_expert_kernel, layer),
        grid_spec=grid_spec,
        out_shape=jax.ShapeDtypeStruct((N_PLANES, PLANE_ROWS, LANES), jnp.int32),
        compiler_params=_params("arbitrary"),
        name="expert_mlp",
    )(block_expert, block_count, first.astype(jnp.int32), slot.astype(jnp.int32), nxt, xs, wg, wu, wd)


def _slots(route, counts):
    counts = counts[0, :N_EXPERTS].astype(jnp.int32)
    padded = ((counts + MOE_BLOCK - 1) // MOE_BLOCK) * MOE_BLOCK
    pend = jnp.cumsum(padded)
    pstart = pend - padded
    bstart = jnp.arange(MOE_NBLOCKS, dtype=jnp.int32) * MOE_BLOCK
    block_expert = jnp.minimum(jnp.sum(bstart[:, None] >= pend[None, :], axis=1), N_EXPERTS - 1)
    block_expert = block_expert.astype(jnp.int32)
    mine = block_expert[:, None] == jnp.arange(N_EXPERTS, dtype=jnp.int32)[None, :]
    left = jnp.sum(jnp.where(mine, counts + pstart, 0), axis=1) - bstart
    block_count = jnp.clip(left, 0, MOE_BLOCK).astype(jnp.int32)
    base = jnp.pad(pstart.astype(F32), (0, LANES - N_EXPERTS)).reshape(1, LANES)
    return _slot_rows(route, base), block_expert, block_count


def _slot_rows_kernel(rt_ref, base_ref, o_ref):
    rt = rt_ref[...]
    lanef = _lane_iota(rt.shape).astype(F32)
    base = base_ref[...]
    dest = jnp.zeros(rt.shape, F32)
    for k in range(TOP_K):
        b = jnp.sum(jnp.where(lanef == rt[:, k:k + 1], base, 0.0), axis=1, keepdims=True)
        dest = jnp.where(lanef == k, b + rt[:, 2 * TOP_K + k:2 * TOP_K + k + 1], dest)
    dest_t = dest.T.astype(jnp.int32)
    for k in range(TOP_K):
        for p in range(N_PLANES):
            for c in range(rt.shape[0] // LANES):
                o_ref[k * N_PLANES + p, c:c + 1, :] = dest_t[k:k + 1, c * LANES:(c + 1) * LANES] + p * PLANE_ROWS


def _slot_rows(route, base):
    tm = 8 * LANES
    return pl.pallas_call(
        _slot_rows_kernel,
        grid=(N_TOK // tm,),
        in_specs=[pl.BlockSpec((tm, LANES), lambda i: (i, 0)), pl.BlockSpec((1, LANES), lambda i: (0, 0))],
        out_specs=pl.BlockSpec((TOP_K * N_PLANES, tm // LANES, LANES), lambda i: (0, i, 0)),
        out_shape=jax.ShapeDtypeStruct((TOP_K * N_PLANES, N_TOK // LANES, LANES), jnp.int32),
        compiler_params=_params("arbitrary"),
        name="slot_rows",
    )(route, base)


def _sc_workers():
    info = plsc.get_sparse_core_info()
    return info.num_cores, info.num_cores * info.num_subcores


def _sc_scatter2(src, idx, out_rows):
    n_win = src.shape[0] // SC_WINDOW
    nc, nw = _sc_workers()
    steps = n_win // nw
    mesh = plsc.VectorSubcoreMesh(core_axis_name="c", subcore_axis_name="s")

    @functools.partial(
        pl.kernel, mesh=mesh,
        out_type=jax.ShapeDtypeStruct((out_rows, LANES), src.dtype),
        scratch_types=[pltpu.VMEM((2 * steps, SC_WINDOW), jnp.int32),
                       pltpu.VMEM((SC_INFLIGHT, SC_WINDOW, LANES), src.dtype),
                       pltpu.SemaphoreType.DMA((SC_INFLIGHT,)), pltpu.SemaphoreType.DMA((SC_INFLIGHT,))],
        name="sc_dispatch_scatter",
    )
    def k(src_hbm, idx_hbm, out_hbm, idx_v, rows_v, lsem, wsem):
        first = (lax.axis_index("s") * nc + lax.axis_index("c")) * steps
        pltpu.sync_copy(idx_hbm.at[pl.ds(first, steps)], idx_v.at[pl.ds(0, steps)])
        pltpu.sync_copy(idx_hbm.at[pl.ds(n_win + first, steps)], idx_v.at[pl.ds(steps, steps)])

        @pl.loop(0, steps, step=SC_INFLIGHT)
        def _(j):
            loads = [pltpu.async_copy(src_hbm.at[pl.ds((first + j + b) * SC_WINDOW, SC_WINDOW)],
                                      rows_v.at[b], lsem.at[b]) for b in range(SC_INFLIGHT)]
            writes = []
            for b in range(SC_INFLIGHT):
                loads[b].wait()
                for half in range(TOP_K):
                    dst = out_hbm.at[idx_v.at[half * steps + j + b]]
                    writes.append(pltpu.async_copy(rows_v.at[b], dst, wsem.at[b]))
            for w in writes:
                w.wait()

    return k(src, idx)


def _sc_gather(table, idx):
    n_out = idx.shape[0] * SC_WINDOW
    nc, nw = _sc_workers()
    steps = n_out // nw // SC_WINDOW
    mesh = plsc.VectorSubcoreMesh(core_axis_name="c", subcore_axis_name="s")

    @functools.partial(
        pl.kernel, mesh=mesh,
        out_type=jax.ShapeDtypeStruct((n_out, LANES), table.dtype),
        scratch_types=[pltpu.VMEM((steps, SC_WINDOW), jnp.int32),
                       pltpu.VMEM((SC_INFLIGHT, SC_WINDOW, LANES), table.dtype),
                       pltpu.SemaphoreType.DMA((SC_INFLIGHT,)), pltpu.SemaphoreType.DMA((SC_INFLIGHT,))],
        name="sc_combine_gather",
    )
    def k(table_hbm, idx_hbm, out_hbm, idx_v, rows_v, gsem, wsem):
        first = (lax.axis_index("s") * nc + lax.axis_index("c")) * steps
        pltpu.sync_copy(idx_hbm.at[pl.ds(first, steps)], idx_v)

        @pl.loop(0, steps, step=SC_INFLIGHT)
        def _(j):
            gathers = [pltpu.async_copy(table_hbm.at[idx_v.at[j + b]], rows_v.at[b], gsem.at[b])
                       for b in range(SC_INFLIGHT)]
            writes = []
            for b in range(SC_INFLIGHT):
                gathers[b].wait()
                dst = out_hbm.at[pl.ds((first + j + b) * SC_WINDOW, SC_WINDOW)]
                writes.append(pltpu.async_copy(rows_v.at[b], dst, wsem.at[b]))
            for w in writes:
                w.wait()

    return k(table, idx)


def _final_kernel(x_ref, rt_ref, y_ref, gt_ref, g_ref, o_ref):
    x = x_ref[...] + gt_ref[...] * _combine(rt_ref, y_ref)
    ms = jnp.mean(x * x, axis=-1, keepdims=True)
    o_ref[...] = x * lax.rsqrt(ms + EPS) * g_ref[...]


def _final(x, moe, gate, g):
    tpb = SEQ // TM
    row = pl.BlockSpec((TM, D_MODEL), lambda i: (i, 0))
    return pl.pallas_call(
        _final_kernel,
        grid=(N_TOK // TM,),
        in_specs=[row, pl.BlockSpec((TM, LANES), lambda i: (i, 0)),
                  pl.BlockSpec((TOP_K, N_PLANES, TM, LANES), lambda i: (0, 0, i, 0)),
                  pl.BlockSpec((None, 1, D_MODEL), lambda i: (i // tpb, 0, 0)),
                  pl.BlockSpec((1, D_MODEL), lambda i: (0, 0))],
        out_specs=row,
        out_shape=jax.ShapeDtypeStruct((N_TOK, D_MODEL), F32),
        compiler_params=_params("arbitrary"),
        name="final_norm",
    )(x, moe[0], moe[1], gate, g.reshape(1, D_MODEL))


def kernel(x, c, positions, w_ada, b_ada, g_mix, w_in, b_forget, lambda_q1, lambda_k1, lambda_q2,
           lambda_k2, g_subln, g_fox_out, w_out, g_ffn, w_router_group, b_router_group,
           w_router_expert, b_router_expert, w_expert_gate, w_expert_up, w_expert_down, g_final):
    mod = _modulation(c, w_ada, b_ada)
    mod = mod.reshape(DEPTH, BATCH, 6, 1, D_MODEL)
    tables = _rope_tables(positions)
    pq = _forget_placement()
    w_in_bf = jnp.pad(w_in.astype(BF16), ((0, 0), (0, 0), (0, IN_COLS_PAD - IN_COLS)))
    xf = x.reshape(N_TOK, D_MODEL)
    moe = None
    gate = None
    for l in range(DEPTH):
        sh1, sc1, gt1, sh2, sc2, gt2 = (mod[l, :, j] for j in range(6))
        w_bf = (l, w_in_bf)
        bfp =jnp.pad(b_forget[l], (0, LANES - N_FOX_HEADS)).reshape(1, LANES)
        xf, (dq, dk, dv, fq, fk, fv) = _inproj(xf, moe, gate, sc1, sh1, g_mix[l], w_bf, bfp, tables, pq)

        lambda_init = 0.8 - 0.6 * float(np.exp(-0.3 * l))
        lamv = jnp.zeros((8, LANES), F32).at[0:4, 0:HEAD_DIM].set(
            jnp.stack([lambda_q1[l], lambda_k1[l], lambda_q2[l], lambda_k2[l]]))
        g_d = g_subln[l].reshape(1, LANES)
        g_f = jnp.concatenate([g_fox_out[l], g_fox_out[l]]).reshape(1, LANES)
        od = _attention(True, lambda_init, dq, dk, dv, g_d, lamv)
        of = _attention(False, lambda_init, fq, fk, fv, g_f, lamv)

        wr32 = jnp.pad(jnp.concatenate([w_router_group[l], w_router_expert[l]], axis=1),
                       ((0, 0), (0, LANES - N_GROUPS - N_EXPERTS)))
        wr_hi = wr32.astype(BF16)
        wr_lo = (wr32 - wr_hi.astype(F32)).astype(BF16)
        wr = jnp.concatenate([wr_hi, wr_lo], axis=1)
        br = jnp.pad(jnp.concatenate([b_router_group[l], b_router_expert[l]]),
                     (0, LANES - N_GROUPS - N_EXPERTS)).reshape(1, LANES)
        xf, h2, route, counts = _outproj(xf, od, of, gt1, sc2, sh2, g_ffn[l], w_out[l].astype(BF16),
                                         wr, br)

        rows, block_expert, block_count = _slots(route, counts)
        rows = rows.reshape(TOP_K * N_PLANES * N_TOK // SC_WINDOW, SC_WINDOW)
        xs = _sc_scatter2(h2.reshape(N_PLANES * N_TOK, LANES), rows, N_PLANES * PLANE_ROWS)
        ys = _experts(l, block_expert, block_count, xs.reshape(N_PLANES, PLANE_ROWS, LANES),
                      w_expert_gate, w_expert_up, w_expert_down)
        y2 = _sc_gather(ys.reshape(N_PLANES * PLANE_ROWS, LANES), rows)
        moe = (route, y2.reshape(TOP_K, N_PLANES, N_TOK, LANES))
        gate = gt2
    out = _final(xf, moe, gate, g_final)
    return out.reshape(BATCH, SEQ, D_MODEL)
```

```python
import functools

import numpy as np
import jax
import jax.numpy as jnp
from jax import lax
from jax.experimental import pallas as pl
from jax.experimental.pallas import tpu as pltpu
from jax.experimental.pallas import tpu_sc as plsc

D_MODEL = 1024
BATCH = 4
SEQ = 4096
DEPTH = 4
N_TOK = BATCH * SEQ

CHUNK = 64
HEAD_DIM = 64
N_DIFF_HEADS = 4
N_FOX_HEADS = 8
DIFF_WIDTH = 512
FOX_WIDTH = 512
IN_COLS = 3 * DIFF_WIDTH + 3 * FOX_WIDTH + N_FOX_HEADS
ROT_DIM = 16
ROPE_THETA = 500000.0
N_GROUPS = 4
EXPERTS_PER_GROUP = 8
N_EXPERTS = 32
TOP_K = 2
D_EXPERT = 512
EPS = 1e-6

LANES = 128
IN_COLS_PAD = 3200
FF_COL = 3 * DIFF_WIDTH + 3 * FOX_WIDTH
QK_WIDTH = 8 * LANES
TM = 512
OUTPROJ_TILES = 2
TQ = 512
ATTN_TQ = 256
N_LATE_CHAINS = 1
ONES_ROWS = 16
ATTN_ROWS = 64
LOG2E = 1.4426950408889634
MOE_BLOCK = 512
MOE_ROWS = N_TOK * TOP_K + N_EXPERTS * MOE_BLOCK
MOE_NBLOCKS = MOE_ROWS // MOE_BLOCK
PLANE_ROWS = MOE_ROWS
N_PLANES = D_MODEL // 2 // LANES
SC_WINDOW = 128
SC_INFLIGHT = 4
NEG = -1e30
VMEM_LIMIT = 56 * 1024 * 1024

F32 = jnp.float32
BF16 = jnp.bfloat16


def _bf16_round(x):
    return x.astype(BF16).astype(F32)


def _lane_iota(shape):
    return lax.broadcasted_iota(jnp.int32, shape, 1)


def _params(*sem):
    return pltpu.CompilerParams(dimension_semantics=sem, vmem_limit_bytes=VMEM_LIMIT)


def _pack_planes(y, o_ref, rows=slice(None)):
    bits = lax.bitcast_convert_type(_bf16_round(y), jnp.uint32)
    half = D_MODEL // 2
    word = bits[:, half:] | lax.shift_right_logical(bits[:, :half], jnp.uint32(16))
    word = lax.bitcast_convert_type(word, jnp.int32)
    for p in range(N_PLANES):
        o_ref[p, rows, :] = word[:, p * LANES:(p + 1) * LANES]


def _unpack_planes(planes):
    lo, hi = [], []
    for w in planes:
        u = lax.bitcast_convert_type(w, jnp.uint32)
        lo.append(lax.bitcast_convert_type(lax.shift_left(u, jnp.uint32(16)), F32))
        hi.append(lax.bitcast_convert_type(u & jnp.uint32(0xFFFF0000), F32))
    return jnp.concatenate(lo + hi, axis=1)


def _combine(route_ref, y_ref):
    rt = route_ref[...]
    y0 = _unpack_planes([y_ref[0, p] for p in range(N_PLANES)])
    y1 = _unpack_planes([y_ref[1, p] for p in range(N_PLANES)])
    return rt[:, 2:3] * y0 + rt[:, 3:4] * y1


def _mod_kernel(c_ref, w_ref, b_ref, o_ref):
    c = c_ref[...]
    cond = c / (1.0 + jnp.exp(-c))
    ch = cond.astype(BF16)
    cl = (cond - ch.astype(F32)).astype(BF16)
    w = w_ref[...]
    wh = w.astype(BF16)
    wl = (w - wh.astype(F32)).astype(BF16)
    acc = jnp.dot(ch, wh, preferred_element_type=F32)
    acc += jnp.dot(cl, wh, preferred_element_type=F32)
    acc += jnp.dot(ch, wl, preferred_element_type=F32)
    o_ref[...] = acc + b_ref[...]


def _modulation(c, w_ada, b_ada):
    rows = 16
    tn = 1536
    c_pad = jnp.zeros((rows, D_MODEL), F32).at[:BATCH].set(c)
    out = pl.pallas_call(
        _mod_kernel,
        grid=(DEPTH, 6 * D_MODEL // tn),
        in_specs=[
            pl.BlockSpec((rows, D_MODEL), lambda l, n: (0, 0)),
            pl.BlockSpec((None, D_MODEL, tn), lambda l, n: (l, 0, n)),
            pl.BlockSpec((None, 1, tn), lambda l, n: (l, 0, n)),
        ],
        out_specs=pl.BlockSpec((None, rows, tn), lambda l, n: (l, 0, n)),
        out_shape=jax.ShapeDtypeStruct((DEPTH, rows, 6 * D_MODEL), F32),
        compiler_params=_params("arbitrary", "arbitrary"),
        name="adaln_mod",
    )(c_pad, w_ada, b_ada.reshape(DEPTH, 1, 6 * D_MODEL))
    return out[:, :BATCH]


def _rope_kernel(pos_ref, inv_ref, c_ref, sa_ref, sb_ref):
    ang = pos_ref[...].astype(F32) * inv_ref[...]
    j = _lane_iota(ang.shape) % HEAD_DIM
    cosv = jnp.cos(ang)
    sinv = jnp.sin(ang)
    half = ROT_DIM // 2
    c_ref[...] = jnp.where(j < ROT_DIM, cosv, 1.0)
    sa_ref[...] = jnp.where(j < half, -sinv, 0.0)
    sb_ref[...] = jnp.where((j >= half) & (j < ROT_DIM), sinv, 0.0)


def _rope_tables(positions):
    half = ROT_DIM // 2
    inv = ROPE_THETA ** (-jnp.arange(0, ROT_DIM, 2, dtype=F32) / ROT_DIM)
    lane = np.arange(LANES)
    inv_lane = inv[(lane % HEAD_DIM) % half].reshape(1, LANES)
    spec = pl.BlockSpec((TM, LANES), lambda i: (i, 0))
    shape = jax.ShapeDtypeStruct((N_TOK, LANES), F32)
    return pl.pallas_call(
        _rope_kernel,
        grid=(N_TOK // TM,),
        in_specs=[pl.BlockSpec((TM, 1), lambda i: (i, 0)),
                  pl.BlockSpec((1, LANES), lambda i: (0, 0))],
        out_specs=[spec, spec, spec],
        out_shape=[shape, shape, shape],
        compiler_params=_params("arbitrary"),
        name="rope_tables",
    )(positions.reshape(N_TOK, 1), inv_lane)


def _rms_mod(x, g, sc, sh):
    ms = jnp.mean(x * x, axis=-1, keepdims=True)
    return (x * lax.rsqrt(ms + EPS) * g) * (1.0 + sc) + sh


def _inproj_kernel(fuse, *refs):
    if fuse:
        (x_ref, rt_ref, y_ref, gt_ref, sc_ref, sh_ref, g_ref, w_ref, bf_ref, c_ref, sa_ref, sb_ref,
         pq_ref, xo_ref, dq_ref, dk_ref, dv_ref, fq_ref, fk_ref, fv_ref, carry_ref) = refs
        x = x_ref[...] + gt_ref[...] * _combine(rt_ref, y_ref)
        xo_ref[...] = x
    else:
        (x_ref, sc_ref, sh_ref, g_ref, w_ref, bf_ref, c_ref, sa_ref, sb_ref,
         pq_ref, dq_ref, dk_ref, dv_ref, fq_ref, fk_ref, fv_ref, carry_ref) = refs
        x = x_ref[...]
    hb = _rms_mod(x, g_ref[...], sc_ref[...], sh_ref[...]).astype(BF16)

    @pl.when(pl.program_id(0) % (SEQ // TM) == 0)
    def _():
        carry_ref[...] = jnp.zeros_like(carry_ref)

    lane = _lane_iota((TM, LANES))
    nh = N_FOX_HEADS

    def pack3(a):
        hi = _bf16_round(a)
        r1 = a - hi
        mid = _bf16_round(r1)
        lo = _bf16_round(r1 - mid)
        return jnp.where(lane < nh, hi,
                         jnp.where(lane < 2 * nh, pltpu.roll(mid, nh, 1),
                                   jnp.where(lane < 3 * nh, pltpu.roll(lo, 2 * nh, 1), 0.0)))

    z = jnp.dot(hb, w_ref[:, FF_COL:FF_COL + LANES], preferred_element_type=F32) + bf_ref[...]

    low = lane < HEAD_DIM
    rc, rsa, rsb = c_ref[...], sa_ref[...], sb_ref[...]
    scale = HEAD_DIM ** -0.5 * LOG2E

    def split_store(chunk, o_ref, m, extra_a=None, extra_b=None):
        a = jnp.where(low, chunk, 0.0)
        b = jnp.where(low, pltpu.roll(chunk, HEAD_DIM, 1), 0.0)
        if extra_a is not None:
            a = a + extra_a
            b = b + extra_b
        o_ref[:, (2 * m) * LANES:(2 * m + 1) * LANES] = a.astype(BF16)
        o_ref[:, (2 * m + 1) * LANES:(2 * m + 2) * LANES] = b.astype(BF16)

    def rope(xc):
        return xc * rc + pltpu.roll(xc, LANES - ROT_DIM // 2, 1) * rsa + pltpu.roll(xc, ROT_DIM // 2, 1) * rsb

    pdq = jnp.dot(hb, w_ref[:, 0:DIFF_WIDTH], preferred_element_type=F32)
    for m in range(N_DIFF_HEADS):
        split_store(rope(pdq[:, m * LANES:(m + 1) * LANES]) * scale, dq_ref, m)
    pdk = jnp.dot(hb, w_ref[:, DIFF_WIDTH:2 * DIFF_WIDTH], preferred_element_type=F32)
    for m in range(N_DIFF_HEADS):
        split_store(rope(pdk[:, m * LANES:(m + 1) * LANES]), dk_ref, m)
    def store_values_t(pv, o_ref, width):
        ones = jnp.ones((ONES_ROWS, TM), BF16)
        for m in range(4):
            vt = pv[:, m * LANES:(m + 1) * LANES].T.astype(BF16)
            for i in range(LANES // width):
                o_ref[m * (LANES // width) + i, 0:width, :] = vt[i * width:(i + 1) * width]
                o_ref[m * (LANES // width) + i, width:width + ONES_ROWS, :] = ones

    store_values_t(jnp.dot(hb, w_ref[:, 2 * DIFF_WIDTH:3 * DIFF_WIDTH], preferred_element_type=F32),
                   dv_ref, 2 * HEAD_DIM)
    o = 3 * DIFF_WIDTH
    store_values_t(jnp.dot(hb, w_ref[:, o + 2 * FOX_WIDTH:o + 3 * FOX_WIDTH],
                           preferred_element_type=F32), fv_ref, HEAD_DIM)

    logf =jnp.minimum(z, 0.0) - jnp.log(1.0 + jnp.exp(-jnp.abs(z)))
    logf = jnp.where(lane < nh, logf, 0.0)
    row = lax.broadcasted_iota(jnp.int32, (TM, TM), 0)
    col = lax.broadcasted_iota(jnp.int32, (TM, TM), 1)
    tri = (row >= col).astype(BF16)
    r = jnp.dot(tri, pack3(logf).astype(BF16), preferred_element_type=F32)
    cs = r + pltpu.roll(r, LANES - nh, 1) + pltpu.roll(r, LANES - 2 * nh, 1)
    cf = jnp.where(lane < nh, cs + carry_ref[0:1, :], 0.0)
    carry_ref[...] = jnp.broadcast_to(cf[TM - 1:TM, :], carry_ref.shape)

    t3 = jnp.where(lane == 3 * nh, 1.0, pack3(cf * LOG2E)).astype(BF16)
    aug = jnp.dot(t3, pq_ref[...], preferred_element_type=F32)

    pfq =jnp.dot(hb, w_ref[:, o:o + FOX_WIDTH], preferred_element_type=F32)
    for m in range(N_FOX_HEADS // 2):
        split_store(pfq[:, m * LANES:(m + 1) * LANES] * scale, fq_ref, m,
                    aug[:, (2 * m) * LANES:(2 * m + 1) * LANES],
                    aug[:, (2 * m + 1) * LANES:(2 * m + 2) * LANES])
    pfk = jnp.dot(hb, w_ref[:, o + FOX_WIDTH:o + 2 * FOX_WIDTH], preferred_element_type=F32)
    for m in range(N_FOX_HEADS // 2):
        split_store(pfk[:, m * LANES:(m + 1) * LANES], fk_ref, m,
                    aug[:, QK_WIDTH + (2 * m) * LANES:QK_WIDTH + (2 * m + 1) * LANES],
                    aug[:, QK_WIDTH + (2 * m + 1) * LANES:QK_WIDTH + (2 * m + 2) * LANES])


def _forget_placement():
    nh = N_FOX_HEADS
    p = np.zeros((LANES, 2 * QK_WIDTH), np.float32)
    for h in range(nh):
        base_q = h * LANES + HEAD_DIM
        base_k = QK_WIDTH + h * LANES + HEAD_DIM
        for part in range(3):
            p[part * nh + h, base_q + part] = 1.0
            p[3 * nh, base_q + 3 + part] = 1.0
            p[3 * nh, base_k + part] = 1.0
            p[part * nh + h, base_k + 3 + part] = -1.0
    return jnp.asarray(p, BF16)


def _inproj(x, moe, gate, sc, sh, g, w_bf, b_forget, tables, pq):
    fuse = moe is not None
    tpb = SEQ // TM
    row = pl.BlockSpec((TM, D_MODEL), lambda i: (i, 0))
    per_batch = pl.BlockSpec((None, 1, D_MODEL), lambda i: (i // tpb, 0, 0))
    const = lambda shape: pl.BlockSpec(shape, lambda i: (0,) * len(shape))
    tab = pl.BlockSpec((TM, LANES), lambda i: (i, 0))
    in_specs = [row]
    args = [x]
    if fuse:
        in_specs += [tab, pl.BlockSpec((TOP_K, N_PLANES, TM, LANES), lambda i: (0, 0, i, 0)), per_batch]
        args += [moe[0], moe[1], gate]
    layer, w_all = w_bf
    in_specs += [per_batch, per_batch, const((1, D_MODEL)),
                 pl.BlockSpec((None, D_MODEL, IN_COLS_PAD), lambda i: (layer, 0, 0)),
                 const((1, LANES)), tab, tab, tab, const((LANES, 2 * QK_WIDTH))]
    args += [sc, sh, g.reshape(1, D_MODEL), w_all, b_forget, *tables, pq]
    wide = pl.BlockSpec((TM, QK_WIDTH), lambda i: (i, 0))
    def vspec(heads, width):
        rows = width + ONES_ROWS
        return (pl.BlockSpec((None, heads, None, rows, TM), lambda i: (i // tpb, 0, i % tpb, 0, 0)),
                jax.ShapeDtypeStruct((BATCH, heads, tpb, rows, TM), BF16))

    wide_s = jax.ShapeDtypeStruct((N_TOK, QK_WIDTH), BF16)
    dv_spec, dv_s = vspec(N_DIFF_HEADS, 2 * HEAD_DIM)
    fv_spec, fv_s = vspec(N_FOX_HEADS, HEAD_DIM)
    out_specs = [wide, wide, dv_spec, wide, wide, fv_spec]
    out_shape = [wide_s, wide_s, dv_s, wide_s, wide_s, fv_s]
    if fuse:
        out_specs = [row] + out_specs
        out_shape = [jax.ShapeDtypeStruct((N_TOK, D_MODEL), F32)] + out_shape
    outs = pl.pallas_call(
        functools.partial(_inproj_kernel, fuse),
        grid=(N_TOK // TM,),
        in_specs=in_specs,
        out_specs=out_specs,
        out_shape=out_shape,
        scratch_shapes=[pltpu.VMEM((8, LANES), F32)],
        compiler_params=_params("arbitrary"),
        name="norm_inproj",
    )(*args)
    if fuse:
        return outs[0], outs[1:]
    return x, outs


def _attn_kernel(diff, lambda_init, qa_ref, qb_ref, ka_ref, kb_ref, v_ref, g_ref, lam_ref, o_ref,
                 *scratch):
    nq = SEQ // TQ
    n_half = TQ // ATTN_TQ
    feat = 2 * HEAD_DIM if diff else HEAD_DIM
    chains = []
    for mi, (q_ref, k_ref) in enumerate(((qa_ref, ka_ref), (qb_ref, kb_ref))):
        for h in range(n_half):
            c = mi * n_half + h
            qt_sc, s_sc, p_sc, m_sc, a_sc, acc_sc = scratch[c::2 * n_half]
            vh = 0 if diff else mi
            chains.append((h, k_ref, qt_sc, s_sc, p_sc, m_sc, a_sc, acc_sc, q_ref, vh))
    order = [chains[mi * n_half + h] for h in range(n_half) for mi in range(2)]
    early, late = order[:-N_LATE_CHAINS], order[-N_LATE_CHAINS:]

    def load_queries(qi):
        for mi in range(2):
            q_ref = chains[mi * n_half][8]
            qt = q_ref[pl.ds(pl.multiple_of(qi * TQ, TQ), TQ), :].astype(F32).T.astype(BF16)
            for h in range(n_half):
                chains[mi * n_half + h][2][...] = qt[:, h * ATTN_TQ:(h + 1) * ATTN_TQ]

    def reset_state():
        for chain in chains:
            m_sc, _, acc_sc = chain[5:8]
            m_sc[...] = jnp.full(m_sc.shape, NEG, F32)
            acc_sc[...] = jnp.zeros(acc_sc.shape, F32)

    def n_keys(chain, masked):
        return (chain[0] + 1) * ATTN_TQ if masked else TQ

    def scores(chain, j, masked):
        h, k_ref, qt_sc, s_sc = chain[:4]
        nk = n_keys(chain, masked)
        off = pl.multiple_of(j * TQ, TQ)
        s = jnp.dot(k_ref[pl.ds(off, nk), :], qt_sc[...], preferred_element_type=F32)
        if masked:
            kk = lax.broadcasted_iota(jnp.int32, (nk, ATTN_TQ), 0)
            qq = h * ATTN_TQ + lax.broadcasted_iota(jnp.int32, (nk, ATTN_TQ), 1)
            s = jnp.where((kk // CHUNK <= qq // CHUNK) if diff else (kk <= qq), s, NEG)
        s_sc[0:nk, :] = s

    def softmax(chain, masked):
        s_sc, p_sc, m_sc, a_sc = chain[3:7]
        nk = n_keys(chain, masked)
        m_all = m_sc[...]
        m_parts = []
        for c0 in range(0, ATTN_TQ, LANES):
            cols = slice(c0, c0 + LANES)
            pm = s_sc[0:ATTN_ROWS, cols]
            for r0 in range(ATTN_ROWS, nk, ATTN_ROWS):
                pm = jnp.maximum(pm, s_sc[r0:r0 + ATTN_ROWS, cols])
            m_new = jnp.maximum(m_all[:, cols], jnp.max(pm, axis=0, keepdims=True))
            for r0 in range(0, nk, ATTN_ROWS):
                p = jnp.exp2(s_sc[r0:r0 + ATTN_ROWS, cols] - m_new)
                p_sc[r0:r0 + ATTN_ROWS, cols] = p.astype(BF16)
            m_parts.append(m_new)
        m_new = jnp.concatenate(m_parts, axis=1)
        a_sc[...] = jnp.exp2(m_all - m_new)
        m_sc[...] = m_new

    def values(chain, j, masked=False):
        p_sc, a_sc, acc_sc, vh = chain[4], chain[6], chain[7], chain[9]
        nk = n_keys(chain, masked)
        pv = jnp.dot(v_ref[vh, j, :, 0:nk], p_sc[0:nk, :], preferred_element_type=F32)
        acc_sc[...] = a_sc[...] * acc_sc[...] + pv

    def idle_late():
        for chain in late:
            chain[4][...] = jnp.zeros(chain[4].shape, BF16)
            chain[6][...] = jnp.ones(chain[6].shape, F32)

    def consume(j, cur_masked=False, nxt=None, nxt_masked=False, final=False, before_next=None):
        def open_late(chain):
            scores(chain, j, cur_masked)
            values(chain, jnp.maximum(j - 1, 0))

        open_late(late[0])
        for i, chain in enumerate(early):
            softmax(chain, cur_masked)
            if i == 0:
                for other in late[1:]:
                    open_late(other)
                if before_next is not None:
                    before_next()
            if nxt is not None:
                scores(chain, nxt, nxt_masked)
            values(chain, j, cur_masked)
        for chain in late:
            softmax(chain, cur_masked)
        if final:
            for chain in late:
                values(chain, j, cur_masked)

    def finalize(qi):
        ot = [jnp.concatenate([chains[mi * n_half + h][7][0:feat] / chains[mi * n_half + h][7][feat:feat + 1]
                               for h in range(n_half)], axis=1) for mi in range(2)]
        g = g_ref[...]
        rows = pl.ds(pl.multiple_of(qi * TQ, TQ), TQ)
        if diff:
            lv = lam_ref[...]
            lam = (jnp.exp(jnp.sum(lv[0:1] * lv[1:2], axis=1, keepdims=True))
                   - jnp.exp(jnp.sum(lv[2:3] * lv[3:4], axis=1, keepdims=True)) + lambda_init)
            o = (ot[0] - lam * ot[1]).T
            y = o * lax.rsqrt(jnp.mean(o * o, axis=1, keepdims=True) + EPS) * g
            o_ref[rows, :] = (y * (1.0 - lambda_init)).astype(o_ref.dtype)
        else:
            o = jnp.concatenate(ot, axis=0).T
            low = _lane_iota((TQ, LANES)) < HEAD_DIM
            sq = o * o
            msa = jnp.sum(jnp.where(low, sq, 0.0), axis=1, keepdims=True) / HEAD_DIM
            msb = jnp.sum(jnp.where(low, 0.0, sq), axis=1, keepdims=True) / HEAD_DIM
            inv = jnp.where(low, lax.rsqrt(msa + EPS), lax.rsqrt(msb + EPS))
            o_ref[rows, :] = (o * inv * g).astype(o_ref.dtype)

    load_queries(0)
    reset_state()
    idle_late()
    for chain in early:
        scores(chain, 0, True)

    @pl.loop(0, nq)
    def _(qi):
        n_plain = jnp.maximum(qi - 1, 0)

        def run(first, count):
            for i in range(count):
                consume(first + i, nxt=first + i + 1)

        @pl.loop(0, n_plain // 4)
        def _(t):
            run(4 * t, 4)

        done = (n_plain // 4) * 4

        @pl.when(n_plain - done >= 2)
        def _():
            run(done, 2)

        def last_blocks(to_next_tile):
            if to_next_tile:
                consume(qi, cur_masked=True, nxt=0, final=True, before_next=lambda: load_queries(qi + 1))
            else:
                consume(qi, cur_masked=True, final=True)
            finalize(qi)

        for to_next_tile in (True, False):
            more = (qi < nq - 1) if to_next_tile else (qi == nq - 1)

            last_tile_odd = (nq - 2) % 2 == 1
            for odd in ((True, False) if to_next_tile else (last_tile_odd,)):
                @pl.when(more & (qi > 0) & ((n_plain % 2 == 1) == odd))
                def _():
                    if odd:
                        run(qi - 2, 1)
                    consume(qi - 1, nxt=qi, nxt_masked=True)
                    last_blocks(to_next_tile)

            if to_next_tile:
                @pl.when(qi == 0)
                def _():
                    last_blocks(to_next_tile)

        reset_state()
        idle_late()


def _attention(diff, lambda_init, q, k, v, g, lamv):
    nq = SEQ // TQ
    kspec = lambda par: pl.BlockSpec((SEQ, LANES), lambda b, p: (b, 2 * p + par))
    return pl.pallas_call(
        functools.partial(_attn_kernel, diff, lambda_init),
        grid=(BATCH, 4),
        in_specs=[kspec(0), kspec(1), kspec(0), kspec(1),
                  pl.BlockSpec((None, v.shape[1] // 4, nq, v.shape[3], TQ), lambda b, p: (b, p, 0, 0, 0)),
                  pl.BlockSpec((1, LANES), lambda b, p: (0, 0)),
                  pl.BlockSpec((8, LANES), lambda b, p: (0, 0))],
        out_specs=pl.BlockSpec((SEQ, LANES), lambda b, p: (b, p)),
        out_shape=jax.ShapeDtypeStruct((N_TOK, DIFF_WIDTH), BF16),
        scratch_shapes=[pltpu.VMEM(shape, dt)
                        for shape, dt in (((LANES, ATTN_TQ), BF16), ((TQ, ATTN_TQ), F32),
                                          ((TQ, ATTN_TQ), BF16), ((1, ATTN_TQ), F32),
                                          ((1, ATTN_TQ), F32), ((v.shape[3], ATTN_TQ), F32))
                        for _ in range(2 * TQ // ATTN_TQ)],
        compiler_params=_params("arbitrary", "arbitrary"),
        name="diff_attention" if diff else "fox_attention",
    )(q, q, k, k, v, g, lamv)


def _outproj_kernel(x_ref, od_ref, of_ref, gt_ref, sc_ref, sh_ref, g_ref, wo_ref, wr_ref, br_ref,
                    x1_ref, h2_ref, rt_ref, cnt_ref, carry_ref):
    @pl.when(pl.program_id(0) == 0)
    def _():
        carry_ref[...] = jnp.zeros_like(carry_ref)

    tiles = [slice(t * TM, (t + 1) * TM) for t in range(OUTPROJ_TILES)]
    lane = _lane_iota((TM, LANES))
    lanef = lane.astype(F32)
    big = float(LANES)

    def project(rows):
        mix = jnp.dot(od_ref[rows, :], wo_ref[0:DIFF_WIDTH, :], preferred_element_type=F32)
        return mix + jnp.dot(of_ref[rows, :], wo_ref[DIFF_WIDTH:, :], preferred_element_type=F32)

    def normalise(rows, mix):
        x1 = x_ref[rows, :] + gt_ref[...] * mix
        x1_ref[rows, :] = x1
        h = _rms_mod(x1, g_ref[...], sc_ref[...], sh_ref[...])
        hh = h.astype(BF16)
        _pack_planes(h, h2_ref, rows)
        return hh, (h - hh.astype(F32)).astype(BF16)

    def router_logits(hh, hl):
        r1 = jnp.dot(hh, wr_ref[...], preferred_element_type=F32)
        r2 = jnp.dot(hl, wr_ref[:, 0:LANES], preferred_element_type=F32)
        return r1[:, 0:LANES] + r1[:, LANES:] + r2 + br_ref[...]

    def top_k(logits):
        isg = lane < N_GROUPS
        lg = jnp.where(isg, logits, NEG)
        mg = jnp.max(lg, axis=1, keepdims=True)
        sg = jnp.sum(jnp.where(isg, jnp.exp(lg - mg), 0.0), axis=1, keepdims=True)
        p_g = 1.0 / sg
        gsel = jnp.min(jnp.where(isg & (lg == mg), lanef, big), axis=1, keepdims=True)
        lo = N_GROUPS + gsel * EXPERTS_PER_GROUP
        ise = (lanef >= lo) & (lanef < lo + EXPERTS_PER_GROUP)
        le = jnp.where(ise, logits, NEG)
        t1 = jnp.max(le, axis=1, keepdims=True)
        i1 = jnp.min(jnp.where(ise & (le == t1), lanef, big), axis=1, keepdims=True)
        ise2 = ise & (lanef != i1)
        le2 = jnp.where(ise2, logits, NEG)
        t2 = jnp.max(le2, axis=1, keepdims=True)
        i2 = jnp.min(jnp.where(ise2 & (le2 == t2), lanef, big), axis=1, keepdims=True)
        d = jnp.exp(t2 - t1)
        return i1 - N_GROUPS, i2 - N_GROUPS, p_g / (1.0 + d), p_g * d / (1.0 + d)

    def earlier_in_tile(e1, e2):
        both = jnp.where((lanef == e1) | (lanef == e2), 1.0, 0.0)
        row = lax.broadcasted_iota(jnp.int32, (TM, TM), 0)
        col = lax.broadcasted_iota(jnp.int32, (TM, TM), 1)
        before = jnp.dot((row > col).astype(BF16), both.astype(BF16), preferred_element_type=F32)
        return before, jnp.sum(both, axis=0, keepdims=True)

    mixes = [project(rows) for rows in tiles]
    splits = [normalise(rows, mix) for rows, mix in zip(tiles, mixes)]
    logits = [router_logits(hh, hl) for hh, hl in splits]
    picks = [top_k(lg) for lg in logits]
    befores = [earlier_in_tile(e1, e2) for e1, e2, _, _ in picks]
    counts = carry_ref[0:1, :]
    for rows, (e1, e2, w1, w2), (before, added) in zip(tiles, picks, befores):
        before = before + counts
        rank1 = jnp.sum(jnp.where(lanef == e1, before, 0.0), axis=1, keepdims=True)
        rank2 = jnp.sum(jnp.where(lanef == e2, before, 0.0), axis=1, keepdims=True)
        out = jnp.zeros((TM, LANES), F32)
        for j, v in enumerate((e1, e2, w1, w2, rank1, rank2)):
            out = jnp.where(lane == j, v, out)
        rt_ref[rows, :] = out
        counts = counts + added
    carry_ref[...] = jnp.broadcast_to(counts, carry_ref.shape)
    cnt_ref[...] = jnp.broadcast_to(counts, cnt_ref.shape)


def _outproj(x, od, of, gt, sc, sh, g, wo_bf, wr, br):
    tm = OUTPROJ_TILES * TM
    tpb = SEQ // tm
    row = pl.BlockSpec((tm, D_MODEL), lambda i: (i, 0))
    half = pl.BlockSpec((tm, DIFF_WIDTH), lambda i: (i, 0))
    per_batch = pl.BlockSpec((None, 1, D_MODEL), lambda i: (i // tpb, 0, 0))
    const = lambda shape: pl.BlockSpec(shape, lambda i: (0,) * len(shape))
    return pl.pallas_call(
        _outproj_kernel,
        grid=(N_TOK // tm,),
        in_specs=[row, half, half, per_batch, per_batch, per_batch, const((1, D_MODEL)),
                  const((D_MODEL, D_MODEL)), const((D_MODEL, 2 * LANES)), const((1, LANES))],
        out_specs=[row, pl.BlockSpec((N_PLANES, tm, LANES), lambda i: (0, i, 0)),
                   pl.BlockSpec((tm, LANES), lambda i: (i, 0)), const((8, LANES))],
        out_shape=[jax.ShapeDtypeStruct((N_TOK, D_MODEL), F32),
                   jax.ShapeDtypeStruct((N_PLANES, N_TOK, LANES), jnp.int32),
                   jax.ShapeDtypeStruct((N_TOK, LANES), F32),
                   jax.ShapeDtypeStruct((8, LANES), F32)],
        scratch_shapes=[pltpu.VMEM((8, LANES), F32)],
        compiler_params=_params("arbitrary"),
        name="outproj_router",
    )(x, od, of, gt, sc, sh, g.reshape(1, D_MODEL), wo_bf, wr, br)


def _expert_kernel(layer, be_ref, cnt_ref, first_ref, slot_ref, next_ref, last_ref, xs_ref, wg_hbm, wu_hbm,
                   wd_hbm, ys_ref, wg_sc, wu_sc, wd_sc, wg_f32, wu_f32, wd_f32, sem):
    i = pl.program_id(0)
    cnt = cnt_ref[i]

    def weight_copies(expert, slot):
        return [pltpu.make_async_copy(hbm.at[layer, expert], buf.at[slot], sem.at[slot, n])
                for n, (hbm, buf) in enumerate(((wg_hbm, wg_f32), (wu_hbm, wu_f32), (wd_hbm, wd_f32)))]

    @pl.when(first_ref[i] == 1)
    def _():
        slot = slot_ref[i]

        @pl.when(i == 0)
        def _():
            for copy in weight_copies(be_ref[0], 0):
                copy.start()

        for copy in weight_copies(be_ref[i], slot):
            copy.wait()
        wg_sc[...] = wg_f32[slot].astype(BF16)
        wu_sc[...] = wu_f32[slot].astype(BF16)
        wd_sc[...] = wd_f32[slot].astype(BF16)

        @pl.when(next_ref[i] >= 0)
        def _():
            for copy in weight_copies(next_ref[i], 1 - slot):
                copy.start()

    def mlp(n_rows):
        rows = slice(0, n_rows)
        live = lax.broadcasted_iota(jnp.int32, (n_rows, LANES), 0) < cnt
        xb = _unpack_planes([jnp.where(live, xs_ref[p, rows, :], 0) for p in range(N_PLANES)]).astype(BF16)
        a = jnp.dot(xb, wg_sc[...], preferred_element_type=F32)
        u = jnp.dot(xb, wu_sc[...], preferred_element_type=F32)
        hid = (a / (1.0 + jnp.exp(-a)) * u).astype(BF16)
        _pack_planes(jnp.dot(hid, wd_sc[...], preferred_element_type=F32), ys_ref, rows)
        if n_rows < MOE_BLOCK:
            ys_ref[:, n_rows:, :] = jnp.zeros((N_PLANES, MOE_BLOCK - n_rows, LANES), ys_ref.dtype)

    half = MOE_BLOCK // 2

    @pl.when(cnt > half)
    def _():
        mlp(MOE_BLOCK)

    @pl.when((cnt > 0) & (cnt <= half))
    def _():
        mlp(half)


def _experts(layer, block_expert, block_count, xs, wg, wu, wd):
    idx = jnp.arange(MOE_NBLOCKS, dtype=jnp.int32)
    first = jnp.concatenate([jnp.ones((1,), jnp.bool_), block_expert[1:] != block_expert[:-1]])
    slot = (jnp.cumsum(first.astype(jnp.int32)) - 1) % 2
    later_first = (idx[None, :] > idx[:, None]) & first[None, :]
    nxt = jnp.min(jnp.where(later_first, block_expert[None, :], N_EXPERTS), axis=1)
    nxt = jnp.where(nxt == N_EXPERTS, -1, nxt).astype(jnp.int32)

    last_used = jnp.maximum(jnp.sum((block_count > 0).astype(jnp.int32)) - 1, 0).reshape(1)
    planes = pl.BlockSpec((N_PLANES, MOE_BLOCK, LANES),
                          lambda i, be, bc, fi, sl, nx, lu: (0, jnp.minimum(i, lu[0]), 0))
    hbm = pl.BlockSpec(memory_space=pl.ANY)
    grid_spec = pltpu.PrefetchScalarGridSpec(
        num_scalar_prefetch=6,
        grid=(MOE_NBLOCKS,),
        in_specs=[planes, hbm, hbm, hbm],
        out_specs=planes,
        scratch_shapes=[pltpu.VMEM((D_MODEL, D_EXPERT), BF16), pltpu.VMEM((D_MODEL, D_EXPERT), BF16),
                        pltpu.VMEM((D_EXPERT, D_MODEL), BF16),
                        pltpu.VMEM((2, D_MODEL, D_EXPERT), F32), pltpu.VMEM((2, D_MODEL, D_EXPERT), F32),
                        pltpu.VMEM((2, D_EXPERT, D_MODEL), F32), pltpu.SemaphoreType.DMA((2, 3))],
    )
    return pl.pallas_call(
        functools.partial(_expert_kernel, layer),
        grid_spec=grid_spec,
        out_shape=jax.ShapeDtypeStruct((N_PLANES, PLANE_ROWS, LANES), jnp.int32),
        compiler_params=_params("arbitrary"),
        name="expert_mlp",
    )(block_expert, block_count, first.astype(jnp.int32), slot.astype(jnp.int32), nxt, last_used,
      xs, wg, wu, wd)


def _slots(route, counts):
    counts = counts[0, :N_EXPERTS].astype(jnp.int32)
    padded = ((counts + MOE_BLOCK - 1) // MOE_BLOCK) * MOE_BLOCK
    pend = jnp.cumsum(padded)
    pstart = pend - padded
    bstart = jnp.arange(MOE_NBLOCKS, dtype=jnp.int32) * MOE_BLOCK
    block_expert = jnp.minimum(jnp.sum(bstart[:, None] >= pend[None, :], axis=1), N_EXPERTS - 1)
    block_expert = block_expert.astype(jnp.int32)
    mine = block_expert[:, None] == jnp.arange(N_EXPERTS, dtype=jnp.int32)[None, :]
    left = jnp.sum(jnp.where(mine, counts + pstart, 0), axis=1) - bstart
    block_count = jnp.clip(left, 0, MOE_BLOCK).astype(jnp.int32)
    base = jnp.pad(pstart.astype(F32), (0, LANES - N_EXPERTS)).reshape(1, LANES)
    return _slot_rows(route, base), block_expert, block_count


def _slot_rows_kernel(rt_ref, base_ref, o_ref):
    rt = rt_ref[...]
    lanef = _lane_iota(rt.shape).astype(F32)
    base = base_ref[...]
    dest = jnp.zeros(rt.shape, F32)
    for k in range(TOP_K):
        b = jnp.sum(jnp.where(lanef == rt[:, k:k + 1], base, 0.0), axis=1, keepdims=True)
        dest = jnp.where(lanef == k, b + rt[:, 2 * TOP_K + k:2 * TOP_K + k + 1], dest)
    dest_t = dest.T.astype(jnp.int32)
    for k in range(TOP_K):
        for p in range(N_PLANES):
            for c in range(rt.shape[0] // LANES):
                o_ref[k * N_PLANES + p, c:c + 1, :] = dest_t[k:k + 1, c * LANES:(c + 1) * LANES] + p * PLANE_ROWS


def _slot_rows(route, base):
    tm = 8 * LANES
    return pl.pallas_call(
        _slot_rows_kernel,
        grid=(N_TOK // tm,),
        in_specs=[pl.BlockSpec((tm, LANES), lambda i: (i, 0)), pl.BlockSpec((1, LANES), lambda i: (0, 0))],
        out_specs=pl.BlockSpec((TOP_K * N_PLANES, tm // LANES, LANES), lambda i: (0, i, 0)),
        out_shape=jax.ShapeDtypeStruct((TOP_K * N_PLANES, N_TOK // LANES, LANES), jnp.int32),
        compiler_params=_params("arbitrary"),
        name="slot_rows",
    )(route, base)


def _sc_workers():
    info = plsc.get_sparse_core_info()
    return info.num_cores, info.num_cores * info.num_subcores


def _sc_scatter2(src, idx, out_rows):
    n_win = src.shape[0] // SC_WINDOW
    nc, nw = _sc_workers()
    steps = n_win // nw
    mesh = plsc.VectorSubcoreMesh(core_axis_name="c", subcore_axis_name="s")

    @functools.partial(
        pl.kernel, mesh=mesh,
        out_type=jax.ShapeDtypeStruct((out_rows, LANES), src.dtype),
        scratch_types=[pltpu.VMEM((2 * steps, SC_WINDOW), jnp.int32),
                       pltpu.VMEM((SC_INFLIGHT, SC_WINDOW, LANES), src.dtype),
                       pltpu.SemaphoreType.DMA((SC_INFLIGHT,)), pltpu.SemaphoreType.DMA((SC_INFLIGHT,))],
        name="sc_dispatch_scatter",
    )
    def k(src_hbm, idx_hbm, out_hbm, idx_v, rows_v, lsem, wsem):
        first = (lax.axis_index("s") * nc + lax.axis_index("c")) * steps
        pltpu.sync_copy(idx_hbm.at[pl.ds(first, steps)], idx_v.at[pl.ds(0, steps)])
        pltpu.sync_copy(idx_hbm.at[pl.ds(n_win + first, steps)], idx_v.at[pl.ds(steps, steps)])

        @pl.loop(0, steps, step=SC_INFLIGHT)
        def _(j):
            loads = [pltpu.async_copy(src_hbm.at[pl.ds((first + j + b) * SC_WINDOW, SC_WINDOW)],
                                      rows_v.at[b], lsem.at[b]) for b in range(SC_INFLIGHT)]
            writes = []
            for b in range(SC_INFLIGHT):
                loads[b].wait()
                for half in range(TOP_K):
                    dst = out_hbm.at[idx_v.at[half * steps + j + b]]
                    writes.append(pltpu.async_copy(rows_v.at[b], dst, wsem.at[b]))
            for w in writes:
                w.wait()

    return k(src, idx)


def _sc_gather(table, idx):
    n_out = idx.shape[0] * SC_WINDOW
    nc, nw = _sc_workers()
    steps = n_out // nw // SC_WINDOW
    mesh = plsc.VectorSubcoreMesh(core_axis_name="c", subcore_axis_name="s")

    @functools.partial(
        pl.kernel, mesh=mesh,
        out_type=jax.ShapeDtypeStruct((n_out, LANES), table.dtype),
        scratch_types=[pltpu.VMEM((steps, SC_WINDOW), jnp.int32),
                       pltpu.VMEM((SC_INFLIGHT, SC_WINDOW, LANES), table.dtype),
                       pltpu.SemaphoreType.DMA((SC_INFLIGHT,)), pltpu.SemaphoreType.DMA((SC_INFLIGHT,))],
        name="sc_combine_gather",
    )
    def k(table_hbm, idx_hbm, out_hbm, idx_v, rows_v, gsem, wsem):
        first = (lax.axis_index("s") * nc + lax.axis_index("c")) * steps
        pltpu.sync_copy(idx_hbm.at[pl.ds(first, steps)], idx_v)

        @pl.loop(0, steps, step=SC_INFLIGHT)
        def _(j):
            gathers = [pltpu.async_copy(table_hbm.at[idx_v.at[j + b]], rows_v.at[b], gsem.at[b])
                       for b in range(SC_INFLIGHT)]
            writes = []
            for b in range(SC_INFLIGHT):
                gathers[b].wait()
                dst = out_hbm.at[pl.ds((first + j + b) * SC_WINDOW, SC_WINDOW)]
                writes.append(pltpu.async_copy(rows_v.at[b], dst, wsem.at[b]))
            for w in writes:
                w.wait()

    return k(table, idx)


def _final_kernel(x_ref, rt_ref, y_ref, gt_ref, g_ref, o_ref):
    x = x_ref[...] + gt_ref[...] * _combine(rt_ref, y_ref)
    ms = jnp.mean(x * x, axis=-1, keepdims=True)
    o_ref[...] = x * lax.rsqrt(ms + EPS) * g_ref[...]


def _final(x, moe, gate, g):
    tpb = SEQ // TM
    row = pl.BlockSpec((TM, D_MODEL), lambda i: (i, 0))
    return pl.pallas_call(
        _final_kernel,
        grid=(N_TOK // TM,),
        in_specs=[row, pl.BlockSpec((TM, LANES), lambda i: (i, 0)),
                  pl.BlockSpec((TOP_K, N_PLANES, TM, LANES), lambda i: (0, 0, i, 0)),
                  pl.BlockSpec((None, 1, D_MODEL), lambda i: (i // tpb, 0, 0)),
                  pl.BlockSpec((1, D_MODEL), lambda i: (0, 0))],
        out_specs=row,
        out_shape=jax.ShapeDtypeStruct((N_TOK, D_MODEL), F32),
        compiler_params=_params("arbitrary"),
        name="final_norm",
    )(x, moe[0], moe[1], gate, g.reshape(1, D_MODEL))


def kernel(x, c, positions, w_ada, b_ada, g_mix, w_in, b_forget, lambda_q1, lambda_k1, lambda_q2,
           lambda_k2, g_subln, g_fox_out, w_out, g_ffn, w_router_group, b_router_group,
           w_router_expert, b_router_expert, w_expert_gate, w_expert_up, w_expert_down, g_final):
    mod = _modulation(c, w_ada, b_ada)
    mod = mod.reshape(DEPTH, BATCH, 6, 1, D_MODEL)
    tables = _rope_tables(positions)
    pq = _forget_placement()
    w_in_bf = jnp.pad(w_in.astype(BF16), ((0, 0), (0, 0), (0, IN_COLS_PAD - IN_COLS)))
    xf = x.reshape(N_TOK, D_MODEL)
    moe = None
    gate = None
    for l in range(DEPTH):
        sh1, sc1, gt1, sh2, sc2, gt2 = (mod[l, :, j] for j in range(6))
        w_bf = (l, w_in_bf)
        bfp =jnp.pad(b_forget[l], (0, LANES - N_FOX_HEADS)).reshape(1, LANES)
        xf, (dq, dk, dv, fq, fk, fv) = _inproj(xf, moe, gate, sc1, sh1, g_mix[l], w_bf, bfp, tables, pq)

        lambda_init = 0.8 - 0.6 * float(np.exp(-0.3 * l))
        lamv = jnp.zeros((8, LANES), F32).at[0:4, 0:HEAD_DIM].set(
            jnp.stack([lambda_q1[l], lambda_k1[l], lambda_q2[l], lambda_k2[l]]))
        g_d = g_subln[l].reshape(1, LANES)
        g_f = jnp.concatenate([g_fox_out[l], g_fox_out[l]]).reshape(1, LANES)
        od = _attention(True, lambda_init, dq, dk, dv, g_d, lamv)
        of = _attention(False, lambda_init, fq, fk, fv, g_f, lamv)

        wr32 = jnp.pad(jnp.concatenate([w_router_group[l], w_router_expert[l]], axis=1),
                       ((0, 0), (0, LANES - N_GROUPS - N_EXPERTS)))
        wr_hi = wr32.astype(BF16)
        wr_lo = (wr32 - wr_hi.astype(F32)).astype(BF16)
        wr = jnp.concatenate([wr_hi, wr_lo], axis=1)
        br = jnp.pad(jnp.concatenate([b_router_group[l], b_router_expert[l]]),
                     (0, LANES - N_GROUPS - N_EXPERTS)).reshape(1, LANES)
        xf, h2, route, counts = _outproj(xf, od, of, gt1, sc2, sh2, g_ffn[l], w_out[l].astype(BF16),
                                         wr, br)

        rows, block_expert, block_count = _slots(route, counts)
        rows = rows.reshape(TOP_K * N_PLANES * N_TOK // SC_WINDOW, SC_WINDOW)
        xs = _sc_scatter2(h2.reshape(N_PLANES * N_TOK, LANES), rows, N_PLANES * PLANE_ROWS)
        ys = _experts(l, block_expert, block_count, xs.reshape(N_PLANES, PLANE_ROWS, LANES),
                      w_expert_gate, w_expert_up, w_expert_down)
        y2 = _sc_gather(ys.reshape(N_PLANES * PLANE_ROWS, LANES), rows)
        moe = (route, y2.reshape(TOP_K, N_PLANES, N_TOK, LANES))
        gate = gt2
    out = _final(xf, moe, gate, g_final)
    return out.reshape(BATCH, SEQ, D_MODEL)
```

```python
import functools

import numpy as np
import jax
import jax.numpy as jnp
from jax import lax
from jax.experimental import pallas as pl
from jax.experimental.pallas import tpu as pltpu
from jax.experimental.pallas import tpu_sc as plsc

D_MODEL = 1024
BATCH = 4
SEQ = 4096
DEPTH = 4
N_TOK = BATCH * SEQ

CHUNK = 64
HEAD_DIM = 64
N_DIFF_HEADS = 4
N_FOX_HEADS = 8
DIFF_WIDTH = 512
FOX_WIDTH = 512
IN_COLS = 3 * DIFF_WIDTH + 3 * FOX_WIDTH + N_FOX_HEADS
ROT_DIM = 16
ROPE_THETA = 500000.0
N_GROUPS = 4
EXPERTS_PER_GROUP = 8
N_EXPERTS = 32
TOP_K = 2
D_EXPERT = 512
EPS = 1e-6

LANES = 128
IN_COLS_PAD = 3200
FF_COL = 3 * DIFF_WIDTH + 3 * FOX_WIDTH
QK_WIDTH = 8 * LANES
TM = 512
OUTPROJ_TILES = 2
TQ = 512
ATTN_TQ = 256
N_LATE_CHAINS = 1
ONES_ROWS = 16
ATTN_ROWS = 128
LOG2E = 1.4426950408889634
MOE_BLOCK = 512
MOE_ROWS = N_TOK * TOP_K + N_EXPERTS * MOE_BLOCK
MOE_NBLOCKS = MOE_ROWS // MOE_BLOCK
PLANE_ROWS = MOE_ROWS
N_PLANES = D_MODEL // 2 // LANES
SC_WINDOW = 128
SC_INFLIGHT = 4
NEG = -1e30
VMEM_LIMIT = 56 * 1024 * 1024

F32 = jnp.float32
BF16 = jnp.bfloat16


def _bf16_round(x):
    return x.astype(BF16).astype(F32)


def _lane_iota(shape):
    return lax.broadcasted_iota(jnp.int32, shape, 1)


def _params(*sem):
    return pltpu.CompilerParams(dimension_semantics=sem, vmem_limit_bytes=VMEM_LIMIT)


def _pack_planes(y, o_ref, rows=slice(None)):
    bits = lax.bitcast_convert_type(_bf16_round(y), jnp.uint32)
    half = D_MODEL // 2
    word = bits[:, half:] | lax.shift_right_logical(bits[:, :half], jnp.uint32(16))
    word = lax.bitcast_convert_type(word, jnp.int32)
    for p in range(N_PLANES):
        o_ref[p, rows, :] = word[:, p * LANES:(p + 1) * LANES]


def _unpack_planes(planes):
    lo, hi = [], []
    for w in planes:
        u = lax.bitcast_convert_type(w, jnp.uint32)
        lo.append(lax.bitcast_convert_type(lax.shift_left(u, jnp.uint32(16)), F32))
        hi.append(lax.bitcast_convert_type(u & jnp.uint32(0xFFFF0000), F32))
    return jnp.concatenate(lo + hi, axis=1)


def _combine(route_ref, y_ref):
    rt = route_ref[...]
    y0 = _unpack_planes([y_ref[0, p] for p in range(N_PLANES)])
    y1 = _unpack_planes([y_ref[1, p] for p in range(N_PLANES)])
    return rt[:, 2:3] * y0 + rt[:, 3:4] * y1


def _mod_kernel(c_ref, w_ref, b_ref, o_ref):
    c = c_ref[...]
    cond = c / (1.0 + jnp.exp(-c))
    ch = cond.astype(BF16)
    cl = (cond - ch.astype(F32)).astype(BF16)
    w = w_ref[...]
    wh = w.astype(BF16)
    wl = (w - wh.astype(F32)).astype(BF16)
    acc = jnp.dot(ch, wh, preferred_element_type=F32)
    acc += jnp.dot(cl, wh, preferred_element_type=F32)
    acc += jnp.dot(ch, wl, preferred_element_type=F32)
    o_ref[...] = acc + b_ref[...]


def _modulation(c, w_ada, b_ada):
    rows = 16
    tn = 1536
    c_pad = jnp.zeros((rows, D_MODEL), F32).at[:BATCH].set(c)
    out = pl.pallas_call(
        _mod_kernel,
        grid=(DEPTH, 6 * D_MODEL // tn),
        in_specs=[
            pl.BlockSpec((rows, D_MODEL), lambda l, n: (0, 0)),
            pl.BlockSpec((None, D_MODEL, tn), lambda l, n: (l, 0, n)),
            pl.BlockSpec((None, 1, tn), lambda l, n: (l, 0, n)),
        ],
        out_specs=pl.BlockSpec((None, rows, tn), lambda l, n: (l, 0, n)),
        out_shape=jax.ShapeDtypeStruct((DEPTH, rows, 6 * D_MODEL), F32),
        compiler_params=_params("arbitrary", "arbitrary"),
        name="adaln_mod",
    )(c_pad, w_ada, b_ada.reshape(DEPTH, 1, 6 * D_MODEL))
    return out[:, :BATCH]


def _rope_kernel(pos_ref, inv_ref, c_ref, sa_ref, sb_ref):
    ang = pos_ref[...].astype(F32) * inv_ref[...]
    j = _lane_iota(ang.shape) % HEAD_DIM
    cosv = jnp.cos(ang)
    sinv = jnp.sin(ang)
    half = ROT_DIM // 2
    c_ref[...] = jnp.where(j < ROT_DIM, cosv, 1.0)
    sa_ref[...] = jnp.where(j < half, -sinv, 0.0)
    sb_ref[...] = jnp.where((j >= half) & (j < ROT_DIM), sinv, 0.0)


def _rope_tables(positions):
    half = ROT_DIM // 2
    inv = ROPE_THETA ** (-jnp.arange(0, ROT_DIM, 2, dtype=F32) / ROT_DIM)
    lane = np.arange(LANES)
    inv_lane = inv[(lane % HEAD_DIM) % half].reshape(1, LANES)
    spec = pl.BlockSpec((TM, LANES), lambda i: (i, 0))
    shape = jax.ShapeDtypeStruct((N_TOK, LANES), F32)
    return pl.pallas_call(
        _rope_kernel,
        grid=(N_TOK // TM,),
        in_specs=[pl.BlockSpec((TM, 1), lambda i: (i, 0)),
                  pl.BlockSpec((1, LANES), lambda i: (0, 0))],
        out_specs=[spec, spec, spec],
        out_shape=[shape, shape, shape],
        compiler_params=_params("arbitrary"),
        name="rope_tables",
    )(positions.reshape(N_TOK, 1), inv_lane)


def _rms_mod(x, g, sc, sh):
    ms = jnp.mean(x * x, axis=-1, keepdims=True)
    return (x * lax.rsqrt(ms + EPS) * g) * (1.0 + sc) + sh


def _inproj_kernel(fuse, *refs):
    if fuse:
        (x_ref, rt_ref, y_ref, gt_ref, sc_ref, sh_ref, g_ref, w_ref, bf_ref, c_ref, sa_ref, sb_ref,
         pq_ref, xo_ref, dq_ref, dk_ref, dv_ref, fq_ref, fk_ref, fv_ref, carry_ref) = refs
        x = x_ref[...] + gt_ref[...] * _combine(rt_ref, y_ref)
        xo_ref[...] = x
    else:
        (x_ref, sc_ref, sh_ref, g_ref, w_ref, bf_ref, c_ref, sa_ref, sb_ref,
         pq_ref, dq_ref, dk_ref, dv_ref, fq_ref, fk_ref, fv_ref, carry_ref) = refs
        x = x_ref[...]
    hb = _rms_mod(x, g_ref[...], sc_ref[...], sh_ref[...]).astype(BF16)

    @pl.when(pl.program_id(0) % (SEQ // TM) == 0)
    def _():
        carry_ref[...] = jnp.zeros_like(carry_ref)

    lane = _lane_iota((TM, LANES))
    nh = N_FOX_HEADS

    def pack3(a):
        hi = _bf16_round(a)
        r1 = a - hi
        mid = _bf16_round(r1)
        lo = _bf16_round(r1 - mid)
        return jnp.where(lane < nh, hi,
                         jnp.where(lane < 2 * nh, pltpu.roll(mid, nh, 1),
                                   jnp.where(lane < 3 * nh, pltpu.roll(lo, 2 * nh, 1), 0.0)))

    z = jnp.dot(hb, w_ref[:, FF_COL:FF_COL + LANES], preferred_element_type=F32) + bf_ref[...]

    low = lane < HEAD_DIM
    rc, rsa, rsb = c_ref[...], sa_ref[...], sb_ref[...]
    scale = HEAD_DIM ** -0.5 * LOG2E

    def split_store(chunk, o_ref, m, extra_a=None, extra_b=None):
        a = jnp.where(low, chunk, 0.0)
        b = jnp.where(low, pltpu.roll(chunk, HEAD_DIM, 1), 0.0)
        if extra_a is not None:
            a = a + extra_a
            b = b + extra_b
        o_ref[:, (2 * m) * LANES:(2 * m + 1) * LANES] = a.astype(BF16)
        o_ref[:, (2 * m + 1) * LANES:(2 * m + 2) * LANES] = b.astype(BF16)

    def rope(xc):
        return xc * rc + pltpu.roll(xc, LANES - ROT_DIM // 2, 1) * rsa + pltpu.roll(xc, ROT_DIM // 2, 1) * rsb

    pdq = jnp.dot(hb, w_ref[:, 0:DIFF_WIDTH], preferred_element_type=F32)
    for m in range(N_DIFF_HEADS):
        split_store(rope(pdq[:, m * LANES:(m + 1) * LANES]) * scale, dq_ref, m)
    pdk = jnp.dot(hb, w_ref[:, DIFF_WIDTH:2 * DIFF_WIDTH], preferred_element_type=F32)
    for m in range(N_DIFF_HEADS):
        split_store(rope(pdk[:, m * LANES:(m + 1) * LANES]), dk_ref, m)
    def store_values_t(pv, o_ref, width):
        ones = jnp.ones((ONES_ROWS, TM), BF16)
        for m in range(4):
            vt = pv[:, m * LANES:(m + 1) * LANES].T.astype(BF16)
            for i in range(LANES // width):
                o_ref[m * (LANES // width) + i, 0:width, :] = vt[i * width:(i + 1) * width]
                o_ref[m * (LANES // width) + i, width:width + ONES_ROWS, :] = ones

    store_values_t(jnp.dot(hb, w_ref[:, 2 * DIFF_WIDTH:3 * DIFF_WIDTH], preferred_element_type=F32),
                   dv_ref, 2 * HEAD_DIM)
    o = 3 * DIFF_WIDTH
    store_values_t(jnp.dot(hb, w_ref[:, o + 2 * FOX_WIDTH:o + 3 * FOX_WIDTH],
                           preferred_element_type=F32), fv_ref, HEAD_DIM)

    logf =jnp.minimum(z, 0.0) - jnp.log(1.0 + jnp.exp(-jnp.abs(z)))
    logf = jnp.where(lane < nh, logf, 0.0)
    row = lax.broadcasted_iota(jnp.int32, (TM, TM), 0)
    col = lax.broadcasted_iota(jnp.int32, (TM, TM), 1)
    tri = (row >= col).astype(BF16)
    r = jnp.dot(tri, pack3(logf).astype(BF16), preferred_element_type=F32)
    cs = r + pltpu.roll(r, LANES - nh, 1) + pltpu.roll(r, LANES - 2 * nh, 1)
    cf = jnp.where(lane < nh, cs + carry_ref[0:1, :], 0.0)
    carry_ref[...] = jnp.broadcast_to(cf[TM - 1:TM, :], carry_ref.shape)

    t3 = jnp.where(lane == 3 * nh, 1.0, pack3(cf * LOG2E)).astype(BF16)
    aug = jnp.dot(t3, pq_ref[...], preferred_element_type=F32)

    pfq =jnp.dot(hb, w_ref[:, o:o + FOX_WIDTH], preferred_element_type=F32)
    for m in range(N_FOX_HEADS // 2):
        split_store(pfq[:, m * LANES:(m + 1) * LANES] * scale, fq_ref, m,
                    aug[:, (2 * m) * LANES:(2 * m + 1) * LANES],
                    aug[:, (2 * m + 1) * LANES:(2 * m + 2) * LANES])
    pfk = jnp.dot(hb, w_ref[:, o + FOX_WIDTH:o + 2 * FOX_WIDTH], preferred_element_type=F32)
    for m in range(N_FOX_HEADS // 2):
        split_store(pfk[:, m * LANES:(m + 1) * LANES], fk_ref, m,
                    aug[:, QK_WIDTH + (2 * m) * LANES:QK_WIDTH + (2 * m + 1) * LANES],
                    aug[:, QK_WIDTH + (2 * m + 1) * LANES:QK_WIDTH + (2 * m + 2) * LANES])


def _forget_placement():
    nh = N_FOX_HEADS
    p = np.zeros((LANES, 2 * QK_WIDTH), np.float32)
    for h in range(nh):
        base_q = h * LANES + HEAD_DIM
        base_k = QK_WIDTH + h * LANES + HEAD_DIM
        for part in range(3):
            p[part * nh + h, base_q + part] = 1.0
            p[3 * nh, base_q + 3 + part] = 1.0
            p[3 * nh, base_k + part] = 1.0
            p[part * nh + h, base_k + 3 + part] = -1.0
    return jnp.asarray(p, BF16)


def _inproj(x, moe, gate, sc, sh, g, w_bf, b_forget, tables, pq):
    fuse = moe is not None
    tpb = SEQ // TM
    row = pl.BlockSpec((TM, D_MODEL), lambda i: (i, 0))
    per_batch = pl.BlockSpec((None, 1, D_MODEL), lambda i: (i // tpb, 0, 0))
    const = lambda shape: pl.BlockSpec(shape, lambda i: (0,) * len(shape))
    tab = pl.BlockSpec((TM, LANES), lambda i: (i, 0))
    in_specs = [row]
    args = [x]
    if fuse:
        in_specs += [tab, pl.BlockSpec((TOP_K, N_PLANES, TM, LANES), lambda i: (0, 0, i, 0)), per_batch]
        args += [moe[0], moe[1], gate]
    layer, w_all = w_bf
    in_specs += [per_batch, per_batch, const((1, D_MODEL)),
                 pl.BlockSpec((None, D_MODEL, IN_COLS_PAD), lambda i: (layer, 0, 0)),
                 const((1, LANES)), tab, tab, tab, const((LANES, 2 * QK_WIDTH))]
    args += [sc, sh, g.reshape(1, D_MODEL), w_all, b_forget, *tables, pq]
    wide = pl.BlockSpec((TM, QK_WIDTH), lambda i: (i, 0))
    def vspec(heads, width):
        rows = width + ONES_ROWS
        return (pl.BlockSpec((None, heads, None, rows, TM), lambda i: (i // tpb, 0, i % tpb, 0, 0)),
                jax.ShapeDtypeStruct((BATCH, heads, tpb, rows, TM), BF16))

    wide_s = jax.ShapeDtypeStruct((N_TOK, QK_WIDTH), BF16)
    dv_spec, dv_s = vspec(N_DIFF_HEADS, 2 * HEAD_DIM)
    fv_spec, fv_s = vspec(N_FOX_HEADS, HEAD_DIM)
    out_specs = [wide, wide, dv_spec, wide, wide, fv_spec]
    out_shape = [wide_s, wide_s, dv_s, wide_s, wide_s, fv_s]
    if fuse:
        out_specs = [row] + out_specs
        out_shape = [jax.ShapeDtypeStruct((N_TOK, D_MODEL), F32)] + out_shape
    outs = pl.pallas_call(
        functools.partial(_inproj_kernel, fuse),
        grid=(N_TOK // TM,),
        in_specs=in_specs,
        out_specs=out_specs,
        out_shape=out_shape,
        scratch_shapes=[pltpu.VMEM((8, LANES), F32)],
        compiler_params=_params("arbitrary"),
        name="norm_inproj",
    )(*args)
    if fuse:
        return outs[0], outs[1:]
    return x, outs


def _attn_kernel(diff, lambda_init, qa_ref, qb_ref, ka_ref, kb_ref, v_ref, g_ref, lam_ref, o_ref,
                 *scratch):
    nq = SEQ // TQ
    n_half = TQ // ATTN_TQ
    feat = 2 * HEAD_DIM if diff else HEAD_DIM
    chains = []
    for mi, (q_ref, k_ref) in enumerate(((qa_ref, ka_ref), (qb_ref, kb_ref))):
        for h in range(n_half):
            c = mi * n_half + h
            qt_sc, s_sc, p_sc, m_sc, a_sc, acc_sc = scratch[c::2 * n_half]
            vh = 0 if diff else mi
            chains.append((h, k_ref, qt_sc, s_sc, p_sc, m_sc, a_sc, acc_sc, q_ref, vh))
    order = [chains[mi * n_half + h] for h in range(n_half) for mi in range(2)]
    early, late = order[:-N_LATE_CHAINS], order[-N_LATE_CHAINS:]

    def load_queries(qi):
        for mi in range(2):
            q_ref = chains[mi * n_half][8]
            qt = q_ref[pl.ds(pl.multiple_of(qi * TQ, TQ), TQ), :].astype(F32).T.astype(BF16)
            for h in range(n_half):
                chains[mi * n_half + h][2][...] = qt[:, h * ATTN_TQ:(h + 1) * ATTN_TQ]

    def reset_state():
        for chain in chains:
            m_sc, _, acc_sc = chain[5:8]
            m_sc[...] = jnp.full(m_sc.shape, NEG, F32)
            acc_sc[...] = jnp.zeros(acc_sc.shape, F32)

    def n_keys(chain, masked):
        return (chain[0] + 1) * ATTN_TQ if masked else TQ

    def scores(chain, j, masked):
        h, k_ref, qt_sc, s_sc = chain[:4]
        nk = n_keys(chain, masked)
        off = pl.multiple_of(j * TQ, TQ)
        s = jnp.dot(k_ref[pl.ds(off, nk), :], qt_sc[...], preferred_element_type=F32)
        if masked:
            kk = lax.broadcasted_iota(jnp.int32, (nk, ATTN_TQ), 0)
            qq = h * ATTN_TQ + lax.broadcasted_iota(jnp.int32, (nk, ATTN_TQ), 1)
            s = jnp.where((kk // CHUNK <= qq // CHUNK) if diff else (kk <= qq), s, NEG)
        s_sc[0:nk, :] = s

    def softmax(chain, masked):
        s_sc, p_sc, m_sc, a_sc = chain[3:7]
        nk = n_keys(chain, masked)
        m_all = m_sc[...]
        m_parts = []
        for c0 in range(0, ATTN_TQ, LANES):
            cols = slice(c0, c0 + LANES)
            pm = s_sc[0:ATTN_ROWS, cols]
            for r0 in range(ATTN_ROWS, nk, ATTN_ROWS):
                pm = jnp.maximum(pm, s_sc[r0:r0 + ATTN_ROWS, cols])
            m_new = jnp.maximum(m_all[:, cols], jnp.max(pm, axis=0, keepdims=True))
            for r0 in range(0, nk, ATTN_ROWS):
                p = jnp.exp2(s_sc[r0:r0 + ATTN_ROWS, cols] - m_new)
                p_sc[r0:r0 + ATTN_ROWS, cols] = p.astype(BF16)
            m_parts.append(m_new)
        m_new = jnp.concatenate(m_parts, axis=1)
        a_sc[...] = jnp.exp2(m_all - m_new)
        m_sc[...] = m_new

    def values(chain, j, masked=False):
        p_sc, a_sc, acc_sc, vh = chain[4], chain[6], chain[7], chain[9]
        nk = n_keys(chain, masked)
        pv = jnp.dot(v_ref[vh, j, :, 0:nk], p_sc[0:nk, :], preferred_element_type=F32)
        acc_sc[...] = a_sc[...] * acc_sc[...] + pv

    def idle_late():
        for chain in late:
            chain[4][...] = jnp.zeros(chain[4].shape, BF16)
            chain[6][...] = jnp.ones(chain[6].shape, F32)

    def consume(j, cur_masked=False, nxt=None, nxt_masked=False, final=False, before_next=None):
        def open_late(chain):
            scores(chain, j, cur_masked)
            values(chain, jnp.maximum(j - 1, 0))

        open_late(late[0])
        for i, chain in enumerate(early):
            softmax(chain, cur_masked)
            if i == 0:
                for other in late[1:]:
                    open_late(other)
                if before_next is not None:
                    before_next()
            if nxt is not None:
                scores(chain, nxt, nxt_masked)
            values(chain, j, cur_masked)
        for chain in late:
            softmax(chain, cur_masked)
        if final:
            for chain in late:
                values(chain, j, cur_masked)

    def finalize(qi):
        ot = [jnp.concatenate([chains[mi * n_half + h][7][0:feat] / chains[mi * n_half + h][7][feat:feat + 1]
                               for h in range(n_half)], axis=1) for mi in range(2)]
        g = g_ref[...]
        rows = pl.ds(pl.multiple_of(qi * TQ, TQ), TQ)
        if diff:
            lv = lam_ref[...]
            lam = (jnp.exp(jnp.sum(lv[0:1] * lv[1:2], axis=1, keepdims=True))
                   - jnp.exp(jnp.sum(lv[2:3] * lv[3:4], axis=1, keepdims=True)) + lambda_init)
            o = (ot[0] - lam * ot[1]).T
            y = o * lax.rsqrt(jnp.mean(o * o, axis=1, keepdims=True) + EPS) * g
            o_ref[rows, :] = (y * (1.0 - lambda_init)).astype(o_ref.dtype)
        else:
            o = jnp.concatenate(ot, axis=0).T
            low = _lane_iota((TQ, LANES)) < HEAD_DIM
            sq = o * o
            msa = jnp.sum(jnp.where(low, sq, 0.0), axis=1, keepdims=True) / HEAD_DIM
            msb = jnp.sum(jnp.where(low, 0.0, sq), axis=1, keepdims=True) / HEAD_DIM
            inv = jnp.where(low, lax.rsqrt(msa + EPS), lax.rsqrt(msb + EPS))
            o_ref[rows, :] = (o * inv * g).astype(o_ref.dtype)

    load_queries(0)
    reset_state()
    idle_late()
    for chain in early:
        scores(chain, 0, True)

    @pl.loop(0, nq)
    def _(qi):
        n_plain = jnp.maximum(qi - 1, 0)

        def run(first, count):
            for i in range(count):
                consume(first + i, nxt=first + i + 1)

        @pl.loop(0, n_plain // 4)
        def _(t):
            run(4 * t, 4)

        done = (n_plain // 4) * 4

        @pl.when(n_plain - done >= 2)
        def _():
            run(done, 2)

        def last_blocks(to_next_tile):
            if to_next_tile:
                consume(qi, cur_masked=True, nxt=0, final=True, before_next=lambda: load_queries(qi + 1))
            else:
                consume(qi, cur_masked=True, final=True)
            finalize(qi)

        for to_next_tile in (True, False):
            more = (qi < nq - 1) if to_next_tile else (qi == nq - 1)

            last_tile_odd = (nq - 2) % 2 == 1
            for odd in ((True, False) if to_next_tile else (last_tile_odd,)):
                @pl.when(more & (qi > 0) & ((n_plain % 2 == 1) == odd))
                def _():
                    if odd:
                        run(qi - 2, 1)
                    consume(qi - 1, nxt=qi, nxt_masked=True)
                    last_blocks(to_next_tile)

            if to_next_tile:
                @pl.when(qi == 0)
                def _():
                    last_blocks(to_next_tile)

        reset_state()
        idle_late()


def _attention(diff, lambda_init, q, k, v, g, lamv):
    nq = SEQ // TQ
    kspec = lambda par: pl.BlockSpec((SEQ, LANES), lambda b, p: (b, 2 * p + par))
    return pl.pallas_call(
        functools.partial(_attn_kernel, diff, lambda_init),
        grid=(BATCH, 4),
        in_specs=[kspec(0), kspec(1), kspec(0), kspec(1),
                  pl.BlockSpec((None, v.shape[1] // 4, nq, v.shape[3], TQ), lambda b, p: (b, p, 0, 0, 0)),
                  pl.BlockSpec((1, LANES), lambda b, p: (0, 0)),
                  pl.BlockSpec((8, LANES), lambda b, p: (0, 0))],
        out_specs=pl.BlockSpec((SEQ, LANES), lambda b, p: (b, p)),
        out_shape=jax.ShapeDtypeStruct((N_TOK, DIFF_WIDTH), BF16),
        scratch_shapes=[pltpu.VMEM(shape, dt)
                        for shape, dt in (((LANES, ATTN_TQ), BF16), ((TQ, ATTN_TQ), F32),
                                          ((TQ, ATTN_TQ), BF16), ((1, ATTN_TQ), F32),
                                          ((1, ATTN_TQ), F32), ((v.shape[3], ATTN_TQ), F32))
                        for _ in range(2 * TQ // ATTN_TQ)],
        compiler_params=_params("arbitrary", "arbitrary"),
        name="diff_attention" if diff else "fox_attention",
    )(q, q, k, k, v, g, lamv)


def _outproj_kernel(x_ref, od_ref, of_ref, gt_ref, sc_ref, sh_ref, g_ref, wo_ref, wr_ref, br_ref,
                    x1_ref, h2_ref, rt_ref, cnt_ref, carry_ref):
    @pl.when(pl.program_id(0) == 0)
    def _():
        carry_ref[...] = jnp.zeros_like(carry_ref)

    tiles = [slice(t * TM, (t + 1) * TM) for t in range(OUTPROJ_TILES)]
    lane = _lane_iota((TM, LANES))
    lanef = lane.astype(F32)
    big = float(LANES)

    def project(rows):
        mix = jnp.dot(od_ref[rows, :], wo_ref[0:DIFF_WIDTH, :], preferred_element_type=F32)
        return mix + jnp.dot(of_ref[rows, :], wo_ref[DIFF_WIDTH:, :], preferred_element_type=F32)

    def normalise(rows, mix):
        x1 = x_ref[rows, :] + gt_ref[...] * mix
        x1_ref[rows, :] = x1
        h = _rms_mod(x1, g_ref[...], sc_ref[...], sh_ref[...])
        hh = h.astype(BF16)
        _pack_planes(h, h2_ref, rows)
        return hh, (h - hh.astype(F32)).astype(BF16)

    def router_logits(hh, hl):
        r1 = jnp.dot(hh, wr_ref[...], preferred_element_type=F32)
        r2 = jnp.dot(hl, wr_ref[:, 0:LANES], preferred_element_type=F32)
        return r1[:, 0:LANES] + r1[:, LANES:] + r2 + br_ref[...]

    def top_k(logits):
        isg = lane < N_GROUPS
        lg = jnp.where(isg, logits, NEG)
        mg = jnp.max(lg, axis=1, keepdims=True)
        sg = jnp.sum(jnp.where(isg, jnp.exp(lg - mg), 0.0), axis=1, keepdims=True)
        p_g = 1.0 / sg
        gsel = jnp.min(jnp.where(isg & (lg == mg), lanef, big), axis=1, keepdims=True)
        lo = N_GROUPS + gsel * EXPERTS_PER_GROUP
        ise = (lanef >= lo) & (lanef < lo + EXPERTS_PER_GROUP)
        le = jnp.where(ise, logits, NEG)
        t1 = jnp.max(le, axis=1, keepdims=True)
        i1 = jnp.min(jnp.where(ise & (le == t1), lanef, big), axis=1, keepdims=True)
        ise2 = ise & (lanef != i1)
        le2 = jnp.where(ise2, logits, NEG)
        t2 = jnp.max(le2, axis=1, keepdims=True)
        i2 = jnp.min(jnp.where(ise2 & (le2 == t2), lanef, big), axis=1, keepdims=True)
        d = jnp.exp(t2 - t1)
        return i1 - N_GROUPS, i2 - N_GROUPS, p_g / (1.0 + d), p_g * d / (1.0 + d)

    def earlier_in_tile(e1, e2):
        both = jnp.where((lanef == e1) | (lanef == e2), 1.0, 0.0)
        row = lax.broadcasted_iota(jnp.int32, (TM, TM), 0)
        col = lax.broadcasted_iota(jnp.int32, (TM, TM), 1)
        before = jnp.dot((row > col).astype(BF16), both.astype(BF16), preferred_element_type=F32)
        return before, jnp.sum(both, axis=0, keepdims=True)

    mixes = [project(rows) for rows in tiles]
    splits = [normalise(rows, mix) for rows, mix in zip(tiles, mixes)]
    logits = [router_logits(hh, hl) for hh, hl in splits]
    picks = [top_k(lg) for lg in logits]
    befores = [earlier_in_tile(e1, e2) for e1, e2, _, _ in picks]
    counts = carry_ref[0:1, :]
    for rows, (e1, e2, w1, w2), (before, added) in zip(tiles, picks, befores):
        before = before + counts
        rank1 = jnp.sum(jnp.where(lanef == e1, before, 0.0), axis=1, keepdims=True)
        rank2 = jnp.sum(jnp.where(lanef == e2, before, 0.0), axis=1, keepdims=True)
        out = jnp.zeros((TM, LANES), F32)
        for j, v in enumerate((e1, e2, w1, w2, rank1, rank2)):
            out = jnp.where(lane == j, v, out)
        rt_ref[rows, :] = out
        counts = counts + added
    carry_ref[...] = jnp.broadcast_to(counts, carry_ref.shape)
    cnt_ref[...] = jnp.broadcast_to(counts, cnt_ref.shape)


def _outproj(x, od, of, gt, sc, sh, g, wo_bf, wr, br):
    tm = OUTPROJ_TILES * TM
    tpb = SEQ // tm
    row = pl.BlockSpec((tm, D_MODEL), lambda i: (i, 0))
    half = pl.BlockSpec((tm, DIFF_WIDTH), lambda i: (i, 0))
    per_batch = pl.BlockSpec((None, 1, D_MODEL), lambda i: (i // tpb, 0, 0))
    const = lambda shape: pl.BlockSpec(shape, lambda i: (0,) * len(shape))
    return pl.pallas_call(
        _outproj_kernel,
        grid=(N_TOK // tm,),
        in_specs=[row, half, half, per_batch, per_batch, per_batch, const((1, D_MODEL)),
                  const((D_MODEL, D_MODEL)), const((D_MODEL, 2 * LANES)), const((1, LANES))],
        out_specs=[row, pl.BlockSpec((N_PLANES, tm, LANES), lambda i: (0, i, 0)),
                   pl.BlockSpec((tm, LANES), lambda i: (i, 0)), const((8, LANES))],
        out_shape=[jax.ShapeDtypeStruct((N_TOK, D_MODEL), F32),
                   jax.ShapeDtypeStruct((N_PLANES, N_TOK, LANES), jnp.int32),
                   jax.ShapeDtypeStruct((N_TOK, LANES), F32),
                   jax.ShapeDtypeStruct((8, LANES), F32)],
        scratch_shapes=[pltpu.VMEM((8, LANES), F32)],
        compiler_params=_params("arbitrary"),
        name="outproj_router",
    )(x, od, of, gt, sc, sh, g.reshape(1, D_MODEL), wo_bf, wr, br)


def _expert_kernel(layer, be_ref, cnt_ref, first_ref, slot_ref, next_ref, last_ref, xs_ref, wg_hbm, wu_hbm,
                   wd_hbm, ys_ref, wg_sc, wu_sc, wd_sc, wg_f32, wu_f32, wd_f32, sem):
    i = pl.program_id(0)
    cnt = cnt_ref[i]

    def weight_copies(expert, slot):
        return [pltpu.make_async_copy(hbm.at[layer, expert], buf.at[slot], sem.at[slot, n])
                for n, (hbm, buf) in enumerate(((wg_hbm, wg_f32), (wu_hbm, wu_f32), (wd_hbm, wd_f32)))]

    @pl.when(first_ref[i] == 1)
    def _():
        slot = slot_ref[i]

        @pl.when(i == 0)
        def _():
            for copy in weight_copies(be_ref[0], 0):
                copy.start()

        for copy in weight_copies(be_ref[i], slot):
            copy.wait()
        wg_sc[...] = wg_f32[slot].astype(BF16)
        wu_sc[...] = wu_f32[slot].astype(BF16)
        wd_sc[...] = wd_f32[slot].astype(BF16)

        @pl.when(next_ref[i] >= 0)
        def _():
            for copy in weight_copies(next_ref[i], 1 - slot):
                copy.start()

    def mlp(n_rows):
        rows = slice(0, n_rows)
        live = lax.broadcasted_iota(jnp.int32, (n_rows, LANES), 0) < cnt
        xb = _unpack_planes([jnp.where(live, xs_ref[p, rows, :], 0) for p in range(N_PLANES)]).astype(BF16)
        a = jnp.dot(xb, wg_sc[...], preferred_element_type=F32)
        u = jnp.dot(xb, wu_sc[...], preferred_element_type=F32)
        hid = (a / (1.0 + jnp.exp(-a)) * u).astype(BF16)
        _pack_planes(jnp.dot(hid, wd_sc[...], preferred_element_type=F32), ys_ref, rows)
        if n_rows < MOE_BLOCK:
            ys_ref[:, n_rows:, :] = jnp.zeros((N_PLANES, MOE_BLOCK - n_rows, LANES), ys_ref.dtype)

    half = MOE_BLOCK // 2

    @pl.when(cnt > half)
    def _():
        mlp(MOE_BLOCK)

    @pl.when((cnt > 0) & (cnt <= half))
    def _():
        mlp(half)


def _experts(layer, block_expert, block_count, xs, wg, wu, wd):
    idx = jnp.arange(MOE_NBLOCKS, dtype=jnp.int32)
    first = jnp.concatenate([jnp.ones((1,), jnp.bool_), block_expert[1:] != block_expert[:-1]])
    slot = (jnp.cumsum(first.astype(jnp.int32)) - 1) % 2
    later_first = (idx[None, :] > idx[:, None]) & first[None, :]
    nxt = jnp.min(jnp.where(later_first, block_expert[None, :], N_EXPERTS), axis=1)
    nxt = jnp.where(nxt == N_EXPERTS, -1, nxt).astype(jnp.int32)

    last_used = jnp.maximum(jnp.sum((block_count > 0).astype(jnp.int32)) - 1, 0).reshape(1)
    planes = pl.BlockSpec((N_PLANES, MOE_BLOCK, LANES),
                          lambda i, be, bc, fi, sl, nx, lu: (0, jnp.minimum(i, lu[0]), 0))
    hbm = pl.BlockSpec(memory_space=pl.ANY)
    grid_spec = pltpu.PrefetchScalarGridSpec(
        num_scalar_prefetch=6,
        grid=(MOE_NBLOCKS,),
        in_specs=[planes, hbm, hbm, hbm],
        out_specs=planes,
        scratch_shapes=[pltpu.VMEM((D_MODEL, D_EXPERT), BF16), pltpu.VMEM((D_MODEL, D_EXPERT), BF16),
                        pltpu.VMEM((D_EXPERT, D_MODEL), BF16),
                        pltpu.VMEM((2, D_MODEL, D_EXPERT), F32), pltpu.VMEM((2, D_MODEL, D_EXPERT), F32),
                        pltpu.VMEM((2, D_EXPERT, D_MODEL), F32), pltpu.SemaphoreType.DMA((2, 3))],
    )
    return pl.pallas_call(
        functools.partial(_expert_kernel, layer),
        grid_spec=grid_spec,
        out_shape=jax.ShapeDtypeStruct((N_PLANES, PLANE_ROWS, LANES), jnp.int32),
        compiler_params=_params("arbitrary"),
        name="expert_mlp",
    )(block_expert, block_count, first.astype(jnp.int32), slot.astype(jnp.int32), nxt, last_used,
      xs, wg, wu, wd)


def _slots(route, counts):
    counts = counts[0, :N_EXPERTS].astype(jnp.int32)
    padded = ((counts + MOE_BLOCK - 1) // MOE_BLOCK) * MOE_BLOCK
    pend = jnp.cumsum(padded)
    pstart = pend - padded
    bstart = jnp.arange(MOE_NBLOCKS, dtype=jnp.int32) * MOE_BLOCK
    block_expert = jnp.minimum(jnp.sum(bstart[:, None] >= pend[None, :], axis=1), N_EXPERTS - 1)
    block_expert = block_expert.astype(jnp.int32)
    mine = block_expert[:, None] == jnp.arange(N_EXPERTS, dtype=jnp.int32)[None, :]
    left = jnp.sum(jnp.where(mine, counts + pstart, 0), axis=1) - bstart
    block_count = jnp.clip(left, 0, MOE_BLOCK).astype(jnp.int32)
    base = jnp.pad(pstart.astype(F32), (0, LANES - N_EXPERTS)).reshape(1, LANES)
    return _slot_rows(route, base), block_expert, block_count


def _slot_rows_kernel(rt_ref, base_ref, o_ref):
    rt = rt_ref[...]
    lanef = _lane_iota(rt.shape).astype(F32)
    base = base_ref[...]
    dest = jnp.zeros(rt.shape, F32)
    for k in range(TOP_K):
        b = jnp.sum(jnp.where(lanef == rt[:, k:k + 1], base, 0.0), axis=1, keepdims=True)
        dest = jnp.where(lanef == k, b + rt[:, 2 * TOP_K + k:2 * TOP_K + k + 1], dest)
    dest_t = dest.T.astype(jnp.int32)
    for k in range(TOP_K):
        for p in range(N_PLANES):
            for c in range(rt.shape[0] // LANES):
                o_ref[k * N_PLANES + p, c:c + 1, :] = dest_t[k:k + 1, c * LANES:(c + 1) * LANES] + p * PLANE_ROWS


def _slot_rows(route, base):
    tm = 8 * LANES
    return pl.pallas_call(
        _slot_rows_kernel,
        grid=(N_TOK // tm,),
        in_specs=[pl.BlockSpec((tm, LANES), lambda i: (i, 0)), pl.BlockSpec((1, LANES), lambda i: (0, 0))],
        out_specs=pl.BlockSpec((TOP_K * N_PLANES, tm // LANES, LANES), lambda i: (0, i, 0)),
        out_shape=jax.ShapeDtypeStruct((TOP_K * N_PLANES, N_TOK // LANES, LANES), jnp.int32),
        compiler_params=_params("arbitrary"),
        name="slot_rows",
    )(route, base)


def _sc_workers():
    info = plsc.get_sparse_core_info()
    return info.num_cores, info.num_cores * info.num_subcores


def _sc_scatter2(src, idx, out_rows):
    n_win = src.shape[0] // SC_WINDOW
    nc, nw = _sc_workers()
    steps = n_win // nw
    mesh = plsc.VectorSubcoreMesh(core_axis_name="c", subcore_axis_name="s")

    @functools.partial(
        pl.kernel, mesh=mesh,
        out_type=jax.ShapeDtypeStruct((out_rows, LANES), src.dtype),
        scratch_types=[pltpu.VMEM((2 * steps, SC_WINDOW), jnp.int32),
                       pltpu.VMEM((SC_INFLIGHT, SC_WINDOW, LANES), src.dtype),
                       pltpu.SemaphoreType.DMA((SC_INFLIGHT,)), pltpu.SemaphoreType.DMA((SC_INFLIGHT,))],
        name="sc_dispatch_scatter",
    )
    def k(src_hbm, idx_hbm, out_hbm, idx_v, rows_v, lsem, wsem):
        first = (lax.axis_index("s") * nc + lax.axis_index("c")) * steps
        pltpu.sync_copy(idx_hbm.at[pl.ds(first, steps)], idx_v.at[pl.ds(0, steps)])
        pltpu.sync_copy(idx_hbm.at[pl.ds(n_win + first, steps)], idx_v.at[pl.ds(steps, steps)])

        @pl.loop(0, steps, step=SC_INFLIGHT)
        def _(j):
            loads = [pltpu.async_copy(src_hbm.at[pl.ds((first + j + b) * SC_WINDOW, SC_WINDOW)],
                                      rows_v.at[b], lsem.at[b]) for b in range(SC_INFLIGHT)]
            writes = []
            for b in range(SC_INFLIGHT):
                loads[b].wait()
                for half in range(TOP_K):
                    dst = out_hbm.at[idx_v.at[half * steps + j + b]]
                    writes.append(pltpu.async_copy(rows_v.at[b], dst, wsem.at[b]))
            for w in writes:
                w.wait()

    return k(src, idx)


def _sc_gather(table, idx):
    n_out = idx.shape[0] * SC_WINDOW
    nc, nw = _sc_workers()
    steps = n_out // nw // SC_WINDOW
    mesh = plsc.VectorSubcoreMesh(core_axis_name="c", subcore_axis_name="s")

    @functools.partial(
        pl.kernel, mesh=mesh,
        out_type=jax.ShapeDtypeStruct((n_out, LANES), table.dtype),
        scratch_types=[pltpu.VMEM((steps, SC_WINDOW), jnp.int32),
                       pltpu.VMEM((SC_INFLIGHT, SC_WINDOW, LANES), table.dtype),
                       pltpu.SemaphoreType.DMA((SC_INFLIGHT,)), pltpu.SemaphoreType.DMA((SC_INFLIGHT,))],
        name="sc_combine_gather",
    )
    def k(table_hbm, idx_hbm, out_hbm, idx_v, rows_v, gsem, wsem):
        first = (lax.axis_index("s") * nc + lax.axis_index("c")) * steps
        pltpu.sync_copy(idx_hbm.at[pl.ds(first, steps)], idx_v)

        @pl.loop(0, steps, step=SC_INFLIGHT)
        def _(j):
            gathers = [pltpu.async_copy(table_hbm.at[idx_v.at[j + b]], rows_v.at[b], gsem.at[b])
                       for b in range(SC_INFLIGHT)]
            writes = []
            for b in range(SC_INFLIGHT):
                gathers[b].wait()
                dst = out_hbm.at[pl.ds((first + j + b) * SC_WINDOW, SC_WINDOW)]
                writes.append(pltpu.async_copy(rows_v.at[b], dst, wsem.at[b]))
            for w in writes:
                w.wait()

    return k(table, idx)


def _final_kernel(x_ref, rt_ref, y_ref, gt_ref, g_ref, o_ref):
    x = x_ref[...] + gt_ref[...] * _combine(rt_ref, y_ref)
    ms = jnp.mean(x * x, axis=-1, keepdims=True)
    o_ref[...] = x * lax.rsqrt(ms + EPS) * g_ref[...]


def _final(x, moe, gate, g):
    tpb = SEQ // TM
    row = pl.BlockSpec((TM, D_MODEL), lambda i: (i, 0))
    return pl.pallas_call(
        _final_kernel,
        grid=(N_TOK // TM,),
        in_specs=[row, pl.BlockSpec((TM, LANES), lambda i: (i, 0)),
                  pl.BlockSpec((TOP_K, N_PLANES, TM, LANES), lambda i: (0, 0, i, 0)),
                  pl.BlockSpec((None, 1, D_MODEL), lambda i: (i // tpb, 0, 0)),
                  pl.BlockSpec((1, D_MODEL), lambda i: (0, 0))],
        out_specs=row,
        out_shape=jax.ShapeDtypeStruct((N_TOK, D_MODEL), F32),
        compiler_params=_params("arbitrary"),
        name="final_norm",
    )(x, moe[0], moe[1], gate, g.reshape(1, D_MODEL))


def kernel(x, c, positions, w_ada, b_ada, g_mix, w_in, b_forget, lambda_q1, lambda_k1, lambda_q2,
           lambda_k2, g_subln, g_fox_out, w_out, g_ffn, w_router_group, b_router_group,
           w_router_expert, b_router_expert, w_expert_gate, w_expert_up, w_expert_down, g_final):
    mod = _modulation(c, w_ada, b_ada)
    mod = mod.reshape(DEPTH, BATCH, 6, 1, D_MODEL)
    tables = _rope_tables(positions)
    pq = _forget_placement()
    w_in_bf = jnp.pad(w_in.astype(BF16), ((0, 0), (0, 0), (0, IN_COLS_PAD - IN_COLS)))
    xf = x.reshape(N_TOK, D_MODEL)
    moe = None
    gate = None
    for l in range(DEPTH):
        sh1, sc1, gt1, sh2, sc2, gt2 = (mod[l, :, j] for j in range(6))
        w_bf = (l, w_in_bf)
        bfp =jnp.pad(b_forget[l], (0, LANES - N_FOX_HEADS)).reshape(1, LANES)
        xf, (dq, dk, dv, fq, fk, fv) = _inproj(xf, moe, gate, sc1, sh1, g_mix[l], w_bf, bfp, tables, pq)

        lambda_init = 0.8 - 0.6 * float(np.exp(-0.3 * l))
        lamv = jnp.zeros((8, LANES), F32).at[0:4, 0:HEAD_DIM].set(
            jnp.stack([lambda_q1[l], lambda_k1[l], lambda_q2[l], lambda_k2[l]]))
        g_d = g_subln[l].reshape(1, LANES)
        g_f = jnp.concatenate([g_fox_out[l], g_fox_out[l]]).reshape(1, LANES)
        od = _attention(True, lambda_init, dq, dk, dv, g_d, lamv)
        of = _attention(False, lambda_init, fq, fk, fv, g_f, lamv)

        wr32 = jnp.pad(jnp.concatenate([w_router_group[l], w_router_expert[l]], axis=1),
                       ((0, 0), (0, LANES - N_GROUPS - N_EXPERTS)))
        wr_hi = wr32.astype(BF16)
        wr_lo = (wr32 - wr_hi.astype(F32)).astype(BF16)
        wr = jnp.concatenate([wr_hi, wr_lo], axis=1)
        br = jnp.pad(jnp.concatenate([b_router_group[l], b_router_expert[l]]),
                     (0, LANES - N_GROUPS - N_EXPERTS)).reshape(1, LANES)
        xf, h2, route, counts = _outproj(xf, od, of, gt1, sc2, sh2, g_ffn[l], w_out[l].astype(BF16),
                                         wr, br)

        rows, block_expert, block_count = _slots(route, counts)
        rows = rows.reshape(TOP_K * N_PLANES * N_TOK // SC_WINDOW, SC_WINDOW)
        xs = _sc_scatter2(h2.reshape(N_PLANES * N_TOK, LANES), rows, N_PLANES * PLANE_ROWS)
        ys = _experts(l, block_expert, block_count, xs.reshape(N_PLANES, PLANE_ROWS, LANES),
                      w_expert_gate, w_expert_up, w_expert_down)
        y2 = _sc_gather(ys.reshape(N_PLANES * PLANE_ROWS, LANES), rows)
        moe = (route, y2.reshape(TOP_K, N_PLANES, N_TOK, LANES))
        gate = gt2
    out = _final(xf, moe, gate, g_final)
    return out.reshape(BATCH, SEQ, D_MODEL)
```

```python
import functools

import numpy as np
import jax
import jax.numpy as jnp
from jax import lax
from jax.experimental import pallas as pl
from jax.experimental.pallas import tpu as pltpu
from jax.experimental.pallas import tpu_sc as plsc

D_MODEL = 1024
BATCH = 4
SEQ = 4096
DEPTH = 4
N_TOK = BATCH * SEQ

CHUNK = 64
HEAD_DIM = 64
N_DIFF_HEADS = 4
N_FOX_HEADS = 8
DIFF_WIDTH = 512
FOX_WIDTH = 512
IN_COLS = 3 * DIFF_WIDTH + 3 * FOX_WIDTH + N_FOX_HEADS
ROT_DIM = 16
ROPE_THETA = 500000.0
N_GROUPS = 4
EXPERTS_PER_GROUP = 8
N_EXPERTS = 32
TOP_K = 2
D_EXPERT = 512
EPS = 1e-6

LANES = 128
IN_COLS_PAD = 3200
FF_COL = 3 * DIFF_WIDTH + 3 * FOX_WIDTH
QK_WIDTH = 8 * LANES
TM = 512
OUTPROJ_TILES = 2
TQ = 512
ATTN_MAPS = 4
ATTN_TQ = 256
N_LATE_CHAINS = 1
ONES_ROWS = 16
ATTN_ROWS = 128
LOG2E = 1.4426950408889634
MOE_BLOCK = 512
MOE_ROWS = N_TOK * TOP_K + N_EXPERTS * MOE_BLOCK
MOE_NBLOCKS = MOE_ROWS // MOE_BLOCK
PLANE_ROWS = MOE_ROWS
N_PLANES = D_MODEL // 2 // LANES
SC_WINDOW = 128
SC_INFLIGHT = 4
NEG = -1e30
VMEM_LIMIT = 56 * 1024 * 1024

F32 = jnp.float32
BF16 = jnp.bfloat16


def _bf16_round(x):
    return x.astype(BF16).astype(F32)


def _lane_iota(shape):
    return lax.broadcasted_iota(jnp.int32, shape, 1)


def _params(*sem):
    return pltpu.CompilerParams(dimension_semantics=sem, vmem_limit_bytes=VMEM_LIMIT)


def _pack_planes(y, o_ref, rows=slice(None)):
    bits = lax.bitcast_convert_type(_bf16_round(y), jnp.uint32)
    half = D_MODEL // 2
    word = bits[:, half:] | lax.shift_right_logical(bits[:, :half], jnp.uint32(16))
    word = lax.bitcast_convert_type(word, jnp.int32)
    for p in range(N_PLANES):
        o_ref[p, rows, :] = word[:, p * LANES:(p + 1) * LANES]


def _unpack_planes(planes):
    lo, hi = [], []
    for w in planes:
        u = lax.bitcast_convert_type(w, jnp.uint32)
        lo.append(lax.bitcast_convert_type(lax.shift_left(u, jnp.uint32(16)), F32))
        hi.append(lax.bitcast_convert_type(u & jnp.uint32(0xFFFF0000), F32))
    return jnp.concatenate(lo + hi, axis=1)


def _combine(route_ref, y_ref):
    rt = route_ref[...]
    y0 = _unpack_planes([y_ref[0, p] for p in range(N_PLANES)])
    y1 = _unpack_planes([y_ref[1, p] for p in range(N_PLANES)])
    return rt[:, 2:3] * y0 + rt[:, 3:4] * y1


def _mod_kernel(c_ref, w_ref, b_ref, o_ref):
    c = c_ref[...]
    cond = c / (1.0 + jnp.exp(-c))
    ch = cond.astype(BF16)
    cl = (cond - ch.astype(F32)).astype(BF16)
    w = w_ref[...]
    wh = w.astype(BF16)
    wl = (w - wh.astype(F32)).astype(BF16)
    acc = jnp.dot(ch, wh, preferred_element_type=F32)
    acc += jnp.dot(cl, wh, preferred_element_type=F32)
    acc += jnp.dot(ch, wl, preferred_element_type=F32)
    o_ref[...] = acc + b_ref[...]


def _modulation(c, w_ada, b_ada):
    rows = 16
    tn = 1536
    c_pad = jnp.zeros((rows, D_MODEL), F32).at[:BATCH].set(c)
    out = pl.pallas_call(
        _mod_kernel,
        grid=(DEPTH, 6 * D_MODEL // tn),
        in_specs=[
            pl.BlockSpec((rows, D_MODEL), lambda l, n: (0, 0)),
            pl.BlockSpec((None, D_MODEL, tn), lambda l, n: (l, 0, n)),
            pl.BlockSpec((None, 1, tn), lambda l, n: (l, 0, n)),
        ],
        out_specs=pl.BlockSpec((None, rows, tn), lambda l, n: (l, 0, n)),
        out_shape=jax.ShapeDtypeStruct((DEPTH, rows, 6 * D_MODEL), F32),
        compiler_params=_params("arbitrary", "arbitrary"),
        name="adaln_mod",
    )(c_pad, w_ada, b_ada.reshape(DEPTH, 1, 6 * D_MODEL))
    return out[:, :BATCH]


def _rope_kernel(pos_ref, inv_ref, c_ref, sa_ref, sb_ref):
    ang = pos_ref[...].astype(F32) * inv_ref[...]
    j = _lane_iota(ang.shape) % HEAD_DIM
    cosv = jnp.cos(ang)
    sinv = jnp.sin(ang)
    half = ROT_DIM // 2
    c_ref[...] = jnp.where(j < ROT_DIM, cosv, 1.0)
    sa_ref[...] = jnp.where(j < half, -sinv, 0.0)
    sb_ref[...] = jnp.where((j >= half) & (j < ROT_DIM), sinv, 0.0)


def _rope_tables(positions):
    half = ROT_DIM // 2
    inv = ROPE_THETA ** (-jnp.arange(0, ROT_DIM, 2, dtype=F32) / ROT_DIM)
    lane = np.arange(LANES)
    inv_lane = inv[(lane % HEAD_DIM) % half].reshape(1, LANES)
    spec = pl.BlockSpec((TM, LANES), lambda i: (i, 0))
    shape = jax.ShapeDtypeStruct((N_TOK, LANES), F32)
    return pl.pallas_call(
        _rope_kernel,
        grid=(N_TOK // TM,),
        in_specs=[pl.BlockSpec((TM, 1), lambda i: (i, 0)),
                  pl.BlockSpec((1, LANES), lambda i: (0, 0))],
        out_specs=[spec, spec, spec],
        out_shape=[shape, shape, shape],
        compiler_params=_params("arbitrary"),
        name="rope_tables",
    )(positions.reshape(N_TOK, 1), inv_lane)


def _rms_mod(x, g, sc, sh):
    ms = jnp.mean(x * x, axis=-1, keepdims=True)
    return (x * lax.rsqrt(ms + EPS) * g) * (1.0 + sc) + sh


def _inproj_kernel(fuse, *refs):
    if fuse:
        (x_ref, rt_ref, y_ref, gt_ref, sc_ref, sh_ref, g_ref, w_ref, bf_ref, c_ref, sa_ref, sb_ref,
         pq_ref, xo_ref, dq_ref, dk_ref, dv_ref, fq_ref, fk_ref, fv_ref, carry_ref) = refs
        x = x_ref[...] + gt_ref[...] * _combine(rt_ref, y_ref)
        xo_ref[...] = x
    else:
        (x_ref, sc_ref, sh_ref, g_ref, w_ref, bf_ref, c_ref, sa_ref, sb_ref,
         pq_ref, dq_ref, dk_ref, dv_ref, fq_ref, fk_ref, fv_ref, carry_ref) = refs
        x = x_ref[...]
    hb = _rms_mod(x, g_ref[...], sc_ref[...], sh_ref[...]).astype(BF16)

    @pl.when(pl.program_id(0) % (SEQ // TM) == 0)
    def _():
        carry_ref[...] = jnp.zeros_like(carry_ref)

    lane = _lane_iota((TM, LANES))
    nh = N_FOX_HEADS

    def pack3(a):
        hi = _bf16_round(a)
        r1 = a - hi
        mid = _bf16_round(r1)
        lo = _bf16_round(r1 - mid)
        return jnp.where(lane < nh, hi,
                         jnp.where(lane < 2 * nh, pltpu.roll(mid, nh, 1),
                                   jnp.where(lane < 3 * nh, pltpu.roll(lo, 2 * nh, 1), 0.0)))

    z = jnp.dot(hb, w_ref[:, FF_COL:FF_COL + LANES], preferred_element_type=F32) + bf_ref[...]

    low = lane < HEAD_DIM
    rc, rsa, rsb = c_ref[...], sa_ref[...], sb_ref[...]
    scale = HEAD_DIM ** -0.5 * LOG2E

    def split_store(chunk, o_ref, m, extra_a=None, extra_b=None):
        a = jnp.where(low, chunk, 0.0)
        b = jnp.where(low, pltpu.roll(chunk, HEAD_DIM, 1), 0.0)
        if extra_a is not None:
            a = a + extra_a
            b = b + extra_b
        o_ref[:, (2 * m) * LANES:(2 * m + 1) * LANES] = a.astype(BF16)
        o_ref[:, (2 * m + 1) * LANES:(2 * m + 2) * LANES] = b.astype(BF16)

    def rope(xc):
        return xc * rc + pltpu.roll(xc, LANES - ROT_DIM // 2, 1) * rsa + pltpu.roll(xc, ROT_DIM // 2, 1) * rsb

    pdq = jnp.dot(hb, w_ref[:, 0:DIFF_WIDTH], preferred_element_type=F32)
    for m in range(N_DIFF_HEADS):
        split_store(rope(pdq[:, m * LANES:(m + 1) * LANES]) * scale, dq_ref, m)
    pdk = jnp.dot(hb, w_ref[:, DIFF_WIDTH:2 * DIFF_WIDTH], preferred_element_type=F32)
    for m in range(N_DIFF_HEADS):
        split_store(rope(pdk[:, m * LANES:(m + 1) * LANES]), dk_ref, m)
    def store_values_t(pv, o_ref, width):
        ones = jnp.ones((ONES_ROWS, TM), BF16)
        for m in range(4):
            vt = pv[:, m * LANES:(m + 1) * LANES].T.astype(BF16)
            for i in range(LANES // width):
                o_ref[m * (LANES // width) + i, 0:width, :] = vt[i * width:(i + 1) * width]
                o_ref[m * (LANES // width) + i, width:width + ONES_ROWS, :] = ones

    store_values_t(jnp.dot(hb, w_ref[:, 2 * DIFF_WIDTH:3 * DIFF_WIDTH], preferred_element_type=F32),
                   dv_ref, 2 * HEAD_DIM)
    o = 3 * DIFF_WIDTH
    store_values_t(jnp.dot(hb, w_ref[:, o + 2 * FOX_WIDTH:o + 3 * FOX_WIDTH],
                           preferred_element_type=F32), fv_ref, HEAD_DIM)

    logf =jnp.minimum(z, 0.0) - jnp.log(1.0 + jnp.exp(-jnp.abs(z)))
    logf = jnp.where(lane < nh, logf, 0.0)
    row = lax.broadcasted_iota(jnp.int32, (TM, TM), 0)
    col = lax.broadcasted_iota(jnp.int32, (TM, TM), 1)
    tri = (row >= col).astype(BF16)
    r = jnp.dot(tri, pack3(logf).astype(BF16), preferred_element_type=F32)
    cs = r + pltpu.roll(r, LANES - nh, 1) + pltpu.roll(r, LANES - 2 * nh, 1)
    cf = jnp.where(lane < nh, cs + carry_ref[0:1, :], 0.0)
    carry_ref[...] = jnp.broadcast_to(cf[TM - 1:TM, :], carry_ref.shape)

    t3 = jnp.where(lane == 3 * nh, 1.0, pack3(cf * LOG2E)).astype(BF16)
    aug = jnp.dot(t3, pq_ref[...], preferred_element_type=F32)

    pfq =jnp.dot(hb, w_ref[:, o:o + FOX_WIDTH], preferred_element_type=F32)
    for m in range(N_FOX_HEADS // 2):
        split_store(pfq[:, m * LANES:(m + 1) * LANES] * scale, fq_ref, m,
                    aug[:, (2 * m) * LANES:(2 * m + 1) * LANES],
                    aug[:, (2 * m + 1) * LANES:(2 * m + 2) * LANES])
    pfk = jnp.dot(hb, w_ref[:, o + FOX_WIDTH:o + 2 * FOX_WIDTH], preferred_element_type=F32)
    for m in range(N_FOX_HEADS // 2):
        split_store(pfk[:, m * LANES:(m + 1) * LANES], fk_ref, m,
                    aug[:, QK_WIDTH + (2 * m) * LANES:QK_WIDTH + (2 * m + 1) * LANES],
                    aug[:, QK_WIDTH + (2 * m + 1) * LANES:QK_WIDTH + (2 * m + 2) * LANES])


def _forget_placement():
    nh = N_FOX_HEADS
    p = np.zeros((LANES, 2 * QK_WIDTH), np.float32)
    for h in range(nh):
        base_q = h * LANES + HEAD_DIM
        base_k = QK_WIDTH + h * LANES + HEAD_DIM
        for part in range(3):
            p[part * nh + h, base_q + part] = 1.0
            p[3 * nh, base_q + 3 + part] = 1.0
            p[3 * nh, base_k + part] = 1.0
            p[part * nh + h, base_k + 3 + part] = -1.0
    return jnp.asarray(p, BF16)


def _inproj(x, moe, gate, sc, sh, g, w_bf, b_forget, tables, pq):
    fuse = moe is not None
    tpb = SEQ // TM
    row = pl.BlockSpec((TM, D_MODEL), lambda i: (i, 0))
    per_batch = pl.BlockSpec((None, 1, D_MODEL), lambda i: (i // tpb, 0, 0))
    const = lambda shape: pl.BlockSpec(shape, lambda i: (0,) * len(shape))
    tab = pl.BlockSpec((TM, LANES), lambda i: (i, 0))
    in_specs = [row]
    args = [x]
    if fuse:
        in_specs += [tab, pl.BlockSpec((TOP_K, N_PLANES, TM, LANES), lambda i: (0, 0, i, 0)), per_batch]
        args += [moe[0], moe[1], gate]
    layer, w_all = w_bf
    in_specs += [per_batch, per_batch, const((1, D_MODEL)),
                 pl.BlockSpec((None, D_MODEL, IN_COLS_PAD), lambda i: (layer, 0, 0)),
                 const((1, LANES)), tab, tab, tab, const((LANES, 2 * QK_WIDTH))]
    args += [sc, sh, g.reshape(1, D_MODEL), w_all, b_forget, *tables, pq]
    wide = pl.BlockSpec((TM, QK_WIDTH), lambda i: (i, 0))
    def vspec(heads, width):
        rows = width + ONES_ROWS
        return (pl.BlockSpec((None, heads, None, rows, TM), lambda i: (i // tpb, 0, i % tpb, 0, 0)),
                jax.ShapeDtypeStruct((BATCH, heads, tpb, rows, TM), BF16))

    wide_s = jax.ShapeDtypeStruct((N_TOK, QK_WIDTH), BF16)
    dv_spec, dv_s = vspec(N_DIFF_HEADS, 2 * HEAD_DIM)
    fv_spec, fv_s = vspec(N_FOX_HEADS, HEAD_DIM)
    out_specs = [wide, wide, dv_spec, wide, wide, fv_spec]
    out_shape = [wide_s, wide_s, dv_s, wide_s, wide_s, fv_s]
    if fuse:
        out_specs = [row] + out_specs
        out_shape = [jax.ShapeDtypeStruct((N_TOK, D_MODEL), F32)] + out_shape
    outs = pl.pallas_call(
        functools.partial(_inproj_kernel, fuse),
        grid=(N_TOK // TM,),
        in_specs=in_specs,
        out_specs=out_specs,
        out_shape=out_shape,
        scratch_shapes=[pltpu.VMEM((8, LANES), F32)],
        compiler_params=_params("arbitrary"),
        name="norm_inproj",
    )(*args)
    if fuse:
        return outs[0], outs[1:]
    return x, outs


def _attn_kernel(diff, lambda_init, *refs):
    q_refs, k_refs = refs[:ATTN_MAPS], refs[ATTN_MAPS:2 * ATTN_MAPS]
    v_ref, g_ref, lam_ref, o_ref = refs[2 * ATTN_MAPS:2 * ATTN_MAPS + 4]
    scratch = refs[2 * ATTN_MAPS + 4:]
    nq = SEQ // TQ
    n_half = TQ // ATTN_TQ
    feat = 2 * HEAD_DIM if diff else HEAD_DIM
    chains = []
    for mi, (q_ref, k_ref) in enumerate(zip(q_refs, k_refs)):
        for h in range(n_half):
            c = mi * n_half + h
            qt_sc, s_sc, p_sc, m_sc, a_sc, acc_sc = scratch[c::ATTN_MAPS * n_half]
            vh = mi // 2 if diff else mi
            chains.append((h, k_ref, qt_sc, s_sc, p_sc, m_sc, a_sc, acc_sc, q_ref, vh))
    order = [chains[mi * n_half + h] for h in range(n_half) for mi in range(ATTN_MAPS)]
    early, late = order[:-N_LATE_CHAINS], order[-N_LATE_CHAINS:]

    def load_queries(qi):
        for mi in range(ATTN_MAPS):
            q_ref = chains[mi * n_half][8]
            qt = q_ref[pl.ds(pl.multiple_of(qi * TQ, TQ), TQ), :].astype(F32).T.astype(BF16)
            for h in range(n_half):
                chains[mi * n_half + h][2][...] = qt[:, h * ATTN_TQ:(h + 1) * ATTN_TQ]

    def reset_state():
        for chain in chains:
            m_sc, _, acc_sc = chain[5:8]
            m_sc[...] = jnp.full(m_sc.shape, NEG, F32)
            acc_sc[...] = jnp.zeros(acc_sc.shape, F32)

    def n_keys(chain, masked):
        return (chain[0] + 1) * ATTN_TQ if masked else TQ

    def scores(chain, j, masked):
        h, k_ref, qt_sc, s_sc = chain[:4]
        nk = n_keys(chain, masked)
        off = pl.multiple_of(j * TQ, TQ)
        s = jnp.dot(k_ref[pl.ds(off, nk), :], qt_sc[...], preferred_element_type=F32)
        if masked:
            kk = lax.broadcasted_iota(jnp.int32, (nk, ATTN_TQ), 0)
            qq = h * ATTN_TQ + lax.broadcasted_iota(jnp.int32, (nk, ATTN_TQ), 1)
            s = jnp.where((kk // CHUNK <= qq // CHUNK) if diff else (kk <= qq), s, NEG)
        s_sc[0:nk, :] = s

    def softmax(chain, masked):
        s_sc, p_sc, m_sc, a_sc = chain[3:7]
        nk = n_keys(chain, masked)
        m_all = m_sc[...]
        m_parts = []
        for c0 in range(0, ATTN_TQ, LANES):
            cols = slice(c0, c0 + LANES)
            pm = s_sc[0:ATTN_ROWS, cols]
            for r0 in range(ATTN_ROWS, nk, ATTN_ROWS):
                pm = jnp.maximum(pm, s_sc[r0:r0 + ATTN_ROWS, cols])
            m_new = jnp.maximum(m_all[:, cols], jnp.max(pm, axis=0, keepdims=True))
            for r0 in range(0, nk, ATTN_ROWS):
                p = jnp.exp2(s_sc[r0:r0 + ATTN_ROWS, cols] - m_new)
                p_sc[r0:r0 + ATTN_ROWS, cols] = p.astype(BF16)
            m_parts.append(m_new)
        m_new = jnp.concatenate(m_parts, axis=1)
        a_sc[...] = jnp.exp2(m_all - m_new)
        m_sc[...] = m_new

    def values(chain, j, masked=False):
        p_sc, a_sc, acc_sc, vh = chain[4], chain[6], chain[7], chain[9]
        nk = n_keys(chain, masked)
        pv = jnp.dot(v_ref[vh, j, :, 0:nk], p_sc[0:nk, :], preferred_element_type=F32)
        acc_sc[...] = a_sc[...] * acc_sc[...] + pv

    def idle_late():
        for chain in late:
            chain[4][...] = jnp.zeros(chain[4].shape, BF16)
            chain[6][...] = jnp.ones(chain[6].shape, F32)

    def consume(j, cur_masked=False, nxt=None, nxt_masked=False, final=False, before_next=None):
        def open_late(chain):
            scores(chain, j, cur_masked)
            values(chain, jnp.maximum(j - 1, 0))

        open_late(late[0])
        for i, chain in enumerate(early):
            softmax(chain, cur_masked)
            if i == 0:
                for other in late[1:]:
                    open_late(other)
                if before_next is not None:
                    before_next()
            if nxt is not None:
                scores(chain, nxt, nxt_masked)
            values(chain, j, cur_masked)
        for chain in late:
            softmax(chain, cur_masked)
        if final:
            for chain in late:
                values(chain, j, cur_masked)

    def finalize(qi):
        g = g_ref[...]
        rows = pl.ds(pl.multiple_of(qi * TQ, TQ), TQ)
        for pair in range(ATTN_MAPS // 2):
            ot = [jnp.concatenate([chains[mi * n_half + h][7][0:feat] / chains[mi * n_half + h][7][feat:feat + 1]
                                   for h in range(n_half)], axis=1)
                  for mi in (2 * pair, 2 * pair + 1)]
            cols = slice(pair * LANES, (pair + 1) * LANES)
            if diff:
                lv = lam_ref[...]
                lam = (jnp.exp(jnp.sum(lv[0:1] * lv[1:2], axis=1, keepdims=True))
                       - jnp.exp(jnp.sum(lv[2:3] * lv[3:4], axis=1, keepdims=True)) + lambda_init)
                o = (ot[0] - lam * ot[1]).T
                y = o * lax.rsqrt(jnp.mean(o * o, axis=1, keepdims=True) + EPS) * g
                o_ref[rows, cols] = (y * (1.0 - lambda_init)).astype(o_ref.dtype)
            else:
                o = jnp.concatenate(ot, axis=0).T
                low = _lane_iota((TQ, LANES)) < HEAD_DIM
                sq = o * o
                msa = jnp.sum(jnp.where(low, sq, 0.0), axis=1, keepdims=True) / HEAD_DIM
                msb = jnp.sum(jnp.where(low, 0.0, sq), axis=1, keepdims=True) / HEAD_DIM
                inv = jnp.where(low, lax.rsqrt(msa + EPS), lax.rsqrt(msb + EPS))
                o_ref[rows, cols] = (o * inv * g).astype(o_ref.dtype)

    load_queries(0)
    reset_state()
    idle_late()
    for chain in early:
        scores(chain, 0, True)

    @pl.loop(0, nq)
    def _(qi):
        n_plain = jnp.maximum(qi - 1, 0)

        def run(first, count):
            for i in range(count):
                consume(first + i, nxt=first + i + 1)

        @pl.loop(0, n_plain // 4)
        def _(t):
            run(4 * t, 4)

        done = (n_plain // 4) * 4

        @pl.when(n_plain - done >= 2)
        def _():
            run(done, 2)

        def last_blocks(to_next_tile):
            if to_next_tile:
                consume(qi, cur_masked=True, nxt=0, final=True, before_next=lambda: load_queries(qi + 1))
            else:
                consume(qi, cur_masked=True, final=True)
            finalize(qi)

        for to_next_tile in (True, False):
            more = (qi < nq - 1) if to_next_tile else (qi == nq - 1)

            last_tile_odd = (nq - 2) % 2 == 1
            for odd in ((True, False) if to_next_tile else (last_tile_odd,)):
                @pl.when(more & (qi > 0) & ((n_plain % 2 == 1) == odd))
                def _():
                    if odd:
                        run(qi - 2, 1)
                    consume(qi - 1, nxt=qi, nxt_masked=True)
                    last_blocks(to_next_tile)

            if to_next_tile:
                @pl.when(qi == 0)
                def _():
                    last_blocks(to_next_tile)

        reset_state()
        idle_late()


def _attention(diff, lambda_init, q, k, v, g, lamv):
    nq = SEQ // TQ
    n_steps = 8 // ATTN_MAPS
    v_heads = v.shape[1] // n_steps
    kspec = lambda m: pl.BlockSpec((SEQ, LANES), lambda b, p: (b, ATTN_MAPS * p + m))
    maps = [kspec(m) for m in range(ATTN_MAPS)]
    return pl.pallas_call(
        functools.partial(_attn_kernel, diff, lambda_init),
        grid=(BATCH, n_steps),
        in_specs=maps + maps + [
            pl.BlockSpec((None, v_heads, nq, v.shape[3], TQ), lambda b, p: (b, p, 0, 0, 0)),
            pl.BlockSpec((1, LANES), lambda b, p: (0, 0)),
            pl.BlockSpec((8, LANES), lambda b, p: (0, 0))],
        out_specs=pl.BlockSpec((SEQ, ATTN_MAPS // 2 * LANES), lambda b, p: (b, p)),
        out_shape=jax.ShapeDtypeStruct((N_TOK, DIFF_WIDTH), BF16),
        scratch_shapes=[pltpu.VMEM(shape, dt)
                        for shape, dt in (((LANES, ATTN_TQ), BF16), ((TQ, ATTN_TQ), F32),
                                          ((TQ, ATTN_TQ), BF16), ((1, ATTN_TQ), F32),
                                          ((1, ATTN_TQ), F32), ((v.shape[3], ATTN_TQ), F32))
                        for _ in range(ATTN_MAPS * TQ // ATTN_TQ)],
        compiler_params=_params("arbitrary", "arbitrary"),
        name="diff_attention" if diff else "fox_attention",
    )(*([q] * ATTN_MAPS), *([k] * ATTN_MAPS), v, g, lamv)


def _outproj_kernel(x_ref, od_ref, of_ref, gt_ref, sc_ref, sh_ref, g_ref, wo_ref, wr_ref, br_ref,
                    x1_ref, h2_ref, rt_ref, cnt_ref, carry_ref):
    @pl.when(pl.program_id(0) == 0)
    def _():
        carry_ref[...] = jnp.zeros_like(carry_ref)

    tiles = [slice(t * TM, (t + 1) * TM) for t in range(OUTPROJ_TILES)]
    lane = _lane_iota((TM, LANES))
    lanef = lane.astype(F32)
    big = float(LANES)

    def project(rows):
        mix = jnp.dot(od_ref[rows, :], wo_ref[0:DIFF_WIDTH, :], preferred_element_type=F32)
        return mix + jnp.dot(of_ref[rows, :], wo_ref[DIFF_WIDTH:, :], preferred_element_type=F32)

    def normalise(rows, mix):
        x1 = x_ref[rows, :] + gt_ref[...] * mix
        x1_ref[rows, :] = x1
        h = _rms_mod(x1, g_ref[...], sc_ref[...], sh_ref[...])
        hh = h.astype(BF16)
        _pack_planes(h, h2_ref, rows)
        return hh, (h - hh.astype(F32)).astype(BF16)

    def router_logits(hh, hl):
        r1 = jnp.dot(hh, wr_ref[...], preferred_element_type=F32)
        r2 = jnp.dot(hl, wr_ref[:, 0:LANES], preferred_element_type=F32)
        return r1[:, 0:LANES] + r1[:, LANES:] + r2 + br_ref[...]

    def top_k(logits):
        isg = lane < N_GROUPS
        lg = jnp.where(isg, logits, NEG)
        mg = jnp.max(lg, axis=1, keepdims=True)
        sg = jnp.sum(jnp.where(isg, jnp.exp(lg - mg), 0.0), axis=1, keepdims=True)
        p_g = 1.0 / sg
        gsel = jnp.min(jnp.where(isg & (lg == mg), lanef, big), axis=1, keepdims=True)
        lo = N_GROUPS + gsel * EXPERTS_PER_GROUP
        ise = (lanef >= lo) & (lanef < lo + EXPERTS_PER_GROUP)
        le = jnp.where(ise, logits, NEG)
        t1 = jnp.max(le, axis=1, keepdims=True)
        i1 = jnp.min(jnp.where(ise & (le == t1), lanef, big), axis=1, keepdims=True)
        ise2 = ise & (lanef != i1)
        le2 = jnp.where(ise2, logits, NEG)
        t2 = jnp.max(le2, axis=1, keepdims=True)
        i2 = jnp.min(jnp.where(ise2 & (le2 == t2), lanef, big), axis=1, keepdims=True)
        d = jnp.exp(t2 - t1)
        return i1 - N_GROUPS, i2 - N_GROUPS, p_g / (1.0 + d), p_g * d / (1.0 + d)

    def earlier_in_tile(e1, e2):
        both = jnp.where((lanef == e1) | (lanef == e2), 1.0, 0.0)
        row = lax.broadcasted_iota(jnp.int32, (TM, TM), 0)
        col = lax.broadcasted_iota(jnp.int32, (TM, TM), 1)
        before = jnp.dot((row > col).astype(BF16), both.astype(BF16), preferred_element_type=F32)
        return before, jnp.sum(both, axis=0, keepdims=True)

    mixes = [project(rows) for rows in tiles]
    splits = [normalise(rows, mix) for rows, mix in zip(tiles, mixes)]
    logits = [router_logits(hh, hl) for hh, hl in splits]
    picks = [top_k(lg) for lg in logits]
    befores = [earlier_in_tile(e1, e2) for e1, e2, _, _ in picks]
    counts = carry_ref[0:1, :]
    for rows, (e1, e2, w1, w2), (before, added) in zip(tiles, picks, befores):
        before = before + counts
        rank1 = jnp.sum(jnp.where(lanef == e1, before, 0.0), axis=1, keepdims=True)
        rank2 = jnp.sum(jnp.where(lanef == e2, before, 0.0), axis=1, keepdims=True)
        out = jnp.zeros((TM, LANES), F32)
        for j, v in enumerate((e1, e2, w1, w2, rank1, rank2)):
            out = jnp.where(lane == j, v, out)
        rt_ref[rows, :] = out
        counts = counts + added
    carry_ref[...] = jnp.broadcast_to(counts, carry_ref.shape)
    cnt_ref[...] = jnp.broadcast_to(counts, cnt_ref.shape)


def _outproj(x, od, of, gt, sc, sh, g, wo_bf, wr, br):
    tm = OUTPROJ_TILES * TM
    tpb = SEQ // tm
    row = pl.BlockSpec((tm, D_MODEL), lambda i: (i, 0))
    half = pl.BlockSpec((tm, DIFF_WIDTH), lambda i: (i, 0))
    per_batch = pl.BlockSpec((None, 1, D_MODEL), lambda i: (i // tpb, 0, 0))
    const = lambda shape: pl.BlockSpec(shape, lambda i: (0,) * len(shape))
    return pl.pallas_call(
        _outproj_kernel,
        grid=(N_TOK // tm,),
        in_specs=[row, half, half, per_batch, per_batch, per_batch, const((1, D_MODEL)),
                  const((D_MODEL, D_MODEL)), const((D_MODEL, 2 * LANES)), const((1, LANES))],
        out_specs=[row, pl.BlockSpec((N_PLANES, tm, LANES), lambda i: (0, i, 0)),
                   pl.BlockSpec((tm, LANES), lambda i: (i, 0)), const((8, LANES))],
        out_shape=[jax.ShapeDtypeStruct((N_TOK, D_MODEL), F32),
                   jax.ShapeDtypeStruct((N_PLANES, N_TOK, LANES), jnp.int32),
                   jax.ShapeDtypeStruct((N_TOK, LANES), F32),
                   jax.ShapeDtypeStruct((8, LANES), F32)],
        scratch_shapes=[pltpu.VMEM((8, LANES), F32)],
        compiler_params=_params("arbitrary"),
        name="outproj_router",
    )(x, od, of, gt, sc, sh, g.reshape(1, D_MODEL), wo_bf, wr, br)


def _expert_kernel(layer, be_ref, cnt_ref, first_ref, slot_ref, next_ref, last_ref, xs_ref, wg_hbm, wu_hbm,
                   wd_hbm, ys_ref, wg_sc, wu_sc, wd_sc, wg_f32, wu_f32, wd_f32, sem):
    i = pl.program_id(0)
    cnt = cnt_ref[i]

    def weight_copies(expert, slot):
        return [pltpu.make_async_copy(hbm.at[layer, expert], buf.at[slot], sem.at[slot, n])
                for n, (hbm, buf) in enumerate(((wg_hbm, wg_f32), (wu_hbm, wu_f32), (wd_hbm, wd_f32)))]

    @pl.when(first_ref[i] == 1)
    def _():
        slot = slot_ref[i]

        @pl.when(i == 0)
        def _():
            for copy in weight_copies(be_ref[0], 0):
                copy.start()

        for copy in weight_copies(be_ref[i], slot):
            copy.wait()
        wg_sc[...] = wg_f32[slot].astype(BF16)
        wu_sc[...] = wu_f32[slot].astype(BF16)
        wd_sc[...] = wd_f32[slot].astype(BF16)

        @pl.when(next_ref[i] >= 0)
        def _():
            for copy in weight_copies(next_ref[i], 1 - slot):
                copy.start()

    def mlp(n_rows):
        rows = slice(0, n_rows)
        live = lax.broadcasted_iota(jnp.int32, (n_rows, LANES), 0) < cnt
        xb = _unpack_planes([jnp.where(live, xs_ref[p, rows, :], 0) for p in range(N_PLANES)]).astype(BF16)
        a = jnp.dot(xb, wg_sc[...], preferred_element_type=F32)
        u = jnp.dot(xb, wu_sc[...], preferred_element_type=F32)
        hid = (a / (1.0 + jnp.exp(-a)) * u).astype(BF16)
        _pack_planes(jnp.dot(hid, wd_sc[...], preferred_element_type=F32), ys_ref, rows)
        if n_rows < MOE_BLOCK:
            ys_ref[:, n_rows:, :] = jnp.zeros((N_PLANES, MOE_BLOCK - n_rows, LANES), ys_ref.dtype)

    half = MOE_BLOCK // 2

    @pl.when(cnt > half)
    def _():
        mlp(MOE_BLOCK)

    @pl.when((cnt > 0) & (cnt <= half))
    def _():
        mlp(half)


def _experts(layer, block_expert, block_count, xs, wg, wu, wd):
    idx = jnp.arange(MOE_NBLOCKS, dtype=jnp.int32)
    first = jnp.concatenate([jnp.ones((1,), jnp.bool_), block_expert[1:] != block_expert[:-1]])
    slot = (jnp.cumsum(first.astype(jnp.int32)) - 1) % 2
    later_first = (idx[None, :] > idx[:, None]) & first[None, :]
    nxt = jnp.min(jnp.where(later_first, block_expert[None, :], N_EXPERTS), axis=1)
    nxt = jnp.where(nxt == N_EXPERTS, -1, nxt).astype(jnp.int32)

    last_used = jnp.maximum(jnp.sum((block_count > 0).astype(jnp.int32)) - 1, 0).reshape(1)
    planes = pl.BlockSpec((N_PLANES, MOE_BLOCK, LANES),
                          lambda i, be, bc, fi, sl, nx, lu: (0, jnp.minimum(i, lu[0]), 0))
    hbm = pl.BlockSpec(memory_space=pl.ANY)
    grid_spec = pltpu.PrefetchScalarGridSpec(
        num_scalar_prefetch=6,
        grid=(MOE_NBLOCKS,),
        in_specs=[planes, hbm, hbm, hbm],
        out_specs=planes,
        scratch_shapes=[pltpu.VMEM((D_MODEL, D_EXPERT), BF16), pltpu.VMEM((D_MODEL, D_EXPERT), BF16),
                        pltpu.VMEM((D_EXPERT, D_MODEL), BF16),
                        pltpu.VMEM((2, D_MODEL, D_EXPERT), F32), pltpu.VMEM((2, D_MODEL, D_EXPERT), F32),
                        pltpu.VMEM((2, D_EXPERT, D_MODEL), F32), pltpu.SemaphoreType.DMA((2, 3))],
    )
    return pl.pallas_call(
        functools.partial(_expert_kernel, layer),
        grid_spec=grid_spec,
        out_shape=jax.ShapeDtypeStruct((N_PLANES, PLANE_ROWS, LANES), jnp.int32),
        compiler_params=_params("arbitrary"),
        name="expert_mlp",
    )(block_expert, block_count, first.astype(jnp.int32), slot.astype(jnp.int32), nxt, last_used,
      xs, wg, wu, wd)


def _slots(route, counts):
    counts = counts[0, :N_EXPERTS].astype(jnp.int32)
    padded = ((counts + MOE_BLOCK - 1) // MOE_BLOCK) * MOE_BLOCK
    pend = jnp.cumsum(padded)
    pstart = pend - padded
    bstart = jnp.arange(MOE_NBLOCKS, dtype=jnp.int32) * MOE_BLOCK
    block_expert = jnp.minimum(jnp.sum(bstart[:, None] >= pend[None, :], axis=1), N_EXPERTS - 1)
    block_expert = block_expert.astype(jnp.int32)
    mine = block_expert[:, None] == jnp.arange(N_EXPERTS, dtype=jnp.int32)[None, :]
    left = jnp.sum(jnp.where(mine, counts + pstart, 0), axis=1) - bstart
    block_count = jnp.clip(left, 0, MOE_BLOCK).astype(jnp.int32)
    base = jnp.pad(pstart.astype(F32), (0, LANES - N_EXPERTS)).reshape(1, LANES)
    return _slot_rows(route, base), block_expert, block_count


def _slot_rows_kernel(rt_ref, base_ref, o_ref):
    rt = rt_ref[...]
    lanef = _lane_iota(rt.shape).astype(F32)
    base = base_ref[...]
    dest = jnp.zeros(rt.shape, F32)
    for k in range(TOP_K):
        b = jnp.sum(jnp.where(lanef == rt[:, k:k + 1], base, 0.0), axis=1, keepdims=True)
        dest = jnp.where(lanef == k, b + rt[:, 2 * TOP_K + k:2 * TOP_K + k + 1], dest)
    dest_t = dest.T.astype(jnp.int32)
    for k in range(TOP_K):
        for p in range(N_PLANES):
            for c in range(rt.shape[0] // LANES):
                o_ref[k * N_PLANES + p, c:c + 1, :] = dest_t[k:k + 1, c * LANES:(c + 1) * LANES] + p * PLANE_ROWS


def _slot_rows(route, base):
    tm = 8 * LANES
    return pl.pallas_call(
        _slot_rows_kernel,
        grid=(N_TOK // tm,),
        in_specs=[pl.BlockSpec((tm, LANES), lambda i: (i, 0)), pl.BlockSpec((1, LANES), lambda i: (0, 0))],
        out_specs=pl.BlockSpec((TOP_K * N_PLANES, tm // LANES, LANES), lambda i: (0, i, 0)),
        out_shape=jax.ShapeDtypeStruct((TOP_K * N_PLANES, N_TOK // LANES, LANES), jnp.int32),
        compiler_params=_params("arbitrary"),
        name="slot_rows",
    )(route, base)


def _sc_workers():
    info = plsc.get_sparse_core_info()
    return info.num_cores, info.num_cores * info.num_subcores


def _sc_scatter2(src, idx, out_rows):
    n_win = src.shape[0] // SC_WINDOW
    nc, nw = _sc_workers()
    steps = n_win // nw
    mesh = plsc.VectorSubcoreMesh(core_axis_name="c", subcore_axis_name="s")

    @functools.partial(
        pl.kernel, mesh=mesh,
        out_type=jax.ShapeDtypeStruct((out_rows, LANES), src.dtype),
        scratch_types=[pltpu.VMEM((2 * steps, SC_WINDOW), jnp.int32),
                       pltpu.VMEM((SC_INFLIGHT, SC_WINDOW, LANES), src.dtype),
                       pltpu.SemaphoreType.DMA((SC_INFLIGHT,)), pltpu.SemaphoreType.DMA((SC_INFLIGHT,))],
        name="sc_dispatch_scatter",
    )
    def k(src_hbm, idx_hbm, out_hbm, idx_v, rows_v, lsem, wsem):
        first = (lax.axis_index("s") * nc + lax.axis_index("c")) * steps
        pltpu.sync_copy(idx_hbm.at[pl.ds(first, steps)], idx_v.at[pl.ds(0, steps)])
        pltpu.sync_copy(idx_hbm.at[pl.ds(n_win + first, steps)], idx_v.at[pl.ds(steps, steps)])

        @pl.loop(0, steps, step=SC_INFLIGHT)
        def _(j):
            loads = [pltpu.async_copy(src_hbm.at[pl.ds((first + j + b) * SC_WINDOW, SC_WINDOW)],
                                      rows_v.at[b], lsem.at[b]) for b in range(SC_INFLIGHT)]
            writes = []
            for b in range(SC_INFLIGHT):
                loads[b].wait()
                for half in range(TOP_K):
                    dst = out_hbm.at[idx_v.at[half * steps + j + b]]
                    writes.append(pltpu.async_copy(rows_v.at[b], dst, wsem.at[b]))
            for w in writes:
                w.wait()

    return k(src, idx)


def _sc_gather(table, idx):
    n_out = idx.shape[0] * SC_WINDOW
    nc, nw = _sc_workers()
    steps = n_out // nw // SC_WINDOW
    mesh = plsc.VectorSubcoreMesh(core_axis_name="c", subcore_axis_name="s")

    @functools.partial(
        pl.kernel, mesh=mesh,
        out_type=jax.ShapeDtypeStruct((n_out, LANES), table.dtype),
        scratch_types=[pltpu.VMEM((steps, SC_WINDOW), jnp.int32),
                       pltpu.VMEM((SC_INFLIGHT, SC_WINDOW, LANES), table.dtype),
                       pltpu.SemaphoreType.DMA((SC_INFLIGHT,)), pltpu.SemaphoreType.DMA((SC_INFLIGHT,))],
        name="sc_combine_gather",
    )
    def k(table_hbm, idx_hbm, out_hbm, idx_v, rows_v, gsem, wsem):
        first = (lax.axis_index("s") * nc + lax.axis_index("c")) * steps
        pltpu.sync_copy(idx_hbm.at[pl.ds(first, steps)], idx_v)

        @pl.loop(0, steps, step=SC_INFLIGHT)
        def _(j):
            gathers = [pltpu.async_copy(table_hbm.at[idx_v.at[j + b]], rows_v.at[b], gsem.at[b])
                       for b in range(SC_INFLIGHT)]
            writes = []
            for b in range(SC_INFLIGHT):
                gathers[b].wait()
                dst = out_hbm.at[pl.ds((first + j + b) * SC_WINDOW, SC_WINDOW)]
                writes.append(pltpu.async_copy(rows_v.at[b], dst, wsem.at[b]))
            for w in writes:
                w.wait()

    return k(table, idx)


def _final_kernel(x_ref, rt_ref, y_ref, gt_ref, g_ref, o_ref):
    x = x_ref[...] + gt_ref[...] * _combine(rt_ref, y_ref)
    ms = jnp.mean(x * x, axis=-1, keepdims=True)
    o_ref[...] = x * lax.rsqrt(ms + EPS) * g_ref[...]


def _final(x, moe, gate, g):
    tpb = SEQ // TM
    row = pl.BlockSpec((TM, D_MODEL), lambda i: (i, 0))
    return pl.pallas_call(
        _final_kernel,
        grid=(N_TOK // TM,),
        in_specs=[row, pl.BlockSpec((TM, LANES), lambda i: (i, 0)),
                  pl.BlockSpec((TOP_K, N_PLANES, TM, LANES), lambda i: (0, 0, i, 0)),
                  pl.BlockSpec((None, 1, D_MODEL), lambda i: (i // tpb, 0, 0)),
                  pl.BlockSpec((1, D_MODEL), lambda i: (0, 0))],
        out_specs=row,
        out_shape=jax.ShapeDtypeStruct((N_TOK, D_MODEL), F32),
        compiler_params=_params("arbitrary"),
        name="final_norm",
    )(x, moe[0], moe[1], gate, g.reshape(1, D_MODEL))


def kernel(x, c, positions, w_ada, b_ada, g_mix, w_in, b_forget, lambda_q1, lambda_k1, lambda_q2,
           lambda_k2, g_subln, g_fox_out, w_out, g_ffn, w_router_group, b_router_group,
           w_router_expert, b_router_expert, w_expert_gate, w_expert_up, w_expert_down, g_final):
    mod = _modulation(c, w_ada, b_ada)
    mod = mod.reshape(DEPTH, BATCH, 6, 1, D_MODEL)
    tables = _rope_tables(positions)
    pq = _forget_placement()
    w_in_bf = jnp.pad(w_in.astype(BF16), ((0, 0), (0, 0), (0, IN_COLS_PAD - IN_COLS)))
    xf = x.reshape(N_TOK, D_MODEL)
    moe = None
    gate = None
    for l in range(DEPTH):
        sh1, sc1, gt1, sh2, sc2, gt2 = (mod[l, :, j] for j in range(6))
        w_bf = (l, w_in_bf)
        bfp =jnp.pad(b_forget[l], (0, LANES - N_FOX_HEADS)).reshape(1, LANES)
        xf, (dq, dk, dv, fq, fk, fv) = _inproj(xf, moe, gate, sc1, sh1, g_mix[l], w_bf, bfp, tables, pq)

        lambda_init = 0.8 - 0.6 * float(np.exp(-0.3 * l))
        lamv = jnp.zeros((8, LANES), F32).at[0:4, 0:HEAD_DIM].set(
            jnp.stack([lambda_q1[l], lambda_k1[l], lambda_q2[l], lambda_k2[l]]))
        g_d = g_subln[l].reshape(1, LANES)
        g_f = jnp.concatenate([g_fox_out[l], g_fox_out[l]]).reshape(1, LANES)
        od = _attention(True, lambda_init, dq, dk, dv, g_d, lamv)
        of = _attention(False, lambda_init, fq, fk, fv, g_f, lamv)

        wr32 = jnp.pad(jnp.concatenate([w_router_group[l], w_router_expert[l]], axis=1),
                       ((0, 0), (0, LANES - N_GROUPS - N_EXPERTS)))
        wr_hi = wr32.astype(BF16)
        wr_lo = (wr32 - wr_hi.astype(F32)).astype(BF16)
        wr = jnp.concatenate([wr_hi, wr_lo], axis=1)
        br = jnp.pad(jnp.concatenate([b_router_group[l], b_router_expert[l]]),
                     (0, LANES - N_GROUPS - N_EXPERTS)).reshape(1, LANES)
        xf, h2, route, counts = _outproj(xf, od, of, gt1, sc2, sh2, g_ffn[l], w_out[l].astype(BF16),
                                         wr, br)

        rows, block_expert, block_count = _slots(route, counts)
        rows = rows.reshape(TOP_K * N_PLANES * N_TOK // SC_WINDOW, SC_WINDOW)
        xs = _sc_scatter2(h2.reshape(N_PLANES * N_TOK, LANES), rows, N_PLANES * PLANE_ROWS)
        ys = _experts(l, block_expert, block_count, xs.reshape(N_PLANES, PLANE_ROWS, LANES),
                      w_expert_gate, w_expert_up, w_expert_down)
        y2 = _sc_gather(ys.reshape(N_PLANES * PLANE_ROWS, LANES), rows)
        moe = (route, y2.reshape(TOP_K, N_PLANES, N_TOK, LANES))
        gate = gt2
    out = _final(xf, moe, gate, g_final)
    return out.reshape(BATCH, SEQ, D_MODEL)
```

```python
import functools

import numpy as np
import jax
import jax.numpy as jnp
from jax import lax
from jax.experimental import pallas as pl
from jax.experimental.pallas import tpu as pltpu
from jax.experimental.pallas import tpu_sc as plsc

D_MODEL = 1024
BATCH = 4
SEQ = 4096
DEPTH = 4
N_TOK = BATCH * SEQ

CHUNK = 64
HEAD_DIM = 64
N_DIFF_HEADS = 4
N_FOX_HEADS = 8
DIFF_WIDTH = 512
FOX_WIDTH = 512
IN_COLS = 3 * DIFF_WIDTH + 3 * FOX_WIDTH + N_FOX_HEADS
ROT_DIM = 16
ROPE_THETA = 500000.0
N_GROUPS = 4
EXPERTS_PER_GROUP = 8
N_EXPERTS = 32
TOP_K = 2
D_EXPERT = 512
EPS = 1e-6

LANES = 128
IN_COLS_PAD = 3200
FF_COL = 3 * DIFF_WIDTH + 3 * FOX_WIDTH
QK_WIDTH = 8 * LANES
TM = 512
OUTPROJ_TILES = 2
TQ = 512
ATTN_MAPS = 4
ATTN_TQ = 256
N_LATE_CHAINS = 1
ONES_ROWS = 16
ATTN_ROWS = 128
LOG2E = 1.4426950408889634
MOE_BLOCK = 512
MOE_ROWS = N_TOK * TOP_K + N_EXPERTS * MOE_BLOCK
MOE_NBLOCKS = MOE_ROWS // MOE_BLOCK
PLANE_ROWS = MOE_ROWS
N_PLANES = D_MODEL // 2 // LANES
SC_WINDOW = 128
SC_INFLIGHT = 4
NEG = -1e30
VMEM_LIMIT = 56 * 1024 * 1024

F32 = jnp.float32
BF16 = jnp.bfloat16


def _bf16_round(x):
    return x.astype(BF16).astype(F32)


def _lane_iota(shape):
    return lax.broadcasted_iota(jnp.int32, shape, 1)


def _params(*sem):
    return pltpu.CompilerParams(dimension_semantics=sem, vmem_limit_bytes=VMEM_LIMIT)


def _pack_planes(y, o_ref, rows=slice(None)):
    bits = lax.bitcast_convert_type(_bf16_round(y), jnp.uint32)
    half = D_MODEL // 2
    word = bits[:, half:] | lax.shift_right_logical(bits[:, :half], jnp.uint32(16))
    word = lax.bitcast_convert_type(word, jnp.int32)
    for p in range(N_PLANES):
        o_ref[p, rows, :] = word[:, p * LANES:(p + 1) * LANES]


def _unpack_planes(planes):
    lo, hi = [], []
    for w in planes:
        u = lax.bitcast_convert_type(w, jnp.uint32)
        lo.append(lax.bitcast_convert_type(lax.shift_left(u, jnp.uint32(16)), F32))
        hi.append(lax.bitcast_convert_type(u & jnp.uint32(0xFFFF0000), F32))
    return jnp.concatenate(lo + hi, axis=1)


def _combine(route_ref, y_ref):
    rt = route_ref[...]
    y0 = _unpack_planes([y_ref[0, p] for p in range(N_PLANES)])
    y1 = _unpack_planes([y_ref[1, p] for p in range(N_PLANES)])
    return rt[:, 2:3] * y0 + rt[:, 3:4] * y1


def _mod_kernel(c_ref, w_ref, b_ref, o_ref):
    c = c_ref[...]
    cond = c / (1.0 + jnp.exp(-c))
    ch = cond.astype(BF16)
    cl = (cond - ch.astype(F32)).astype(BF16)
    w = w_ref[...]
    wh = w.astype(BF16)
    wl = (w - wh.astype(F32)).astype(BF16)
    acc = jnp.dot(ch, wh, preferred_element_type=F32)
    acc += jnp.dot(cl, wh, preferred_element_type=F32)
    acc += jnp.dot(ch, wl, preferred_element_type=F32)
    o_ref[...] = acc + b_ref[...]


def _modulation(c, w_ada, b_ada):
    rows = 16
    tn = 1536
    c_pad = jnp.zeros((rows, D_MODEL), F32).at[:BATCH].set(c)
    out = pl.pallas_call(
        _mod_kernel,
        grid=(DEPTH, 6 * D_MODEL // tn),
        in_specs=[
            pl.BlockSpec((rows, D_MODEL), lambda l, n: (0, 0)),
            pl.BlockSpec((None, D_MODEL, tn), lambda l, n: (l, 0, n)),
            pl.BlockSpec((None, 1, tn), lambda l, n: (l, 0, n)),
        ],
        out_specs=pl.BlockSpec((None, rows, tn), lambda l, n: (l, 0, n)),
        out_shape=jax.ShapeDtypeStruct((DEPTH, rows, 6 * D_MODEL), F32),
        compiler_params=_params("arbitrary", "arbitrary"),
        name="adaln_mod",
    )(c_pad, w_ada, b_ada.reshape(DEPTH, 1, 6 * D_MODEL))
    return out[:, :BATCH]


def _rope_kernel(pos_ref, inv_ref, c_ref, sa_ref, sb_ref):
    ang = pos_ref[...].astype(F32) * inv_ref[...]
    j = _lane_iota(ang.shape) % HEAD_DIM
    cosv = jnp.cos(ang)
    sinv = jnp.sin(ang)
    half = ROT_DIM // 2
    c_ref[...] = jnp.where(j < ROT_DIM, cosv, 1.0)
    sa_ref[...] = jnp.where(j < half, -sinv, 0.0)
    sb_ref[...] = jnp.where((j >= half) & (j < ROT_DIM), sinv, 0.0)


def _rope_tables(positions):
    half = ROT_DIM // 2
    inv = ROPE_THETA ** (-jnp.arange(0, ROT_DIM, 2, dtype=F32) / ROT_DIM)
    lane = np.arange(LANES)
    inv_lane = inv[(lane % HEAD_DIM) % half].reshape(1, LANES)
    spec = pl.BlockSpec((TM, LANES), lambda i: (i, 0))
    shape = jax.ShapeDtypeStruct((N_TOK, LANES), F32)
    return pl.pallas_call(
        _rope_kernel,
        grid=(N_TOK // TM,),
        in_specs=[pl.BlockSpec((TM, 1), lambda i: (i, 0)),
                  pl.BlockSpec((1, LANES), lambda i: (0, 0))],
        out_specs=[spec, spec, spec],
        out_shape=[shape, shape, shape],
        compiler_params=_params("arbitrary"),
        name="rope_tables",
    )(positions.reshape(N_TOK, 1), inv_lane)


def _rms_mod(x, g, sc, sh):
    ms = jnp.mean(x * x, axis=-1, keepdims=True)
    return (x * lax.rsqrt(ms + EPS) * g) * (1.0 + sc) + sh


def _inproj_kernel(fuse, *refs):
    if fuse:
        (x_ref, rt_ref, y_ref, gt_ref, sc_ref, sh_ref, g_ref, w_ref, bf_ref, c_ref, sa_ref, sb_ref,
         pq_ref, xo_ref, dq_ref, dk_ref, dv_ref, fq_ref, fk_ref, fv_ref, carry_ref, h_even, h_odd) = refs
    else:
        (x_ref, sc_ref, sh_ref, g_ref, w_ref, bf_ref, c_ref, sa_ref, sb_ref,
         pq_ref, dq_ref, dk_ref, dv_ref, fq_ref, fk_ref, fv_ref, carry_ref, h_even, h_odd) = refs
    step = pl.program_id(0)

    def normalise(h_ref):
        if fuse:
            x = x_ref[...] + gt_ref[...] * _combine(rt_ref, y_ref)
            xo_ref[...] = x
        else:
            x = x_ref[...]
        h_ref[...] = _rms_mod(x, g_ref[...], sc_ref[...], sh_ref[...]).astype(BF16)

    @pl.when(step == 0)
    def _():
        h_odd[...] = jnp.zeros_like(h_odd)

    @pl.when((step == 0) | ((step - 1) % (SEQ // TM) == 0))
    def _():
        carry_ref[...] = jnp.zeros_like(carry_ref)

    for parity, (h_new, h_old) in enumerate(((h_even, h_odd), (h_odd, h_even))):
        @pl.when(step % 2 == parity)
        def _():
            _inproj_project(h_old[...], w_ref, bf_ref, c_ref, sa_ref, sb_ref, pq_ref, dq_ref, dk_ref,
                            dv_ref, fq_ref, fk_ref, fv_ref, carry_ref)
            normalise(h_new)


def _inproj_project(hb, w_ref, bf_ref, c_ref, sa_ref, sb_ref, pq_ref, dq_ref, dk_ref, dv_ref, fq_ref,
                    fk_ref, fv_ref, carry_ref):
    lane = _lane_iota((TM, LANES))
    nh = N_FOX_HEADS

    def pack3(a):
        hi = _bf16_round(a)
        r1 = a - hi
        mid = _bf16_round(r1)
        lo = _bf16_round(r1 - mid)
        return jnp.where(lane < nh, hi,
                         jnp.where(lane < 2 * nh, pltpu.roll(mid, nh, 1),
                                   jnp.where(lane < 3 * nh, pltpu.roll(lo, 2 * nh, 1), 0.0)))

    z = jnp.dot(hb, w_ref[:, FF_COL:FF_COL + LANES], preferred_element_type=F32) + bf_ref[...]

    low = lane < HEAD_DIM
    rc, rsa, rsb = c_ref[...], sa_ref[...], sb_ref[...]
    scale = HEAD_DIM ** -0.5 * LOG2E

    def split_store(chunk, o_ref, m, extra_a=None, extra_b=None):
        a = jnp.where(low, chunk, 0.0)
        b = jnp.where(low, pltpu.roll(chunk, HEAD_DIM, 1), 0.0)
        if extra_a is not None:
            a = a + extra_a
            b = b + extra_b
        o_ref[:, (2 * m) * LANES:(2 * m + 1) * LANES] = a.astype(BF16)
        o_ref[:, (2 * m + 1) * LANES:(2 * m + 2) * LANES] = b.astype(BF16)

    def rope(xc):
        return xc * rc + pltpu.roll(xc, LANES - ROT_DIM // 2, 1) * rsa + pltpu.roll(xc, ROT_DIM // 2, 1) * rsb

    pdq = jnp.dot(hb, w_ref[:, 0:DIFF_WIDTH], preferred_element_type=F32)
    for m in range(N_DIFF_HEADS):
        split_store(rope(pdq[:, m * LANES:(m + 1) * LANES]) * scale, dq_ref, m)
    pdk = jnp.dot(hb, w_ref[:, DIFF_WIDTH:2 * DIFF_WIDTH], preferred_element_type=F32)
    for m in range(N_DIFF_HEADS):
        split_store(rope(pdk[:, m * LANES:(m + 1) * LANES]), dk_ref, m)
    def store_values_t(pv, o_ref, width):
        ones = jnp.ones((ONES_ROWS, TM), BF16)
        for m in range(4):
            vt = pv[:, m * LANES:(m + 1) * LANES].T.astype(BF16)
            for i in range(LANES // width):
                o_ref[m * (LANES // width) + i, 0:width, :] = vt[i * width:(i + 1) * width]
                o_ref[m * (LANES // width) + i, width:width + ONES_ROWS, :] = ones

    store_values_t(jnp.dot(hb, w_ref[:, 2 * DIFF_WIDTH:3 * DIFF_WIDTH], preferred_element_type=F32),
                   dv_ref, 2 * HEAD_DIM)
    o = 3 * DIFF_WIDTH
    store_values_t(jnp.dot(hb, w_ref[:, o + 2 * FOX_WIDTH:o + 3 * FOX_WIDTH],
                           preferred_element_type=F32), fv_ref, HEAD_DIM)

    logf =jnp.minimum(z, 0.0) - jnp.log(1.0 + jnp.exp(-jnp.abs(z)))
    logf = jnp.where(lane < nh, logf, 0.0)
    row = lax.broadcasted_iota(jnp.int32, (TM, TM), 0)
    col = lax.broadcasted_iota(jnp.int32, (TM, TM), 1)
    tri = (row >= col).astype(BF16)
    r = jnp.dot(tri, pack3(logf).astype(BF16), preferred_element_type=F32)
    cs = r + pltpu.roll(r, LANES - nh, 1) + pltpu.roll(r, LANES - 2 * nh, 1)
    cf = jnp.where(lane < nh, cs + carry_ref[0:1, :], 0.0)
    carry_ref[...] = jnp.broadcast_to(cf[TM - 1:TM, :], carry_ref.shape)

    t3 = jnp.where(lane == 3 * nh, 1.0, pack3(cf * LOG2E)).astype(BF16)
    aug = jnp.dot(t3, pq_ref[...], preferred_element_type=F32)

    pfq =jnp.dot(hb, w_ref[:, o:o + FOX_WIDTH], preferred_element_type=F32)
    for m in range(N_FOX_HEADS // 2):
        split_store(pfq[:, m * LANES:(m + 1) * LANES] * scale, fq_ref, m,
                    aug[:, (2 * m) * LANES:(2 * m + 1) * LANES],
                    aug[:, (2 * m + 1) * LANES:(2 * m + 2) * LANES])
    pfk = jnp.dot(hb, w_ref[:, o + FOX_WIDTH:o + 2 * FOX_WIDTH], preferred_element_type=F32)
    for m in range(N_FOX_HEADS // 2):
        split_store(pfk[:, m * LANES:(m + 1) * LANES], fk_ref, m,
                    aug[:, QK_WIDTH + (2 * m) * LANES:QK_WIDTH + (2 * m + 1) * LANES],
                    aug[:, QK_WIDTH + (2 * m + 1) * LANES:QK_WIDTH + (2 * m + 2) * LANES])


def _forget_placement():
    nh = N_FOX_HEADS
    p = np.zeros((LANES, 2 * QK_WIDTH), np.float32)
    for h in range(nh):
        base_q = h * LANES + HEAD_DIM
        base_k = QK_WIDTH + h * LANES + HEAD_DIM
        for part in range(3):
            p[part * nh + h, base_q + part] = 1.0
            p[3 * nh, base_q + 3 + part] = 1.0
            p[3 * nh, base_k + part] = 1.0
            p[part * nh + h, base_k + 3 + part] = -1.0
    return jnp.asarray(p, BF16)


def _inproj(x, moe, gate, sc, sh, g, w_bf, b_forget, tables, pq):
    fuse = moe is not None
    tpb = SEQ // TM
    n_tiles = N_TOK // TM
    new = lambda s: jnp.minimum(s, n_tiles - 1)
    old = lambda s: jnp.maximum(s - 1, 0)
    row = pl.BlockSpec((TM, D_MODEL), lambda s: (new(s), 0))
    per_batch = pl.BlockSpec((None, 1, D_MODEL), lambda s: (new(s) // tpb, 0, 0))
    const = lambda shape: pl.BlockSpec(shape, lambda s: (0,) * len(shape))
    tab = pl.BlockSpec((TM, LANES), lambda s: (old(s), 0))
    in_specs = [row]
    args = [x]
    if fuse:
        in_specs += [pl.BlockSpec((TM, LANES), lambda s: (new(s), 0)),
                     pl.BlockSpec((TOP_K, N_PLANES, TM, LANES), lambda s: (0, 0, new(s), 0)), per_batch]
        args += [moe[0], moe[1], gate]
    layer, w_all = w_bf
    in_specs += [per_batch, per_batch, const((1, D_MODEL)),
                 pl.BlockSpec((None, D_MODEL, IN_COLS_PAD), lambda s: (layer, 0, 0)),
                 const((1, LANES)), tab, tab, tab, const((LANES, 2 * QK_WIDTH))]
    args += [sc, sh, g.reshape(1, D_MODEL), w_all, b_forget, *tables, pq]
    wide = pl.BlockSpec((TM, QK_WIDTH), lambda s: (old(s), 0))
    def vspec(heads, width):
        rows = width + ONES_ROWS
        return (pl.BlockSpec((None, heads, None, rows, TM),
                             lambda s: (old(s) // tpb, 0, old(s) % tpb, 0, 0)),
                jax.ShapeDtypeStruct((BATCH, heads, tpb, rows, TM), BF16))

    wide_s = jax.ShapeDtypeStruct((N_TOK, QK_WIDTH), BF16)
    dv_spec, dv_s = vspec(N_DIFF_HEADS, 2 * HEAD_DIM)
    fv_spec, fv_s = vspec(N_FOX_HEADS, HEAD_DIM)
    out_specs = [wide, wide, dv_spec, wide, wide, fv_spec]
    out_shape = [wide_s, wide_s, dv_s, wide_s, wide_s, fv_s]
    if fuse:
        out_specs = [row] + out_specs
        out_shape = [jax.ShapeDtypeStruct((N_TOK, D_MODEL), F32)] + out_shape
    outs = pl.pallas_call(
        functools.partial(_inproj_kernel, fuse),
        grid=(n_tiles + 1,),
        in_specs=in_specs,
        out_specs=out_specs,
        out_shape=out_shape,
        scratch_shapes=[pltpu.VMEM((8, LANES), F32), pltpu.VMEM((TM, D_MODEL), BF16),
                        pltpu.VMEM((TM, D_MODEL), BF16)],
        compiler_params=_params("arbitrary"),
        name="norm_inproj",
    )(*args)
    if fuse:
        return outs[0], outs[1:]
    return x, outs


def _attn_kernel(diff, lambda_init, *refs):
    q_refs, k_refs = refs[:ATTN_MAPS], refs[ATTN_MAPS:2 * ATTN_MAPS]
    v_ref, g_ref, lam_ref, o_ref = refs[2 * ATTN_MAPS:2 * ATTN_MAPS + 4]
    scratch = refs[2 * ATTN_MAPS + 4:]
    nq = SEQ // TQ
    n_half = TQ // ATTN_TQ
    feat = 2 * HEAD_DIM if diff else HEAD_DIM
    chains = []
    for mi, (q_ref, k_ref) in enumerate(zip(q_refs, k_refs)):
        for h in range(n_half):
            c = mi * n_half + h
            qt_sc, s_sc, p_sc, m_sc, a_sc, acc_sc = scratch[c::ATTN_MAPS * n_half]
            vh = mi // 2 if diff else mi
            chains.append((h, k_ref, qt_sc, s_sc, p_sc, m_sc, a_sc, acc_sc, q_ref, vh))
    order = [chains[mi * n_half + h] for h in range(n_half) for mi in range(ATTN_MAPS)]
    early, late = order[:-N_LATE_CHAINS], order[-N_LATE_CHAINS:]

    def load_queries(qi):
        for mi in range(ATTN_MAPS):
            q_ref = chains[mi * n_half][8]
            qt = q_ref[pl.ds(pl.multiple_of(qi * TQ, TQ), TQ), :].astype(F32).T.astype(BF16)
            for h in range(n_half):
                chains[mi * n_half + h][2][...] = qt[:, h * ATTN_TQ:(h + 1) * ATTN_TQ]

    def reset_state():
        for chain in chains:
            m_sc, _, acc_sc = chain[5:8]
            m_sc[...] = jnp.full(m_sc.shape, NEG, F32)
            acc_sc[...] = jnp.zeros(acc_sc.shape, F32)

    def n_keys(chain, masked):
        return (chain[0] + 1) * ATTN_TQ if masked else TQ

    def scores(chain, j, masked):
        h, k_ref, qt_sc, s_sc = chain[:4]
        nk = n_keys(chain, masked)
        off = pl.multiple_of(j * TQ, TQ)
        s = jnp.dot(k_ref[pl.ds(off, nk), :], qt_sc[...], preferred_element_type=F32)
        if masked:
            kk = lax.broadcasted_iota(jnp.int32, (nk, ATTN_TQ), 0)
            qq = h * ATTN_TQ + lax.broadcasted_iota(jnp.int32, (nk, ATTN_TQ), 1)
            s = jnp.where((kk // CHUNK <= qq // CHUNK) if diff else (kk <= qq), s, NEG)
        s_sc[0:nk, :] = s

    def softmax(chain, masked):
        s_sc, p_sc, m_sc, a_sc = chain[3:7]
        nk = n_keys(chain, masked)
        m_all = m_sc[...]
        m_parts = []
        for c0 in range(0, ATTN_TQ, LANES):
            cols = slice(c0, c0 + LANES)
            pm = s_sc[0:ATTN_ROWS, cols]
            for r0 in range(ATTN_ROWS, nk, ATTN_ROWS):
                pm = jnp.maximum(pm, s_sc[r0:r0 + ATTN_ROWS, cols])
            m_new = jnp.maximum(m_all[:, cols], jnp.max(pm, axis=0, keepdims=True))
            for r0 in range(0, nk, ATTN_ROWS):
                p = jnp.exp2(s_sc[r0:r0 + ATTN_ROWS, cols] - m_new)
                p_sc[r0:r0 + ATTN_ROWS, cols] = p.astype(BF16)
            m_parts.append(m_new)
        m_new = jnp.concatenate(m_parts, axis=1)
        a_sc[...] = jnp.exp2(m_all - m_new)
        m_sc[...] = m_new

    def values(chain, j, masked=False):
        p_sc, a_sc, acc_sc, vh = chain[4], chain[6], chain[7], chain[9]
        nk = n_keys(chain, masked)
        pv = jnp.dot(v_ref[vh, j, :, 0:nk], p_sc[0:nk, :], preferred_element_type=F32)
        acc_sc[...] = a_sc[...] * acc_sc[...] + pv

    def idle_late():
        for chain in late:
            chain[4][...] = jnp.zeros(chain[4].shape, BF16)
            chain[6][...] = jnp.ones(chain[6].shape, F32)

    def consume(j, cur_masked=False, nxt=None, nxt_masked=False, final=False, before_next=None):
        def open_late(chain):
            scores(chain, j, cur_masked)
            values(chain, jnp.maximum(j - 1, 0))

        open_late(late[0])
        for i, chain in enumerate(early):
            softmax(chain, cur_masked)
            if i == 0:
                for other in late[1:]:
                    open_late(other)
                if before_next is not None:
                    before_next()
            if nxt is not None:
                scores(chain, nxt, nxt_masked)
            values(chain, j, cur_masked)
        for chain in late:
            softmax(chain, cur_masked)
        if final:
            for chain in late:
                values(chain, j, cur_masked)

    def finalize(qi):
        g = g_ref[...]
        rows = pl.ds(pl.multiple_of(qi * TQ, TQ), TQ)
        for pair in range(ATTN_MAPS // 2):
            ot = [jnp.concatenate([chains[mi * n_half + h][7][0:feat] / chains[mi * n_half + h][7][feat:feat + 1]
                                   for h in range(n_half)], axis=1)
                  for mi in (2 * pair, 2 * pair + 1)]
            cols = slice(pair * LANES, (pair + 1) * LANES)
            if diff:
                lv = lam_ref[...]
                lam = (jnp.exp(jnp.sum(lv[0:1] * lv[1:2], axis=1, keepdims=True))
                       - jnp.exp(jnp.sum(lv[2:3] * lv[3:4], axis=1, keepdims=True)) + lambda_init)
                o = (ot[0] - lam * ot[1]).T
                y = o * lax.rsqrt(jnp.mean(o * o, axis=1, keepdims=True) + EPS) * g
                o_ref[rows, cols] = (y * (1.0 - lambda_init)).astype(o_ref.dtype)
            else:
                o = jnp.concatenate(ot, axis=0).T
                low = _lane_iota((TQ, LANES)) < HEAD_DIM
                sq = o * o
                msa = jnp.sum(jnp.where(low, sq, 0.0), axis=1, keepdims=True) / HEAD_DIM
                msb = jnp.sum(jnp.where(low, 0.0, sq), axis=1, keepdims=True) / HEAD_DIM
                inv = jnp.where(low, lax.rsqrt(msa + EPS), lax.rsqrt(msb + EPS))
                o_ref[rows, cols] = (o * inv * g).astype(o_ref.dtype)

    load_queries(0)
    reset_state()
    idle_late()
    for chain in early:
        scores(chain, 0, True)

    @pl.loop(0, nq)
    def _(qi):
        n_plain = jnp.maximum(qi - 1, 0)

        def run(first, count):
            for i in range(count):
                consume(first + i, nxt=first + i + 1)

        @pl.loop(0, n_plain // 4)
        def _(t):
            run(4 * t, 4)

        done = (n_plain // 4) * 4

        @pl.when(n_plain - done >= 2)
        def _():
            run(done, 2)

        def last_blocks(to_next_tile):
            if to_next_tile:
                consume(qi, cur_masked=True, nxt=0, final=True, before_next=lambda: load_queries(qi + 1))
            else:
                consume(qi, cur_masked=True, final=True)
            finalize(qi)

        for to_next_tile in (True, False):
            more = (qi < nq - 1) if to_next_tile else (qi == nq - 1)

            last_tile_odd = (nq - 2) % 2 == 1
            for odd in ((True, False) if to_next_tile else (last_tile_odd,)):
                @pl.when(more & (qi > 0) & ((n_plain % 2 == 1) == odd))
                def _():
                    if odd:
                        run(qi - 2, 1)
                    consume(qi - 1, nxt=qi, nxt_masked=True)
                    last_blocks(to_next_tile)

            if to_next_tile:
                @pl.when(qi == 0)
                def _():
                    last_blocks(to_next_tile)

        reset_state()
        idle_late()


def _attention(diff, lambda_init, q, k, v, g, lamv):
    nq = SEQ // TQ
    n_steps = 8 // ATTN_MAPS
    v_heads = v.shape[1] // n_steps
    kspec = lambda m: pl.BlockSpec((SEQ, LANES), lambda b, p: (b, ATTN_MAPS * p + m))
    maps = [kspec(m) for m in range(ATTN_MAPS)]
    return pl.pallas_call(
        functools.partial(_attn_kernel, diff, lambda_init),
        grid=(BATCH, n_steps),
        in_specs=maps + maps + [
            pl.BlockSpec((None, v_heads, nq, v.shape[3], TQ), lambda b, p: (b, p, 0, 0, 0)),
            pl.BlockSpec((1, LANES), lambda b, p: (0, 0)),
            pl.BlockSpec((8, LANES), lambda b, p: (0, 0))],
        out_specs=pl.BlockSpec((SEQ, ATTN_MAPS // 2 * LANES), lambda b, p: (b, p)),
        out_shape=jax.ShapeDtypeStruct((N_TOK, DIFF_WIDTH), BF16),
        scratch_shapes=[pltpu.VMEM(shape, dt)
                        for shape, dt in (((LANES, ATTN_TQ), BF16), ((TQ, ATTN_TQ), F32),
                                          ((TQ, ATTN_TQ), BF16), ((1, ATTN_TQ), F32),
                                          ((1, ATTN_TQ), F32), ((v.shape[3], ATTN_TQ), F32))
                        for _ in range(ATTN_MAPS * TQ // ATTN_TQ)],
        compiler_params=_params("arbitrary", "arbitrary"),
        name="diff_attention" if diff else "fox_attention",
    )(*([q] * ATTN_MAPS), *([k] * ATTN_MAPS), v, g, lamv)


def _outproj_kernel(x_ref, od_ref, of_ref, gt_ref, sc_ref, sh_ref, g_ref, wo_ref, wr_ref, br_ref,
                    x1_ref, h2_ref, rt_ref, cnt_ref, carry_ref):
    @pl.when(pl.program_id(0) == 0)
    def _():
        carry_ref[...] = jnp.zeros_like(carry_ref)

    tiles = [slice(t * TM, (t + 1) * TM) for t in range(OUTPROJ_TILES)]
    lane = _lane_iota((TM, LANES))
    lanef = lane.astype(F32)
    big = float(LANES)

    def project(rows):
        mix = jnp.dot(od_ref[rows, :], wo_ref[0:DIFF_WIDTH, :], preferred_element_type=F32)
        return mix + jnp.dot(of_ref[rows, :], wo_ref[DIFF_WIDTH:, :], preferred_element_type=F32)

    def normalise(rows, mix):
        x1 = x_ref[rows, :] + gt_ref[...] * mix
        x1_ref[rows, :] = x1
        h = _rms_mod(x1, g_ref[...], sc_ref[...], sh_ref[...])
        hh = h.astype(BF16)
        _pack_planes(h, h2_ref, rows)
        return hh, (h - hh.astype(F32)).astype(BF16)

    def router_logits(hh, hl):
        r1 = jnp.dot(hh, wr_ref[...], preferred_element_type=F32)
        r2 = jnp.dot(hl, wr_ref[:, 0:LANES], preferred_element_type=F32)
        return r1[:, 0:LANES] + r1[:, LANES:] + r2 + br_ref[...]

    def top_k(logits):
        isg = lane < N_GROUPS
        lg = jnp.where(isg, logits, NEG)
        mg = jnp.max(lg, axis=1, keepdims=True)
        sg = jnp.sum(jnp.where(isg, jnp.exp(lg - mg), 0.0), axis=1, keepdims=True)
        p_g = 1.0 / sg
        gsel = jnp.min(jnp.where(isg & (lg == mg), lanef, big), axis=1, keepdims=True)
        lo = N_GROUPS + gsel * EXPERTS_PER_GROUP
        ise = (lanef >= lo) & (lanef < lo + EXPERTS_PER_GROUP)
        le = jnp.where(ise, logits, NEG)
        t1 = jnp.max(le, axis=1, keepdims=True)
        i1 = jnp.min(jnp.where(ise & (le == t1), lanef, big), axis=1, keepdims=True)
        ise2 = ise & (lanef != i1)
        le2 = jnp.where(ise2, logits, NEG)
        t2 = jnp.max(le2, axis=1, keepdims=True)
        i2 = jnp.min(jnp.where(ise2 & (le2 == t2), lanef, big), axis=1, keepdims=True)
        d = jnp.exp(t2 - t1)
        return i1 - N_GROUPS, i2 - N_GROUPS, p_g / (1.0 + d), p_g * d / (1.0 + d)

    def earlier_in_tile(e1, e2):
        both = jnp.where((lanef == e1) | (lanef == e2), 1.0, 0.0)
        row = lax.broadcasted_iota(jnp.int32, (TM, TM), 0)
        col = lax.broadcasted_iota(jnp.int32, (TM, TM), 1)
        before = jnp.dot((row > col).astype(BF16), both.astype(BF16), preferred_element_type=F32)
        return before, jnp.sum(both, axis=0, keepdims=True)

    mixes = [project(rows) for rows in tiles]
    splits = [normalise(rows, mix) for rows, mix in zip(tiles, mixes)]
    logits = [router_logits(hh, hl) for hh, hl in splits]
    picks = [top_k(lg) for lg in logits]
    befores = [earlier_in_tile(e1, e2) for e1, e2, _, _ in picks]
    counts = carry_ref[0:1, :]
    for rows, (e1, e2, w1, w2), (before, added) in zip(tiles, picks, befores):
        before = before + counts
        rank1 = jnp.sum(jnp.where(lanef == e1, before, 0.0), axis=1, keepdims=True)
        rank2 = jnp.sum(jnp.where(lanef == e2, before, 0.0), axis=1, keepdims=True)
        out = jnp.zeros((TM, LANES), F32)
        for j, v in enumerate((e1, e2, w1, w2, rank1, rank2)):
            out = jnp.where(lane == j, v, out)
        rt_ref[rows, :] = out
        counts = counts + added
    carry_ref[...] = jnp.broadcast_to(counts, carry_ref.shape)
    cnt_ref[...] = jnp.broadcast_to(counts, cnt_ref.shape)


def _outproj(x, od, of, gt, sc, sh, g, wo_bf, wr, br):
    tm = OUTPROJ_TILES * TM
    tpb = SEQ // tm
    row = pl.BlockSpec((tm, D_MODEL), lambda i: (i, 0))
    half = pl.BlockSpec((tm, DIFF_WIDTH), lambda i: (i, 0))
    per_batch = pl.BlockSpec((None, 1, D_MODEL), lambda i: (i // tpb, 0, 0))
    const = lambda shape: pl.BlockSpec(shape, lambda i: (0,) * len(shape))
    return pl.pallas_call(
        _outproj_kernel,
        grid=(N_TOK // tm,),
        in_specs=[row, half, half, per_batch, per_batch, per_batch, const((1, D_MODEL)),
                  const((D_MODEL, D_MODEL)), const((D_MODEL, 2 * LANES)), const((1, LANES))],
        out_specs=[row, pl.BlockSpec((N_PLANES, tm, LANES), lambda i: (0, i, 0)),
                   pl.BlockSpec((tm, LANES), lambda i: (i, 0)), const((8, LANES))],
        out_shape=[jax.ShapeDtypeStruct((N_TOK, D_MODEL), F32),
                   jax.ShapeDtypeStruct((N_PLANES, N_TOK, LANES), jnp.int32),
                   jax.ShapeDtypeStruct((N_TOK, LANES), F32),
                   jax.ShapeDtypeStruct((8, LANES), F32)],
        scratch_shapes=[pltpu.VMEM((8, LANES), F32)],
        compiler_params=_params("arbitrary"),
        name="outproj_router",
    )(x, od, of, gt, sc, sh, g.reshape(1, D_MODEL), wo_bf, wr, br)


def _expert_kernel(layer, be_ref, cnt_ref, first_ref, slot_ref, next_ref, last_ref, xs_ref, wg_hbm, wu_hbm,
                   wd_hbm, ys_ref, wg_sc, wu_sc, wd_sc, wg_f32, wu_f32, wd_f32, sem):
    i = pl.program_id(0)
    cnt = cnt_ref[i]

    def weight_copies(expert, slot):
        return [pltpu.make_async_copy(hbm.at[layer, expert], buf.at[slot], sem.at[slot, n])
                for n, (hbm, buf) in enumerate(((wg_hbm, wg_f32), (wu_hbm, wu_f32), (wd_hbm, wd_f32)))]

    @pl.when(first_ref[i] == 1)
    def _():
        slot = slot_ref[i]

        @pl.when(i == 0)
        def _():
            for copy in weight_copies(be_ref[0], 0):
                copy.start()

        for copy in weight_copies(be_ref[i], slot):
            copy.wait()
        wg_sc[...] = wg_f32[slot].astype(BF16)
        wu_sc[...] = wu_f32[slot].astype(BF16)
        wd_sc[...] = wd_f32[slot].astype(BF16)

        @pl.when(next_ref[i] >= 0)
        def _():
            for copy in weight_copies(next_ref[i], 1 - slot):
                copy.start()

    def mlp(n_rows):
        rows = slice(0, n_rows)
        live = lax.broadcasted_iota(jnp.int32, (n_rows, LANES), 0) < cnt
        xb = _unpack_planes([jnp.where(live, xs_ref[p, rows, :], 0) for p in range(N_PLANES)]).astype(BF16)
        a = jnp.dot(xb, wg_sc[...], preferred_element_type=F32)
        u = jnp.dot(xb, wu_sc[...], preferred_element_type=F32)
        hid = (a / (1.0 + jnp.exp(-a)) * u).astype(BF16)
        _pack_planes(jnp.dot(hid, wd_sc[...], preferred_element_type=F32), ys_ref, rows)
        if n_rows < MOE_BLOCK:
            ys_ref[:, n_rows:, :] = jnp.zeros((N_PLANES, MOE_BLOCK - n_rows, LANES), ys_ref.dtype)

    half = MOE_BLOCK // 2

    @pl.when(cnt > half)
    def _():
        mlp(MOE_BLOCK)

    @pl.when((cnt > 0) & (cnt <= half))
    def _():
        mlp(half)


def _experts(layer, block_expert, block_count, xs, wg, wu, wd):
    idx = jnp.arange(MOE_NBLOCKS, dtype=jnp.int32)
    first = jnp.concatenate([jnp.ones((1,), jnp.bool_), block_expert[1:] != block_expert[:-1]])
    slot = (jnp.cumsum(first.astype(jnp.int32)) - 1) % 2
    later_first = (idx[None, :] > idx[:, None]) & first[None, :]
    nxt = jnp.min(jnp.where(later_first, block_expert[None, :], N_EXPERTS), axis=1)
    nxt = jnp.where(nxt == N_EXPERTS, -1, nxt).astype(jnp.int32)

    last_used = jnp.maximum(jnp.sum((block_count > 0).astype(jnp.int32)) - 1, 0).reshape(1)
    planes = pl.BlockSpec((N_PLANES, MOE_BLOCK, LANES),
                          lambda i, be, bc, fi, sl, nx, lu: (0, jnp.minimum(i, lu[0]), 0))
    hbm = pl.BlockSpec(memory_space=pl.ANY)
    grid_spec = pltpu.PrefetchScalarGridSpec(
        num_scalar_prefetch=6,
        grid=(MOE_NBLOCKS,),
        in_specs=[planes, hbm, hbm, hbm],
        out_specs=planes,
        scratch_shapes=[pltpu.VMEM((D_MODEL, D_EXPERT), BF16), pltpu.VMEM((D_MODEL, D_EXPERT), BF16),
                        pltpu.VMEM((D_EXPERT, D_MODEL), BF16),
                        pltpu.VMEM((2, D_MODEL, D_EXPERT), F32), pltpu.VMEM((2, D_MODEL, D_EXPERT), F32),
                        pltpu.VMEM((2, D_EXPERT, D_MODEL), F32), pltpu.SemaphoreType.DMA((2, 3))],
    )
    return pl.pallas_call(
        functools.partial(_expert_kernel, layer),
        grid_spec=grid_spec,
        out_shape=jax.ShapeDtypeStruct((N_PLANES, PLANE_ROWS, LANES), jnp.int32),
        compiler_params=_params("arbitrary"),
        name="expert_mlp",
    )(block_expert, block_count, first.astype(jnp.int32), slot.astype(jnp.int32), nxt, last_used,
      xs, wg, wu, wd)


def _slots(route, counts):
    counts = counts[0, :N_EXPERTS].astype(jnp.int32)
    padded = ((counts + MOE_BLOCK - 1) // MOE_BLOCK) * MOE_BLOCK
    pend = jnp.cumsum(padded)
    pstart = pend - padded
    bstart = jnp.arange(MOE_NBLOCKS, dtype=jnp.int32) * MOE_BLOCK
    block_expert = jnp.minimum(jnp.sum(bstart[:, None] >= pend[None, :], axis=1), N_EXPERTS - 1)
    block_expert = block_expert.astype(jnp.int32)
    mine = block_expert[:, None] == jnp.arange(N_EXPERTS, dtype=jnp.int32)[None, :]
    left = jnp.sum(jnp.where(mine, counts + pstart, 0), axis=1) - bstart
    block_count = jnp.clip(left, 0, MOE_BLOCK).astype(jnp.int32)
    base = jnp.pad(pstart.astype(F32), (0, LANES - N_EXPERTS)).reshape(1, LANES)
    return _slot_rows(route, base), block_expert, block_count


def _slot_rows_kernel(rt_ref, base_ref, o_ref):
    rt = rt_ref[...]
    lanef = _lane_iota(rt.shape).astype(F32)
    base = base_ref[...]
    dest = jnp.zeros(rt.shape, F32)
    for k in range(TOP_K):
        b = jnp.sum(jnp.where(lanef == rt[:, k:k + 1], base, 0.0), axis=1, keepdims=True)
        dest = jnp.where(lanef == k, b + rt[:, 2 * TOP_K + k:2 * TOP_K + k + 1], dest)
    dest_t = dest.T.astype(jnp.int32)
    for k in range(TOP_K):
        for p in range(N_PLANES):
            for c in range(rt.shape[0] // LANES):
                o_ref[k * N_PLANES + p, c:c + 1, :] = dest_t[k:k + 1, c * LANES:(c + 1) * LANES] + p * PLANE_ROWS


def _slot_rows(route, base):
    tm = 8 * LANES
    return pl.pallas_call(
        _slot_rows_kernel,
        grid=(N_TOK // tm,),
        in_specs=[pl.BlockSpec((tm, LANES), lambda i: (i, 0)), pl.BlockSpec((1, LANES), lambda i: (0, 0))],
        out_specs=pl.BlockSpec((TOP_K * N_PLANES, tm // LANES, LANES), lambda i: (0, i, 0)),
        out_shape=jax.ShapeDtypeStruct((TOP_K * N_PLANES, N_TOK // LANES, LANES), jnp.int32),
        compiler_params=_params("arbitrary"),
        name="slot_rows",
    )(route, base)


def _sc_workers():
    info = plsc.get_sparse_core_info()
    return info.num_cores, info.num_cores * info.num_subcores


def _sc_scatter2(src, idx, out_rows):
    n_win = src.shape[0] // SC_WINDOW
    nc, nw = _sc_workers()
    steps = n_win // nw
    mesh = plsc.VectorSubcoreMesh(core_axis_name="c", subcore_axis_name="s")

    @functools.partial(
        pl.kernel, mesh=mesh,
        out_type=jax.ShapeDtypeStruct((out_rows, LANES), src.dtype),
        scratch_types=[pltpu.VMEM((2 * steps, SC_WINDOW), jnp.int32),
                       pltpu.VMEM((SC_INFLIGHT, SC_WINDOW, LANES), src.dtype),
                       pltpu.SemaphoreType.DMA((SC_INFLIGHT,)), pltpu.SemaphoreType.DMA((SC_INFLIGHT,))],
        name="sc_dispatch_scatter",
    )
    def k(src_hbm, idx_hbm, out_hbm, idx_v, rows_v, lsem, wsem):
        first = (lax.axis_index("s") * nc + lax.axis_index("c")) * steps
        pltpu.sync_copy(idx_hbm.at[pl.ds(first, steps)], idx_v.at[pl.ds(0, steps)])
        pltpu.sync_copy(idx_hbm.at[pl.ds(n_win + first, steps)], idx_v.at[pl.ds(steps, steps)])

        @pl.loop(0, steps, step=SC_INFLIGHT)
        def _(j):
            loads = [pltpu.async_copy(src_hbm.at[pl.ds((first + j + b) * SC_WINDOW, SC_WINDOW)],
                                      rows_v.at[b], lsem.at[b]) for b in range(SC_INFLIGHT)]
            writes = []
            for b in range(SC_INFLIGHT):
                loads[b].wait()
                for half in range(TOP_K):
                    dst = out_hbm.at[idx_v.at[half * steps + j + b]]
                    writes.append(pltpu.async_copy(rows_v.at[b], dst, wsem.at[b]))
            for w in writes:
                w.wait()

    return k(src, idx)


def _sc_gather(table, idx):
    n_out = idx.shape[0] * SC_WINDOW
    nc, nw = _sc_workers()
    steps = n_out // nw // SC_WINDOW
    mesh = plsc.VectorSubcoreMesh(core_axis_name="c", subcore_axis_name="s")

    @functools.partial(
        pl.kernel, mesh=mesh,
        out_type=jax.ShapeDtypeStruct((n_out, LANES), table.dtype),
        scratch_types=[pltpu.VMEM((steps, SC_WINDOW), jnp.int32),
                       pltpu.VMEM((SC_INFLIGHT, SC_WINDOW, LANES), table.dtype),
                       pltpu.SemaphoreType.DMA((SC_INFLIGHT,)), pltpu.SemaphoreType.DMA((SC_INFLIGHT,))],
        name="sc_combine_gather",
    )
    def k(table_hbm, idx_hbm, out_hbm, idx_v, rows_v, gsem, wsem):
        first = (lax.axis_index("s") * nc + lax.axis_index("c")) * steps
        pltpu.sync_copy(idx_hbm.at[pl.ds(first, steps)], idx_v)

        @pl.loop(0, steps, step=SC_INFLIGHT)
        def _(j):
            gathers = [pltpu.async_copy(table_hbm.at[idx_v.at[j + b]], rows_v.at[b], gsem.at[b])
                       for b in range(SC_INFLIGHT)]
            writes = []
            for b in range(SC_INFLIGHT):
                gathers[b].wait()
                dst = out_hbm.at[pl.ds((first + j + b) * SC_WINDOW, SC_WINDOW)]
                writes.append(pltpu.async_copy(rows_v.at[b], dst, wsem.at[b]))
            for w in writes:
                w.wait()

    return k(table, idx)


def _final_kernel(x_ref, rt_ref, y_ref, gt_ref, g_ref, o_ref):
    x = x_ref[...] + gt_ref[...] * _combine(rt_ref, y_ref)
    ms = jnp.mean(x * x, axis=-1, keepdims=True)
    o_ref[...] = x * lax.rsqrt(ms + EPS) * g_ref[...]


def _final(x, moe, gate, g):
    tpb = SEQ // TM
    row = pl.BlockSpec((TM, D_MODEL), lambda i: (i, 0))
    return pl.pallas_call(
        _final_kernel,
        grid=(N_TOK // TM,),
        in_specs=[row, pl.BlockSpec((TM, LANES), lambda i: (i, 0)),
                  pl.BlockSpec((TOP_K, N_PLANES, TM, LANES), lambda i: (0, 0, i, 0)),
                  pl.BlockSpec((None, 1, D_MODEL), lambda i: (i // tpb, 0, 0)),
                  pl.BlockSpec((1, D_MODEL), lambda i: (0, 0))],
        out_specs=row,
        out_shape=jax.ShapeDtypeStruct((N_TOK, D_MODEL), F32),
        compiler_params=_params("arbitrary"),
        name="final_norm",
    )(x, moe[0], moe[1], gate, g.reshape(1, D_MODEL))


def kernel(x, c, positions, w_ada, b_ada, g_mix, w_in, b_forget, lambda_q1, lambda_k1, lambda_q2,
           lambda_k2, g_subln, g_fox_out, w_out, g_ffn, w_router_group, b_router_group,
           w_router_expert, b_router_expert, w_expert_gate, w_expert_up, w_expert_down, g_final):
    mod = _modulation(c, w_ada, b_ada)
    mod = mod.reshape(DEPTH, BATCH, 6, 1, D_MODEL)
    tables = _rope_tables(positions)
    pq = _forget_placement()
    w_in_bf = jnp.pad(w_in.astype(BF16), ((0, 0), (0, 0), (0, IN_COLS_PAD - IN_COLS)))
    xf = x.reshape(N_TOK, D_MODEL)
    moe = None
    gate = None
    for l in range(DEPTH):
        sh1, sc1, gt1, sh2, sc2, gt2 = (mod[l, :, j] for j in range(6))
        w_bf = (l, w_in_bf)
        bfp =jnp.pad(b_forget[l], (0, LANES - N_FOX_HEADS)).reshape(1, LANES)
        xf, (dq, dk, dv, fq, fk, fv) = _inproj(xf, moe, gate, sc1, sh1, g_mix[l], w_bf, bfp, tables, pq)

        lambda_init = 0.8 - 0.6 * float(np.exp(-0.3 * l))
        lamv = jnp.zeros((8, LANES), F32).at[0:4, 0:HEAD_DIM].set(
            jnp.stack([lambda_q1[l], lambda_k1[l], lambda_q2[l], lambda_k2[l]]))
        g_d = g_subln[l].reshape(1, LANES)
        g_f = jnp.concatenate([g_fox_out[l], g_fox_out[l]]).reshape(1, LANES)
        od = _attention(True, lambda_init, dq, dk, dv, g_d, lamv)
        of = _attention(False, lambda_init, fq, fk, fv, g_f, lamv)

        wr32 = jnp.pad(jnp.concatenate([w_router_group[l], w_router_expert[l]], axis=1),
                       ((0, 0), (0, LANES - N_GROUPS - N_EXPERTS)))
        wr_hi = wr32.astype(BF16)
        wr_lo = (wr32 - wr_hi.astype(F32)).astype(BF16)
        wr = jnp.concatenate([wr_hi, wr_lo], axis=1)
        br = jnp.pad(jnp.concatenate([b_router_group[l], b_router_expert[l]]),
                     (0, LANES - N_GROUPS - N_EXPERTS)).reshape(1, LANES)
        xf, h2, route, counts = _outproj(xf, od, of, gt1, sc2, sh2, g_ffn[l], w_out[l].astype(BF16),
                                         wr, br)

        rows, block_expert, block_count = _slots(route, counts)
        rows = rows.reshape(TOP_K * N_PLANES * N_TOK // SC_WINDOW, SC_WINDOW)
        xs = _sc_scatter2(h2.reshape(N_PLANES * N_TOK, LANES), rows, N_PLANES * PLANE_ROWS)
        ys = _experts(l, block_expert, block_count, xs.reshape(N_PLANES, PLANE_ROWS, LANES),
                      w_expert_gate, w_expert_up, w_expert_down)
        y2 = _sc_gather(ys.reshape(N_PLANES * PLANE_ROWS, LANES), rows)
        moe = (route, y2.reshape(TOP_K, N_PLANES, N_TOK, LANES))
        gate = gt2
    out = _final(xf, moe, gate, g_final)
    return out.reshape(BATCH, SEQ, D_MODEL)
```

```python
import functools

import numpy as np
import jax
import jax.numpy as jnp
from jax import lax
from jax.experimental import pallas as pl
from jax.experimental.pallas import tpu as pltpu
from jax.experimental.pallas import tpu_sc as plsc

D_MODEL = 1024
BATCH = 4
SEQ = 4096
DEPTH = 4
N_TOK = BATCH * SEQ

CHUNK = 64
HEAD_DIM = 64
N_DIFF_HEADS = 4
N_FOX_HEADS = 8
DIFF_WIDTH = 512
FOX_WIDTH = 512
IN_COLS = 3 * DIFF_WIDTH + 3 * FOX_WIDTH + N_FOX_HEADS
ROT_DIM = 16
ROPE_THETA = 500000.0
N_GROUPS = 4
EXPERTS_PER_GROUP = 8
N_EXPERTS = 32
TOP_K = 2
D_EXPERT = 512
EPS = 1e-6

LANES = 128
IN_COLS_PAD = 3200
FF_COL = 3 * DIFF_WIDTH + 3 * FOX_WIDTH
QK_WIDTH = 8 * LANES
TM = 512
OUTPROJ_TILES = 2
TQ = 512
ATTN_MAPS = 4
ATTN_TQ = 256
N_LATE_CHAINS = 1
ONES_ROWS = 16
ATTN_ROWS = 128
LOG2E = 1.4426950408889634
MOE_BLOCK = 512
MOE_ROWS = N_TOK * TOP_K + N_EXPERTS * MOE_BLOCK
MOE_NBLOCKS = MOE_ROWS // MOE_BLOCK
PLANE_ROWS = MOE_ROWS
N_PLANES = D_MODEL // 2 // LANES
SC_WINDOW = 128
SC_INFLIGHT = 4
NEG = -1e30
VMEM_LIMIT = 56 * 1024 * 1024

F32 = jnp.float32
BF16 = jnp.bfloat16


def _bf16_round(x):
    return x.astype(BF16).astype(F32)


def _lane_iota(shape):
    return lax.broadcasted_iota(jnp.int32, shape, 1)


def _params(*sem):
    return pltpu.CompilerParams(dimension_semantics=sem, vmem_limit_bytes=VMEM_LIMIT)


def _pack_planes(y, o_ref, rows=slice(None)):
    bits = lax.bitcast_convert_type(_bf16_round(y), jnp.uint32)
    half = D_MODEL // 2
    word = bits[:, half:] | lax.shift_right_logical(bits[:, :half], jnp.uint32(16))
    word = lax.bitcast_convert_type(word, jnp.int32)
    for p in range(N_PLANES):
        o_ref[p, rows, :] = word[:, p * LANES:(p + 1) * LANES]


def _unpack_planes(planes):
    lo, hi = [], []
    for w in planes:
        u = lax.bitcast_convert_type(w, jnp.uint32)
        lo.append(lax.bitcast_convert_type(lax.shift_left(u, jnp.uint32(16)), F32))
        hi.append(lax.bitcast_convert_type(u & jnp.uint32(0xFFFF0000), F32))
    return jnp.concatenate(lo + hi, axis=1)


def _combine(route_ref, y_ref):
    rt = route_ref[...]
    y0 = _unpack_planes([y_ref[0, p] for p in range(N_PLANES)])
    y1 = _unpack_planes([y_ref[1, p] for p in range(N_PLANES)])
    return rt[:, 2:3] * y0 + rt[:, 3:4] * y1


def _mod_kernel(c_ref, w_ref, b_ref, o_ref):
    c = c_ref[...]
    cond = c / (1.0 + jnp.exp(-c))
    ch = cond.astype(BF16)
    cl = (cond - ch.astype(F32)).astype(BF16)
    w = w_ref[...]
    wh = w.astype(BF16)
    wl = (w - wh.astype(F32)).astype(BF16)
    acc = jnp.dot(ch, wh, preferred_element_type=F32)
    acc += jnp.dot(cl, wh, preferred_element_type=F32)
    acc += jnp.dot(ch, wl, preferred_element_type=F32)
    o_ref[...] = acc + b_ref[...]


def _modulation(c, w_ada, b_ada):
    rows = 16
    tn = 1536
    c_pad = jnp.zeros((rows, D_MODEL), F32).at[:BATCH].set(c)
    out = pl.pallas_call(
        _mod_kernel,
        grid=(DEPTH, 6 * D_MODEL // tn),
        in_specs=[
            pl.BlockSpec((rows, D_MODEL), lambda l, n: (0, 0)),
            pl.BlockSpec((None, D_MODEL, tn), lambda l, n: (l, 0, n)),
            pl.BlockSpec((None, 1, tn), lambda l, n: (l, 0, n)),
        ],
        out_specs=pl.BlockSpec((None, rows, tn), lambda l, n: (l, 0, n)),
        out_shape=jax.ShapeDtypeStruct((DEPTH, rows, 6 * D_MODEL), F32),
        compiler_params=_params("arbitrary", "arbitrary"),
        name="adaln_mod",
    )(c_pad, w_ada, b_ada.reshape(DEPTH, 1, 6 * D_MODEL))
    return out[:, :BATCH]


def _rope_kernel(pos_ref, inv_ref, c_ref, sa_ref, sb_ref):
    ang = pos_ref[...].astype(F32) * inv_ref[...]
    j = _lane_iota(ang.shape) % HEAD_DIM
    cosv = jnp.cos(ang)
    sinv = jnp.sin(ang)
    half = ROT_DIM // 2
    c_ref[...] = jnp.where(j < ROT_DIM, cosv, 1.0)
    sa_ref[...] = jnp.where(j < half, -sinv, 0.0)
    sb_ref[...] = jnp.where((j >= half) & (j < ROT_DIM), sinv, 0.0)


def _rope_tables(positions):
    half = ROT_DIM // 2
    inv = ROPE_THETA ** (-jnp.arange(0, ROT_DIM, 2, dtype=F32) / ROT_DIM)
    lane = np.arange(LANES)
    inv_lane = inv[(lane % HEAD_DIM) % half].reshape(1, LANES)
    spec = pl.BlockSpec((TM, LANES), lambda i: (i, 0))
    shape = jax.ShapeDtypeStruct((N_TOK, LANES), F32)
    return pl.pallas_call(
        _rope_kernel,
        grid=(N_TOK // TM,),
        in_specs=[pl.BlockSpec((TM, 1), lambda i: (i, 0)),
                  pl.BlockSpec((1, LANES), lambda i: (0, 0))],
        out_specs=[spec, spec, spec],
        out_shape=[shape, shape, shape],
        compiler_params=_params("arbitrary"),
        name="rope_tables",
    )(positions.reshape(N_TOK, 1), inv_lane)


def _rms_mod(x, g, sc, sh):
    ms = jnp.mean(x * x, axis=-1, keepdims=True)
    return (x * lax.rsqrt(ms + EPS) * g) * (1.0 + sc) + sh


def _inproj_kernel(fuse, *refs):
    if fuse:
        (x_ref, rt_ref, y_ref, gt_ref, sc_ref, sh_ref, g_ref, w_ref, bf_ref, c_ref, sa_ref, sb_ref,
         pq_ref, xo_ref, dq_ref, dk_ref, dv_ref, fq_ref, fk_ref, fv_ref, carry_ref, h_even, h_odd) = refs
    else:
        (x_ref, sc_ref, sh_ref, g_ref, w_ref, bf_ref, c_ref, sa_ref, sb_ref,
         pq_ref, dq_ref, dk_ref, dv_ref, fq_ref, fk_ref, fv_ref, carry_ref, h_even, h_odd) = refs
    step = pl.program_id(0)

    def normalise(h_ref):
        if fuse:
            x = x_ref[...] + gt_ref[...] * _combine(rt_ref, y_ref)
            xo_ref[...] = x
        else:
            x = x_ref[...]
        h_ref[...] = _rms_mod(x, g_ref[...], sc_ref[...], sh_ref[...]).astype(BF16)

    @pl.when(step == 0)
    def _():
        h_odd[...] = jnp.zeros_like(h_odd)

    @pl.when((step == 0) | ((step - 1) % (SEQ // TM) == 0))
    def _():
        carry_ref[...] = jnp.zeros_like(carry_ref)

    for parity, (h_new, h_old) in enumerate(((h_even, h_odd), (h_odd, h_even))):
        @pl.when(step % 2 == parity)
        def _():
            _inproj_project(h_old[...], w_ref, bf_ref, c_ref, sa_ref, sb_ref, pq_ref, dq_ref, dk_ref,
                            dv_ref, fq_ref, fk_ref, fv_ref, carry_ref)
            normalise(h_new)


def _inproj_project(hb, w_ref, bf_ref, c_ref, sa_ref, sb_ref, pq_ref, dq_ref, dk_ref, dv_ref, fq_ref,
                    fk_ref, fv_ref, carry_ref):
    lane = _lane_iota((TM, LANES))
    nh = N_FOX_HEADS

    def pack3(a):
        hi = _bf16_round(a)
        r1 = a - hi
        mid = _bf16_round(r1)
        lo = _bf16_round(r1 - mid)
        return jnp.where(lane < nh, hi,
                         jnp.where(lane < 2 * nh, pltpu.roll(mid, nh, 1),
                                   jnp.where(lane < 3 * nh, pltpu.roll(lo, 2 * nh, 1), 0.0)))

    z = jnp.dot(hb, w_ref[:, FF_COL:FF_COL + LANES], preferred_element_type=F32) + bf_ref[...]

    low = lane < HEAD_DIM
    rc, rsa, rsb = c_ref[...], sa_ref[...], sb_ref[...]
    scale = HEAD_DIM ** -0.5 * LOG2E

    def split_store(chunk, o_ref, m, extra_a=None, extra_b=None):
        a = jnp.where(low, chunk, 0.0)
        b = jnp.where(low, pltpu.roll(chunk, HEAD_DIM, 1), 0.0)
        if extra_a is not None:
            a = a + extra_a
            b = b + extra_b
        o_ref[:, (2 * m) * LANES:(2 * m + 1) * LANES] = a.astype(BF16)
        o_ref[:, (2 * m + 1) * LANES:(2 * m + 2) * LANES] = b.astype(BF16)

    def rope(xc):
        return xc * rc + pltpu.roll(xc, LANES - ROT_DIM // 2, 1) * rsa + pltpu.roll(xc, ROT_DIM // 2, 1) * rsb

    pdq = jnp.dot(hb, w_ref[:, 0:DIFF_WIDTH], preferred_element_type=F32)
    for m in range(N_DIFF_HEADS):
        split_store(rope(pdq[:, m * LANES:(m + 1) * LANES]) * scale, dq_ref, m)
    pdk = jnp.dot(hb, w_ref[:, DIFF_WIDTH:2 * DIFF_WIDTH], preferred_element_type=F32)
    for m in range(N_DIFF_HEADS):
        split_store(rope(pdk[:, m * LANES:(m + 1) * LANES]), dk_ref, m)
    def store_values_t(pv, o_ref, width):
        ones = jnp.ones((ONES_ROWS, TM), BF16)
        for m in range(4):
            vt = pv[:, m * LANES:(m + 1) * LANES].T.astype(BF16)
            for i in range(LANES // width):
                o_ref[m * (LANES // width) + i, 0:width, :] = vt[i * width:(i + 1) * width]
                o_ref[m * (LANES // width) + i, width:width + ONES_ROWS, :] = ones

    store_values_t(jnp.dot(hb, w_ref[:, 2 * DIFF_WIDTH:3 * DIFF_WIDTH], preferred_element_type=F32),
                   dv_ref, 2 * HEAD_DIM)
    o = 3 * DIFF_WIDTH
    store_values_t(jnp.dot(hb, w_ref[:, o + 2 * FOX_WIDTH:o + 3 * FOX_WIDTH],
                           preferred_element_type=F32), fv_ref, HEAD_DIM)

    logf =jnp.minimum(z, 0.0) - jnp.log(1.0 + jnp.exp(-jnp.abs(z)))
    logf = jnp.where(lane < nh, logf, 0.0)
    row = lax.broadcasted_iota(jnp.int32, (TM, TM), 0)
    col = lax.broadcasted_iota(jnp.int32, (TM, TM), 1)
    tri = (row >= col).astype(BF16)
    r = jnp.dot(tri, pack3(logf).astype(BF16), preferred_element_type=F32)
    cs = r + pltpu.roll(r, LANES - nh, 1) + pltpu.roll(r, LANES - 2 * nh, 1)
    cf = jnp.where(lane < nh, cs + carry_ref[0:1, :], 0.0)
    carry_ref[...] = jnp.broadcast_to(cf[TM - 1:TM, :], carry_ref.shape)

    t3 = jnp.where(lane == 3 * nh, 1.0, pack3(cf * LOG2E)).astype(BF16)
    aug = jnp.dot(t3, pq_ref[...], preferred_element_type=F32)

    pfq =jnp.dot(hb, w_ref[:, o:o + FOX_WIDTH], preferred_element_type=F32)
    for m in range(N_FOX_HEADS // 2):
        split_store(pfq[:, m * LANES:(m + 1) * LANES] * scale, fq_ref, m,
                    aug[:, (2 * m) * LANES:(2 * m + 1) * LANES],
                    aug[:, (2 * m + 1) * LANES:(2 * m + 2) * LANES])
    pfk = jnp.dot(hb, w_ref[:, o + FOX_WIDTH:o + 2 * FOX_WIDTH], preferred_element_type=F32)
    for m in range(N_FOX_HEADS // 2):
        split_store(pfk[:, m * LANES:(m + 1) * LANES], fk_ref, m,
                    aug[:, QK_WIDTH + (2 * m) * LANES:QK_WIDTH + (2 * m + 1) * LANES],
                    aug[:, QK_WIDTH + (2 * m + 1) * LANES:QK_WIDTH + (2 * m + 2) * LANES])


def _forget_placement():
    nh = N_FOX_HEADS
    p = np.zeros((LANES, 2 * QK_WIDTH), np.float32)
    for h in range(nh):
        base_q = h * LANES + HEAD_DIM
        base_k = QK_WIDTH + h * LANES + HEAD_DIM
        for part in range(3):
            p[part * nh + h, base_q + part] = 1.0
            p[3 * nh, base_q + 3 + part] = 1.0
            p[3 * nh, base_k + part] = 1.0
            p[part * nh + h, base_k + 3 + part] = -1.0
    return jnp.asarray(p, BF16)


def _inproj(x, moe, gate, sc, sh, g, w_bf, b_forget, tables, pq):
    fuse = moe is not None
    tpb = SEQ // TM
    n_tiles = N_TOK // TM
    new = lambda s: jnp.minimum(s, n_tiles - 1)
    old = lambda s: jnp.maximum(s - 1, 0)
    row = pl.BlockSpec((TM, D_MODEL), lambda s: (new(s), 0))
    per_batch = pl.BlockSpec((None, 1, D_MODEL), lambda s: (new(s) // tpb, 0, 0))
    const = lambda shape: pl.BlockSpec(shape, lambda s: (0,) * len(shape))
    tab = pl.BlockSpec((TM, LANES), lambda s: (old(s), 0))
    in_specs = [row]
    args = [x]
    if fuse:
        in_specs += [pl.BlockSpec((TM, LANES), lambda s: (new(s), 0)),
                     pl.BlockSpec((TOP_K, N_PLANES, TM, LANES), lambda s: (0, 0, new(s), 0)), per_batch]
        args += [moe[0], moe[1], gate]
    layer, w_all = w_bf
    in_specs += [per_batch, per_batch, const((1, D_MODEL)),
                 pl.BlockSpec((None, D_MODEL, IN_COLS_PAD), lambda s: (layer, 0, 0)),
                 const((1, LANES)), tab, tab, tab, const((LANES, 2 * QK_WIDTH))]
    args += [sc, sh, g.reshape(1, D_MODEL), w_all, b_forget, *tables, pq]
    wide = pl.BlockSpec((TM, QK_WIDTH), lambda s: (old(s), 0))
    def vspec(heads, width):
        rows = width + ONES_ROWS
        return (pl.BlockSpec((None, heads, None, rows, TM),
                             lambda s: (old(s) // tpb, 0, old(s) % tpb, 0, 0)),
                jax.ShapeDtypeStruct((BATCH, heads, tpb, rows, TM), BF16))

    wide_s = jax.ShapeDtypeStruct((N_TOK, QK_WIDTH), BF16)
    dv_spec, dv_s = vspec(N_DIFF_HEADS, 2 * HEAD_DIM)
    fv_spec, fv_s = vspec(N_FOX_HEADS, HEAD_DIM)
    out_specs = [wide, wide, dv_spec, wide, wide, fv_spec]
    out_shape = [wide_s, wide_s, dv_s, wide_s, wide_s, fv_s]
    if fuse:
        out_specs = [row] + out_specs
        out_shape = [jax.ShapeDtypeStruct((N_TOK, D_MODEL), F32)] + out_shape
    outs = pl.pallas_call(
        functools.partial(_inproj_kernel, fuse),
        grid=(n_tiles + 1,),
        in_specs=in_specs,
        out_specs=out_specs,
        out_shape=out_shape,
        scratch_shapes=[pltpu.VMEM((8, LANES), F32), pltpu.VMEM((TM, D_MODEL), BF16),
                        pltpu.VMEM((TM, D_MODEL), BF16)],
        compiler_params=_params("arbitrary"),
        name="norm_inproj",
    )(*args)
    if fuse:
        return outs[0], outs[1:]
    return x, outs


def _attn_kernel(diff, lambda_init, *refs):
    q_refs, k_refs = refs[:ATTN_MAPS], refs[ATTN_MAPS:2 * ATTN_MAPS]
    v_ref, g_ref, lam_ref, o_ref = refs[2 * ATTN_MAPS:2 * ATTN_MAPS + 4]
    scratch = refs[2 * ATTN_MAPS + 4:]
    nq = SEQ // TQ
    n_half = TQ // ATTN_TQ
    feat = 2 * HEAD_DIM if diff else HEAD_DIM
    chains = []
    for mi, (q_ref, k_ref) in enumerate(zip(q_refs, k_refs)):
        for h in range(n_half):
            c = mi * n_half + h
            qt_sc, s_sc, p_sc, m_sc, a_sc, acc_sc = scratch[c::ATTN_MAPS * n_half]
            vh = mi // 2 if diff else mi
            chains.append((h, k_ref, qt_sc, s_sc, p_sc, m_sc, a_sc, acc_sc, q_ref, vh))
    order = [chains[mi * n_half + h] for h in range(n_half) for mi in range(ATTN_MAPS)]
    early, late = order[:-N_LATE_CHAINS], order[-N_LATE_CHAINS:]

    def load_queries(qi):
        for mi in range(ATTN_MAPS):
            q_ref = chains[mi * n_half][8]
            qt = q_ref[pl.ds(pl.multiple_of(qi * TQ, TQ), TQ), :].astype(F32).T.astype(BF16)
            for h in range(n_half):
                chains[mi * n_half + h][2][...] = qt[:, h * ATTN_TQ:(h + 1) * ATTN_TQ]

    def reset_state():
        for chain in chains:
            m_sc, _, acc_sc = chain[5:8]
            m_sc[...] = jnp.full(m_sc.shape, NEG, F32)
            acc_sc[...] = jnp.zeros(acc_sc.shape, F32)

    def n_keys(chain, masked):
        return (chain[0] + 1) * ATTN_TQ if masked else TQ

    def scores(chain, j, masked):
        h, k_ref, qt_sc, s_sc = chain[:4]
        nk = n_keys(chain, masked)
        off = pl.multiple_of(j * TQ, TQ)
        s = jnp.dot(k_ref[pl.ds(off, nk), :], qt_sc[...], preferred_element_type=F32)
        if masked:
            kk = lax.broadcasted_iota(jnp.int32, (nk, ATTN_TQ), 0)
            qq = h * ATTN_TQ + lax.broadcasted_iota(jnp.int32, (nk, ATTN_TQ), 1)
            s = jnp.where((kk // CHUNK <= qq // CHUNK) if diff else (kk <= qq), s, NEG)
        s_sc[0:nk, :] = s

    def softmax(chain, masked):
        s_sc, p_sc, m_sc, a_sc = chain[3:7]
        nk = n_keys(chain, masked)
        m_all = m_sc[...]
        m_parts = []
        for c0 in range(0, ATTN_TQ, LANES):
            cols = slice(c0, c0 + LANES)
            pm = s_sc[0:ATTN_ROWS, cols]
            for r0 in range(ATTN_ROWS, nk, ATTN_ROWS):
                pm = jnp.maximum(pm, s_sc[r0:r0 + ATTN_ROWS, cols])
            m_new = jnp.maximum(m_all[:, cols], jnp.max(pm, axis=0, keepdims=True))
            for r0 in range(0, nk, ATTN_ROWS):
                p = jnp.exp2(s_sc[r0:r0 + ATTN_ROWS, cols] - m_new)
                p_sc[r0:r0 + ATTN_ROWS, cols] = p.astype(BF16)
            m_parts.append(m_new)
        m_new = jnp.concatenate(m_parts, axis=1)
        a_sc[...] = jnp.exp2(m_all - m_new)
        m_sc[...] = m_new

    def values(chain, j, masked=False):
        p_sc, a_sc, acc_sc, vh = chain[4], chain[6], chain[7], chain[9]
        nk = n_keys(chain, masked)
        pv = jnp.dot(v_ref[vh, j, :, 0:nk], p_sc[0:nk, :], preferred_element_type=F32)
        acc_sc[...] = a_sc[...] * acc_sc[...] + pv

    def idle_late():
        for chain in late:
            chain[4][...] = jnp.zeros(chain[4].shape, BF16)
            chain[6][...] = jnp.ones(chain[6].shape, F32)

    def consume(j, cur_masked=False, nxt=None, nxt_masked=False, final=False, before_next=None):
        def open_late(chain):
            scores(chain, j, cur_masked)
            values(chain, jnp.maximum(j - 1, 0))

        open_late(late[0])
        for i, chain in enumerate(early):
            softmax(chain, cur_masked)
            if i == 0:
                for other in late[1:]:
                    open_late(other)
                if before_next is not None:
                    before_next()
            if nxt is not None:
                scores(chain, nxt, nxt_masked)
            values(chain, j, cur_masked)
        for chain in late:
            softmax(chain, cur_masked)
        if final:
            for chain in late:
                values(chain, j, cur_masked)

    def finalize(qi):
        g = g_ref[...]
        rows = pl.ds(pl.multiple_of(qi * TQ, TQ), TQ)
        for pair in range(ATTN_MAPS // 2):
            ot = [jnp.concatenate([chains[mi * n_half + h][7][0:feat] / chains[mi * n_half + h][7][feat:feat + 1]
                                   for h in range(n_half)], axis=1)
                  for mi in (2 * pair, 2 * pair + 1)]
            cols = slice(pair * LANES, (pair + 1) * LANES)
            if diff:
                lv = lam_ref[...]
                lam = (jnp.exp(jnp.sum(lv[0:1] * lv[1:2], axis=1, keepdims=True))
                       - jnp.exp(jnp.sum(lv[2:3] * lv[3:4], axis=1, keepdims=True)) + lambda_init)
                o = (ot[0] - lam * ot[1]).T
                y = o * lax.rsqrt(jnp.mean(o * o, axis=1, keepdims=True) + EPS) * g
                o_ref[rows, cols] = (y * (1.0 - lambda_init)).astype(o_ref.dtype)
            else:
                o = jnp.concatenate(ot, axis=0).T
                low = _lane_iota((TQ, LANES)) < HEAD_DIM
                sq = o * o
                msa = jnp.sum(jnp.where(low, sq, 0.0), axis=1, keepdims=True) / HEAD_DIM
                msb = jnp.sum(jnp.where(low, 0.0, sq), axis=1, keepdims=True) / HEAD_DIM
                inv = jnp.where(low, lax.rsqrt(msa + EPS), lax.rsqrt(msb + EPS))
                o_ref[rows, cols] = (o * inv * g).astype(o_ref.dtype)

    load_queries(0)
    reset_state()
    idle_late()
    for chain in early:
        scores(chain, 0, True)

    @pl.loop(0, nq)
    def _(qi):
        n_plain = jnp.maximum(qi - 1, 0)

        def run(first, count):
            for i in range(count):
                consume(first + i, nxt=first + i + 1)

        @pl.loop(0, n_plain // 4)
        def _(t):
            run(4 * t, 4)

        done = (n_plain // 4) * 4

        @pl.when(n_plain - done >= 2)
        def _():
            run(done, 2)

        def last_blocks(to_next_tile):
            if to_next_tile:
                consume(qi, cur_masked=True, nxt=0, final=True, before_next=lambda: load_queries(qi + 1))
            else:
                consume(qi, cur_masked=True, final=True)
            finalize(qi)

        for to_next_tile in (True, False):
            more = (qi < nq - 1) if to_next_tile else (qi == nq - 1)

            last_tile_odd = (nq - 2) % 2 == 1
            for odd in ((True, False) if to_next_tile else (last_tile_odd,)):
                @pl.when(more & (qi > 0) & ((n_plain % 2 == 1) == odd))
                def _():
                    if odd:
                        run(qi - 2, 1)
                    consume(qi - 1, nxt=qi, nxt_masked=True)
                    last_blocks(to_next_tile)

            if to_next_tile:
                @pl.when(qi == 0)
                def _():
                    last_blocks(to_next_tile)

        reset_state()
        idle_late()


def _attention(diff, lambda_init, q, k, v, g, lamv):
    nq = SEQ // TQ
    n_steps = 8 // ATTN_MAPS
    v_heads = v.shape[1] // n_steps
    kspec = lambda m: pl.BlockSpec((SEQ, LANES), lambda b, p: (b, ATTN_MAPS * p + m))
    maps = [kspec(m) for m in range(ATTN_MAPS)]
    return pl.pallas_call(
        functools.partial(_attn_kernel, diff, lambda_init),
        grid=(BATCH, n_steps),
        in_specs=maps + maps + [
            pl.BlockSpec((None, v_heads, nq, v.shape[3], TQ), lambda b, p: (b, p, 0, 0, 0)),
            pl.BlockSpec((1, LANES), lambda b, p: (0, 0)),
            pl.BlockSpec((8, LANES), lambda b, p: (0, 0))],
        out_specs=pl.BlockSpec((SEQ, ATTN_MAPS // 2 * LANES), lambda b, p: (b, p)),
        out_shape=jax.ShapeDtypeStruct((N_TOK, DIFF_WIDTH), BF16),
        scratch_shapes=[pltpu.VMEM(shape, dt)
                        for shape, dt in (((LANES, ATTN_TQ), BF16), ((TQ, ATTN_TQ), F32),
                                          ((TQ, ATTN_TQ), BF16), ((1, ATTN_TQ), F32),
                                          ((1, ATTN_TQ), F32), ((v.shape[3], ATTN_TQ), F32))
                        for _ in range(ATTN_MAPS * TQ // ATTN_TQ)],
        compiler_params=_params("arbitrary", "arbitrary"),
        name="diff_attention" if diff else "fox_attention",
    )(*([q] * ATTN_MAPS), *([k] * ATTN_MAPS), v, g, lamv)


def _outproj_kernel(x_ref, od_ref, of_ref, gt_ref, sc_ref, sh_ref, g_ref, wo_ref, wr_ref, br_ref,
                    x1_ref, h2_ref, rt_ref, rtt_ref, cnt_ref, carry_ref):
    @pl.when(pl.program_id(0) == 0)
    def _():
        carry_ref[...] = jnp.zeros_like(carry_ref)

    tiles = [slice(t * TM, (t + 1) * TM) for t in range(OUTPROJ_TILES)]
    lane = _lane_iota((TM, LANES))
    lanef = lane.astype(F32)
    big = float(LANES)

    def project(rows):
        mix = jnp.dot(od_ref[rows, :], wo_ref[0:DIFF_WIDTH, :], preferred_element_type=F32)
        return mix + jnp.dot(of_ref[rows, :], wo_ref[DIFF_WIDTH:, :], preferred_element_type=F32)

    def normalise(rows, mix):
        x1 = x_ref[rows, :] + gt_ref[...] * mix
        x1_ref[rows, :] = x1
        h = _rms_mod(x1, g_ref[...], sc_ref[...], sh_ref[...])
        hh = h.astype(BF16)
        _pack_planes(h, h2_ref, rows)
        return hh, (h - hh.astype(F32)).astype(BF16)

    def router_logits(hh, hl):
        r1 = jnp.dot(hh, wr_ref[...], preferred_element_type=F32)
        r2 = jnp.dot(hl, wr_ref[:, 0:LANES], preferred_element_type=F32)
        return r1[:, 0:LANES] + r1[:, LANES:] + r2 + br_ref[...]

    def top_k(logits):
        isg = lane < N_GROUPS
        lg = jnp.where(isg, logits, NEG)
        mg = jnp.max(lg, axis=1, keepdims=True)
        sg = jnp.sum(jnp.where(isg, jnp.exp(lg - mg), 0.0), axis=1, keepdims=True)
        p_g = 1.0 / sg
        gsel = jnp.min(jnp.where(isg & (lg == mg), lanef, big), axis=1, keepdims=True)
        lo = N_GROUPS + gsel * EXPERTS_PER_GROUP
        ise = (lanef >= lo) & (lanef < lo + EXPERTS_PER_GROUP)
        le = jnp.where(ise, logits, NEG)
        t1 = jnp.max(le, axis=1, keepdims=True)
        i1 = jnp.min(jnp.where(ise & (le == t1), lanef, big), axis=1, keepdims=True)
        ise2 = ise & (lanef != i1)
        le2 = jnp.where(ise2, logits, NEG)
        t2 = jnp.max(le2, axis=1, keepdims=True)
        i2 = jnp.min(jnp.where(ise2 & (le2 == t2), lanef, big), axis=1, keepdims=True)
        d = jnp.exp(t2 - t1)
        return i1 - N_GROUPS, i2 - N_GROUPS, p_g / (1.0 + d), p_g * d / (1.0 + d)

    def earlier_in_tile(e1, e2):
        both = jnp.where((lanef == e1) | (lanef == e2), 1.0, 0.0)
        row = lax.broadcasted_iota(jnp.int32, (TM, TM), 0)
        col = lax.broadcasted_iota(jnp.int32, (TM, TM), 1)
        before = jnp.dot((row > col).astype(BF16), both.astype(BF16), preferred_element_type=F32)
        return before, jnp.sum(both, axis=0, keepdims=True)

    mixes = [project(rows) for rows in tiles]
    splits = [normalise(rows, mix) for rows, mix in zip(tiles, mixes)]
    logits = [router_logits(hh, hl) for hh, hl in splits]
    picks = [top_k(lg) for lg in logits]
    befores = [earlier_in_tile(e1, e2) for e1, e2, _, _ in picks]
    counts = carry_ref[0:1, :]
    for rows, (e1, e2, w1, w2), (before, added) in zip(tiles, picks, befores):
        before = before + counts
        rank1 = jnp.sum(jnp.where(lanef == e1, before, 0.0), axis=1, keepdims=True)
        rank2 = jnp.sum(jnp.where(lanef == e2, before, 0.0), axis=1, keepdims=True)
        out = jnp.zeros((TM, LANES), F32)
        for j, v in enumerate((e1, e2, w1, w2, rank1, rank2)):
            out = jnp.where(lane == j, v, out)
        rt_ref[rows, :] = out
        rtt_ref[:, rows] = out.T[0:8, :]
        counts = counts + added
    carry_ref[...] = jnp.broadcast_to(counts, carry_ref.shape)
    cnt_ref[...] = jnp.broadcast_to(counts, cnt_ref.shape)


def _outproj(x, od, of, gt, sc, sh, g, wo_bf, wr, br):
    tm = OUTPROJ_TILES * TM
    tpb = SEQ // tm
    row = pl.BlockSpec((tm, D_MODEL), lambda i: (i, 0))
    half = pl.BlockSpec((tm, DIFF_WIDTH), lambda i: (i, 0))
    per_batch = pl.BlockSpec((None, 1, D_MODEL), lambda i: (i // tpb, 0, 0))
    const = lambda shape: pl.BlockSpec(shape, lambda i: (0,) * len(shape))
    return pl.pallas_call(
        _outproj_kernel,
        grid=(N_TOK // tm,),
        in_specs=[row, half, half, per_batch, per_batch, per_batch, const((1, D_MODEL)),
                  const((D_MODEL, D_MODEL)), const((D_MODEL, 2 * LANES)), const((1, LANES))],
        out_specs=[row, pl.BlockSpec((N_PLANES, tm, LANES), lambda i: (0, i, 0)),
                   pl.BlockSpec((tm, LANES), lambda i: (i, 0)), pl.BlockSpec((8, tm), lambda i: (0, i)),
                   const((8, LANES))],
        out_shape=[jax.ShapeDtypeStruct((N_TOK, D_MODEL), F32),
                   jax.ShapeDtypeStruct((N_PLANES, N_TOK, LANES), jnp.int32),
                   jax.ShapeDtypeStruct((N_TOK, LANES), F32),
                   jax.ShapeDtypeStruct((8, N_TOK), F32),
                   jax.ShapeDtypeStruct((8, LANES), F32)],
        scratch_shapes=[pltpu.VMEM((8, LANES), F32)],
        compiler_params=_params("arbitrary"),
        name="outproj_router",
    )(x, od, of, gt, sc, sh, g.reshape(1, D_MODEL), wo_bf, wr, br)


def _expert_kernel(layer, be_ref, cnt_ref, first_ref, slot_ref, next_ref, last_ref, xs_ref, wg_hbm, wu_hbm,
                   wd_hbm, ys_ref, wg_sc, wu_sc, wd_sc, wg_f32, wu_f32, wd_f32, sem):
    i = pl.program_id(0)
    cnt = cnt_ref[i]

    def weight_copies(expert, slot):
        return [pltpu.make_async_copy(hbm.at[layer, expert], buf.at[slot], sem.at[slot, n])
                for n, (hbm, buf) in enumerate(((wg_hbm, wg_f32), (wu_hbm, wu_f32), (wd_hbm, wd_f32)))]

    @pl.when(first_ref[i] == 1)
    def _():
        slot = slot_ref[i]

        @pl.when(i == 0)
        def _():
            for copy in weight_copies(be_ref[0], 0):
                copy.start()

        for copy in weight_copies(be_ref[i], slot):
            copy.wait()
        wg_sc[...] = wg_f32[slot].astype(BF16)
        wu_sc[...] = wu_f32[slot].astype(BF16)
        wd_sc[...] = wd_f32[slot].astype(BF16)

        @pl.when(next_ref[i] >= 0)
        def _():
            for copy in weight_copies(next_ref[i], 1 - slot):
                copy.start()

    def mlp(n_rows):
        rows = slice(0, n_rows)
        live = lax.broadcasted_iota(jnp.int32, (n_rows, LANES), 0) < cnt
        xb = _unpack_planes([jnp.where(live, xs_ref[p, rows, :], 0) for p in range(N_PLANES)]).astype(BF16)
        a = jnp.dot(xb, wg_sc[...], preferred_element_type=F32)
        u = jnp.dot(xb, wu_sc[...], preferred_element_type=F32)
        hid = (a / (1.0 + jnp.exp(-a)) * u).astype(BF16)
        _pack_planes(jnp.dot(hid, wd_sc[...], preferred_element_type=F32), ys_ref, rows)
        if n_rows < MOE_BLOCK:
            ys_ref[:, n_rows:, :] = jnp.zeros((N_PLANES, MOE_BLOCK - n_rows, LANES), ys_ref.dtype)

    half = MOE_BLOCK // 2

    @pl.when(cnt > half)
    def _():
        mlp(MOE_BLOCK)

    @pl.when((cnt > 0) & (cnt <= half))
    def _():
        mlp(half)


def _experts(layer, block_expert, block_count, xs, wg, wu, wd):
    idx = jnp.arange(MOE_NBLOCKS, dtype=jnp.int32)
    first = jnp.concatenate([jnp.ones((1,), jnp.bool_), block_expert[1:] != block_expert[:-1]])
    slot = (jnp.cumsum(first.astype(jnp.int32)) - 1) % 2
    later_first = (idx[None, :] > idx[:, None]) & first[None, :]
    nxt = jnp.min(jnp.where(later_first, block_expert[None, :], N_EXPERTS), axis=1)
    nxt = jnp.where(nxt == N_EXPERTS, -1, nxt).astype(jnp.int32)

    last_used = jnp.maximum(jnp.sum((block_count > 0).astype(jnp.int32)) - 1, 0).reshape(1)
    planes = pl.BlockSpec((N_PLANES, MOE_BLOCK, LANES),
                          lambda i, be, bc, fi, sl, nx, lu: (0, jnp.minimum(i, lu[0]), 0))
    hbm = pl.BlockSpec(memory_space=pl.ANY)
    grid_spec = pltpu.PrefetchScalarGridSpec(
        num_scalar_prefetch=6,
        grid=(MOE_NBLOCKS,),
        in_specs=[planes, hbm, hbm, hbm],
        out_specs=planes,
        scratch_shapes=[pltpu.VMEM((D_MODEL, D_EXPERT), BF16), pltpu.VMEM((D_MODEL, D_EXPERT), BF16),
                        pltpu.VMEM((D_EXPERT, D_MODEL), BF16),
                        pltpu.VMEM((2, D_MODEL, D_EXPERT), F32), pltpu.VMEM((2, D_MODEL, D_EXPERT), F32),
                        pltpu.VMEM((2, D_EXPERT, D_MODEL), F32), pltpu.SemaphoreType.DMA((2, 3))],
    )
    return pl.pallas_call(
        functools.partial(_expert_kernel, layer),
        grid_spec=grid_spec,
        out_shape=jax.ShapeDtypeStruct((N_PLANES, PLANE_ROWS, LANES), jnp.int32),
        compiler_params=_params("arbitrary"),
        name="expert_mlp",
    )(block_expert, block_count, first.astype(jnp.int32), slot.astype(jnp.int32), nxt, last_used,
      xs, wg, wu, wd)


def _slots(route_t, counts):
    counts = counts[0, :N_EXPERTS].astype(jnp.int32)
    padded = ((counts + MOE_BLOCK - 1) // MOE_BLOCK) * MOE_BLOCK
    pend = jnp.cumsum(padded)
    pstart = pend - padded
    bstart = jnp.arange(MOE_NBLOCKS, dtype=jnp.int32) * MOE_BLOCK
    block_expert = jnp.minimum(jnp.sum(bstart[:, None] >= pend[None, :], axis=1), N_EXPERTS - 1)
    block_expert = block_expert.astype(jnp.int32)
    mine = block_expert[:, None] == jnp.arange(N_EXPERTS, dtype=jnp.int32)[None, :]
    left = jnp.sum(jnp.where(mine, counts + pstart, 0), axis=1) - bstart
    block_count = jnp.clip(left, 0, MOE_BLOCK).astype(jnp.int32)
    return _slot_rows(route_t, pstart.astype(jnp.int32)), block_expert, block_count


def _slot_rows_kernel(base_ref, rt_ref, o_ref):
    n_tok = rt_ref.shape[1]
    for k in range(TOP_K):
        e = rt_ref[k:k + 1, :]
        b = jnp.zeros((1, n_tok), F32)
        for j in range(N_EXPERTS):
            b = jnp.where(e == float(j), base_ref[j].astype(F32), b)
        dest = (b + rt_ref[2 * TOP_K + k:2 * TOP_K + k + 1, :]).astype(jnp.int32)
        for p in range(N_PLANES):
            for c in range(n_tok // LANES):
                o_ref[k * N_PLANES + p, c:c + 1, :] = dest[:, c * LANES:(c + 1) * LANES] + p * PLANE_ROWS


def _slot_rows(route_t, base):
    tm = 16 * LANES
    grid_spec = pltpu.PrefetchScalarGridSpec(
        num_scalar_prefetch=1,
        grid=(N_TOK // tm,),
        in_specs=[pl.BlockSpec((8, tm), lambda i, base: (0, i))],
        out_specs=pl.BlockSpec((TOP_K * N_PLANES, tm // LANES, LANES), lambda i, base: (0, i, 0)),
    )
    return pl.pallas_call(
        _slot_rows_kernel,
        grid_spec=grid_spec,
        out_shape=jax.ShapeDtypeStruct((TOP_K * N_PLANES, N_TOK // LANES, LANES), jnp.int32),
        compiler_params=_params("arbitrary"),
        name="slot_rows",
    )(base, route_t)


def _sc_workers():
    info = plsc.get_sparse_core_info()
    return info.num_cores, info.num_cores * info.num_subcores


def _sc_scatter2(src, idx, out_rows):
    n_win = src.shape[0] // SC_WINDOW
    nc, nw = _sc_workers()
    steps = n_win // nw
    mesh = plsc.VectorSubcoreMesh(core_axis_name="c", subcore_axis_name="s")

    @functools.partial(
        pl.kernel, mesh=mesh,
        out_type=jax.ShapeDtypeStruct((out_rows, LANES), src.dtype),
        scratch_types=[pltpu.VMEM((2 * steps, SC_WINDOW), jnp.int32),
                       pltpu.VMEM((SC_INFLIGHT, SC_WINDOW, LANES), src.dtype),
                       pltpu.SemaphoreType.DMA((SC_INFLIGHT,)), pltpu.SemaphoreType.DMA((SC_INFLIGHT,))],
        name="sc_dispatch_scatter",
    )
    def k(src_hbm, idx_hbm, out_hbm, idx_v, rows_v, lsem, wsem):
        first = (lax.axis_index("s") * nc + lax.axis_index("c")) * steps
        pltpu.sync_copy(idx_hbm.at[pl.ds(first, steps)], idx_v.at[pl.ds(0, steps)])
        pltpu.sync_copy(idx_hbm.at[pl.ds(n_win + first, steps)], idx_v.at[pl.ds(steps, steps)])

        @pl.loop(0, steps, step=SC_INFLIGHT)
        def _(j):
            loads = [pltpu.async_copy(src_hbm.at[pl.ds((first + j + b) * SC_WINDOW, SC_WINDOW)],
                                      rows_v.at[b], lsem.at[b]) for b in range(SC_INFLIGHT)]
            writes = []
            for b in range(SC_INFLIGHT):
                loads[b].wait()
                for half in range(TOP_K):
                    dst = out_hbm.at[idx_v.at[half * steps + j + b]]
                    writes.append(pltpu.async_copy(rows_v.at[b], dst, wsem.at[b]))
            for w in writes:
                w.wait()

    return k(src, idx)


def _sc_gather(table, idx):
    n_out = idx.shape[0] * SC_WINDOW
    nc, nw = _sc_workers()
    steps = n_out // nw // SC_WINDOW
    mesh = plsc.VectorSubcoreMesh(core_axis_name="c", subcore_axis_name="s")

    @functools.partial(
        pl.kernel, mesh=mesh,
        out_type=jax.ShapeDtypeStruct((n_out, LANES), table.dtype),
        scratch_types=[pltpu.VMEM((steps, SC_WINDOW), jnp.int32),
                       pltpu.VMEM((SC_INFLIGHT, SC_WINDOW, LANES), table.dtype),
                       pltpu.SemaphoreType.DMA((SC_INFLIGHT,)), pltpu.SemaphoreType.DMA((SC_INFLIGHT,))],
        name="sc_combine_gather",
    )
    def k(table_hbm, idx_hbm, out_hbm, idx_v, rows_v, gsem, wsem):
        first = (lax.axis_index("s") * nc + lax.axis_index("c")) * steps
        pltpu.sync_copy(idx_hbm.at[pl.ds(first, steps)], idx_v)

        @pl.loop(0, steps, step=SC_INFLIGHT)
        def _(j):
            gathers = [pltpu.async_copy(table_hbm.at[idx_v.at[j + b]], rows_v.at[b], gsem.at[b])
                       for b in range(SC_INFLIGHT)]
            writes = []
            for b in range(SC_INFLIGHT):
                gathers[b].wait()
                dst = out_hbm.at[pl.ds((first + j + b) * SC_WINDOW, SC_WINDOW)]
                writes.append(pltpu.async_copy(rows_v.at[b], dst, wsem.at[b]))
            for w in writes:
                w.wait()

    return k(table, idx)


def _final_kernel(x_ref, rt_ref, y_ref, gt_ref, g_ref, o_ref):
    x = x_ref[...] + gt_ref[...] * _combine(rt_ref, y_ref)
    ms = jnp.mean(x * x, axis=-1, keepdims=True)
    o_ref[...] = x * lax.rsqrt(ms + EPS) * g_ref[...]


def _final(x, moe, gate, g):
    tpb = SEQ // TM
    row = pl.BlockSpec((TM, D_MODEL), lambda i: (i, 0))
    return pl.pallas_call(
        _final_kernel,
        grid=(N_TOK // TM,),
        in_specs=[row, pl.BlockSpec((TM, LANES), lambda i: (i, 0)),
                  pl.BlockSpec((TOP_K, N_PLANES, TM, LANES), lambda i: (0, 0, i, 0)),
                  pl.BlockSpec((None, 1, D_MODEL), lambda i: (i // tpb, 0, 0)),
                  pl.BlockSpec((1, D_MODEL), lambda i: (0, 0))],
        out_specs=row,
        out_shape=jax.ShapeDtypeStruct((N_TOK, D_MODEL), F32),
        compiler_params=_params("arbitrary"),
        name="final_norm",
    )(x, moe[0], moe[1], gate, g.reshape(1, D_MODEL))


def kernel(x, c, positions, w_ada, b_ada, g_mix, w_in, b_forget, lambda_q1, lambda_k1, lambda_q2,
           lambda_k2, g_subln, g_fox_out, w_out, g_ffn, w_router_group, b_router_group,
           w_router_expert, b_router_expert, w_expert_gate, w_expert_up, w_expert_down, g_final):
    mod = _modulation(c, w_ada, b_ada)
    mod = mod.reshape(DEPTH, BATCH, 6, 1, D_MODEL)
    tables = _rope_tables(positions)
    pq = _forget_placement()
    w_in_bf = jnp.pad(w_in.astype(BF16), ((0, 0), (0, 0), (0, IN_COLS_PAD - IN_COLS)))
    xf = x.reshape(N_TOK, D_MODEL)
    moe = None
    gate = None
    for l in range(DEPTH):
        sh1, sc1, gt1, sh2, sc2, gt2 = (mod[l, :, j] for j in range(6))
        w_bf = (l, w_in_bf)
        bfp =jnp.pad(b_forget[l], (0, LANES - N_FOX_HEADS)).reshape(1, LANES)
        xf, (dq, dk, dv, fq, fk, fv) = _inproj(xf, moe, gate, sc1, sh1, g_mix[l], w_bf, bfp, tables, pq)

        lambda_init = 0.8 - 0.6 * float(np.exp(-0.3 * l))
        lamv = jnp.zeros((8, LANES), F32).at[0:4, 0:HEAD_DIM].set(
            jnp.stack([lambda_q1[l], lambda_k1[l], lambda_q2[l], lambda_k2[l]]))
        g_d = g_subln[l].reshape(1, LANES)
        g_f = jnp.concatenate([g_fox_out[l], g_fox_out[l]]).reshape(1, LANES)
        od = _attention(True, lambda_init, dq, dk, dv, g_d, lamv)
        of = _attention(False, lambda_init, fq, fk, fv, g_f, lamv)

        wr32 = jnp.pad(jnp.concatenate([w_router_group[l], w_router_expert[l]], axis=1),
                       ((0, 0), (0, LANES - N_GROUPS - N_EXPERTS)))
        wr_hi = wr32.astype(BF16)
        wr_lo = (wr32 - wr_hi.astype(F32)).astype(BF16)
        wr = jnp.concatenate([wr_hi, wr_lo], axis=1)
        br = jnp.pad(jnp.concatenate([b_router_group[l], b_router_expert[l]]),
                     (0, LANES - N_GROUPS - N_EXPERTS)).reshape(1, LANES)
        xf, h2, route, route_t, counts = _outproj(xf, od, of, gt1, sc2, sh2, g_ffn[l],
                                                  w_out[l].astype(BF16), wr, br)

        rows, block_expert, block_count = _slots(route_t, counts)
        rows = rows.reshape(TOP_K * N_PLANES * N_TOK // SC_WINDOW, SC_WINDOW)
        xs = _sc_scatter2(h2.reshape(N_PLANES * N_TOK, LANES), rows, N_PLANES * PLANE_ROWS)
        ys = _experts(l, block_expert, block_count, xs.reshape(N_PLANES, PLANE_ROWS, LANES),
                      w_expert_gate, w_expert_up, w_expert_down)
        y2 = _sc_gather(ys.reshape(N_PLANES * PLANE_ROWS, LANES), rows)
        moe = (route, y2.reshape(TOP_K, N_PLANES, N_TOK, LANES))
        gate = gt2
    out = _final(xf, moe, gate, g_final)
    return out.reshape(BATCH, SEQ, D_MODEL)
```

```python
import functools

import numpy as np
import jax
import jax.numpy as jnp
from jax import lax
from jax.experimental import pallas as pl
from jax.experimental.pallas import tpu as pltpu
from jax.experimental.pallas import tpu_sc as plsc

D_MODEL = 1024
BATCH = 4
SEQ = 4096
DEPTH = 4
N_TOK = BATCH * SEQ

CHUNK = 64
HEAD_DIM = 64
N_DIFF_HEADS = 4
N_FOX_HEADS = 8
DIFF_WIDTH = 512
FOX_WIDTH = 512
IN_COLS = 3 * DIFF_WIDTH + 3 * FOX_WIDTH + N_FOX_HEADS
ROT_DIM = 16
ROPE_THETA = 500000.0
N_GROUPS = 4
EXPERTS_PER_GROUP = 8
N_EXPERTS = 32
TOP_K = 2
D_EXPERT = 512
EPS = 1e-6

LANES = 128
IN_COLS_PAD = 3200
FF_COL = 3 * DIFF_WIDTH + 3 * FOX_WIDTH
QK_WIDTH = 8 * LANES
TM = 512
OUTPROJ_TILES = 2
TQ = 512
ATTN_MAPS = 4
ATTN_TQ = 256
N_LATE_CHAINS = 1
ONES_ROWS = 16
ATTN_ROWS = 128
LOG2E = 1.4426950408889634
MOE_BLOCK = 512
MOE_ROWS = N_TOK * TOP_K + N_EXPERTS * MOE_BLOCK
MOE_NBLOCKS = MOE_ROWS // MOE_BLOCK
PLANE_ROWS = MOE_ROWS
N_PLANES = D_MODEL // 2 // LANES
SC_WINDOW = 128
SC_INFLIGHT = 4
NEG = -1e30
VMEM_LIMIT = 56 * 1024 * 1024

F32 = jnp.float32
BF16 = jnp.bfloat16


def _bf16_round(x):
    return x.astype(BF16).astype(F32)


def _lane_iota(shape):
    return lax.broadcasted_iota(jnp.int32, shape, 1)


def _params(*sem):
    return pltpu.CompilerParams(dimension_semantics=sem, vmem_limit_bytes=VMEM_LIMIT)


def _pack_planes(y, o_ref, rows=slice(None)):
    bits = lax.bitcast_convert_type(_bf16_round(y), jnp.uint32)
    half = D_MODEL // 2
    word = bits[:, half:] | lax.shift_right_logical(bits[:, :half], jnp.uint32(16))
    word = lax.bitcast_convert_type(word, jnp.int32)
    for p in range(N_PLANES):
        o_ref[p, rows, :] = word[:, p * LANES:(p + 1) * LANES]


def _unpack_planes(planes):
    lo, hi = [], []
    for w in planes:
        u = lax.bitcast_convert_type(w, jnp.uint32)
        lo.append(lax.bitcast_convert_type(lax.shift_left(u, jnp.uint32(16)), F32))
        hi.append(lax.bitcast_convert_type(u & jnp.uint32(0xFFFF0000), F32))
    return jnp.concatenate(lo + hi, axis=1)


def _combine(route_ref, y_ref):
    rt = route_ref[...]
    y0 = _unpack_planes([y_ref[0, p] for p in range(N_PLANES)])
    y1 = _unpack_planes([y_ref[1, p] for p in range(N_PLANES)])
    return rt[:, 2:3] * y0 + rt[:, 3:4] * y1


def _mod_kernel(c_ref, w_ref, b_ref, o_ref):
    c = c_ref[...]
    cond = c / (1.0 + jnp.exp(-c))
    ch = cond.astype(BF16)
    cl = (cond - ch.astype(F32)).astype(BF16)
    w = w_ref[...]
    wh = w.astype(BF16)
    wl = (w - wh.astype(F32)).astype(BF16)
    acc = jnp.dot(ch, wh, preferred_element_type=F32)
    acc += jnp.dot(cl, wh, preferred_element_type=F32)
    acc += jnp.dot(ch, wl, preferred_element_type=F32)
    o_ref[...] = acc + b_ref[...]


def _modulation(c, w_ada, b_ada):
    rows = 16
    tn = 1536
    c_pad = jnp.zeros((rows, D_MODEL), F32).at[:BATCH].set(c)
    out = pl.pallas_call(
        _mod_kernel,
        grid=(DEPTH, 6 * D_MODEL // tn),
        in_specs=[
            pl.BlockSpec((rows, D_MODEL), lambda l, n: (0, 0)),
            pl.BlockSpec((None, D_MODEL, tn), lambda l, n: (l, 0, n)),
            pl.BlockSpec((None, 1, tn), lambda l, n: (l, 0, n)),
        ],
        out_specs=pl.BlockSpec((None, rows, tn), lambda l, n: (l, 0, n)),
        out_shape=jax.ShapeDtypeStruct((DEPTH, rows, 6 * D_MODEL), F32),
        compiler_params=_params("arbitrary", "arbitrary"),
        name="adaln_mod",
    )(c_pad, w_ada, b_ada.reshape(DEPTH, 1, 6 * D_MODEL))
    return out[:, :BATCH]


def _rope_kernel(pos_ref, inv_ref, c_ref, sa_ref, sb_ref):
    ang = pos_ref[...].astype(F32) * inv_ref[...]
    j = _lane_iota(ang.shape) % HEAD_DIM
    cosv = jnp.cos(ang)
    sinv = jnp.sin(ang)
    half = ROT_DIM // 2
    c_ref[...] = jnp.where(j < ROT_DIM, cosv, 1.0)
    sa_ref[...] = jnp.where(j < half, -sinv, 0.0)
    sb_ref[...] = jnp.where((j >= half) & (j < ROT_DIM), sinv, 0.0)


def _rope_tables(positions):
    half = ROT_DIM // 2
    inv = ROPE_THETA ** (-jnp.arange(0, ROT_DIM, 2, dtype=F32) / ROT_DIM)
    lane = np.arange(LANES)
    inv_lane = inv[(lane % HEAD_DIM) % half].reshape(1, LANES)
    spec = pl.BlockSpec((TM, LANES), lambda i: (i, 0))
    shape = jax.ShapeDtypeStruct((N_TOK, LANES), F32)
    return pl.pallas_call(
        _rope_kernel,
        grid=(N_TOK // TM,),
        in_specs=[pl.BlockSpec((TM, 1), lambda i: (i, 0)),
                  pl.BlockSpec((1, LANES), lambda i: (0, 0))],
        out_specs=[spec, spec, spec],
        out_shape=[shape, shape, shape],
        compiler_params=_params("arbitrary"),
        name="rope_tables",
    )(positions.reshape(N_TOK, 1), inv_lane)


def _rms_mod(x, g, sc, sh):
    ms = jnp.mean(x * x, axis=-1, keepdims=True)
    return (x * lax.rsqrt(ms + EPS) * g) * (1.0 + sc) + sh


def _inproj_kernel(fuse, *refs):
    if fuse:
        (x_ref, rt_ref, y_ref, gt_ref, sc_ref, sh_ref, g_ref, w_ref, bf_ref, c_ref, sa_ref, sb_ref,
         pq_ref, xo_ref, dq_ref, dk_ref, dv_ref, fq_ref, fk_ref, fv_ref, carry_ref, h_even, h_odd) = refs
    else:
        (x_ref, sc_ref, sh_ref, g_ref, w_ref, bf_ref, c_ref, sa_ref, sb_ref,
         pq_ref, dq_ref, dk_ref, dv_ref, fq_ref, fk_ref, fv_ref, carry_ref, h_even, h_odd) = refs
    step = pl.program_id(0)

    def normalise(h_ref):
        if fuse:
            x = x_ref[...] + gt_ref[...] * _combine(rt_ref, y_ref)
            xo_ref[...] = x
        else:
            x = x_ref[...]
        h_ref[...] = _rms_mod(x, g_ref[...], sc_ref[...], sh_ref[...]).astype(BF16)

    @pl.when(step == 0)
    def _():
        h_odd[...] = jnp.zeros_like(h_odd)

    @pl.when((step == 0) | ((step - 1) % (SEQ // TM) == 0))
    def _():
        carry_ref[...] = jnp.zeros_like(carry_ref)

    for parity, (h_new, h_old) in enumerate(((h_even, h_odd), (h_odd, h_even))):
        @pl.when(step % 2 == parity)
        def _():
            _inproj_project(h_old[...], w_ref, bf_ref, c_ref, sa_ref, sb_ref, pq_ref, dq_ref, dk_ref,
                            dv_ref, fq_ref, fk_ref, fv_ref, carry_ref)
            normalise(h_new)


def _inproj_project(hb, w_ref, bf_ref, c_ref, sa_ref, sb_ref, pq_ref, dq_ref, dk_ref, dv_ref, fq_ref,
                    fk_ref, fv_ref, carry_ref):
    lane = _lane_iota((TM, LANES))
    nh = N_FOX_HEADS

    def pack3(a):
        hi = _bf16_round(a)
        r1 = a - hi
        mid = _bf16_round(r1)
        lo = _bf16_round(r1 - mid)
        return jnp.where(lane < nh, hi,
                         jnp.where(lane < 2 * nh, pltpu.roll(mid, nh, 1),
                                   jnp.where(lane < 3 * nh, pltpu.roll(lo, 2 * nh, 1), 0.0)))

    z = jnp.dot(hb, w_ref[:, FF_COL:FF_COL + LANES], preferred_element_type=F32) + bf_ref[...]

    low = lane < HEAD_DIM
    rc, rsa, rsb = c_ref[...], sa_ref[...], sb_ref[...]
    scale = HEAD_DIM ** -0.5 * LOG2E

    def split_store(chunk, o_ref, m, extra_a=None, extra_b=None):
        a = jnp.where(low, chunk, 0.0)
        b = jnp.where(low, pltpu.roll(chunk, HEAD_DIM, 1), 0.0)
        if extra_a is not None:
            a = a + extra_a
            b = b + extra_b
        o_ref[:, (2 * m) * LANES:(2 * m + 1) * LANES] = a.astype(BF16)
        o_ref[:, (2 * m + 1) * LANES:(2 * m + 2) * LANES] = b.astype(BF16)

    def rope(xc):
        return xc * rc + pltpu.roll(xc, LANES - ROT_DIM // 2, 1) * rsa + pltpu.roll(xc, ROT_DIM // 2, 1) * rsb

    pdq = jnp.dot(hb, w_ref[:, 0:DIFF_WIDTH], preferred_element_type=F32)
    for m in range(N_DIFF_HEADS):
        split_store(rope(pdq[:, m * LANES:(m + 1) * LANES]) * scale, dq_ref, m)
    pdk = jnp.dot(hb, w_ref[:, DIFF_WIDTH:2 * DIFF_WIDTH], preferred_element_type=F32)
    for m in range(N_DIFF_HEADS):
        split_store(rope(pdk[:, m * LANES:(m + 1) * LANES]), dk_ref, m)
    def store_values_t(pv, o_ref, width):
        ones = jnp.ones((ONES_ROWS, TM), BF16)
        for m in range(4):
            vt = pv[:, m * LANES:(m + 1) * LANES].T.astype(BF16)
            for i in range(LANES // width):
                o_ref[m * (LANES // width) + i, 0:width, :] = vt[i * width:(i + 1) * width]
                o_ref[m * (LANES // width) + i, width:width + ONES_ROWS, :] = ones

    store_values_t(jnp.dot(hb, w_ref[:, 2 * DIFF_WIDTH:3 * DIFF_WIDTH], preferred_element_type=F32),
                   dv_ref, 2 * HEAD_DIM)
    o = 3 * DIFF_WIDTH
    store_values_t(jnp.dot(hb, w_ref[:, o + 2 * FOX_WIDTH:o + 3 * FOX_WIDTH],
                           preferred_element_type=F32), fv_ref, HEAD_DIM)

    logf =jnp.minimum(z, 0.0) - jnp.log(1.0 + jnp.exp(-jnp.abs(z)))
    logf = jnp.where(lane < nh, logf, 0.0)
    row = lax.broadcasted_iota(jnp.int32, (TM, TM), 0)
    col = lax.broadcasted_iota(jnp.int32, (TM, TM), 1)
    tri = (row >= col).astype(BF16)
    r = jnp.dot(tri, pack3(logf).astype(BF16), preferred_element_type=F32)
    cs = r + pltpu.roll(r, LANES - nh, 1) + pltpu.roll(r, LANES - 2 * nh, 1)
    cf = jnp.where(lane < nh, cs + carry_ref[0:1, :], 0.0)
    carry_ref[...] = jnp.broadcast_to(cf[TM - 1:TM, :], carry_ref.shape)

    t3 = jnp.where(lane == 3 * nh, 1.0, pack3(cf * LOG2E)).astype(BF16)
    aug = jnp.dot(t3, pq_ref[...], preferred_element_type=F32)

    pfq =jnp.dot(hb, w_ref[:, o:o + FOX_WIDTH], preferred_element_type=F32)
    for m in range(N_FOX_HEADS // 2):
        split_store(pfq[:, m * LANES:(m + 1) * LANES] * scale, fq_ref, m,
                    aug[:, (2 * m) * LANES:(2 * m + 1) * LANES],
                    aug[:, (2 * m + 1) * LANES:(2 * m + 2) * LANES])
    pfk = jnp.dot(hb, w_ref[:, o + FOX_WIDTH:o + 2 * FOX_WIDTH], preferred_element_type=F32)
    for m in range(N_FOX_HEADS // 2):
        split_store(pfk[:, m * LANES:(m + 1) * LANES], fk_ref, m,
                    aug[:, QK_WIDTH + (2 * m) * LANES:QK_WIDTH + (2 * m + 1) * LANES],
                    aug[:, QK_WIDTH + (2 * m + 1) * LANES:QK_WIDTH + (2 * m + 2) * LANES])


def _forget_placement():
    nh = N_FOX_HEADS
    p = np.zeros((LANES, 2 * QK_WIDTH), np.float32)
    for h in range(nh):
        base_q = h * LANES + HEAD_DIM
        base_k = QK_WIDTH + h * LANES + HEAD_DIM
        for part in range(3):
            p[part * nh + h, base_q + part] = 1.0
            p[3 * nh, base_q + 3 + part] = 1.0
            p[3 * nh, base_k + part] = 1.0
            p[part * nh + h, base_k + 3 + part] = -1.0
    return jnp.asarray(p, BF16)


def _inproj(x, moe, gate, sc, sh, g, w_bf, b_forget, tables, pq):
    fuse = moe is not None
    tpb = SEQ // TM
    n_tiles = N_TOK // TM
    new = lambda s: jnp.minimum(s, n_tiles - 1)
    old = lambda s: jnp.maximum(s - 1, 0)
    row = pl.BlockSpec((TM, D_MODEL), lambda s: (new(s), 0))
    per_batch = pl.BlockSpec((None, 1, D_MODEL), lambda s: (new(s) // tpb, 0, 0))
    const = lambda shape: pl.BlockSpec(shape, lambda s: (0,) * len(shape))
    tab = pl.BlockSpec((TM, LANES), lambda s: (old(s), 0))
    in_specs = [row]
    args = [x]
    if fuse:
        in_specs += [pl.BlockSpec((TM, LANES), lambda s: (new(s), 0)),
                     pl.BlockSpec((TOP_K, N_PLANES, TM, LANES), lambda s: (0, 0, new(s), 0)), per_batch]
        args += [moe[0], moe[1], gate]
    layer, w_all = w_bf
    in_specs += [per_batch, per_batch, const((1, D_MODEL)),
                 pl.BlockSpec((None, D_MODEL, IN_COLS_PAD), lambda s: (layer, 0, 0)),
                 const((1, LANES)), tab, tab, tab, const((LANES, 2 * QK_WIDTH))]
    args += [sc, sh, g.reshape(1, D_MODEL), w_all, b_forget, *tables, pq]
    wide = pl.BlockSpec((TM, QK_WIDTH), lambda s: (old(s), 0))
    def vspec(heads, width):
        rows = width + ONES_ROWS
        return (pl.BlockSpec((None, heads, None, rows, TM),
                             lambda s: (old(s) // tpb, 0, old(s) % tpb, 0, 0)),
                jax.ShapeDtypeStruct((BATCH, heads, tpb, rows, TM), BF16))

    wide_s = jax.ShapeDtypeStruct((N_TOK, QK_WIDTH), BF16)
    dv_spec, dv_s = vspec(N_DIFF_HEADS, 2 * HEAD_DIM)
    fv_spec, fv_s = vspec(N_FOX_HEADS, HEAD_DIM)
    out_specs = [wide, wide, dv_spec, wide, wide, fv_spec]
    out_shape = [wide_s, wide_s, dv_s, wide_s, wide_s, fv_s]
    if fuse:
        out_specs = [row] + out_specs
        out_shape = [jax.ShapeDtypeStruct((N_TOK, D_MODEL), F32)] + out_shape
    outs = pl.pallas_call(
        functools.partial(_inproj_kernel, fuse),
        grid=(n_tiles + 1,),
        in_specs=in_specs,
        out_specs=out_specs,
        out_shape=out_shape,
        scratch_shapes=[pltpu.VMEM((8, LANES), F32), pltpu.VMEM((TM, D_MODEL), BF16),
                        pltpu.VMEM((TM, D_MODEL), BF16)],
        compiler_params=_params("arbitrary"),
        name="norm_inproj",
    )(*args)
    if fuse:
        return outs[0], outs[1:]
    return x, outs


def _attn_kernel(diff, lambda_init, *refs):
    q_refs, k_refs = refs[:ATTN_MAPS], refs[ATTN_MAPS:2 * ATTN_MAPS]
    v_ref, g_ref, lam_ref, o_ref = refs[2 * ATTN_MAPS:2 * ATTN_MAPS + 4]
    scratch = refs[2 * ATTN_MAPS + 4:]
    nq = SEQ // TQ
    n_half = TQ // ATTN_TQ
    feat = 2 * HEAD_DIM if diff else HEAD_DIM
    chains = []
    for mi, (q_ref, k_ref) in enumerate(zip(q_refs, k_refs)):
        for h in range(n_half):
            c = mi * n_half + h
            qt_sc, s_sc, p_sc, m_sc, a_sc, acc_sc = scratch[c::ATTN_MAPS * n_half]
            vh = mi // 2 if diff else mi
            chains.append((h, k_ref, qt_sc, s_sc, p_sc, m_sc, a_sc, acc_sc, q_ref, vh))
    order = [chains[mi * n_half + h] for h in range(n_half) for mi in range(ATTN_MAPS)]
    early, late = order[:-N_LATE_CHAINS], order[-N_LATE_CHAINS:]

    def load_queries(qi):
        for mi in range(ATTN_MAPS):
            q_ref = chains[mi * n_half][8]
            qt = q_ref[pl.ds(pl.multiple_of(qi * TQ, TQ), TQ), :].astype(F32).T.astype(BF16)
            for h in range(n_half):
                chains[mi * n_half + h][2][...] = qt[:, h * ATTN_TQ:(h + 1) * ATTN_TQ]

    def reset_state():
        for chain in chains:
            m_sc, _, acc_sc = chain[5:8]
            m_sc[...] = jnp.full(m_sc.shape, NEG, F32)
            acc_sc[...] = jnp.zeros(acc_sc.shape, F32)

    def n_keys(chain, masked):
        return (chain[0] + 1) * ATTN_TQ if masked else TQ

    def scores(chain, j, masked):
        h, k_ref, qt_sc, s_sc = chain[:4]
        nk = n_keys(chain, masked)
        off = pl.multiple_of(j * TQ, TQ)
        s = jnp.dot(k_ref[pl.ds(off, nk), :], qt_sc[...], preferred_element_type=F32)
        if masked:
            kk = lax.broadcasted_iota(jnp.int32, (nk, ATTN_TQ), 0)
            qq = h * ATTN_TQ + lax.broadcasted_iota(jnp.int32, (nk, ATTN_TQ), 1)
            s = jnp.where((kk // CHUNK <= qq // CHUNK) if diff else (kk <= qq), s, NEG)
        s_sc[0:nk, :] = s

    def softmax(chain, masked):
        s_sc, p_sc, m_sc, a_sc = chain[3:7]
        nk = n_keys(chain, masked)
        m_all = m_sc[...]
        m_parts = []
        for c0 in range(0, ATTN_TQ, LANES):
            cols = slice(c0, c0 + LANES)
            pm = s_sc[0:ATTN_ROWS, cols]
            for r0 in range(ATTN_ROWS, nk, ATTN_ROWS):
                pm = jnp.maximum(pm, s_sc[r0:r0 + ATTN_ROWS, cols])
            m_new = jnp.maximum(m_all[:, cols], jnp.max(pm, axis=0, keepdims=True))
            for r0 in range(0, nk, ATTN_ROWS):
                p = jnp.exp2(s_sc[r0:r0 + ATTN_ROWS, cols] - m_new)
                p_sc[r0:r0 + ATTN_ROWS, cols] = p.astype(BF16)
            m_parts.append(m_new)
        m_new = jnp.concatenate(m_parts, axis=1)
        a_sc[...] = jnp.exp2(m_all - m_new)
        m_sc[...] = m_new

    def values(chain, j, masked=False):
        p_sc, a_sc, acc_sc, vh = chain[4], chain[6], chain[7], chain[9]
        nk = n_keys(chain, masked)
        pv = jnp.dot(v_ref[vh, j, :, 0:nk], p_sc[0:nk, :], preferred_element_type=F32)
        acc_sc[...] = a_sc[...] * acc_sc[...] + pv

    def idle_late():
        for chain in late:
            chain[4][...] = jnp.zeros(chain[4].shape, BF16)
            chain[6][...] = jnp.ones(chain[6].shape, F32)

    def consume(j, cur_masked=False, nxt=None, nxt_masked=False, final=False, before_next=None):
        def open_late(chain):
            scores(chain, j, cur_masked)
            values(chain, jnp.maximum(j - 1, 0))

        open_late(late[0])
        for i, chain in enumerate(early):
            softmax(chain, cur_masked)
            if i == 0:
                for other in late[1:]:
                    open_late(other)
                if before_next is not None:
                    before_next()
            if nxt is not None:
                scores(chain, nxt, nxt_masked)
            values(chain, j, cur_masked)
        for chain in late:
            softmax(chain, cur_masked)
        if final:
            for chain in late:
                values(chain, j, cur_masked)

    def finalize(qi):
        g = g_ref[...]
        rows = pl.ds(pl.multiple_of(qi * TQ, TQ), TQ)
        for pair in range(ATTN_MAPS // 2):
            ot = [jnp.concatenate([chains[mi * n_half + h][7][0:feat] / chains[mi * n_half + h][7][feat:feat + 1]
                                   for h in range(n_half)], axis=1)
                  for mi in (2 * pair, 2 * pair + 1)]
            cols = slice(pair * LANES, (pair + 1) * LANES)
            if diff:
                lv = lam_ref[...]
                lam = (jnp.exp(jnp.sum(lv[0:1] * lv[1:2], axis=1, keepdims=True))
                       - jnp.exp(jnp.sum(lv[2:3] * lv[3:4], axis=1, keepdims=True)) + lambda_init)
                o = (ot[0] - lam * ot[1]).T
                y = o * lax.rsqrt(jnp.mean(o * o, axis=1, keepdims=True) + EPS) * g
                o_ref[rows, cols] = (y * (1.0 - lambda_init)).astype(o_ref.dtype)
            else:
                o = jnp.concatenate(ot, axis=0).T
                low = _lane_iota((TQ, LANES)) < HEAD_DIM
                sq = o * o
                msa = jnp.sum(jnp.where(low, sq, 0.0), axis=1, keepdims=True) / HEAD_DIM
                msb = jnp.sum(jnp.where(low, 0.0, sq), axis=1, keepdims=True) / HEAD_DIM
                inv = jnp.where(low, lax.rsqrt(msa + EPS), lax.rsqrt(msb + EPS))
                o_ref[rows, cols] = (o * inv * g).astype(o_ref.dtype)

    load_queries(0)
    reset_state()
    idle_late()
    for chain in early:
        scores(chain, 0, True)

    @pl.loop(0, nq)
    def _(qi):
        n_plain = jnp.maximum(qi - 1, 0)

        def run(first, count):
            for i in range(count):
                consume(first + i, nxt=first + i + 1)

        @pl.loop(0, n_plain // 4)
        def _(t):
            run(4 * t, 4)

        done = (n_plain // 4) * 4

        @pl.when(n_plain - done >= 2)
        def _():
            run(done, 2)

        def last_blocks(to_next_tile):
            if to_next_tile:
                consume(qi, cur_masked=True, nxt=0, final=True, before_next=lambda: load_queries(qi + 1))
            else:
                consume(qi, cur_masked=True, final=True)
            finalize(qi)

        for to_next_tile in (True, False):
            more = (qi < nq - 1) if to_next_tile else (qi == nq - 1)

            last_tile_odd = (nq - 2) % 2 == 1
            for odd in ((True, False) if to_next_tile else (last_tile_odd,)):
                @pl.when(more & (qi > 0) & ((n_plain % 2 == 1) == odd))
                def _():
                    if odd:
                        run(qi - 2, 1)
                    consume(qi - 1, nxt=qi, nxt_masked=True)
                    last_blocks(to_next_tile)

            if to_next_tile:
                @pl.when(qi == 0)
                def _():
                    last_blocks(to_next_tile)

        reset_state()
        idle_late()


def _attention(diff, lambda_init, q, k, v, g, lamv):
    nq = SEQ // TQ
    n_steps = 8 // ATTN_MAPS
    v_heads = v.shape[1] // n_steps
    kspec = lambda m: pl.BlockSpec((SEQ, LANES), lambda b, p: (b, ATTN_MAPS * p + m))
    maps = [kspec(m) for m in range(ATTN_MAPS)]
    return pl.pallas_call(
        functools.partial(_attn_kernel, diff, lambda_init),
        grid=(BATCH, n_steps),
        in_specs=maps + maps + [
            pl.BlockSpec((None, v_heads, nq, v.shape[3], TQ), lambda b, p: (b, p, 0, 0, 0)),
            pl.BlockSpec((1, LANES), lambda b, p: (0, 0)),
            pl.BlockSpec((8, LANES), lambda b, p: (0, 0))],
        out_specs=pl.BlockSpec((SEQ, ATTN_MAPS // 2 * LANES), lambda b, p: (b, p)),
        out_shape=jax.ShapeDtypeStruct((N_TOK, DIFF_WIDTH), BF16),
        scratch_shapes=[pltpu.VMEM(shape, dt)
                        for shape, dt in (((LANES, ATTN_TQ), BF16), ((TQ, ATTN_TQ), F32),
                                          ((TQ, ATTN_TQ), BF16), ((1, ATTN_TQ), F32),
                                          ((1, ATTN_TQ), F32), ((v.shape[3], ATTN_TQ), F32))
                        for _ in range(ATTN_MAPS * TQ // ATTN_TQ)],
        compiler_params=_params("arbitrary", "arbitrary"),
        name="diff_attention" if diff else "fox_attention",
    )(*([q] * ATTN_MAPS), *([k] * ATTN_MAPS), v, g, lamv)


def _outproj_kernel(x_ref, od_ref, of_ref, gt_ref, sc_ref, sh_ref, g_ref, wo_ref, wr_ref, br_ref,
                    x1_ref, h2_ref, rt_ref, rtt_ref, cnt_ref, carry_ref):
    @pl.when(pl.program_id(0) == 0)
    def _():
        carry_ref[...] = jnp.zeros_like(carry_ref)

    tiles = [slice(t * TM, (t + 1) * TM) for t in range(OUTPROJ_TILES)]
    lane = _lane_iota((TM, LANES))
    lanef = lane.astype(F32)
    big = float(LANES)

    def project(rows):
        mix = jnp.dot(od_ref[rows, :], wo_ref[0:DIFF_WIDTH, :], preferred_element_type=F32)
        return mix + jnp.dot(of_ref[rows, :], wo_ref[DIFF_WIDTH:, :], preferred_element_type=F32)

    def normalise(rows, mix):
        x1 = x_ref[rows, :] + gt_ref[...] * mix
        x1_ref[rows, :] = x1
        h = _rms_mod(x1, g_ref[...], sc_ref[...], sh_ref[...])
        hh = h.astype(BF16)
        _pack_planes(h, h2_ref, rows)
        return hh, (h - hh.astype(F32)).astype(BF16)

    def router_logits(hh, hl):
        r1 = jnp.dot(hh, wr_ref[...], preferred_element_type=F32)
        r2 = jnp.dot(hl, wr_ref[:, 0:LANES], preferred_element_type=F32)
        return r1[:, 0:LANES] + r1[:, LANES:] + r2 + br_ref[...]

    def top_k(logits):
        isg = lane < N_GROUPS
        lg = jnp.where(isg, logits, NEG)
        mg = jnp.max(lg, axis=1, keepdims=True)
        sg = jnp.sum(jnp.where(isg, jnp.exp(lg - mg), 0.0), axis=1, keepdims=True)
        p_g = 1.0 / sg
        gsel = jnp.min(jnp.where(isg & (lg == mg), lanef, big), axis=1, keepdims=True)
        lo = N_GROUPS + gsel * EXPERTS_PER_GROUP
        ise = (lanef >= lo) & (lanef < lo + EXPERTS_PER_GROUP)
        le = jnp.where(ise, logits, NEG)
        t1 = jnp.max(le, axis=1, keepdims=True)
        i1 = jnp.min(jnp.where(ise & (le == t1), lanef, big), axis=1, keepdims=True)
        ise2 = ise & (lanef != i1)
        le2 = jnp.where(ise2, logits, NEG)
        t2 = jnp.max(le2, axis=1, keepdims=True)
        i2 = jnp.min(jnp.where(ise2 & (le2 == t2), lanef, big), axis=1, keepdims=True)
        d = jnp.exp(t2 - t1)
        return i1 - N_GROUPS, i2 - N_GROUPS, p_g / (1.0 + d), p_g * d / (1.0 + d)

    def earlier_in_tile(e1, e2):
        both = jnp.where((lanef == e1) | (lanef == e2), 1.0, 0.0)
        row = lax.broadcasted_iota(jnp.int32, (TM, TM), 0)
        col = lax.broadcasted_iota(jnp.int32, (TM, TM), 1)
        before = jnp.dot((row > col).astype(BF16), both.astype(BF16), preferred_element_type=F32)
        return before, jnp.sum(both, axis=0, keepdims=True)

    mixes = [project(rows) for rows in tiles]
    splits = [normalise(rows, mix) for rows, mix in zip(tiles, mixes)]
    logits = [router_logits(hh, hl) for hh, hl in splits]
    picks = [top_k(lg) for lg in logits]
    befores = [earlier_in_tile(e1, e2) for e1, e2, _, _ in picks]
    counts = carry_ref[0:1, :]
    for rows, (e1, e2, w1, w2), (before, added) in zip(tiles, picks, befores):
        before = before + counts
        rank1 = jnp.sum(jnp.where(lanef == e1, before, 0.0), axis=1, keepdims=True)
        rank2 = jnp.sum(jnp.where(lanef == e2, before, 0.0), axis=1, keepdims=True)
        out = jnp.zeros((TM, LANES), F32)
        for j, v in enumerate((e1, e2, w1, w2, rank1, rank2)):
            out = jnp.where(lane == j, v, out)
        rt_ref[rows, :] = out
        rtt_ref[:, rows] = out.T[0:8, :]
        counts = counts + added
    carry_ref[...] = jnp.broadcast_to(counts, carry_ref.shape)
    cnt_ref[...] = jnp.broadcast_to(counts, cnt_ref.shape)


def _outproj(x, od, of, gt, sc, sh, g, wo_bf, wr, br):
    tm = OUTPROJ_TILES * TM
    tpb = SEQ // tm
    row = pl.BlockSpec((tm, D_MODEL), lambda i: (i, 0))
    half = pl.BlockSpec((tm, DIFF_WIDTH), lambda i: (i, 0))
    per_batch = pl.BlockSpec((None, 1, D_MODEL), lambda i: (i // tpb, 0, 0))
    const = lambda shape: pl.BlockSpec(shape, lambda i: (0,) * len(shape))
    return pl.pallas_call(
        _outproj_kernel,
        grid=(N_TOK // tm,),
        in_specs=[row, half, half, per_batch, per_batch, per_batch, const((1, D_MODEL)),
                  const((D_MODEL, D_MODEL)), const((D_MODEL, 2 * LANES)), const((1, LANES))],
        out_specs=[row, pl.BlockSpec((N_PLANES, tm, LANES), lambda i: (0, i, 0)),
                   pl.BlockSpec((tm, LANES), lambda i: (i, 0)), pl.BlockSpec((8, tm), lambda i: (0, i)),
                   const((8, LANES))],
        out_shape=[jax.ShapeDtypeStruct((N_TOK, D_MODEL), F32),
                   jax.ShapeDtypeStruct((N_PLANES, N_TOK, LANES), jnp.int32),
                   jax.ShapeDtypeStruct((N_TOK, LANES), F32),
                   jax.ShapeDtypeStruct((8, N_TOK), F32),
                   jax.ShapeDtypeStruct((8, LANES), F32)],
        scratch_shapes=[pltpu.VMEM((8, LANES), F32)],
        compiler_params=_params("arbitrary"),
        name="outproj_router",
    )(x, od, of, gt, sc, sh, g.reshape(1, D_MODEL), wo_bf, wr, br)


def _expert_kernel(layer, be_ref, cnt_ref, first_ref, slot_ref, next_ref, last_ref, xs_ref, wg_hbm, wu_hbm,
                   wd_hbm, ys_ref, wg_sc, wu_sc, wd_sc, wg_f32, wu_f32, wd_f32, sem):
    i = pl.program_id(0)
    cnt = cnt_ref[i]

    def weight_copies(expert, slot):
        return [pltpu.make_async_copy(hbm.at[layer, expert], buf.at[slot], sem.at[slot, n])
                for n, (hbm, buf) in enumerate(((wg_hbm, wg_f32), (wu_hbm, wu_f32), (wd_hbm, wd_f32)))]

    @pl.when(first_ref[i] == 1)
    def _():
        slot = slot_ref[i]

        @pl.when(i == 0)
        def _():
            for copy in weight_copies(be_ref[0], 0):
                copy.start()

        for copy in weight_copies(be_ref[i], slot):
            copy.wait()
        wg_sc[...] = wg_f32[slot].astype(BF16)
        wu_sc[...] = wu_f32[slot].astype(BF16)
        wd_sc[...] = wd_f32[slot].astype(BF16)

        @pl.when(next_ref[i] >= 0)
        def _():
            for copy in weight_copies(next_ref[i], 1 - slot):
                copy.start()

    def mlp(n_rows):
        rows = slice(0, n_rows)
        live = lax.broadcasted_iota(jnp.int32, (n_rows, LANES), 0) < cnt
        xb = _unpack_planes([jnp.where(live, xs_ref[p, rows, :], 0) for p in range(N_PLANES)]).astype(BF16)
        a = jnp.dot(xb, wg_sc[...], preferred_element_type=F32)
        u = jnp.dot(xb, wu_sc[...], preferred_element_type=F32)
        hid = (a / (1.0 + jnp.exp(-a)) * u).astype(BF16)
        _pack_planes(jnp.dot(hid, wd_sc[...], preferred_element_type=F32), ys_ref, rows)
        if n_rows < MOE_BLOCK:
            ys_ref[:, n_rows:, :] = jnp.zeros((N_PLANES, MOE_BLOCK - n_rows, LANES), ys_ref.dtype)

    half = MOE_BLOCK // 2

    @pl.when(cnt > half)
    def _():
        mlp(MOE_BLOCK)

    @pl.when((cnt > 0) & (cnt <= half))
    def _():
        mlp(half)


def _experts(layer, block_expert, block_count, xs, wg, wu, wd):
    idx = jnp.arange(MOE_NBLOCKS, dtype=jnp.int32)
    first = jnp.concatenate([jnp.ones((1,), jnp.bool_), block_expert[1:] != block_expert[:-1]])
    slot = (jnp.cumsum(first.astype(jnp.int32)) - 1) % 2
    later_first = (idx[None, :] > idx[:, None]) & first[None, :]
    nxt = jnp.min(jnp.where(later_first, block_expert[None, :], N_EXPERTS), axis=1)
    nxt = jnp.where(nxt == N_EXPERTS, -1, nxt).astype(jnp.int32)

    last_used = jnp.maximum(jnp.sum((block_count > 0).astype(jnp.int32)) - 1, 0).reshape(1)
    planes = pl.BlockSpec((N_PLANES, MOE_BLOCK, LANES),
                          lambda i, be, bc, fi, sl, nx, lu: (0, jnp.minimum(i, lu[0]), 0))
    hbm = pl.BlockSpec(memory_space=pl.ANY)
    grid_spec = pltpu.PrefetchScalarGridSpec(
        num_scalar_prefetch=6,
        grid=(MOE_NBLOCKS,),
        in_specs=[planes, hbm, hbm, hbm],
        out_specs=planes,
        scratch_shapes=[pltpu.VMEM((D_MODEL, D_EXPERT), BF16), pltpu.VMEM((D_MODEL, D_EXPERT), BF16),
                        pltpu.VMEM((D_EXPERT, D_MODEL), BF16),
                        pltpu.VMEM((2, D_MODEL, D_EXPERT), F32), pltpu.VMEM((2, D_MODEL, D_EXPERT), F32),
                        pltpu.VMEM((2, D_EXPERT, D_MODEL), F32), pltpu.SemaphoreType.DMA((2, 3))],
    )
    return pl.pallas_call(
        functools.partial(_expert_kernel, layer),
        grid_spec=grid_spec,
        out_shape=jax.ShapeDtypeStruct((N_PLANES, PLANE_ROWS, LANES), jnp.int32),
        compiler_params=_params("arbitrary"),
        name="expert_mlp",
    )(block_expert, block_count, first.astype(jnp.int32), slot.astype(jnp.int32), nxt, last_used,
      xs, wg, wu, wd)


def _slots(route_t, counts):
    counts = counts[0, :N_EXPERTS].astype(jnp.int32)
    padded = ((counts + MOE_BLOCK - 1) // MOE_BLOCK) * MOE_BLOCK
    pend = jnp.cumsum(padded)
    pstart = pend - padded
    bstart = jnp.arange(MOE_NBLOCKS, dtype=jnp.int32) * MOE_BLOCK
    block_expert = jnp.minimum(jnp.sum(bstart[:, None] >= pend[None, :], axis=1), N_EXPERTS - 1)
    block_expert = block_expert.astype(jnp.int32)
    mine = block_expert[:, None] == jnp.arange(N_EXPERTS, dtype=jnp.int32)[None, :]
    left = jnp.sum(jnp.where(mine, counts + pstart, 0), axis=1) - bstart
    block_count = jnp.clip(left, 0, MOE_BLOCK).astype(jnp.int32)
    return _slot_rows(route_t, pstart.astype(jnp.int32)), block_expert, block_count


def _slot_rows_kernel(base_ref, rt_ref, o_ref):
    n_tok = rt_ref.shape[1]
    for k in range(TOP_K):
        e = rt_ref[k:k + 1, :]
        b = jnp.zeros((1, n_tok), F32)
        for j in range(N_EXPERTS):
            b = jnp.where(e == float(j), base_ref[j].astype(F32), b)
        dest = (b + rt_ref[2 * TOP_K + k:2 * TOP_K + k + 1, :]).astype(jnp.int32)
        for p in range(N_PLANES):
            for c in range(n_tok // LANES):
                o_ref[k * N_PLANES + p, c:c + 1, :] = dest[:, c * LANES:(c + 1) * LANES] + p * PLANE_ROWS


def _slot_rows(route_t, base):
    tm = 16 * LANES
    grid_spec = pltpu.PrefetchScalarGridSpec(
        num_scalar_prefetch=1,
        grid=(N_TOK // tm,),
        in_specs=[pl.BlockSpec((8, tm), lambda i, base: (0, i))],
        out_specs=pl.BlockSpec((TOP_K * N_PLANES, tm // LANES, LANES), lambda i, base: (0, i, 0)),
    )
    return pl.pallas_call(
        _slot_rows_kernel,
        grid_spec=grid_spec,
        out_shape=jax.ShapeDtypeStruct((TOP_K * N_PLANES, N_TOK // LANES, LANES), jnp.int32),
        compiler_params=_params("arbitrary"),
        name="slot_rows",
    )(base, route_t)


def _sc_workers():
    info = plsc.get_sparse_core_info()
    return info.num_cores, info.num_cores * info.num_subcores


def _sc_scatter2(src, idx, out_rows):
    n_win = src.shape[0] // SC_WINDOW
    nc, nw = _sc_workers()
    steps = n_win // nw
    mesh = plsc.VectorSubcoreMesh(core_axis_name="c", subcore_axis_name="s")

    @functools.partial(
        pl.kernel, mesh=mesh,
        out_type=jax.ShapeDtypeStruct((out_rows, LANES), src.dtype),
        scratch_types=[pltpu.VMEM((2 * steps, SC_WINDOW), jnp.int32),
                       pltpu.VMEM((SC_INFLIGHT, SC_WINDOW, LANES), src.dtype),
                       pltpu.SemaphoreType.DMA((SC_INFLIGHT,)), pltpu.SemaphoreType.DMA((SC_INFLIGHT,))],
        name="sc_dispatch_scatter",
    )
    def k(src_hbm, idx_hbm, out_hbm, idx_v, rows_v, lsem, wsem):
        first = (lax.axis_index("s") * nc + lax.axis_index("c")) * steps
        pltpu.sync_copy(idx_hbm.at[pl.ds(first, steps)], idx_v.at[pl.ds(0, steps)])
        pltpu.sync_copy(idx_hbm.at[pl.ds(n_win + first, steps)], idx_v.at[pl.ds(steps, steps)])

        @pl.loop(0, steps, step=SC_INFLIGHT)
        def _(j):
            loads = [pltpu.async_copy(src_hbm.at[pl.ds((first + j + b) * SC_WINDOW, SC_WINDOW)],
                                      rows_v.at[b], lsem.at[b]) for b in range(SC_INFLIGHT)]
            writes = []
            for b in range(SC_INFLIGHT):
                loads[b].wait()
                for half in range(TOP_K):
                    dst = out_hbm.at[idx_v.at[half * steps + j + b]]
                    writes.append(pltpu.async_copy(rows_v.at[b], dst, wsem.at[b]))
            for w in writes:
                w.wait()

    return k(src, idx)


def _sc_gather(table, idx):
    n_out = idx.shape[0] * SC_WINDOW
    nc, nw = _sc_workers()
    steps = n_out // nw // SC_WINDOW
    group = SC_INFLIGHT // 2
    mesh = plsc.VectorSubcoreMesh(core_axis_name="c", subcore_axis_name="s")

    @functools.partial(
        pl.kernel, mesh=mesh,
        out_type=jax.ShapeDtypeStruct((n_out, LANES), table.dtype),
        scratch_types=[pltpu.VMEM((steps, SC_WINDOW), jnp.int32),
                       pltpu.VMEM((2, group, SC_WINDOW, LANES), table.dtype),
                       pltpu.SemaphoreType.DMA((2, group)), pltpu.SemaphoreType.DMA((2, group))],
        name="sc_combine_gather",
    )
    def k(table_hbm, idx_hbm, out_hbm, idx_v, rows_v, gsem, wsem):
        first = (lax.axis_index("s") * nc + lax.axis_index("c")) * steps
        pltpu.sync_copy(idx_hbm.at[pl.ds(first, steps)], idx_v)

        def gather(w, half, b):
            return pltpu.make_async_copy(table_hbm.at[idx_v.at[w]], rows_v.at[half, b], gsem.at[half, b])

        def write_back(w, half, b):
            dst = out_hbm.at[pl.ds((first + w) * SC_WINDOW, SC_WINDOW)]
            return pltpu.make_async_copy(rows_v.at[half, b], dst, wsem.at[half, b])

        def start_gathers(w0, half):
            for b in range(group):
                gather(w0 + b, half, b).start()

        def drain(w0, half):
            for b in range(group):
                gather(w0 + b, half, b).wait()
                write_back(w0 + b, half, b).start()
            for b in range(group):
                write_back(w0 + b, half, b).wait()

        start_gathers(0, 0)

        @pl.loop(0, steps, step=2 * group)
        def _(w0):
            start_gathers(w0 + group, 1)
            drain(w0, 0)

            @pl.when(w0 + 2 * group < steps)
            def _():
                start_gathers(w0 + 2 * group, 0)

            drain(w0 + group, 1)

    return k(table, idx)


def _final_kernel(x_ref, rt_ref, y_ref, gt_ref, g_ref, o_ref):
    x = x_ref[...] + gt_ref[...] * _combine(rt_ref, y_ref)
    ms = jnp.mean(x * x, axis=-1, keepdims=True)
    o_ref[...] = x * lax.rsqrt(ms + EPS) * g_ref[...]


def _final(x, moe, gate, g):
    tpb = SEQ // TM
    row = pl.BlockSpec((TM, D_MODEL), lambda i: (i, 0))
    return pl.pallas_call(
        _final_kernel,
        grid=(N_TOK // TM,),
        in_specs=[row, pl.BlockSpec((TM, LANES), lambda i: (i, 0)),
                  pl.BlockSpec((TOP_K, N_PLANES, TM, LANES), lambda i: (0, 0, i, 0)),
                  pl.BlockSpec((None, 1, D_MODEL), lambda i: (i // tpb, 0, 0)),
                  pl.BlockSpec((1, D_MODEL), lambda i: (0, 0))],
        out_specs=row,
        out_shape=jax.ShapeDtypeStruct((N_TOK, D_MODEL), F32),
        compiler_params=_params("arbitrary"),
        name="final_norm",
    )(x, moe[0], moe[1], gate, g.reshape(1, D_MODEL))


def kernel(x, c, positions, w_ada, b_ada, g_mix, w_in, b_forget, lambda_q1, lambda_k1, lambda_q2,
           lambda_k2, g_subln, g_fox_out, w_out, g_ffn, w_router_group, b_router_group,
           w_router_expert, b_router_expert, w_expert_gate, w_expert_up, w_expert_down, g_final):
    mod = _modulation(c, w_ada, b_ada)
    mod = mod.reshape(DEPTH, BATCH, 6, 1, D_MODEL)
    tables = _rope_tables(positions)
    pq = _forget_placement()
    w_in_bf = jnp.pad(w_in.astype(BF16), ((0, 0), (0, 0), (0, IN_COLS_PAD - IN_COLS)))
    xf = x.reshape(N_TOK, D_MODEL)
    moe = None
    gate = None
    for l in range(DEPTH):
        sh1, sc1, gt1, sh2, sc2, gt2 = (mod[l, :, j] for j in range(6))
        w_bf = (l, w_in_bf)
        bfp =jnp.pad(b_forget[l], (0, LANES - N_FOX_HEADS)).reshape(1, LANES)
        xf, (dq, dk, dv, fq, fk, fv) = _inproj(xf, moe, gate, sc1, sh1, g_mix[l], w_bf, bfp, tables, pq)

        lambda_init = 0.8 - 0.6 * float(np.exp(-0.3 * l))
        lamv = jnp.zeros((8, LANES), F32).at[0:4, 0:HEAD_DIM].set(
            jnp.stack([lambda_q1[l], lambda_k1[l], lambda_q2[l], lambda_k2[l]]))
        g_d = g_subln[l].reshape(1, LANES)
        g_f = jnp.concatenate([g_fox_out[l], g_fox_out[l]]).reshape(1, LANES)
        od = _attention(True, lambda_init, dq, dk, dv, g_d, lamv)
        of = _attention(False, lambda_init, fq, fk, fv, g_f, lamv)

        wr32 = jnp.pad(jnp.concatenate([w_router_group[l], w_router_expert[l]], axis=1),
                       ((0, 0), (0, LANES - N_GROUPS - N_EXPERTS)))
        wr_hi = wr32.astype(BF16)
        wr_lo = (wr32 - wr_hi.astype(F32)).astype(BF16)
        wr = jnp.concatenate([wr_hi, wr_lo], axis=1)
        br = jnp.pad(jnp.concatenate([b_router_group[l], b_router_expert[l]]),
                     (0, LANES - N_GROUPS - N_EXPERTS)).reshape(1, LANES)
        xf, h2, route, route_t, counts = _outproj(xf, od, of, gt1, sc2, sh2, g_ffn[l],
                                                  w_out[l].astype(BF16), wr, br)

        rows, block_expert, block_count = _slots(route_t, counts)
        rows = rows.reshape(TOP_K * N_PLANES * N_TOK // SC_WINDOW, SC_WINDOW)
        xs = _sc_scatter2(h2.reshape(N_PLANES * N_TOK, LANES), rows, N_PLANES * PLANE_ROWS)
        ys = _experts(l, block_expert, block_count, xs.reshape(N_PLANES, PLANE_ROWS, LANES),
                      w_expert_gate, w_expert_up, w_expert_down)
        y2 = _sc_gather(ys.reshape(N_PLANES * PLANE_ROWS, LANES), rows)
        moe = (route, y2.reshape(TOP_K, N_PLANES, N_TOK, LANES))
        gate = gt2
    out = _final(xf, moe, gate, g_final)
    return out.reshape(BATCH, SEQ, D_MODEL)
```

```python
import functools

import numpy as np
import jax
import jax.numpy as jnp
from jax import lax
from jax.experimental import pallas as pl
from jax.experimental.pallas import tpu as pltpu
from jax.experimental.pallas import tpu_sc as plsc

D_MODEL = 1024
BATCH = 4
SEQ = 4096
DEPTH = 4
N_TOK = BATCH * SEQ

CHUNK = 64
HEAD_DIM = 64
N_DIFF_HEADS = 4
N_FOX_HEADS = 8
DIFF_WIDTH = 512
FOX_WIDTH = 512
IN_COLS = 3 * DIFF_WIDTH + 3 * FOX_WIDTH + N_FOX_HEADS
ROT_DIM = 16
ROPE_THETA = 500000.0
N_GROUPS = 4
EXPERTS_PER_GROUP = 8
N_EXPERTS = 32
TOP_K = 2
D_EXPERT = 512
EPS = 1e-6

LANES = 128
IN_COLS_PAD = 3200
FF_COL = 3 * DIFF_WIDTH + 3 * FOX_WIDTH
QK_WIDTH = 8 * LANES
TM = 512
OUTPROJ_TILES = 2
TQ = 512
ATTN_MAPS = 4
ATTN_TQ = 256
N_LATE_CHAINS = 1
ONES_ROWS = 16
ATTN_ROWS = 128
LOG2E = 1.4426950408889634
MOE_BLOCK = 512
MOE_ROWS = N_TOK * TOP_K + N_EXPERTS * MOE_BLOCK
MOE_NBLOCKS = MOE_ROWS // MOE_BLOCK
PLANE_ROWS = MOE_ROWS
N_PLANES = D_MODEL // 2 // LANES
SC_WINDOW = 128
SC_INFLIGHT = 4
NEG = -1e30
VMEM_LIMIT = 56 * 1024 * 1024

F32 = jnp.float32
BF16 = jnp.bfloat16


def _bf16_round(x):
    return x.astype(BF16).astype(F32)


def _lane_iota(shape):
    return lax.broadcasted_iota(jnp.int32, shape, 1)


def _params(*sem):
    return pltpu.CompilerParams(dimension_semantics=sem, vmem_limit_bytes=VMEM_LIMIT)


def _pack_planes(y, o_ref, rows=slice(None)):
    bits = lax.bitcast_convert_type(_bf16_round(y), jnp.uint32)
    half = D_MODEL // 2
    word = bits[:, half:] | lax.shift_right_logical(bits[:, :half], jnp.uint32(16))
    word = lax.bitcast_convert_type(word, jnp.int32)
    for p in range(N_PLANES):
        o_ref[p, rows, :] = word[:, p * LANES:(p + 1) * LANES]


def _unpack_planes(planes):
    lo, hi = [], []
    for w in planes:
        u = lax.bitcast_convert_type(w, jnp.uint32)
        lo.append(lax.bitcast_convert_type(lax.shift_left(u, jnp.uint32(16)), F32))
        hi.append(lax.bitcast_convert_type(u & jnp.uint32(0xFFFF0000), F32))
    return jnp.concatenate(lo + hi, axis=1)


def _combine(route_ref, y_ref):
    rt = route_ref[...]
    y0 = _unpack_planes([y_ref[0, p] for p in range(N_PLANES)])
    y1 = _unpack_planes([y_ref[1, p] for p in range(N_PLANES)])
    return rt[:, 2:3] * y0 + rt[:, 3:4] * y1


def _mod_kernel(ct_ref, w_ref, b_ref, o_ref):
    ct = ct_ref[...]
    cond = ct / (1.0 + jnp.exp(-ct))
    tn = w_ref.shape[1]
    w = w_ref[...].reshape(D_MODEL // 8, 8, tn)
    rows = []
    for b in range(BATCH):
        col = cond[:, b:b + 1].reshape(D_MODEL // 8, 8, 1)
        rows.append(jnp.sum(jnp.sum(w * col, axis=0), axis=0, keepdims=True))
    rows.append(jnp.zeros((o_ref.shape[0] - BATCH, tn), F32))
    o_ref[...] = jnp.concatenate(rows, axis=0) + b_ref[...]


def _modulation(c, w_ada, b_ada):
    rows = 8
    tn = 1536
    c_pad = jnp.zeros((D_MODEL, LANES), F32).at[:, :BATCH].set(c.T)
    out = pl.pallas_call(
        _mod_kernel,
        grid=(DEPTH, 6 * D_MODEL // tn),
        in_specs=[
            pl.BlockSpec((D_MODEL, LANES), lambda l, n: (0, 0)),
            pl.BlockSpec((None, D_MODEL, tn), lambda l, n: (l, 0, n)),
            pl.BlockSpec((None, 1, tn), lambda l, n: (l, 0, n)),
        ],
        out_specs=pl.BlockSpec((None, rows, tn), lambda l, n: (l, 0, n)),
        out_shape=jax.ShapeDtypeStruct((DEPTH, rows, 6 * D_MODEL), F32),
        compiler_params=_params("arbitrary", "arbitrary"),
        name="adaln_mod",
    )(c_pad, w_ada, b_ada.reshape(DEPTH, 1, 6 * D_MODEL))
    return out[:, :BATCH]


def _rope_kernel(pos_ref, inv_ref, c_ref, sa_ref, sb_ref):
    ang = pos_ref[...].astype(F32) * inv_ref[...]
    j = _lane_iota(ang.shape) % HEAD_DIM
    cosv = jnp.cos(ang)
    sinv = jnp.sin(ang)
    half = ROT_DIM // 2
    c_ref[...] = jnp.where(j < ROT_DIM, cosv, 1.0)
    sa_ref[...] = jnp.where(j < half, -sinv, 0.0)
    sb_ref[...] = jnp.where((j >= half) & (j < ROT_DIM), sinv, 0.0)


def _rope_tables(positions):
    half = ROT_DIM // 2
    inv = ROPE_THETA ** (-jnp.arange(0, ROT_DIM, 2, dtype=F32) / ROT_DIM)
    lane = np.arange(LANES)
    inv_lane = inv[(lane % HEAD_DIM) % half].reshape(1, LANES)
    spec = pl.BlockSpec((TM, LANES), lambda i: (i, 0))
    shape = jax.ShapeDtypeStruct((N_TOK, LANES), F32)
    return pl.pallas_call(
        _rope_kernel,
        grid=(N_TOK // TM,),
        in_specs=[pl.BlockSpec((TM, 1), lambda i: (i, 0)),
                  pl.BlockSpec((1, LANES), lambda i: (0, 0))],
        out_specs=[spec, spec, spec],
        out_shape=[shape, shape, shape],
        compiler_params=_params("arbitrary"),
        name="rope_tables",
    )(positions.reshape(N_TOK, 1), inv_lane)


def _rms_mod(x, g, sc, sh):
    ms = jnp.mean(x * x, axis=-1, keepdims=True)
    return (x * lax.rsqrt(ms + EPS) * g) * (1.0 + sc) + sh


def _inproj_kernel(fuse, *refs):
    if fuse:
        (x_ref, rt_ref, y_ref, gt_ref, sc_ref, sh_ref, g_ref, w_ref, bf_ref, c_ref, sa_ref, sb_ref,
         pq_ref, xo_ref, dq_ref, dk_ref, dv_ref, fq_ref, fk_ref, fv_ref, carry_ref, h_even, h_odd) = refs
    else:
        (x_ref, sc_ref, sh_ref, g_ref, w_ref, bf_ref, c_ref, sa_ref, sb_ref,
         pq_ref, dq_ref, dk_ref, dv_ref, fq_ref, fk_ref, fv_ref, carry_ref, h_even, h_odd) = refs
    step = pl.program_id(0)

    def normalise(h_ref):
        if fuse:
            x = x_ref[...] + gt_ref[...] * _combine(rt_ref, y_ref)
            xo_ref[...] = x
        else:
            x = x_ref[...]
        h_ref[...] = _rms_mod(x, g_ref[...], sc_ref[...], sh_ref[...]).astype(BF16)

    @pl.when(step == 0)
    def _():
        h_odd[...] = jnp.zeros_like(h_odd)

    @pl.when((step == 0) | ((step - 1) % (SEQ // TM) == 0))
    def _():
        carry_ref[...] = jnp.zeros_like(carry_ref)

    for parity, (h_new, h_old) in enumerate(((h_even, h_odd), (h_odd, h_even))):
        @pl.when(step % 2 == parity)
        def _():
            _inproj_project(h_old[...], w_ref, bf_ref, c_ref, sa_ref, sb_ref, pq_ref, dq_ref, dk_ref,
                            dv_ref, fq_ref, fk_ref, fv_ref, carry_ref)
            normalise(h_new)


def _inproj_project(hb, w_ref, bf_ref, c_ref, sa_ref, sb_ref, pq_ref, dq_ref, dk_ref, dv_ref, fq_ref,
                    fk_ref, fv_ref, carry_ref):
    lane = _lane_iota((TM, LANES))
    nh = N_FOX_HEADS

    def pack3(a):
        hi = _bf16_round(a)
        r1 = a - hi
        mid = _bf16_round(r1)
        lo = _bf16_round(r1 - mid)
        return jnp.where(lane < nh, hi,
                         jnp.where(lane < 2 * nh, pltpu.roll(mid, nh, 1),
                                   jnp.where(lane < 3 * nh, pltpu.roll(lo, 2 * nh, 1), 0.0)))

    z = jnp.dot(hb, w_ref[:, FF_COL:FF_COL + LANES], preferred_element_type=F32) + bf_ref[...]

    low = lane < HEAD_DIM
    rc, rsa, rsb = c_ref[...], sa_ref[...], sb_ref[...]
    scale = HEAD_DIM ** -0.5 * LOG2E

    def split_store(chunk, o_ref, m, extra_a=None, extra_b=None):
        a = jnp.where(low, chunk, 0.0)
        b = jnp.where(low, pltpu.roll(chunk, HEAD_DIM, 1), 0.0)
        if extra_a is not None:
            a = a + extra_a
            b = b + extra_b
        o_ref[:, (2 * m) * LANES:(2 * m + 1) * LANES] = a.astype(BF16)
        o_ref[:, (2 * m + 1) * LANES:(2 * m + 2) * LANES] = b.astype(BF16)

    def rope(xc):
        return xc * rc + pltpu.roll(xc, LANES - ROT_DIM // 2, 1) * rsa + pltpu.roll(xc, ROT_DIM // 2, 1) * rsb

    pdq = jnp.dot(hb, w_ref[:, 0:DIFF_WIDTH], preferred_element_type=F32)
    for m in range(N_DIFF_HEADS):
        split_store(rope(pdq[:, m * LANES:(m + 1) * LANES]) * scale, dq_ref, m)
    pdk = jnp.dot(hb, w_ref[:, DIFF_WIDTH:2 * DIFF_WIDTH], preferred_element_type=F32)
    for m in range(N_DIFF_HEADS):
        split_store(rope(pdk[:, m * LANES:(m + 1) * LANES]), dk_ref, m)
    def store_values_t(pv, o_ref, width):
        ones = jnp.ones((ONES_ROWS, TM), BF16)
        for m in range(4):
            vt = pv[:, m * LANES:(m + 1) * LANES].T.astype(BF16)
            for i in range(LANES // width):
                o_ref[m * (LANES // width) + i, 0:width, :] = vt[i * width:(i + 1) * width]
                o_ref[m * (LANES // width) + i, width:width + ONES_ROWS, :] = ones

    store_values_t(jnp.dot(hb, w_ref[:, 2 * DIFF_WIDTH:3 * DIFF_WIDTH], preferred_element_type=F32),
                   dv_ref, 2 * HEAD_DIM)
    o = 3 * DIFF_WIDTH
    store_values_t(jnp.dot(hb, w_ref[:, o + 2 * FOX_WIDTH:o + 3 * FOX_WIDTH],
                           preferred_element_type=F32), fv_ref, HEAD_DIM)

    logf =jnp.minimum(z, 0.0) - jnp.log(1.0 + jnp.exp(-jnp.abs(z)))
    logf = jnp.where(lane < nh, logf, 0.0)
    row = lax.broadcasted_iota(jnp.int32, (TM, TM), 0)
    col = lax.broadcasted_iota(jnp.int32, (TM, TM), 1)
    tri = (row >= col).astype(BF16)
    r = jnp.dot(tri, pack3(logf).astype(BF16), preferred_element_type=F32)
    cs = r + pltpu.roll(r, LANES - nh, 1) + pltpu.roll(r, LANES - 2 * nh, 1)
    cf = jnp.where(lane < nh, cs + carry_ref[0:1, :], 0.0)
    carry_ref[...] = jnp.broadcast_to(cf[TM - 1:TM, :], carry_ref.shape)

    t3 = jnp.where(lane == 3 * nh, 1.0, pack3(cf * LOG2E)).astype(BF16)
    aug = jnp.dot(t3, pq_ref[...], preferred_element_type=F32)

    pfq =jnp.dot(hb, w_ref[:, o:o + FOX_WIDTH], preferred_element_type=F32)
    for m in range(N_FOX_HEADS // 2):
        split_store(pfq[:, m * LANES:(m + 1) * LANES] * scale, fq_ref, m,
                    aug[:, (2 * m) * LANES:(2 * m + 1) * LANES],
                    aug[:, (2 * m + 1) * LANES:(2 * m + 2) * LANES])
    pfk = jnp.dot(hb, w_ref[:, o + FOX_WIDTH:o + 2 * FOX_WIDTH], preferred_element_type=F32)
    for m in range(N_FOX_HEADS // 2):
        split_store(pfk[:, m * LANES:(m + 1) * LANES], fk_ref, m,
                    aug[:, QK_WIDTH + (2 * m) * LANES:QK_WIDTH + (2 * m + 1) * LANES],
                    aug[:, QK_WIDTH + (2 * m + 1) * LANES:QK_WIDTH + (2 * m + 2) * LANES])


def _forget_placement():
    nh = N_FOX_HEADS
    p = np.zeros((LANES, 2 * QK_WIDTH), np.float32)
    for h in range(nh):
        base_q = h * LANES + HEAD_DIM
        base_k = QK_WIDTH + h * LANES + HEAD_DIM
        for part in range(3):
            p[part * nh + h, base_q + part] = 1.0
            p[3 * nh, base_q + 3 + part] = 1.0
            p[3 * nh, base_k + part] = 1.0
            p[part * nh + h, base_k + 3 + part] = -1.0
    return jnp.asarray(p, BF16)


def _inproj(x, moe, gate, sc, sh, g, w_bf, b_forget, tables, pq):
    fuse = moe is not None
    tpb = SEQ // TM
    n_tiles = N_TOK // TM
    new = lambda s: jnp.minimum(s, n_tiles - 1)
    old = lambda s: jnp.maximum(s - 1, 0)
    row = pl.BlockSpec((TM, D_MODEL), lambda s: (new(s), 0))
    per_batch = pl.BlockSpec((None, 1, D_MODEL), lambda s: (new(s) // tpb, 0, 0))
    const = lambda shape: pl.BlockSpec(shape, lambda s: (0,) * len(shape))
    tab = pl.BlockSpec((TM, LANES), lambda s: (old(s), 0))
    in_specs = [row]
    args = [x]
    if fuse:
        in_specs += [pl.BlockSpec((TM, LANES), lambda s: (new(s), 0)),
                     pl.BlockSpec((TOP_K, N_PLANES, TM, LANES), lambda s: (0, 0, new(s), 0)), per_batch]
        args += [moe[0], moe[1], gate]
    layer, w_all = w_bf
    in_specs += [per_batch, per_batch, const((1, D_MODEL)),
                 pl.BlockSpec((None, D_MODEL, IN_COLS_PAD), lambda s: (layer, 0, 0)),
                 const((1, LANES)), tab, tab, tab, const((LANES, 2 * QK_WIDTH))]
    args += [sc, sh, g.reshape(1, D_MODEL), w_all, b_forget, *tables, pq]
    wide = pl.BlockSpec((TM, QK_WIDTH), lambda s: (old(s), 0))
    def vspec(heads, width):
        rows = width + ONES_ROWS
        return (pl.BlockSpec((None, heads, None, rows, TM),
                             lambda s: (old(s) // tpb, 0, old(s) % tpb, 0, 0)),
                jax.ShapeDtypeStruct((BATCH, heads, tpb, rows, TM), BF16))

    wide_s = jax.ShapeDtypeStruct((N_TOK, QK_WIDTH), BF16)
    dv_spec, dv_s = vspec(N_DIFF_HEADS, 2 * HEAD_DIM)
    fv_spec, fv_s = vspec(N_FOX_HEADS, HEAD_DIM)
    out_specs = [wide, wide, dv_spec, wide, wide, fv_spec]
    out_shape = [wide_s, wide_s, dv_s, wide_s, wide_s, fv_s]
    if fuse:
        out_specs = [row] + out_specs
        out_shape = [jax.ShapeDtypeStruct((N_TOK, D_MODEL), F32)] + out_shape
    outs = pl.pallas_call(
        functools.partial(_inproj_kernel, fuse),
        grid=(n_tiles + 1,),
        in_specs=in_specs,
        out_specs=out_specs,
        out_shape=out_shape,
        scratch_shapes=[pltpu.VMEM((8, LANES), F32), pltpu.VMEM((TM, D_MODEL), BF16),
                        pltpu.VMEM((TM, D_MODEL), BF16)],
        compiler_params=_params("arbitrary"),
        name="norm_inproj",
    )(*args)
    if fuse:
        return outs[0], outs[1:]
    return x, outs


def _attn_kernel(diff, lambda_init, *refs):
    q_refs, k_refs = refs[:ATTN_MAPS], refs[ATTN_MAPS:2 * ATTN_MAPS]
    v_ref, g_ref, lam_ref, o_ref = refs[2 * ATTN_MAPS:2 * ATTN_MAPS + 4]
    scratch = refs[2 * ATTN_MAPS + 4:]
    nq = SEQ // TQ
    n_half = TQ // ATTN_TQ
    feat = 2 * HEAD_DIM if diff else HEAD_DIM
    chains = []
    for mi, (q_ref, k_ref) in enumerate(zip(q_refs, k_refs)):
        for h in range(n_half):
            c = mi * n_half + h
            qt_sc, s_sc, p_sc, m_sc, a_sc, acc_sc = scratch[c::ATTN_MAPS * n_half]
            vh = mi // 2 if diff else mi
            chains.append((h, k_ref, qt_sc, s_sc, p_sc, m_sc, a_sc, acc_sc, q_ref, vh))
    order = [chains[mi * n_half + h] for h in range(n_half) for mi in range(ATTN_MAPS)]
    early, late = order[:-N_LATE_CHAINS], order[-N_LATE_CHAINS:]

    def load_queries(qi):
        for mi in range(ATTN_MAPS):
            q_ref = chains[mi * n_half][8]
            qt = q_ref[pl.ds(pl.multiple_of(qi * TQ, TQ), TQ), :].astype(F32).T.astype(BF16)
            for h in range(n_half):
                chains[mi * n_half + h][2][...] = qt[:, h * ATTN_TQ:(h + 1) * ATTN_TQ]

    def reset_state():
        for chain in chains:
            m_sc, _, acc_sc = chain[5:8]
            m_sc[...] = jnp.full(m_sc.shape, NEG, F32)
            acc_sc[...] = jnp.zeros(acc_sc.shape, F32)

    def n_keys(chain, masked):
        return (chain[0] + 1) * ATTN_TQ if masked else TQ

    def scores(chain, j, masked):
        h, k_ref, qt_sc, s_sc = chain[:4]
        nk = n_keys(chain, masked)
        off = pl.multiple_of(j * TQ, TQ)
        s = jnp.dot(k_ref[pl.ds(off, nk), :], qt_sc[...], preferred_element_type=F32)
        if masked:
            kk = lax.broadcasted_iota(jnp.int32, (nk, ATTN_TQ), 0)
            qq = h * ATTN_TQ + lax.broadcasted_iota(jnp.int32, (nk, ATTN_TQ), 1)
            s = jnp.where((kk // CHUNK <= qq // CHUNK) if diff else (kk <= qq), s, NEG)
        s_sc[0:nk, :] = s

    def softmax(chain, masked):
        s_sc, p_sc, m_sc, a_sc = chain[3:7]
        nk = n_keys(chain, masked)
        m_all = m_sc[...]
        m_parts = []
        for c0 in range(0, ATTN_TQ, LANES):
            cols = slice(c0, c0 + LANES)
            pm = s_sc[0:ATTN_ROWS, cols]
            for r0 in range(ATTN_ROWS, nk, ATTN_ROWS):
                pm = jnp.maximum(pm, s_sc[r0:r0 + ATTN_ROWS, cols])
            m_new = jnp.maximum(m_all[:, cols], jnp.max(pm, axis=0, keepdims=True))
            for r0 in range(0, nk, ATTN_ROWS):
                p = jnp.exp2(s_sc[r0:r0 + ATTN_ROWS, cols] - m_new)
                p_sc[r0:r0 + ATTN_ROWS, cols] = p.astype(BF16)
            m_parts.append(m_new)
        m_new = jnp.concatenate(m_parts, axis=1)
        a_sc[...] = jnp.exp2(m_all - m_new)
        m_sc[...] = m_new

    def values(chain, j, masked=False):
        p_sc, a_sc, acc_sc, vh = chain[4], chain[6], chain[7], chain[9]
        nk = n_keys(chain, masked)
        pv = jnp.dot(v_ref[vh, j, :, 0:nk], p_sc[0:nk, :], preferred_element_type=F32)
        acc_sc[...] = a_sc[...] * acc_sc[...] + pv

    def idle_late():
        for chain in late:
            chain[4][...] = jnp.zeros(chain[4].shape, BF16)
            chain[6][...] = jnp.ones(chain[6].shape, F32)

    def consume(j, cur_masked=False, nxt=None, nxt_masked=False, final=False, before_next=None):
        def open_late(chain):
            scores(chain, j, cur_masked)
            values(chain, jnp.maximum(j - 1, 0))

        open_late(late[0])
        for i, chain in enumerate(early):
            softmax(chain, cur_masked)
            if i == 0:
                for other in late[1:]:
                    open_late(other)
                if before_next is not None:
                    before_next()
            if nxt is not None:
                scores(chain, nxt, nxt_masked)
            values(chain, j, cur_masked)
        for chain in late:
            softmax(chain, cur_masked)
        if final:
            for chain in late:
                values(chain, j, cur_masked)

    def finalize(qi):
        g = g_ref[...]
        rows = pl.ds(pl.multiple_of(qi * TQ, TQ), TQ)
        for pair in range(ATTN_MAPS // 2):
            ot = [jnp.concatenate([chains[mi * n_half + h][7][0:feat] / chains[mi * n_half + h][7][feat:feat + 1]
                                   for h in range(n_half)], axis=1)
                  for mi in (2 * pair, 2 * pair + 1)]
            cols = slice(pair * LANES, (pair + 1) * LANES)
            if diff:
                lv = lam_ref[...]
                lam = (jnp.exp(jnp.sum(lv[0:1] * lv[1:2], axis=1, keepdims=True))
                       - jnp.exp(jnp.sum(lv[2:3] * lv[3:4], axis=1, keepdims=True)) + lambda_init)
                o = (ot[0] - lam * ot[1]).T
                y = o * lax.rsqrt(jnp.mean(o * o, axis=1, keepdims=True) + EPS) * g
                o_ref[rows, cols] = (y * (1.0 - lambda_init)).astype(o_ref.dtype)
            else:
                o = jnp.concatenate(ot, axis=0).T
                low = _lane_iota((TQ, LANES)) < HEAD_DIM
                sq = o * o
                msa = jnp.sum(jnp.where(low, sq, 0.0), axis=1, keepdims=True) / HEAD_DIM
                msb = jnp.sum(jnp.where(low, 0.0, sq), axis=1, keepdims=True) / HEAD_DIM
                inv = jnp.where(low, lax.rsqrt(msa + EPS), lax.rsqrt(msb + EPS))
                o_ref[rows, cols] = (o * inv * g).astype(o_ref.dtype)

    load_queries(0)
    reset_state()
    idle_late()
    for chain in early:
        scores(chain, 0, True)

    @pl.loop(0, nq)
    def _(qi):
        n_plain = jnp.maximum(qi - 1, 0)

        def run(first, count):
            for i in range(count):
                consume(first + i, nxt=first + i + 1)

        @pl.loop(0, n_plain // 4)
        def _(t):
            run(4 * t, 4)

        done = (n_plain // 4) * 4

        @pl.when(n_plain - done >= 2)
        def _():
            run(done, 2)

        def last_blocks(to_next_tile):
            if to_next_tile:
                consume(qi, cur_masked=True, nxt=0, final=True, before_next=lambda: load_queries(qi + 1))
            else:
                consume(qi, cur_masked=True, final=True)
            finalize(qi)

        for to_next_tile in (True, False):
            more = (qi < nq - 1) if to_next_tile else (qi == nq - 1)

            last_tile_odd = (nq - 2) % 2 == 1
            for odd in ((True, False) if to_next_tile else (last_tile_odd,)):
                @pl.when(more & (qi > 0) & ((n_plain % 2 == 1) == odd))
                def _():
                    if odd:
                        run(qi - 2, 1)
                    consume(qi - 1, nxt=qi, nxt_masked=True)
                    last_blocks(to_next_tile)

            if to_next_tile:
                @pl.when(qi == 0)
                def _():
                    last_blocks(to_next_tile)

        reset_state()
        idle_late()


def _attention(diff, lambda_init, q, k, v, g, lamv):
    nq = SEQ // TQ
    n_steps = 8 // ATTN_MAPS
    v_heads = v.shape[1] // n_steps
    kspec = lambda m: pl.BlockSpec((SEQ, LANES), lambda b, p: (b, ATTN_MAPS * p + m))
    maps = [kspec(m) for m in range(ATTN_MAPS)]
    return pl.pallas_call(
        functools.partial(_attn_kernel, diff, lambda_init),
        grid=(BATCH, n_steps),
        in_specs=maps + maps + [
            pl.BlockSpec((None, v_heads, nq, v.shape[3], TQ), lambda b, p: (b, p, 0, 0, 0)),
            pl.BlockSpec((1, LANES), lambda b, p: (0, 0)),
            pl.BlockSpec((8, LANES), lambda b, p: (0, 0))],
        out_specs=pl.BlockSpec((SEQ, ATTN_MAPS // 2 * LANES), lambda b, p: (b, p)),
        out_shape=jax.ShapeDtypeStruct((N_TOK, DIFF_WIDTH), BF16),
        scratch_shapes=[pltpu.VMEM(shape, dt)
                        for shape, dt in (((LANES, ATTN_TQ), BF16), ((TQ, ATTN_TQ), F32),
                                          ((TQ, ATTN_TQ), BF16), ((1, ATTN_TQ), F32),
                                          ((1, ATTN_TQ), F32), ((v.shape[3], ATTN_TQ), F32))
                        for _ in range(ATTN_MAPS * TQ // ATTN_TQ)],
        compiler_params=_params("arbitrary", "arbitrary"),
        name="diff_attention" if diff else "fox_attention",
    )(*([q] * ATTN_MAPS), *([k] * ATTN_MAPS), v, g, lamv)


def _outproj_kernel(x_ref, od_ref, of_ref, gt_ref, sc_ref, sh_ref, g_ref, wo_ref, wr_ref, br_ref,
                    x1_ref, h2_ref, rt_ref, rtt_ref, cnt_ref, carry_ref):
    @pl.when(pl.program_id(0) == 0)
    def _():
        carry_ref[...] = jnp.zeros_like(carry_ref)

    tiles = [slice(t * TM, (t + 1) * TM) for t in range(OUTPROJ_TILES)]
    lane = _lane_iota((TM, LANES))
    lanef = lane.astype(F32)
    big = float(LANES)

    def project(rows):
        mix = jnp.dot(od_ref[rows, :], wo_ref[0:DIFF_WIDTH, :], preferred_element_type=F32)
        return mix + jnp.dot(of_ref[rows, :], wo_ref[DIFF_WIDTH:, :], preferred_element_type=F32)

    def normalise(rows, mix):
        x1 = x_ref[rows, :] + gt_ref[...] * mix
        x1_ref[rows, :] = x1
        h = _rms_mod(x1, g_ref[...], sc_ref[...], sh_ref[...])
        hh = h.astype(BF16)
        _pack_planes(h, h2_ref, rows)
        return hh, (h - hh.astype(F32)).astype(BF16)

    def router_logits(hh, hl):
        r1 = jnp.dot(hh, wr_ref[...], preferred_element_type=F32)
        r2 = jnp.dot(hl, wr_ref[:, 0:LANES], preferred_element_type=F32)
        return r1[:, 0:LANES] + r1[:, LANES:] + r2 + br_ref[...]

    def top_k(logits):
        isg = lane < N_GROUPS
        lg = jnp.where(isg, logits, NEG)
        mg = jnp.max(lg, axis=1, keepdims=True)
        sg = jnp.sum(jnp.where(isg, jnp.exp(lg - mg), 0.0), axis=1, keepdims=True)
        p_g = 1.0 / sg
        gsel = jnp.min(jnp.where(isg & (lg == mg), lanef, big), axis=1, keepdims=True)
        lo = N_GROUPS + gsel * EXPERTS_PER_GROUP
        ise = (lanef >= lo) & (lanef < lo + EXPERTS_PER_GROUP)
        le = jnp.where(ise, logits, NEG)
        t1 = jnp.max(le, axis=1, keepdims=True)
        i1 = jnp.min(jnp.where(ise & (le == t1), lanef, big), axis=1, keepdims=True)
        ise2 = ise & (lanef != i1)
        le2 = jnp.where(ise2, logits, NEG)
        t2 = jnp.max(le2, axis=1, keepdims=True)
        i2 = jnp.min(jnp.where(ise2 & (le2 == t2), lanef, big), axis=1, keepdims=True)
        d = jnp.exp(t2 - t1)
        return i1 - N_GROUPS, i2 - N_GROUPS, p_g / (1.0 + d), p_g * d / (1.0 + d)

    def earlier_in_tile(e1, e2):
        both = jnp.where((lanef == e1) | (lanef == e2), 1.0, 0.0)
        row = lax.broadcasted_iota(jnp.int32, (TM, TM), 0)
        col = lax.broadcasted_iota(jnp.int32, (TM, TM), 1)
        before = jnp.dot((row > col).astype(BF16), both.astype(BF16), preferred_element_type=F32)
        return before, jnp.sum(both, axis=0, keepdims=True)

    mixes = [project(rows) for rows in tiles]
    splits = [normalise(rows, mix) for rows, mix in zip(tiles, mixes)]
    logits = [router_logits(hh, hl) for hh, hl in splits]
    picks = [top_k(lg) for lg in logits]
    befores = [earlier_in_tile(e1, e2) for e1, e2, _, _ in picks]
    counts = carry_ref[0:1, :]
    for rows, (e1, e2, w1, w2), (before, added) in zip(tiles, picks, befores):
        before = before + counts
        rank1 = jnp.sum(jnp.where(lanef == e1, before, 0.0), axis=1, keepdims=True)
        rank2 = jnp.sum(jnp.where(lanef == e2, before, 0.0), axis=1, keepdims=True)
        out = jnp.zeros((TM, LANES), F32)
        for j, v in enumerate((e1, e2, w1, w2, rank1, rank2)):
            out = jnp.where(lane == j, v, out)
        rt_ref[rows, :] = out
        rtt_ref[:, rows] = out.T[0:8, :]
        counts = counts + added
    carry_ref[...] = jnp.broadcast_to(counts, carry_ref.shape)
    cnt_ref[...] = jnp.broadcast_to(counts, cnt_ref.shape)


def _outproj(x, od, of, gt, sc, sh, g, wo_bf, wr, br):
    tm = OUTPROJ_TILES * TM
    tpb = SEQ // tm
    row = pl.BlockSpec((tm, D_MODEL), lambda i: (i, 0))
    half = pl.BlockSpec((tm, DIFF_WIDTH), lambda i: (i, 0))
    per_batch = pl.BlockSpec((None, 1, D_MODEL), lambda i: (i // tpb, 0, 0))
    const = lambda shape: pl.BlockSpec(shape, lambda i: (0,) * len(shape))
    return pl.pallas_call(
        _outproj_kernel,
        grid=(N_TOK // tm,),
        in_specs=[row, half, half, per_batch, per_batch, per_batch, const((1, D_MODEL)),
                  const((D_MODEL, D_MODEL)), const((D_MODEL, 2 * LANES)), const((1, LANES))],
        out_specs=[row, pl.BlockSpec((N_PLANES, tm, LANES), lambda i: (0, i, 0)),
                   pl.BlockSpec((tm, LANES), lambda i: (i, 0)), pl.BlockSpec((8, tm), lambda i: (0, i)),
                   const((8, LANES))],
        out_shape=[jax.ShapeDtypeStruct((N_TOK, D_MODEL), F32),
                   jax.ShapeDtypeStruct((N_PLANES, N_TOK, LANES), jnp.int32),
                   jax.ShapeDtypeStruct((N_TOK, LANES), F32),
                   jax.ShapeDtypeStruct((8, N_TOK), F32),
                   jax.ShapeDtypeStruct((8, LANES), F32)],
        scratch_shapes=[pltpu.VMEM((8, LANES), F32)],
        compiler_params=_params("arbitrary"),
        name="outproj_router",
    )(x, od, of, gt, sc, sh, g.reshape(1, D_MODEL), wo_bf, wr, br)


def _expert_kernel(layer, be_ref, cnt_ref, first_ref, slot_ref, next_ref, last_ref, xs_ref, wg_hbm, wu_hbm,
                   wd_hbm, ys_ref, wg_sc, wu_sc, wd_sc, wg_f32, wu_f32, wd_f32, sem):
    i = pl.program_id(0)
    cnt = cnt_ref[i]

    def weight_copies(expert, slot):
        return [pltpu.make_async_copy(hbm.at[layer, expert], buf.at[slot], sem.at[slot, n])
                for n, (hbm, buf) in enumerate(((wg_hbm, wg_f32), (wu_hbm, wu_f32), (wd_hbm, wd_f32)))]

    @pl.when(first_ref[i] == 1)
    def _():
        slot = slot_ref[i]

        @pl.when(i == 0)
        def _():
            for copy in weight_copies(be_ref[0], 0):
                copy.start()

        for copy in weight_copies(be_ref[i], slot):
            copy.wait()
        wg_sc[...] = wg_f32[slot].astype(BF16)
        wu_sc[...] = wu_f32[slot].astype(BF16)
        wd_sc[...] = wd_f32[slot].astype(BF16)

        @pl.when(next_ref[i] >= 0)
        def _():
            for copy in weight_copies(next_ref[i], 1 - slot):
                copy.start()

    def mlp(n_rows):
        rows = slice(0, n_rows)
        live = lax.broadcasted_iota(jnp.int32, (n_rows, LANES), 0) < cnt
        xb = _unpack_planes([jnp.where(live, xs_ref[p, rows, :], 0) for p in range(N_PLANES)]).astype(BF16)
        a = jnp.dot(xb, wg_sc[...], preferred_element_type=F32)
        u = jnp.dot(xb, wu_sc[...], preferred_element_type=F32)
        hid = (a / (1.0 + jnp.exp(-a)) * u).astype(BF16)
        _pack_planes(jnp.dot(hid, wd_sc[...], preferred_element_type=F32), ys_ref, rows)
        if n_rows < MOE_BLOCK:
            ys_ref[:, n_rows:, :] = jnp.zeros((N_PLANES, MOE_BLOCK - n_rows, LANES), ys_ref.dtype)

    half = MOE_BLOCK // 2

    @pl.when(cnt > half)
    def _():
        mlp(MOE_BLOCK)

    @pl.when((cnt > 0) & (cnt <= half))
    def _():
        mlp(half)


def _experts(layer, block_expert, block_count, xs, wg, wu, wd):
    idx = jnp.arange(MOE_NBLOCKS, dtype=jnp.int32)
    first = jnp.concatenate([jnp.ones((1,), jnp.bool_), block_expert[1:] != block_expert[:-1]])
    slot = (jnp.cumsum(first.astype(jnp.int32)) - 1) % 2
    later_first = (idx[None, :] > idx[:, None]) & first[None, :]
    nxt = jnp.min(jnp.where(later_first, block_expert[None, :], N_EXPERTS), axis=1)
    nxt = jnp.where(nxt == N_EXPERTS, -1, nxt).astype(jnp.int32)

    last_used = jnp.maximum(jnp.sum((block_count > 0).astype(jnp.int32)) - 1, 0).reshape(1)
    planes = pl.BlockSpec((N_PLANES, MOE_BLOCK, LANES),
                          lambda i, be, bc, fi, sl, nx, lu: (0, jnp.minimum(i, lu[0]), 0))
    hbm = pl.BlockSpec(memory_space=pl.ANY)
    grid_spec = pltpu.PrefetchScalarGridSpec(
        num_scalar_prefetch=6,
        grid=(MOE_NBLOCKS,),
        in_specs=[planes, hbm, hbm, hbm],
        out_specs=planes,
        scratch_shapes=[pltpu.VMEM((D_MODEL, D_EXPERT), BF16), pltpu.VMEM((D_MODEL, D_EXPERT), BF16),
                        pltpu.VMEM((D_EXPERT, D_MODEL), BF16),
                        pltpu.VMEM((2, D_MODEL, D_EXPERT), F32), pltpu.VMEM((2, D_MODEL, D_EXPERT), F32),
                        pltpu.VMEM((2, D_EXPERT, D_MODEL), F32), pltpu.SemaphoreType.DMA((2, 3))],
    )
    return pl.pallas_call(
        functools.partial(_expert_kernel, layer),
        grid_spec=grid_spec,
        out_shape=jax.ShapeDtypeStruct((N_PLANES, PLANE_ROWS, LANES), jnp.int32),
        compiler_params=_params("arbitrary"),
        name="expert_mlp",
    )(block_expert, block_count, first.astype(jnp.int32), slot.astype(jnp.int32), nxt, last_used,
      xs, wg, wu, wd)


def _slots(route_t, counts):
    counts = counts[0, :N_EXPERTS].astype(jnp.int32)
    padded = ((counts + MOE_BLOCK - 1) // MOE_BLOCK) * MOE_BLOCK
    pend = jnp.cumsum(padded)
    pstart = pend - padded
    bstart = jnp.arange(MOE_NBLOCKS, dtype=jnp.int32) * MOE_BLOCK
    block_expert = jnp.minimum(jnp.sum(bstart[:, None] >= pend[None, :], axis=1), N_EXPERTS - 1)
    block_expert = block_expert.astype(jnp.int32)
    mine = block_expert[:, None] == jnp.arange(N_EXPERTS, dtype=jnp.int32)[None, :]
    left = jnp.sum(jnp.where(mine, counts + pstart, 0), axis=1) - bstart
    block_count = jnp.clip(left, 0, MOE_BLOCK).astype(jnp.int32)
    return _slot_rows(route_t, pstart.astype(jnp.int32)), block_expert, block_count


def _slot_rows_kernel(base_ref, rt_ref, o_ref):
    n_tok = rt_ref.shape[1]
    for k in range(TOP_K):
        e = rt_ref[k:k + 1, :]
        b = jnp.zeros((1, n_tok), F32)
        for j in range(N_EXPERTS):
            b = jnp.where(e == float(j), base_ref[j].astype(F32), b)
        dest = (b + rt_ref[2 * TOP_K + k:2 * TOP_K + k + 1, :]).astype(jnp.int32)
        for p in range(N_PLANES):
            for c in range(n_tok // LANES):
                o_ref[k * N_PLANES + p, c:c + 1, :] = dest[:, c * LANES:(c + 1) * LANES] + p * PLANE_ROWS


def _slot_rows(route_t, base):
    tm = 16 * LANES
    grid_spec = pltpu.PrefetchScalarGridSpec(
        num_scalar_prefetch=1,
        grid=(N_TOK // tm,),
        in_specs=[pl.BlockSpec((8, tm), lambda i, base: (0, i))],
        out_specs=pl.BlockSpec((TOP_K * N_PLANES, tm // LANES, LANES), lambda i, base: (0, i, 0)),
    )
    return pl.pallas_call(
        _slot_rows_kernel,
        grid_spec=grid_spec,
        out_shape=jax.ShapeDtypeStruct((TOP_K * N_PLANES, N_TOK // LANES, LANES), jnp.int32),
        compiler_params=_params("arbitrary"),
        name="slot_rows",
    )(base, route_t)


def _sc_workers():
    info = plsc.get_sparse_core_info()
    return info.num_cores, info.num_cores * info.num_subcores


def _sc_scatter2(src, idx, out_rows):
    n_win = src.shape[0] // SC_WINDOW
    nc, nw = _sc_workers()
    steps = n_win // nw
    mesh = plsc.VectorSubcoreMesh(core_axis_name="c", subcore_axis_name="s")

    @functools.partial(
        pl.kernel, mesh=mesh,
        out_type=jax.ShapeDtypeStruct((out_rows, LANES), src.dtype),
        scratch_types=[pltpu.VMEM((2 * steps, SC_WINDOW), jnp.int32),
                       pltpu.VMEM((SC_INFLIGHT, SC_WINDOW, LANES), src.dtype),
                       pltpu.SemaphoreType.DMA((SC_INFLIGHT,)), pltpu.SemaphoreType.DMA((SC_INFLIGHT,))],
        name="sc_dispatch_scatter",
    )
    def k(src_hbm, idx_hbm, out_hbm, idx_v, rows_v, lsem, wsem):
        first = (lax.axis_index("s") * nc + lax.axis_index("c")) * steps
        pltpu.sync_copy(idx_hbm.at[pl.ds(first, steps)], idx_v.at[pl.ds(0, steps)])
        pltpu.sync_copy(idx_hbm.at[pl.ds(n_win + first, steps)], idx_v.at[pl.ds(steps, steps)])

        @pl.loop(0, steps, step=SC_INFLIGHT)
        def _(j):
            loads = [pltpu.async_copy(src_hbm.at[pl.ds((first + j + b) * SC_WINDOW, SC_WINDOW)],
                                      rows_v.at[b], lsem.at[b]) for b in range(SC_INFLIGHT)]
            writes = []
            for b in range(SC_INFLIGHT):
                loads[b].wait()
                for half in range(TOP_K):
                    dst = out_hbm.at[idx_v.at[half * steps + j + b]]
                    writes.append(pltpu.async_copy(rows_v.at[b], dst, wsem.at[b]))
            for w in writes:
                w.wait()

    return k(src, idx)


def _sc_gather(table, idx):
    n_out = idx.shape[0] * SC_WINDOW
    nc, nw = _sc_workers()
    steps = n_out // nw // SC_WINDOW
    mesh = plsc.VectorSubcoreMesh(core_axis_name="c", subcore_axis_name="s")

    @functools.partial(
        pl.kernel, mesh=mesh,
        out_type=jax.ShapeDtypeStruct((n_out, LANES), table.dtype),
        scratch_types=[pltpu.VMEM((steps, SC_WINDOW), jnp.int32),
                       pltpu.VMEM((SC_INFLIGHT, SC_WINDOW, LANES), table.dtype),
                       pltpu.SemaphoreType.DMA((SC_INFLIGHT,)), pltpu.SemaphoreType.DMA((SC_INFLIGHT,))],
        name="sc_combine_gather",
    )
    def k(table_hbm, idx_hbm, out_hbm, idx_v, rows_v, gsem, wsem):
        first = (lax.axis_index("s") * nc + lax.axis_index("c")) * steps
        pltpu.sync_copy(idx_hbm.at[pl.ds(first, steps)], idx_v)

        @pl.loop(0, steps, step=SC_INFLIGHT)
        def _(j):
            gathers = [pltpu.async_copy(table_hbm.at[idx_v.at[j + b]], rows_v.at[b], gsem.at[b])
                       for b in range(SC_INFLIGHT)]
            writes = []
            for b in range(SC_INFLIGHT):
                gathers[b].wait()
                dst = out_hbm.at[pl.ds((first + j + b) * SC_WINDOW, SC_WINDOW)]
                writes.append(pltpu.async_copy(rows_v.at[b], dst, wsem.at[b]))
            for w in writes:
                w.wait()

    return k(table, idx)


def _final_kernel(x_ref, rt_ref, y_ref, gt_ref, g_ref, o_ref):
    x = x_ref[...] + gt_ref[...] * _combine(rt_ref, y_ref)
    ms = jnp.mean(x * x, axis=-1, keepdims=True)
    o_ref[...] = x * lax.rsqrt(ms + EPS) * g_ref[...]


def _final(x, moe, gate, g):
    tpb = SEQ // TM
    row = pl.BlockSpec((TM, D_MODEL), lambda i: (i, 0))
    return pl.pallas_call(
        _final_kernel,
        grid=(N_TOK // TM,),
        in_specs=[row, pl.BlockSpec((TM, LANES), lambda i: (i, 0)),
                  pl.BlockSpec((TOP_K, N_PLANES, TM, LANES), lambda i: (0, 0, i, 0)),
                  pl.BlockSpec((None, 1, D_MODEL), lambda i: (i // tpb, 0, 0)),
                  pl.BlockSpec((1, D_MODEL), lambda i: (0, 0))],
        out_specs=row,
        out_shape=jax.ShapeDtypeStruct((N_TOK, D_MODEL), F32),
        compiler_params=_params("arbitrary"),
        name="final_norm",
    )(x, moe[0], moe[1], gate, g.reshape(1, D_MODEL))


def kernel(x, c, positions, w_ada, b_ada, g_mix, w_in, b_forget, lambda_q1, lambda_k1, lambda_q2,
           lambda_k2, g_subln, g_fox_out, w_out, g_ffn, w_router_group, b_router_group,
           w_router_expert, b_router_expert, w_expert_gate, w_expert_up, w_expert_down, g_final):
    mod = _modulation(c, w_ada, b_ada)
    mod = mod.reshape(DEPTH, BATCH, 6, 1, D_MODEL)
    tables = _rope_tables(positions)
    pq = _forget_placement()
    w_in_bf = jnp.pad(w_in.astype(BF16), ((0, 0), (0, 0), (0, IN_COLS_PAD - IN_COLS)))
    xf = x.reshape(N_TOK, D_MODEL)
    moe = None
    gate = None
    for l in range(DEPTH):
        sh1, sc1, gt1, sh2, sc2, gt2 = (mod[l, :, j] for j in range(6))
        w_bf = (l, w_in_bf)
        bfp =jnp.pad(b_forget[l], (0, LANES - N_FOX_HEADS)).reshape(1, LANES)
        xf, (dq, dk, dv, fq, fk, fv) = _inproj(xf, moe, gate, sc1, sh1, g_mix[l], w_bf, bfp, tables, pq)

        lambda_init = 0.8 - 0.6 * float(np.exp(-0.3 * l))
        lamv = jnp.zeros((8, LANES), F32).at[0:4, 0:HEAD_DIM].set(
            jnp.stack([lambda_q1[l], lambda_k1[l], lambda_q2[l], lambda_k2[l]]))
        g_d = g_subln[l].reshape(1, LANES)
        g_f = jnp.concatenate([g_fox_out[l], g_fox_out[l]]).reshape(1, LANES)
        od = _attention(True, lambda_init, dq, dk, dv, g_d, lamv)
        of = _attention(False, lambda_init, fq, fk, fv, g_f, lamv)

        wr32 = jnp.pad(jnp.concatenate([w_router_group[l], w_router_expert[l]], axis=1),
                       ((0, 0), (0, LANES - N_GROUPS - N_EXPERTS)))
        wr_hi = wr32.astype(BF16)
        wr_lo = (wr32 - wr_hi.astype(F32)).astype(BF16)
        wr = jnp.concatenate([wr_hi, wr_lo], axis=1)
        br = jnp.pad(jnp.concatenate([b_router_group[l], b_router_expert[l]]),
                     (0, LANES - N_GROUPS - N_EXPERTS)).reshape(1, LANES)
        xf, h2, route, route_t, counts = _outproj(xf, od, of, gt1, sc2, sh2, g_ffn[l],
                                                  w_out[l].astype(BF16), wr, br)

        rows, block_expert, block_count = _slots(route_t, counts)
        rows = rows.reshape(TOP_K * N_PLANES * N_TOK // SC_WINDOW, SC_WINDOW)
        xs = _sc_scatter2(h2.reshape(N_PLANES * N_TOK, LANES), rows, N_PLANES * PLANE_ROWS)
        ys = _experts(l, block_expert, block_count, xs.reshape(N_PLANES, PLANE_ROWS, LANES),
                      w_expert_gate, w_expert_up, w_expert_down)
        y2 = _sc_gather(ys.reshape(N_PLANES * PLANE_ROWS, LANES), rows)
        moe = (route, y2.reshape(TOP_K, N_PLANES, N_TOK, LANES))
        gate = gt2
    out = _final(xf, moe, gate, g_final)
    return out.reshape(BATCH, SEQ, D_MODEL)
```

```python
import functools

import numpy as np
import jax
import jax.numpy as jnp
from jax import lax
from jax.experimental import pallas as pl
from jax.experimental.pallas import tpu as pltpu
from jax.experimental.pallas import tpu_sc as plsc

D_MODEL = 1024
BATCH = 4
SEQ = 4096
DEPTH = 4
N_TOK = BATCH * SEQ

CHUNK = 64
HEAD_DIM = 64
N_DIFF_HEADS = 4
N_FOX_HEADS = 8
DIFF_WIDTH = 512
FOX_WIDTH = 512
IN_COLS = 3 * DIFF_WIDTH + 3 * FOX_WIDTH + N_FOX_HEADS
ROT_DIM = 16
ROPE_THETA = 500000.0
N_GROUPS = 4
EXPERTS_PER_GROUP = 8
N_EXPERTS = 32
TOP_K = 2
D_EXPERT = 512
EPS = 1e-6

LANES = 128
IN_COLS_PAD = 3200
FF_COL = 3 * DIFF_WIDTH + 3 * FOX_WIDTH
QK_WIDTH = 8 * LANES
TM = 512
OUTPROJ_TILES = 2
TQ = 512
ATTN_MAPS = 4
ATTN_TQ = 256
N_LATE_CHAINS = 1
ONES_ROWS = 16
ATTN_ROWS = 128
LOG2E = 1.4426950408889634
MOE_BLOCK = 512
MOE_ROWS = N_TOK * TOP_K + N_EXPERTS * MOE_BLOCK
MOE_NBLOCKS = MOE_ROWS // MOE_BLOCK
PLANE_ROWS = MOE_ROWS + 8 * 273
N_PLANES = D_MODEL // 2 // LANES
SC_WINDOW = 128
SC_INFLIGHT = 4
NEG = -1e30
VMEM_LIMIT = 56 * 1024 * 1024

F32 = jnp.float32
BF16 = jnp.bfloat16


def _bf16_round(x):
    return x.astype(BF16).astype(F32)


def _lane_iota(shape):
    return lax.broadcasted_iota(jnp.int32, shape, 1)


def _params(*sem):
    return pltpu.CompilerParams(dimension_semantics=sem, vmem_limit_bytes=VMEM_LIMIT)


def _pack_planes(y, o_ref, rows=slice(None)):
    bits = lax.bitcast_convert_type(_bf16_round(y), jnp.uint32)
    half = D_MODEL // 2
    word = bits[:, half:] | lax.shift_right_logical(bits[:, :half], jnp.uint32(16))
    word = lax.bitcast_convert_type(word, jnp.int32)
    for p in range(N_PLANES):
        o_ref[p, rows, :] = word[:, p * LANES:(p + 1) * LANES]


def _unpack_planes(planes):
    lo, hi = [], []
    for w in planes:
        u = lax.bitcast_convert_type(w, jnp.uint32)
        lo.append(lax.bitcast_convert_type(lax.shift_left(u, jnp.uint32(16)), F32))
        hi.append(lax.bitcast_convert_type(u & jnp.uint32(0xFFFF0000), F32))
    return jnp.concatenate(lo + hi, axis=1)


def _combine(route_ref, y_ref):
    rt = route_ref[...]
    y0 = _unpack_planes([y_ref[0, p] for p in range(N_PLANES)])
    y1 = _unpack_planes([y_ref[1, p] for p in range(N_PLANES)])
    return rt[:, 2:3] * y0 + rt[:, 3:4] * y1


def _mod_kernel(c_ref, w_ref, b_ref, o_ref):
    c = c_ref[...]
    cond = c / (1.0 + jnp.exp(-c))
    ch = cond.astype(BF16)
    cl = (cond - ch.astype(F32)).astype(BF16)
    w = w_ref[...]
    wh = w.astype(BF16)
    wl = (w - wh.astype(F32)).astype(BF16)
    acc = jnp.dot(ch, wh, preferred_element_type=F32)
    acc += jnp.dot(cl, wh, preferred_element_type=F32)
    acc += jnp.dot(ch, wl, preferred_element_type=F32)
    o_ref[...] = acc + b_ref[...]


def _modulation(c, w_ada, b_ada):
    rows = 16
    tn = 1536
    c_pad = jnp.zeros((rows, D_MODEL), F32).at[:BATCH].set(c)
    out = pl.pallas_call(
        _mod_kernel,
        grid=(DEPTH, 6 * D_MODEL // tn),
        in_specs=[
            pl.BlockSpec((rows, D_MODEL), lambda l, n: (0, 0)),
            pl.BlockSpec((None, D_MODEL, tn), lambda l, n: (l, 0, n)),
            pl.BlockSpec((None, 1, tn), lambda l, n: (l, 0, n)),
        ],
        out_specs=pl.BlockSpec((None, rows, tn), lambda l, n: (l, 0, n)),
        out_shape=jax.ShapeDtypeStruct((DEPTH, rows, 6 * D_MODEL), F32),
        compiler_params=_params("arbitrary", "arbitrary"),
        name="adaln_mod",
    )(c_pad, w_ada, b_ada.reshape(DEPTH, 1, 6 * D_MODEL))
    return out[:, :BATCH]


def _rope_kernel(pos_ref, inv_ref, c_ref, sa_ref, sb_ref):
    ang = pos_ref[...].astype(F32) * inv_ref[...]
    j = _lane_iota(ang.shape) % HEAD_DIM
    cosv = jnp.cos(ang)
    sinv = jnp.sin(ang)
    half = ROT_DIM // 2
    c_ref[...] = jnp.where(j < ROT_DIM, cosv, 1.0)
    sa_ref[...] = jnp.where(j < half, -sinv, 0.0)
    sb_ref[...] = jnp.where((j >= half) & (j < ROT_DIM), sinv, 0.0)


def _rope_tables(positions):
    half = ROT_DIM // 2
    inv = ROPE_THETA ** (-jnp.arange(0, ROT_DIM, 2, dtype=F32) / ROT_DIM)
    lane = np.arange(LANES)
    inv_lane = inv[(lane % HEAD_DIM) % half].reshape(1, LANES)
    spec = pl.BlockSpec((TM, LANES), lambda i: (i, 0))
    shape = jax.ShapeDtypeStruct((N_TOK, LANES), F32)
    return pl.pallas_call(
        _rope_kernel,
        grid=(N_TOK // TM,),
        in_specs=[pl.BlockSpec((TM, 1), lambda i: (i, 0)),
                  pl.BlockSpec((1, LANES), lambda i: (0, 0))],
        out_specs=[spec, spec, spec],
        out_shape=[shape, shape, shape],
        compiler_params=_params("arbitrary"),
        name="rope_tables",
    )(positions.reshape(N_TOK, 1), inv_lane)


def _rms_mod(x, g, sc, sh):
    ms = jnp.mean(x * x, axis=-1, keepdims=True)
    return (x * lax.rsqrt(ms + EPS) * g) * (1.0 + sc) + sh


def _inproj_kernel(fuse, *refs):
    if fuse:
        (x_ref, rt_ref, y_ref, gt_ref, sc_ref, sh_ref, g_ref, w_ref, bf_ref, c_ref, sa_ref, sb_ref,
         pq_ref, xo_ref, dq_ref, dk_ref, dv_ref, fq_ref, fk_ref, fv_ref, carry_ref, h_even, h_odd) = refs
    else:
        (x_ref, sc_ref, sh_ref, g_ref, w_ref, bf_ref, c_ref, sa_ref, sb_ref,
         pq_ref, dq_ref, dk_ref, dv_ref, fq_ref, fk_ref, fv_ref, carry_ref, h_even, h_odd) = refs
    step = pl.program_id(0)

    def normalise(h_ref):
        if fuse:
            x = x_ref[...] + gt_ref[...] * _combine(rt_ref, y_ref)
            xo_ref[...] = x
        else:
            x = x_ref[...]
        h_ref[...] = _rms_mod(x, g_ref[...], sc_ref[...], sh_ref[...]).astype(BF16)

    @pl.when(step == 0)
    def _():
        h_odd[...] = jnp.zeros_like(h_odd)

    @pl.when((step == 0) | ((step - 1) % (SEQ // TM) == 0))
    def _():
        carry_ref[...] = jnp.zeros_like(carry_ref)

    for parity, (h_new, h_old) in enumerate(((h_even, h_odd), (h_odd, h_even))):
        @pl.when(step % 2 == parity)
        def _():
            _inproj_project(h_old[...], w_ref, bf_ref, c_ref, sa_ref, sb_ref, pq_ref, dq_ref, dk_ref,
                            dv_ref, fq_ref, fk_ref, fv_ref, carry_ref)
            normalise(h_new)


def _inproj_project(hb, w_ref, bf_ref, c_ref, sa_ref, sb_ref, pq_ref, dq_ref, dk_ref, dv_ref, fq_ref,
                    fk_ref, fv_ref, carry_ref):
    lane = _lane_iota((TM, LANES))
    nh = N_FOX_HEADS

    def pack3(a):
        hi = _bf16_round(a)
        r1 = a - hi
        mid = _bf16_round(r1)
        lo = _bf16_round(r1 - mid)
        return jnp.where(lane < nh, hi,
                         jnp.where(lane < 2 * nh, pltpu.roll(mid, nh, 1),
                                   jnp.where(lane < 3 * nh, pltpu.roll(lo, 2 * nh, 1), 0.0)))

    z = jnp.dot(hb, w_ref[:, FF_COL:FF_COL + LANES], preferred_element_type=F32) + bf_ref[...]

    low = lane < HEAD_DIM
    rc, rsa, rsb = c_ref[...], sa_ref[...], sb_ref[...]
    scale = HEAD_DIM ** -0.5 * LOG2E

    def split_store(chunk, o_ref, m, extra_a=None, extra_b=None):
        a = jnp.where(low, chunk, 0.0)
        b = jnp.where(low, pltpu.roll(chunk, HEAD_DIM, 1), 0.0)
        if extra_a is not None:
            a = a + extra_a
            b = b + extra_b
        o_ref[:, (2 * m) * LANES:(2 * m + 1) * LANES] = a.astype(BF16)
        o_ref[:, (2 * m + 1) * LANES:(2 * m + 2) * LANES] = b.astype(BF16)

    def rope(xc):
        return xc * rc + pltpu.roll(xc, LANES - ROT_DIM // 2, 1) * rsa + pltpu.roll(xc, ROT_DIM // 2, 1) * rsb

    pdq = jnp.dot(hb, w_ref[:, 0:DIFF_WIDTH], preferred_element_type=F32)
    for m in range(N_DIFF_HEADS):
        split_store(rope(pdq[:, m * LANES:(m + 1) * LANES]) * scale, dq_ref, m)
    pdk = jnp.dot(hb, w_ref[:, DIFF_WIDTH:2 * DIFF_WIDTH], preferred_element_type=F32)
    for m in range(N_DIFF_HEADS):
        split_store(rope(pdk[:, m * LANES:(m + 1) * LANES]), dk_ref, m)
    def store_values_t(pv, o_ref, width):
        ones = jnp.ones((ONES_ROWS, TM), BF16)
        for m in range(4):
            vt = pv[:, m * LANES:(m + 1) * LANES].T.astype(BF16)
            for i in range(LANES // width):
                o_ref[m * (LANES // width) + i, 0:width, :] = vt[i * width:(i + 1) * width]
                o_ref[m * (LANES // width) + i, width:width + ONES_ROWS, :] = ones

    store_values_t(jnp.dot(hb, w_ref[:, 2 * DIFF_WIDTH:3 * DIFF_WIDTH], preferred_element_type=F32),
                   dv_ref, 2 * HEAD_DIM)
    o = 3 * DIFF_WIDTH
    store_values_t(jnp.dot(hb, w_ref[:, o + 2 * FOX_WIDTH:o + 3 * FOX_WIDTH],
                           preferred_element_type=F32), fv_ref, HEAD_DIM)

    logf =jnp.minimum(z, 0.0) - jnp.log(1.0 + jnp.exp(-jnp.abs(z)))
    logf = jnp.where(lane < nh, logf, 0.0)
    row = lax.broadcasted_iota(jnp.int32, (TM, TM), 0)
    col = lax.broadcasted_iota(jnp.int32, (TM, TM), 1)
    tri = (row >= col).astype(BF16)
    r = jnp.dot(tri, pack3(logf).astype(BF16), preferred_element_type=F32)
    cs = r + pltpu.roll(r, LANES - nh, 1) + pltpu.roll(r, LANES - 2 * nh, 1)
    cf = jnp.where(lane < nh, cs + carry_ref[0:1, :], 0.0)
    carry_ref[...] = jnp.broadcast_to(cf[TM - 1:TM, :], carry_ref.shape)

    t3 = jnp.where(lane == 3 * nh, 1.0, pack3(cf * LOG2E)).astype(BF16)
    aug = jnp.dot(t3, pq_ref[...], preferred_element_type=F32)

    pfq =jnp.dot(hb, w_ref[:, o:o + FOX_WIDTH], preferred_element_type=F32)
    for m in range(N_FOX_HEADS // 2):
        split_store(pfq[:, m * LANES:(m + 1) * LANES] * scale, fq_ref, m,
                    aug[:, (2 * m) * LANES:(2 * m + 1) * LANES],
                    aug[:, (2 * m + 1) * LANES:(2 * m + 2) * LANES])
    pfk = jnp.dot(hb, w_ref[:, o + FOX_WIDTH:o + 2 * FOX_WIDTH], preferred_element_type=F32)
    for m in range(N_FOX_HEADS // 2):
        split_store(pfk[:, m * LANES:(m + 1) * LANES], fk_ref, m,
                    aug[:, QK_WIDTH + (2 * m) * LANES:QK_WIDTH + (2 * m + 1) * LANES],
                    aug[:, QK_WIDTH + (2 * m + 1) * LANES:QK_WIDTH + (2 * m + 2) * LANES])


def _forget_placement():
    nh = N_FOX_HEADS
    p = np.zeros((LANES, 2 * QK_WIDTH), np.float32)
    for h in range(nh):
        base_q = h * LANES + HEAD_DIM
        base_k = QK_WIDTH + h * LANES + HEAD_DIM
        for part in range(3):
            p[part * nh + h, base_q + part] = 1.0
            p[3 * nh, base_q + 3 + part] = 1.0
            p[3 * nh, base_k + part] = 1.0
            p[part * nh + h, base_k + 3 + part] = -1.0
    return jnp.asarray(p, BF16)


def _inproj(x, moe, gate, sc, sh, g, w_bf, b_forget, tables, pq):
    fuse = moe is not None
    tpb = SEQ // TM
    n_tiles = N_TOK // TM
    new = lambda s: jnp.minimum(s, n_tiles - 1)
    old = lambda s: jnp.maximum(s - 1, 0)
    row = pl.BlockSpec((TM, D_MODEL), lambda s: (new(s), 0))
    per_batch = pl.BlockSpec((None, 1, D_MODEL), lambda s: (new(s) // tpb, 0, 0))
    const = lambda shape: pl.BlockSpec(shape, lambda s: (0,) * len(shape))
    tab = pl.BlockSpec((TM, LANES), lambda s: (old(s), 0))
    in_specs = [row]
    args = [x]
    if fuse:
        in_specs += [pl.BlockSpec((TM, LANES), lambda s: (new(s), 0)),
                     pl.BlockSpec((TOP_K, N_PLANES, TM, LANES), lambda s: (0, 0, new(s), 0)), per_batch]
        args += [moe[0], moe[1], gate]
    layer, w_all = w_bf
    in_specs += [per_batch, per_batch, const((1, D_MODEL)),
                 pl.BlockSpec((None, D_MODEL, IN_COLS_PAD), lambda s: (layer, 0, 0)),
                 const((1, LANES)), tab, tab, tab, const((LANES, 2 * QK_WIDTH))]
    args += [sc, sh, g.reshape(1, D_MODEL), w_all, b_forget, *tables, pq]
    wide = pl.BlockSpec((TM, QK_WIDTH), lambda s: (old(s), 0))
    def vspec(heads, width):
        rows = width + ONES_ROWS
        return (pl.BlockSpec((None, heads, None, rows, TM),
                             lambda s: (old(s) // tpb, 0, old(s) % tpb, 0, 0)),
                jax.ShapeDtypeStruct((BATCH, heads, tpb, rows, TM), BF16))

    wide_s = jax.ShapeDtypeStruct((N_TOK, QK_WIDTH), BF16)
    dv_spec, dv_s = vspec(N_DIFF_HEADS, 2 * HEAD_DIM)
    fv_spec, fv_s = vspec(N_FOX_HEADS, HEAD_DIM)
    out_specs = [wide, wide, dv_spec, wide, wide, fv_spec]
    out_shape = [wide_s, wide_s, dv_s, wide_s, wide_s, fv_s]
    if fuse:
        out_specs = [row] + out_specs
        out_shape = [jax.ShapeDtypeStruct((N_TOK, D_MODEL), F32)] + out_shape
    outs = pl.pallas_call(
        functools.partial(_inproj_kernel, fuse),
        grid=(n_tiles + 1,),
        in_specs=in_specs,
        out_specs=out_specs,
        out_shape=out_shape,
        scratch_shapes=[pltpu.VMEM((8, LANES), F32), pltpu.VMEM((TM, D_MODEL), BF16),
                        pltpu.VMEM((TM, D_MODEL), BF16)],
        compiler_params=_params("arbitrary"),
        name="norm_inproj",
    )(*args)
    if fuse:
        return outs[0], outs[1:]
    return x, outs


def _attn_kernel(diff, lambda_init, *refs):
    q_refs, k_refs = refs[:ATTN_MAPS], refs[ATTN_MAPS:2 * ATTN_MAPS]
    v_ref, g_ref, lam_ref, o_ref = refs[2 * ATTN_MAPS:2 * ATTN_MAPS + 4]
    scratch = refs[2 * ATTN_MAPS + 4:]
    nq = SEQ // TQ
    n_half = TQ // ATTN_TQ
    feat = 2 * HEAD_DIM if diff else HEAD_DIM
    chains = []
    for mi, (q_ref, k_ref) in enumerate(zip(q_refs, k_refs)):
        for h in range(n_half):
            c = mi * n_half + h
            qt_sc, s_sc, p_sc, m_sc, a_sc, acc_sc = scratch[c::ATTN_MAPS * n_half]
            vh = mi // 2 if diff else mi
            chains.append((h, k_ref, qt_sc, s_sc, p_sc, m_sc, a_sc, acc_sc, q_ref, vh))
    order = [chains[mi * n_half + h] for h in range(n_half) for mi in range(ATTN_MAPS)]
    early, late = order[:-N_LATE_CHAINS], order[-N_LATE_CHAINS:]

    def load_queries(qi):
        for mi in range(ATTN_MAPS):
            q_ref = chains[mi * n_half][8]
            qt = q_ref[pl.ds(pl.multiple_of(qi * TQ, TQ), TQ), :].astype(F32).T.astype(BF16)
            for h in range(n_half):
                chains[mi * n_half + h][2][...] = qt[:, h * ATTN_TQ:(h + 1) * ATTN_TQ]

    def reset_state():
        for chain in chains:
            m_sc, _, acc_sc = chain[5:8]
            m_sc[...] = jnp.full(m_sc.shape, NEG, F32)
            acc_sc[...] = jnp.zeros(acc_sc.shape, F32)

    def n_keys(chain, masked):
        return (chain[0] + 1) * ATTN_TQ if masked else TQ

    def scores(chain, j, masked):
        h, k_ref, qt_sc, s_sc = chain[:4]
        nk = n_keys(chain, masked)
        off = pl.multiple_of(j * TQ, TQ)
        s = jnp.dot(k_ref[pl.ds(off, nk), :], qt_sc[...], preferred_element_type=F32)
        if masked:
            kk = lax.broadcasted_iota(jnp.int32, (nk, ATTN_TQ), 0)
            qq = h * ATTN_TQ + lax.broadcasted_iota(jnp.int32, (nk, ATTN_TQ), 1)
            s = jnp.where((kk // CHUNK <= qq // CHUNK) if diff else (kk <= qq), s, NEG)
        s_sc[0:nk, :] = s

    def softmax(chain, masked):
        s_sc, p_sc, m_sc, a_sc = chain[3:7]
        nk = n_keys(chain, masked)
        m_all = m_sc[...]
        m_parts = []
        for c0 in range(0, ATTN_TQ, LANES):
            cols = slice(c0, c0 + LANES)
            pm = s_sc[0:ATTN_ROWS, cols]
            for r0 in range(ATTN_ROWS, nk, ATTN_ROWS):
                pm = jnp.maximum(pm, s_sc[r0:r0 + ATTN_ROWS, cols])
            m_new = jnp.maximum(m_all[:, cols], jnp.max(pm, axis=0, keepdims=True))
            for r0 in range(0, nk, ATTN_ROWS):
                p = jnp.exp2(s_sc[r0:r0 + ATTN_ROWS, cols] - m_new)
                p_sc[r0:r0 + ATTN_ROWS, cols] = p.astype(BF16)
            m_parts.append(m_new)
        m_new = jnp.concatenate(m_parts, axis=1)
        a_sc[...] = jnp.exp2(m_all - m_new)
        m_sc[...] = m_new

    def values(chain, j, masked=False):
        p_sc, a_sc, acc_sc, vh = chain[4], chain[6], chain[7], chain[9]
        nk = n_keys(chain, masked)
        pv = jnp.dot(v_ref[vh, j, :, 0:nk], p_sc[0:nk, :], preferred_element_type=F32)
        acc_sc[...] = a_sc[...] * acc_sc[...] + pv

    def idle_late():
        for chain in late:
            chain[4][...] = jnp.zeros(chain[4].shape, BF16)
            chain[6][...] = jnp.ones(chain[6].shape, F32)

    def consume(j, cur_masked=False, nxt=None, nxt_masked=False, final=False, before_next=None):
        def open_late(chain):
            scores(chain, j, cur_masked)
            values(chain, jnp.maximum(j - 1, 0))

        open_late(late[0])
        for i, chain in enumerate(early):
            softmax(chain, cur_masked)
            if i == 0:
                for other in late[1:]:
                    open_late(other)
                if before_next is not None:
                    before_next()
            if nxt is not None:
                scores(chain, nxt, nxt_masked)
            values(chain, j, cur_masked)
        for chain in late:
            softmax(chain, cur_masked)
        if final:
            for chain in late:
                values(chain, j, cur_masked)

    def finalize(qi):
        g = g_ref[...]
        rows = pl.ds(pl.multiple_of(qi * TQ, TQ), TQ)
        for pair in range(ATTN_MAPS // 2):
            ot = [jnp.concatenate([chains[mi * n_half + h][7][0:feat] / chains[mi * n_half + h][7][feat:feat + 1]
                                   for h in range(n_half)], axis=1)
                  for mi in (2 * pair, 2 * pair + 1)]
            cols = slice(pair * LANES, (pair + 1) * LANES)
            if diff:
                lv = lam_ref[...]
                lam = (jnp.exp(jnp.sum(lv[0:1] * lv[1:2], axis=1, keepdims=True))
                       - jnp.exp(jnp.sum(lv[2:3] * lv[3:4], axis=1, keepdims=True)) + lambda_init)
                o = (ot[0] - lam * ot[1]).T
                y = o * lax.rsqrt(jnp.mean(o * o, axis=1, keepdims=True) + EPS) * g
                o_ref[rows, cols] = (y * (1.0 - lambda_init)).astype(o_ref.dtype)
            else:
                o = jnp.concatenate(ot, axis=0).T
                low = _lane_iota((TQ, LANES)) < HEAD_DIM
                sq = o * o
                msa = jnp.sum(jnp.where(low, sq, 0.0), axis=1, keepdims=True) / HEAD_DIM
                msb = jnp.sum(jnp.where(low, 0.0, sq), axis=1, keepdims=True) / HEAD_DIM
                inv = jnp.where(low, lax.rsqrt(msa + EPS), lax.rsqrt(msb + EPS))
                o_ref[rows, cols] = (o * inv * g).astype(o_ref.dtype)

    load_queries(0)
    reset_state()
    idle_late()
    for chain in early:
        scores(chain, 0, True)

    @pl.loop(0, nq)
    def _(qi):
        n_plain = jnp.maximum(qi - 1, 0)

        def run(first, count):
            for i in range(count):
                consume(first + i, nxt=first + i + 1)

        @pl.loop(0, n_plain // 4)
        def _(t):
            run(4 * t, 4)

        done = (n_plain // 4) * 4

        @pl.when(n_plain - done >= 2)
        def _():
            run(done, 2)

        def last_blocks(to_next_tile):
            if to_next_tile:
                consume(qi, cur_masked=True, nxt=0, final=True, before_next=lambda: load_queries(qi + 1))
            else:
                consume(qi, cur_masked=True, final=True)
            finalize(qi)

        for to_next_tile in (True, False):
            more = (qi < nq - 1) if to_next_tile else (qi == nq - 1)

            last_tile_odd = (nq - 2) % 2 == 1
            for odd in ((True, False) if to_next_tile else (last_tile_odd,)):
                @pl.when(more & (qi > 0) & ((n_plain % 2 == 1) == odd))
                def _():
                    if odd:
                        run(qi - 2, 1)
                    consume(qi - 1, nxt=qi, nxt_masked=True)
                    last_blocks(to_next_tile)

            if to_next_tile:
                @pl.when(qi == 0)
                def _():
                    last_blocks(to_next_tile)

        reset_state()
        idle_late()


def _attention(diff, lambda_init, q, k, v, g, lamv):
    nq = SEQ // TQ
    n_steps = 8 // ATTN_MAPS
    v_heads = v.shape[1] // n_steps
    kspec = lambda m: pl.BlockSpec((SEQ, LANES), lambda b, p: (b, ATTN_MAPS * p + m))
    maps = [kspec(m) for m in range(ATTN_MAPS)]
    return pl.pallas_call(
        functools.partial(_attn_kernel, diff, lambda_init),
        grid=(BATCH, n_steps),
        in_specs=maps + maps + [
            pl.BlockSpec((None, v_heads, nq, v.shape[3], TQ), lambda b, p: (b, p, 0, 0, 0)),
            pl.BlockSpec((1, LANES), lambda b, p: (0, 0)),
            pl.BlockSpec((8, LANES), lambda b, p: (0, 0))],
        out_specs=pl.BlockSpec((SEQ, ATTN_MAPS // 2 * LANES), lambda b, p: (b, p)),
        out_shape=jax.ShapeDtypeStruct((N_TOK, DIFF_WIDTH), BF16),
        scratch_shapes=[pltpu.VMEM(shape, dt)
                        for shape, dt in (((LANES, ATTN_TQ), BF16), ((TQ, ATTN_TQ), F32),
                                          ((TQ, ATTN_TQ), BF16), ((1, ATTN_TQ), F32),
                                          ((1, ATTN_TQ), F32), ((v.shape[3], ATTN_TQ), F32))
                        for _ in range(ATTN_MAPS * TQ // ATTN_TQ)],
        compiler_params=_params("arbitrary", "arbitrary"),
        name="diff_attention" if diff else "fox_attention",
    )(*([q] * ATTN_MAPS), *([k] * ATTN_MAPS), v, g, lamv)


def _outproj_kernel(x_ref, od_ref, of_ref, gt_ref, sc_ref, sh_ref, g_ref, wo_ref, wr_ref, br_ref,
                    x1_ref, h2_ref, rt_ref, rtt_ref, cnt_ref, carry_ref):
    @pl.when(pl.program_id(0) == 0)
    def _():
        carry_ref[...] = jnp.zeros_like(carry_ref)

    tiles = [slice(t * TM, (t + 1) * TM) for t in range(OUTPROJ_TILES)]
    lane = _lane_iota((TM, LANES))
    lanef = lane.astype(F32)
    big = float(LANES)

    def project(rows):
        mix = jnp.dot(od_ref[rows, :], wo_ref[0:DIFF_WIDTH, :], preferred_element_type=F32)
        return mix + jnp.dot(of_ref[rows, :], wo_ref[DIFF_WIDTH:, :], preferred_element_type=F32)

    def normalise(rows, mix):
        x1 = x_ref[rows, :] + gt_ref[...] * mix
        x1_ref[rows, :] = x1
        h = _rms_mod(x1, g_ref[...], sc_ref[...], sh_ref[...])
        hh = h.astype(BF16)
        _pack_planes(h, h2_ref, rows)
        return hh, (h - hh.astype(F32)).astype(BF16)

    def router_logits(hh, hl):
        r1 = jnp.dot(hh, wr_ref[...], preferred_element_type=F32)
        r2 = jnp.dot(hl, wr_ref[:, 0:LANES], preferred_element_type=F32)
        return r1[:, 0:LANES] + r1[:, LANES:] + r2 + br_ref[...]

    def top_k(logits):
        isg = lane < N_GROUPS
        lg = jnp.where(isg, logits, NEG)
        mg = jnp.max(lg, axis=1, keepdims=True)
        sg = jnp.sum(jnp.where(isg, jnp.exp(lg - mg), 0.0), axis=1, keepdims=True)
        p_g = 1.0 / sg
        gsel = jnp.min(jnp.where(isg & (lg == mg), lanef, big), axis=1, keepdims=True)
        lo = N_GROUPS + gsel * EXPERTS_PER_GROUP
        ise = (lanef >= lo) & (lanef < lo + EXPERTS_PER_GROUP)
        le = jnp.where(ise, logits, NEG)
        t1 = jnp.max(le, axis=1, keepdims=True)
        i1 = jnp.min(jnp.where(ise & (le == t1), lanef, big), axis=1, keepdims=True)
        ise2 = ise & (lanef != i1)
        le2 = jnp.where(ise2, logits, NEG)
        t2 = jnp.max(le2, axis=1, keepdims=True)
        i2 = jnp.min(jnp.where(ise2 & (le2 == t2), lanef, big), axis=1, keepdims=True)
        d = jnp.exp(t2 - t1)
        return i1 - N_GROUPS, i2 - N_GROUPS, p_g / (1.0 + d), p_g * d / (1.0 + d)

    def earlier_in_tile(e1, e2):
        both = jnp.where((lanef == e1) | (lanef == e2), 1.0, 0.0)
        row = lax.broadcasted_iota(jnp.int32, (TM, TM), 0)
        col = lax.broadcasted_iota(jnp.int32, (TM, TM), 1)
        before = jnp.dot((row > col).astype(BF16), both.astype(BF16), preferred_element_type=F32)
        return before, jnp.sum(both, axis=0, keepdims=True)

    mixes = [project(rows) for rows in tiles]
    splits = [normalise(rows, mix) for rows, mix in zip(tiles, mixes)]
    logits = [router_logits(hh, hl) for hh, hl in splits]
    picks = [top_k(lg) for lg in logits]
    befores = [earlier_in_tile(e1, e2) for e1, e2, _, _ in picks]
    counts = carry_ref[0:1, :]
    for rows, (e1, e2, w1, w2), (before, added) in zip(tiles, picks, befores):
        before = before + counts
        rank1 = jnp.sum(jnp.where(lanef == e1, before, 0.0), axis=1, keepdims=True)
        rank2 = jnp.sum(jnp.where(lanef == e2, before, 0.0), axis=1, keepdims=True)
        out = jnp.zeros((TM, LANES), F32)
        for j, v in enumerate((e1, e2, w1, w2, rank1, rank2)):
            out = jnp.where(lane == j, v, out)
        rt_ref[rows, :] = out
        rtt_ref[:, rows] = out.T[0:8, :]
        counts = counts + added
    carry_ref[...] = jnp.broadcast_to(counts, carry_ref.shape)
    cnt_ref[...] = jnp.broadcast_to(counts, cnt_ref.shape)


def _outproj(x, od, of, gt, sc, sh, g, wo_bf, wr, br):
    tm = OUTPROJ_TILES * TM
    tpb = SEQ // tm
    row = pl.BlockSpec((tm, D_MODEL), lambda i: (i, 0))
    half = pl.BlockSpec((tm, DIFF_WIDTH), lambda i: (i, 0))
    per_batch = pl.BlockSpec((None, 1, D_MODEL), lambda i: (i // tpb, 0, 0))
    const = lambda shape: pl.BlockSpec(shape, lambda i: (0,) * len(shape))
    return pl.pallas_call(
        _outproj_kernel,
        grid=(N_TOK // tm,),
        in_specs=[row, half, half, per_batch, per_batch, per_batch, const((1, D_MODEL)),
                  const((D_MODEL, D_MODEL)), const((D_MODEL, 2 * LANES)), const((1, LANES))],
        out_specs=[row, pl.BlockSpec((N_PLANES, tm, LANES), lambda i: (0, i, 0)),
                   pl.BlockSpec((tm, LANES), lambda i: (i, 0)), pl.BlockSpec((8, tm), lambda i: (0, i)),
                   const((8, LANES))],
        out_shape=[jax.ShapeDtypeStruct((N_TOK, D_MODEL), F32),
                   jax.ShapeDtypeStruct((N_PLANES, N_TOK, LANES), jnp.int32),
                   jax.ShapeDtypeStruct((N_TOK, LANES), F32),
                   jax.ShapeDtypeStruct((8, N_TOK), F32),
                   jax.ShapeDtypeStruct((8, LANES), F32)],
        scratch_shapes=[pltpu.VMEM((8, LANES), F32)],
        compiler_params=_params("arbitrary"),
        name="outproj_router",
    )(x, od, of, gt, sc, sh, g.reshape(1, D_MODEL), wo_bf, wr, br)


def _expert_kernel(layer, be_ref, cnt_ref, first_ref, slot_ref, next_ref, last_ref, xs_ref, wg_hbm, wu_hbm,
                   wd_hbm, ys_ref, wg_sc, wu_sc, wd_sc, wg_f32, wu_f32, wd_f32, sem):
    i = pl.program_id(0)
    cnt = cnt_ref[i]

    def weight_copies(expert, slot):
        return [pltpu.make_async_copy(hbm.at[layer, expert], buf.at[slot], sem.at[slot, n])
                for n, (hbm, buf) in enumerate(((wg_hbm, wg_f32), (wu_hbm, wu_f32), (wd_hbm, wd_f32)))]

    @pl.when(first_ref[i] == 1)
    def _():
        slot = slot_ref[i]

        @pl.when(i == 0)
        def _():
            for copy in weight_copies(be_ref[0], 0):
                copy.start()

        for copy in weight_copies(be_ref[i], slot):
            copy.wait()
        wg_sc[...] = wg_f32[slot].astype(BF16)
        wu_sc[...] = wu_f32[slot].astype(BF16)
        wd_sc[...] = wd_f32[slot].astype(BF16)

        @pl.when(next_ref[i] >= 0)
        def _():
            for copy in weight_copies(next_ref[i], 1 - slot):
                copy.start()

    def mlp(n_rows):
        rows = slice(0, n_rows)
        live = lax.broadcasted_iota(jnp.int32, (n_rows, LANES), 0) < cnt
        xb = _unpack_planes([jnp.where(live, xs_ref[p, rows, :], 0) for p in range(N_PLANES)]).astype(BF16)
        a = jnp.dot(xb, wg_sc[...], preferred_element_type=F32)
        u = jnp.dot(xb, wu_sc[...], preferred_element_type=F32)
        hid = (a / (1.0 + jnp.exp(-a)) * u).astype(BF16)
        _pack_planes(jnp.dot(hid, wd_sc[...], preferred_element_type=F32), ys_ref, rows)
        if n_rows < MOE_BLOCK:
            ys_ref[:, n_rows:, :] = jnp.zeros((N_PLANES, MOE_BLOCK - n_rows, LANES), ys_ref.dtype)

    half = MOE_BLOCK // 2

    @pl.when(cnt > half)
    def _():
        mlp(MOE_BLOCK)

    @pl.when((cnt > 0) & (cnt <= half))
    def _():
        mlp(half)


def _experts(layer, block_expert, block_count, xs, wg, wu, wd):
    idx = jnp.arange(MOE_NBLOCKS, dtype=jnp.int32)
    first = jnp.concatenate([jnp.ones((1,), jnp.bool_), block_expert[1:] != block_expert[:-1]])
    slot = (jnp.cumsum(first.astype(jnp.int32)) - 1) % 2
    later_first = (idx[None, :] > idx[:, None]) & first[None, :]
    nxt = jnp.min(jnp.where(later_first, block_expert[None, :], N_EXPERTS), axis=1)
    nxt = jnp.where(nxt == N_EXPERTS, -1, nxt).astype(jnp.int32)

    last_used = jnp.maximum(jnp.sum((block_count > 0).astype(jnp.int32)) - 1, 0).reshape(1)
    planes = pl.BlockSpec((N_PLANES, MOE_BLOCK, LANES),
                          lambda i, be, bc, fi, sl, nx, lu: (0, jnp.minimum(i, lu[0]), 0))
    hbm = pl.BlockSpec(memory_space=pl.ANY)
    grid_spec = pltpu.PrefetchScalarGridSpec(
        num_scalar_prefetch=6,
        grid=(MOE_NBLOCKS,),
        in_specs=[planes, hbm, hbm, hbm],
        out_specs=planes,
        scratch_shapes=[pltpu.VMEM((D_MODEL, D_EXPERT), BF16), pltpu.VMEM((D_MODEL, D_EXPERT), BF16),
                        pltpu.VMEM((D_EXPERT, D_MODEL), BF16),
                        pltpu.VMEM((2, D_MODEL, D_EXPERT), F32), pltpu.VMEM((2, D_MODEL, D_EXPERT), F32),
                        pltpu.VMEM((2, D_EXPERT, D_MODEL), F32), pltpu.SemaphoreType.DMA((2, 3))],
    )
    return pl.pallas_call(
        functools.partial(_expert_kernel, layer),
        grid_spec=grid_spec,
        out_shape=jax.ShapeDtypeStruct((N_PLANES, PLANE_ROWS, LANES), jnp.int32),
        compiler_params=_params("arbitrary"),
        name="expert_mlp",
    )(block_expert, block_count, first.astype(jnp.int32), slot.astype(jnp.int32), nxt, last_used,
      xs, wg, wu, wd)


def _slots(route_t, counts):
    counts = counts[0, :N_EXPERTS].astype(jnp.int32)
    padded = ((counts + MOE_BLOCK - 1) // MOE_BLOCK) * MOE_BLOCK
    pend = jnp.cumsum(padded)
    pstart = pend - padded
    bstart = jnp.arange(MOE_NBLOCKS, dtype=jnp.int32) * MOE_BLOCK
    block_expert = jnp.minimum(jnp.sum(bstart[:, None] >= pend[None, :], axis=1), N_EXPERTS - 1)
    block_expert = block_expert.astype(jnp.int32)
    mine = block_expert[:, None] == jnp.arange(N_EXPERTS, dtype=jnp.int32)[None, :]
    left = jnp.sum(jnp.where(mine, counts + pstart, 0), axis=1) - bstart
    block_count = jnp.clip(left, 0, MOE_BLOCK).astype(jnp.int32)
    return _slot_rows(route_t, pstart.astype(jnp.int32)), block_expert, block_count


def _slot_rows_kernel(base_ref, rt_ref, o_ref):
    n_tok = rt_ref.shape[1]
    for k in range(TOP_K):
        e = rt_ref[k:k + 1, :]
        b = jnp.zeros((1, n_tok), F32)
        for j in range(N_EXPERTS):
            b = jnp.where(e == float(j), base_ref[j].astype(F32), b)
        dest = (b + rt_ref[2 * TOP_K + k:2 * TOP_K + k + 1, :]).astype(jnp.int32)
        for p in range(N_PLANES):
            for c in range(n_tok // LANES):
                o_ref[k * N_PLANES + p, c:c + 1, :] = dest[:, c * LANES:(c + 1) * LANES] + p * PLANE_ROWS


def _slot_rows(route_t, base):
    tm = 16 * LANES
    grid_spec = pltpu.PrefetchScalarGridSpec(
        num_scalar_prefetch=1,
        grid=(N_TOK // tm,),
        in_specs=[pl.BlockSpec((8, tm), lambda i, base: (0, i))],
        out_specs=pl.BlockSpec((TOP_K * N_PLANES, tm // LANES, LANES), lambda i, base: (0, i, 0)),
    )
    return pl.pallas_call(
        _slot_rows_kernel,
        grid_spec=grid_spec,
        out_shape=jax.ShapeDtypeStruct((TOP_K * N_PLANES, N_TOK // LANES, LANES), jnp.int32),
        compiler_params=_params("arbitrary"),
        name="slot_rows",
    )(base, route_t)


def _sc_workers():
    info = plsc.get_sparse_core_info()
    return info.num_cores, info.num_cores * info.num_subcores


def _sc_scatter2(src, idx, out_rows):
    n_win = src.shape[0] // SC_WINDOW
    nc, nw = _sc_workers()
    steps = n_win // nw
    mesh = plsc.VectorSubcoreMesh(core_axis_name="c", subcore_axis_name="s")

    @functools.partial(
        pl.kernel, mesh=mesh,
        out_type=jax.ShapeDtypeStruct((out_rows, LANES), src.dtype),
        scratch_types=[pltpu.VMEM((2 * steps, SC_WINDOW), jnp.int32),
                       pltpu.VMEM((SC_INFLIGHT, SC_WINDOW, LANES), src.dtype),
                       pltpu.SemaphoreType.DMA((SC_INFLIGHT,)), pltpu.SemaphoreType.DMA((SC_INFLIGHT,))],
        name="sc_dispatch_scatter",
    )
    def k(src_hbm, idx_hbm, out_hbm, idx_v, rows_v, lsem, wsem):
        first = (lax.axis_index("s") * nc + lax.axis_index("c")) * steps
        pltpu.sync_copy(idx_hbm.at[pl.ds(first, steps)], idx_v.at[pl.ds(0, steps)])
        pltpu.sync_copy(idx_hbm.at[pl.ds(n_win + first, steps)], idx_v.at[pl.ds(steps, steps)])

        @pl.loop(0, steps, step=SC_INFLIGHT)
        def _(j):
            loads = [pltpu.async_copy(src_hbm.at[pl.ds((first + j + b) * SC_WINDOW, SC_WINDOW)],
                                      rows_v.at[b], lsem.at[b]) for b in range(SC_INFLIGHT)]
            writes = []
            for b in range(SC_INFLIGHT):
                loads[b].wait()
                for half in range(TOP_K):
                    dst = out_hbm.at[idx_v.at[half * steps + j + b]]
                    writes.append(pltpu.async_copy(rows_v.at[b], dst, wsem.at[b]))
            for w in writes:
                w.wait()

    return k(src, idx)


def _sc_gather(table, idx):
    n_out = idx.shape[0] * SC_WINDOW
    nc, nw = _sc_workers()
    steps = n_out // nw // SC_WINDOW
    mesh = plsc.VectorSubcoreMesh(core_axis_name="c", subcore_axis_name="s")

    @functools.partial(
        pl.kernel, mesh=mesh,
        out_type=jax.ShapeDtypeStruct((n_out, LANES), table.dtype),
        scratch_types=[pltpu.VMEM((steps, SC_WINDOW), jnp.int32),
                       pltpu.VMEM((SC_INFLIGHT, SC_WINDOW, LANES), table.dtype),
                       pltpu.SemaphoreType.DMA((SC_INFLIGHT,)), pltpu.SemaphoreType.DMA((SC_INFLIGHT,))],
        name="sc_combine_gather",
    )
    def k(table_hbm, idx_hbm, out_hbm, idx_v, rows_v, gsem, wsem):
        first = (lax.axis_index("s") * nc + lax.axis_index("c")) * steps
        pltpu.sync_copy(idx_hbm.at[pl.ds(first, steps)], idx_v)

        @pl.loop(0, steps, step=SC_INFLIGHT)
        def _(j):
            gathers = [pltpu.async_copy(table_hbm.at[idx_v.at[j + b]], rows_v.at[b], gsem.at[b])
                       for b in range(SC_INFLIGHT)]
            writes = []
            for b in range(SC_INFLIGHT):
                gathers[b].wait()
                dst = out_hbm.at[pl.ds((first + j + b) * SC_WINDOW, SC_WINDOW)]
                writes.append(pltpu.async_copy(rows_v.at[b], dst, wsem.at[b]))
            for w in writes:
                w.wait()

    return k(table, idx)


def _final_kernel(x_ref, rt_ref, y_ref, gt_ref, g_ref, o_ref):
    x = x_ref[...] + gt_ref[...] * _combine(rt_ref, y_ref)
    ms = jnp.mean(x * x, axis=-1, keepdims=True)
    o_ref[...] = x * lax.rsqrt(ms + EPS) * g_ref[...]


def _final(x, moe, gate, g):
    tpb = SEQ // TM
    row = pl.BlockSpec((TM, D_MODEL), lambda i: (i, 0))
    return pl.pallas_call(
        _final_kernel,
        grid=(N_TOK // TM,),
        in_specs=[row, pl.BlockSpec((TM, LANES), lambda i: (i, 0)),
                  pl.BlockSpec((TOP_K, N_PLANES, TM, LANES), lambda i: (0, 0, i, 0)),
                  pl.BlockSpec((None, 1, D_MODEL), lambda i: (i // tpb, 0, 0)),
                  pl.BlockSpec((1, D_MODEL), lambda i: (0, 0))],
        out_specs=row,
        out_shape=jax.ShapeDtypeStruct((N_TOK, D_MODEL), F32),
        compiler_params=_params("arbitrary"),
        name="final_norm",
    )(x, moe[0], moe[1], gate, g.reshape(1, D_MODEL))


def kernel(x, c, positions, w_ada, b_ada, g_mix, w_in, b_forget, lambda_q1, lambda_k1, lambda_q2,
           lambda_k2, g_subln, g_fox_out, w_out, g_ffn, w_router_group, b_router_group,
           w_router_expert, b_router_expert, w_expert_gate, w_expert_up, w_expert_down, g_final):
    mod = _modulation(c, w_ada, b_ada)
    mod = mod.reshape(DEPTH, BATCH, 6, 1, D_MODEL)
    tables = _rope_tables(positions)
    pq = _forget_placement()
    w_in_bf = jnp.pad(w_in.astype(BF16), ((0, 0), (0, 0), (0, IN_COLS_PAD - IN_COLS)))
    xf = x.reshape(N_TOK, D_MODEL)
    moe = None
    gate = None
    for l in range(DEPTH):
        sh1, sc1, gt1, sh2, sc2, gt2 = (mod[l, :, j] for j in range(6))
        w_bf = (l, w_in_bf)
        bfp =jnp.pad(b_forget[l], (0, LANES - N_FOX_HEADS)).reshape(1, LANES)
        xf, (dq, dk, dv, fq, fk, fv) = _inproj(xf, moe, gate, sc1, sh1, g_mix[l], w_bf, bfp, tables, pq)

        lambda_init = 0.8 - 0.6 * float(np.exp(-0.3 * l))
        lamv = jnp.zeros((8, LANES), F32).at[0:4, 0:HEAD_DIM].set(
            jnp.stack([lambda_q1[l], lambda_k1[l], lambda_q2[l], lambda_k2[l]]))
        g_d = g_subln[l].reshape(1, LANES)
        g_f = jnp.concatenate([g_fox_out[l], g_fox_out[l]]).reshape(1, LANES)
        od = _attention(True, lambda_init, dq, dk, dv, g_d, lamv)
        of = _attention(False, lambda_init, fq, fk, fv, g_f, lamv)

        wr32 = jnp.pad(jnp.concatenate([w_router_group[l], w_router_expert[l]], axis=1),
                       ((0, 0), (0, LANES - N_GROUPS - N_EXPERTS)))
        wr_hi = wr32.astype(BF16)
        wr_lo = (wr32 - wr_hi.astype(F32)).astype(BF16)
        wr = jnp.concatenate([wr_hi, wr_lo], axis=1)
        br = jnp.pad(jnp.concatenate([b_router_group[l], b_router_expert[l]]),
                     (0, LANES - N_GROUPS - N_EXPERTS)).reshape(1, LANES)
        xf, h2, route, route_t, counts = _outproj(xf, od, of, gt1, sc2, sh2, g_ffn[l],
                                                  w_out[l].astype(BF16), wr, br)

        rows, block_expert, block_count = _slots(route_t, counts)
        rows = rows.reshape(TOP_K * N_PLANES * N_TOK // SC_WINDOW, SC_WINDOW)
        xs = _sc_scatter2(h2.reshape(N_PLANES * N_TOK, LANES), rows, N_PLANES * PLANE_ROWS)
        ys = _experts(l, block_expert, block_count, xs.reshape(N_PLANES, PLANE_ROWS, LANES),
                      w_expert_gate, w_expert_up, w_expert_down)
        y2 = _sc_gather(ys.reshape(N_PLANES * PLANE_ROWS, LANES), rows)
        moe = (route, y2.reshape(TOP_K, N_PLANES, N_TOK, LANES))
        gate = gt2
    out = _final(xf, moe, gate, g_final)
    return out.reshape(BATCH, SEQ, D_MODEL)
```

```python
import functools

import numpy as np
import jax
import jax.numpy as jnp
from jax import lax
from jax.experimental import pallas as pl
from jax.experimental.pallas import tpu as pltpu
from jax.experimental.pallas import tpu_sc as plsc

D_MODEL = 1024
BATCH = 4
SEQ = 4096
DEPTH = 4
N_TOK = BATCH * SEQ

CHUNK = 64
HEAD_DIM = 64
N_DIFF_HEADS = 4
N_FOX_HEADS = 8
DIFF_WIDTH = 512
FOX_WIDTH = 512
IN_COLS = 3 * DIFF_WIDTH + 3 * FOX_WIDTH + N_FOX_HEADS
ROT_DIM = 16
ROPE_THETA = 500000.0
N_GROUPS = 4
EXPERTS_PER_GROUP = 8
N_EXPERTS = 32
TOP_K = 2
D_EXPERT = 512
EPS = 1e-6

LANES = 128
IN_COLS_PAD = 3200
FF_COL = 3 * DIFF_WIDTH + 3 * FOX_WIDTH
QK_WIDTH = 8 * LANES
TM = 512
OUTPROJ_TILES = 2
TQ = 512
ATTN_MAPS = 4
ATTN_TQ = 256
N_LATE_CHAINS = 1
ONES_ROWS = 16
ATTN_ROWS = 128
LOG2E = 1.4426950408889634
MOE_BLOCK = 512
MOE_ROWS = N_TOK * TOP_K + N_EXPERTS * MOE_BLOCK
MOE_NBLOCKS = MOE_ROWS // MOE_BLOCK
PLANE_ROWS = MOE_ROWS
N_PLANES = D_MODEL // 2 // LANES
SC_WINDOW = 128
SC_INFLIGHT = 4
NEG = -1e30
VMEM_LIMIT = 56 * 1024 * 1024

F32 = jnp.float32
BF16 = jnp.bfloat16


def _bf16_round(x):
    return x.astype(BF16).astype(F32)


def _lane_iota(shape):
    return lax.broadcasted_iota(jnp.int32, shape, 1)


def _params(*sem):
    return pltpu.CompilerParams(dimension_semantics=sem, vmem_limit_bytes=VMEM_LIMIT)


def _pack_planes(y, o_ref, rows=slice(None)):
    bits = lax.bitcast_convert_type(_bf16_round(y), jnp.uint32)
    half = D_MODEL // 2
    word = bits[:, half:] | lax.shift_right_logical(bits[:, :half], jnp.uint32(16))
    word = lax.bitcast_convert_type(word, jnp.int32)
    for p in range(N_PLANES):
        o_ref[p, rows, :] = word[:, p * LANES:(p + 1) * LANES]


def _unpack_planes(planes):
    lo, hi = [], []
    for w in planes:
        u = lax.bitcast_convert_type(w, jnp.uint32)
        lo.append(lax.bitcast_convert_type(lax.shift_left(u, jnp.uint32(16)), F32))
        hi.append(lax.bitcast_convert_type(u & jnp.uint32(0xFFFF0000), F32))
    return jnp.concatenate(lo + hi, axis=1)


def _combine(route_ref, y_ref):
    rt = route_ref[...]
    y0 = _unpack_planes([y_ref[0, p] for p in range(N_PLANES)])
    y1 = _unpack_planes([y_ref[1, p] for p in range(N_PLANES)])
    return rt[:, 2:3] * y0 + rt[:, 3:4] * y1


def _mod_kernel(c_ref, w_ref, b_ref, o_ref):
    c = c_ref[...]
    cond = c / (1.0 + jnp.exp(-c))
    ch = cond.astype(BF16)
    cl = (cond - ch.astype(F32)).astype(BF16)
    w = w_ref[...]
    wh = w.astype(BF16)
    wl = (w - wh.astype(F32)).astype(BF16)
    acc = jnp.dot(ch, wh, preferred_element_type=F32)
    acc += jnp.dot(cl, wh, preferred_element_type=F32)
    acc += jnp.dot(ch, wl, preferred_element_type=F32)
    o_ref[...] = acc + b_ref[...]


def _modulation(c, w_ada, b_ada):
    rows = 16
    tn = 1536
    c_pad = jnp.zeros((rows, D_MODEL), F32).at[:BATCH].set(c)
    out = pl.pallas_call(
        _mod_kernel,
        grid=(DEPTH, 6 * D_MODEL // tn),
        in_specs=[
            pl.BlockSpec((rows, D_MODEL), lambda l, n: (0, 0)),
            pl.BlockSpec((None, D_MODEL, tn), lambda l, n: (l, 0, n)),
            pl.BlockSpec((None, 1, tn), lambda l, n: (l, 0, n)),
        ],
        out_specs=pl.BlockSpec((None, rows, tn), lambda l, n: (l, 0, n)),
        out_shape=jax.ShapeDtypeStruct((DEPTH, rows, 6 * D_MODEL), F32),
        compiler_params=_params("arbitrary", "arbitrary"),
        name="adaln_mod",
    )(c_pad, w_ada, b_ada.reshape(DEPTH, 1, 6 * D_MODEL))
    return out[:, :BATCH]


def _rope_kernel(pos_ref, inv_ref, c_ref, sa_ref, sb_ref):
    ang = pos_ref[...].astype(F32) * inv_ref[...]
    j = _lane_iota(ang.shape) % HEAD_DIM
    cosv = jnp.cos(ang)
    sinv = jnp.sin(ang)
    half = ROT_DIM // 2
    c_ref[...] = jnp.where(j < ROT_DIM, cosv, 1.0)
    sa_ref[...] = jnp.where(j < half, -sinv, 0.0)
    sb_ref[...] = jnp.where((j >= half) & (j < ROT_DIM), sinv, 0.0)


def _rope_tables(positions):
    half = ROT_DIM // 2
    inv = ROPE_THETA ** (-jnp.arange(0, ROT_DIM, 2, dtype=F32) / ROT_DIM)
    lane = np.arange(LANES)
    inv_lane = inv[(lane % HEAD_DIM) % half].reshape(1, LANES)
    spec = pl.BlockSpec((TM, LANES), lambda i: (i, 0))
    shape = jax.ShapeDtypeStruct((N_TOK, LANES), F32)
    return pl.pallas_call(
        _rope_kernel,
        grid=(N_TOK // TM,),
        in_specs=[pl.BlockSpec((TM, 1), lambda i: (i, 0)),
                  pl.BlockSpec((1, LANES), lambda i: (0, 0))],
        out_specs=[spec, spec, spec],
        out_shape=[shape, shape, shape],
        compiler_params=_params("arbitrary"),
        name="rope_tables",
    )(positions.reshape(N_TOK, 1), inv_lane)


def _rms_mod(x, g, sc, sh):
    ms = jnp.mean(x * x, axis=-1, keepdims=True)
    return (x * lax.rsqrt(ms + EPS) * g) * (1.0 + sc) + sh


def _inproj_kernel(fuse, *refs):
    if fuse:
        (x_ref, rt_ref, y_ref, gt_ref, sc_ref, sh_ref, g_ref, w_ref, bf_ref, c_ref, sa_ref, sb_ref,
         pq_ref, xo_ref, dq_ref, dk_ref, dv_ref, fq_ref, fk_ref, fv_ref, carry_ref, h_even, h_odd) = refs
    else:
        (x_ref, sc_ref, sh_ref, g_ref, w_ref, bf_ref, c_ref, sa_ref, sb_ref,
         pq_ref, dq_ref, dk_ref, dv_ref, fq_ref, fk_ref, fv_ref, carry_ref, h_even, h_odd) = refs
    step = pl.program_id(0)

    def normalise(h_ref):
        if fuse:
            x = x_ref[...] + gt_ref[...] * _combine(rt_ref, y_ref)
            xo_ref[...] = x
        else:
            x = x_ref[...]
        h_ref[...] = _rms_mod(x, g_ref[...], sc_ref[...], sh_ref[...]).astype(BF16)

    @pl.when(step == 0)
    def _():
        h_odd[...] = jnp.zeros_like(h_odd)

    @pl.when((step == 0) | ((step - 1) % (SEQ // TM) == 0))
    def _():
        carry_ref[...] = jnp.zeros_like(carry_ref)

    for parity, (h_new, h_old) in enumerate(((h_even, h_odd), (h_odd, h_even))):
        @pl.when(step % 2 == parity)
        def _():
            _inproj_project(h_old[...], w_ref, bf_ref, c_ref, sa_ref, sb_ref, pq_ref, dq_ref, dk_ref,
                            dv_ref, fq_ref, fk_ref, fv_ref, carry_ref)
            normalise(h_new)


def _inproj_project(hb, w_ref, bf_ref, c_ref, sa_ref, sb_ref, pq_ref, dq_ref, dk_ref, dv_ref, fq_ref,
                    fk_ref, fv_ref, carry_ref):
    lane = _lane_iota((TM, LANES))
    nh = N_FOX_HEADS

    def pack3(a):
        hi = _bf16_round(a)
        r1 = a - hi
        mid = _bf16_round(r1)
        lo = _bf16_round(r1 - mid)
        return jnp.where(lane < nh, hi,
                         jnp.where(lane < 2 * nh, pltpu.roll(mid, nh, 1),
                                   jnp.where(lane < 3 * nh, pltpu.roll(lo, 2 * nh, 1), 0.0)))

    z = jnp.dot(hb, w_ref[:, FF_COL:FF_COL + LANES], preferred_element_type=F32) + bf_ref[...]

    low = lane < HEAD_DIM
    rc, rsa, rsb = c_ref[...], sa_ref[...], sb_ref[...]
    scale = HEAD_DIM ** -0.5 * LOG2E

    def split_store(chunk, o_ref, m, extra_a=None, extra_b=None):
        a = jnp.where(low, chunk, 0.0)
        b = jnp.where(low, pltpu.roll(chunk, HEAD_DIM, 1), 0.0)
        if extra_a is not None:
            a = a + extra_a
            b = b + extra_b
        o_ref[:, (2 * m) * LANES:(2 * m + 1) * LANES] = a.astype(BF16)
        o_ref[:, (2 * m + 1) * LANES:(2 * m + 2) * LANES] = b.astype(BF16)

    def rope(xc):
        return xc * rc + pltpu.roll(xc, LANES - ROT_DIM // 2, 1) * rsa + pltpu.roll(xc, ROT_DIM // 2, 1) * rsb

    pdq = jnp.dot(hb, w_ref[:, 0:DIFF_WIDTH], preferred_element_type=F32)
    for m in range(N_DIFF_HEADS):
        split_store(rope(pdq[:, m * LANES:(m + 1) * LANES]) * scale, dq_ref, m)
    pdk = jnp.dot(hb, w_ref[:, DIFF_WIDTH:2 * DIFF_WIDTH], preferred_element_type=F32)
    for m in range(N_DIFF_HEADS):
        split_store(rope(pdk[:, m * LANES:(m + 1) * LANES]), dk_ref, m)
    def store_values_t(pv, o_ref, width):
        ones = jnp.ones((ONES_ROWS, TM), BF16)
        for m in range(4):
            vt = pv[:, m * LANES:(m + 1) * LANES].T.astype(BF16)
            for i in range(LANES // width):
                o_ref[m * (LANES // width) + i, 0:width, :] = vt[i * width:(i + 1) * width]
                o_ref[m * (LANES // width) + i, width:width + ONES_ROWS, :] = ones

    store_values_t(jnp.dot(hb, w_ref[:, 2 * DIFF_WIDTH:3 * DIFF_WIDTH], preferred_element_type=F32),
                   dv_ref, 2 * HEAD_DIM)
    o = 3 * DIFF_WIDTH
    store_values_t(jnp.dot(hb, w_ref[:, o + 2 * FOX_WIDTH:o + 3 * FOX_WIDTH],
                           preferred_element_type=F32), fv_ref, HEAD_DIM)

    logf =jnp.minimum(z, 0.0) - jnp.log(1.0 + jnp.exp(-jnp.abs(z)))
    logf = jnp.where(lane < nh, logf, 0.0)
    row = lax.broadcasted_iota(jnp.int32, (TM, TM), 0)
    col = lax.broadcasted_iota(jnp.int32, (TM, TM), 1)
    tri = (row >= col).astype(BF16)
    r = jnp.dot(tri, pack3(logf).astype(BF16), preferred_element_type=F32)
    cs = r + pltpu.roll(r, LANES - nh, 1) + pltpu.roll(r, LANES - 2 * nh, 1)
    cf = jnp.where(lane < nh, cs + carry_ref[0:1, :], 0.0)
    carry_ref[...] = jnp.broadcast_to(cf[TM - 1:TM, :], carry_ref.shape)

    t3 = jnp.where(lane == 3 * nh, 1.0, pack3(cf * LOG2E)).astype(BF16)
    aug = jnp.dot(t3, pq_ref[...], preferred_element_type=F32)

    pfq =jnp.dot(hb, w_ref[:, o:o + FOX_WIDTH], preferred_element_type=F32)
    for m in range(N_FOX_HEADS // 2):
        split_store(pfq[:, m * LANES:(m + 1) * LANES] * scale, fq_ref, m,
                    aug[:, (2 * m) * LANES:(2 * m + 1) * LANES],
                    aug[:, (2 * m + 1) * LANES:(2 * m + 2) * LANES])
    pfk = jnp.dot(hb, w_ref[:, o + FOX_WIDTH:o + 2 * FOX_WIDTH], preferred_element_type=F32)
    for m in range(N_FOX_HEADS // 2):
        split_store(pfk[:, m * LANES:(m + 1) * LANES], fk_ref, m,
                    aug[:, QK_WIDTH + (2 * m) * LANES:QK_WIDTH + (2 * m + 1) * LANES],
                    aug[:, QK_WIDTH + (2 * m + 1) * LANES:QK_WIDTH + (2 * m + 2) * LANES])


def _forget_placement():
    nh = N_FOX_HEADS
    p = np.zeros((LANES, 2 * QK_WIDTH), np.float32)
    for h in range(nh):
        base_q = h * LANES + HEAD_DIM
        base_k = QK_WIDTH + h * LANES + HEAD_DIM
        for part in range(3):
            p[part * nh + h, base_q + part] = 1.0
            p[3 * nh, base_q + 3 + part] = 1.0
            p[3 * nh, base_k + part] = 1.0
            p[part * nh + h, base_k + 3 + part] = -1.0
    return jnp.asarray(p, BF16)


def _inproj(x, moe, gate, sc, sh, g, w_bf, b_forget, tables, pq):
    fuse = moe is not None
    tpb = SEQ // TM
    n_tiles = N_TOK // TM
    new = lambda s: jnp.minimum(s, n_tiles - 1)
    old = lambda s: jnp.maximum(s - 1, 0)
    row = pl.BlockSpec((TM, D_MODEL), lambda s: (new(s), 0))
    per_batch = pl.BlockSpec((None, 1, D_MODEL), lambda s: (new(s) // tpb, 0, 0))
    const = lambda shape: pl.BlockSpec(shape, lambda s: (0,) * len(shape))
    tab = pl.BlockSpec((TM, LANES), lambda s: (old(s), 0))
    in_specs = [row]
    args = [x]
    if fuse:
        in_specs += [pl.BlockSpec((TM, LANES), lambda s: (new(s), 0)),
                     pl.BlockSpec((TOP_K, N_PLANES, TM, LANES), lambda s: (0, 0, new(s), 0)), per_batch]
        args += [moe[0], moe[1], gate]
    layer, w_all = w_bf
    in_specs += [per_batch, per_batch, const((1, D_MODEL)),
                 pl.BlockSpec((None, D_MODEL, IN_COLS_PAD), lambda s: (layer, 0, 0)),
                 const((1, LANES)), tab, tab, tab, const((LANES, 2 * QK_WIDTH))]
    args += [sc, sh, g.reshape(1, D_MODEL), w_all, b_forget, *tables, pq]
    wide = pl.BlockSpec((TM, QK_WIDTH), lambda s: (old(s), 0))
    def vspec(heads, width):
        rows = width + ONES_ROWS
        return (pl.BlockSpec((None, heads, None, rows, TM),
                             lambda s: (old(s) // tpb, 0, old(s) % tpb, 0, 0)),
                jax.ShapeDtypeStruct((BATCH, heads, tpb, rows, TM), BF16))

    wide_s = jax.ShapeDtypeStruct((N_TOK, QK_WIDTH), BF16)
    dv_spec, dv_s = vspec(N_DIFF_HEADS, 2 * HEAD_DIM)
    fv_spec, fv_s = vspec(N_FOX_HEADS, HEAD_DIM)
    out_specs = [wide, wide, dv_spec, wide, wide, fv_spec]
    out_shape = [wide_s, wide_s, dv_s, wide_s, wide_s, fv_s]
    if fuse:
        out_specs = [row] + out_specs
        out_shape = [jax.ShapeDtypeStruct((N_TOK, D_MODEL), F32)] + out_shape
    outs = pl.pallas_call(
        functools.partial(_inproj_kernel, fuse),
        grid=(n_tiles + 1,),
        in_specs=in_specs,
        out_specs=out_specs,
        out_shape=out_shape,
        scratch_shapes=[pltpu.VMEM((8, LANES), F32), pltpu.VMEM((TM, D_MODEL), BF16),
                        pltpu.VMEM((TM, D_MODEL), BF16)],
        compiler_params=_params("arbitrary"),
        name="norm_inproj",
    )(*args)
    if fuse:
        return outs[0], outs[1:]
    return x, outs


def _attn_kernel(diff, lambda_init, *refs):
    q_refs, k_refs = refs[:ATTN_MAPS], refs[ATTN_MAPS:2 * ATTN_MAPS]
    v_ref, g_ref, lam_ref, o_ref = refs[2 * ATTN_MAPS:2 * ATTN_MAPS + 4]
    scratch = refs[2 * ATTN_MAPS + 4:]
    nq = SEQ // TQ
    n_half = TQ // ATTN_TQ
    feat = 2 * HEAD_DIM if diff else HEAD_DIM
    chains = []
    for mi, (q_ref, k_ref) in enumerate(zip(q_refs, k_refs)):
        for h in range(n_half):
            c = mi * n_half + h
            qt_sc, s_sc, p_sc, m_sc, a_sc, acc_sc = scratch[c::ATTN_MAPS * n_half]
            vh = mi // 2 if diff else mi
            chains.append((h, k_ref, qt_sc, s_sc, p_sc, m_sc, a_sc, acc_sc, q_ref, vh))
    order = [chains[mi * n_half + h] for h in range(n_half) for mi in range(ATTN_MAPS)]
    early, late = order[:-N_LATE_CHAINS], order[-N_LATE_CHAINS:]

    def load_queries(qi):
        for mi in range(ATTN_MAPS):
            q_ref = chains[mi * n_half][8]
            qt = q_ref[pl.ds(pl.multiple_of(qi * TQ, TQ), TQ), :].astype(F32).T.astype(BF16)
            for h in range(n_half):
                chains[mi * n_half + h][2][...] = qt[:, h * ATTN_TQ:(h + 1) * ATTN_TQ]

    def reset_state():
        for chain in chains:
            m_sc, _, acc_sc = chain[5:8]
            m_sc[...] = jnp.full(m_sc.shape, NEG, F32)
            acc_sc[...] = jnp.zeros(acc_sc.shape, F32)

    def n_keys(chain, masked):
        return (chain[0] + 1) * ATTN_TQ if masked else TQ

    def scores(chain, j, masked):
        h, k_ref, qt_sc, s_sc = chain[:4]
        nk = n_keys(chain, masked)
        off = pl.multiple_of(j * TQ, TQ)
        s = jnp.dot(k_ref[pl.ds(off, nk), :], qt_sc[...], preferred_element_type=F32)
        if masked:
            kk = lax.broadcasted_iota(jnp.int32, (nk, ATTN_TQ), 0)
            qq = h * ATTN_TQ + lax.broadcasted_iota(jnp.int32, (nk, ATTN_TQ), 1)
            s = jnp.where((kk // CHUNK <= qq // CHUNK) if diff else (kk <= qq), s, NEG)
        s_sc[0:nk, :] = s

    def softmax(chain, masked):
        s_sc, p_sc, m_sc, a_sc = chain[3:7]
        nk = n_keys(chain, masked)
        m_all = m_sc[...]
        m_parts = []
        for c0 in range(0, ATTN_TQ, LANES):
            cols = slice(c0, c0 + LANES)
            pm = s_sc[0:ATTN_ROWS, cols]
            for r0 in range(ATTN_ROWS, nk, ATTN_ROWS):
                pm = jnp.maximum(pm, s_sc[r0:r0 + ATTN_ROWS, cols])
            m_new = jnp.maximum(m_all[:, cols], jnp.max(pm, axis=0, keepdims=True))
            for r0 in range(0, nk, ATTN_ROWS):
                p = jnp.exp2(s_sc[r0:r0 + ATTN_ROWS, cols] - m_new)
                p_sc[r0:r0 + ATTN_ROWS, cols] = p.astype(BF16)
            m_parts.append(m_new)
        m_new = jnp.concatenate(m_parts, axis=1)
        a_sc[...] = jnp.exp2(m_all - m_new)
        m_sc[...] = m_new

    def values(chain, j, masked=False):
        p_sc, a_sc, acc_sc, vh = chain[4], chain[6], chain[7], chain[9]
        nk = n_keys(chain, masked)
        pv = jnp.dot(v_ref[vh, j, :, 0:nk], p_sc[0:nk, :], preferred_element_type=F32)
        acc_sc[...] = a_sc[...] * acc_sc[...] + pv

    def idle_late():
        for chain in late:
            chain[4][...] = jnp.zeros(chain[4].shape, BF16)
            chain[6][...] = jnp.ones(chain[6].shape, F32)

    def consume(j, cur_masked=False, nxt=None, nxt_masked=False, final=False, before_next=None):
        def open_late(chain):
            scores(chain, j, cur_masked)
            values(chain, jnp.maximum(j - 1, 0))

        open_late(late[0])
        for i, chain in enumerate(early):
            softmax(chain, cur_masked)
            if i == 0:
                for other in late[1:]:
                    open_late(other)
                if before_next is not None:
                    before_next()
            if nxt is not None:
                scores(chain, nxt, nxt_masked)
            values(chain, j, cur_masked)
        for chain in late:
            softmax(chain, cur_masked)
        if final:
            for chain in late:
                values(chain, j, cur_masked)

    def finalize(qi):
        g = g_ref[...]
        rows = pl.ds(pl.multiple_of(qi * TQ, TQ), TQ)
        for pair in range(ATTN_MAPS // 2):
            def normalised(chain):
                acc_sc = chain[7]
                return acc_sc[0:feat] * (1.0 / acc_sc[feat:feat + 1])

            ot = [jnp.concatenate([normalised(chains[mi * n_half + h]) for h in range(n_half)], axis=1)
                  for mi in (2 * pair, 2 * pair + 1)]
            cols = slice(pair * LANES, (pair + 1) * LANES)
            if diff:
                lv = lam_ref[...]
                lam = (jnp.exp(jnp.sum(lv[0:1] * lv[1:2], axis=1, keepdims=True))
                       - jnp.exp(jnp.sum(lv[2:3] * lv[3:4], axis=1, keepdims=True)) + lambda_init)
                o = (ot[0] - lam * ot[1]).T
                y = o * lax.rsqrt(jnp.mean(o * o, axis=1, keepdims=True) + EPS) * g
                o_ref[rows, cols] = (y * (1.0 - lambda_init)).astype(o_ref.dtype)
            else:
                o = jnp.concatenate(ot, axis=0).T
                low = _lane_iota((TQ, LANES)) < HEAD_DIM
                sq = o * o
                msa = jnp.sum(jnp.where(low, sq, 0.0), axis=1, keepdims=True) / HEAD_DIM
                msb = jnp.sum(jnp.where(low, 0.0, sq), axis=1, keepdims=True) / HEAD_DIM
                inv = jnp.where(low, lax.rsqrt(msa + EPS), lax.rsqrt(msb + EPS))
                o_ref[rows, cols] = (o * inv * g).astype(o_ref.dtype)

    load_queries(0)
    reset_state()
    idle_late()
    for chain in early:
        scores(chain, 0, True)

    @pl.loop(0, nq)
    def _(qi):
        n_plain = jnp.maximum(qi - 1, 0)

        def run(first, count):
            for i in range(count):
                consume(first + i, nxt=first + i + 1)

        @pl.loop(0, n_plain // 4)
        def _(t):
            run(4 * t, 4)

        done = (n_plain // 4) * 4

        @pl.when(n_plain - done >= 2)
        def _():
            run(done, 2)

        def last_blocks(to_next_tile):
            if to_next_tile:
                consume(qi, cur_masked=True, nxt=0, final=True, before_next=lambda: load_queries(qi + 1))
            else:
                consume(qi, cur_masked=True, final=True)
            finalize(qi)

        for to_next_tile in (True, False):
            more = (qi < nq - 1) if to_next_tile else (qi == nq - 1)

            last_tile_odd = (nq - 2) % 2 == 1
            for odd in ((True, False) if to_next_tile else (last_tile_odd,)):
                @pl.when(more & (qi > 0) & ((n_plain % 2 == 1) == odd))
                def _():
                    if odd:
                        run(qi - 2, 1)
                    consume(qi - 1, nxt=qi, nxt_masked=True)
                    last_blocks(to_next_tile)

            if to_next_tile:
                @pl.when(qi == 0)
                def _():
                    last_blocks(to_next_tile)

        reset_state()
        idle_late()


def _attention(diff, lambda_init, q, k, v, g, lamv):
    nq = SEQ // TQ
    n_steps = 8 // ATTN_MAPS
    v_heads = v.shape[1] // n_steps
    kspec = lambda m: pl.BlockSpec((SEQ, LANES), lambda b, p: (b, ATTN_MAPS * p + m))
    maps = [kspec(m) for m in range(ATTN_MAPS)]
    return pl.pallas_call(
        functools.partial(_attn_kernel, diff, lambda_init),
        grid=(BATCH, n_steps),
        in_specs=maps + maps + [
            pl.BlockSpec((None, v_heads, nq, v.shape[3], TQ), lambda b, p: (b, p, 0, 0, 0)),
            pl.BlockSpec((1, LANES), lambda b, p: (0, 0)),
            pl.BlockSpec((8, LANES), lambda b, p: (0, 0))],
        out_specs=pl.BlockSpec((SEQ, ATTN_MAPS // 2 * LANES), lambda b, p: (b, p)),
        out_shape=jax.ShapeDtypeStruct((N_TOK, DIFF_WIDTH), BF16),
        scratch_shapes=[pltpu.VMEM(shape, dt)
                        for shape, dt in (((LANES, ATTN_TQ), BF16), ((TQ, ATTN_TQ), F32),
                                          ((TQ, ATTN_TQ), BF16), ((1, ATTN_TQ), F32),
                                          ((1, ATTN_TQ), F32), ((v.shape[3], ATTN_TQ), F32))
                        for _ in range(ATTN_MAPS * TQ // ATTN_TQ)],
        compiler_params=_params("arbitrary", "arbitrary"),
        name="diff_attention" if diff else "fox_attention",
    )(*([q] * ATTN_MAPS), *([k] * ATTN_MAPS), v, g, lamv)


def _outproj_kernel(x_ref, od_ref, of_ref, gt_ref, sc_ref, sh_ref, g_ref, wo_ref, wr_ref, br_ref,
                    x1_ref, h2_ref, rt_ref, rtt_ref, cnt_ref, carry_ref):
    @pl.when(pl.program_id(0) == 0)
    def _():
        carry_ref[...] = jnp.zeros_like(carry_ref)

    tiles = [slice(t * TM, (t + 1) * TM) for t in range(OUTPROJ_TILES)]
    lane = _lane_iota((TM, LANES))
    lanef = lane.astype(F32)
    big = float(LANES)

    def project(rows):
        mix = jnp.dot(od_ref[rows, :], wo_ref[0:DIFF_WIDTH, :], preferred_element_type=F32)
        return mix + jnp.dot(of_ref[rows, :], wo_ref[DIFF_WIDTH:, :], preferred_element_type=F32)

    def normalise(rows, mix):
        x1 = x_ref[rows, :] + gt_ref[...] * mix
        x1_ref[rows, :] = x1
        h = _rms_mod(x1, g_ref[...], sc_ref[...], sh_ref[...])
        hh = h.astype(BF16)
        _pack_planes(h, h2_ref, rows)
        return hh, (h - hh.astype(F32)).astype(BF16)

    def router_logits(hh, hl):
        r1 = jnp.dot(hh, wr_ref[...], preferred_element_type=F32)
        r2 = jnp.dot(hl, wr_ref[:, 0:LANES], preferred_element_type=F32)
        return r1[:, 0:LANES] + r1[:, LANES:] + r2 + br_ref[...]

    def top_k(logits):
        isg = lane < N_GROUPS
        lg = jnp.where(isg, logits, NEG)
        mg = jnp.max(lg, axis=1, keepdims=True)
        sg = jnp.sum(jnp.where(isg, jnp.exp(lg - mg), 0.0), axis=1, keepdims=True)
        p_g = 1.0 / sg
        gsel = jnp.min(jnp.where(isg & (lg == mg), lanef, big), axis=1, keepdims=True)
        lo = N_GROUPS + gsel * EXPERTS_PER_GROUP
        ise = (lanef >= lo) & (lanef < lo + EXPERTS_PER_GROUP)
        le = jnp.where(ise, logits, NEG)
        t1 = jnp.max(le, axis=1, keepdims=True)
        i1 = jnp.min(jnp.where(ise & (le == t1), lanef, big), axis=1, keepdims=True)
        ise2 = ise & (lanef != i1)
        le2 = jnp.where(ise2, logits, NEG)
        t2 = jnp.max(le2, axis=1, keepdims=True)
        i2 = jnp.min(jnp.where(ise2 & (le2 == t2), lanef, big), axis=1, keepdims=True)
        d = jnp.exp(t2 - t1)
        return i1 - N_GROUPS, i2 - N_GROUPS, p_g / (1.0 + d), p_g * d / (1.0 + d)

    def earlier_in_tile(e1, e2):
        both = jnp.where((lanef == e1) | (lanef == e2), 1.0, 0.0)
        row = lax.broadcasted_iota(jnp.int32, (TM, TM), 0)
        col = lax.broadcasted_iota(jnp.int32, (TM, TM), 1)
        before = jnp.dot((row > col).astype(BF16), both.astype(BF16), preferred_element_type=F32)
        return before, jnp.sum(both, axis=0, keepdims=True)

    mixes = [project(rows) for rows in tiles]
    splits = [normalise(rows, mix) for rows, mix in zip(tiles, mixes)]
    logits = [router_logits(hh, hl) for hh, hl in splits]
    picks = [top_k(lg) for lg in logits]
    befores = [earlier_in_tile(e1, e2) for e1, e2, _, _ in picks]
    counts = carry_ref[0:1, :]
    for rows, (e1, e2, w1, w2), (before, added) in zip(tiles, picks, befores):
        before = before + counts
        rank1 = jnp.sum(jnp.where(lanef == e1, before, 0.0), axis=1, keepdims=True)
        rank2 = jnp.sum(jnp.where(lanef == e2, before, 0.0), axis=1, keepdims=True)
        out = jnp.zeros((TM, LANES), F32)
        for j, v in enumerate((e1, e2, w1, w2, rank1, rank2)):
            out = jnp.where(lane == j, v, out)
        rt_ref[rows, :] = out
        rtt_ref[:, rows] = out.T[0:8, :]
        counts = counts + added
    carry_ref[...] = jnp.broadcast_to(counts, carry_ref.shape)
    cnt_ref[...] = jnp.broadcast_to(counts, cnt_ref.shape)


def _outproj(x, od, of, gt, sc, sh, g, wo_bf, wr, br):
    tm = OUTPROJ_TILES * TM
    tpb = SEQ // tm
    row = pl.BlockSpec((tm, D_MODEL), lambda i: (i, 0))
    half = pl.BlockSpec((tm, DIFF_WIDTH), lambda i: (i, 0))
    per_batch = pl.BlockSpec((None, 1, D_MODEL), lambda i: (i // tpb, 0, 0))
    const = lambda shape: pl.BlockSpec(shape, lambda i: (0,) * len(shape))
    return pl.pallas_call(
        _outproj_kernel,
        grid=(N_TOK // tm,),
        in_specs=[row, half, half, per_batch, per_batch, per_batch, const((1, D_MODEL)),
                  const((D_MODEL, D_MODEL)), const((D_MODEL, 2 * LANES)), const((1, LANES))],
        out_specs=[row, pl.BlockSpec((N_PLANES, tm, LANES), lambda i: (0, i, 0)),
                   pl.BlockSpec((tm, LANES), lambda i: (i, 0)), pl.BlockSpec((8, tm), lambda i: (0, i)),
                   const((8, LANES))],
        out_shape=[jax.ShapeDtypeStruct((N_TOK, D_MODEL), F32),
                   jax.ShapeDtypeStruct((N_PLANES, N_TOK, LANES), jnp.int32),
                   jax.ShapeDtypeStruct((N_TOK, LANES), F32),
                   jax.ShapeDtypeStruct((8, N_TOK), F32),
                   jax.ShapeDtypeStruct((8, LANES), F32)],
        scratch_shapes=[pltpu.VMEM((8, LANES), F32)],
        compiler_params=_params("arbitrary"),
        name="outproj_router",
    )(x, od, of, gt, sc, sh, g.reshape(1, D_MODEL), wo_bf, wr, br)


def _expert_kernel(layer, be_ref, cnt_ref, first_ref, slot_ref, next_ref, last_ref, xs_ref, wg_hbm, wu_hbm,
                   wd_hbm, ys_ref, wg_sc, wu_sc, wd_sc, wg_f32, wu_f32, wd_f32, sem):
    i = pl.program_id(0)
    cnt = cnt_ref[i]

    def weight_copies(expert, slot):
        return [pltpu.make_async_copy(hbm.at[layer, expert], buf.at[slot], sem.at[slot, n])
                for n, (hbm, buf) in enumerate(((wg_hbm, wg_f32), (wu_hbm, wu_f32), (wd_hbm, wd_f32)))]

    @pl.when(first_ref[i] == 1)
    def _():
        slot = slot_ref[i]

        @pl.when(i == 0)
        def _():
            for copy in weight_copies(be_ref[0], 0):
                copy.start()

        for copy in weight_copies(be_ref[i], slot):
            copy.wait()
        wg_sc[...] = wg_f32[slot].astype(BF16)
        wu_sc[...] = wu_f32[slot].astype(BF16)
        wd_sc[...] = wd_f32[slot].astype(BF16)

        @pl.when(next_ref[i] >= 0)
        def _():
            for copy in weight_copies(next_ref[i], 1 - slot):
                copy.start()

    def mlp(n_rows):
        rows = slice(0, n_rows)
        live = lax.broadcasted_iota(jnp.int32, (n_rows, LANES), 0) < cnt
        xb = _unpack_planes([jnp.where(live, xs_ref[p, rows, :], 0) for p in range(N_PLANES)]).astype(BF16)
        a = jnp.dot(xb, wg_sc[...], preferred_element_type=F32)
        u = jnp.dot(xb, wu_sc[...], preferred_element_type=F32)
        hid = (a / (1.0 + jnp.exp(-a)) * u).astype(BF16)
        _pack_planes(jnp.dot(hid, wd_sc[...], preferred_element_type=F32), ys_ref, rows)
        if n_rows < MOE_BLOCK:
            ys_ref[:, n_rows:, :] = jnp.zeros((N_PLANES, MOE_BLOCK - n_rows, LANES), ys_ref.dtype)

    half = MOE_BLOCK // 2

    @pl.when(cnt > half)
    def _():
        mlp(MOE_BLOCK)

    @pl.when((cnt > 0) & (cnt <= half))
    def _():
        mlp(half)


def _experts(layer, block_expert, block_count, xs, wg, wu, wd):
    idx = jnp.arange(MOE_NBLOCKS, dtype=jnp.int32)
    first = jnp.concatenate([jnp.ones((1,), jnp.bool_), block_expert[1:] != block_expert[:-1]])
    slot = (jnp.cumsum(first.astype(jnp.int32)) - 1) % 2
    later_first = (idx[None, :] > idx[:, None]) & first[None, :]
    nxt = jnp.min(jnp.where(later_first, block_expert[None, :], N_EXPERTS), axis=1)
    nxt = jnp.where(nxt == N_EXPERTS, -1, nxt).astype(jnp.int32)

    last_used = jnp.maximum(jnp.sum((block_count > 0).astype(jnp.int32)) - 1, 0).reshape(1)
    planes = pl.BlockSpec((N_PLANES, MOE_BLOCK, LANES),
                          lambda i, be, bc, fi, sl, nx, lu: (0, jnp.minimum(i, lu[0]), 0))
    hbm = pl.BlockSpec(memory_space=pl.ANY)
    grid_spec = pltpu.PrefetchScalarGridSpec(
        num_scalar_prefetch=6,
        grid=(MOE_NBLOCKS,),
        in_specs=[planes, hbm, hbm, hbm],
        out_specs=planes,
        scratch_shapes=[pltpu.VMEM((D_MODEL, D_EXPERT), BF16), pltpu.VMEM((D_MODEL, D_EXPERT), BF16),
                        pltpu.VMEM((D_EXPERT, D_MODEL), BF16),
                        pltpu.VMEM((2, D_MODEL, D_EXPERT), F32), pltpu.VMEM((2, D_MODEL, D_EXPERT), F32),
                        pltpu.VMEM((2, D_EXPERT, D_MODEL), F32), pltpu.SemaphoreType.DMA((2, 3))],
    )
    return pl.pallas_call(
        functools.partial(_expert_kernel, layer),
        grid_spec=grid_spec,
        out_shape=jax.ShapeDtypeStruct((N_PLANES, PLANE_ROWS, LANES), jnp.int32),
        compiler_params=_params("arbitrary"),
        name="expert_mlp",
    )(block_expert, block_count, first.astype(jnp.int32), slot.astype(jnp.int32), nxt, last_used,
      xs, wg, wu, wd)


def _slots(route_t, counts):
    counts = counts[0, :N_EXPERTS].astype(jnp.int32)
    padded = ((counts + MOE_BLOCK - 1) // MOE_BLOCK) * MOE_BLOCK
    pend = jnp.cumsum(padded)
    pstart = pend - padded
    bstart = jnp.arange(MOE_NBLOCKS, dtype=jnp.int32) * MOE_BLOCK
    block_expert = jnp.minimum(jnp.sum(bstart[:, None] >= pend[None, :], axis=1), N_EXPERTS - 1)
    block_expert = block_expert.astype(jnp.int32)
    mine = block_expert[:, None] == jnp.arange(N_EXPERTS, dtype=jnp.int32)[None, :]
    left = jnp.sum(jnp.where(mine, counts + pstart, 0), axis=1) - bstart
    block_count = jnp.clip(left, 0, MOE_BLOCK).astype(jnp.int32)
    return _slot_rows(route_t, pstart.astype(jnp.int32)), block_expert, block_count


def _slot_rows_kernel(base_ref, rt_ref, o_ref):
    n_tok = rt_ref.shape[1]
    for k in range(TOP_K):
        e = rt_ref[k:k + 1, :]
        b = jnp.zeros((1, n_tok), F32)
        for j in range(N_EXPERTS):
            b = jnp.where(e == float(j), base_ref[j].astype(F32), b)
        dest = (b + rt_ref[2 * TOP_K + k:2 * TOP_K + k + 1, :]).astype(jnp.int32)
        for p in range(N_PLANES):
            for c in range(n_tok // LANES):
                o_ref[k * N_PLANES + p, c:c + 1, :] = dest[:, c * LANES:(c + 1) * LANES] + p * PLANE_ROWS


def _slot_rows(route_t, base):
    tm = 16 * LANES
    grid_spec = pltpu.PrefetchScalarGridSpec(
        num_scalar_prefetch=1,
        grid=(N_TOK // tm,),
        in_specs=[pl.BlockSpec((8, tm), lambda i, base: (0, i))],
        out_specs=pl.BlockSpec((TOP_K * N_PLANES, tm // LANES, LANES), lambda i, base: (0, i, 0)),
    )
    return pl.pallas_call(
        _slot_rows_kernel,
        grid_spec=grid_spec,
        out_shape=jax.ShapeDtypeStruct((TOP_K * N_PLANES, N_TOK // LANES, LANES), jnp.int32),
        compiler_params=_params("arbitrary"),
        name="slot_rows",
    )(base, route_t)


def _sc_workers():
    info = plsc.get_sparse_core_info()
    return info.num_cores, info.num_cores * info.num_subcores


def _sc_scatter2(src, idx, out_rows):
    n_win = src.shape[0] // SC_WINDOW
    nc, nw = _sc_workers()
    steps = n_win // nw
    mesh = plsc.VectorSubcoreMesh(core_axis_name="c", subcore_axis_name="s")

    @functools.partial(
        pl.kernel, mesh=mesh,
        out_type=jax.ShapeDtypeStruct((out_rows, LANES), src.dtype),
        scratch_types=[pltpu.VMEM((2 * steps, SC_WINDOW), jnp.int32),
                       pltpu.VMEM((SC_INFLIGHT, SC_WINDOW, LANES), src.dtype),
                       pltpu.SemaphoreType.DMA((SC_INFLIGHT,)), pltpu.SemaphoreType.DMA((SC_INFLIGHT,))],
        name="sc_dispatch_scatter",
    )
    def k(src_hbm, idx_hbm, out_hbm, idx_v, rows_v, lsem, wsem):
        first = (lax.axis_index("s") * nc + lax.axis_index("c")) * steps
        pltpu.sync_copy(idx_hbm.at[pl.ds(first, steps)], idx_v.at[pl.ds(0, steps)])
        pltpu.sync_copy(idx_hbm.at[pl.ds(n_win + first, steps)], idx_v.at[pl.ds(steps, steps)])

        @pl.loop(0, steps, step=SC_INFLIGHT)
        def _(j):
            loads = [pltpu.async_copy(src_hbm.at[pl.ds((first + j + b) * SC_WINDOW, SC_WINDOW)],
                                      rows_v.at[b], lsem.at[b]) for b in range(SC_INFLIGHT)]
            writes = []
            for b in range(SC_INFLIGHT):
                loads[b].wait()
                for half in range(TOP_K):
                    dst = out_hbm.at[idx_v.at[half * steps + j + b]]
                    writes.append(pltpu.async_copy(rows_v.at[b], dst, wsem.at[b]))
            for w in writes:
                w.wait()

    return k(src, idx)


def _sc_gather(table, idx):
    n_out = idx.shape[0] * SC_WINDOW
    nc, nw = _sc_workers()
    steps = n_out // nw // SC_WINDOW
    mesh = plsc.VectorSubcoreMesh(core_axis_name="c", subcore_axis_name="s")

    @functools.partial(
        pl.kernel, mesh=mesh,
        out_type=jax.ShapeDtypeStruct((n_out, LANES), table.dtype),
        scratch_types=[pltpu.VMEM((steps, SC_WINDOW), jnp.int32),
                       pltpu.VMEM((SC_INFLIGHT, SC_WINDOW, LANES), table.dtype),
                       pltpu.SemaphoreType.DMA((SC_INFLIGHT,)), pltpu.SemaphoreType.DMA((SC_INFLIGHT,))],
        name="sc_combine_gather",
    )
    def k(table_hbm, idx_hbm, out_hbm, idx_v, rows_v, gsem, wsem):
        first = (lax.axis_index("s") * nc + lax.axis_index("c")) * steps
        pltpu.sync_copy(idx_hbm.at[pl.ds(first, steps)], idx_v)

        @pl.loop(0, steps, step=SC_INFLIGHT)
        def _(j):
            gathers = [pltpu.async_copy(table_hbm.at[idx_v.at[j + b]], rows_v.at[b], gsem.at[b])
                       for b in range(SC_INFLIGHT)]
            writes = []
            for b in range(SC_INFLIGHT):
                gathers[b].wait()
                dst = out_hbm.at[pl.ds((first + j + b) * SC_WINDOW, SC_WINDOW)]
                writes.append(pltpu.async_copy(rows_v.at[b], dst, wsem.at[b]))
            for w in writes:
                w.wait()

    return k(table, idx)


def _final_kernel(x_ref, rt_ref, y_ref, gt_ref, g_ref, o_ref):
    x = x_ref[...] + gt_ref[...] * _combine(rt_ref, y_ref)
    ms = jnp.mean(x * x, axis=-1, keepdims=True)
    o_ref[...] = x * lax.rsqrt(ms + EPS) * g_ref[...]


def _final(x, moe, gate, g):
    tpb = SEQ // TM
    row = pl.BlockSpec((TM, D_MODEL), lambda i: (i, 0))
    return pl.pallas_call(
        _final_kernel,
        grid=(N_TOK // TM,),
        in_specs=[row, pl.BlockSpec((TM, LANES), lambda i: (i, 0)),
                  pl.BlockSpec((TOP_K, N_PLANES, TM, LANES), lambda i: (0, 0, i, 0)),
                  pl.BlockSpec((None, 1, D_MODEL), lambda i: (i // tpb, 0, 0)),
                  pl.BlockSpec((1, D_MODEL), lambda i: (0, 0))],
        out_specs=row,
        out_shape=jax.ShapeDtypeStruct((N_TOK, D_MODEL), F32),
        compiler_params=_params("arbitrary"),
        name="final_norm",
    )(x, moe[0], moe[1], gate, g.reshape(1, D_MODEL))


def kernel(x, c, positions, w_ada, b_ada, g_mix, w_in, b_forget, lambda_q1, lambda_k1, lambda_q2,
           lambda_k2, g_subln, g_fox_out, w_out, g_ffn, w_router_group, b_router_group,
           w_router_expert, b_router_expert, w_expert_gate, w_expert_up, w_expert_down, g_final):
    mod = _modulation(c, w_ada, b_ada)
    mod = mod.reshape(DEPTH, BATCH, 6, 1, D_MODEL)
    tables = _rope_tables(positions)
    pq = _forget_placement()
    w_in_bf = jnp.pad(w_in.astype(BF16), ((0, 0), (0, 0), (0, IN_COLS_PAD - IN_COLS)))
    xf = x.reshape(N_TOK, D_MODEL)
    moe = None
    gate = None
    for l in range(DEPTH):
        sh1, sc1, gt1, sh2, sc2, gt2 = (mod[l, :, j] for j in range(6))
        w_bf = (l, w_in_bf)
        bfp =jnp.pad(b_forget[l], (0, LANES - N_FOX_HEADS)).reshape(1, LANES)
        xf, (dq, dk, dv, fq, fk, fv) = _inproj(xf, moe, gate, sc1, sh1, g_mix[l], w_bf, bfp, tables, pq)

        lambda_init = 0.8 - 0.6 * float(np.exp(-0.3 * l))
        lamv = jnp.zeros((8, LANES), F32).at[0:4, 0:HEAD_DIM].set(
            jnp.stack([lambda_q1[l], lambda_k1[l], lambda_q2[l], lambda_k2[l]]))
        g_d = g_subln[l].reshape(1, LANES)
        g_f = jnp.concatenate([g_fox_out[l], g_fox_out[l]]).reshape(1, LANES)
        od = _attention(True, lambda_init, dq, dk, dv, g_d, lamv)
        of = _attention(False, lambda_init, fq, fk, fv, g_f, lamv)

        wr32 = jnp.pad(jnp.concatenate([w_router_group[l], w_router_expert[l]], axis=1),
                       ((0, 0), (0, LANES - N_GROUPS - N_EXPERTS)))
        wr_hi = wr32.astype(BF16)
        wr_lo = (wr32 - wr_hi.astype(F32)).astype(BF16)
        wr = jnp.concatenate([wr_hi, wr_lo], axis=1)
        br = jnp.pad(jnp.concatenate([b_router_group[l], b_router_expert[l]]),
                     (0, LANES - N_GROUPS - N_EXPERTS)).reshape(1, LANES)
        xf, h2, route, route_t, counts = _outproj(xf, od, of, gt1, sc2, sh2, g_ffn[l],
                                                  w_out[l].astype(BF16), wr, br)

        rows, block_expert, block_count = _slots(route_t, counts)
        rows = rows.reshape(TOP_K * N_PLANES * N_TOK // SC_WINDOW, SC_WINDOW)
        xs = _sc_scatter2(h2.reshape(N_PLANES * N_TOK, LANES), rows, N_PLANES * PLANE_ROWS)
        ys = _experts(l, block_expert, block_count, xs.reshape(N_PLANES, PLANE_ROWS, LANES),
                      w_expert_gate, w_expert_up, w_expert_down)
        y2 = _sc_gather(ys.reshape(N_PLANES * PLANE_ROWS, LANES), rows)
        moe = (route, y2.reshape(TOP_K, N_PLANES, N_TOK, LANES))
        gate = gt2
    out = _final(xf, moe, gate, g_final)
    return out.reshape(BATCH, SEQ, D_MODEL)
```

```python
import functools

import numpy as np
import jax
import jax.numpy as jnp
from jax import lax
from jax.experimental import pallas as pl
from jax.experimental.pallas import tpu as pltpu
from jax.experimental.pallas import tpu_sc as plsc

D_MODEL = 1024
BATCH = 4
SEQ = 4096
DEPTH = 4
N_TOK = BATCH * SEQ

CHUNK = 64
HEAD_DIM = 64
N_DIFF_HEADS = 4
N_FOX_HEADS = 8
DIFF_WIDTH = 512
FOX_WIDTH = 512
IN_COLS = 3 * DIFF_WIDTH + 3 * FOX_WIDTH + N_FOX_HEADS
ROT_DIM = 16
ROPE_THETA = 500000.0
N_GROUPS = 4
EXPERTS_PER_GROUP = 8
N_EXPERTS = 32
TOP_K = 2
D_EXPERT = 512
EPS = 1e-6

LANES = 128
IN_COLS_PAD = 3200
FF_COL = 3 * DIFF_WIDTH + 3 * FOX_WIDTH
QK_WIDTH = 8 * LANES
TM = 512
OUTPROJ_TILES = 2
TQ = 512
ATTN_MAPS = 4
ATTN_TQ = 256
N_LATE_CHAINS = 1
ONES_ROWS = 16
ATTN_ROWS = 128
LOG2E = 1.4426950408889634
MOE_BLOCK = 512
MOE_ROWS = N_TOK * TOP_K + N_EXPERTS * MOE_BLOCK
MOE_NBLOCKS = MOE_ROWS // MOE_BLOCK
PLANE_ROWS = MOE_ROWS
N_PLANES = D_MODEL // 2 // LANES
SC_WINDOW = 128
SC_INFLIGHT = 4
NEG = -1e30
VMEM_LIMIT = 56 * 1024 * 1024

F32 = jnp.float32
BF16 = jnp.bfloat16


def _bf16_round(x):
    return x.astype(BF16).astype(F32)


def _lane_iota(shape):
    return lax.broadcasted_iota(jnp.int32, shape, 1)


def _params(*sem):
    return pltpu.CompilerParams(dimension_semantics=sem, vmem_limit_bytes=VMEM_LIMIT)


def _pack_planes(y, o_ref, rows=slice(None)):
    bits = lax.bitcast_convert_type(_bf16_round(y), jnp.uint32)
    half = D_MODEL // 2
    word = bits[:, half:] | lax.shift_right_logical(bits[:, :half], jnp.uint32(16))
    word = lax.bitcast_convert_type(word, jnp.int32)
    for p in range(N_PLANES):
        o_ref[p, rows, :] = word[:, p * LANES:(p + 1) * LANES]


def _unpack_planes(planes):
    lo, hi = [], []
    for w in planes:
        u = lax.bitcast_convert_type(w, jnp.uint32)
        lo.append(lax.bitcast_convert_type(lax.shift_left(u, jnp.uint32(16)), F32))
        hi.append(lax.bitcast_convert_type(u & jnp.uint32(0xFFFF0000), F32))
    return jnp.concatenate(lo + hi, axis=1)


def _combine(route_ref, y_ref):
    rt = route_ref[...]
    y0 = _unpack_planes([y_ref[0, p] for p in range(N_PLANES)])
    y1 = _unpack_planes([y_ref[1, p] for p in range(N_PLANES)])
    return rt[:, 2:3] * y0 + rt[:, 3:4] * y1


def _mod_kernel(c_ref, w_ref, b_ref, o_ref):
    c = c_ref[...]
    cond = c / (1.0 + jnp.exp(-c))
    ch = cond.astype(BF16)
    cl = (cond - ch.astype(F32)).astype(BF16)
    w = w_ref[...]
    wh = w.astype(BF16)
    wl = (w - wh.astype(F32)).astype(BF16)
    acc = jnp.dot(ch, wh, preferred_element_type=F32)
    acc += jnp.dot(cl, wh, preferred_element_type=F32)
    acc += jnp.dot(ch, wl, preferred_element_type=F32)
    o_ref[...] = acc + b_ref[...]


def _modulation(c, w_ada, b_ada):
    rows = 16
    tn = 1536
    c_pad = jnp.zeros((rows, D_MODEL), F32).at[:BATCH].set(c)
    out = pl.pallas_call(
        _mod_kernel,
        grid=(DEPTH, 6 * D_MODEL // tn),
        in_specs=[
            pl.BlockSpec((rows, D_MODEL), lambda l, n: (0, 0)),
            pl.BlockSpec((None, D_MODEL, tn), lambda l, n: (l, 0, n)),
            pl.BlockSpec((None, 1, tn), lambda l, n: (l, 0, n)),
        ],
        out_specs=pl.BlockSpec((None, rows, tn), lambda l, n: (l, 0, n)),
        out_shape=jax.ShapeDtypeStruct((DEPTH, rows, 6 * D_MODEL), F32),
        compiler_params=_params("arbitrary", "arbitrary"),
        name="adaln_mod",
    )(c_pad, w_ada, b_ada.reshape(DEPTH, 1, 6 * D_MODEL))
    return out[:, :BATCH]


def _rope_kernel(pos_ref, inv_ref, c_ref, sa_ref, sb_ref):
    ang = pos_ref[...].astype(F32) * inv_ref[...]
    j = _lane_iota(ang.shape) % HEAD_DIM
    cosv = jnp.cos(ang)
    sinv = jnp.sin(ang)
    half = ROT_DIM // 2
    c_ref[...] = jnp.where(j < ROT_DIM, cosv, 1.0)
    sa_ref[...] = jnp.where(j < half, -sinv, 0.0)
    sb_ref[...] = jnp.where((j >= half) & (j < ROT_DIM), sinv, 0.0)


def _rope_tables(positions):
    half = ROT_DIM // 2
    inv = ROPE_THETA ** (-jnp.arange(0, ROT_DIM, 2, dtype=F32) / ROT_DIM)
    lane = np.arange(LANES)
    inv_lane = inv[(lane % HEAD_DIM) % half].reshape(1, LANES)
    spec = pl.BlockSpec((TM, LANES), lambda i: (i, 0))
    shape = jax.ShapeDtypeStruct((N_TOK, LANES), F32)
    return pl.pallas_call(
        _rope_kernel,
        grid=(N_TOK // TM,),
        in_specs=[pl.BlockSpec((TM, 1), lambda i: (i, 0)),
                  pl.BlockSpec((1, LANES), lambda i: (0, 0))],
        out_specs=[spec, spec, spec],
        out_shape=[shape, shape, shape],
        compiler_params=_params("arbitrary"),
        name="rope_tables",
    )(positions.reshape(N_TOK, 1), inv_lane)


def _rms_mod(x, g, sc, sh):
    ms = jnp.mean(x * x, axis=-1, keepdims=True)
    return (x * lax.rsqrt(ms + EPS) * g) * (1.0 + sc) + sh


def _inproj_kernel(fuse, *refs):
    if fuse:
        (x_ref, rt_ref, y_ref, gt_ref, sc_ref, sh_ref, g_ref, w_ref, bf_ref, c_ref, sa_ref, sb_ref,
         pq_ref, xo_ref, dq_ref, dk_ref, dv_ref, fq_ref, fk_ref, fv_ref, carry_ref, h_even, h_odd) = refs
    else:
        (x_ref, sc_ref, sh_ref, g_ref, w_ref, bf_ref, c_ref, sa_ref, sb_ref,
         pq_ref, dq_ref, dk_ref, dv_ref, fq_ref, fk_ref, fv_ref, carry_ref, h_even, h_odd) = refs
    step = pl.program_id(0)

    def normalise(h_ref):
        if fuse:
            x = x_ref[...] + gt_ref[...] * _combine(rt_ref, y_ref)
            xo_ref[...] = x
        else:
            x = x_ref[...]
        h_ref[...] = _rms_mod(x, g_ref[...], sc_ref[...], sh_ref[...]).astype(BF16)

    @pl.when(step == 0)
    def _():
        h_odd[...] = jnp.zeros_like(h_odd)

    @pl.when((step == 0) | ((step - 1) % (SEQ // TM) == 0))
    def _():
        carry_ref[...] = jnp.zeros_like(carry_ref)

    for parity, (h_new, h_old) in enumerate(((h_even, h_odd), (h_odd, h_even))):
        @pl.when(step % 2 == parity)
        def _():
            _inproj_project(h_old[...], w_ref, bf_ref, c_ref, sa_ref, sb_ref, pq_ref, dq_ref, dk_ref,
                            dv_ref, fq_ref, fk_ref, fv_ref, carry_ref)
            normalise(h_new)


def _inproj_project(hb, w_ref, bf_ref, c_ref, sa_ref, sb_ref, pq_ref, dq_ref, dk_ref, dv_ref, fq_ref,
                    fk_ref, fv_ref, carry_ref):
    lane = _lane_iota((TM, LANES))
    nh = N_FOX_HEADS

    def pack3(a):
        hi = _bf16_round(a)
        r1 = a - hi
        mid = _bf16_round(r1)
        lo = _bf16_round(r1 - mid)
        return jnp.where(lane < nh, hi,
                         jnp.where(lane < 2 * nh, pltpu.roll(mid, nh, 1),
                                   jnp.where(lane < 3 * nh, pltpu.roll(lo, 2 * nh, 1), 0.0)))

    z = jnp.dot(hb, w_ref[:, FF_COL:FF_COL + LANES], preferred_element_type=F32) + bf_ref[...]

    low = lane < HEAD_DIM
    rc, rsa, rsb = c_ref[...], sa_ref[...], sb_ref[...]
    scale = HEAD_DIM ** -0.5 * LOG2E

    def split_store(chunk, o_ref, m, extra_a=None, extra_b=None):
        a = jnp.where(low, chunk, 0.0)
        b = jnp.where(low, pltpu.roll(chunk, HEAD_DIM, 1), 0.0)
        if extra_a is not None:
            a = a + extra_a
            b = b + extra_b
        o_ref[:, (2 * m) * LANES:(2 * m + 1) * LANES] = a.astype(BF16)
        o_ref[:, (2 * m + 1) * LANES:(2 * m + 2) * LANES] = b.astype(BF16)

    def rope(xc):
        return xc * rc + pltpu.roll(xc, LANES - ROT_DIM // 2, 1) * rsa + pltpu.roll(xc, ROT_DIM // 2, 1) * rsb

    pdq = jnp.dot(hb, w_ref[:, 0:DIFF_WIDTH], preferred_element_type=F32)
    for m in range(N_DIFF_HEADS):
        split_store(rope(pdq[:, m * LANES:(m + 1) * LANES]) * scale, dq_ref, m)
    pdk = jnp.dot(hb, w_ref[:, DIFF_WIDTH:2 * DIFF_WIDTH], preferred_element_type=F32)
    for m in range(N_DIFF_HEADS):
        split_store(rope(pdk[:, m * LANES:(m + 1) * LANES]), dk_ref, m)
    def store_values_t(pv, o_ref, width):
        ones = jnp.ones((ONES_ROWS, TM), BF16)
        for m in range(4):
            vt = pv[:, m * LANES:(m + 1) * LANES].T.astype(BF16)
            for i in range(LANES // width):
                o_ref[m * (LANES // width) + i, 0:width, :] = vt[i * width:(i + 1) * width]
                o_ref[m * (LANES // width) + i, width:width + ONES_ROWS, :] = ones

    store_values_t(jnp.dot(hb, w_ref[:, 2 * DIFF_WIDTH:3 * DIFF_WIDTH], preferred_element_type=F32),
                   dv_ref, 2 * HEAD_DIM)
    o = 3 * DIFF_WIDTH
    store_values_t(jnp.dot(hb, w_ref[:, o + 2 * FOX_WIDTH:o + 3 * FOX_WIDTH],
                           preferred_element_type=F32), fv_ref, HEAD_DIM)

    logf =jnp.minimum(z, 0.0) - jnp.log(1.0 + jnp.exp(-jnp.abs(z)))
    logf = jnp.where(lane < nh, logf, 0.0)
    row = lax.broadcasted_iota(jnp.int32, (TM, TM), 0)
    col = lax.broadcasted_iota(jnp.int32, (TM, TM), 1)
    tri = (row >= col).astype(BF16)
    r = jnp.dot(tri, pack3(logf).astype(BF16), preferred_element_type=F32)
    cs = r + pltpu.roll(r, LANES - nh, 1) + pltpu.roll(r, LANES - 2 * nh, 1)
    cf = jnp.where(lane < nh, cs + carry_ref[0:1, :], 0.0)
    carry_ref[...] = jnp.broadcast_to(cf[TM - 1:TM, :], carry_ref.shape)

    t3 = jnp.where(lane == 3 * nh, 1.0, pack3(cf * LOG2E)).astype(BF16)
    aug = jnp.dot(t3, pq_ref[...], preferred_element_type=F32)

    pfq =jnp.dot(hb, w_ref[:, o:o + FOX_WIDTH], preferred_element_type=F32)
    for m in range(N_FOX_HEADS // 2):
        split_store(pfq[:, m * LANES:(m + 1) * LANES] * scale, fq_ref, m,
                    aug[:, (2 * m) * LANES:(2 * m + 1) * LANES],
                    aug[:, (2 * m + 1) * LANES:(2 * m + 2) * LANES])
    pfk = jnp.dot(hb, w_ref[:, o + FOX_WIDTH:o + 2 * FOX_WIDTH], preferred_element_type=F32)
    for m in range(N_FOX_HEADS // 2):
        split_store(pfk[:, m * LANES:(m + 1) * LANES], fk_ref, m,
                    aug[:, QK_WIDTH + (2 * m) * LANES:QK_WIDTH + (2 * m + 1) * LANES],
                    aug[:, QK_WIDTH + (2 * m + 1) * LANES:QK_WIDTH + (2 * m + 2) * LANES])


def _forget_placement():
    nh = N_FOX_HEADS
    p = np.zeros((LANES, 2 * QK_WIDTH), np.float32)
    for h in range(nh):
        base_q = h * LANES + HEAD_DIM
        base_k = QK_WIDTH + h * LANES + HEAD_DIM
        for part in range(3):
            p[part * nh + h, base_q + part] = 1.0
            p[3 * nh, base_q + 3 + part] = 1.0
            p[3 * nh, base_k + part] = 1.0
            p[part * nh + h, base_k + 3 + part] = -1.0
    return jnp.asarray(p, BF16)


def _inproj(x, moe, gate, sc, sh, g, w_bf, b_forget, tables, pq):
    fuse = moe is not None
    tpb = SEQ // TM
    n_tiles = N_TOK // TM
    new = lambda s: jnp.minimum(s, n_tiles - 1)
    old = lambda s: jnp.maximum(s - 1, 0)
    row = pl.BlockSpec((TM, D_MODEL), lambda s: (new(s), 0))
    per_batch = pl.BlockSpec((None, 1, D_MODEL), lambda s: (new(s) // tpb, 0, 0))
    const = lambda shape: pl.BlockSpec(shape, lambda s: (0,) * len(shape))
    tab = pl.BlockSpec((TM, LANES), lambda s: (old(s), 0))
    in_specs = [row]
    args = [x]
    if fuse:
        in_specs += [pl.BlockSpec((TM, LANES), lambda s: (new(s), 0)),
                     pl.BlockSpec((TOP_K, N_PLANES, TM, LANES), lambda s: (0, 0, new(s), 0)), per_batch]
        args += [moe[0], moe[1], gate]
    layer, w_all = w_bf
    in_specs += [per_batch, per_batch, const((1, D_MODEL)),
                 pl.BlockSpec((None, D_MODEL, IN_COLS_PAD), lambda s: (layer, 0, 0)),
                 const((1, LANES)), tab, tab, tab, const((LANES, 2 * QK_WIDTH))]
    args += [sc, sh, g.reshape(1, D_MODEL), w_all, b_forget, *tables, pq]
    wide = pl.BlockSpec((TM, QK_WIDTH), lambda s: (old(s), 0))
    def vspec(heads, width):
        rows = width + ONES_ROWS
        return (pl.BlockSpec((None, heads, None, rows, TM),
                             lambda s: (old(s) // tpb, 0, old(s) % tpb, 0, 0)),
                jax.ShapeDtypeStruct((BATCH, heads, tpb, rows, TM), BF16))

    wide_s = jax.ShapeDtypeStruct((N_TOK, QK_WIDTH), BF16)
    dv_spec, dv_s = vspec(N_DIFF_HEADS, 2 * HEAD_DIM)
    fv_spec, fv_s = vspec(N_FOX_HEADS, HEAD_DIM)
    out_specs = [wide, wide, dv_spec, wide, wide, fv_spec]
    out_shape = [wide_s, wide_s, dv_s, wide_s, wide_s, fv_s]
    if fuse:
        out_specs = [row] + out_specs
        out_shape = [jax.ShapeDtypeStruct((N_TOK, D_MODEL), F32)] + out_shape
    outs = pl.pallas_call(
        functools.partial(_inproj_kernel, fuse),
        grid=(n_tiles + 1,),
        in_specs=in_specs,
        out_specs=out_specs,
        out_shape=out_shape,
        scratch_shapes=[pltpu.VMEM((8, LANES), F32), pltpu.VMEM((TM, D_MODEL), BF16),
                        pltpu.VMEM((TM, D_MODEL), BF16)],
        compiler_params=_params("arbitrary"),
        name="norm_inproj",
    )(*args)
    if fuse:
        return outs[0], outs[1:]
    return x, outs


def _attn_kernel(diff, lambda_init, *refs):
    q_refs, k_refs = refs[:ATTN_MAPS], refs[ATTN_MAPS:2 * ATTN_MAPS]
    v_ref, g_ref, lam_ref, o_ref = refs[2 * ATTN_MAPS:2 * ATTN_MAPS + 4]
    scratch = refs[2 * ATTN_MAPS + 4:]
    nq = SEQ // TQ
    n_half = TQ // ATTN_TQ
    feat = 2 * HEAD_DIM if diff else HEAD_DIM
    chains = []
    for mi, (q_ref, k_ref) in enumerate(zip(q_refs, k_refs)):
        for h in range(n_half):
            c = mi * n_half + h
            qt_sc, s_sc, p_sc, m_sc, a_sc, acc_sc = scratch[c::ATTN_MAPS * n_half]
            vh = mi // 2 if diff else mi
            chains.append((h, k_ref, qt_sc, s_sc, p_sc, m_sc, a_sc, acc_sc, q_ref, vh))
    order = [chains[mi * n_half + h] for h in range(n_half) for mi in range(ATTN_MAPS)]
    early, late = order[:-N_LATE_CHAINS], order[-N_LATE_CHAINS:]

    def load_queries(qi):
        for mi in range(ATTN_MAPS):
            q_ref = chains[mi * n_half][8]
            qt = q_ref[pl.ds(pl.multiple_of(qi * TQ, TQ), TQ), :].astype(F32).T.astype(BF16)
            for h in range(n_half):
                chains[mi * n_half + h][2][...] = qt[:, h * ATTN_TQ:(h + 1) * ATTN_TQ]

    def reset_state():
        for chain in chains:
            m_sc, _, acc_sc = chain[5:8]
            m_sc[...] = jnp.full(m_sc.shape, NEG, F32)
            acc_sc[...] = jnp.zeros(acc_sc.shape, F32)

    def n_keys(chain, masked):
        return (chain[0] + 1) * ATTN_TQ if masked else TQ

    def scores(chain, j, masked):
        h, k_ref, qt_sc, s_sc = chain[:4]
        nk = n_keys(chain, masked)
        off = pl.multiple_of(j * TQ, TQ)
        s = jnp.dot(k_ref[pl.ds(off, nk), :], qt_sc[...], preferred_element_type=F32)
        if masked:
            kk = lax.broadcasted_iota(jnp.int32, (nk, ATTN_TQ), 0)
            qq = h * ATTN_TQ + lax.broadcasted_iota(jnp.int32, (nk, ATTN_TQ), 1)
            s = jnp.where((kk // CHUNK <= qq // CHUNK) if diff else (kk <= qq), s, NEG)
        s_sc[0:nk, :] = s

    def softmax(chain, masked):
        s_sc, p_sc, m_sc, a_sc = chain[3:7]
        nk = n_keys(chain, masked)
        m_all = m_sc[...]
        m_parts = []
        for c0 in range(0, ATTN_TQ, LANES):
            cols = slice(c0, c0 + LANES)
            pm = s_sc[0:ATTN_ROWS, cols]
            for r0 in range(ATTN_ROWS, nk, ATTN_ROWS):
                pm = jnp.maximum(pm, s_sc[r0:r0 + ATTN_ROWS, cols])
            m_new = jnp.maximum(m_all[:, cols], jnp.max(pm, axis=0, keepdims=True))
            for r0 in range(0, nk, ATTN_ROWS):
                p = jnp.exp2(s_sc[r0:r0 + ATTN_ROWS, cols] - m_new)
                p_sc[r0:r0 + ATTN_ROWS, cols] = p.astype(BF16)
            m_parts.append(m_new)
        m_new = jnp.concatenate(m_parts, axis=1)
        a_sc[...] = jnp.exp2(m_all - m_new)
        m_sc[...] = m_new

    def values(chain, j, masked=False):
        p_sc, a_sc, acc_sc, vh = chain[4], chain[6], chain[7], chain[9]
        nk = n_keys(chain, masked)
        pv = jnp.dot(v_ref[vh, j, :, 0:nk], p_sc[0:nk, :], preferred_element_type=F32)
        acc_sc[...] = a_sc[...] * acc_sc[...] + pv

    def idle_late():
        for chain in late:
            chain[4][...] = jnp.zeros(chain[4].shape, BF16)
            chain[6][...] = jnp.ones(chain[6].shape, F32)

    def consume(j, cur_masked=False, nxt=None, nxt_masked=False, final=False, before_next=None):
        def open_late(chain):
            scores(chain, j, cur_masked)
            values(chain, jnp.maximum(j - 1, 0))

        open_late(late[0])
        for i, chain in enumerate(early):
            softmax(chain, cur_masked)
            if i == 0:
                for other in late[1:]:
                    open_late(other)
                if before_next is not None:
                    before_next()
            if nxt is not None:
                scores(chain, nxt, nxt_masked)
            values(chain, j, cur_masked)
        for chain in late:
            softmax(chain, cur_masked)
        if final:
            for chain in late:
                values(chain, j, cur_masked)

    def finalize(qi):
        g = g_ref[...]
        rows = pl.ds(pl.multiple_of(qi * TQ, TQ), TQ)
        for pair in range(ATTN_MAPS // 2):
            def normalised(chain):
                acc_sc = chain[7]
                return acc_sc[0:feat] * (1.0 / acc_sc[feat:feat + 1])

            ot = [jnp.concatenate([normalised(chains[mi * n_half + h]) for h in range(n_half)], axis=1)
                  for mi in (2 * pair, 2 * pair + 1)]
            cols = slice(pair * LANES, (pair + 1) * LANES)
            if diff:
                lv = lam_ref[...]
                lam = (jnp.exp(jnp.sum(lv[0:1] * lv[1:2], axis=1, keepdims=True))
                       - jnp.exp(jnp.sum(lv[2:3] * lv[3:4], axis=1, keepdims=True)) + lambda_init)
                o = (ot[0] - lam * ot[1]).T
                y = o * lax.rsqrt(jnp.mean(o * o, axis=1, keepdims=True) + EPS) * g
                o_ref[rows, cols] = (y * (1.0 - lambda_init)).astype(o_ref.dtype)
            else:
                o = jnp.concatenate(ot, axis=0).T
                low = _lane_iota((TQ, LANES)) < HEAD_DIM
                sq = o * o
                msa = jnp.sum(jnp.where(low, sq, 0.0), axis=1, keepdims=True) / HEAD_DIM
                msb = jnp.sum(jnp.where(low, 0.0, sq), axis=1, keepdims=True) / HEAD_DIM
                inv = jnp.where(low, lax.rsqrt(msa + EPS), lax.rsqrt(msb + EPS))
                o_ref[rows, cols] = (o * inv * g).astype(o_ref.dtype)

    load_queries(0)
    reset_state()
    idle_late()
    for chain in early:
        scores(chain, 0, True)

    @pl.loop(0, nq)
    def _(qi):
        n_plain = jnp.maximum(qi - 1, 0)

        def run(first, count):
            for i in range(count):
                consume(first + i, nxt=first + i + 1)

        @pl.loop(0, n_plain // 4)
        def _(t):
            run(4 * t, 4)

        done = (n_plain // 4) * 4

        @pl.when(n_plain - done >= 2)
        def _():
            run(done, 2)

        def last_blocks(to_next_tile):
            if to_next_tile:
                consume(qi, cur_masked=True, nxt=0, final=True, before_next=lambda: load_queries(qi + 1))
            else:
                consume(qi, cur_masked=True, final=True)
            finalize(qi)

        for to_next_tile in (True, False):
            more = (qi < nq - 1) if to_next_tile else (qi == nq - 1)

            last_tile_odd = (nq - 2) % 2 == 1
            for odd in ((True, False) if to_next_tile else (last_tile_odd,)):
                @pl.when(more & (qi > 0) & ((n_plain % 2 == 1) == odd))
                def _():
                    if odd:
                        run(qi - 2, 1)
                    consume(qi - 1, nxt=qi, nxt_masked=True)
                    last_blocks(to_next_tile)

            if to_next_tile:
                @pl.when(qi == 0)
                def _():
                    last_blocks(to_next_tile)

        reset_state()
        idle_late()


def _attention(diff, lambda_init, q, k, v, g, lamv):
    nq = SEQ // TQ
    n_steps = 8 // ATTN_MAPS
    v_heads = v.shape[1] // n_steps
    kspec = lambda m: pl.BlockSpec((SEQ, LANES), lambda b, p: (b, ATTN_MAPS * p + m))
    maps = [kspec(m) for m in range(ATTN_MAPS)]
    return pl.pallas_call(
        functools.partial(_attn_kernel, diff, lambda_init),
        grid=(BATCH, n_steps),
        in_specs=maps + maps + [
            pl.BlockSpec((None, v_heads, nq, v.shape[3], TQ), lambda b, p: (b, p, 0, 0, 0)),
            pl.BlockSpec((1, LANES), lambda b, p: (0, 0)),
            pl.BlockSpec((8, LANES), lambda b, p: (0, 0))],
        out_specs=pl.BlockSpec((SEQ, ATTN_MAPS // 2 * LANES), lambda b, p: (b, p)),
        out_shape=jax.ShapeDtypeStruct((N_TOK, DIFF_WIDTH), BF16),
        scratch_shapes=[pltpu.VMEM(shape, dt)
                        for shape, dt in (((LANES, ATTN_TQ), BF16), ((TQ, ATTN_TQ), F32),
                                          ((TQ, ATTN_TQ), BF16), ((1, ATTN_TQ), F32),
                                          ((1, ATTN_TQ), F32), ((v.shape[3], ATTN_TQ), F32))
                        for _ in range(ATTN_MAPS * TQ // ATTN_TQ)],
        compiler_params=_params("arbitrary", "arbitrary"),
        name="diff_attention" if diff else "fox_attention",
    )(*([q] * ATTN_MAPS), *([k] * ATTN_MAPS), v, g, lamv)


def _outproj_kernel(x_ref, od_ref, of_ref, gt_ref, sc_ref, sh_ref, g_ref, wo_ref, wr_ref, br_ref,
                    x1_ref, h2_ref, rt_ref, rtt_ref, cnt_ref, carry_ref):
    @pl.when(pl.program_id(0) == 0)
    def _():
        carry_ref[...] = jnp.zeros_like(carry_ref)

    tiles = [slice(t * TM, (t + 1) * TM) for t in range(OUTPROJ_TILES)]
    lane = _lane_iota((TM, LANES))
    lanef = lane.astype(F32)
    big = float(LANES)

    def project(rows):
        mix = jnp.dot(od_ref[rows, :], wo_ref[0:DIFF_WIDTH, :], preferred_element_type=F32)
        return mix + jnp.dot(of_ref[rows, :], wo_ref[DIFF_WIDTH:, :], preferred_element_type=F32)

    def normalise(rows, mix):
        x1 = x_ref[rows, :] + gt_ref[...] * mix
        x1_ref[rows, :] = x1
        h = _rms_mod(x1, g_ref[...], sc_ref[...], sh_ref[...])
        hh = h.astype(BF16)
        _pack_planes(h, h2_ref, rows)
        return hh, (h - hh.astype(F32)).astype(BF16)

    def router_logits(hh, hl):
        r1 = jnp.dot(hh, wr_ref[...], preferred_element_type=F32)
        r2 = jnp.dot(hl, wr_ref[:, 0:LANES], preferred_element_type=F32)
        return r1[:, 0:LANES] + r1[:, LANES:] + r2 + br_ref[...]

    def top_k(logits):
        isg = lane < N_GROUPS
        lg = jnp.where(isg, logits, NEG)
        mg = jnp.max(lg, axis=1, keepdims=True)
        sg = jnp.sum(jnp.where(isg, jnp.exp(lg - mg), 0.0), axis=1, keepdims=True)
        p_g = 1.0 / sg
        gsel = jnp.min(jnp.where(isg & (lg == mg), lanef, big), axis=1, keepdims=True)
        lo = N_GROUPS + gsel * EXPERTS_PER_GROUP
        ise = (lanef >= lo) & (lanef < lo + EXPERTS_PER_GROUP)
        le = jnp.where(ise, logits, NEG)
        t1 = jnp.max(le, axis=1, keepdims=True)
        i1 = jnp.min(jnp.where(ise & (le == t1), lanef, big), axis=1, keepdims=True)
        ise2 = ise & (lanef != i1)
        le2 = jnp.where(ise2, logits, NEG)
        t2 = jnp.max(le2, axis=1, keepdims=True)
        i2 = jnp.min(jnp.where(ise2 & (le2 == t2), lanef, big), axis=1, keepdims=True)
        d = jnp.exp(t2 - t1)
        return i1 - N_GROUPS, i2 - N_GROUPS, p_g / (1.0 + d), p_g * d / (1.0 + d)

    def earlier_in_tile(e1, e2):
        both = jnp.where((lanef == e1) | (lanef == e2), 1.0, 0.0)
        row = lax.broadcasted_iota(jnp.int32, (TM, TM), 0)
        col = lax.broadcasted_iota(jnp.int32, (TM, TM), 1)
        before = jnp.dot((row > col).astype(BF16), both.astype(BF16), preferred_element_type=F32)
        return before, jnp.sum(both, axis=0, keepdims=True)

    mixes = [project(rows) for rows in tiles]
    splits = [normalise(rows, mix) for rows, mix in zip(tiles, mixes)]
    logits = [router_logits(hh, hl) for hh, hl in splits]
    picks = [top_k(lg) for lg in logits]
    befores = [earlier_in_tile(e1, e2) for e1, e2, _, _ in picks]
    counts = carry_ref[0:1, :]
    for rows, (e1, e2, w1, w2), (before, added) in zip(tiles, picks, befores):
        before = before + counts
        rank1 = jnp.sum(jnp.where(lanef == e1, before, 0.0), axis=1, keepdims=True)
        rank2 = jnp.sum(jnp.where(lanef == e2, before, 0.0), axis=1, keepdims=True)
        out = jnp.zeros((TM, LANES), F32)
        for j, v in enumerate((e1, e2, w1, w2, rank1, rank2)):
            out = jnp.where(lane == j, v, out)
        rt_ref[rows, :] = out
        rtt_ref[:, rows] = out.T[0:8, :]
        counts = counts + added
    carry_ref[...] = jnp.broadcast_to(counts, carry_ref.shape)
    cnt_ref[...] = jnp.broadcast_to(counts, cnt_ref.shape)


def _outproj(x, od, of, gt, sc, sh, g, wo_bf, wr, br):
    tm = OUTPROJ_TILES * TM
    tpb = SEQ // tm
    row = pl.BlockSpec((tm, D_MODEL), lambda i: (i, 0))
    half = pl.BlockSpec((tm, DIFF_WIDTH), lambda i: (i, 0))
    per_batch = pl.BlockSpec((None, 1, D_MODEL), lambda i: (i // tpb, 0, 0))
    const = lambda shape: pl.BlockSpec(shape, lambda i: (0,) * len(shape))
    return pl.pallas_call(
        _outproj_kernel,
        grid=(N_TOK // tm,),
        in_specs=[row, half, half, per_batch, per_batch, per_batch, const((1, D_MODEL)),
                  const((D_MODEL, D_MODEL)), const((D_MODEL, 2 * LANES)), const((1, LANES))],
        out_specs=[row, pl.BlockSpec((N_PLANES, tm, LANES), lambda i: (0, i, 0)),
                   pl.BlockSpec((tm, LANES), lambda i: (i, 0)), pl.BlockSpec((8, tm), lambda i: (0, i)),
                   const((8, LANES))],
        out_shape=[jax.ShapeDtypeStruct((N_TOK, D_MODEL), F32),
                   jax.ShapeDtypeStruct((N_PLANES, N_TOK, LANES), jnp.int32),
                   jax.ShapeDtypeStruct((N_TOK, LANES), F32),
                   jax.ShapeDtypeStruct((8, N_TOK), F32),
                   jax.ShapeDtypeStruct((8, LANES), F32)],
        scratch_shapes=[pltpu.VMEM((8, LANES), F32)],
        compiler_params=_params("arbitrary"),
        name="outproj_router",
    )(x, od, of, gt, sc, sh, g.reshape(1, D_MODEL), wo_bf, wr, br)


def _expert_kernel(layer, be_ref, cnt_ref, first_ref, slot_ref, next_ref, last_ref, xs_ref, wg_hbm, wu_hbm,
                   wd_hbm, ys_ref, wg_sc, wu_sc, wd_sc, wg_f32, wu_f32, wd_f32, sem):
    i = pl.program_id(0)
    cnt = cnt_ref[i]

    def weight_copies(expert, slot):
        return [pltpu.make_async_copy(hbm.at[layer, expert], buf.at[slot], sem.at[slot, n])
                for n, (hbm, buf) in enumerate(((wg_hbm, wg_f32), (wu_hbm, wu_f32), (wd_hbm, wd_f32)))]

    @pl.when(first_ref[i] == 1)
    def _():
        slot = slot_ref[i]

        @pl.when(i == 0)
        def _():
            for copy in weight_copies(be_ref[0], 0):
                copy.start()

        for copy in weight_copies(be_ref[i], slot):
            copy.wait()
        wg_sc[...] = wg_f32[slot].astype(BF16)
        wu_sc[...] = wu_f32[slot].astype(BF16)
        wd_sc[...] = wd_f32[slot].astype(BF16)

        @pl.when(next_ref[i] >= 0)
        def _():
            for copy in weight_copies(next_ref[i], 1 - slot):
                copy.start()

    def mlp(n_rows):
        rows = slice(0, n_rows)
        live = lax.broadcasted_iota(jnp.int32, (n_rows, LANES), 0) < cnt
        xb = _unpack_planes([jnp.where(live, xs_ref[p, rows, :], 0) for p in range(N_PLANES)]).astype(BF16)
        a = jnp.dot(xb, wg_sc[...], preferred_element_type=F32)
        u = jnp.dot(xb, wu_sc[...], preferred_element_type=F32)
        hid = (a / (1.0 + jnp.exp(-a)) * u).astype(BF16)
        _pack_planes(jnp.dot(hid, wd_sc[...], preferred_element_type=F32), ys_ref, rows)
        if n_rows < MOE_BLOCK:
            ys_ref[:, n_rows:, :] = jnp.zeros((N_PLANES, MOE_BLOCK - n_rows, LANES), ys_ref.dtype)

    half, quarter = MOE_BLOCK // 2, MOE_BLOCK // 4

    @pl.when(cnt > half)
    def _():
        mlp(MOE_BLOCK)

    @pl.when((cnt > quarter) & (cnt <= half))
    def _():
        mlp(half)

    @pl.when((cnt > 0) & (cnt <= quarter))
    def _():
        mlp(quarter)


def _experts(layer, block_expert, block_count, xs, wg, wu, wd):
    idx = jnp.arange(MOE_NBLOCKS, dtype=jnp.int32)
    first = jnp.concatenate([jnp.ones((1,), jnp.bool_), block_expert[1:] != block_expert[:-1]])
    slot = (jnp.cumsum(first.astype(jnp.int32)) - 1) % 2
    later_first = (idx[None, :] > idx[:, None]) & first[None, :]
    nxt = jnp.min(jnp.where(later_first, block_expert[None, :], N_EXPERTS), axis=1)
    nxt = jnp.where(nxt == N_EXPERTS, -1, nxt).astype(jnp.int32)

    last_used = jnp.maximum(jnp.sum((block_count > 0).astype(jnp.int32)) - 1, 0).reshape(1)
    planes = pl.BlockSpec((N_PLANES, MOE_BLOCK, LANES),
                          lambda i, be, bc, fi, sl, nx, lu: (0, jnp.minimum(i, lu[0]), 0))
    hbm = pl.BlockSpec(memory_space=pl.ANY)
    grid_spec = pltpu.PrefetchScalarGridSpec(
        num_scalar_prefetch=6,
        grid=(MOE_NBLOCKS,),
        in_specs=[planes, hbm, hbm, hbm],
        out_specs=planes,
        scratch_shapes=[pltpu.VMEM((D_MODEL, D_EXPERT), BF16), pltpu.VMEM((D_MODEL, D_EXPERT), BF16),
                        pltpu.VMEM((D_EXPERT, D_MODEL), BF16),
                        pltpu.VMEM((2, D_MODEL, D_EXPERT), F32), pltpu.VMEM((2, D_MODEL, D_EXPERT), F32),
                        pltpu.VMEM((2, D_EXPERT, D_MODEL), F32), pltpu.SemaphoreType.DMA((2, 3))],
    )
    return pl.pallas_call(
        functools.partial(_expert_kernel, layer),
        grid_spec=grid_spec,
        out_shape=jax.ShapeDtypeStruct((N_PLANES, PLANE_ROWS, LANES), jnp.int32),
        compiler_params=_params("arbitrary"),
        name="expert_mlp",
    )(block_expert, block_count, first.astype(jnp.int32), slot.astype(jnp.int32), nxt, last_used,
      xs, wg, wu, wd)


def _slots(route_t, counts):
    counts = counts[0, :N_EXPERTS].astype(jnp.int32)
    padded = ((counts + MOE_BLOCK - 1) // MOE_BLOCK) * MOE_BLOCK
    pend = jnp.cumsum(padded)
    pstart = pend - padded
    bstart = jnp.arange(MOE_NBLOCKS, dtype=jnp.int32) * MOE_BLOCK
    block_expert = jnp.minimum(jnp.sum(bstart[:, None] >= pend[None, :], axis=1), N_EXPERTS - 1)
    block_expert = block_expert.astype(jnp.int32)
    mine = block_expert[:, None] == jnp.arange(N_EXPERTS, dtype=jnp.int32)[None, :]
    left = jnp.sum(jnp.where(mine, counts + pstart, 0), axis=1) - bstart
    block_count = jnp.clip(left, 0, MOE_BLOCK).astype(jnp.int32)
    return _slot_rows(route_t, pstart.astype(jnp.int32)), block_expert, block_count


def _slot_rows_kernel(base_ref, rt_ref, o_ref):
    n_tok = rt_ref.shape[1]
    for k in range(TOP_K):
        e = rt_ref[k:k + 1, :]
        b = jnp.zeros((1, n_tok), F32)
        for j in range(N_EXPERTS):
            b = jnp.where(e == float(j), base_ref[j].astype(F32), b)
        dest = (b + rt_ref[2 * TOP_K + k:2 * TOP_K + k + 1, :]).astype(jnp.int32)
        for p in range(N_PLANES):
            for c in range(n_tok // LANES):
                o_ref[k * N_PLANES + p, c:c + 1, :] = dest[:, c * LANES:(c + 1) * LANES] + p * PLANE_ROWS


def _slot_rows(route_t, base):
    tm = 16 * LANES
    grid_spec = pltpu.PrefetchScalarGridSpec(
        num_scalar_prefetch=1,
        grid=(N_TOK // tm,),
        in_specs=[pl.BlockSpec((8, tm), lambda i, base: (0, i))],
        out_specs=pl.BlockSpec((TOP_K * N_PLANES, tm // LANES, LANES), lambda i, base: (0, i, 0)),
    )
    return pl.pallas_call(
        _slot_rows_kernel,
        grid_spec=grid_spec,
        out_shape=jax.ShapeDtypeStruct((TOP_K * N_PLANES, N_TOK // LANES, LANES), jnp.int32),
        compiler_params=_params("arbitrary"),
        name="slot_rows",
    )(base, route_t)


def _sc_workers():
    info = plsc.get_sparse_core_info()
    return info.num_cores, info.num_cores * info.num_subcores


def _sc_scatter2(src, idx, out_rows):
    n_win = src.shape[0] // SC_WINDOW
    nc, nw = _sc_workers()
    steps = n_win // nw
    mesh = plsc.VectorSubcoreMesh(core_axis_name="c", subcore_axis_name="s")

    @functools.partial(
        pl.kernel, mesh=mesh,
        out_type=jax.ShapeDtypeStruct((out_rows, LANES), src.dtype),
        scratch_types=[pltpu.VMEM((2 * steps, SC_WINDOW), jnp.int32),
                       pltpu.VMEM((SC_INFLIGHT, SC_WINDOW, LANES), src.dtype),
                       pltpu.SemaphoreType.DMA((SC_INFLIGHT,)), pltpu.SemaphoreType.DMA((SC_INFLIGHT,))],
        name="sc_dispatch_scatter",
    )
    def k(src_hbm, idx_hbm, out_hbm, idx_v, rows_v, lsem, wsem):
        first = (lax.axis_index("s") * nc + lax.axis_index("c")) * steps
        pltpu.sync_copy(idx_hbm.at[pl.ds(first, steps)], idx_v.at[pl.ds(0, steps)])
        pltpu.sync_copy(idx_hbm.at[pl.ds(n_win + first, steps)], idx_v.at[pl.ds(steps, steps)])

        @pl.loop(0, steps, step=SC_INFLIGHT)
        def _(j):
            loads = [pltpu.async_copy(src_hbm.at[pl.ds((first + j + b) * SC_WINDOW, SC_WINDOW)],
                                      rows_v.at[b], lsem.at[b]) for b in range(SC_INFLIGHT)]
            writes = []
            for b in range(SC_INFLIGHT):
                loads[b].wait()
                for half in range(TOP_K):
                    dst = out_hbm.at[idx_v.at[half * steps + j + b]]
                    writes.append(pltpu.async_copy(rows_v.at[b], dst, wsem.at[b]))
            for w in writes:
                w.wait()

    return k(src, idx)


def _sc_gather(table, idx):
    n_out = idx.shape[0] * SC_WINDOW
    nc, nw = _sc_workers()
    steps = n_out // nw // SC_WINDOW
    mesh = plsc.VectorSubcoreMesh(core_axis_name="c", subcore_axis_name="s")

    @functools.partial(
        pl.kernel, mesh=mesh,
        out_type=jax.ShapeDtypeStruct((n_out, LANES), table.dtype),
        scratch_types=[pltpu.VMEM((steps, SC_WINDOW), jnp.int32),
                       pltpu.VMEM((SC_INFLIGHT, SC_WINDOW, LANES), table.dtype),
                       pltpu.SemaphoreType.DMA((SC_INFLIGHT,)), pltpu.SemaphoreType.DMA((SC_INFLIGHT,))],
        name="sc_combine_gather",
    )
    def k(table_hbm, idx_hbm, out_hbm, idx_v, rows_v, gsem, wsem):
        first = (lax.axis_index("s") * nc + lax.axis_index("c")) * steps
        pltpu.sync_copy(idx_hbm.at[pl.ds(first, steps)], idx_v)

        @pl.loop(0, steps, step=SC_INFLIGHT)
        def _(j):
            gathers = [pltpu.async_copy(table_hbm.at[idx_v.at[j + b]], rows_v.at[b], gsem.at[b])
                       for b in range(SC_INFLIGHT)]
            writes = []
            for b in range(SC_INFLIGHT):
                gathers[b].wait()
                dst = out_hbm.at[pl.ds((first + j + b) * SC_WINDOW, SC_WINDOW)]
                writes.append(pltpu.async_copy(rows_v.at[b], dst, wsem.at[b]))
            for w in writes:
                w.wait()

    return k(table, idx)


def _final_kernel(x_ref, rt_ref, y_ref, gt_ref, g_ref, o_ref):
    x = x_ref[...] + gt_ref[...] * _combine(rt_ref, y_ref)
    ms = jnp.mean(x * x, axis=-1, keepdims=True)
    o_ref[...] = x * lax.rsqrt(ms + EPS) * g_ref[...]


def _final(x, moe, gate, g):
    tpb = SEQ // TM
    row = pl.BlockSpec((TM, D_MODEL), lambda i: (i, 0))
    return pl.pallas_call(
        _final_kernel,
        grid=(N_TOK // TM,),
        in_specs=[row, pl.BlockSpec((TM, LANES), lambda i: (i, 0)),
                  pl.BlockSpec((TOP_K, N_PLANES, TM, LANES), lambda i: (0, 0, i, 0)),
                  pl.BlockSpec((None, 1, D_MODEL), lambda i: (i // tpb, 0, 0)),
                  pl.BlockSpec((1, D_MODEL), lambda i: (0, 0))],
        out_specs=row,
        out_shape=jax.ShapeDtypeStruct((N_TOK, D_MODEL), F32),
        compiler_params=_params("arbitrary"),
        name="final_norm",
    )(x, moe[0], moe[1], gate, g.reshape(1, D_MODEL))


def kernel(x, c, positions, w_ada, b_ada, g_mix, w_in, b_forget, lambda_q1, lambda_k1, lambda_q2,
           lambda_k2, g_subln, g_fox_out, w_out, g_ffn, w_router_group, b_router_group,
           w_router_expert, b_router_expert, w_expert_gate, w_expert_up, w_expert_down, g_final):
    mod = _modulation(c, w_ada, b_ada)
    mod = mod.reshape(DEPTH, BATCH, 6, 1, D_MODEL)
    tables = _rope_tables(positions)
    pq = _forget_placement()
    w_in_bf = jnp.pad(w_in.astype(BF16), ((0, 0), (0, 0), (0, IN_COLS_PAD - IN_COLS)))
    xf = x.reshape(N_TOK, D_MODEL)
    moe = None
    gate = None
    for l in range(DEPTH):
        sh1, sc1, gt1, sh2, sc2, gt2 = (mod[l, :, j] for j in range(6))
        w_bf = (l, w_in_bf)
        bfp =jnp.pad(b_forget[l], (0, LANES - N_FOX_HEADS)).reshape(1, LANES)
        xf, (dq, dk, dv, fq, fk, fv) = _inproj(xf, moe, gate, sc1, sh1, g_mix[l], w_bf, bfp, tables, pq)

        lambda_init = 0.8 - 0.6 * float(np.exp(-0.3 * l))
        lamv = jnp.zeros((8, LANES), F32).at[0:4, 0:HEAD_DIM].set(
            jnp.stack([lambda_q1[l], lambda_k1[l], lambda_q2[l], lambda_k2[l]]))
        g_d = g_subln[l].reshape(1, LANES)
        g_f = jnp.concatenate([g_fox_out[l], g_fox_out[l]]).reshape(1, LANES)
        od = _attention(True, lambda_init, dq, dk, dv, g_d, lamv)
        of = _attention(False, lambda_init, fq, fk, fv, g_f, lamv)

        wr32 = jnp.pad(jnp.concatenate([w_router_group[l], w_router_expert[l]], axis=1),
                       ((0, 0), (0, LANES - N_GROUPS - N_EXPERTS)))
        wr_hi = wr32.astype(BF16)
        wr_lo = (wr32 - wr_hi.astype(F32)).astype(BF16)
        wr = jnp.concatenate([wr_hi, wr_lo], axis=1)
        br = jnp.pad(jnp.concatenate([b_router_group[l], b_router_expert[l]]),
                     (0, LANES - N_GROUPS - N_EXPERTS)).reshape(1, LANES)
        xf, h2, route, route_t, counts = _outproj(xf, od, of, gt1, sc2, sh2, g_ffn[l],
                                                  w_out[l].astype(BF16), wr, br)

        rows, block_expert, block_count = _slots(route_t, counts)
        rows = rows.reshape(TOP_K * N_PLANES * N_TOK // SC_WINDOW, SC_WINDOW)
        xs = _sc_scatter2(h2.reshape(N_PLANES * N_TOK, LANES), rows, N_PLANES * PLANE_ROWS)
        ys = _experts(l, block_expert, block_count, xs.reshape(N_PLANES, PLANE_ROWS, LANES),
                      w_expert_gate, w_expert_up, w_expert_down)
        y2 = _sc_gather(ys.reshape(N_PLANES * PLANE_ROWS, LANES), rows)
        moe = (route, y2.reshape(TOP_K, N_PLANES, N_TOK, LANES))
        gate = gt2
    out = _final(xf, moe, gate, g_final)
    return out.reshape(BATCH, SEQ, D_MODEL)
```
